```python
import math
import jax, jax.numpy as jnp
from jax import lax
import numpy as np

D_MODEL = 1024
BATCH = 8
SEQ = 8192
DEPTH = 2

N_A_LAYERS = DEPTH // 2
N_B_LAYERS = DEPTH - N_A_LAYERS
CHUNK = 128
A_WIDTH = D_MODEL
A_GROUPS = 8
A_GROUP_DIM = A_WIDTH // A_GROUPS
HEAD_DIM = 64
N_Q_HEADS = D_MODEL // HEAD_DIM
N_KV_HEADS = 4
GQA_GROUP = N_Q_HEADS // N_KV_HEADS
WINDOW = 128
BLOCK = 128
D_FF = 2816
CONV_WIDTH = 3
PLE_DIM = 256
EPS = 1e-6

kernel_name = "yoco_gmlp_swa_sink_hybrid"


def _alibi_slopes(n):
    return np.array([2.0 ** (-8.0 * (h + 1) / n) for h in range(n)], dtype=np.float32)


def rmsnorm(x, g):
    xf = x.astype(jnp.float32)
    y = xf * lax.rsqrt(jnp.mean(xf * xf, axis=-1, keepdims=True) + EPS)
    return (y * g.astype(jnp.float32)).astype(x.dtype)


def mixer_a(xn, w_in, g_v, w_s, b_s, w_out):
    B, S, _ = xn.shape
    z = jax.nn.gelu(xn @ w_in)
    u, v = jnp.split(z, 2, axis=-1)
    v = rmsnorm(v, g_v)
    nc = S // CHUNK
    v = v.reshape(B, nc, CHUNK, A_GROUPS, A_GROUP_DIM)
    causal = jnp.tril(jnp.ones((CHUNK, CHUNK), dtype=bool))
    w = jnp.where(causal[None], w_s, jnp.zeros((), w_s.dtype)).astype(v.dtype)
    s = jnp.einsum('hts,bcshd->bcthd', w, v) + b_s.T.astype(v.dtype)[None, None, :, :, None]
    s = s.reshape(B, S, A_WIDTH)
    return (u * s) @ w_out


def shared_kv(h, g, w_kv):
    B, S, _ = h.shape
    nb = S // BLOCK
    kv = rmsnorm(h, g) @ w_kv
    k, v = jnp.split(kv, 2, axis=-1)

    def band(t):
        tb = t.reshape(B, nb, BLOCK, N_KV_HEADS, HEAD_DIM)
        prev = jnp.pad(tb[:, :-1], ((0, 0), (1, 0), (0, 0), (0, 0), (0, 0)))
        return jnp.concatenate([prev, tb], axis=2)

    return band(k), band(v)


def mixer_b(xn, w_q, sinks, w_o, kblk, vblk):
    B, S, _ = xn.shape
    nb = S // BLOCK
    q = (xn @ w_q).reshape(B, nb, BLOCK, N_KV_HEADS, GQA_GROUP, HEAD_DIM)
    scores = jnp.einsum('bnikgd,bnjkd->bnkgij', q.astype(jnp.float32),
                        kblk.astype(jnp.float32)) * (HEAD_DIM ** -0.5)
    i = jnp.arange(BLOCK)[:, None]
    j = jnp.arange(2 * BLOCK)[None, :]
    dist = i + BLOCK - j
    in_band = (dist >= 0) & (dist < WINDOW)
    blk = jnp.arange(nb)[:, None, None]
    valid = in_band[None] & ((blk > 0) | (j[None] >= BLOCK))
    slopes = jnp.asarray(_alibi_slopes(N_Q_HEADS)).reshape(N_KV_HEADS, GQA_GROUP)
    scores = scores - slopes[:, :, None, None] * dist.astype(jnp.float32)
    scores = jnp.where(valid[None, :, None, None], scores, -jnp.inf)
    sink = sinks.astype(jnp.float32).reshape(N_KV_HEADS, GQA_GROUP)[:, :, None, None]
    m = jnp.maximum(jnp.max(scores, axis=-1, keepdims=True), sink)
    e = jnp.exp(scores - m)
    probs = e / (jnp.sum(e, axis=-1, keepdims=True) + jnp.exp(sink - m))
    out = jnp.einsum('bnkgij,bnjkd->bnikgd', probs.astype(vblk.dtype), vblk)
    return out.reshape(B, S, N_Q_HEADS * HEAD_DIM) @ w_o


def conv_ffn(xn, w_up, conv_w, conv_b, w_down):
    S = xn.shape[1]
    h = xn @ w_up
    hp = jnp.pad(h, ((0, 0), (CONV_WIDTH - 1, 0), (0, 0)))
    c = conv_b + sum(hp[:, t:t + S] * conv_w[t] for t in range(CONV_WIDTH))
    g, u = jnp.split(c, 2, axis=-1)
    return (jax.nn.silu(g) * u) @ w_down


def per_layer_embed(h, p_i, g_norm, w_in, w_gate, b_gate):
    gate = jax.nn.sigmoid(rmsnorm(h, g_norm) @ w_gate + b_gate)
    return (p_i @ w_in) * gate


def _fwd_setup_inputs(seed: int = 0) -> dict:
    key = jax.random.key(seed)
    ks = jax.random.split(key, 26)
    f32 = jnp.float32

    def nrm(k, shape, scale):
        return jax.random.normal(k, shape, f32) * scale

    def gain(k, shape):
        return 1.0 + 0.02 * jax.random.normal(k, shape, f32)

    d = D_MODEL
    return {
        "x": nrm(ks[0], (BATCH, SEQ, d), 1.0),
        "p": nrm(ks[1], (DEPTH, BATCH, SEQ, PLE_DIM), 1.0),
        "norm_mix": gain(ks[2], (DEPTH, d)),
        "norm_ffn": gain(ks[3], (DEPTH, d)),
        "norm_ple": gain(ks[4], (DEPTH, d)),
        "norm_kv": gain(ks[5], (d,)),
        "norm_final": gain(ks[6], (d,)),
        "a_w_in": nrm(ks[7], (N_A_LAYERS, d, 2 * A_WIDTH), d ** -0.5),
        "a_norm_v": gain(ks[8], (N_A_LAYERS, A_WIDTH)),
        "a_w_s": nrm(ks[9], (N_A_LAYERS, A_GROUPS, CHUNK, CHUNK), CHUNK ** -0.5),
        "a_b_s": 1.0 + nrm(ks[10], (N_A_LAYERS, A_GROUPS, CHUNK), 0.02),
        "a_w_out": nrm(ks[11], (N_A_LAYERS, A_WIDTH, d), A_WIDTH ** -0.5),
        "w_kv": nrm(ks[12], (d, 2 * N_KV_HEADS * HEAD_DIM), d ** -0.5),
        "b_w_q": nrm(ks[13], (N_B_LAYERS, d, N_Q_HEADS * HEAD_DIM), d ** -0.5),
        "b_sinks": nrm(ks[14], (N_B_LAYERS, N_Q_HEADS), 0.5),
        "b_w_o": nrm(ks[15], (N_B_LAYERS, N_Q_HEADS * HEAD_DIM, d), (N_Q_HEADS * HEAD_DIM) ** -0.5),
        "f_w_up": nrm(ks[16], (DEPTH, d, 2 * D_FF), d ** -0.5),
        "f_conv_w": nrm(ks[17], (DEPTH, CONV_WIDTH, 2 * D_FF), CONV_WIDTH ** -0.5),
        "f_conv_b": nrm(ks[18], (DEPTH, 2 * D_FF), 0.01),
        "f_w_down": nrm(ks[19], (DEPTH, D_FF, d), D_FF ** -0.5),
        "ple_w_in": nrm(ks[20], (DEPTH, PLE_DIM, d), PLE_DIM ** -0.5),
        "ple_w_gate": nrm(ks[21], (DEPTH, d, d), d ** -0.5),
        "ple_b_gate": nrm(ks[22], (DEPTH, d), 0.01),
    }


def _fwd_reference(x, p, norm_mix, norm_ffn, norm_ple, norm_kv, norm_final,
              a_w_in, a_norm_v, a_w_s, a_b_s, a_w_out,
              w_kv, b_w_q, b_sinks, b_w_o,
              f_w_up, f_conv_w, f_conv_b, f_w_down,
              ple_w_in, ple_w_gate, ple_b_gate):
    h = x
    kblk = None
    vblk = None
    for i in range(DEPTH):
        xn = rmsnorm(h, norm_mix[i])
        if i < N_A_LAYERS:
            a = i
            h = h + mixer_a(xn, a_w_in[a], a_norm_v[a], a_w_s[a], a_b_s[a], a_w_out[a])
        else:
            b = i - N_A_LAYERS
            h = h + mixer_b(xn, b_w_q[b], b_sinks[b], b_w_o[b], kblk, vblk)
        h = h + conv_ffn(rmsnorm(h, norm_ffn[i]), f_w_up[i], f_conv_w[i], f_conv_b[i], f_w_down[i])
        h = h + per_layer_embed(h, p[i], norm_ple[i], ple_w_in[i], ple_w_gate[i], ple_b_gate[i])
        if i == N_A_LAYERS - 1:
            kblk, vblk = shared_kv(h, norm_kv, w_kv)
    return rmsnorm(h, norm_final)


import jax as _jax
import jax.numpy as _jnp

TWIN_FORMAT = 'train_step'
FWD_PARAMS = ['x', 'p', 'norm_mix', 'norm_ffn', 'norm_ple', 'norm_kv', 'norm_final', 'a_w_in', 'a_norm_v', 'a_w_s', 'a_b_s', 'a_w_out', 'w_kv', 'b_w_q', 'b_sinks', 'b_w_o', 'f_w_up', 'f_conv_w', 'f_conv_b', 'f_w_down', 'ple_w_in', 'ple_w_gate', 'ple_b_gate']
TWIN_WEIGHTS = ['norm_mix', 'norm_ffn', 'norm_ple', 'norm_kv', 'norm_final', 'a_w_in', 'a_norm_v', 'a_w_s', 'a_b_s', 'a_w_out', 'w_kv', 'b_w_q', 'b_sinks', 'b_w_o', 'f_w_up', 'f_conv_w', 'f_conv_b', 'f_w_down', 'ple_w_in', 'ple_w_gate', 'ple_b_gate']
TWIN_DIFF_INPUT = 'x'
TWIN_INPUTS = ['x', 'p', 'norm_mix', 'norm_ffn', 'norm_ple', 'norm_kv', 'norm_final', 'a_w_in', 'a_norm_v', 'a_w_s', 'a_b_s', 'a_w_out', 'w_kv', 'b_w_q', 'b_sinks', 'b_w_o', 'f_w_up', 'f_conv_w', 'f_conv_b', 'f_w_down', 'ple_w_in', 'ple_w_gate', 'ple_b_gate', 'loss_target', 'm_norm_mix', 'm_norm_ffn', 'm_norm_ple', 'm_norm_kv', 'm_norm_final', 'm_a_w_in', 'm_a_norm_v', 'm_a_w_s', 'm_a_b_s', 'm_a_w_out', 'm_w_kv', 'm_b_w_q', 'm_b_sinks', 'm_b_w_o', 'm_f_w_up', 'm_f_conv_w', 'm_f_conv_b', 'm_f_w_down', 'm_ple_w_in', 'm_ple_w_gate', 'm_ple_b_gate', 'v_norm_mix', 'v_norm_ffn', 'v_norm_ple', 'v_norm_kv', 'v_norm_final', 'v_a_w_in', 'v_a_norm_v', 'v_a_w_s', 'v_a_b_s', 'v_a_w_out', 'v_w_kv', 'v_b_w_q', 'v_b_sinks', 'v_b_w_o', 'v_f_w_up', 'v_f_conv_w', 'v_f_conv_b', 'v_f_w_down', 'v_ple_w_in', 'v_ple_w_gate', 'v_ple_b_gate']
TWIN_OUTPUTS = ['loss', 'grad_x', 'grad_norm_mix', 'grad_norm_ffn', 'grad_norm_ple', 'grad_norm_kv', 'grad_norm_final', 'grad_a_w_in', 'grad_a_norm_v', 'grad_a_w_s', 'grad_a_b_s', 'grad_a_w_out', 'grad_w_kv', 'grad_b_w_q', 'grad_b_sinks', 'grad_b_w_o', 'grad_f_w_up', 'grad_f_conv_w', 'grad_f_conv_b', 'grad_f_w_down', 'grad_ple_w_in', 'grad_ple_w_gate', 'grad_ple_b_gate', 'delta_norm_mix', 'delta_norm_ffn', 'delta_norm_ple', 'delta_norm_kv', 'delta_norm_final', 'delta_a_w_in', 'delta_a_norm_v', 'delta_a_w_s', 'delta_a_b_s', 'delta_a_w_out', 'delta_w_kv', 'delta_b_w_q', 'delta_b_sinks', 'delta_b_w_o', 'delta_f_w_up', 'delta_f_conv_w', 'delta_f_conv_b', 'delta_f_w_down', 'delta_ple_w_in', 'delta_ple_w_gate', 'delta_ple_b_gate', 'new_m_norm_mix', 'new_m_norm_ffn', 'new_m_norm_ple', 'new_m_norm_kv', 'new_m_norm_final', 'new_m_a_w_in', 'new_m_a_norm_v', 'new_m_a_w_s', 'new_m_a_b_s', 'new_m_a_w_out', 'new_m_w_kv', 'new_m_b_w_q', 'new_m_b_sinks', 'new_m_b_w_o', 'new_m_f_w_up', 'new_m_f_conv_w', 'new_m_f_conv_b', 'new_m_f_w_down', 'new_m_ple_w_in', 'new_m_ple_w_gate', 'new_m_ple_b_gate', 'new_v_norm_mix', 'new_v_norm_ffn', 'new_v_norm_ple', 'new_v_norm_kv', 'new_v_norm_final', 'new_v_a_w_in', 'new_v_a_norm_v', 'new_v_a_w_s', 'new_v_a_b_s', 'new_v_a_w_out', 'new_v_w_kv', 'new_v_b_w_q', 'new_v_b_sinks', 'new_v_b_w_o', 'new_v_f_w_up', 'new_v_f_conv_w', 'new_v_f_conv_b', 'new_v_f_w_down', 'new_v_ple_w_in', 'new_v_ple_w_gate', 'new_v_ple_b_gate']
TWIN_LEAF_KINDS = {'loss': 'loss', 'grad_x': 'grad_x', 'grad_norm_mix': 'grad_w', 'grad_norm_ffn': 'grad_w', 'grad_norm_ple': 'grad_w', 'grad_norm_kv': 'grad_w', 'grad_norm_final': 'grad_w', 'grad_a_w_in': 'grad_w', 'grad_a_norm_v': 'grad_w', 'grad_a_w_s': 'grad_w', 'grad_a_b_s': 'grad_w', 'grad_a_w_out': 'grad_w', 'grad_w_kv': 'grad_w', 'grad_b_w_q': 'grad_w', 'grad_b_sinks': 'grad_w', 'grad_b_w_o': 'grad_w', 'grad_f_w_up': 'grad_w', 'grad_f_conv_w': 'grad_w', 'grad_f_conv_b': 'grad_w', 'grad_f_w_down': 'grad_w', 'grad_ple_w_in': 'grad_w', 'grad_ple_w_gate': 'grad_w', 'grad_ple_b_gate': 'grad_w', 'delta_norm_mix': 'delta_w', 'delta_norm_ffn': 'delta_w', 'delta_norm_ple': 'delta_w', 'delta_norm_kv': 'delta_w', 'delta_norm_final': 'delta_w', 'delta_a_w_in': 'delta_w', 'delta_a_norm_v': 'delta_w', 'delta_a_w_s': 'delta_w', 'delta_a_b_s': 'delta_w', 'delta_a_w_out': 'delta_w', 'delta_w_kv': 'delta_w', 'delta_b_w_q': 'delta_w', 'delta_b_sinks': 'delta_w', 'delta_b_w_o': 'delta_w', 'delta_f_w_up': 'delta_w', 'delta_f_conv_w': 'delta_w', 'delta_f_conv_b': 'delta_w', 'delta_f_w_down': 'delta_w', 'delta_ple_w_in': 'delta_w', 'delta_ple_w_gate': 'delta_w', 'delta_ple_b_gate': 'delta_w', 'new_m_norm_mix': 'new_m', 'new_m_norm_ffn': 'new_m', 'new_m_norm_ple': 'new_m', 'new_m_norm_kv': 'new_m', 'new_m_norm_final': 'new_m', 'new_m_a_w_in': 'new_m', 'new_m_a_norm_v': 'new_m', 'new_m_a_w_s': 'new_m', 'new_m_a_b_s': 'new_m', 'new_m_a_w_out': 'new_m', 'new_m_w_kv': 'new_m', 'new_m_b_w_q': 'new_m', 'new_m_b_sinks': 'new_m', 'new_m_b_w_o': 'new_m', 'new_m_f_w_up': 'new_m', 'new_m_f_conv_w': 'new_m', 'new_m_f_conv_b': 'new_m', 'new_m_f_w_down': 'new_m', 'new_m_ple_w_in': 'new_m', 'new_m_ple_w_gate': 'new_m', 'new_m_ple_b_gate': 'new_m', 'new_v_norm_mix': 'new_v', 'new_v_norm_ffn': 'new_v', 'new_v_norm_ple': 'new_v', 'new_v_norm_kv': 'new_v', 'new_v_norm_final': 'new_v', 'new_v_a_w_in': 'new_v', 'new_v_a_norm_v': 'new_v', 'new_v_a_w_s': 'new_v', 'new_v_a_b_s': 'new_v', 'new_v_a_w_out': 'new_v', 'new_v_w_kv': 'new_v', 'new_v_b_w_q': 'new_v', 'new_v_b_sinks': 'new_v', 'new_v_b_w_o': 'new_v', 'new_v_f_w_up': 'new_v', 'new_v_f_conv_w': 'new_v', 'new_v_f_conv_b': 'new_v', 'new_v_f_w_down': 'new_v', 'new_v_ple_w_in': 'new_v', 'new_v_ple_w_gate': 'new_v', 'new_v_ple_b_gate': 'new_v'}


def _forward(args):
    return _fwd_reference(*[args[k] for k in FWD_PARAMS])


def _output_shape():
    out = _jax.eval_shape(lambda: _forward(_fwd_setup_inputs(0)))
    return out.shape, out.dtype

N_MICROBATCH = 1
ADAM_LR = 0.001
ADAM_B1 = 0.9
ADAM_B2 = 0.999
ADAM_EPS = 1e-08
ADAM_WD = 0.01
ADAM_STEP = 10
PER_EXAMPLE_BATCH_AXIS = {'x': 0, 'p': 1, 'loss_target': 0}
SHARED_INPUTS = []
_WEIGHT_DTYPES = {'norm_mix': _jnp.float32, 'norm_ffn': _jnp.float32, 'norm_ple': _jnp.float32, 'norm_kv': _jnp.float32, 'norm_final': _jnp.float32, 'a_w_in': _jnp.float32, 'a_norm_v': _jnp.float32, 'a_w_s': _jnp.float32, 'a_b_s': _jnp.float32, 'a_w_out': _jnp.float32, 'w_kv': _jnp.float32, 'b_w_q': _jnp.float32, 'b_sinks': _jnp.float32, 'b_w_o': _jnp.float32, 'f_w_up': _jnp.float32, 'f_conv_w': _jnp.float32, 'f_conv_b': _jnp.float32, 'f_w_down': _jnp.float32, 'ple_w_in': _jnp.float32, 'ple_w_gate': _jnp.float32, 'ple_b_gate': _jnp.float32}
MOMENT_SCALE = {'norm_mix': 1.417395e-01, 'norm_ffn': 1.420529e-01, 'norm_ple': 3.536040e-02, 'norm_kv': 7.820053e-02, 'norm_final': 6.410451e+01, 'a_w_in': 1.410653e-01, 'a_norm_v': 9.275015e-02, 'a_w_s': 9.437407e-02, 'a_b_s': 1.441539e-01, 'a_w_out': 1.761374e-01, 'w_kv': 1.029655e-01, 'b_w_q': 3.864196e-02, 'b_sinks': 6.206227e-02, 'b_w_o': 6.205626e-02, 'f_w_up': 5.942589e-02, 'f_conv_w': 6.004159e-02, 'f_conv_b': 5.967724e-02, 'f_w_down': 9.764929e-02, 'ple_w_in': 8.532127e-02, 'ple_w_gate': 3.415914e-02, 'ple_b_gate': 5.449905e-02}


def _to_microbatches(a, axis):
    t = _jnp.moveaxis(a, axis, 0)
    t = t.reshape((N_MICROBATCH, t.shape[0] // N_MICROBATCH) + t.shape[1:])
    return _jnp.moveaxis(t, 1, axis + 1)


def setup_inputs(seed: int = 0) -> dict:
    inp = _fwd_setup_inputs(seed)
    key = _jax.random.fold_in(_jax.random.key(seed), 7919)
    shape, _ = _output_shape()
    out = dict(inp)
    out["loss_target"] = _jax.random.normal(_jax.random.fold_in(key, 0), shape, _jnp.float32)
    for i, name in enumerate(TWIN_WEIGHTS):
        w = inp[name].astype(_jnp.float32)
        if MOMENT_SCALE is None:
            s = _jnp.sqrt(_jnp.mean(_jnp.square(w)) + 1e-30)
        else:
            s = MOMENT_SCALE[name]
        km, kv = _jax.random.split(_jax.random.fold_in(key, i + 1))
        out[name] = w
        out["m_" + name] = s * _jax.random.normal(km, w.shape, _jnp.float32)
        out["v_" + name] = (s * s) * _jax.random.uniform(kv, w.shape, _jnp.float32, 0.5, 1.5)
    if N_MICROBATCH > 1:
        for name, axis in PER_EXAMPLE_BATCH_AXIS.items():
            out[name] = _to_microbatches(out[name], axis)
    return {'x': out['x'], 'p': out['p'], 'norm_mix': out['norm_mix'], 'norm_ffn': out['norm_ffn'], 'norm_ple': out['norm_ple'], 'norm_kv': out['norm_kv'], 'norm_final': out['norm_final'], 'a_w_in': out['a_w_in'], 'a_norm_v': out['a_norm_v'], 'a_w_s': out['a_w_s'], 'a_b_s': out['a_b_s'], 'a_w_out': out['a_w_out'], 'w_kv': out['w_kv'], 'b_w_q': out['b_w_q'], 'b_sinks': out['b_sinks'], 'b_w_o': out['b_w_o'], 'f_w_up': out['f_w_up'], 'f_conv_w': out['f_conv_w'], 'f_conv_b': out['f_conv_b'], 'f_w_down': out['f_w_down'], 'ple_w_in': out['ple_w_in'], 'ple_w_gate': out['ple_w_gate'], 'ple_b_gate': out['ple_b_gate'], 'loss_target': out['loss_target'], 'm_norm_mix': out['m_norm_mix'], 'm_norm_ffn': out['m_norm_ffn'], 'm_norm_ple': out['m_norm_ple'], 'm_norm_kv': out['m_norm_kv'], 'm_norm_final': out['m_norm_final'], 'm_a_w_in': out['m_a_w_in'], 'm_a_norm_v': out['m_a_norm_v'], 'm_a_w_s': out['m_a_w_s'], 'm_a_b_s': out['m_a_b_s'], 'm_a_w_out': out['m_a_w_out'], 'm_w_kv': out['m_w_kv'], 'm_b_w_q': out['m_b_w_q'], 'm_b_sinks': out['m_b_sinks'], 'm_b_w_o': out['m_b_w_o'], 'm_f_w_up': out['m_f_w_up'], 'm_f_conv_w': out['m_f_conv_w'], 'm_f_conv_b': out['m_f_conv_b'], 'm_f_w_down': out['m_f_w_down'], 'm_ple_w_in': out['m_ple_w_in'], 'm_ple_w_gate': out['m_ple_w_gate'], 'm_ple_b_gate': out['m_ple_b_gate'], 'v_norm_mix': out['v_norm_mix'], 'v_norm_ffn': out['v_norm_ffn'], 'v_norm_ple': out['v_norm_ple'], 'v_norm_kv': out['v_norm_kv'], 'v_norm_final': out['v_norm_final'], 'v_a_w_in': out['v_a_w_in'], 'v_a_norm_v': out['v_a_norm_v'], 'v_a_w_s': out['v_a_w_s'], 'v_a_b_s': out['v_a_b_s'], 'v_a_w_out': out['v_a_w_out'], 'v_w_kv': out['v_w_kv'], 'v_b_w_q': out['v_b_w_q'], 'v_b_sinks': out['v_b_sinks'], 'v_b_w_o': out['v_b_w_o'], 'v_f_w_up': out['v_f_w_up'], 'v_f_conv_w': out['v_f_conv_w'], 'v_f_conv_b': out['v_f_conv_b'], 'v_f_w_down': out['v_f_w_down'], 'v_ple_w_in': out['v_ple_w_in'], 'v_ple_w_gate': out['v_ple_w_gate'], 'v_ple_b_gate': out['v_ple_b_gate']}


def _loss(weights, diff, rest, loss_target):
    with _jax.named_scope("forward"):
        args = {**rest, TWIN_DIFF_INPUT: diff, **{k: w.astype(_WEIGHT_DTYPES[k]) for k, w in weights.items()}}
        y = _forward(args)
    with _jax.named_scope("loss_head"):
        err = _jnp.square(y.astype(_jnp.float32) - loss_target)
        return 0.5 * _jnp.sum(_jnp.mean(err, axis=-1)) if err.ndim else 0.5 * err


def _adamw(w, g, m, v):
    m = ADAM_B1 * m + (1.0 - ADAM_B1) * g
    v = ADAM_B2 * v + (1.0 - ADAM_B2) * _jnp.square(g)
    m_hat = m / (1.0 - ADAM_B1 ** ADAM_STEP)
    v_hat = v / (1.0 - ADAM_B2 ** ADAM_STEP)
    delta = -ADAM_LR * (m_hat / (_jnp.sqrt(v_hat) + ADAM_EPS) + ADAM_WD * w)
    return delta, m, v


def reference(x, p, norm_mix, norm_ffn, norm_ple, norm_kv, norm_final, a_w_in, a_norm_v, a_w_s, a_b_s, a_w_out, w_kv, b_w_q, b_sinks, b_w_o, f_w_up, f_conv_w, f_conv_b, f_w_down, ple_w_in, ple_w_gate, ple_b_gate, loss_target, m_norm_mix, m_norm_ffn, m_norm_ple, m_norm_kv, m_norm_final, m_a_w_in, m_a_norm_v, m_a_w_s, m_a_b_s, m_a_w_out, m_w_kv, m_b_w_q, m_b_sinks, m_b_w_o, m_f_w_up, m_f_conv_w, m_f_conv_b, m_f_w_down, m_ple_w_in, m_ple_w_gate, m_ple_b_gate, v_norm_mix, v_norm_ffn, v_norm_ple, v_norm_kv, v_norm_final, v_a_w_in, v_a_norm_v, v_a_w_s, v_a_b_s, v_a_w_out, v_w_kv, v_b_w_q, v_b_sinks, v_b_w_o, v_f_w_up, v_f_conv_w, v_f_conv_b, v_f_w_down, v_ple_w_in, v_ple_w_gate, v_ple_b_gate):
    given = dict(x=x, p=p, norm_mix=norm_mix, norm_ffn=norm_ffn, norm_ple=norm_ple, norm_kv=norm_kv, norm_final=norm_final, a_w_in=a_w_in, a_norm_v=a_norm_v, a_w_s=a_w_s, a_b_s=a_b_s, a_w_out=a_w_out, w_kv=w_kv, b_w_q=b_w_q, b_sinks=b_sinks, b_w_o=b_w_o, f_w_up=f_w_up, f_conv_w=f_conv_w, f_conv_b=f_conv_b, f_w_down=f_w_down, ple_w_in=ple_w_in, ple_w_gate=ple_w_gate, ple_b_gate=ple_b_gate, loss_target=loss_target, m_norm_mix=m_norm_mix, m_norm_ffn=m_norm_ffn, m_norm_ple=m_norm_ple, m_norm_kv=m_norm_kv, m_norm_final=m_norm_final, m_a_w_in=m_a_w_in, m_a_norm_v=m_a_norm_v, m_a_w_s=m_a_w_s, m_a_b_s=m_a_b_s, m_a_w_out=m_a_w_out, m_w_kv=m_w_kv, m_b_w_q=m_b_w_q, m_b_sinks=m_b_sinks, m_b_w_o=m_b_w_o, m_f_w_up=m_f_w_up, m_f_conv_w=m_f_conv_w, m_f_conv_b=m_f_conv_b, m_f_w_down=m_f_w_down, m_ple_w_in=m_ple_w_in, m_ple_w_gate=m_ple_w_gate, m_ple_b_gate=m_ple_b_gate, v_norm_mix=v_norm_mix, v_norm_ffn=v_norm_ffn, v_norm_ple=v_norm_ple, v_norm_kv=v_norm_kv, v_norm_final=v_norm_final, v_a_w_in=v_a_w_in, v_a_norm_v=v_a_norm_v, v_a_w_s=v_a_w_s, v_a_b_s=v_a_b_s, v_a_w_out=v_a_w_out, v_w_kv=v_w_kv, v_b_w_q=v_b_w_q, v_b_sinks=v_b_sinks, v_b_w_o=v_b_w_o, v_f_w_up=v_f_w_up, v_f_conv_w=v_f_conv_w, v_f_conv_b=v_f_conv_b, v_f_w_down=v_f_w_down, v_ple_w_in=v_ple_w_in, v_ple_w_gate=v_ple_w_gate, v_ple_b_gate=v_ple_b_gate)
    weights = {n: given[n] for n in TWIN_WEIGHTS}
    shared = {n: given[n] for n in SHARED_INPUTS}
    per_example = {n: given[n] for n in ['x', 'p']}
    grad_fn = _jax.value_and_grad(_loss, argnums=(0, 1))

    def one_microbatch(ex, loss_target):
        ex = dict(ex)
        diff = ex.pop(TWIN_DIFF_INPUT)
        return grad_fn(weights, diff, {**shared, **ex}, loss_target)

    if N_MICROBATCH == 1:
        loss, (grad_w, grad_x) = one_microbatch(per_example, given["loss_target"])
    else:
        def body(carry, xs):
            loss_sum, grad_sum = carry
            l_k, (gw_k, gx_k) = one_microbatch(xs[0], xs[1])
            with _jax.named_scope("update"):
                return (loss_sum + l_k, _jax.tree.map(_jnp.add, grad_sum, gw_k)), gx_k

        init = (_jnp.zeros((), _jnp.float32), _jax.tree.map(_jnp.zeros_like, weights))
        (loss, grad_w), grad_x = _jax.lax.scan(body, init, (per_example, given["loss_target"]))
    with _jax.named_scope("update"):
        delta_w, new_m, new_v = {}, {}, {}
        for n in TWIN_WEIGHTS:
            delta_w[n], new_m[n], new_v[n] = _adamw(weights[n], grad_w[n], given["m_" + n], given["v_" + n])
    return (loss, grad_x, *[grad_w[n] for n in TWIN_WEIGHTS], *[delta_w[n] for n in TWIN_WEIGHTS],
            *[new_m[n] for n in TWIN_WEIGHTS], *[new_v[n] for n in TWIN_WEIGHTS])
```

```python
import functools
import math

import numpy as np
import jax
import jax.numpy as jnp
from jax import lax
from jax.experimental import pallas as pl
from jax.experimental.pallas import tpu as pltpu

F32 = jnp.float32
BF16 = jnp.bfloat16

D_MODEL = 1024
CHUNK = 128
A_GROUPS = 8
HEAD_DIM = 64
N_Q_HEADS = 16
N_KV_HEADS = 4
GQA_GROUP = N_Q_HEADS // N_KV_HEADS
KV_DIM = N_KV_HEADS * HEAD_DIM
BLOCK = 128
D_FF = 2816
N_FF = 2 * D_FF
FF_BLK = N_FF // 4
PLE_DIM = 256
EPS = 1e-6
NEG = -1e30
N_SHARD = 4

ADAM_LR = 0.001
ADAM_B1 = 0.9
ADAM_B2 = 0.999
ADAM_EPS = 1e-08
ADAM_WD = 0.01
ADAM_STEP = 10

VMEM_LIMIT = 60 * 1024 * 1024
MESH = pl.DeviceIdType.MESH
ANY = pl.BlockSpec(memory_space=pl.ANY)
SMEM = pl.BlockSpec(memory_space=pltpu.SMEM)

_SLOPES = [float(np.float32(2.0 ** (-8.0 * (h + 1) / N_Q_HEADS))) for h in range(N_Q_HEADS)]


def _dot(a, b):
    return jnp.dot(a, b, preferred_element_type=F32)


def _dot_nt(a, b):
    return lax.dot_general(a, b, (((1,), (1,)), ((), ())), preferred_element_type=F32)


def _dot_tn(a, b):
    return lax.dot_general(a, b, (((0,), (0,)), ((), ())), preferred_element_type=F32)


def _rms(x, g):
    r = lax.rsqrt(jnp.mean(x * x, axis=-1, keepdims=True) + EPS)
    xh = x * r
    return xh * g, xh, r


def _rms_bwd(dy, xh, r, g):
    dxh = dy * g
    dg = jnp.sum(dy * xh, axis=0, keepdims=True)
    dx = r * (dxh - xh * jnp.mean(dxh * xh, axis=-1, keepdims=True))
    return dx, dg


_GELU_C = math.sqrt(2.0 / math.pi)


def _gelu(x):
    t = jnp.tanh(_GELU_C * (x + 0.044715 * (x * x * x)))
    return 0.5 * x * (1.0 + t)


def _gelu_grad(x):
    x2 = x * x
    t = jnp.tanh(_GELU_C * (x + 0.044715 * (x2 * x)))
    return 0.5 * (1.0 + t) + 0.5 * x * (1.0 - t * t) * (_GELU_C * (1.0 + 3.0 * 0.044715 * x2))


def _sigmoid(x):
    return 1.0 / (1.0 + jnp.exp(-x))


def _load_once(pairs, sem):
    @pl.when(pl.program_id(0) == 0)
    def _():
        cps = [pltpu.make_async_copy(s, d, sem.at[i]) for i, (s, d) in enumerate(pairs)]
        for cp in cps:
            cp.start()
        for cp in cps:
            cp.wait()


def _params(n_axes=1, vmem=VMEM_LIMIT):
    return pltpu.CompilerParams(dimension_semantics=("arbitrary",) * n_axes, vmem_limit_bytes=vmem)


def _row_spec(tm, n, rev_nt=None):
    if rev_nt is None:
        return pl.BlockSpec((tm, n), lambda i: (i, 0))
    return pl.BlockSpec((tm, n), lambda i: (rev_nt - 1 - i, 0))


def _const_spec(shape):
    nd = len(shape)
    return pl.BlockSpec(shape, lambda i: (0,) * nd)


def _zero_first(refs):
    @pl.when(pl.program_id(0) == 0)
    def _():
        for r in refs:
            r[...] = jnp.zeros(r.shape, r.dtype)


def _mixer_a_fwd(x, nmix, gv, wsm, bsb, w_in, w_out):
    T = x.shape[0]
    tm = min(512, T)
    nt = T // tm
    nw = 2 * D_MODEL // N_SHARD

    def body(x_ref, nmix_ref, gv_ref, ws_ref, bsb_ref, w_in_hbm, w_out_hbm,
             h1_ref, zp_ref, w_in_v, w_out_v, gated_v, sem):
        _load_once([(w_in_hbm, w_in_v), (w_out_hbm, w_out_v)], sem)
        xv = x_ref[...]
        xn = _rms(xv, nmix_ref[...])[0].astype(BF16)
        for j in range(N_SHARD):
            zp_ref[:, j * nw:(j + 1) * nw] = _dot(xn, w_in_v[j])
        z = _gelu(zp_ref[...])
        u = z[:, :D_MODEL]
        vn = _rms(z[:, D_MODEL:], gv_ref[...])[0].astype(BF16)
        for c in range(tm // CHUNK):
            rows = slice(c * CHUNK, (c + 1) * CHUNK)
            for h in range(A_GROUPS):
                cols = slice(h * CHUNK, (h + 1) * CHUNK)
                s = _dot(ws_ref[h], vn[rows, cols]) + bsb_ref[h]
                gated_v[rows, cols] = (u[rows, cols] * s).astype(BF16)
        h1_ref[...] = xv + _dot(gated_v[...], w_out_v[...])

    return pl.pallas_call(
        body, name="mixer_a_fwd", grid=(nt,),
        in_specs=[_row_spec(tm, D_MODEL), _const_spec((1, D_MODEL)), _const_spec((1, D_MODEL)),
                  _const_spec((A_GROUPS, CHUNK, CHUNK)), _const_spec((A_GROUPS, CHUNK, CHUNK)), ANY, ANY],
        out_specs=[_row_spec(tm, D_MODEL), _row_spec(tm, 2 * D_MODEL)],
        out_shape=[jax.ShapeDtypeStruct((T, D_MODEL), F32), jax.ShapeDtypeStruct((T, 2 * D_MODEL), F32)],
        scratch_shapes=[pltpu.VMEM((N_SHARD, D_MODEL, nw), BF16), pltpu.VMEM((D_MODEL, D_MODEL), BF16),
                        pltpu.VMEM((tm, D_MODEL), BF16), pltpu.SemaphoreType.DMA((2,))],
        compiler_params=_params(),
    )(x, nmix, gv, wsm, bsb, w_in, w_out)


def _mixer_a_bwd(dh, x, zp, nmix, gv, wsm, bsb, tril, w_in, w_out):
    T = x.shape[0]
    tm = min(256, T)
    nt = T // tm
    nw = 2 * D_MODEL // N_SHARD

    def body(dh_ref, x_ref, zp_ref, nmix_ref, gv_ref, ws_ref, bsb_ref, tril_ref, w_in_hbm, w_out_hbm,
             dx_ref, gated_ref, dzp_ref, xn_ref, dws_ref, dbs_ref, dgv_ref, dnmix_ref,
             w_in_v, w_out_v, du_v, dvn_v, dbs_v, sem):
        _load_once([(w_in_hbm, w_in_v), (w_out_hbm, w_out_v)], sem)
        _zero_first([dws_ref, dbs_v, dgv_ref, dnmix_ref])
        i = pl.program_id(0)
        dhv = dh_ref[...]
        xv = x_ref[...]
        xn, xh, r = _rms(xv, nmix_ref[...])
        xn_ref[...] = xn.astype(BF16)
        zpv = zp_ref[...]
        z = _gelu(zpv)
        u = z[:, :D_MODEL]
        vn_f, vh, rv = _rms(z[:, D_MODEL:], gv_ref[...])
        vn = vn_f.astype(BF16)
        dgated = _dot_nt(dhv.astype(BF16), w_out_v[...])
        for c in range(tm // CHUNK):
            rows = slice(c * CHUNK, (c + 1) * CHUNK)
            for h in range(A_GROUPS):
                cols = slice(h * CHUNK, (h + 1) * CHUNK)
                vn_h = vn[rows, cols]
                s = _dot(ws_ref[h], vn_h) + bsb_ref[h]
                dgt = dgated[rows, cols]
                u_h = u[rows, cols]
                gated_ref[rows, cols] = (u_h * s).astype(BF16)
                du_v[rows, cols] = dgt * s
                ds = dgt * u_h
                dsb = ds.astype(BF16)
                dws_ref[h] += _dot_nt(dsb, vn_h)
                dbs_v[h] += ds
                dvn_v[rows, cols] = _dot_tn(ws_ref[h], dsb)
        dv, dgv = _rms_bwd(dvn_v[...], vh, rv, gv_ref[...])
        dgv_ref[...] += dgv
        dzu = (du_v[...] * _gelu_grad(zpv[:, :D_MODEL])).astype(BF16)
        dzv = (dv * _gelu_grad(zpv[:, D_MODEL:])).astype(BF16)
        dzp_ref[:, :D_MODEL] = dzu
        dzp_ref[:, D_MODEL:] = dzv
        dxn = _dot_nt(dzu[:, :nw], w_in_v[0]) + _dot_nt(dzu[:, nw:], w_in_v[1])
        dxn += _dot_nt(dzv[:, :nw], w_in_v[2]) + _dot_nt(dzv[:, nw:], w_in_v[3])
        dxx, dn = _rms_bwd(dxn, xh, r, nmix_ref[...])
        dnmix_ref[...] += dn
        dx_ref[...] = dhv + dxx

        @pl.when(i == nt - 1)
        def _():
            for h in range(A_GROUPS):
                dws_ref[h] = dws_ref[h] * tril_ref[...]
                dbs_ref[h] = jnp.broadcast_to(jnp.sum(dbs_v[h], axis=1, keepdims=True), (CHUNK, CHUNK))

    grp = (A_GROUPS, CHUNK, CHUNK)
    return pl.pallas_call(
        body, name="mixer_a_bwd", grid=(nt,),
        in_specs=[_row_spec(tm, D_MODEL), _row_spec(tm, D_MODEL), _row_spec(tm, 2 * D_MODEL),
                  _const_spec((1, D_MODEL)), _const_spec((1, D_MODEL)), _const_spec(grp), _const_spec(grp),
                  _const_spec((CHUNK, CHUNK)), ANY, ANY],
        out_specs=[_row_spec(tm, D_MODEL), _row_spec(tm, D_MODEL), _row_spec(tm, 2 * D_MODEL),
                   _row_spec(tm, D_MODEL), _const_spec(grp), _const_spec(grp),
                   _const_spec((1, D_MODEL)), _const_spec((1, D_MODEL))],
        out_shape=[jax.ShapeDtypeStruct((T, D_MODEL), F32), jax.ShapeDtypeStruct((T, D_MODEL), BF16),
                   jax.ShapeDtypeStruct((T, 2 * D_MODEL), BF16), jax.ShapeDtypeStruct((T, D_MODEL), BF16),
                   jax.ShapeDtypeStruct(grp, F32), jax.ShapeDtypeStruct(grp, F32),
                   jax.ShapeDtypeStruct((1, D_MODEL), F32), jax.ShapeDtypeStruct((1, D_MODEL), F32)],
        scratch_shapes=[pltpu.VMEM((N_SHARD, D_MODEL, nw), BF16), pltpu.VMEM((D_MODEL, D_MODEL), BF16),
                        pltpu.VMEM((tm, D_MODEL), F32), pltpu.VMEM((tm, D_MODEL), F32),
                        pltpu.VMEM(grp, F32), pltpu.SemaphoreType.DMA((2,))],
        compiler_params=_params(),
    )(dh, x, zp, nmix, gv, wsm, bsb, tril, w_in, w_out)


def _load_ffn_weights(w_up_hbm, w_dn_hbm, layer, w_up_v, w_dn_v, sem):
    rows = D_FF // N_SHARD
    pairs = [(w_up_hbm.at[j, layer], w_up_v.at[j]) for j in range(N_SHARD)]
    pairs += [(w_dn_hbm.at[j, layer], w_dn_v.at[pl.ds(j * rows, rows)]) for j in range(N_SHARD)]
    _load_once(pairs, sem)


def _ffn_fwd(h, nffn, cw, cb, w_up, w_dn, layer):
    T = h.shape[0]
    tm = min(256, T)
    nt = T // tm

    def body(h_ref, n_ref, cw_ref, cb_ref, w_up_hbm, w_dn_hbm, out_ref, hh_ref,
             w_up_v, w_dn_v, carry_v, sem):
        _load_ffn_weights(w_up_hbm, w_dn_hbm, layer, w_up_v, w_dn_v, sem)
        _zero_first([carry_v])
        xv = h_ref[...]
        xf = _rms(xv, n_ref[...])[0].astype(BF16)
        acc = xv
        for j in range(2):
            cs = []
            for blk in (j, j + 2):
                cols = slice(blk * FF_BLK, (blk + 1) * FF_BLK)
                hh = _dot(xf, w_up_v[blk])
                hh_ref[:, cols] = hh.astype(BF16)
                ext = jnp.concatenate([carry_v[blk], hh], axis=0)
                carry_v[blk] = hh[tm - 8:, :]
                s1 = pltpu.roll(ext, 1, 0)[8:]
                s2 = pltpu.roll(ext, 2, 0)[8:]
                cs.append(cb_ref[:, cols] + cw_ref[0:1, cols] * s2 + cw_ref[1:2, cols] * s1
                          + cw_ref[2:3, cols] * hh)
            act = (cs[0] * _sigmoid(cs[0]) * cs[1]).astype(BF16)
            acc = acc + _dot(act, w_dn_v[j * FF_BLK:(j + 1) * FF_BLK, :])
        out_ref[...] = acc

    return pl.pallas_call(
        body, name=f"ffn_fwd{layer}", grid=(nt,),
        in_specs=[_row_spec(tm, D_MODEL), _const_spec((1, D_MODEL)), _const_spec((3, N_FF)),
                  _const_spec((1, N_FF)), ANY, ANY],
        out_specs=[_row_spec(tm, D_MODEL), _row_spec(tm, N_FF)],
        out_shape=[jax.ShapeDtypeStruct((T, D_MODEL), F32), jax.ShapeDtypeStruct((T, N_FF), BF16)],
        scratch_shapes=[pltpu.VMEM((N_SHARD, D_MODEL, FF_BLK), BF16), pltpu.VMEM((D_FF, D_MODEL), BF16),
                        pltpu.VMEM((N_SHARD, 8, FF_BLK), F32), pltpu.SemaphoreType.DMA((2 * N_SHARD,))],
        compiler_params=_params(),
    )(h, nffn, cw, cb, w_up, w_dn)


def _ffn_bwd(dh, h, hh, nffn, cw, cb, w_up, w_dn, layer):
    T = h.shape[0]
    tm = min(256, T)
    nt = T // tm
    pv = 16

    def body(dh_ref, h_ref, hh_ref, hhp_ref, n_ref, cw_ref, cb_ref, w_up_hbm, w_dn_hbm,
             dhin_ref, act_ref, dhh_ref, xf_ref, dcw_ref, dcb_ref, dn_ref,
             w_up_v, w_dn_v, carry_v, sem):
        _load_ffn_weights(w_up_hbm, w_dn_hbm, layer, w_up_v, w_dn_v, sem)
        _zero_first([carry_v, dcw_ref, dcb_ref, dn_ref])
        ti = nt - 1 - pl.program_id(0)
        keep_prev = jnp.where(ti > 0, 1.0, 0.0).astype(F32)
        dout = dh_ref[...]
        doutb = dout.astype(BF16)
        xf_f, xh, r = _rms(h_ref[...], n_ref[...])
        xf_ref[...] = xf_f.astype(BF16)
        dxf = jnp.zeros((tm, D_MODEL), F32)
        for j in range(2):
            fw = []
            for blk in (j, j + 2):
                cols = slice(blk * FF_BLK, (blk + 1) * FF_BLK)
                hhv = hh_ref[:, cols].astype(F32)
                prev = hhp_ref[:, cols].astype(F32) * keep_prev
                ext = jnp.concatenate([prev, hhv], axis=0)
                s1 = pltpu.roll(ext, 1, 0)[pv:]
                s2 = pltpu.roll(ext, 2, 0)[pv:]
                cv = (cb_ref[:, cols] + cw_ref[0:1, cols] * s2 + cw_ref[1:2, cols] * s1
                      + cw_ref[2:3, cols] * hhv)
                fw.append((blk, cols, hhv, s1, s2, cv))
            cg, cu = fw[0][5], fw[1][5]
            sg = _sigmoid(cg)
            sil = cg * sg
            act_ref[:, j * FF_BLK:(j + 1) * FF_BLK] = (sil * cu).astype(BF16)
            dact = _dot_nt(doutb, w_dn_v[j * FF_BLK:(j + 1) * FF_BLK, :])
            dcs = (dact * cu * (sg * (1.0 + cg * (1.0 - sg))), dact * sil)
            for (blk, cols, hhv, s1, s2, _), dc in zip(fw, dcs):
                dcb_ref[:, cols] += jnp.sum(dc, axis=0, keepdims=True)
                dcw_ref[0:1, cols] += jnp.sum(dc * s2, axis=0, keepdims=True)
                dcw_ref[1:2, cols] += jnp.sum(dc * s1, axis=0, keepdims=True)
                dcw_ref[2:3, cols] += jnp.sum(dc * hhv, axis=0, keepdims=True)
                ext = jnp.concatenate([dc, carry_v[blk]], axis=0)
                carry_v[blk] = dc[:8, :]
                n = tm + 8
                a1 = pltpu.roll(ext, n - 1, 0)[:tm]
                a2 = pltpu.roll(ext, n - 2, 0)[:tm]
                dhh = (cw_ref[2:3, cols] * dc + cw_ref[1:2, cols] * a1 + cw_ref[0:1, cols] * a2).astype(BF16)
                dhh_ref[:, cols] = dhh
                dxf = dxf + _dot_nt(dhh, w_up_v[blk])
        dxx, dn = _rms_bwd(dxf, xh, r, n_ref[...])
        dn_ref[...] += dn
        dhin_ref[...] = dout + dxx

    rev = functools.partial(_row_spec, rev_nt=nt)
    prev_spec = pl.BlockSpec((pv, N_FF), lambda i: (jnp.maximum((nt - 1 - i) * (tm // pv) - 1, 0), 0))
    return pl.pallas_call(
        body, name=f"ffn_bwd{layer}", grid=(nt,),
        in_specs=[rev(tm, D_MODEL), rev(tm, D_MODEL), rev(tm, N_FF), prev_spec,
                  _const_spec((1, D_MODEL)), _const_spec((3, N_FF)), _const_spec((1, N_FF)), ANY, ANY],
        out_specs=[rev(tm, D_MODEL), rev(tm, D_FF), rev(tm, N_FF), rev(tm, D_MODEL),
                   _const_spec((3, N_FF)), _const_spec((1, N_FF)), _const_spec((1, D_MODEL))],
        out_shape=[jax.ShapeDtypeStruct((T, D_MODEL), F32), jax.ShapeDtypeStruct((T, D_FF), BF16),
                   jax.ShapeDtypeStruct((T, N_FF), BF16), jax.ShapeDtypeStruct((T, D_MODEL), BF16),
                   jax.ShapeDtypeStruct((3, N_FF), F32), jax.ShapeDtypeStruct((1, N_FF), F32),
                   jax.ShapeDtypeStruct((1, D_MODEL), F32)],
        scratch_shapes=[pltpu.VMEM((N_SHARD, D_MODEL, FF_BLK), BF16), pltpu.VMEM((D_FF, D_MODEL), BF16),
                        pltpu.VMEM((N_SHARD, 8, FF_BLK), F32), pltpu.SemaphoreType.DMA((2 * N_SHARD,))],
        compiler_params=_params(),
    )(dh, h, hh, hh, nffn, cw, cb, w_up, w_dn)


def _load_ple_weights(w_pin_hbm, w_gate_hbm, layer, w_pin_v, w_gate_v, sem, extra=()):
    rows = D_MODEL // N_SHARD
    pairs = [(w_pin_hbm.at[j, layer], w_pin_v.at[j]) for j in range(N_SHARD)]
    pairs += [(w_gate_hbm.at[j, layer], w_gate_v.at[pl.ds(j * rows, rows)]) for j in range(N_SHARD)]
    _load_once(pairs + list(extra), sem)


def _ple_fwd_kv(h, p, nple, bg, nkv, w_pin, w_gate, w_kv):
    T = h.shape[0]
    tm = min(512, T)
    nt = T // tm
    pw = D_MODEL // N_SHARD

    def body(h_ref, p_ref, n_ref, bg_ref, nkv_ref, w_pin_hbm, w_gate_hbm, w_kv_hbm,
             out_ref, pe_ref, a_ref, kv_ref, w_pin_v, w_gate_v, w_kv_v, sem):
        _load_ple_weights(w_pin_hbm, w_gate_hbm, 0, w_pin_v, w_gate_v, sem, [(w_kv_hbm, w_kv_v)])
        xv = h_ref[...]
        xg = _rms(xv, n_ref[...])[0].astype(BF16)
        a = _dot(xg, w_gate_v[...]) + bg_ref[...]
        a_ref[...] = a
        pb = p_ref[...].astype(BF16)
        for j in range(N_SHARD):
            pe_ref[:, j * pw:(j + 1) * pw] = _dot(pb, w_pin_v[j])
        hn = xv + pe_ref[...] * _sigmoid(a)
        out_ref[...] = hn
        kvn = _rms(hn, nkv_ref[...])[0].astype(BF16)
        kv_ref[...] = _dot(kvn, w_kv_v[...]).astype(BF16)

    vec = _const_spec((1, D_MODEL))
    return pl.pallas_call(
        body, name="ple_fwd0", grid=(nt,),
        in_specs=[_row_spec(tm, D_MODEL), _row_spec(tm, PLE_DIM), vec, vec, vec, ANY, ANY, ANY],
        out_specs=[_row_spec(tm, D_MODEL), _row_spec(tm, D_MODEL), _row_spec(tm, D_MODEL),
                   _row_spec(tm, 2 * KV_DIM)],
        out_shape=[jax.ShapeDtypeStruct((T, D_MODEL), F32), jax.ShapeDtypeStruct((T, D_MODEL), F32),
                   jax.ShapeDtypeStruct((T, D_MODEL), F32), jax.ShapeDtypeStruct((T, 2 * KV_DIM), BF16)],
        scratch_shapes=[pltpu.VMEM((N_SHARD, PLE_DIM, pw), BF16), pltpu.VMEM((D_MODEL, D_MODEL), BF16),
                        pltpu.VMEM((D_MODEL, 2 * KV_DIM), BF16), pltpu.SemaphoreType.DMA((2 * N_SHARD + 1,))],
        compiler_params=_params(),
    )(h, p, nple, bg, nkv, w_pin, w_gate, w_kv)


def _ple_fwd_final(h, p, tgt, nple, bg, nfin, w_pin, w_gate):
    T = h.shape[0]
    tm = min(512, T)
    nt = T // tm
    pw = D_MODEL // N_SHARD

    def body(h_ref, p_ref, t_ref, n_ref, bg_ref, nf_ref, w_pin_hbm, w_gate_hbm,
             dh_ref, pe_ref, a_ref, loss_ref, dnf_ref, w_pin_v, w_gate_v, sem):
        _load_ple_weights(w_pin_hbm, w_gate_hbm, 1, w_pin_v, w_gate_v, sem)
        _zero_first([loss_ref, dnf_ref])
        xv = h_ref[...]
        xg = _rms(xv, n_ref[...])[0].astype(BF16)
        a = _dot(xg, w_gate_v[...]) + bg_ref[...]
        a_ref[...] = a
        pb = p_ref[...].astype(BF16)
        for j in range(N_SHARD):
            pe_ref[:, j * pw:(j + 1) * pw] = _dot(pb, w_pin_v[j])
        hn = xv + pe_ref[...] * _sigmoid(a)
        y, yh, r = _rms(hn, nf_ref[...])
        diff = y - t_ref[...]
        loss_ref[...] += 0.5 * jnp.sum(jnp.mean(diff * diff, axis=-1, keepdims=True))
        dy = diff * (1.0 / D_MODEL)
        dhn, dnf = _rms_bwd(dy, yh, r, nf_ref[...])
        dnf_ref[...] += dnf
        dh_ref[...] = dhn

    vec = _const_spec((1, D_MODEL))
    return pl.pallas_call(
        body, name="ple_fwd1", grid=(nt,),
        in_specs=[_row_spec(tm, D_MODEL), _row_spec(tm, PLE_DIM), _row_spec(tm, D_MODEL), vec, vec, vec, ANY, ANY],
        out_specs=[_row_spec(tm, D_MODEL), _row_spec(tm, D_MODEL), _row_spec(tm, D_MODEL),
                   _const_spec((8, 128)), vec],
        out_shape=[jax.ShapeDtypeStruct((T, D_MODEL), F32), jax.ShapeDtypeStruct((T, D_MODEL), F32),
                   jax.ShapeDtypeStruct((T, D_MODEL), F32), jax.ShapeDtypeStruct((8, 128), F32),
                   jax.ShapeDtypeStruct((1, D_MODEL), F32)],
        scratch_shapes=[pltpu.VMEM((N_SHARD, PLE_DIM, pw), BF16), pltpu.VMEM((D_MODEL, D_MODEL), BF16),
                        pltpu.SemaphoreType.DMA((2 * N_SHARD,))],
        compiler_params=_params(),
    )(h, p, tgt, nple, bg, nfin, w_pin, w_gate)


def _ple_bwd(dh, hb, pe, a, nple, w_gate, layer, kv_args=None):
    T = hb.shape[0]
    tm = min(512, T)
    nt = T // tm
    with_kv = kv_args is not None
    rows = D_MODEL // N_SHARD

    def body(*refs):
        if with_kv:
            (dh_ref, hb_ref, pe_ref, a_ref, n_ref, w_gate_hbm, hc_ref, dkv_ref, nkv_ref, w_kv_hbm,
             dhb_ref, dpe_ref, da_ref, xg_ref, dbg_ref, dn_ref, kvn_ref, dnkv_ref,
             w_gate_v, w_kv_v, sem) = refs
        else:
            (dh_ref, hb_ref, pe_ref, a_ref, n_ref, w_gate_hbm,
             dhb_ref, dpe_ref, da_ref, xg_ref, dbg_ref, dn_ref, w_gate_v, sem) = refs
        pairs = [(w_gate_hbm.at[j, layer], w_gate_v.at[pl.ds(j * rows, rows)]) for j in range(N_SHARD)]
        if with_kv:
            pairs.append((w_kv_hbm, w_kv_v))
        _load_once(pairs, sem)
        _zero_first([dbg_ref, dn_ref] + ([dnkv_ref] if with_kv else []))
        do = dh_ref[...]
        if with_kv:
            dkvn = _dot_nt(dkv_ref[...].astype(BF16), w_kv_v[...])
            kvn, kh, kr = _rms(hc_ref[...], nkv_ref[...])
            kvn_ref[...] = kvn.astype(BF16)
            dk, dnkv = _rms_bwd(dkvn, kh, kr, nkv_ref[...])
            dnkv_ref[...] += dnkv
            do = do + dk
        gate = _sigmoid(a_ref[...])
        dpe_ref[...] = (do * gate).astype(BF16)
        da = do * pe_ref[...] * (gate * (1.0 - gate))
        dab = da.astype(BF16)
        da_ref[...] = dab
        dbg_ref[...] += jnp.sum(da, axis=0, keepdims=True)
        dxg = _dot_nt(dab, w_gate_v[...])
        xg, xh, r = _rms(hb_ref[...], n_ref[...])
        xg_ref[...] = xg.astype(BF16)
        dxx, dn = _rms_bwd(dxg, xh, r, n_ref[...])
        dn_ref[...] += dn
        dhb_ref[...] = do + dxx

    vec = _const_spec((1, D_MODEL))
    row = _row_spec(tm, D_MODEL)
    in_specs = [row, row, row, row, vec, ANY]
    args = [dh, hb, pe, a, nple, w_gate]
    out_specs = [row, row, row, row, vec, vec]
    out_shape = [jax.ShapeDtypeStruct((T, D_MODEL), F32), jax.ShapeDtypeStruct((T, D_MODEL), BF16),
                 jax.ShapeDtypeStruct((T, D_MODEL), BF16), jax.ShapeDtypeStruct((T, D_MODEL), BF16),
                 jax.ShapeDtypeStruct((1, D_MODEL), F32), jax.ShapeDtypeStruct((1, D_MODEL), F32)]
    scratch = [pltpu.VMEM((D_MODEL, D_MODEL), BF16)]
    if with_kv:
        hc, dkv, nkv, w_kv = kv_args
        in_specs += [row, _row_spec(tm, 2 * KV_DIM), vec, ANY]
        args += [hc, dkv, nkv, w_kv]
        out_specs += [row, vec]
        out_shape += [jax.ShapeDtypeStruct((T, D_MODEL), BF16), jax.ShapeDtypeStruct((1, D_MODEL), F32)]
        scratch.append(pltpu.VMEM((D_MODEL, 2 * KV_DIM), BF16))
    scratch.append(pltpu.SemaphoreType.DMA((N_SHARD + 1,)))
    return pl.pallas_call(
        body, name=f"ple_bwd{layer}", grid=(nt,), in_specs=in_specs, out_specs=out_specs,
        out_shape=out_shape, scratch_shapes=scratch, compiler_params=_params(),
    )(*args)


def _band_masks(is_first):
    ii = lax.broadcasted_iota(jnp.int32, (BLOCK, 2 * BLOCK), 0)
    jj = lax.broadcasted_iota(jnp.int32, (BLOCK, 2 * BLOCK), 1)
    dist = ii + BLOCK - jj
    valid = (dist >= 0) & (dist < BLOCK) & ((jj >= BLOCK) | jnp.logical_not(is_first))
    return dist.astype(F32), valid


def _attn_fwd(h, nmix, kv, sinks, w_q, w_o):
    T = h.shape[0]
    tm = min(512, T)
    nt = T // tm
    nb = tm // BLOCK

    def body(h_ref, n_ref, kv_ref, kvp_ref, sink_ref, w_q_hbm, w_o_hbm,
             out_ref, q_ref, ao_ref, lse_ref, w_q_v, w_o_v, kvs_v, sem):
        _load_once([(w_q_hbm, w_q_v), (w_o_hbm, w_o_v)], sem)
        ti = pl.program_id(0)
        xv = h_ref[...]
        xn = _rms(xv, n_ref[...])[0].astype(BF16)
        q_ref[...] = (_dot(xn, w_q_v[...]) * (HEAD_DIM ** -0.5)).astype(BF16)
        kvs_v[0:BLOCK, :] = kvp_ref[...]
        kvs_v[BLOCK:, :] = kv_ref[...]
        lane = lax.broadcasted_iota(jnp.int32, (BLOCK, 128), 1)

        def blk_body(b, carry):
            r0 = pl.multiple_of(b * BLOCK, BLOCK)
            distf, valid = _band_masks(jnp.logical_and(ti == 0, b == 0))
            qb = q_ref[pl.ds(r0, BLOCK), :]
            band = kvs_v[pl.ds(r0, 2 * BLOCK), :]
            lse_mat = jnp.zeros((BLOCK, 128), F32)
            outs = []
            for hq in range(N_Q_HEADS):
                kh = hq // GQA_GROUP
                k_h = band[:, kh * HEAD_DIM:(kh + 1) * HEAD_DIM]
                v_h = band[:, KV_DIM + kh * HEAD_DIM:KV_DIM + (kh + 1) * HEAD_DIM]
                s = _dot_nt(qb[:, hq * HEAD_DIM:(hq + 1) * HEAD_DIM], k_h) - _SLOPES[hq] * distf
                s = jnp.where(valid, s, NEG)
                sink = sink_ref[hq]
                m = jnp.maximum(jnp.max(s, axis=1, keepdims=True), sink)
                e = jnp.exp(s - m)
                den = jnp.sum(e, axis=1, keepdims=True) + jnp.exp(sink - m)
                outs.append(_dot((e / den).astype(BF16), v_h))
                lse_mat = jnp.where(lane == hq, m + jnp.log(den), lse_mat)
            ao_ref[pl.ds(r0, BLOCK), :] = jnp.concatenate(outs, axis=1).astype(BF16)
            lse_ref[pl.ds(r0, BLOCK), :] = lse_mat
            return carry

        lax.fori_loop(0, nb, blk_body, 0)
        out_ref[...] = xv + _dot(ao_ref[...], w_o_v[...])

    row = _row_spec(tm, D_MODEL)
    prev_spec = pl.BlockSpec((BLOCK, 2 * KV_DIM), lambda i: (jnp.maximum(i * nb - 1, 0), 0))
    return pl.pallas_call(
        body, name="attn_fwd", grid=(nt,),
        in_specs=[row, _const_spec((1, D_MODEL)), _row_spec(tm, 2 * KV_DIM), prev_spec, SMEM, ANY, ANY],
        out_specs=[row, row, row, _row_spec(tm, 128)],
        out_shape=[jax.ShapeDtypeStruct((T, D_MODEL), F32), jax.ShapeDtypeStruct((T, D_MODEL), BF16),
                   jax.ShapeDtypeStruct((T, D_MODEL), BF16), jax.ShapeDtypeStruct((T, 128), F32)],
        scratch_shapes=[pltpu.VMEM((D_MODEL, D_MODEL), BF16), pltpu.VMEM((D_MODEL, D_MODEL), BF16),
                        pltpu.VMEM((tm + BLOCK, 2 * KV_DIM), BF16), pltpu.SemaphoreType.DMA((2,))],
        compiler_params=_params(),
    )(h, nmix, kv, kv, sinks, w_q, w_o)


def _attn_bwd(dh, h, q, kv, ao, lse, nmix, sinks, w_q, w_o):
    T = h.shape[0]
    tm = min(512, T)
    nt = T // tm
    nb = tm // BLOCK

    def body(dh_ref, h_ref, q_ref, kv_ref, kvp_ref, ao_ref, lse_ref, n_ref, sink_ref, w_q_hbm, w_o_hbm,
             dhin_ref, dq_ref, xn_ref, dkv_ref, dsink_ref, dn_ref,
             w_q_v, w_o_v, kvs_v, dao_v, dq_v, dkv_v, carry_v, sem):
        _load_once([(w_q_hbm, w_q_v), (w_o_hbm, w_o_v)], sem)
        _zero_first([carry_v, dsink_ref, dn_ref])
        ti = nt - 1 - pl.program_id(0)
        dout = dh_ref[...]
        dao_v[...] = _dot_nt(dout.astype(BF16), w_o_v[...])
        kvs_v[0:BLOCK, :] = kvp_ref[...]
        kvs_v[BLOCK:, :] = kv_ref[...]
        dkv_v[0:tm, :] = jnp.zeros((tm, 2 * KV_DIM), F32)
        dkv_v[tm:, :] = carry_v[...]
        lane = lax.broadcasted_iota(jnp.int32, (BLOCK, 128), 1)
        lane8 = lax.broadcasted_iota(jnp.int32, (8, 128), 1)

        def blk_body(b, dsk):
            r0 = pl.multiple_of(b * BLOCK, BLOCK)
            distf, valid = _band_masks(jnp.logical_and(ti == 0, b == 0))
            qb = q_ref[pl.ds(r0, BLOCK), :]
            band = kvs_v[pl.ds(r0, 2 * BLOCK), :]
            aob = ao_ref[pl.ds(r0, BLOCK), :].astype(F32)
            daob = dao_v[pl.ds(r0, BLOCK), :]
            lse_mat = lse_ref[pl.ds(r0, BLOCK), :]
            dqs = []
            dks = []
            dvs = []
            for kh in range(N_KV_HEADS):
                k_h = band[:, kh * HEAD_DIM:(kh + 1) * HEAD_DIM]
                v_h = band[:, KV_DIM + kh * HEAD_DIM:KV_DIM + (kh + 1) * HEAD_DIM]
                dk = jnp.zeros((2 * BLOCK, HEAD_DIM), F32)
                dv = jnp.zeros((2 * BLOCK, HEAD_DIM), F32)
                for g in range(GQA_GROUP):
                    hq = kh * GQA_GROUP + g
                    hc = slice(hq * HEAD_DIM, (hq + 1) * HEAD_DIM)
                    q_h = qb[:, hc]
                    s = _dot_nt(q_h, k_h) - _SLOPES[hq] * distf
                    s = jnp.where(valid, s, NEG)
                    lse = jnp.sum(jnp.where(lane == hq, lse_mat, 0.0), axis=1, keepdims=True)
                    pr = jnp.exp(s - lse)
                    dao_h = daob[:, hc]
                    dd = jnp.sum(dao_h * aob[:, hc], axis=1, keepdims=True)
                    dao_hb = dao_h.astype(BF16)
                    dp = _dot_nt(dao_hb, v_h)
                    dsb = (pr * (dp - dd)).astype(BF16)
                    dqs.append(_dot(dsb, k_h) * (HEAD_DIM ** -0.5))
                    dk = dk + _dot_tn(dsb, q_h)
                    dv = dv + _dot_tn(pr.astype(BF16), dao_hb)
                    dsv = -jnp.sum(jnp.exp(sink_ref[hq] - lse) * dd)
                    dsk = dsk + jnp.where(lane8 == hq, dsv, 0.0)
                dks.append(dk)
                dvs.append(dv)
            dq_v[pl.ds(r0, BLOCK), :] = jnp.concatenate(dqs, axis=1)
            dkv_v[pl.ds(r0, 2 * BLOCK), :] += jnp.concatenate(dks + dvs, axis=1)
            return dsk

        dsk = lax.fori_loop(0, nb, blk_body, jnp.zeros((8, 128), F32))
        dsink_ref[...] += dsk
        dqb = dq_v[...].astype(BF16)
        dq_ref[...] = dqb
        dxn = _dot_nt(dqb, w_q_v[...])
        xn, xh, r = _rms(h_ref[...], n_ref[...])
        xn_ref[...] = xn.astype(BF16)
        dxx, dn = _rms_bwd(dxn, xh, r, n_ref[...])
        dn_ref[...] += dn
        dhin_ref[...] = dout + dxx
        dkv_ref[...] = dkv_v[BLOCK:, :]
        carry_v[...] = dkv_v[0:BLOCK, :]

    rev = functools.partial(_row_spec, rev_nt=nt)
    row = rev(tm, D_MODEL)
    prev_spec = pl.BlockSpec((BLOCK, 2 * KV_DIM), lambda i: (jnp.maximum((nt - 1 - i) * nb - 1, 0), 0))
    return pl.pallas_call(
        body, name="attn_bwd", grid=(nt,),
        in_specs=[row, row, row, rev(tm, 2 * KV_DIM), prev_spec, row, rev(tm, 128),
                  _const_spec((1, D_MODEL)), SMEM, ANY, ANY],
        out_specs=[row, row, row, rev(tm, 2 * KV_DIM), _const_spec((8, 128)), _const_spec((1, D_MODEL))],
        out_shape=[jax.ShapeDtypeStruct((T, D_MODEL), F32), jax.ShapeDtypeStruct((T, D_MODEL), BF16),
                   jax.ShapeDtypeStruct((T, D_MODEL), BF16), jax.ShapeDtypeStruct((T, 2 * KV_DIM), F32),
                   jax.ShapeDtypeStruct((8, 128), F32), jax.ShapeDtypeStruct((1, D_MODEL), F32)],
        scratch_shapes=[pltpu.VMEM((D_MODEL, D_MODEL), BF16), pltpu.VMEM((D_MODEL, D_MODEL), BF16),
                        pltpu.VMEM((tm + BLOCK, 2 * KV_DIM), BF16), pltpu.VMEM((tm, D_MODEL), F32),
                        pltpu.VMEM((tm, D_MODEL), F32), pltpu.VMEM((tm + BLOCK, 2 * KV_DIM), F32),
                        pltpu.VMEM((BLOCK, 2 * KV_DIM), F32), pltpu.SemaphoreType.DMA((2,))],
        compiler_params=_params(),
    )(dh, h, q, kv, kv, ao, lse, nmix, sinks, w_q, w_o)


def _wgrad(a, b, bk, bn, col_blocks, name):
    T, K = a.shape
    N = b.shape[1]
    tt = min(1024, T)
    nk, nn, ntt = K // bk, N // bn, T // tt

    def body(a_ref, b_ref, o_ref):
        @pl.when(pl.program_id(2) == 0)
        def _():
            o_ref[...] = jnp.zeros(o_ref.shape, F32)
        o_ref[...] += _dot_tn(a_ref[...].astype(BF16), b_ref[...].astype(BF16))

    if col_blocks:
        assert nk == 1
        out_spec = pl.BlockSpec((None, K, bn), lambda k, n, t: (n, 0, 0))
        out_shape = jax.ShapeDtypeStruct((nn, K, bn), F32)
    else:
        out_spec = pl.BlockSpec((bk, bn), lambda k, n, t: (k, n))
        out_shape = jax.ShapeDtypeStruct((K, N), F32)
    return pl.pallas_call(
        body, name=name, grid=(nk, nn, ntt),
        in_specs=[pl.BlockSpec((tt, bk), lambda k, n, t: (t, k)), pl.BlockSpec((tt, bn), lambda k, n, t: (t, n))],
        out_specs=out_spec, out_shape=out_shape,
        compiler_params=pltpu.CompilerParams(dimension_semantics=("arbitrary",) * 3, vmem_limit_bytes=VMEM_LIMIT),
    )(a, b)


def _mesh_pos():
    return lax.axis_index("x"), lax.axis_index("y"), lax.axis_index("c")


def _other_chips(x, y):
    return [(1 - x, y), (x, 1 - y), (1 - x, 1 - y)]


def _allgather_halves(shards):
    n = len(shards)

    def body(*refs):
        ins, outs = refs[:n], refs[n:2 * n]
        send_sems, recv_sems, loc_sems = refs[2 * n:]
        x, y, c = _mesh_pos()
        sibling = (x, y, 1 - c)
        chips = _other_chips(x, y)

        def blk(t, px, py, pc):
            return outs[t].at[2 * px + py, pc]

        def copy(t, k, block, to, src=None):
            return pltpu.make_async_remote_copy(
                src_ref=blk(t, *block) if src is None else src, dst_ref=blk(t, *block),
                send_sem=send_sems.at[t, k], recv_sem=recv_sems.at[t, k],
                device_id=to, device_id_type=MESH)

        me = (x, y, c)
        mine = [pltpu.make_async_copy(ins[t].at[c], blk(t, *me), loc_sems.at[t]) for t in range(n)]
        for cp in mine:
            cp.start()
        first = []
        for t in range(n):
            first.append(copy(t, 0, me, sibling, src=ins[t].at[c]))
            first += [copy(t, 1 + j, me, (*chip, c), src=ins[t].at[c]) for j, chip in enumerate(chips)]
        for cp in first:
            cp.start()
        passed = []
        for j, chip in enumerate(chips):
            for t in range(n):
                copy(t, 1 + j, (*chip, c), me).wait_recv()
                cp = copy(t, 4 + j, (*chip, c), sibling)
                cp.start()
                passed.append(cp)
        for t in range(n):
            copy(t, 0, (x, y, 1 - c), me).wait_recv()
            for j, chip in enumerate(chips):
                copy(t, 4 + j, (*chip, 1 - c), me).wait_recv()
        for cp in first + passed:
            cp.wait_send()
        for cp in mine:
            cp.wait()

    return pl.pallas_call(
        body, name="weights_allgather",
        in_specs=[ANY] * n, out_specs=[ANY] * n,
        out_shape=[jax.ShapeDtypeStruct((N_SHARD,) + s.shape, s.dtype) for s in shards],
        scratch_shapes=[pltpu.SemaphoreType.DMA((n, 7)), pltpu.SemaphoreType.DMA((n, 7)),
                        pltpu.SemaphoreType.DMA((n,))],
    )(*shards)


def _sibling_send_halves(grads, small):
    n = len(grads)

    def body(*refs):
        ins, sm_in = refs[:n], refs[n]
        outs, sm_out = refs[n + 1:2 * n + 1], refs[2 * n + 1]
        send_sems, recv_sems = refs[2 * n + 2:]
        x, y, c = _mesh_pos()
        sibling = (x, y, 1 - c)
        cps = []
        for t in range(n):
            half = ins[t].shape[1] // 2
            src = ins[t].at[:, pl.ds((1 - c) * half, half), :]
            cps.append(pltpu.make_async_remote_copy(
                src_ref=src, dst_ref=outs[t], send_sem=send_sems.at[t], recv_sem=recv_sems.at[t],
                device_id=sibling, device_id_type=MESH))
        cps.append(pltpu.make_async_remote_copy(
            src_ref=sm_in, dst_ref=sm_out, send_sem=send_sems.at[n], recv_sem=recv_sems.at[n],
            device_id=sibling, device_id_type=MESH))
        for cp in cps:
            cp.start()
        for cp in cps:
            cp.wait()

    out_shape = [jax.ShapeDtypeStruct((N_SHARD, g.shape[1] // 2, g.shape[2]), F32) for g in grads]
    out_shape.append(jax.ShapeDtypeStruct(small.shape, F32))
    return pl.pallas_call(
        body, name="grads_sibling_send",
        in_specs=[ANY] * (n + 1), out_specs=[ANY] * (n + 1), out_shape=out_shape,
        scratch_shapes=[pltpu.SemaphoreType.DMA((n + 1,)), pltpu.SemaphoreType.DMA((n + 1,))],
    )(*grads, small)


def _add_own_half(g, s, c_arr, name):
    nsh, half, cols = s.shape
    br = half
    for cand in (512, 352, 256, 128):
        if half % cand == 0:
            br = cand
            break
    nr = half // br

    def body(c_ref, g_ref, s_ref, o_ref):
        o_ref[...] = g_ref[...] + s_ref[...]

    return pl.pallas_call(
        body, name=name,
        grid_spec=pltpu.PrefetchScalarGridSpec(
            num_scalar_prefetch=1, grid=(nsh, nr),
            in_specs=[pl.BlockSpec((None, br, cols), lambda j, r, c_ref: (j, c_ref[0] * nr + r, 0)),
                      pl.BlockSpec((None, br, cols), lambda j, r, c_ref: (j, r, 0))],
            out_specs=pl.BlockSpec((None, br, cols), lambda j, r, c_ref: (j, r, 0))),
        out_shape=jax.ShapeDtypeStruct(s.shape, F32),
        compiler_params=pltpu.CompilerParams(dimension_semantics=("arbitrary", "arbitrary")),
    )(c_arr, g, s)


def _chip_exchange(parts, small):
    n = len(parts)

    def body(*refs):
        ins, sm_in = refs[:n], refs[n]
        outs, sm_out = refs[n + 1:2 * n + 1], refs[2 * n + 1]
        send_sems, recv_sems = refs[2 * n + 2:]
        x, y, c = _mesh_pos()
        cps = []
        for j, (cx, cy) in enumerate(_other_chips(x, y)):
            for t in range(n):
                cps.append(pltpu.make_async_remote_copy(
                    src_ref=ins[t].at[2 * cx + cy], dst_ref=outs[t].at[j],
                    send_sem=send_sems.at[t, j], recv_sem=recv_sems.at[t, j],
                    device_id=(cx, cy, c), device_id_type=MESH))
            cps.append(pltpu.make_async_remote_copy(
                src_ref=sm_in, dst_ref=sm_out.at[j], send_sem=send_sems.at[n, j], recv_sem=recv_sems.at[n, j],
                device_id=(cx, cy, c), device_id_type=MESH))
        for cp in cps:
            cp.start()
        for cp in cps:
            cp.wait()

    out_shape = [jax.ShapeDtypeStruct((3,) + p.shape[1:], F32) for p in parts]
    out_shape.append(jax.ShapeDtypeStruct((3,) + small.shape, F32))
    return pl.pallas_call(
        body, name="grads_chip_exchange",
        in_specs=[ANY] * (n + 1), out_specs=[ANY] * (n + 1), out_shape=out_shape,
        scratch_shapes=[pltpu.SemaphoreType.DMA((n + 1, 3)), pltpu.SemaphoreType.DMA((n + 1, 3))],
    )(*parts, small)


def _sum_chips(part, recv, s_arr, name):
    own_indexed = part.ndim == 3
    rows, cols = recv.shape[1:]
    br = rows
    for cand in (512, 352, 296, 256, 128):
        if rows % cand == 0:
            br = cand
            break
    nr = rows // br

    def body(s_ref, p_ref, q_ref, o_ref):
        o_ref[...] = (p_ref[...] + q_ref[2]) + (q_ref[0] + q_ref[1])

    if own_indexed:
        p_spec = pl.BlockSpec((None, br, cols), lambda r, s_ref: (s_ref[0], r, 0))
    else:
        p_spec = pl.BlockSpec((br, cols), lambda r, s_ref: (r, 0))
    return pl.pallas_call(
        body, name=name,
        grid_spec=pltpu.PrefetchScalarGridSpec(
            num_scalar_prefetch=1, grid=(nr,),
            in_specs=[p_spec, pl.BlockSpec((3, br, cols), lambda r, s_ref: (0, r, 0))],
            out_specs=pl.BlockSpec((br, cols), lambda r, s_ref: (r, 0))),
        out_shape=jax.ShapeDtypeStruct((rows, cols), F32),
        compiler_params=pltpu.CompilerParams(dimension_semantics=("arbitrary",)),
    )(s_arr, part, recv)


def _sibling_merge(halves, groups):
    n = len(halves)
    ng = len(groups)

    def body(*refs):
        ins, outs = refs[:n], refs[n:n + ng]
        send_sems, recv_sems, loc_sems = refs[n + ng:]
        x, y, c = _mesh_pos()
        sibling = (x, y, 1 - c)
        locs, rems = [], []
        for gi, members in enumerate(groups):
            for layer, t in enumerate(members):
                half = ins[t].shape[0]
                dst = outs[gi].at[layer, pl.ds(c * half, half), :]
                locs.append(pltpu.make_async_copy(ins[t], dst, loc_sems.at[t]))
                rems.append(pltpu.make_async_remote_copy(
                    src_ref=ins[t], dst_ref=dst, send_sem=send_sems.at[t], recv_sem=recv_sems.at[t],
                    device_id=sibling, device_id_type=MESH))
        for cp in locs + rems:
            cp.start()
        for cp in rems:
            cp.wait_send()
        for gi, members in enumerate(groups):
            for layer, t in enumerate(members):
                half = ins[t].shape[0]
                dst = outs[gi].at[layer, pl.ds((1 - c) * half, half), :]
                pltpu.make_async_remote_copy(
                    src_ref=ins[t], dst_ref=dst, send_sem=send_sems.at[t], recv_sem=recv_sems.at[t],
                    device_id=sibling, device_id_type=MESH).wait_recv()
        for cp in locs:
            cp.wait()

    out_shape = [jax.ShapeDtypeStruct((len(m), 2 * halves[m[0]].shape[0], halves[m[0]].shape[1]), F32)
                 for m in groups]
    return pl.pallas_call(
        body, name="grads_sibling_merge",
        in_specs=[ANY] * n, out_specs=[ANY] * ng, out_shape=out_shape,
        scratch_shapes=[pltpu.SemaphoreType.DMA((n,)), pltpu.SemaphoreType.DMA((n,)),
                        pltpu.SemaphoreType.DMA((n,))],
    )(*halves)


def _adamw(w, g, m, v, name):
    R, C = w.shape
    br = R
    for cand in (512, 352, 296, 256, 128, 64, 8):
        if R % cand == 0 and cand * C * 4 <= 2 * 1024 * 1024:
            br = cand
            break

    def body(w_ref, g_ref, m_ref, v_ref, d_ref, mo_ref, vo_ref):
        gv = g_ref[...]
        mn = ADAM_B1 * m_ref[...] + (1.0 - ADAM_B1) * gv
        vn = ADAM_B2 * v_ref[...] + (1.0 - ADAM_B2) * (gv * gv)
        m_hat = mn / (1.0 - ADAM_B1 ** ADAM_STEP)
        v_hat = vn / (1.0 - ADAM_B2 ** ADAM_STEP)
        d_ref[...] = -ADAM_LR * (m_hat / (jnp.sqrt(v_hat) + ADAM_EPS) + ADAM_WD * w_ref[...])
        mo_ref[...] = mn
        vo_ref[...] = vn

    spec = pl.BlockSpec((br, C), lambda i: (i, 0))
    return pl.pallas_call(
        body, name=name, grid=(R // br,), in_specs=[spec] * 4, out_specs=[spec] * 3,
        out_shape=[jax.ShapeDtypeStruct((R, C), F32)] * 3, compiler_params=_params(),
    )(w, g, m, v)


_PACK_UNIT = 1024


def _pack(arrs):
    flat = []
    for a in arrs:
        f = a.reshape(-1).astype(F32)
        pad = (-f.shape[0]) % _PACK_UNIT
        if pad:
            f = jnp.concatenate([f, jnp.zeros((pad,), F32)])
        flat.append(f)
    return jnp.concatenate(flat).reshape(-1, 128)


def _unpack(packed, shapes):
    flat = packed.reshape(-1)
    out, off = [], 0
    for shp in shapes:
        size = int(np.prod(shp))
        out.append(flat[off:off + size].reshape(shp))
        off += size + ((-size) % _PACK_UNIT)
    return out


def kernel(x, p, norm_mix, norm_ffn, norm_ple, norm_kv, norm_final, a_w_in, a_norm_v, a_w_s, a_b_s, a_w_out, w_kv, b_w_q, b_sinks, b_w_o, f_w_up, f_conv_w, f_conv_b, f_w_down, ple_w_in, ple_w_gate, ple_b_gate, loss_target, m_norm_mix, m_norm_ffn, m_norm_ple, m_norm_kv, m_norm_final, m_a_w_in, m_a_norm_v, m_a_w_s, m_a_b_s, m_a_w_out, m_w_kv, m_b_w_q, m_b_sinks, m_b_w_o, m_f_w_up, m_f_conv_w, m_f_conv_b, m_f_w_down, m_ple_w_in, m_ple_w_gate, m_ple_b_gate, v_norm_mix, v_norm_ffn, v_norm_ple, v_norm_kv, v_norm_final, v_a_w_in, v_a_norm_v, v_a_w_s, v_a_b_s, v_a_w_out, v_w_kv, v_b_w_q, v_b_sinks, v_b_w_o, v_f_w_up, v_f_conv_w, v_f_conv_b, v_f_w_down, v_ple_w_in, v_ple_w_gate, v_ple_b_gate):
    given = dict(locals())

    def halves(a):
        a = a.astype(BF16)
        if a.shape[0] == 2:
            return a
        a = a.reshape(a.shape[-2:])
        return a.reshape(2, a.shape[0] // 2, a.shape[1])

    small_shard = _pack([a_norm_v, f_conv_w])
    pad_rows = (-small_shard.shape[0]) % 16
    if pad_rows:
        small_shard = jnp.concatenate([small_shard, jnp.zeros((pad_rows, 128), F32)])
    small_shard = small_shard.reshape(2, small_shard.shape[0] // 2, 128)
    big_names = ['a_w_in', 'a_w_out', 'w_kv', 'b_w_q', 'b_w_o', 'f_w_up', 'f_w_down', 'ple_w_in', 'ple_w_gate']
    gathered = _allgather_halves([halves(given[k]) for k in big_names] + [small_shard])
    G = dict(zip(big_names, gathered[:-1]))
    small_full = gathered[-1].reshape(N_SHARD, -1)
    gv_full = small_full[:, :256].reshape(1, D_MODEL)
    cw_full = small_full[:, _PACK_UNIT:_PACK_UNIT + 2 * 3 * FF_BLK].reshape(N_SHARD, 2, 3, FF_BLK)
    cw_full = jnp.transpose(cw_full, (1, 2, 0, 3)).reshape(2, 3, N_FF)

    w_in = G['a_w_in'].reshape(N_SHARD, D_MODEL, 2 * D_MODEL // N_SHARD)
    w_out = G['a_w_out'].reshape(D_MODEL, D_MODEL)
    w_kv_f = G['w_kv'].reshape(D_MODEL, 2 * KV_DIM)
    w_q = G['b_w_q'].reshape(D_MODEL, D_MODEL)
    w_o = G['b_w_o'].reshape(D_MODEL, D_MODEL)
    w_up, w_dn, w_pin, w_gate = G['f_w_up'], G['f_w_down'], G['ple_w_in'], G['ple_w_gate']

    loss_acc, dx, grads_local = _local_step(
        x[0], p[0, 0], p[1, 0], loss_target[0], norm_mix, norm_ffn, norm_ple, norm_kv, norm_final, a_w_s, a_b_s,
        b_sinks, f_conv_b, ple_b_gate, gv_full, cw_full, w_in, w_out, w_kv_f, w_q, w_o, w_up, w_dn, w_pin, w_gate)
    (g_win, g_wout, g_wkv, g_wq, g_wo, g_up0, g_up1, g_dn0, g_dn1, g_pin0, g_pin1, g_gate0, g_gate1,
     small_grads) = grads_local
    return _reduce_and_update(given, dx, loss_acc, g_win, g_wout, g_wkv, g_wq, g_wo, g_up0, g_up1, g_dn0, g_dn1,
                              g_pin0, g_pin1, g_gate0, g_gate1, small_grads)


def _local_step(xs, p0, p1, tgt, norm_mix, norm_ffn, norm_ple, norm_kv, norm_final, a_w_s, a_b_s, b_sinks,
                f_conv_b, ple_b_gate, gv_full, cw_full, w_in, w_out, w_kv_f, w_q, w_o, w_up, w_dn, w_pin, w_gate):
    tril = jnp.tril(jnp.ones((CHUNK, CHUNK), F32))
    wsm = (a_w_s[0] * tril[None]).astype(BF16)
    bsb = jnp.broadcast_to(a_b_s[0][:, :, None], (A_GROUPS, CHUNK, CHUNK))
    sinks = b_sinks[0]
    row = lambda a: a.reshape(1, -1)

    h1, zp = _mixer_a_fwd(xs, row(norm_mix[0]), gv_full, wsm, bsb, w_in, w_out)
    h2, hh0 = _ffn_fwd(h1, row(norm_ffn[0]), cw_full[0], row(f_conv_b[0]), w_up, w_dn, 0)
    h3, pe0, a0, kv = _ple_fwd_kv(h2, p0, row(norm_ple[0]), row(ple_b_gate[0]), row(norm_kv), w_pin, w_gate, w_kv_f)
    h4, q, ao, lse = _attn_fwd(h3, row(norm_mix[1]), kv, sinks, w_q, w_o)
    h5, hh1 = _ffn_fwd(h4, row(norm_ffn[1]), cw_full[1], row(f_conv_b[1]), w_up, w_dn, 1)
    dh6, pe1, a1, loss_acc, dn_final = _ple_fwd_final(
        h5, p1, tgt, row(norm_ple[1]), row(ple_b_gate[1]), row(norm_final), w_pin, w_gate)

    dh5, dpe1, da1, xg1, dbg1, dnple1 = _ple_bwd(dh6, h5, pe1, a1, row(norm_ple[1]), w_gate, 1)
    g_pin1 = _wgrad(p1, dpe1, PLE_DIM, D_MODEL // N_SHARD, True, "wgrad_ple_in1")
    g_gate1 = _wgrad(xg1, da1, D_MODEL, D_MODEL, False, "wgrad_ple_gate1")
    dh4, act1, dhh1, xf1, dcw1, dcb1, dnffn1 = _ffn_bwd(
        dh5, h4, hh1, row(norm_ffn[1]), cw_full[1], row(f_conv_b[1]), w_up, w_dn, 1)
    g_dn1 = _wgrad(act1, dh5, FF_BLK, D_MODEL, False, "wgrad_ffn_down1")
    g_up1 = _wgrad(xf1, dhh1, D_MODEL, FF_BLK, True, "wgrad_ffn_up1")
    dh3a, dq, xn1, dkv, dsink, dnmix1 = _attn_bwd(dh4, h3, q, kv, ao, lse, row(norm_mix[1]), sinks, w_q, w_o)
    g_wo = _wgrad(ao, dh4, D_MODEL, D_MODEL, False, "wgrad_attn_o")
    g_wq = _wgrad(xn1, dq, D_MODEL, D_MODEL, False, "wgrad_attn_q")
    dh2, dpe0, da0, xg0, dbg0, dnple0, kvn, dnkv = _ple_bwd(
        dh3a, h2, pe0, a0, row(norm_ple[0]), w_gate, 0, kv_args=(h3, dkv, row(norm_kv), w_kv_f))
    g_wkv = _wgrad(kvn, dkv, D_MODEL, 2 * KV_DIM, False, "wgrad_kv")
    g_pin0 = _wgrad(p0, dpe0, PLE_DIM, D_MODEL // N_SHARD, True, "wgrad_ple_in0")
    g_gate0 = _wgrad(xg0, da0, D_MODEL, D_MODEL, False, "wgrad_ple_gate0")
    dh1, act0, dhh0, xf0, dcw0, dcb0, dnffn0 = _ffn_bwd(
        dh2, h1, hh0, row(norm_ffn[0]), cw_full[0], row(f_conv_b[0]), w_up, w_dn, 0)
    g_dn0 = _wgrad(act0, dh2, FF_BLK, D_MODEL, False, "wgrad_ffn_down0")
    g_up0 = _wgrad(xf0, dhh0, D_MODEL, FF_BLK, True, "wgrad_ffn_up0")
    dx, gated, dzp, xn0, dws, dbs, dgv, dnmix0 = _mixer_a_bwd(
        dh1, xs, zp, row(norm_mix[0]), gv_full, wsm, bsb, tril, w_in, w_out)
    g_win = _wgrad(xn0, dzp, D_MODEL, 2 * D_MODEL // N_SHARD, True, "wgrad_a_in")
    g_wout = _wgrad(gated, dh1, D_MODEL, D_MODEL, False, "wgrad_a_out")

    small_grads = {
        'norm_mix': jnp.concatenate([dnmix0, dnmix1]), 'norm_ffn': jnp.concatenate([dnffn0, dnffn1]),
        'norm_ple': jnp.concatenate([dnple0, dnple1]), 'norm_kv': dnkv, 'norm_final': dn_final,
        'a_norm_v': dgv, 'a_w_s': dws, 'a_b_s': dbs[:, :, 0], 'b_sinks': dsink[0, :N_Q_HEADS],
        'f_conv_w': jnp.stack([dcw0, dcw1]), 'f_conv_b': jnp.concatenate([dcb0, dcb1]),
        'ple_b_gate': jnp.concatenate([dbg0, dbg1]),
    }
    return loss_acc, dx, (g_win, g_wout, g_wkv, g_wq, g_wo, g_up0, g_up1, g_dn0, g_dn1, g_pin0, g_pin1,
                          g_gate0, g_gate1, small_grads)


def _reduce_and_update(given, dx, loss_acc, g_win, g_wout, g_wkv, g_wq, g_wo, g_up0, g_up1, g_dn0, g_dn1,
                       g_pin0, g_pin1, g_gate0, g_gate1, small_grads):
    weight_names = ['norm_mix', 'norm_ffn', 'norm_ple', 'norm_kv', 'norm_final', 'a_w_in', 'a_norm_v', 'a_w_s',
                    'a_b_s', 'a_w_out', 'w_kv', 'b_w_q', 'b_sinks', 'b_w_o', 'f_w_up', 'f_conv_w', 'f_conv_b',
                    'f_w_down', 'ple_w_in', 'ple_w_gate', 'ple_b_gate']
    cx, cy, cc = _mesh_pos()
    shard = 2 * cx + cy
    c_arr = jnp.reshape(cc, (1,)).astype(jnp.int32)
    s_arr = jnp.reshape(shard, (1,)).astype(jnp.int32)
    x = given['x']

    def shard_rows(g):
        return g.reshape(N_SHARD, g.shape[0] // N_SHARD, g.shape[1])

    tensors = [
        ('a_w_in', 0, g_win), ('a_w_out', 0, shard_rows(g_wout)), ('w_kv', 0, shard_rows(g_wkv)),
        ('b_w_q', 0, shard_rows(g_wq)), ('b_w_o', 0, shard_rows(g_wo)),
        ('f_w_up', 0, g_up0), ('f_w_up', 1, g_up1),
        ('f_w_down', 0, shard_rows(g_dn0)), ('f_w_down', 1, shard_rows(g_dn1)),
        ('ple_w_in', 0, g_pin0), ('ple_w_in', 1, g_pin1),
        ('ple_w_gate', 0, shard_rows(g_gate0)), ('ple_w_gate', 1, shard_rows(g_gate1)),
    ]
    small_shapes = {
        'norm_mix': (2, D_MODEL), 'norm_ffn': (2, D_MODEL), 'norm_ple': (2, D_MODEL), 'norm_kv': (D_MODEL,),
        'norm_final': (D_MODEL,), 'a_norm_v': (1, D_MODEL), 'a_w_s': (1, A_GROUPS, CHUNK, CHUNK),
        'a_b_s': (1, A_GROUPS, CHUNK), 'b_sinks': (1, N_Q_HEADS), 'f_conv_w': (2, 3, N_FF),
        'f_conv_b': (2, N_FF), 'ple_b_gate': (2, D_MODEL),
    }
    small_names = list(small_shapes)
    small_g = _pack([small_grads[k] for k in small_names])

    grads4 = [t[2] for t in tensors]
    sib = _sibling_send_halves(grads4, small_g)
    parts = [_add_own_half(g, s, c_arr, f"chip_partial_{i}") for i, (g, s) in enumerate(zip(grads4, sib[:-1]))]
    small_chip = _adamw_free_add(small_g, sib[-1])
    recv = _chip_exchange(parts, small_chip)
    halves_red = [_sum_chips(pt, rc, s_arr, f"chip_sum_{i}") for i, (pt, rc) in enumerate(zip(parts, recv[:-1]))]
    small_red = _sum_chips(small_chip, recv[-1], s_arr, "chip_sum_small")
    group_names, groups = [], []
    for i, (name, layer, _) in enumerate(tensors):
        if layer == 0:
            group_names.append(name)
            groups.append([i])
        else:
            groups[group_names.index(name)].append(i)
    merged = dict(zip(group_names, _sibling_merge(halves_red, groups)))

    out_g, out_d, out_m, out_v = {}, {}, {}, {}
    for name in group_names:
        w = given[name]
        g = merged[name].reshape(w.shape)
        c2 = w.shape[-1]
        d, mn, vn = _adamw(w.reshape(-1, c2), g.reshape(-1, c2), given['m_' + name].reshape(-1, c2),
                           given['v_' + name].reshape(-1, c2), f"adamw_{name}")
        out_g[name], out_d[name], out_m[name], out_v[name] = g, d.reshape(w.shape), mn.reshape(w.shape), vn.reshape(w.shape)

    full_small = dict(zip(small_names, _unpack(small_red, [small_shapes[k] for k in small_names])))
    local_small = dict(full_small)
    local_small['a_norm_v'] = lax.dynamic_slice_in_dim(full_small['a_norm_v'], shard * 256, 256, axis=1)
    local_small['f_conv_w'] = lax.dynamic_slice_in_dim(full_small['f_conv_w'], shard * FF_BLK, FF_BLK, axis=2)
    sg = _pack([local_small[k] for k in small_names])
    sw = _pack([given[k] for k in small_names])
    sm = _pack([given['m_' + k] for k in small_names])
    sv = _pack([given['v_' + k] for k in small_names])
    sd, smn, svn = _adamw(sw, sg, sm, sv, "adamw_small")
    local_shapes = [given[k].shape for k in small_names]
    for dst, packed in ((out_d, sd), (out_m, smn), (out_v, svn)):
        dst.update(zip(small_names, _unpack(packed, local_shapes)))
    for k in small_names:
        out_g[k] = local_small[k].reshape(given[k].shape)

    loss = lax.psum(loss_acc[0, 0], ("x", "y", "c"))
    grad_x = dx.reshape(x.shape)
    return (loss, grad_x, *[out_g[k] for k in weight_names], *[out_d[k] for k in weight_names],
            *[out_m[k] for k in weight_names], *[out_v[k] for k in weight_names])


def _adamw_free_add(a, b):
    def body(a_ref, b_ref, o_ref):
        o_ref[...] = a_ref[...] + b_ref[...]

    return pl.pallas_call(body, name="chip_partial_small", out_shape=jax.ShapeDtypeStruct(a.shape, F32))(a, b)
```

```python
import functools
import math

import numpy as np
import jax
import jax.numpy as jnp
from jax import lax
from jax.experimental import pallas as pl
from jax.experimental.pallas import tpu as pltpu

F32 = jnp.float32
BF16 = jnp.bfloat16

D_MODEL = 1024
CHUNK = 128
A_GROUPS = 8
HEAD_DIM = 64
N_Q_HEADS = 16
N_KV_HEADS = 4
GQA_GROUP = N_Q_HEADS // N_KV_HEADS
KV_DIM = N_KV_HEADS * HEAD_DIM
BLOCK = 128
D_FF = 2816
N_FF = 2 * D_FF
FF_BLK = N_FF // 4
PLE_DIM = 256
EPS = 1e-6
NEG = -1e30
N_SHARD = 4

ADAM_LR = 0.001
ADAM_B1 = 0.9
ADAM_B2 = 0.999
ADAM_EPS = 1e-08
ADAM_WD = 0.01
ADAM_STEP = 10

VMEM_LIMIT = 60 * 1024 * 1024
MESH = pl.DeviceIdType.MESH
ANY = pl.BlockSpec(memory_space=pl.ANY)
SMEM = pl.BlockSpec(memory_space=pltpu.SMEM)

_SLOPES = [float(np.float32(2.0 ** (-8.0 * (h + 1) / N_Q_HEADS))) for h in range(N_Q_HEADS)]


def _dot(a, b):
    return jnp.dot(a, b, preferred_element_type=F32)


def _dot_nt(a, b):
    return lax.dot_general(a, b, (((1,), (1,)), ((), ())), preferred_element_type=F32)


def _dot_tn(a, b):
    return lax.dot_general(a, b, (((0,), (0,)), ((), ())), preferred_element_type=F32)


def _rms(x, g):
    r = lax.rsqrt(jnp.mean(x * x, axis=-1, keepdims=True) + EPS)
    xh = x * r
    return xh * g, xh, r


def _rms_bwd(dy, xh, r, g):
    dxh = dy * g
    dg = jnp.sum(dy * xh, axis=0, keepdims=True)
    dx = r * (dxh - xh * jnp.mean(dxh * xh, axis=-1, keepdims=True))
    return dx, dg


_GELU_C = math.sqrt(2.0 / math.pi)


def _gelu(x):
    t = jnp.tanh(_GELU_C * (x + 0.044715 * (x * x * x)))
    return 0.5 * x * (1.0 + t)


def _gelu_grad(x):
    x2 = x * x
    t = jnp.tanh(_GELU_C * (x + 0.044715 * (x2 * x)))
    return 0.5 * (1.0 + t) + 0.5 * x * (1.0 - t * t) * (_GELU_C * (1.0 + 3.0 * 0.044715 * x2))


def _sigmoid(x):
    return 1.0 / (1.0 + jnp.exp(-x))


def _load_once(pairs, sem):
    @pl.when(pl.program_id(0) == 0)
    def _():
        cps = [pltpu.make_async_copy(s, d, sem.at[i]) for i, (s, d) in enumerate(pairs)]
        for cp in cps:
            cp.start()
        for cp in cps:
            cp.wait()


def _params(n_axes=1, vmem=VMEM_LIMIT):
    return pltpu.CompilerParams(dimension_semantics=("arbitrary",) * n_axes, vmem_limit_bytes=vmem)


def _row_spec(tm, n, rev_nt=None):
    if rev_nt is None:
        return pl.BlockSpec((tm, n), lambda i: (i, 0))
    return pl.BlockSpec((tm, n), lambda i: (rev_nt - 1 - i, 0))


def _const_spec(shape):
    nd = len(shape)
    return pl.BlockSpec(shape, lambda i: (0,) * nd)


def _zero_first(refs):
    @pl.when(pl.program_id(0) == 0)
    def _():
        for r in refs:
            r[...] = jnp.zeros(r.shape, r.dtype)


def _mixer_a_fwd(x, nmix, gv, wsm, bsb, w_in, w_out):
    T = x.shape[0]
    tm = min(512, T)
    nt = T // tm
    nw = 2 * D_MODEL // N_SHARD

    def body(x_ref, nmix_ref, gv_ref, ws_ref, bsb_ref, w_in_hbm, w_out_hbm,
             h1_ref, zp_ref, w_in_v, w_out_v, gated_v, sem):
        _load_once([(w_in_hbm, w_in_v), (w_out_hbm, w_out_v)], sem)
        xv = x_ref[...]
        xn = _rms(xv, nmix_ref[...])[0].astype(BF16)
        for j in range(N_SHARD):
            zp_ref[:, j * nw:(j + 1) * nw] = _dot(xn, w_in_v[j])
        z = _gelu(zp_ref[...])
        u = z[:, :D_MODEL]
        vn = _rms(z[:, D_MODEL:], gv_ref[...])[0].astype(BF16)
        for c in range(tm // CHUNK):
            rows = slice(c * CHUNK, (c + 1) * CHUNK)
            for h in range(A_GROUPS):
                cols = slice(h * CHUNK, (h + 1) * CHUNK)
                s = _dot(ws_ref[h], vn[rows, cols]) + bsb_ref[h]
                gated_v[rows, cols] = (u[rows, cols] * s).astype(BF16)
        h1_ref[...] = xv + _dot(gated_v[...], w_out_v[...])

    return pl.pallas_call(
        body, name="mixer_a_fwd", grid=(nt,),
        in_specs=[_row_spec(tm, D_MODEL), _const_spec((1, D_MODEL)), _const_spec((1, D_MODEL)),
                  _const_spec((A_GROUPS, CHUNK, CHUNK)), _const_spec((A_GROUPS, CHUNK, CHUNK)), ANY, ANY],
        out_specs=[_row_spec(tm, D_MODEL), _row_spec(tm, 2 * D_MODEL)],
        out_shape=[jax.ShapeDtypeStruct((T, D_MODEL), F32), jax.ShapeDtypeStruct((T, 2 * D_MODEL), F32)],
        scratch_shapes=[pltpu.VMEM((N_SHARD, D_MODEL, nw), BF16), pltpu.VMEM((D_MODEL, D_MODEL), BF16),
                        pltpu.VMEM((tm, D_MODEL), BF16), pltpu.SemaphoreType.DMA((2,))],
        compiler_params=_params(),
    )(x, nmix, gv, wsm, bsb, w_in, w_out)


def _mixer_a_bwd(dh, x, zp, nmix, gv, wsm, bsb, tril, w_in, w_out):
    T = x.shape[0]
    tm = min(256, T)
    nt = T // tm
    nw = 2 * D_MODEL // N_SHARD

    def body(dh_ref, x_ref, zp_ref, nmix_ref, gv_ref, ws_ref, bsb_ref, tril_ref, w_in_hbm, w_out_hbm,
             dx_ref, gated_ref, dzp_ref, xn_ref, dws_ref, dbs_ref, dgv_ref, dnmix_ref,
             w_in_v, w_out_v, du_v, dvn_v, dbs_v, sem):
        _load_once([(w_in_hbm, w_in_v), (w_out_hbm, w_out_v)], sem)
        _zero_first([dws_ref, dbs_v, dgv_ref, dnmix_ref])
        i = pl.program_id(0)
        dhv = dh_ref[...]
        xv = x_ref[...]
        xn, xh, r = _rms(xv, nmix_ref[...])
        xn_ref[...] = xn.astype(BF16)
        zpv = zp_ref[...]
        z = _gelu(zpv)
        u = z[:, :D_MODEL]
        vn_f, vh, rv = _rms(z[:, D_MODEL:], gv_ref[...])
        vn = vn_f.astype(BF16)
        dgated = _dot_nt(dhv.astype(BF16), w_out_v[...])
        for c in range(tm // CHUNK):
            rows = slice(c * CHUNK, (c + 1) * CHUNK)
            for h in range(A_GROUPS):
                cols = slice(h * CHUNK, (h + 1) * CHUNK)
                vn_h = vn[rows, cols]
                s = _dot(ws_ref[h], vn_h) + bsb_ref[h]
                dgt = dgated[rows, cols]
                u_h = u[rows, cols]
                gated_ref[rows, cols] = (u_h * s).astype(BF16)
                du_v[rows, cols] = dgt * s
                ds = dgt * u_h
                dsb = ds.astype(BF16)
                dws_ref[h] += _dot_nt(dsb, vn_h)
                dbs_v[h] += ds
                dvn_v[rows, cols] = _dot_tn(ws_ref[h], dsb)
        dv, dgv = _rms_bwd(dvn_v[...], vh, rv, gv_ref[...])
        dgv_ref[...] += dgv
        dzu = (du_v[...] * _gelu_grad(zpv[:, :D_MODEL])).astype(BF16)
        dzv = (dv * _gelu_grad(zpv[:, D_MODEL:])).astype(BF16)
        dzp_ref[:, :D_MODEL] = dzu
        dzp_ref[:, D_MODEL:] = dzv
        dxn = _dot_nt(dzu[:, :nw], w_in_v[0]) + _dot_nt(dzu[:, nw:], w_in_v[1])
        dxn += _dot_nt(dzv[:, :nw], w_in_v[2]) + _dot_nt(dzv[:, nw:], w_in_v[3])
        dxx, dn = _rms_bwd(dxn, xh, r, nmix_ref[...])
        dnmix_ref[...] += dn
        dx_ref[...] = dhv + dxx

        @pl.when(i == nt - 1)
        def _():
            for h in range(A_GROUPS):
                dws_ref[h] = dws_ref[h] * tril_ref[...]
                dbs_ref[h] = jnp.broadcast_to(jnp.sum(dbs_v[h], axis=1, keepdims=True), (CHUNK, CHUNK))

    grp = (A_GROUPS, CHUNK, CHUNK)
    return pl.pallas_call(
        body, name="mixer_a_bwd", grid=(nt,),
        in_specs=[_row_spec(tm, D_MODEL), _row_spec(tm, D_MODEL), _row_spec(tm, 2 * D_MODEL),
                  _const_spec((1, D_MODEL)), _const_spec((1, D_MODEL)), _const_spec(grp), _const_spec(grp),
                  _const_spec((CHUNK, CHUNK)), ANY, ANY],
        out_specs=[_row_spec(tm, D_MODEL), _row_spec(tm, D_MODEL), _row_spec(tm, 2 * D_MODEL),
                   _row_spec(tm, D_MODEL), _const_spec(grp), _const_spec(grp),
                   _const_spec((1, D_MODEL)), _const_spec((1, D_MODEL))],
        out_shape=[jax.ShapeDtypeStruct((T, D_MODEL), F32), jax.ShapeDtypeStruct((T, D_MODEL), BF16),
                   jax.ShapeDtypeStruct((T, 2 * D_MODEL), BF16), jax.ShapeDtypeStruct((T, D_MODEL), BF16),
                   jax.ShapeDtypeStruct(grp, F32), jax.ShapeDtypeStruct(grp, F32),
                   jax.ShapeDtypeStruct((1, D_MODEL), F32), jax.ShapeDtypeStruct((1, D_MODEL), F32)],
        scratch_shapes=[pltpu.VMEM((N_SHARD, D_MODEL, nw), BF16), pltpu.VMEM((D_MODEL, D_MODEL), BF16),
                        pltpu.VMEM((tm, D_MODEL), F32), pltpu.VMEM((tm, D_MODEL), F32),
                        pltpu.VMEM(grp, F32), pltpu.SemaphoreType.DMA((2,))],
        compiler_params=_params(),
    )(dh, x, zp, nmix, gv, wsm, bsb, tril, w_in, w_out)


def _load_ffn_weights(w_up_hbm, w_dn_hbm, layer, w_up_v, w_dn_v, sem):
    rows = D_FF // N_SHARD
    pairs = [(w_up_hbm.at[j, layer], w_up_v.at[j]) for j in range(N_SHARD)]
    pairs += [(w_dn_hbm.at[j, layer], w_dn_v.at[pl.ds(j * rows, rows)]) for j in range(N_SHARD)]
    _load_once(pairs, sem)


def _ffn_fwd(h, nffn, cw, cb, w_up, w_dn, layer):
    T = h.shape[0]
    tm = min(256, T)
    nt = T // tm

    def body(h_ref, n_ref, cw_ref, cb_ref, w_up_hbm, w_dn_hbm, out_ref, hh_ref,
             w_up_v, w_dn_v, carry_v, sem):
        _load_ffn_weights(w_up_hbm, w_dn_hbm, layer, w_up_v, w_dn_v, sem)
        _zero_first([carry_v])
        xv = h_ref[...]
        xf = _rms(xv, n_ref[...])[0].astype(BF16)
        acc = xv
        for j in range(2):
            cs = []
            for blk in (j, j + 2):
                cols = slice(blk * FF_BLK, (blk + 1) * FF_BLK)
                hh = _dot(xf, w_up_v[blk])
                hh_ref[:, cols] = hh.astype(BF16)
                ext = jnp.concatenate([carry_v[blk], hh], axis=0)
                carry_v[blk] = hh[tm - 8:, :]
                s1 = pltpu.roll(ext, 1, 0)[8:]
                s2 = pltpu.roll(ext, 2, 0)[8:]
                cs.append(cb_ref[:, cols] + cw_ref[0:1, cols] * s2 + cw_ref[1:2, cols] * s1
                          + cw_ref[2:3, cols] * hh)
            act = (cs[0] * _sigmoid(cs[0]) * cs[1]).astype(BF16)
            acc = acc + _dot(act, w_dn_v[j * FF_BLK:(j + 1) * FF_BLK, :])
        out_ref[...] = acc

    return pl.pallas_call(
        body, name=f"ffn_fwd{layer}", grid=(nt,),
        in_specs=[_row_spec(tm, D_MODEL), _const_spec((1, D_MODEL)), _const_spec((3, N_FF)),
                  _const_spec((1, N_FF)), ANY, ANY],
        out_specs=[_row_spec(tm, D_MODEL), _row_spec(tm, N_FF)],
        out_shape=[jax.ShapeDtypeStruct((T, D_MODEL), F32), jax.ShapeDtypeStruct((T, N_FF), BF16)],
        scratch_shapes=[pltpu.VMEM((N_SHARD, D_MODEL, FF_BLK), BF16), pltpu.VMEM((D_FF, D_MODEL), BF16),
                        pltpu.VMEM((N_SHARD, 8, FF_BLK), F32), pltpu.SemaphoreType.DMA((2 * N_SHARD,))],
        compiler_params=_params(),
    )(h, nffn, cw, cb, w_up, w_dn)


def _ffn_bwd(dh, h, hh, nffn, cw, cb, w_up, w_dn, layer):
    T = h.shape[0]
    tm = min(256, T)
    nt = T // tm
    pv = 16

    def body(dh_ref, h_ref, hh_ref, hhp_ref, n_ref, cw_ref, cb_ref, w_up_hbm, w_dn_hbm,
             dhin_ref, act_ref, dhh_ref, xf_ref, dcw_ref, dcb_ref, dn_ref,
             w_up_v, w_dn_v, carry_v, sem):
        _load_ffn_weights(w_up_hbm, w_dn_hbm, layer, w_up_v, w_dn_v, sem)
        _zero_first([carry_v, dcw_ref, dcb_ref, dn_ref])
        ti = nt - 1 - pl.program_id(0)
        keep_prev = jnp.where(ti > 0, 1.0, 0.0).astype(F32)
        dout = dh_ref[...]
        doutb = dout.astype(BF16)
        xf_f, xh, r = _rms(h_ref[...], n_ref[...])
        xf_ref[...] = xf_f.astype(BF16)
        dxf = jnp.zeros((tm, D_MODEL), F32)
        for j in range(2):
            fw = []
            for blk in (j, j + 2):
                cols = slice(blk * FF_BLK, (blk + 1) * FF_BLK)
                hhv = hh_ref[:, cols].astype(F32)
                prev = hhp_ref[:, cols].astype(F32) * keep_prev
                ext = jnp.concatenate([prev, hhv], axis=0)
                s1 = pltpu.roll(ext, 1, 0)[pv:]
                s2 = pltpu.roll(ext, 2, 0)[pv:]
                cv = (cb_ref[:, cols] + cw_ref[0:1, cols] * s2 + cw_ref[1:2, cols] * s1
                      + cw_ref[2:3, cols] * hhv)
                fw.append((blk, cols, hhv, s1, s2, cv))
            cg, cu = fw[0][5], fw[1][5]
            sg = _sigmoid(cg)
            sil = cg * sg
            act_ref[:, j * FF_BLK:(j + 1) * FF_BLK] = (sil * cu).astype(BF16)
            dact = _dot_nt(doutb, w_dn_v[j * FF_BLK:(j + 1) * FF_BLK, :])
            dcs = (dact * cu * (sg * (1.0 + cg * (1.0 - sg))), dact * sil)
            for (blk, cols, hhv, s1, s2, _), dc in zip(fw, dcs):
                dcb_ref[:, cols] += jnp.sum(dc, axis=0, keepdims=True)
                dcw_ref[0:1, cols] += jnp.sum(dc * s2, axis=0, keepdims=True)
                dcw_ref[1:2, cols] += jnp.sum(dc * s1, axis=0, keepdims=True)
                dcw_ref[2:3, cols] += jnp.sum(dc * hhv, axis=0, keepdims=True)
                ext = jnp.concatenate([dc, carry_v[blk]], axis=0)
                carry_v[blk] = dc[:8, :]
                n = tm + 8
                a1 = pltpu.roll(ext, n - 1, 0)[:tm]
                a2 = pltpu.roll(ext, n - 2, 0)[:tm]
                dhh = (cw_ref[2:3, cols] * dc + cw_ref[1:2, cols] * a1 + cw_ref[0:1, cols] * a2).astype(BF16)
                dhh_ref[:, cols] = dhh
                dxf = dxf + _dot_nt(dhh, w_up_v[blk])
        dxx, dn = _rms_bwd(dxf, xh, r, n_ref[...])
        dn_ref[...] += dn
        dhin_ref[...] = dout + dxx

    rev = functools.partial(_row_spec, rev_nt=nt)
    prev_spec = pl.BlockSpec((pv, N_FF), lambda i: (jnp.maximum((nt - 1 - i) * (tm // pv) - 1, 0), 0))
    return pl.pallas_call(
        body, name=f"ffn_bwd{layer}", grid=(nt,),
        in_specs=[rev(tm, D_MODEL), rev(tm, D_MODEL), rev(tm, N_FF), prev_spec,
                  _const_spec((1, D_MODEL)), _const_spec((3, N_FF)), _const_spec((1, N_FF)), ANY, ANY],
        out_specs=[rev(tm, D_MODEL), rev(tm, D_FF), rev(tm, N_FF), rev(tm, D_MODEL),
                   _const_spec((3, N_FF)), _const_spec((1, N_FF)), _const_spec((1, D_MODEL))],
        out_shape=[jax.ShapeDtypeStruct((T, D_MODEL), F32), jax.ShapeDtypeStruct((T, D_FF), BF16),
                   jax.ShapeDtypeStruct((T, N_FF), BF16), jax.ShapeDtypeStruct((T, D_MODEL), BF16),
                   jax.ShapeDtypeStruct((3, N_FF), F32), jax.ShapeDtypeStruct((1, N_FF), F32),
                   jax.ShapeDtypeStruct((1, D_MODEL), F32)],
        scratch_shapes=[pltpu.VMEM((N_SHARD, D_MODEL, FF_BLK), BF16), pltpu.VMEM((D_FF, D_MODEL), BF16),
                        pltpu.VMEM((N_SHARD, 8, FF_BLK), F32), pltpu.SemaphoreType.DMA((2 * N_SHARD,))],
        compiler_params=_params(),
    )(dh, h, hh, hh, nffn, cw, cb, w_up, w_dn)


def _load_ple_weights(w_pin_hbm, w_gate_hbm, layer, w_pin_v, w_gate_v, sem, extra=()):
    rows = D_MODEL // N_SHARD
    pairs = [(w_pin_hbm.at[j, layer], w_pin_v.at[j]) for j in range(N_SHARD)]
    pairs += [(w_gate_hbm.at[j, layer], w_gate_v.at[pl.ds(j * rows, rows)]) for j in range(N_SHARD)]
    _load_once(pairs + list(extra), sem)


def _ple_fwd_kv(h, p, nple, bg, nkv, w_pin, w_gate, w_kv):
    T = h.shape[0]
    tm = min(512, T)
    nt = T // tm
    pw = D_MODEL // N_SHARD

    def body(h_ref, p_ref, n_ref, bg_ref, nkv_ref, w_pin_hbm, w_gate_hbm, w_kv_hbm,
             out_ref, pe_ref, a_ref, kv_ref, w_pin_v, w_gate_v, w_kv_v, sem):
        _load_ple_weights(w_pin_hbm, w_gate_hbm, 0, w_pin_v, w_gate_v, sem, [(w_kv_hbm, w_kv_v)])
        xv = h_ref[...]
        xg = _rms(xv, n_ref[...])[0].astype(BF16)
        a = _dot(xg, w_gate_v[...]) + bg_ref[...]
        a_ref[...] = a
        pb = p_ref[...].astype(BF16)
        for j in range(N_SHARD):
            pe_ref[:, j * pw:(j + 1) * pw] = _dot(pb, w_pin_v[j])
        hn = xv + pe_ref[...] * _sigmoid(a)
        out_ref[...] = hn
        kvn = _rms(hn, nkv_ref[...])[0].astype(BF16)
        kv_ref[...] = _dot(kvn, w_kv_v[...]).astype(BF16)

    vec = _const_spec((1, D_MODEL))
    return pl.pallas_call(
        body, name="ple_fwd0", grid=(nt,),
        in_specs=[_row_spec(tm, D_MODEL), _row_spec(tm, PLE_DIM), vec, vec, vec, ANY, ANY, ANY],
        out_specs=[_row_spec(tm, D_MODEL), _row_spec(tm, D_MODEL), _row_spec(tm, D_MODEL),
                   _row_spec(tm, 2 * KV_DIM)],
        out_shape=[jax.ShapeDtypeStruct((T, D_MODEL), F32), jax.ShapeDtypeStruct((T, D_MODEL), F32),
                   jax.ShapeDtypeStruct((T, D_MODEL), F32), jax.ShapeDtypeStruct((T, 2 * KV_DIM), BF16)],
        scratch_shapes=[pltpu.VMEM((N_SHARD, PLE_DIM, pw), BF16), pltpu.VMEM((D_MODEL, D_MODEL), BF16),
                        pltpu.VMEM((D_MODEL, 2 * KV_DIM), BF16), pltpu.SemaphoreType.DMA((2 * N_SHARD + 1,))],
        compiler_params=_params(),
    )(h, p, nple, bg, nkv, w_pin, w_gate, w_kv)


def _ple_fwd_final(h, p, tgt, nple, bg, nfin, w_pin, w_gate):
    T = h.shape[0]
    tm = min(512, T)
    nt = T // tm
    pw = D_MODEL // N_SHARD

    def body(h_ref, p_ref, t_ref, n_ref, bg_ref, nf_ref, w_pin_hbm, w_gate_hbm,
             dh_ref, pe_ref, a_ref, loss_ref, dnf_ref, w_pin_v, w_gate_v, sem):
        _load_ple_weights(w_pin_hbm, w_gate_hbm, 1, w_pin_v, w_gate_v, sem)
        _zero_first([loss_ref, dnf_ref])
        xv = h_ref[...]
        xg = _rms(xv, n_ref[...])[0].astype(BF16)
        a = _dot(xg, w_gate_v[...]) + bg_ref[...]
        a_ref[...] = a
        pb = p_ref[...].astype(BF16)
        for j in range(N_SHARD):
            pe_ref[:, j * pw:(j + 1) * pw] = _dot(pb, w_pin_v[j])
        hn = xv + pe_ref[...] * _sigmoid(a)
        y, yh, r = _rms(hn, nf_ref[...])
        diff = y - t_ref[...]
        loss_ref[...] += 0.5 * jnp.sum(jnp.mean(diff * diff, axis=-1, keepdims=True))
        dy = diff * (1.0 / D_MODEL)
        dhn, dnf = _rms_bwd(dy, yh, r, nf_ref[...])
        dnf_ref[...] += dnf
        dh_ref[...] = dhn

    vec = _const_spec((1, D_MODEL))
    return pl.pallas_call(
        body, name="ple_fwd1", grid=(nt,),
        in_specs=[_row_spec(tm, D_MODEL), _row_spec(tm, PLE_DIM), _row_spec(tm, D_MODEL), vec, vec, vec, ANY, ANY],
        out_specs=[_row_spec(tm, D_MODEL), _row_spec(tm, D_MODEL), _row_spec(tm, D_MODEL),
                   _const_spec((8, 128)), vec],
        out_shape=[jax.ShapeDtypeStruct((T, D_MODEL), F32), jax.ShapeDtypeStruct((T, D_MODEL), F32),
                   jax.ShapeDtypeStruct((T, D_MODEL), F32), jax.ShapeDtypeStruct((8, 128), F32),
                   jax.ShapeDtypeStruct((1, D_MODEL), F32)],
        scratch_shapes=[pltpu.VMEM((N_SHARD, PLE_DIM, pw), BF16), pltpu.VMEM((D_MODEL, D_MODEL), BF16),
                        pltpu.SemaphoreType.DMA((2 * N_SHARD,))],
        compiler_params=_params(),
    )(h, p, tgt, nple, bg, nfin, w_pin, w_gate)


def _ple_bwd(dh, hb, pe, a, nple, w_gate, layer, kv_args=None):
    T = hb.shape[0]
    tm = min(512, T)
    nt = T // tm
    with_kv = kv_args is not None
    rows = D_MODEL // N_SHARD

    def body(*refs):
        if with_kv:
            (dh_ref, hb_ref, pe_ref, a_ref, n_ref, w_gate_hbm, hc_ref, dkv_ref, nkv_ref, w_kv_hbm,
             dhb_ref, dpe_ref, da_ref, xg_ref, dbg_ref, dn_ref, kvn_ref, dnkv_ref,
             w_gate_v, w_kv_v, sem) = refs
        else:
            (dh_ref, hb_ref, pe_ref, a_ref, n_ref, w_gate_hbm,
             dhb_ref, dpe_ref, da_ref, xg_ref, dbg_ref, dn_ref, w_gate_v, sem) = refs
        pairs = [(w_gate_hbm.at[j, layer], w_gate_v.at[pl.ds(j * rows, rows)]) for j in range(N_SHARD)]
        if with_kv:
            pairs.append((w_kv_hbm, w_kv_v))
        _load_once(pairs, sem)
        _zero_first([dbg_ref, dn_ref] + ([dnkv_ref] if with_kv else []))
        do = dh_ref[...]
        if with_kv:
            dkvn = _dot_nt(dkv_ref[...].astype(BF16), w_kv_v[...])
            kvn, kh, kr = _rms(hc_ref[...], nkv_ref[...])
            kvn_ref[...] = kvn.astype(BF16)
            dk, dnkv = _rms_bwd(dkvn, kh, kr, nkv_ref[...])
            dnkv_ref[...] += dnkv
            do = do + dk
        gate = _sigmoid(a_ref[...])
        dpe_ref[...] = (do * gate).astype(BF16)
        da = do * pe_ref[...] * (gate * (1.0 - gate))
        dab = da.astype(BF16)
        da_ref[...] = dab
        dbg_ref[...] += jnp.sum(da, axis=0, keepdims=True)
        dxg = _dot_nt(dab, w_gate_v[...])
        xg, xh, r = _rms(hb_ref[...], n_ref[...])
        xg_ref[...] = xg.astype(BF16)
        dxx, dn = _rms_bwd(dxg, xh, r, n_ref[...])
        dn_ref[...] += dn
        dhb_ref[...] = do + dxx

    vec = _const_spec((1, D_MODEL))
    row = _row_spec(tm, D_MODEL)
    in_specs = [row, row, row, row, vec, ANY]
    args = [dh, hb, pe, a, nple, w_gate]
    out_specs = [row, row, row, row, vec, vec]
    out_shape = [jax.ShapeDtypeStruct((T, D_MODEL), F32), jax.ShapeDtypeStruct((T, D_MODEL), BF16),
                 jax.ShapeDtypeStruct((T, D_MODEL), BF16), jax.ShapeDtypeStruct((T, D_MODEL), BF16),
                 jax.ShapeDtypeStruct((1, D_MODEL), F32), jax.ShapeDtypeStruct((1, D_MODEL), F32)]
    scratch = [pltpu.VMEM((D_MODEL, D_MODEL), BF16)]
    if with_kv:
        hc, dkv, nkv, w_kv = kv_args
        in_specs += [row, _row_spec(tm, 2 * KV_DIM), vec, ANY]
        args += [hc, dkv, nkv, w_kv]
        out_specs += [row, vec]
        out_shape += [jax.ShapeDtypeStruct((T, D_MODEL), BF16), jax.ShapeDtypeStruct((1, D_MODEL), F32)]
        scratch.append(pltpu.VMEM((D_MODEL, 2 * KV_DIM), BF16))
    scratch.append(pltpu.SemaphoreType.DMA((N_SHARD + 1,)))
    return pl.pallas_call(
        body, name=f"ple_bwd{layer}", grid=(nt,), in_specs=in_specs, out_specs=out_specs,
        out_shape=out_shape, scratch_shapes=scratch, compiler_params=_params(),
    )(*args)


def _band_masks(is_first):
    ii = lax.broadcasted_iota(jnp.int32, (BLOCK, 2 * BLOCK), 0)
    jj = lax.broadcasted_iota(jnp.int32, (BLOCK, 2 * BLOCK), 1)
    dist = ii + BLOCK - jj
    valid = (dist >= 0) & (dist < BLOCK) & ((jj >= BLOCK) | jnp.logical_not(is_first))
    return dist.astype(F32), valid


def _attn_fwd(h, nmix, kv, sinks, w_q, w_o):
    T = h.shape[0]
    tm = min(512, T)
    nt = T // tm
    nb = tm // BLOCK

    def body(h_ref, n_ref, kv_ref, kvp_ref, sink_ref, w_q_hbm, w_o_hbm,
             out_ref, q_ref, ao_ref, lse_ref, w_q_v, w_o_v, kvs_v, sem):
        _load_once([(w_q_hbm, w_q_v), (w_o_hbm, w_o_v)], sem)
        ti = pl.program_id(0)
        xv = h_ref[...]
        xn = _rms(xv, n_ref[...])[0].astype(BF16)
        q_ref[...] = (_dot(xn, w_q_v[...]) * (HEAD_DIM ** -0.5)).astype(BF16)
        kvs_v[0:BLOCK, :] = kvp_ref[...]
        kvs_v[BLOCK:, :] = kv_ref[...]
        lane = lax.broadcasted_iota(jnp.int32, (BLOCK, 128), 1)

        def blk_body(b, carry):
            r0 = pl.multiple_of(b * BLOCK, BLOCK)
            distf, valid = _band_masks(jnp.logical_and(ti == 0, b == 0))
            qb = q_ref[pl.ds(r0, BLOCK), :]
            band = kvs_v[pl.ds(r0, 2 * BLOCK), :]
            lse_mat = jnp.zeros((BLOCK, 128), F32)
            outs = []
            for hq in range(N_Q_HEADS):
                kh = hq // GQA_GROUP
                k_h = band[:, kh * HEAD_DIM:(kh + 1) * HEAD_DIM]
                v_h = band[:, KV_DIM + kh * HEAD_DIM:KV_DIM + (kh + 1) * HEAD_DIM]
                s = _dot_nt(qb[:, hq * HEAD_DIM:(hq + 1) * HEAD_DIM], k_h) - _SLOPES[hq] * distf
                s = jnp.where(valid, s, NEG)
                sink = sink_ref[hq]
                m = jnp.maximum(jnp.max(s, axis=1, keepdims=True), sink)
                e = jnp.exp(s - m)
                den = jnp.sum(e, axis=1, keepdims=True) + jnp.exp(sink - m)
                outs.append(_dot((e / den).astype(BF16), v_h))
                lse_mat = jnp.where(lane == hq, m + jnp.log(den), lse_mat)
            ao_ref[pl.ds(r0, BLOCK), :] = jnp.concatenate(outs, axis=1).astype(BF16)
            lse_ref[pl.ds(r0, BLOCK), :] = lse_mat
            return carry

        lax.fori_loop(0, nb, blk_body, 0)
        out_ref[...] = xv + _dot(ao_ref[...], w_o_v[...])

    row = _row_spec(tm, D_MODEL)
    prev_spec = pl.BlockSpec((BLOCK, 2 * KV_DIM), lambda i: (jnp.maximum(i * nb - 1, 0), 0))
    return pl.pallas_call(
        body, name="attn_fwd", grid=(nt,),
        in_specs=[row, _const_spec((1, D_MODEL)), _row_spec(tm, 2 * KV_DIM), prev_spec, SMEM, ANY, ANY],
        out_specs=[row, row, row, _row_spec(tm, 128)],
        out_shape=[jax.ShapeDtypeStruct((T, D_MODEL), F32), jax.ShapeDtypeStruct((T, D_MODEL), BF16),
                   jax.ShapeDtypeStruct((T, D_MODEL), BF16), jax.ShapeDtypeStruct((T, 128), F32)],
        scratch_shapes=[pltpu.VMEM((D_MODEL, D_MODEL), BF16), pltpu.VMEM((D_MODEL, D_MODEL), BF16),
                        pltpu.VMEM((tm + BLOCK, 2 * KV_DIM), BF16), pltpu.SemaphoreType.DMA((2,))],
        compiler_params=_params(),
    )(h, nmix, kv, kv, sinks, w_q, w_o)


def _attn_bwd(dh, h, q, kv, ao, lse, nmix, sinks, w_q, w_o):
    T = h.shape[0]
    tm = min(512, T)
    nt = T // tm
    nb = tm // BLOCK

    def body(dh_ref, h_ref, q_ref, kv_ref, kvp_ref, ao_ref, lse_ref, n_ref, sink_ref, w_q_hbm, w_o_hbm,
             dhin_ref, dq_ref, xn_ref, dkv_ref, dsink_ref, dn_ref,
             w_q_v, w_o_v, kvs_v, dao_v, dq_v, dkv_v, carry_v, sem):
        _load_once([(w_q_hbm, w_q_v), (w_o_hbm, w_o_v)], sem)
        _zero_first([carry_v, dsink_ref, dn_ref])
        ti = nt - 1 - pl.program_id(0)
        dout = dh_ref[...]
        dao_v[...] = _dot_nt(dout.astype(BF16), w_o_v[...])
        kvs_v[0:BLOCK, :] = kvp_ref[...]
        kvs_v[BLOCK:, :] = kv_ref[...]
        dkv_v[0:tm, :] = jnp.zeros((tm, 2 * KV_DIM), F32)
        dkv_v[tm:, :] = carry_v[...]
        lane = lax.broadcasted_iota(jnp.int32, (BLOCK, 128), 1)
        lane8 = lax.broadcasted_iota(jnp.int32, (8, 128), 1)

        def blk_body(b, dsk):
            r0 = pl.multiple_of(b * BLOCK, BLOCK)
            distf, valid = _band_masks(jnp.logical_and(ti == 0, b == 0))
            qb = q_ref[pl.ds(r0, BLOCK), :]
            band = kvs_v[pl.ds(r0, 2 * BLOCK), :]
            aob = ao_ref[pl.ds(r0, BLOCK), :].astype(F32)
            daob = dao_v[pl.ds(r0, BLOCK), :]
            lse_mat = lse_ref[pl.ds(r0, BLOCK), :]
            dqs = []
            dks = []
            dvs = []
            for kh in range(N_KV_HEADS):
                k_h = band[:, kh * HEAD_DIM:(kh + 1) * HEAD_DIM]
                v_h = band[:, KV_DIM + kh * HEAD_DIM:KV_DIM + (kh + 1) * HEAD_DIM]
                dk = jnp.zeros((2 * BLOCK, HEAD_DIM), F32)
                dv = jnp.zeros((2 * BLOCK, HEAD_DIM), F32)
                for g in range(GQA_GROUP):
                    hq = kh * GQA_GROUP + g
                    hc = slice(hq * HEAD_DIM, (hq + 1) * HEAD_DIM)
                    q_h = qb[:, hc]
                    s = _dot_nt(q_h, k_h) - _SLOPES[hq] * distf
                    s = jnp.where(valid, s, NEG)
                    lse = jnp.sum(jnp.where(lane == hq, lse_mat, 0.0), axis=1, keepdims=True)
                    pr = jnp.exp(s - lse)
                    dao_h = daob[:, hc]
                    dd = jnp.sum(dao_h * aob[:, hc], axis=1, keepdims=True)
                    dao_hb = dao_h.astype(BF16)
                    dp = _dot_nt(dao_hb, v_h)
                    dsb = (pr * (dp - dd)).astype(BF16)
                    dqs.append(_dot(dsb, k_h) * (HEAD_DIM ** -0.5))
                    dk = dk + _dot_tn(dsb, q_h)
                    dv = dv + _dot_tn(pr.astype(BF16), dao_hb)
                    dsv = -jnp.sum(jnp.exp(sink_ref[hq] - lse) * dd)
                    dsk = dsk + jnp.where(lane8 == hq, dsv, 0.0)
                dks.append(dk)
                dvs.append(dv)
            dq_v[pl.ds(r0, BLOCK), :] = jnp.concatenate(dqs, axis=1)
            dkv_v[pl.ds(r0, 2 * BLOCK), :] += jnp.concatenate(dks + dvs, axis=1)
            return dsk

        dsk = lax.fori_loop(0, nb, blk_body, jnp.zeros((8, 128), F32))
        dsink_ref[...] += dsk
        dqb = dq_v[...].astype(BF16)
        dq_ref[...] = dqb
        dxn = _dot_nt(dqb, w_q_v[...])
        xn, xh, r = _rms(h_ref[...], n_ref[...])
        xn_ref[...] = xn.astype(BF16)
        dxx, dn = _rms_bwd(dxn, xh, r, n_ref[...])
        dn_ref[...] += dn
        dhin_ref[...] = dout + dxx
        dkv_ref[...] = dkv_v[BLOCK:, :]
        carry_v[...] = dkv_v[0:BLOCK, :]

    rev = functools.partial(_row_spec, rev_nt=nt)
    row = rev(tm, D_MODEL)
    prev_spec = pl.BlockSpec((BLOCK, 2 * KV_DIM), lambda i: (jnp.maximum((nt - 1 - i) * nb - 1, 0), 0))
    return pl.pallas_call(
        body, name="attn_bwd", grid=(nt,),
        in_specs=[row, row, row, rev(tm, 2 * KV_DIM), prev_spec, row, rev(tm, 128),
                  _const_spec((1, D_MODEL)), SMEM, ANY, ANY],
        out_specs=[row, row, row, rev(tm, 2 * KV_DIM), _const_spec((8, 128)), _const_spec((1, D_MODEL))],
        out_shape=[jax.ShapeDtypeStruct((T, D_MODEL), F32), jax.ShapeDtypeStruct((T, D_MODEL), BF16),
                   jax.ShapeDtypeStruct((T, D_MODEL), BF16), jax.ShapeDtypeStruct((T, 2 * KV_DIM), F32),
                   jax.ShapeDtypeStruct((8, 128), F32), jax.ShapeDtypeStruct((1, D_MODEL), F32)],
        scratch_shapes=[pltpu.VMEM((D_MODEL, D_MODEL), BF16), pltpu.VMEM((D_MODEL, D_MODEL), BF16),
                        pltpu.VMEM((tm + BLOCK, 2 * KV_DIM), BF16), pltpu.VMEM((tm, D_MODEL), F32),
                        pltpu.VMEM((tm, D_MODEL), F32), pltpu.VMEM((tm + BLOCK, 2 * KV_DIM), F32),
                        pltpu.VMEM((BLOCK, 2 * KV_DIM), F32), pltpu.SemaphoreType.DMA((2,))],
        compiler_params=_params(),
    )(dh, h, q, kv, kv, ao, lse, nmix, sinks, w_q, w_o)


def _wgrad(a, b, bn, col_sharded, name, layer=0, n_layers=1, stacked=None):
    T, K = a.shape
    N = b.shape[1]
    tt = min(1024, T)
    nn, ntt = N // bn, T // tt
    kr = K // N_SHARD

    def body(a_ref, b_ref, *rest):
        o_ref = rest[-1]

        @pl.when(pl.program_id(1) == 0)
        def _():
            o_ref[...] = jnp.zeros(o_ref.shape, F32)
        d = _dot_tn(a_ref[...].astype(BF16), b_ref[...].astype(BF16))
        if col_sharded:
            o_ref[...] += d
        else:
            for j in range(N_SHARD):
                o_ref[j] += d[j * kr:(j + 1) * kr]

    if col_sharded:
        assert nn == N_SHARD
        out_spec = pl.BlockSpec((None, None, K, bn), lambda n, t: (n, layer, 0, 0))
        out_shape = jax.ShapeDtypeStruct((N_SHARD, n_layers, K, bn), F32)
    else:
        out_spec = pl.BlockSpec((N_SHARD, None, kr, bn), lambda n, t: (0, layer, 0, n))
        out_shape = jax.ShapeDtypeStruct((N_SHARD, n_layers, kr, N), F32)
    in_specs = [pl.BlockSpec((tt, K), lambda n, t: (t, 0)), pl.BlockSpec((tt, bn), lambda n, t: (t, n))]
    args = [a, b]
    aliases = {}
    if stacked is not None:
        in_specs.append(ANY)
        args.append(stacked)
        aliases = {2: 0}
    return pl.pallas_call(
        body, name=name, grid=(nn, ntt), in_specs=in_specs, out_specs=out_spec, out_shape=out_shape,
        input_output_aliases=aliases,
        compiler_params=pltpu.CompilerParams(dimension_semantics=("arbitrary",) * 2, vmem_limit_bytes=VMEM_LIMIT),
    )(*args)


def _mesh_pos():
    return lax.axis_index("x"), lax.axis_index("y"), lax.axis_index("c")


def _other_chips(x, y):
    return [(1 - x, y), (x, 1 - y), (1 - x, 1 - y)]


def _allgather_halves(shards):
    n = len(shards)

    def body(*refs):
        ins, outs = refs[:n], refs[n:2 * n]
        send_sems, recv_sems, loc_sems = refs[2 * n:]
        x, y, c = _mesh_pos()
        sibling = (x, y, 1 - c)
        chips = _other_chips(x, y)

        def blk(t, px, py, pc):
            return outs[t].at[2 * px + py, pc]

        def copy(t, k, block, to, src=None):
            return pltpu.make_async_remote_copy(
                src_ref=blk(t, *block) if src is None else src, dst_ref=blk(t, *block),
                send_sem=send_sems.at[t, k], recv_sem=recv_sems.at[t, k],
                device_id=to, device_id_type=MESH)

        me = (x, y, c)
        mine = [pltpu.make_async_copy(ins[t].at[c], blk(t, *me), loc_sems.at[t]) for t in range(n)]
        for cp in mine:
            cp.start()
        first = []
        for t in range(n):
            first.append(copy(t, 0, me, sibling, src=ins[t].at[c]))
            first += [copy(t, 1 + j, me, (*chip, c), src=ins[t].at[c]) for j, chip in enumerate(chips)]
        for cp in first:
            cp.start()
        passed = []
        for j, chip in enumerate(chips):
            for t in range(n):
                copy(t, 1 + j, (*chip, c), me).wait_recv()
                cp = copy(t, 4 + j, (*chip, c), sibling)
                cp.start()
                passed.append(cp)
        for t in range(n):
            copy(t, 0, (x, y, 1 - c), me).wait_recv()
            for j, chip in enumerate(chips):
                copy(t, 4 + j, (*chip, 1 - c), me).wait_recv()
        for cp in first + passed:
            cp.wait_send()
        for cp in mine:
            cp.wait()

    return pl.pallas_call(
        body, name="weights_allgather",
        in_specs=[ANY] * n, out_specs=[ANY] * n,
        out_shape=[jax.ShapeDtypeStruct((N_SHARD,) + s.shape, s.dtype) for s in shards],
        scratch_shapes=[pltpu.SemaphoreType.DMA((n, 7)), pltpu.SemaphoreType.DMA((n, 7)),
                        pltpu.SemaphoreType.DMA((n,))],
    )(*shards)


def _sibling_send_halves(grads, small):
    n = len(grads)

    def body(*refs):
        ins, sm_in = refs[:n], refs[n]
        outs, sm_out = refs[n + 1:2 * n + 1], refs[2 * n + 1]
        send_sems, recv_sems = refs[2 * n + 2:]
        x, y, c = _mesh_pos()
        sibling = (x, y, 1 - c)
        cps = []
        for t in range(n):
            half = ins[t].shape[1] // 2
            src = ins[t].at[:, pl.ds((1 - c) * half, half), :]
            cps.append(pltpu.make_async_remote_copy(
                src_ref=src, dst_ref=outs[t], send_sem=send_sems.at[t], recv_sem=recv_sems.at[t],
                device_id=sibling, device_id_type=MESH))
        cps.append(pltpu.make_async_remote_copy(
            src_ref=sm_in, dst_ref=sm_out, send_sem=send_sems.at[n], recv_sem=recv_sems.at[n],
            device_id=sibling, device_id_type=MESH))
        for cp in cps:
            cp.start()
        for cp in cps:
            cp.wait()

    out_shape = [jax.ShapeDtypeStruct((N_SHARD, g.shape[1] // 2, g.shape[2]), F32) for g in grads]
    out_shape.append(jax.ShapeDtypeStruct(small.shape, F32))
    return pl.pallas_call(
        body, name="grads_sibling_send",
        in_specs=[ANY] * (n + 1), out_specs=[ANY] * (n + 1), out_shape=out_shape,
        scratch_shapes=[pltpu.SemaphoreType.DMA((n + 1,)), pltpu.SemaphoreType.DMA((n + 1,))],
    )(*grads, small)


def _row_block(rows, cols, mult=8, limit=3 * 512 * 1024, itemsize=4):
    best = None
    for br in range(mult, rows + 1, mult):
        if rows % br == 0 and br * cols * itemsize <= limit:
            best = br
    assert best is not None, (rows, cols)
    return best


def _chip_partial(g, s, ids, name):
    _, half, cols = s.shape
    br = _row_block(half, cols, mult=16)
    nr = half // br

    def body(ids_ref, g_ref, s_ref, o_ref):
        o_ref[...] = (g_ref[...] + s_ref[...]).astype(BF16)

    return pl.pallas_call(
        body, name=name,
        grid_spec=pltpu.PrefetchScalarGridSpec(
            num_scalar_prefetch=1, grid=(3, nr),
            in_specs=[pl.BlockSpec((None, br, cols), lambda j, r, ids_ref: (ids_ref[2 + j], ids_ref[0] * nr + r, 0)),
                      pl.BlockSpec((None, br, cols), lambda j, r, ids_ref: (ids_ref[2 + j], r, 0))],
            out_specs=pl.BlockSpec((None, br, cols), lambda j, r, ids_ref: (j, r, 0))),
        out_shape=jax.ShapeDtypeStruct((3, half, cols), BF16),
        compiler_params=pltpu.CompilerParams(dimension_semantics=("arbitrary", "arbitrary")),
    )(ids, g, s)


def _chip_exchange(parts, small):
    n = len(parts)

    def body(*refs):
        ins, sm_in = refs[:n], refs[n]
        outs, sm_out = refs[n + 1:2 * n + 1], refs[2 * n + 1]
        send_sems, recv_sems = refs[2 * n + 2:]
        x, y, c = _mesh_pos()
        cps = []
        for j, (cx, cy) in enumerate(_other_chips(x, y)):
            for t in range(n):
                cps.append(pltpu.make_async_remote_copy(
                    src_ref=ins[t].at[j], dst_ref=outs[t].at[j],
                    send_sem=send_sems.at[t, j], recv_sem=recv_sems.at[t, j],
                    device_id=(cx, cy, c), device_id_type=MESH))
            cps.append(pltpu.make_async_remote_copy(
                src_ref=sm_in, dst_ref=sm_out.at[j], send_sem=send_sems.at[n, j], recv_sem=recv_sems.at[n, j],
                device_id=(cx, cy, c), device_id_type=MESH))
        for cp in cps:
            cp.start()
        for cp in cps:
            cp.wait()

    out_shape = [jax.ShapeDtypeStruct(p.shape, p.dtype) for p in parts]
    out_shape.append(jax.ShapeDtypeStruct((3,) + small.shape, F32))
    return pl.pallas_call(
        body, name="grads_chip_exchange",
        in_specs=[ANY] * (n + 1), out_specs=[ANY] * (n + 1), out_shape=out_shape,
        scratch_shapes=[pltpu.SemaphoreType.DMA((n + 1, 3)), pltpu.SemaphoreType.DMA((n + 1, 3))],
    )(*parts, small)


def _chip_sum(g, s, q, ids, name):
    _, half, cols = s.shape
    br = _row_block(half, cols, mult=16)
    nr = half // br

    def body(ids_ref, g_ref, s_ref, q_ref, o_ref):
        own = g_ref[...] + s_ref[...]
        o_ref[...] = (own + q_ref[2].astype(F32)) + (q_ref[0].astype(F32) + q_ref[1].astype(F32))

    return pl.pallas_call(
        body, name=name,
        grid_spec=pltpu.PrefetchScalarGridSpec(
            num_scalar_prefetch=1, grid=(nr,),
            in_specs=[pl.BlockSpec((None, br, cols), lambda r, ids_ref: (ids_ref[1], ids_ref[0] * nr + r, 0)),
                      pl.BlockSpec((None, br, cols), lambda r, ids_ref: (ids_ref[1], r, 0)),
                      pl.BlockSpec((3, br, cols), lambda r, ids_ref: (0, r, 0))],
            out_specs=pl.BlockSpec((br, cols), lambda r, ids_ref: (r, 0))),
        out_shape=jax.ShapeDtypeStruct((half, cols), F32),
        compiler_params=pltpu.CompilerParams(dimension_semantics=("arbitrary",)),
    )(ids, g, s, q)


def _small_sum(part, recv):
    def body(p_ref, q_ref, o_ref):
        o_ref[...] = (p_ref[...] + q_ref[2]) + (q_ref[0] + q_ref[1])

    return pl.pallas_call(body, name="chip_sum_small", out_shape=jax.ShapeDtypeStruct(part.shape, F32))(part, recv)


def _sibling_swap(halves):
    n = len(halves)

    def body(*refs):
        ins, outs = refs[:n], refs[n:2 * n]
        send_sems, recv_sems = refs[2 * n:]
        x, y, c = _mesh_pos()
        cps = [pltpu.make_async_remote_copy(
            src_ref=ins[t], dst_ref=outs[t], send_sem=send_sems.at[t], recv_sem=recv_sems.at[t],
            device_id=(x, y, 1 - c), device_id_type=MESH) for t in range(n)]
        for cp in cps:
            cp.start()
        for cp in cps:
            cp.wait()

    return pl.pallas_call(
        body, name="grads_sibling_swap",
        in_specs=[ANY] * n, out_specs=[ANY] * n,
        out_shape=[jax.ShapeDtypeStruct(h.shape, F32) for h in halves],
        scratch_shapes=[pltpu.SemaphoreType.DMA((n,)), pltpu.SemaphoreType.DMA((n,))],
    )(*halves)


def _adamw_math(w, g, m, v):
    mn = ADAM_B1 * m + (1.0 - ADAM_B1) * g
    vn = ADAM_B2 * v + (1.0 - ADAM_B2) * (g * g)
    m_hat = mn / (1.0 - ADAM_B1 ** ADAM_STEP)
    v_hat = vn / (1.0 - ADAM_B2 ** ADAM_STEP)
    return -ADAM_LR * (m_hat / (jnp.sqrt(v_hat) + ADAM_EPS) + ADAM_WD * w), mn, vn


def _adamw(w, g, m, v, name):
    R, C = w.shape
    br = _row_block(R, C)

    def body(w_ref, g_ref, m_ref, v_ref, d_ref, mo_ref, vo_ref):
        d_ref[...], mo_ref[...], vo_ref[...] = _adamw_math(w_ref[...], g_ref[...], m_ref[...], v_ref[...])

    spec = pl.BlockSpec((br, C), lambda i: (i, 0))
    return pl.pallas_call(
        body, name=name, grid=(R // br,), in_specs=[spec] * 4, out_specs=[spec] * 3,
        out_shape=[jax.ShapeDtypeStruct((R, C), F32)] * 3, compiler_params=_params(),
    )(w, g, m, v)


def _adamw_halves(w, own, sib, m, v, ids, name):
    R, C = w.shape
    half = R // 2
    br = _row_block(half, C)
    nh = half // br

    def body(ids_ref, w_ref, own_ref, sib_ref, m_ref, v_ref, g_ref, d_ref, mo_ref, vo_ref):
        is_own = (pl.program_id(0) // nh) == ids_ref[0]
        g = jnp.where(is_own, own_ref[...], sib_ref[...])
        g_ref[...] = g
        d_ref[...], mo_ref[...], vo_ref[...] = _adamw_math(w_ref[...], g, m_ref[...], v_ref[...])

    full = pl.BlockSpec((br, C), lambda r, ids_ref: (r, 0))
    own_spec = pl.BlockSpec((br, C), lambda r, ids_ref: (jnp.clip(r - ids_ref[0] * nh, 0, nh - 1), 0))
    sib_spec = pl.BlockSpec((br, C), lambda r, ids_ref: (jnp.clip(r - (1 - ids_ref[0]) * nh, 0, nh - 1), 0))
    return pl.pallas_call(
        body, name=name,
        grid_spec=pltpu.PrefetchScalarGridSpec(
            num_scalar_prefetch=1, grid=(2 * nh,),
            in_specs=[full, own_spec, sib_spec, full, full], out_specs=[full] * 4),
        out_shape=[jax.ShapeDtypeStruct((R, C), F32)] * 4, compiler_params=_params(),
    )(ids, w, own, sib, m, v)


_PACK_UNIT = 1024


def _pack(arrs):
    flat = []
    for a in arrs:
        f = a.reshape(-1).astype(F32)
        pad = (-f.shape[0]) % _PACK_UNIT
        if pad:
            f = jnp.concatenate([f, jnp.zeros((pad,), F32)])
        flat.append(f)
    return jnp.concatenate(flat).reshape(-1, 128)


def _unpack(packed, shapes):
    flat = packed.reshape(-1)
    out, off = [], 0
    for shp in shapes:
        size = int(np.prod(shp))
        out.append(flat[off:off + size].reshape(shp))
        off += size + ((-size) % _PACK_UNIT)
    return out


def kernel(x, p, norm_mix, norm_ffn, norm_ple, norm_kv, norm_final, a_w_in, a_norm_v, a_w_s, a_b_s, a_w_out, w_kv, b_w_q, b_sinks, b_w_o, f_w_up, f_conv_w, f_conv_b, f_w_down, ple_w_in, ple_w_gate, ple_b_gate, loss_target, m_norm_mix, m_norm_ffn, m_norm_ple, m_norm_kv, m_norm_final, m_a_w_in, m_a_norm_v, m_a_w_s, m_a_b_s, m_a_w_out, m_w_kv, m_b_w_q, m_b_sinks, m_b_w_o, m_f_w_up, m_f_conv_w, m_f_conv_b, m_f_w_down, m_ple_w_in, m_ple_w_gate, m_ple_b_gate, v_norm_mix, v_norm_ffn, v_norm_ple, v_norm_kv, v_norm_final, v_a_w_in, v_a_norm_v, v_a_w_s, v_a_b_s, v_a_w_out, v_w_kv, v_b_w_q, v_b_sinks, v_b_w_o, v_f_w_up, v_f_conv_w, v_f_conv_b, v_f_w_down, v_ple_w_in, v_ple_w_gate, v_ple_b_gate):
    given = dict(locals())

    def halves(a):
        a = a.astype(BF16)
        if a.shape[0] == 2:
            return a
        a = a.reshape(a.shape[-2:])
        return a.reshape(2, a.shape[0] // 2, a.shape[1])

    small_shard = _pack([a_norm_v, f_conv_w])
    pad_rows = (-small_shard.shape[0]) % 16
    if pad_rows:
        small_shard = jnp.concatenate([small_shard, jnp.zeros((pad_rows, 128), F32)])
    small_shard = small_shard.reshape(2, small_shard.shape[0] // 2, 128)
    big_names = ['a_w_in', 'a_w_out', 'w_kv', 'b_w_q', 'b_w_o', 'f_w_up', 'f_w_down', 'ple_w_in', 'ple_w_gate']
    gathered = _allgather_halves([halves(given[k]) for k in big_names] + [small_shard])
    G = dict(zip(big_names, gathered[:-1]))
    small_full = gathered[-1].reshape(N_SHARD, -1)
    gv_full = small_full[:, :256].reshape(1, D_MODEL)
    cw_full = small_full[:, _PACK_UNIT:_PACK_UNIT + 2 * 3 * FF_BLK].reshape(N_SHARD, 2, 3, FF_BLK)
    cw_full = jnp.transpose(cw_full, (1, 2, 0, 3)).reshape(2, 3, N_FF)

    w_in = G['a_w_in'].reshape(N_SHARD, D_MODEL, 2 * D_MODEL // N_SHARD)
    w_out = G['a_w_out'].reshape(D_MODEL, D_MODEL)
    w_kv_f = G['w_kv'].reshape(D_MODEL, 2 * KV_DIM)
    w_q = G['b_w_q'].reshape(D_MODEL, D_MODEL)
    w_o = G['b_w_o'].reshape(D_MODEL, D_MODEL)
    w_up, w_dn, w_pin, w_gate = G['f_w_up'], G['f_w_down'], G['ple_w_in'], G['ple_w_gate']

    loss_acc, dx, big_grads, small_grads = _local_step(
        x[0], p[0, 0], p[1, 0], loss_target[0], norm_mix, norm_ffn, norm_ple, norm_kv, norm_final, a_w_s, a_b_s,
        b_sinks, f_conv_b, ple_b_gate, gv_full, cw_full, w_in, w_out, w_kv_f, w_q, w_o, w_up, w_dn, w_pin, w_gate)
    return _reduce_and_update(given, dx, loss_acc, big_grads, small_grads)


def _local_step(xs, p0, p1, tgt, norm_mix, norm_ffn, norm_ple, norm_kv, norm_final, a_w_s, a_b_s, b_sinks,
                f_conv_b, ple_b_gate, gv_full, cw_full, w_in, w_out, w_kv_f, w_q, w_o, w_up, w_dn, w_pin, w_gate):
    tril = jnp.tril(jnp.ones((CHUNK, CHUNK), F32))
    wsm = (a_w_s[0] * tril[None]).astype(BF16)
    bsb = jnp.broadcast_to(a_b_s[0][:, :, None], (A_GROUPS, CHUNK, CHUNK))
    sinks = b_sinks[0]
    row = lambda a: a.reshape(1, -1)

    h1, zp = _mixer_a_fwd(xs, row(norm_mix[0]), gv_full, wsm, bsb, w_in, w_out)
    h2, hh0 = _ffn_fwd(h1, row(norm_ffn[0]), cw_full[0], row(f_conv_b[0]), w_up, w_dn, 0)
    h3, pe0, a0, kv = _ple_fwd_kv(h2, p0, row(norm_ple[0]), row(ple_b_gate[0]), row(norm_kv), w_pin, w_gate, w_kv_f)
    h4, q, ao, lse = _attn_fwd(h3, row(norm_mix[1]), kv, sinks, w_q, w_o)
    h5, hh1 = _ffn_fwd(h4, row(norm_ffn[1]), cw_full[1], row(f_conv_b[1]), w_up, w_dn, 1)
    dh6, pe1, a1, loss_acc, dn_final = _ple_fwd_final(
        h5, p1, tgt, row(norm_ple[1]), row(ple_b_gate[1]), row(norm_final), w_pin, w_gate)

    dh5, dpe1, da1, xg1, dbg1, dnple1 = _ple_bwd(dh6, h5, pe1, a1, row(norm_ple[1]), w_gate, 1)
    pw = D_MODEL // N_SHARD
    g_pin = _wgrad(p1, dpe1, pw, True, "wgrad_ple_in1", 1, 2)
    g_gate = _wgrad(xg1, da1, D_MODEL // 2, False, "wgrad_ple_gate1", 1, 2)
    dh4, act1, dhh1, xf1, dcw1, dcb1, dnffn1 = _ffn_bwd(
        dh5, h4, hh1, row(norm_ffn[1]), cw_full[1], row(f_conv_b[1]), w_up, w_dn, 1)
    g_dn = _wgrad(act1, dh5, D_MODEL // 2, False, "wgrad_ffn_down1", 1, 2)
    g_up = _wgrad(xf1, dhh1, FF_BLK, True, "wgrad_ffn_up1", 1, 2)
    dh3a, dq, xn1, dkv, dsink, dnmix1 = _attn_bwd(dh4, h3, q, kv, ao, lse, row(norm_mix[1]), sinks, w_q, w_o)
    g_wo = _wgrad(ao, dh4, D_MODEL // 2, False, "wgrad_attn_o")
    g_wq = _wgrad(xn1, dq, D_MODEL // 2, False, "wgrad_attn_q")
    dh2, dpe0, da0, xg0, dbg0, dnple0, kvn, dnkv = _ple_bwd(
        dh3a, h2, pe0, a0, row(norm_ple[0]), w_gate, 0, kv_args=(h3, dkv, row(norm_kv), w_kv_f))
    g_wkv = _wgrad(kvn, dkv, 2 * KV_DIM, False, "wgrad_kv")
    g_pin = _wgrad(p0, dpe0, pw, True, "wgrad_ple_in0", 0, 2, g_pin)
    g_gate = _wgrad(xg0, da0, D_MODEL // 2, False, "wgrad_ple_gate0", 0, 2, g_gate)
    dh1, act0, dhh0, xf0, dcw0, dcb0, dnffn0 = _ffn_bwd(
        dh2, h1, hh0, row(norm_ffn[0]), cw_full[0], row(f_conv_b[0]), w_up, w_dn, 0)
    g_dn = _wgrad(act0, dh2, D_MODEL // 2, False, "wgrad_ffn_down0", 0, 2, g_dn)
    g_up = _wgrad(xf0, dhh0, FF_BLK, True, "wgrad_ffn_up0", 0, 2, g_up)
    dx, gated, dzp, xn0, dws, dbs, dgv, dnmix0 = _mixer_a_bwd(
        dh1, xs, zp, row(norm_mix[0]), gv_full, wsm, bsb, tril, w_in, w_out)
    g_win = _wgrad(xn0, dzp, 2 * pw, True, "wgrad_a_in")
    g_wout = _wgrad(gated, dh1, D_MODEL // 2, False, "wgrad_a_out")
    big_grads = {'a_w_in': g_win, 'a_w_out': g_wout, 'w_kv': g_wkv, 'b_w_q': g_wq, 'b_w_o': g_wo,
                 'f_w_up': g_up, 'f_w_down': g_dn, 'ple_w_in': g_pin, 'ple_w_gate': g_gate}

    small_grads = {
        'norm_mix': jnp.concatenate([dnmix0, dnmix1]), 'norm_ffn': jnp.concatenate([dnffn0, dnffn1]),
        'norm_ple': jnp.concatenate([dnple0, dnple1]), 'norm_kv': dnkv, 'norm_final': dn_final,
        'a_norm_v': dgv, 'a_w_s': dws, 'a_b_s': dbs[:, :, 0], 'b_sinks': dsink[0, :N_Q_HEADS],
        'f_conv_w': jnp.stack([dcw0, dcw1]), 'f_conv_b': jnp.concatenate([dcb0, dcb1]),
        'ple_b_gate': jnp.concatenate([dbg0, dbg1]),
    }
    return loss_acc, dx, big_grads, small_grads


def _reduce_and_update(given, dx, loss_acc, big_grads, small_grads):
    weight_names = ['norm_mix', 'norm_ffn', 'norm_ple', 'norm_kv', 'norm_final', 'a_w_in', 'a_norm_v', 'a_w_s',
                    'a_b_s', 'a_w_out', 'w_kv', 'b_w_q', 'b_sinks', 'b_w_o', 'f_w_up', 'f_conv_w', 'f_conv_b',
                    'f_w_down', 'ple_w_in', 'ple_w_gate', 'ple_b_gate']
    cx, cy, cc = _mesh_pos()
    shard = 2 * cx + cy
    ids = jnp.stack([cc, shard, shard ^ 2, shard ^ 1, shard ^ 3]).astype(jnp.int32)
    x = given['x']
    big_names = list(big_grads)
    grads4 = [big_grads[k].reshape(N_SHARD, -1, big_grads[k].shape[-1]) for k in big_names]
    small_shapes = {
        'norm_mix': (2, D_MODEL), 'norm_ffn': (2, D_MODEL), 'norm_ple': (2, D_MODEL), 'norm_kv': (D_MODEL,),
        'norm_final': (D_MODEL,), 'a_norm_v': (1, D_MODEL), 'a_w_s': (1, A_GROUPS, CHUNK, CHUNK),
        'a_b_s': (1, A_GROUPS, CHUNK), 'b_sinks': (1, N_Q_HEADS), 'f_conv_w': (2, 3, N_FF),
        'f_conv_b': (2, N_FF), 'ple_b_gate': (2, D_MODEL),
    }
    small_names = list(small_shapes)
    small_g = _pack([small_grads[k] for k in small_names])

    sib = _sibling_send_halves(grads4, small_g)
    parts = [_chip_partial(g, s, ids, f"chip_partial_{k}") for k, g, s in zip(big_names, grads4, sib[:-1])]
    small_chip = _small_add(small_g, sib[-1])
    recv = _chip_exchange(parts, small_chip)
    own_halves = [_chip_sum(g, s, q, ids, f"chip_sum_{k}") for k, g, s, q in zip(big_names, grads4, sib[:-1], recv[:-1])]
    small_red = _small_sum(small_chip, recv[-1])
    sib_halves = _sibling_swap(own_halves)

    out_g, out_d, out_m, out_v = {}, {}, {}, {}
    for name, own, sb in zip(big_names, own_halves, sib_halves):
        w = given[name]
        c2 = w.shape[-1]
        res = _adamw_halves(w.reshape(-1, c2), own, sb, given['m_' + name].reshape(-1, c2),
                            given['v_' + name].reshape(-1, c2), ids, f"adamw_{name}")
        out_g[name], out_d[name], out_m[name], out_v[name] = [r.reshape(w.shape) for r in res]

    full_small = dict(zip(small_names, _unpack(small_red, [small_shapes[k] for k in small_names])))
    local_small = dict(full_small)
    local_small['a_norm_v'] = lax.dynamic_slice_in_dim(full_small['a_norm_v'], shard * 256, 256, axis=1)
    local_small['f_conv_w'] = lax.dynamic_slice_in_dim(full_small['f_conv_w'], shard * FF_BLK, FF_BLK, axis=2)
    sg = _pack([local_small[k] for k in small_names])
    sw = _pack([given[k] for k in small_names])
    sm = _pack([given['m_' + k] for k in small_names])
    sv = _pack([given['v_' + k] for k in small_names])
    sd, smn, svn = _adamw(sw, sg, sm, sv, "adamw_small")
    local_shapes = [given[k].shape for k in small_names]
    for dst, packed in ((out_d, sd), (out_m, smn), (out_v, svn)):
        dst.update(zip(small_names, _unpack(packed, local_shapes)))
    for k in small_names:
        out_g[k] = local_small[k].reshape(given[k].shape)

    loss = lax.psum(loss_acc[0, 0], ("x", "y", "c"))
    grad_x = dx.reshape(x.shape)
    return (loss, grad_x, *[out_g[k] for k in weight_names], *[out_d[k] for k in weight_names],
            *[out_m[k] for k in weight_names], *[out_v[k] for k in weight_names])


def _small_add(a, b):
    def body(a_ref, b_ref, o_ref):
        o_ref[...] = a_ref[...] + b_ref[...]

    return pl.pallas_call(body, name="chip_partial_small", out_shape=jax.ShapeDtypeStruct(a.shape, F32))(a, b)
```

```python
import functools
import math

import numpy as np
import jax
import jax.numpy as jnp
from jax import lax
from jax.experimental import pallas as pl
from jax.experimental.pallas import tpu as pltpu

F32 = jnp.float32
BF16 = jnp.bfloat16

D_MODEL = 1024
CHUNK = 128
A_GROUPS = 8
HEAD_DIM = 64
N_Q_HEADS = 16
N_KV_HEADS = 4
GQA_GROUP = N_Q_HEADS // N_KV_HEADS
KV_DIM = N_KV_HEADS * HEAD_DIM
BLOCK = 128
D_FF = 2816
N_FF = 2 * D_FF
FF_BLK = N_FF // 4
PLE_DIM = 256
EPS = 1e-6
NEG = -1e30
N_SHARD = 4

ADAM_LR = 0.001
ADAM_B1 = 0.9
ADAM_B2 = 0.999
ADAM_EPS = 1e-08
ADAM_WD = 0.01
ADAM_STEP = 10

VMEM_LIMIT = 60 * 1024 * 1024
MESH = pl.DeviceIdType.MESH
ANY = pl.BlockSpec(memory_space=pl.ANY)
SMEM = pl.BlockSpec(memory_space=pltpu.SMEM)

_SLOPES = [float(np.float32(2.0 ** (-8.0 * (h + 1) / N_Q_HEADS))) for h in range(N_Q_HEADS)]


def _dot(a, b):
    return jnp.dot(a, b, preferred_element_type=F32)


def _dot_nt(a, b):
    return lax.dot_general(a, b, (((1,), (1,)), ((), ())), preferred_element_type=F32)


def _dot_tn(a, b):
    return lax.dot_general(a, b, (((0,), (0,)), ((), ())), preferred_element_type=F32)


def _rms(x, g):
    r = lax.rsqrt(jnp.mean(x * x, axis=-1, keepdims=True) + EPS)
    xh = x * r
    return xh * g, xh, r


def _rms_bwd(dy, xh, r, g):
    dxh = dy * g
    dg = jnp.sum(dy * xh, axis=0, keepdims=True)
    dx = r * (dxh - xh * jnp.mean(dxh * xh, axis=-1, keepdims=True))
    return dx, dg


_GELU_C = math.sqrt(2.0 / math.pi)


def _gelu(x):
    t = jnp.tanh(_GELU_C * (x + 0.044715 * (x * x * x)))
    return 0.5 * x * (1.0 + t)


def _gelu_grad(x):
    x2 = x * x
    t = jnp.tanh(_GELU_C * (x + 0.044715 * (x2 * x)))
    return 0.5 * (1.0 + t) + 0.5 * x * (1.0 - t * t) * (_GELU_C * (1.0 + 3.0 * 0.044715 * x2))


def _sigmoid(x):
    return 1.0 / (1.0 + jnp.exp(-x))


def _load_once(pairs, sem):
    @pl.when(pl.program_id(0) == 0)
    def _():
        cps = [pltpu.make_async_copy(s, d, sem.at[i]) for i, (s, d) in enumerate(pairs)]
        for cp in cps:
            cp.start()
        for cp in cps:
            cp.wait()


def _params(n_axes=1, vmem=VMEM_LIMIT):
    return pltpu.CompilerParams(dimension_semantics=("arbitrary",) * n_axes, vmem_limit_bytes=vmem)


def _row_spec(tm, n, rev_nt=None):
    if rev_nt is None:
        return pl.BlockSpec((tm, n), lambda i: (i, 0))
    return pl.BlockSpec((tm, n), lambda i: (rev_nt - 1 - i, 0))


def _const_spec(shape):
    nd = len(shape)
    return pl.BlockSpec(shape, lambda i: (0,) * nd)


def _zero_first(refs):
    @pl.when(pl.program_id(0) == 0)
    def _():
        for r in refs:
            r[...] = jnp.zeros(r.shape, r.dtype)


def _mixer_a_fwd(x, nmix, gv, wsm, bsb, w_in, w_out):
    T = x.shape[0]
    tm = min(512, T)
    nt = T // tm
    nw = 2 * D_MODEL // N_SHARD

    def body(x_ref, nmix_ref, gv_ref, ws_ref, bsb_ref, w_in_hbm, w_out_hbm,
             h1_ref, zp_ref, w_in_v, w_out_v, gated_v, sem):
        _load_once([(w_in_hbm, w_in_v), (w_out_hbm, w_out_v)], sem)
        xv = x_ref[...]
        xn = _rms(xv, nmix_ref[...])[0].astype(BF16)
        for j in range(N_SHARD):
            zp_ref[:, j * nw:(j + 1) * nw] = _dot(xn, w_in_v[j])
        z = _gelu(zp_ref[...])
        u = z[:, :D_MODEL]
        vn = _rms(z[:, D_MODEL:], gv_ref[...])[0].astype(BF16)
        for c in range(tm // CHUNK):
            rows = slice(c * CHUNK, (c + 1) * CHUNK)
            for h in range(A_GROUPS):
                cols = slice(h * CHUNK, (h + 1) * CHUNK)
                s = _dot(ws_ref[h], vn[rows, cols]) + bsb_ref[h]
                gated_v[rows, cols] = (u[rows, cols] * s).astype(BF16)
        h1_ref[...] = xv + _dot(gated_v[...], w_out_v[...])

    return pl.pallas_call(
        body, name="mixer_a_fwd", grid=(nt,),
        in_specs=[_row_spec(tm, D_MODEL), _const_spec((1, D_MODEL)), _const_spec((1, D_MODEL)),
                  _const_spec((A_GROUPS, CHUNK, CHUNK)), _const_spec((A_GROUPS, CHUNK, CHUNK)), ANY, ANY],
        out_specs=[_row_spec(tm, D_MODEL), _row_spec(tm, 2 * D_MODEL)],
        out_shape=[jax.ShapeDtypeStruct((T, D_MODEL), F32), jax.ShapeDtypeStruct((T, 2 * D_MODEL), F32)],
        scratch_shapes=[pltpu.VMEM((N_SHARD, D_MODEL, nw), BF16), pltpu.VMEM((D_MODEL, D_MODEL), BF16),
                        pltpu.VMEM((tm, D_MODEL), BF16), pltpu.SemaphoreType.DMA((2,))],
        compiler_params=_params(),
    )(x, nmix, gv, wsm, bsb, w_in, w_out)


def _mixer_a_bwd(dh, x, zp, nmix, gv, wsm, bsb, tril, w_in, w_out):
    T = x.shape[0]
    tm = min(256, T)
    nt = T // tm
    nw = 2 * D_MODEL // N_SHARD

    def body(dh_ref, x_ref, zp_ref, nmix_ref, gv_ref, ws_ref, bsb_ref, tril_ref, w_in_hbm, w_out_hbm,
             dx_ref, gated_ref, dzp_ref, xn_ref, dws_ref, dbs_ref, dgv_ref, dnmix_ref,
             w_in_v, w_out_v, du_v, dvn_v, dbs_v, sem):
        _load_once([(w_in_hbm, w_in_v), (w_out_hbm, w_out_v)], sem)
        _zero_first([dws_ref, dbs_v, dgv_ref, dnmix_ref])
        i = pl.program_id(0)
        dhv = dh_ref[...]
        xv = x_ref[...]
        xn, xh, r = _rms(xv, nmix_ref[...])
        xn_ref[...] = xn.astype(BF16)
        zpv = zp_ref[...]
        z = _gelu(zpv)
        u = z[:, :D_MODEL]
        vn_f, vh, rv = _rms(z[:, D_MODEL:], gv_ref[...])
        vn = vn_f.astype(BF16)
        dgated = _dot_nt(dhv.astype(BF16), w_out_v[...])
        for c in range(tm // CHUNK):
            rows = slice(c * CHUNK, (c + 1) * CHUNK)
            for h in range(A_GROUPS):
                cols = slice(h * CHUNK, (h + 1) * CHUNK)
                vn_h = vn[rows, cols]
                s = _dot(ws_ref[h], vn_h) + bsb_ref[h]
                dgt = dgated[rows, cols]
                u_h = u[rows, cols]
                gated_ref[rows, cols] = (u_h * s).astype(BF16)
                du_v[rows, cols] = dgt * s
                ds = dgt * u_h
                dsb = ds.astype(BF16)
                dws_ref[h] += _dot_nt(dsb, vn_h)
                dbs_v[h] += ds
                dvn_v[rows, cols] = _dot_tn(ws_ref[h], dsb)
        dv, dgv = _rms_bwd(dvn_v[...], vh, rv, gv_ref[...])
        dgv_ref[...] += dgv
        dzu = (du_v[...] * _gelu_grad(zpv[:, :D_MODEL])).astype(BF16)
        dzv = (dv * _gelu_grad(zpv[:, D_MODEL:])).astype(BF16)
        dzp_ref[:, :D_MODEL] = dzu
        dzp_ref[:, D_MODEL:] = dzv
        dxn = _dot_nt(dzu[:, :nw], w_in_v[0]) + _dot_nt(dzu[:, nw:], w_in_v[1])
        dxn += _dot_nt(dzv[:, :nw], w_in_v[2]) + _dot_nt(dzv[:, nw:], w_in_v[3])
        dxx, dn = _rms_bwd(dxn, xh, r, nmix_ref[...])
        dnmix_ref[...] += dn
        dx_ref[...] = dhv + dxx

        @pl.when(i == nt - 1)
        def _():
            for h in range(A_GROUPS):
                dws_ref[h] = dws_ref[h] * tril_ref[...]
                dbs_ref[h] = jnp.broadcast_to(jnp.sum(dbs_v[h], axis=1, keepdims=True), (CHUNK, CHUNK))

    grp = (A_GROUPS, CHUNK, CHUNK)
    return pl.pallas_call(
        body, name="mixer_a_bwd", grid=(nt,),
        in_specs=[_row_spec(tm, D_MODEL), _row_spec(tm, D_MODEL), _row_spec(tm, 2 * D_MODEL),
                  _const_spec((1, D_MODEL)), _const_spec((1, D_MODEL)), _const_spec(grp), _const_spec(grp),
                  _const_spec((CHUNK, CHUNK)), ANY, ANY],
        out_specs=[_row_spec(tm, D_MODEL), _row_spec(tm, D_MODEL), _row_spec(tm, 2 * D_MODEL),
                   _row_spec(tm, D_MODEL), _const_spec(grp), _const_spec(grp),
                   _const_spec((1, D_MODEL)), _const_spec((1, D_MODEL))],
        out_shape=[jax.ShapeDtypeStruct((T, D_MODEL), F32), jax.ShapeDtypeStruct((T, D_MODEL), BF16),
                   jax.ShapeDtypeStruct((T, 2 * D_MODEL), BF16), jax.ShapeDtypeStruct((T, D_MODEL), BF16),
                   jax.ShapeDtypeStruct(grp, F32), jax.ShapeDtypeStruct(grp, F32),
                   jax.ShapeDtypeStruct((1, D_MODEL), F32), jax.ShapeDtypeStruct((1, D_MODEL), F32)],
        scratch_shapes=[pltpu.VMEM((N_SHARD, D_MODEL, nw), BF16), pltpu.VMEM((D_MODEL, D_MODEL), BF16),
                        pltpu.VMEM((tm, D_MODEL), F32), pltpu.VMEM((tm, D_MODEL), F32),
                        pltpu.VMEM(grp, F32), pltpu.SemaphoreType.DMA((2,))],
        compiler_params=_params(),
    )(dh, x, zp, nmix, gv, wsm, bsb, tril, w_in, w_out)


def _load_ffn_weights(w_up_hbm, w_dn_hbm, layer, w_up_v, w_dn_v, sem):
    _load_once([(w_up_hbm, w_up_v), (w_dn_hbm, w_dn_v)], sem)


def _ffn_fwd(h, nffn, cw, cb, w_up, w_dn, layer):
    T = h.shape[0]
    tm = min(256, T)
    nt = T // tm

    def body(h_ref, n_ref, cw_ref, cb_ref, w_up_hbm, w_dn_hbm, out_ref, hh_ref,
             w_up_v, w_dn_v, carry_v, sem):
        _load_ffn_weights(w_up_hbm, w_dn_hbm, layer, w_up_v, w_dn_v, sem)
        _zero_first([carry_v])
        xv = h_ref[...]
        xf = _rms(xv, n_ref[...])[0].astype(BF16)
        acc = xv
        for j in range(2):
            cs = []
            for blk in (j, j + 2):
                cols = slice(blk * FF_BLK, (blk + 1) * FF_BLK)
                hh = _dot(xf, w_up_v[blk])
                hh_ref[:, cols] = hh.astype(BF16)
                ext = jnp.concatenate([carry_v[blk], hh], axis=0)
                carry_v[blk] = hh[tm - 8:, :]
                s1 = pltpu.roll(ext, 1, 0)[8:]
                s2 = pltpu.roll(ext, 2, 0)[8:]
                cs.append(cb_ref[:, cols] + cw_ref[0:1, cols] * s2 + cw_ref[1:2, cols] * s1
                          + cw_ref[2:3, cols] * hh)
            act = (cs[0] * _sigmoid(cs[0]) * cs[1]).astype(BF16)
            acc = acc + _dot(act, w_dn_v[j * FF_BLK:(j + 1) * FF_BLK, :])
        out_ref[...] = acc

    return pl.pallas_call(
        body, name=f"ffn_fwd{layer}", grid=(nt,),
        in_specs=[_row_spec(tm, D_MODEL), _const_spec((1, D_MODEL)), _const_spec((3, N_FF)),
                  _const_spec((1, N_FF)), ANY, ANY],
        out_specs=[_row_spec(tm, D_MODEL), _row_spec(tm, N_FF)],
        out_shape=[jax.ShapeDtypeStruct((T, D_MODEL), F32), jax.ShapeDtypeStruct((T, N_FF), BF16)],
        scratch_shapes=[pltpu.VMEM((N_SHARD, D_MODEL, FF_BLK), BF16), pltpu.VMEM((D_FF, D_MODEL), BF16),
                        pltpu.VMEM((N_SHARD, 8, FF_BLK), F32), pltpu.SemaphoreType.DMA((2 * N_SHARD,))],
        compiler_params=_params(),
    )(h, nffn, cw, cb, w_up, w_dn)


def _ffn_bwd(dh, h, hh, nffn, cw, cb, w_up, w_dn, layer):
    T = h.shape[0]
    tm = min(256, T)
    nt = T // tm
    pv = 16

    def body(dh_ref, h_ref, hh_ref, hhp_ref, n_ref, cw_ref, cb_ref, w_up_hbm, w_dn_hbm,
             dhin_ref, act_ref, dhh_ref, xf_ref, dcw_ref, dcb_ref, dn_ref,
             w_up_v, w_dn_v, carry_v, sem):
        _load_ffn_weights(w_up_hbm, w_dn_hbm, layer, w_up_v, w_dn_v, sem)
        _zero_first([carry_v, dcw_ref, dcb_ref, dn_ref])
        ti = nt - 1 - pl.program_id(0)
        keep_prev = jnp.where(ti > 0, 1.0, 0.0).astype(F32)
        dout = dh_ref[...]
        doutb = dout.astype(BF16)
        xf_f, xh, r = _rms(h_ref[...], n_ref[...])
        xf_ref[...] = xf_f.astype(BF16)
        dxf = jnp.zeros((tm, D_MODEL), F32)
        for j in range(2):
            fw = []
            for blk in (j, j + 2):
                cols = slice(blk * FF_BLK, (blk + 1) * FF_BLK)
                hhv = hh_ref[:, cols].astype(F32)
                prev = hhp_ref[:, cols].astype(F32) * keep_prev
                ext = jnp.concatenate([prev, hhv], axis=0)
                s1 = pltpu.roll(ext, 1, 0)[pv:]
                s2 = pltpu.roll(ext, 2, 0)[pv:]
                cv = (cb_ref[:, cols] + cw_ref[0:1, cols] * s2 + cw_ref[1:2, cols] * s1
                      + cw_ref[2:3, cols] * hhv)
                fw.append((blk, cols, hhv, s1, s2, cv))
            cg, cu = fw[0][5], fw[1][5]
            sg = _sigmoid(cg)
            sil = cg * sg
            act_ref[:, j * FF_BLK:(j + 1) * FF_BLK] = (sil * cu).astype(BF16)
            dact = _dot_nt(doutb, w_dn_v[j * FF_BLK:(j + 1) * FF_BLK, :])
            dcs = (dact * cu * (sg * (1.0 + cg * (1.0 - sg))), dact * sil)
            for (blk, cols, hhv, s1, s2, _), dc in zip(fw, dcs):
                dcb_ref[:, cols] += jnp.sum(dc, axis=0, keepdims=True)
                dcw_ref[0:1, cols] += jnp.sum(dc * s2, axis=0, keepdims=True)
                dcw_ref[1:2, cols] += jnp.sum(dc * s1, axis=0, keepdims=True)
                dcw_ref[2:3, cols] += jnp.sum(dc * hhv, axis=0, keepdims=True)
                ext = jnp.concatenate([dc, carry_v[blk]], axis=0)
                carry_v[blk] = dc[:8, :]
                n = tm + 8
                a1 = pltpu.roll(ext, n - 1, 0)[:tm]
                a2 = pltpu.roll(ext, n - 2, 0)[:tm]
                dhh = (cw_ref[2:3, cols] * dc + cw_ref[1:2, cols] * a1 + cw_ref[0:1, cols] * a2).astype(BF16)
                dhh_ref[:, cols] = dhh
                dxf = dxf + _dot_nt(dhh, w_up_v[blk])
        dxx, dn = _rms_bwd(dxf, xh, r, n_ref[...])
        dn_ref[...] += dn
        dhin_ref[...] = dout + dxx

    rev = functools.partial(_row_spec, rev_nt=nt)
    prev_spec = pl.BlockSpec((pv, N_FF), lambda i: (jnp.maximum((nt - 1 - i) * (tm // pv) - 1, 0), 0))
    return pl.pallas_call(
        body, name=f"ffn_bwd{layer}", grid=(nt,),
        in_specs=[rev(tm, D_MODEL), rev(tm, D_MODEL), rev(tm, N_FF), prev_spec,
                  _const_spec((1, D_MODEL)), _const_spec((3, N_FF)), _const_spec((1, N_FF)), ANY, ANY],
        out_specs=[rev(tm, D_MODEL), rev(tm, D_FF), rev(tm, N_FF), rev(tm, D_MODEL),
                   _const_spec((3, N_FF)), _const_spec((1, N_FF)), _const_spec((1, D_MODEL))],
        out_shape=[jax.ShapeDtypeStruct((T, D_MODEL), F32), jax.ShapeDtypeStruct((T, D_FF), BF16),
                   jax.ShapeDtypeStruct((T, N_FF), BF16), jax.ShapeDtypeStruct((T, D_MODEL), BF16),
                   jax.ShapeDtypeStruct((3, N_FF), F32), jax.ShapeDtypeStruct((1, N_FF), F32),
                   jax.ShapeDtypeStruct((1, D_MODEL), F32)],
        scratch_shapes=[pltpu.VMEM((N_SHARD, D_MODEL, FF_BLK), BF16), pltpu.VMEM((D_FF, D_MODEL), BF16),
                        pltpu.VMEM((N_SHARD, 8, FF_BLK), F32), pltpu.SemaphoreType.DMA((2 * N_SHARD,))],
        compiler_params=_params(),
    )(dh, h, hh, hh, nffn, cw, cb, w_up, w_dn)


def _load_ple_weights(w_pin_hbm, w_gate_hbm, layer, w_pin_v, w_gate_v, sem, extra=()):
    _load_once([(w_pin_hbm, w_pin_v), (w_gate_hbm, w_gate_v)] + list(extra), sem)


def _ple_fwd_kv(h, p, nple, bg, nkv, w_pin, w_gate, w_kv):
    T = h.shape[0]
    tm = min(512, T)
    nt = T // tm
    pw = D_MODEL // N_SHARD

    def body(h_ref, p_ref, n_ref, bg_ref, nkv_ref, w_pin_hbm, w_gate_hbm, w_kv_hbm,
             out_ref, pe_ref, a_ref, kv_ref, w_pin_v, w_gate_v, w_kv_v, sem):
        _load_ple_weights(w_pin_hbm, w_gate_hbm, 0, w_pin_v, w_gate_v, sem, [(w_kv_hbm, w_kv_v)])
        xv = h_ref[...]
        xg = _rms(xv, n_ref[...])[0].astype(BF16)
        a = _dot(xg, w_gate_v[...]) + bg_ref[...]
        a_ref[...] = a
        pb = p_ref[...].astype(BF16)
        for j in range(N_SHARD):
            pe_ref[:, j * pw:(j + 1) * pw] = _dot(pb, w_pin_v[j])
        hn = xv + pe_ref[...] * _sigmoid(a)
        out_ref[...] = hn
        kvn = _rms(hn, nkv_ref[...])[0].astype(BF16)
        kv_ref[...] = _dot(kvn, w_kv_v[...]).astype(BF16)

    vec = _const_spec((1, D_MODEL))
    return pl.pallas_call(
        body, name="ple_fwd0", grid=(nt,),
        in_specs=[_row_spec(tm, D_MODEL), _row_spec(tm, PLE_DIM), vec, vec, vec, ANY, ANY, ANY],
        out_specs=[_row_spec(tm, D_MODEL), _row_spec(tm, D_MODEL), _row_spec(tm, D_MODEL),
                   _row_spec(tm, 2 * KV_DIM)],
        out_shape=[jax.ShapeDtypeStruct((T, D_MODEL), F32), jax.ShapeDtypeStruct((T, D_MODEL), F32),
                   jax.ShapeDtypeStruct((T, D_MODEL), F32), jax.ShapeDtypeStruct((T, 2 * KV_DIM), BF16)],
        scratch_shapes=[pltpu.VMEM((N_SHARD, PLE_DIM, pw), BF16), pltpu.VMEM((D_MODEL, D_MODEL), BF16),
                        pltpu.VMEM((D_MODEL, 2 * KV_DIM), BF16), pltpu.SemaphoreType.DMA((2 * N_SHARD + 1,))],
        compiler_params=_params(),
    )(h, p, nple, bg, nkv, w_pin, w_gate, w_kv)


def _ple_fwd_final(h, p, tgt, nple, bg, nfin, w_pin, w_gate):
    T = h.shape[0]
    tm = min(512, T)
    nt = T // tm
    pw = D_MODEL // N_SHARD

    def body(h_ref, p_ref, t_ref, n_ref, bg_ref, nf_ref, w_pin_hbm, w_gate_hbm,
             dh_ref, pe_ref, a_ref, loss_ref, dnf_ref, w_pin_v, w_gate_v, sem):
        _load_ple_weights(w_pin_hbm, w_gate_hbm, 1, w_pin_v, w_gate_v, sem)
        _zero_first([loss_ref, dnf_ref])
        xv = h_ref[...]
        xg = _rms(xv, n_ref[...])[0].astype(BF16)
        a = _dot(xg, w_gate_v[...]) + bg_ref[...]
        a_ref[...] = a
        pb = p_ref[...].astype(BF16)
        for j in range(N_SHARD):
            pe_ref[:, j * pw:(j + 1) * pw] = _dot(pb, w_pin_v[j])
        hn = xv + pe_ref[...] * _sigmoid(a)
        y, yh, r = _rms(hn, nf_ref[...])
        diff = y - t_ref[...]
        loss_ref[...] += 0.5 * jnp.sum(jnp.mean(diff * diff, axis=-1, keepdims=True))
        dy = diff * (1.0 / D_MODEL)
        dhn, dnf = _rms_bwd(dy, yh, r, nf_ref[...])
        dnf_ref[...] += dnf
        dh_ref[...] = dhn

    vec = _const_spec((1, D_MODEL))
    return pl.pallas_call(
        body, name="ple_fwd1", grid=(nt,),
        in_specs=[_row_spec(tm, D_MODEL), _row_spec(tm, PLE_DIM), _row_spec(tm, D_MODEL), vec, vec, vec, ANY, ANY],
        out_specs=[_row_spec(tm, D_MODEL), _row_spec(tm, D_MODEL), _row_spec(tm, D_MODEL),
                   _const_spec((8, 128)), vec],
        out_shape=[jax.ShapeDtypeStruct((T, D_MODEL), F32), jax.ShapeDtypeStruct((T, D_MODEL), F32),
                   jax.ShapeDtypeStruct((T, D_MODEL), F32), jax.ShapeDtypeStruct((8, 128), F32),
                   jax.ShapeDtypeStruct((1, D_MODEL), F32)],
        scratch_shapes=[pltpu.VMEM((N_SHARD, PLE_DIM, pw), BF16), pltpu.VMEM((D_MODEL, D_MODEL), BF16),
                        pltpu.SemaphoreType.DMA((2 * N_SHARD,))],
        compiler_params=_params(),
    )(h, p, tgt, nple, bg, nfin, w_pin, w_gate)


def _ple_bwd(dh, hb, pe, a, nple, w_gate, layer, kv_args=None):
    T = hb.shape[0]
    tm = min(512, T)
    nt = T // tm
    with_kv = kv_args is not None
    rows = D_MODEL // N_SHARD

    def body(*refs):
        if with_kv:
            (dh_ref, hb_ref, pe_ref, a_ref, n_ref, w_gate_hbm, hc_ref, dkv_ref, nkv_ref, w_kv_hbm,
             dhb_ref, dpe_ref, da_ref, xg_ref, dbg_ref, dn_ref, kvn_ref, dnkv_ref,
             w_gate_v, w_kv_v, sem) = refs
        else:
            (dh_ref, hb_ref, pe_ref, a_ref, n_ref, w_gate_hbm,
             dhb_ref, dpe_ref, da_ref, xg_ref, dbg_ref, dn_ref, w_gate_v, sem) = refs
        pairs = [(w_gate_hbm, w_gate_v)]
        if with_kv:
            pairs.append((w_kv_hbm, w_kv_v))
        _load_once(pairs, sem)
        _zero_first([dbg_ref, dn_ref] + ([dnkv_ref] if with_kv else []))
        do = dh_ref[...]
        if with_kv:
            dkvn = _dot_nt(dkv_ref[...].astype(BF16), w_kv_v[...])
            kvn, kh, kr = _rms(hc_ref[...], nkv_ref[...])
            kvn_ref[...] = kvn.astype(BF16)
            dk, dnkv = _rms_bwd(dkvn, kh, kr, nkv_ref[...])
            dnkv_ref[...] += dnkv
            do = do + dk
        gate = _sigmoid(a_ref[...])
        dpe_ref[...] = (do * gate).astype(BF16)
        da = do * pe_ref[...] * (gate * (1.0 - gate))
        dab = da.astype(BF16)
        da_ref[...] = dab
        dbg_ref[...] += jnp.sum(da, axis=0, keepdims=True)
        dxg = _dot_nt(dab, w_gate_v[...])
        xg, xh, r = _rms(hb_ref[...], n_ref[...])
        xg_ref[...] = xg.astype(BF16)
        dxx, dn = _rms_bwd(dxg, xh, r, n_ref[...])
        dn_ref[...] += dn
        dhb_ref[...] = do + dxx

    vec = _const_spec((1, D_MODEL))
    row = _row_spec(tm, D_MODEL)
    in_specs = [row, row, row, row, vec, ANY]
    args = [dh, hb, pe, a, nple, w_gate]
    out_specs = [row, row, row, row, vec, vec]
    out_shape = [jax.ShapeDtypeStruct((T, D_MODEL), F32), jax.ShapeDtypeStruct((T, D_MODEL), BF16),
                 jax.ShapeDtypeStruct((T, D_MODEL), BF16), jax.ShapeDtypeStruct((T, D_MODEL), BF16),
                 jax.ShapeDtypeStruct((1, D_MODEL), F32), jax.ShapeDtypeStruct((1, D_MODEL), F32)]
    scratch = [pltpu.VMEM((D_MODEL, D_MODEL), BF16)]
    if with_kv:
        hc, dkv, nkv, w_kv = kv_args
        in_specs += [row, _row_spec(tm, 2 * KV_DIM), vec, ANY]
        args += [hc, dkv, nkv, w_kv]
        out_specs += [row, vec]
        out_shape += [jax.ShapeDtypeStruct((T, D_MODEL), BF16), jax.ShapeDtypeStruct((1, D_MODEL), F32)]
        scratch.append(pltpu.VMEM((D_MODEL, 2 * KV_DIM), BF16))
    scratch.append(pltpu.SemaphoreType.DMA((N_SHARD + 1,)))
    return pl.pallas_call(
        body, name=f"ple_bwd{layer}", grid=(nt,), in_specs=in_specs, out_specs=out_specs,
        out_shape=out_shape, scratch_shapes=scratch, compiler_params=_params(),
    )(*args)


def _band_masks(is_first):
    ii = lax.broadcasted_iota(jnp.int32, (BLOCK, 2 * BLOCK), 0)
    jj = lax.broadcasted_iota(jnp.int32, (BLOCK, 2 * BLOCK), 1)
    dist = ii + BLOCK - jj
    valid = (dist >= 0) & (dist < BLOCK) & ((jj >= BLOCK) | jnp.logical_not(is_first))
    return dist.astype(F32), valid


def _attn_fwd(h, nmix, kv, sinks, w_q, w_o):
    T = h.shape[0]
    tm = min(512, T)
    nt = T // tm
    nb = tm // BLOCK

    def body(h_ref, n_ref, kv_ref, kvp_ref, sink_ref, w_q_hbm, w_o_hbm,
             out_ref, q_ref, ao_ref, lse_ref, w_q_v, w_o_v, kvs_v, sem):
        _load_once([(w_q_hbm, w_q_v), (w_o_hbm, w_o_v)], sem)
        ti = pl.program_id(0)
        xv = h_ref[...]
        xn = _rms(xv, n_ref[...])[0].astype(BF16)
        q_ref[...] = (_dot(xn, w_q_v[...]) * (HEAD_DIM ** -0.5)).astype(BF16)
        kvs_v[0:BLOCK, :] = kvp_ref[...]
        kvs_v[BLOCK:, :] = kv_ref[...]
        lane = lax.broadcasted_iota(jnp.int32, (BLOCK, 128), 1)

        def blk_body(b, carry):
            r0 = pl.multiple_of(b * BLOCK, BLOCK)
            distf, valid = _band_masks(jnp.logical_and(ti == 0, b == 0))
            qb = q_ref[pl.ds(r0, BLOCK), :]
            band = kvs_v[pl.ds(r0, 2 * BLOCK), :]
            lse_mat = jnp.zeros((BLOCK, 128), F32)
            outs = []
            for hq in range(N_Q_HEADS):
                kh = hq // GQA_GROUP
                k_h = band[:, kh * HEAD_DIM:(kh + 1) * HEAD_DIM]
                v_h = band[:, KV_DIM + kh * HEAD_DIM:KV_DIM + (kh + 1) * HEAD_DIM]
                s = _dot_nt(qb[:, hq * HEAD_DIM:(hq + 1) * HEAD_DIM], k_h) - _SLOPES[hq] * distf
                s = jnp.where(valid, s, NEG)
                sink = sink_ref[hq]
                m = jnp.maximum(jnp.max(s, axis=1, keepdims=True), sink)
                e = jnp.exp(s - m)
                den = jnp.sum(e, axis=1, keepdims=True) + jnp.exp(sink - m)
                outs.append(_dot((e / den).astype(BF16), v_h))
                lse_mat = jnp.where(lane == hq, m + jnp.log(den), lse_mat)
            ao_ref[pl.ds(r0, BLOCK), :] = jnp.concatenate(outs, axis=1).astype(BF16)
            lse_ref[pl.ds(r0, BLOCK), :] = lse_mat
            return carry

        lax.fori_loop(0, nb, blk_body, 0)
        out_ref[...] = xv + _dot(ao_ref[...], w_o_v[...])

    row = _row_spec(tm, D_MODEL)
    prev_spec = pl.BlockSpec((BLOCK, 2 * KV_DIM), lambda i: (jnp.maximum(i * nb - 1, 0), 0))
    return pl.pallas_call(
        body, name="attn_fwd", grid=(nt,),
        in_specs=[row, _const_spec((1, D_MODEL)), _row_spec(tm, 2 * KV_DIM), prev_spec, SMEM, ANY, ANY],
        out_specs=[row, row, row, _row_spec(tm, 128)],
        out_shape=[jax.ShapeDtypeStruct((T, D_MODEL), F32), jax.ShapeDtypeStruct((T, D_MODEL), BF16),
                   jax.ShapeDtypeStruct((T, D_MODEL), BF16), jax.ShapeDtypeStruct((T, 128), F32)],
        scratch_shapes=[pltpu.VMEM((D_MODEL, D_MODEL), BF16), pltpu.VMEM((D_MODEL, D_MODEL), BF16),
                        pltpu.VMEM((tm + BLOCK, 2 * KV_DIM), BF16), pltpu.SemaphoreType.DMA((2,))],
        compiler_params=_params(),
    )(h, nmix, kv, kv, sinks, w_q, w_o)


def _attn_bwd(dh, h, q, kv, ao, lse, nmix, sinks, w_q, w_o):
    T = h.shape[0]
    tm = min(512, T)
    nt = T // tm
    nb = tm // BLOCK

    def body(dh_ref, h_ref, q_ref, kv_ref, kvp_ref, ao_ref, lse_ref, n_ref, sink_ref, w_q_hbm, w_o_hbm,
             dhin_ref, dq_ref, xn_ref, dkv_ref, dsink_ref, dn_ref,
             w_q_v, w_o_v, kvs_v, dao_v, dq_v, dkv_v, carry_v, sem):
        _load_once([(w_q_hbm, w_q_v), (w_o_hbm, w_o_v)], sem)
        _zero_first([carry_v, dsink_ref, dn_ref])
        ti = nt - 1 - pl.program_id(0)
        dout = dh_ref[...]
        dao_v[...] = _dot_nt(dout.astype(BF16), w_o_v[...])
        kvs_v[0:BLOCK, :] = kvp_ref[...]
        kvs_v[BLOCK:, :] = kv_ref[...]
        dkv_v[0:tm, :] = jnp.zeros((tm, 2 * KV_DIM), F32)
        dkv_v[tm:, :] = carry_v[...]
        lane = lax.broadcasted_iota(jnp.int32, (BLOCK, 128), 1)
        lane8 = lax.broadcasted_iota(jnp.int32, (8, 128), 1)

        def blk_body(b, dsk):
            r0 = pl.multiple_of(b * BLOCK, BLOCK)
            distf, valid = _band_masks(jnp.logical_and(ti == 0, b == 0))
            qb = q_ref[pl.ds(r0, BLOCK), :]
            band = kvs_v[pl.ds(r0, 2 * BLOCK), :]
            aob = ao_ref[pl.ds(r0, BLOCK), :].astype(F32)
            daob = dao_v[pl.ds(r0, BLOCK), :]
            lse_mat = lse_ref[pl.ds(r0, BLOCK), :]
            dqs = []
            dks = []
            dvs = []
            for kh in range(N_KV_HEADS):
                k_h = band[:, kh * HEAD_DIM:(kh + 1) * HEAD_DIM]
                v_h = band[:, KV_DIM + kh * HEAD_DIM:KV_DIM + (kh + 1) * HEAD_DIM]
                dk = jnp.zeros((2 * BLOCK, HEAD_DIM), F32)
                dv = jnp.zeros((2 * BLOCK, HEAD_DIM), F32)
                for g in range(GQA_GROUP):
                    hq = kh * GQA_GROUP + g
                    hc = slice(hq * HEAD_DIM, (hq + 1) * HEAD_DIM)
                    q_h = qb[:, hc]
                    s = _dot_nt(q_h, k_h) - _SLOPES[hq] * distf
                    s = jnp.where(valid, s, NEG)
                    lse = jnp.sum(jnp.where(lane == hq, lse_mat, 0.0), axis=1, keepdims=True)
                    pr = jnp.exp(s - lse)
                    dao_h = daob[:, hc]
                    dd = jnp.sum(dao_h * aob[:, hc], axis=1, keepdims=True)
                    dao_hb = dao_h.astype(BF16)
                    dp = _dot_nt(dao_hb, v_h)
                    dsb = (pr * (dp - dd)).astype(BF16)
                    dqs.append(_dot(dsb, k_h) * (HEAD_DIM ** -0.5))
                    dk = dk + _dot_tn(dsb, q_h)
                    dv = dv + _dot_tn(pr.astype(BF16), dao_hb)
                    dsv = -jnp.sum(jnp.exp(sink_ref[hq] - lse) * dd)
                    dsk = dsk + jnp.where(lane8 == hq, dsv, 0.0)
                dks.append(dk)
                dvs.append(dv)
            dq_v[pl.ds(r0, BLOCK), :] = jnp.concatenate(dqs, axis=1)
            dkv_v[pl.ds(r0, 2 * BLOCK), :] += jnp.concatenate(dks + dvs, axis=1)
            return dsk

        dsk = lax.fori_loop(0, nb, blk_body, jnp.zeros((8, 128), F32))
        dsink_ref[...] += dsk
        dqb = dq_v[...].astype(BF16)
        dq_ref[...] = dqb
        dxn = _dot_nt(dqb, w_q_v[...])
        xn, xh, r = _rms(h_ref[...], n_ref[...])
        xn_ref[...] = xn.astype(BF16)
        dxx, dn = _rms_bwd(dxn, xh, r, n_ref[...])
        dn_ref[...] += dn
        dhin_ref[...] = dout + dxx
        dkv_ref[...] = dkv_v[BLOCK:, :]
        carry_v[...] = dkv_v[0:BLOCK, :]

    rev = functools.partial(_row_spec, rev_nt=nt)
    row = rev(tm, D_MODEL)
    prev_spec = pl.BlockSpec((BLOCK, 2 * KV_DIM), lambda i: (jnp.maximum((nt - 1 - i) * nb - 1, 0), 0))
    return pl.pallas_call(
        body, name="attn_bwd", grid=(nt,),
        in_specs=[row, row, row, rev(tm, 2 * KV_DIM), prev_spec, row, rev(tm, 128),
                  _const_spec((1, D_MODEL)), SMEM, ANY, ANY],
        out_specs=[row, row, row, rev(tm, 2 * KV_DIM), _const_spec((8, 128)), _const_spec((1, D_MODEL))],
        out_shape=[jax.ShapeDtypeStruct((T, D_MODEL), F32), jax.ShapeDtypeStruct((T, D_MODEL), BF16),
                   jax.ShapeDtypeStruct((T, D_MODEL), BF16), jax.ShapeDtypeStruct((T, 2 * KV_DIM), F32),
                   jax.ShapeDtypeStruct((8, 128), F32), jax.ShapeDtypeStruct((1, D_MODEL), F32)],
        scratch_shapes=[pltpu.VMEM((D_MODEL, D_MODEL), BF16), pltpu.VMEM((D_MODEL, D_MODEL), BF16),
                        pltpu.VMEM((tm + BLOCK, 2 * KV_DIM), BF16), pltpu.VMEM((tm, D_MODEL), F32),
                        pltpu.VMEM((tm, D_MODEL), F32), pltpu.VMEM((tm + BLOCK, 2 * KV_DIM), F32),
                        pltpu.VMEM((BLOCK, 2 * KV_DIM), F32), pltpu.SemaphoreType.DMA((2,))],
        compiler_params=_params(),
    )(dh, h, q, kv, kv, ao, lse, nmix, sinks, w_q, w_o)


def _wgrad(a, b, bn, col_sharded, name, layer=0, n_layers=1, stacked=None):
    T, K = a.shape
    N = b.shape[1]
    tt = min(1024, T)
    nn, ntt = N // bn, T // tt
    kr = K // N_SHARD

    def body(a_ref, b_ref, *rest):
        o_ref = rest[-1]

        @pl.when(pl.program_id(1) == 0)
        def _():
            o_ref[...] = jnp.zeros(o_ref.shape, F32)
        d = _dot_tn(a_ref[...].astype(BF16), b_ref[...].astype(BF16))
        if col_sharded:
            o_ref[...] += d
        else:
            for j in range(N_SHARD):
                o_ref[j] += d[j * kr:(j + 1) * kr]

    if col_sharded:
        assert nn == N_SHARD
        out_spec = pl.BlockSpec((None, None, K, bn), lambda n, t: (n, layer, 0, 0))
        out_shape = jax.ShapeDtypeStruct((N_SHARD, n_layers, K, bn), F32)
    else:
        out_spec = pl.BlockSpec((N_SHARD, None, kr, bn), lambda n, t: (0, layer, 0, n))
        out_shape = jax.ShapeDtypeStruct((N_SHARD, n_layers, kr, N), F32)
    in_specs = [pl.BlockSpec((tt, K), lambda n, t: (t, 0)), pl.BlockSpec((tt, bn), lambda n, t: (t, n))]
    args = [a, b]
    aliases = {}
    if stacked is not None:
        in_specs.append(ANY)
        args.append(stacked)
        aliases = {2: 0}
    return pl.pallas_call(
        body, name=name, grid=(nn, ntt), in_specs=in_specs, out_specs=out_spec, out_shape=out_shape,
        input_output_aliases=aliases,
        compiler_params=pltpu.CompilerParams(dimension_semantics=("arbitrary",) * 2, vmem_limit_bytes=VMEM_LIMIT),
    )(*args)


def _mesh_pos():
    return lax.axis_index("x"), lax.axis_index("y"), lax.axis_index("c")


def _other_chips(x, y):
    return [(1 - x, y), (x, 1 - y), (1 - x, 1 - y)]


HBM_SPEC = pl.BlockSpec(memory_space=pltpu.HBM)
SEM_SPEC = pl.BlockSpec(memory_space=pltpu.SEMAPHORE)


def _split_call(name, bufs, waits=(), starts=(), after=()):
    n, nw, ns, na = len(bufs), len(waits), len(starts), len(after)

    def body(*refs):
        brefs = refs[:n]
        wsems = [(refs[n + 2 * k], refs[n + 2 * k + 1]) for k in range(nw)]
        o = n + 2 * nw + na
        ssems = [(refs[o + 2 * k], refs[o + 2 * k + 1]) for k in range(ns)]
        for (ss, rs), (_, _, fn) in zip(wsems, waits):
            for sending, arriving in fn(brefs, ss, rs):
                sending.wait_send()
                arriving.wait_recv()
        for (ss, rs), (_, fn) in zip(ssems, starts):
            for sending, _ in fn(brefs, ss, rs):
                sending.start()
        if ns:
            token = refs[o + 2 * ns + n]
            token[...] = jnp.zeros(token.shape, token.dtype)

    out_shape, out_specs = [], []
    for cnt, _ in starts:
        out_shape += [pltpu.SemaphoreType.DMA((cnt,)), pltpu.SemaphoreType.DMA((cnt,))]
        out_specs += [SEM_SPEC, SEM_SPEC]
    out_shape += [pltpu.HBM(b.shape, b.dtype) for b in bufs]
    out_specs += [HBM_SPEC] * n
    if ns:
        out_shape.append(jax.ShapeDtypeStruct((8, 128), F32))
        out_specs.append(pl.BlockSpec(memory_space=pltpu.VMEM))
    args = [pltpu.with_memory_space_constraint(b, pltpu.HBM) for b in bufs]
    for ss, rs, _ in waits:
        args += [ss, rs]
    args += list(after)
    res = pl.pallas_call(
        body, name=name, out_shape=tuple(out_shape),
        in_specs=[HBM_SPEC] * n + [SEM_SPEC] * (2 * nw) + [ANY] * na, out_specs=tuple(out_specs),
        input_output_aliases={i: 2 * ns + i for i in range(n)},
        compiler_params=pltpu.CompilerParams(has_side_effects=pltpu.SideEffectType.DATAFLOW_SIDE_EFFECTING),
    )(*args)
    sems = [(res[2 * k], res[2 * k + 1]) for k in range(ns)]
    return list(res[2 * ns:2 * ns + n]), sems, (res[2 * ns + n] if ns else None)


def _cast_place(items, name):
    n = len(items)
    mats = [a.shape[-2:] for a, _, _ in items]

    def body(*refs):
        ins, outs, scr, sem = refs[:n], refs[n:2 * n], refs[2 * n:3 * n], refs[3 * n]
        x, y, _ = _mesh_pos()
        cps = []
        for t in range(n):
            scr[t][...] = ins[t][...].astype(scr[t].dtype)
            cp = pltpu.make_async_copy(scr[t], outs[t].at[2 * x + y], sem.at[t])
            cp.start()
            cps.append(cp)
        for cp in cps:
            cp.wait()

    def spec(idx, shape):
        return pl.BlockSpec((None,) * len(idx) + tuple(shape), lambda i: tuple(idx) + (0, 0))

    return pl.pallas_call(
        body, name=name, grid=(1,),
        in_specs=[spec(idx, mat) for (_, idx, _), mat in zip(items, mats)], out_specs=[ANY] * n,
        out_shape=[jax.ShapeDtypeStruct((N_SHARD,) + tuple(mat), dt) for (_, _, dt), mat in zip(items, mats)],
        scratch_shapes=[pltpu.VMEM(tuple(mat), dt) for (_, _, dt), mat in zip(items, mats)]
        + [pltpu.SemaphoreType.DMA((n,))],
        compiler_params=_params(),
    )(*[a for a, _, _ in items])


def _gather_ici(idx):
    def fn(bufs, ss, rs):
        x, y, c = _mesh_pos()
        pairs = []
        for k, t in enumerate(idx):
            half = bufs[t].shape[1] // 2
            mine = bufs[t].at[2 * x + y, pl.ds(c * half, half), :]
            for j, (cx, cy) in enumerate(_other_chips(x, y)):
                theirs = bufs[t].at[2 * cx + cy, pl.ds(c * half, half), :]
                sem = dict(send_sem=ss.at[3 * k + j], recv_sem=rs.at[3 * k + j],
                           device_id=(cx, cy, c), device_id_type=MESH)
                pairs.append((pltpu.make_async_remote_copy(src_ref=mine, dst_ref=mine, **sem),
                              pltpu.make_async_remote_copy(src_ref=mine, dst_ref=theirs, **sem)))
        return pairs
    return fn


def _gather_d2d(idx):
    def fn(bufs, ss, rs):
        x, y, c = _mesh_pos()
        pairs = []
        for k, t in enumerate(idx):
            half = bufs[t].shape[1] // 2
            for j, (cx, cy) in enumerate(_other_chips(x, y)):
                got = bufs[t].at[2 * cx + cy, pl.ds(c * half, half), :]
                theirs = bufs[t].at[2 * cx + cy, pl.ds((1 - c) * half, half), :]
                sem = dict(send_sem=ss.at[3 * k + j], recv_sem=rs.at[3 * k + j],
                           device_id=(x, y, 1 - c), device_id_type=MESH)
                pairs.append((pltpu.make_async_remote_copy(src_ref=got, dst_ref=got, **sem),
                              pltpu.make_async_remote_copy(src_ref=got, dst_ref=theirs, **sem)))
        return pairs
    return fn


def _sibling_send_halves(grads, small):
    n = len(grads)

    def body(*refs):
        ins, sm_in = refs[:n], refs[n]
        outs, sm_out = refs[n + 1:2 * n + 1], refs[2 * n + 1]
        send_sems, recv_sems = refs[2 * n + 2:]
        x, y, c = _mesh_pos()
        sibling = (x, y, 1 - c)
        cps = []
        for t in range(n):
            half = ins[t].shape[1] // 2
            src = ins[t].at[:, pl.ds((1 - c) * half, half), :]
            cps.append(pltpu.make_async_remote_copy(
                src_ref=src, dst_ref=outs[t], send_sem=send_sems.at[t], recv_sem=recv_sems.at[t],
                device_id=sibling, device_id_type=MESH))
        cps.append(pltpu.make_async_remote_copy(
            src_ref=sm_in, dst_ref=sm_out, send_sem=send_sems.at[n], recv_sem=recv_sems.at[n],
            device_id=sibling, device_id_type=MESH))
        for cp in cps:
            cp.start()
        for cp in cps:
            cp.wait()

    out_shape = [jax.ShapeDtypeStruct((N_SHARD, g.shape[1] // 2, g.shape[2]), F32) for g in grads]
    out_shape.append(jax.ShapeDtypeStruct(small.shape, F32))
    return pl.pallas_call(
        body, name="grads_sibling_send",
        in_specs=[ANY] * (n + 1), out_specs=[ANY] * (n + 1), out_shape=out_shape,
        scratch_shapes=[pltpu.SemaphoreType.DMA((n + 1,)), pltpu.SemaphoreType.DMA((n + 1,))],
    )(*grads, small)


def _row_block(rows, cols, mult=8, limit=3 * 512 * 1024, itemsize=4):
    best = None
    for br in range(mult, rows + 1, mult):
        if rows % br == 0 and br * cols * itemsize <= limit:
            best = br
    assert best is not None, (rows, cols)
    return best


def _chip_partial(g, s, ids, name):
    _, half, cols = s.shape
    br = _row_block(half, cols, mult=16)
    nr = half // br

    def body(ids_ref, g_ref, s_ref, o_ref):
        o_ref[...] = (g_ref[...] + s_ref[...]).astype(BF16)

    return pl.pallas_call(
        body, name=name,
        grid_spec=pltpu.PrefetchScalarGridSpec(
            num_scalar_prefetch=1, grid=(3, nr),
            in_specs=[pl.BlockSpec((None, br, cols), lambda j, r, ids_ref: (ids_ref[2 + j], ids_ref[0] * nr + r, 0)),
                      pl.BlockSpec((None, br, cols), lambda j, r, ids_ref: (ids_ref[2 + j], r, 0))],
            out_specs=pl.BlockSpec((None, br, cols), lambda j, r, ids_ref: (j, r, 0))),
        out_shape=jax.ShapeDtypeStruct((3, half, cols), BF16),
        compiler_params=pltpu.CompilerParams(dimension_semantics=("arbitrary", "arbitrary")),
    )(ids, g, s)


def _chip_exchange(parts, small):
    n = len(parts)

    def body(*refs):
        ins, sm_in = refs[:n], refs[n]
        outs, sm_out = refs[n + 1:2 * n + 1], refs[2 * n + 1]
        send_sems, recv_sems = refs[2 * n + 2:]
        x, y, c = _mesh_pos()
        cps = []
        for j, (cx, cy) in enumerate(_other_chips(x, y)):
            for t in range(n):
                cps.append(pltpu.make_async_remote_copy(
                    src_ref=ins[t].at[j], dst_ref=outs[t].at[j],
                    send_sem=send_sems.at[t, j], recv_sem=recv_sems.at[t, j],
                    device_id=(cx, cy, c), device_id_type=MESH))
            cps.append(pltpu.make_async_remote_copy(
                src_ref=sm_in, dst_ref=sm_out.at[j], send_sem=send_sems.at[n, j], recv_sem=recv_sems.at[n, j],
                device_id=(cx, cy, c), device_id_type=MESH))
        for cp in cps:
            cp.start()
        for cp in cps:
            cp.wait()

    out_shape = [jax.ShapeDtypeStruct(p.shape, p.dtype) for p in parts]
    out_shape.append(jax.ShapeDtypeStruct((3,) + small.shape, F32))
    return pl.pallas_call(
        body, name="grads_chip_exchange",
        in_specs=[ANY] * (n + 1), out_specs=[ANY] * (n + 1), out_shape=out_shape,
        scratch_shapes=[pltpu.SemaphoreType.DMA((n + 1, 3)), pltpu.SemaphoreType.DMA((n + 1, 3))],
    )(*parts, small)


def _chip_sum(g, s, q, ids, name):
    _, half, cols = s.shape
    br = _row_block(half, cols, mult=16)
    nr = half // br

    def body(ids_ref, g_ref, s_ref, q_ref, o_ref):
        own = g_ref[...] + s_ref[...]
        o_ref[...] = (own + q_ref[2].astype(F32)) + (q_ref[0].astype(F32) + q_ref[1].astype(F32))

    return pl.pallas_call(
        body, name=name,
        grid_spec=pltpu.PrefetchScalarGridSpec(
            num_scalar_prefetch=1, grid=(nr,),
            in_specs=[pl.BlockSpec((None, br, cols), lambda r, ids_ref: (ids_ref[1], ids_ref[0] * nr + r, 0)),
                      pl.BlockSpec((None, br, cols), lambda r, ids_ref: (ids_ref[1], r, 0)),
                      pl.BlockSpec((3, br, cols), lambda r, ids_ref: (0, r, 0))],
            out_specs=pl.BlockSpec((br, cols), lambda r, ids_ref: (r, 0))),
        out_shape=jax.ShapeDtypeStruct((half, cols), F32),
        compiler_params=pltpu.CompilerParams(dimension_semantics=("arbitrary",)),
    )(ids, g, s, q)


def _small_sum(part, recv):
    def body(p_ref, q_ref, o_ref):
        o_ref[...] = (p_ref[...] + q_ref[2]) + (q_ref[0] + q_ref[1])

    return pl.pallas_call(body, name="chip_sum_small", out_shape=jax.ShapeDtypeStruct(part.shape, F32))(part, recv)


def _sibling_swap(halves):
    n = len(halves)

    def body(*refs):
        ins, outs = refs[:n], refs[n:2 * n]
        send_sems, recv_sems = refs[2 * n:]
        x, y, c = _mesh_pos()
        cps = [pltpu.make_async_remote_copy(
            src_ref=ins[t], dst_ref=outs[t], send_sem=send_sems.at[t], recv_sem=recv_sems.at[t],
            device_id=(x, y, 1 - c), device_id_type=MESH) for t in range(n)]
        for cp in cps:
            cp.start()
        for cp in cps:
            cp.wait()

    return pl.pallas_call(
        body, name="grads_sibling_swap",
        in_specs=[ANY] * n, out_specs=[ANY] * n,
        out_shape=[jax.ShapeDtypeStruct(h.shape, F32) for h in halves],
        scratch_shapes=[pltpu.SemaphoreType.DMA((n,)), pltpu.SemaphoreType.DMA((n,))],
    )(*halves)


def _adamw_math(w, g, m, v):
    mn = ADAM_B1 * m + (1.0 - ADAM_B1) * g
    vn = ADAM_B2 * v + (1.0 - ADAM_B2) * (g * g)
    m_hat = mn / (1.0 - ADAM_B1 ** ADAM_STEP)
    v_hat = vn / (1.0 - ADAM_B2 ** ADAM_STEP)
    return -ADAM_LR * (m_hat / (jnp.sqrt(v_hat) + ADAM_EPS) + ADAM_WD * w), mn, vn


def _adamw(w, g, m, v, name):
    R, C = w.shape
    br = _row_block(R, C)

    def body(w_ref, g_ref, m_ref, v_ref, d_ref, mo_ref, vo_ref):
        d_ref[...], mo_ref[...], vo_ref[...] = _adamw_math(w_ref[...], g_ref[...], m_ref[...], v_ref[...])

    spec = pl.BlockSpec((br, C), lambda i: (i, 0))
    return pl.pallas_call(
        body, name=name, grid=(R // br,), in_specs=[spec] * 4, out_specs=[spec] * 3,
        out_shape=[jax.ShapeDtypeStruct((R, C), F32)] * 3, compiler_params=_params(),
    )(w, g, m, v)


def _adamw_halves(w, own, sib, m, v, ids, name):
    R, C = w.shape
    half = R // 2
    br = _row_block(half, C)
    nh = half // br

    def body(ids_ref, w_ref, own_ref, sib_ref, m_ref, v_ref, g_ref, d_ref, mo_ref, vo_ref):
        is_own = (pl.program_id(0) // nh) == ids_ref[0]
        g = jnp.where(is_own, own_ref[...], sib_ref[...])
        g_ref[...] = g
        d_ref[...], mo_ref[...], vo_ref[...] = _adamw_math(w_ref[...], g, m_ref[...], v_ref[...])

    full = pl.BlockSpec((br, C), lambda r, ids_ref: (r, 0))
    own_spec = pl.BlockSpec((br, C), lambda r, ids_ref: (jnp.clip(r - ids_ref[0] * nh, 0, nh - 1), 0))
    sib_spec = pl.BlockSpec((br, C), lambda r, ids_ref: (jnp.clip(r - (1 - ids_ref[0]) * nh, 0, nh - 1), 0))
    return pl.pallas_call(
        body, name=name,
        grid_spec=pltpu.PrefetchScalarGridSpec(
            num_scalar_prefetch=1, grid=(2 * nh,),
            in_specs=[full, own_spec, sib_spec, full, full], out_specs=[full] * 4),
        out_shape=[jax.ShapeDtypeStruct((R, C), F32)] * 4, compiler_params=_params(),
    )(ids, w, own, sib, m, v)


_PACK_UNIT = 1024


def _pack(arrs):
    flat = []
    for a in arrs:
        f = a.reshape(-1).astype(F32)
        pad = (-f.shape[0]) % _PACK_UNIT
        if pad:
            f = jnp.concatenate([f, jnp.zeros((pad,), F32)])
        flat.append(f)
    return jnp.concatenate(flat).reshape(-1, 128)


def _unpack(packed, shapes):
    flat = packed.reshape(-1)
    out, off = [], 0
    for shp in shapes:
        size = int(np.prod(shp))
        out.append(flat[off:off + size].reshape(shp))
        off += size + ((-size) % _PACK_UNIT)
    return out


def kernel(x, p, norm_mix, norm_ffn, norm_ple, norm_kv, norm_final, a_w_in, a_norm_v, a_w_s, a_b_s, a_w_out, w_kv, b_w_q, b_sinks, b_w_o, f_w_up, f_conv_w, f_conv_b, f_w_down, ple_w_in, ple_w_gate, ple_b_gate, loss_target, m_norm_mix, m_norm_ffn, m_norm_ple, m_norm_kv, m_norm_final, m_a_w_in, m_a_norm_v, m_a_w_s, m_a_b_s, m_a_w_out, m_w_kv, m_b_w_q, m_b_sinks, m_b_w_o, m_f_w_up, m_f_conv_w, m_f_conv_b, m_f_w_down, m_ple_w_in, m_ple_w_gate, m_ple_b_gate, v_norm_mix, v_norm_ffn, v_norm_ple, v_norm_kv, v_norm_final, v_a_w_in, v_a_norm_v, v_a_w_s, v_a_b_s, v_a_w_out, v_w_kv, v_b_w_q, v_b_sinks, v_b_w_o, v_f_w_up, v_f_conv_w, v_f_conv_b, v_f_w_down, v_ple_w_in, v_ple_w_gate, v_ple_b_gate):
    given = dict(locals())

    small_shard = _pack([a_norm_v, f_conv_w])
    pad_rows = (-small_shard.shape[0]) % 16
    if pad_rows:
        small_shard = jnp.concatenate([small_shard, jnp.zeros((pad_rows, 128), F32)])
    groups = [
        [(a_w_in, (0,), BF16), (a_w_out, (0,), BF16), (small_shard, (), F32)],
        [(f_w_up, (0,), BF16), (f_w_down, (0,), BF16)],
        [(ple_w_in, (0,), BF16), (ple_w_gate, (0,), BF16), (w_kv, (), BF16), (b_w_q, (0,), BF16),
         (b_w_o, (0,), BF16), (f_w_up, (1,), BF16), (f_w_down, (1,), BF16), (ple_w_in, (1,), BF16),
         (ple_w_gate, (1,), BF16)],
    ]
    lands, spans, start = [], [], 0
    for gi, items in enumerate(groups):
        lands += _cast_place(items, f"cast_place_g{gi}")
        spans.append(list(range(start, start + len(items))))
        start += len(items)
    lands, ici_sems, _ = _split_call("gather_start", lands,
                                     starts=[(3 * len(sp), _gather_ici(sp)) for sp in spans])

    def finish_group(gi, after):
        sp = spans[gi]
        local = list(range(len(sp)))
        bufs = [lands[t] for t in sp]
        bufs, d2d_sems, _ = _split_call(f"gather_pass_g{gi}", bufs, waits=[(*ici_sems[gi], _gather_ici(local))],
                                        starts=[(3 * len(sp), _gather_d2d(local))], after=after)
        bufs, _, _ = _split_call(f"gather_done_g{gi}", bufs, waits=[(*d2d_sems[0], _gather_d2d(local))])
        return bufs

    def stage0():
        b_in, b_out, b_small = finish_group(0, ())
        small_full = b_small.reshape(N_SHARD, -1)
        gv_full = small_full[:, :256].reshape(1, D_MODEL)
        cw_full = small_full[:, _PACK_UNIT:_PACK_UNIT + 2 * 3 * FF_BLK].reshape(N_SHARD, 2, 3, FF_BLK)
        cw_full = jnp.transpose(cw_full, (1, 2, 0, 3)).reshape(2, 3, N_FF)
        return gv_full, cw_full, b_in, b_out.reshape(D_MODEL, D_MODEL)

    def stage1(after):
        b_up, b_dn = finish_group(1, after)
        return b_up, b_dn.reshape(D_FF, D_MODEL)

    def stage2(after):
        pin0, gate0, kv_w, wq, wo, up1, dn1, pin1, gate1 = finish_group(2, after)
        sq = lambda a: a.reshape(D_MODEL, -1)
        return dict(w_pin=[pin0, pin1], w_gate=[sq(gate0), sq(gate1)], w_kv=sq(kv_w), w_q=sq(wq), w_o=sq(wo),
                    w_up1=up1, w_dn1=dn1.reshape(D_FF, D_MODEL))

    loss_acc, dx, big_grads, small_grads = _local_step(
        x[0], p[0, 0], p[1, 0], loss_target[0], norm_mix, norm_ffn, norm_ple, norm_kv, norm_final, a_w_s, a_b_s,
        b_sinks, f_conv_b, ple_b_gate, stage0, stage1, stage2)
    return _reduce_and_update(given, dx, loss_acc, big_grads, small_grads)


def _local_step(xs, p0, p1, tgt, norm_mix, norm_ffn, norm_ple, norm_kv, norm_final, a_w_s, a_b_s, b_sinks,
                f_conv_b, ple_b_gate, stage0, stage1, stage2):
    tril = jnp.tril(jnp.ones((CHUNK, CHUNK), F32))
    wsm = (a_w_s[0] * tril[None]).astype(BF16)
    bsb = jnp.broadcast_to(a_b_s[0][:, :, None], (A_GROUPS, CHUNK, CHUNK))
    sinks = b_sinks[0]
    row = lambda a: a.reshape(1, -1)

    gv_full, cw_full, w_in, w_out = stage0()
    h1, zp = _mixer_a_fwd(xs, row(norm_mix[0]), gv_full, wsm, bsb, w_in, w_out)
    w_up0, w_dn0 = stage1((h1,))
    h2, hh0 = _ffn_fwd(h1, row(norm_ffn[0]), cw_full[0], row(f_conv_b[0]), w_up0, w_dn0, 0)
    rest = stage2((h2,))
    w_pin, w_gate, w_kv_f, w_q, w_o = rest['w_pin'], rest['w_gate'], rest['w_kv'], rest['w_q'], rest['w_o']
    w_up = [w_up0, rest['w_up1']]
    w_dn = [w_dn0, rest['w_dn1']]
    h3, pe0, a0, kv = _ple_fwd_kv(h2, p0, row(norm_ple[0]), row(ple_b_gate[0]), row(norm_kv), w_pin[0], w_gate[0], w_kv_f)
    h4, q, ao, lse = _attn_fwd(h3, row(norm_mix[1]), kv, sinks, w_q, w_o)
    h5, hh1 = _ffn_fwd(h4, row(norm_ffn[1]), cw_full[1], row(f_conv_b[1]), w_up[1], w_dn[1], 1)
    dh6, pe1, a1, loss_acc, dn_final = _ple_fwd_final(
        h5, p1, tgt, row(norm_ple[1]), row(ple_b_gate[1]), row(norm_final), w_pin[1], w_gate[1])

    dh5, dpe1, da1, xg1, dbg1, dnple1 = _ple_bwd(dh6, h5, pe1, a1, row(norm_ple[1]), w_gate[1], 1)
    pw = D_MODEL // N_SHARD
    g_pin = _wgrad(p1, dpe1, pw, True, "wgrad_ple_in1", 1, 2)
    g_gate = _wgrad(xg1, da1, D_MODEL // 2, False, "wgrad_ple_gate1", 1, 2)
    dh4, act1, dhh1, xf1, dcw1, dcb1, dnffn1 = _ffn_bwd(
        dh5, h4, hh1, row(norm_ffn[1]), cw_full[1], row(f_conv_b[1]), w_up[1], w_dn[1], 1)
    g_dn = _wgrad(act1, dh5, D_MODEL // 2, False, "wgrad_ffn_down1", 1, 2)
    g_up = _wgrad(xf1, dhh1, FF_BLK, True, "wgrad_ffn_up1", 1, 2)
    dh3a, dq, xn1, dkv, dsink, dnmix1 = _attn_bwd(dh4, h3, q, kv, ao, lse, row(norm_mix[1]), sinks, w_q, w_o)
    g_wo = _wgrad(ao, dh4, D_MODEL // 2, False, "wgrad_attn_o")
    g_wq = _wgrad(xn1, dq, D_MODEL // 2, False, "wgrad_attn_q")
    dh2, dpe0, da0, xg0, dbg0, dnple0, kvn, dnkv = _ple_bwd(
        dh3a, h2, pe0, a0, row(norm_ple[0]), w_gate[0], 0, kv_args=(h3, dkv, row(norm_kv), w_kv_f))
    g_wkv = _wgrad(kvn, dkv, 2 * KV_DIM, False, "wgrad_kv")
    g_pin = _wgrad(p0, dpe0, pw, True, "wgrad_ple_in0", 0, 2, g_pin)
    g_gate = _wgrad(xg0, da0, D_MODEL // 2, False, "wgrad_ple_gate0", 0, 2, g_gate)
    dh1, act0, dhh0, xf0, dcw0, dcb0, dnffn0 = _ffn_bwd(
        dh2, h1, hh0, row(norm_ffn[0]), cw_full[0], row(f_conv_b[0]), w_up[0], w_dn[0], 0)
    g_dn = _wgrad(act0, dh2, D_MODEL // 2, False, "wgrad_ffn_down0", 0, 2, g_dn)
    g_up = _wgrad(xf0, dhh0, FF_BLK, True, "wgrad_ffn_up0", 0, 2, g_up)
    dx, gated, dzp, xn0, dws, dbs, dgv, dnmix0 = _mixer_a_bwd(
        dh1, xs, zp, row(norm_mix[0]), gv_full, wsm, bsb, tril, w_in, w_out)
    g_win = _wgrad(xn0, dzp, 2 * pw, True, "wgrad_a_in")
    g_wout = _wgrad(gated, dh1, D_MODEL // 2, False, "wgrad_a_out")
    big_grads = {'a_w_in': g_win, 'a_w_out': g_wout, 'w_kv': g_wkv, 'b_w_q': g_wq, 'b_w_o': g_wo,
                 'f_w_up': g_up, 'f_w_down': g_dn, 'ple_w_in': g_pin, 'ple_w_gate': g_gate}

    small_grads = {
        'norm_mix': jnp.concatenate([dnmix0, dnmix1]), 'norm_ffn': jnp.concatenate([dnffn0, dnffn1]),
        'norm_ple': jnp.concatenate([dnple0, dnple1]), 'norm_kv': dnkv, 'norm_final': dn_final,
        'a_norm_v': dgv, 'a_w_s': dws, 'a_b_s': dbs[:, :, 0], 'b_sinks': dsink[0, :N_Q_HEADS],
        'f_conv_w': jnp.stack([dcw0, dcw1]), 'f_conv_b': jnp.concatenate([dcb0, dcb1]),
        'ple_b_gate': jnp.concatenate([dbg0, dbg1]),
    }
    return loss_acc, dx, big_grads, small_grads


def _reduce_and_update(given, dx, loss_acc, big_grads, small_grads):
    weight_names = ['norm_mix', 'norm_ffn', 'norm_ple', 'norm_kv', 'norm_final', 'a_w_in', 'a_norm_v', 'a_w_s',
                    'a_b_s', 'a_w_out', 'w_kv', 'b_w_q', 'b_sinks', 'b_w_o', 'f_w_up', 'f_conv_w', 'f_conv_b',
                    'f_w_down', 'ple_w_in', 'ple_w_gate', 'ple_b_gate']
    cx, cy, cc = _mesh_pos()
    shard = 2 * cx + cy
    ids = jnp.stack([cc, shard, shard ^ 2, shard ^ 1, shard ^ 3]).astype(jnp.int32)
    x = given['x']
    big_names = list(big_grads)
    grads4 = [big_grads[k].reshape(N_SHARD, -1, big_grads[k].shape[-1]) for k in big_names]
    small_shapes = {
        'norm_mix': (2, D_MODEL), 'norm_ffn': (2, D_MODEL), 'norm_ple': (2, D_MODEL), 'norm_kv': (D_MODEL,),
        'norm_final': (D_MODEL,), 'a_norm_v': (1, D_MODEL), 'a_w_s': (1, A_GROUPS, CHUNK, CHUNK),
        'a_b_s': (1, A_GROUPS, CHUNK), 'b_sinks': (1, N_Q_HEADS), 'f_conv_w': (2, 3, N_FF),
        'f_conv_b': (2, N_FF), 'ple_b_gate': (2, D_MODEL),
    }
    small_names = list(small_shapes)
    small_g = _pack([small_grads[k] for k in small_names])

    sib = _sibling_send_halves(grads4, small_g)
    parts = [_chip_partial(g, s, ids, f"chip_partial_{k}") for k, g, s in zip(big_names, grads4, sib[:-1])]
    small_chip = _small_add(small_g, sib[-1])
    recv = _chip_exchange(parts, small_chip)
    own_halves = [_chip_sum(g, s, q, ids, f"chip_sum_{k}") for k, g, s, q in zip(big_names, grads4, sib[:-1], recv[:-1])]
    small_red = _small_sum(small_chip, recv[-1])
    sib_halves = _sibling_swap(own_halves)

    out_g, out_d, out_m, out_v = {}, {}, {}, {}
    for name, own, sb in zip(big_names, own_halves, sib_halves):
        w = given[name]
        c2 = w.shape[-1]
        res = _adamw_halves(w.reshape(-1, c2), own, sb, given['m_' + name].reshape(-1, c2),
                            given['v_' + name].reshape(-1, c2), ids, f"adamw_{name}")
        out_g[name], out_d[name], out_m[name], out_v[name] = [r.reshape(w.shape) for r in res]

    full_small = dict(zip(small_names, _unpack(small_red, [small_shapes[k] for k in small_names])))
    local_small = dict(full_small)
    local_small['a_norm_v'] = lax.dynamic_slice_in_dim(full_small['a_norm_v'], shard * 256, 256, axis=1)
    local_small['f_conv_w'] = lax.dynamic_slice_in_dim(full_small['f_conv_w'], shard * FF_BLK, FF_BLK, axis=2)
    sg = _pack([local_small[k] for k in small_names])
    sw = _pack([given[k] for k in small_names])
    sm = _pack([given['m_' + k] for k in small_names])
    sv = _pack([given['v_' + k] for k in small_names])
    sd, smn, svn = _adamw(sw, sg, sm, sv, "adamw_small")
    local_shapes = [given[k].shape for k in small_names]
    for dst, packed in ((out_d, sd), (out_m, smn), (out_v, svn)):
        dst.update(zip(small_names, _unpack(packed, local_shapes)))
    for k in small_names:
        out_g[k] = local_small[k].reshape(given[k].shape)

    loss = lax.psum(loss_acc[0, 0], ("x", "y", "c"))
    grad_x = dx.reshape(x.shape)
    return (loss, grad_x, *[out_g[k] for k in weight_names], *[out_d[k] for k in weight_names],
            *[out_m[k] for k in weight_names], *[out_v[k] for k in weight_names])


def _small_add(a, b):
    def body(a_ref, b_ref, o_ref):
        o_ref[...] = a_ref[...] + b_ref[...]

    return pl.pallas_call(body, name="chip_partial_small", out_shape=jax.ShapeDtypeStruct(a.shape, F32))(a, b)
```

```python
import functools
import math

import numpy as np
import jax
import jax.numpy as jnp
from jax import lax
from jax.experimental import pallas as pl
from jax.experimental.pallas import tpu as pltpu

F32 = jnp.float32
BF16 = jnp.bfloat16

D_MODEL = 1024
CHUNK = 128
A_GROUPS = 8
HEAD_DIM = 64
N_Q_HEADS = 16
N_KV_HEADS = 4
GQA_GROUP = N_Q_HEADS // N_KV_HEADS
KV_DIM = N_KV_HEADS * HEAD_DIM
BLOCK = 128
D_FF = 2816
N_FF = 2 * D_FF
FF_BLK = N_FF // 4
PLE_DIM = 256
EPS = 1e-6
NEG = -1e30
N_SHARD = 4

ADAM_LR = 0.001
ADAM_B1 = 0.9
ADAM_B2 = 0.999
ADAM_EPS = 1e-08
ADAM_WD = 0.01
ADAM_STEP = 10

VMEM_LIMIT = 60 * 1024 * 1024
MESH = pl.DeviceIdType.MESH
ANY = pl.BlockSpec(memory_space=pl.ANY)
SMEM = pl.BlockSpec(memory_space=pltpu.SMEM)

_SLOPES = [float(np.float32(2.0 ** (-8.0 * (h + 1) / N_Q_HEADS))) for h in range(N_Q_HEADS)]


def _dot(a, b):
    return jnp.dot(a, b, preferred_element_type=F32)


def _dot_nt(a, b):
    return lax.dot_general(a, b, (((1,), (1,)), ((), ())), preferred_element_type=F32)


def _dot_tn(a, b):
    return lax.dot_general(a, b, (((0,), (0,)), ((), ())), preferred_element_type=F32)


def _rms(x, g):
    r = lax.rsqrt(jnp.mean(x * x, axis=-1, keepdims=True) + EPS)
    xh = x * r
    return xh * g, xh, r


def _rms_bwd(dy, xh, r, g):
    dxh = dy * g
    dg = jnp.sum(dy * xh, axis=0, keepdims=True)
    dx = r * (dxh - xh * jnp.mean(dxh * xh, axis=-1, keepdims=True))
    return dx, dg


_GELU_C = math.sqrt(2.0 / math.pi)


def _gelu(x):
    t = jnp.tanh(_GELU_C * (x + 0.044715 * (x * x * x)))
    return 0.5 * x * (1.0 + t)


def _gelu_grad(x):
    x2 = x * x
    t = jnp.tanh(_GELU_C * (x + 0.044715 * (x2 * x)))
    return 0.5 * (1.0 + t) + 0.5 * x * (1.0 - t * t) * (_GELU_C * (1.0 + 3.0 * 0.044715 * x2))


def _sigmoid(x):
    return 1.0 / (1.0 + jnp.exp(-x))


def _load_once(pairs, sem):
    @pl.when(pl.program_id(0) == 0)
    def _():
        cps = [pltpu.make_async_copy(s, d, sem.at[i]) for i, (s, d) in enumerate(pairs)]
        for cp in cps:
            cp.start()
        for cp in cps:
            cp.wait()


def _params(n_axes=1, vmem=VMEM_LIMIT):
    return pltpu.CompilerParams(dimension_semantics=("arbitrary",) * n_axes, vmem_limit_bytes=vmem)


def _row_spec(tm, n, rev_nt=None):
    if rev_nt is None:
        return pl.BlockSpec((tm, n), lambda i: (i, 0))
    return pl.BlockSpec((tm, n), lambda i: (rev_nt - 1 - i, 0))


def _const_spec(shape):
    nd = len(shape)
    return pl.BlockSpec(shape, lambda i: (0,) * nd)


def _add_deps(body, in_specs, args, deps):
    nd = len(deps)
    if nd == 0:
        return body, list(in_specs), list(args)

    def wrapped(*refs):
        return body(*refs[nd:])

    return wrapped, [ANY] * nd + list(in_specs), list(deps) + list(args)


def _zero_first(refs):
    @pl.when(pl.program_id(0) == 0)
    def _():
        for r in refs:
            r[...] = jnp.zeros(r.shape, r.dtype)


def _mixer_a_fwd(x, nmix, gv, wsm, bsb, w_in, w_out):
    T = x.shape[0]
    tm = min(512, T)
    nt = T // tm
    nw = 2 * D_MODEL // N_SHARD

    def body(x_ref, nmix_ref, gv_ref, ws_ref, bsb_ref, w_in_hbm, w_out_hbm,
             h1_ref, zp_ref, w_in_v, w_out_v, gated_v, sem):
        _load_once([(w_in_hbm, w_in_v), (w_out_hbm, w_out_v)], sem)
        xv = x_ref[...]
        xn = _rms(xv, nmix_ref[...])[0].astype(BF16)
        for j in range(N_SHARD):
            zp_ref[:, j * nw:(j + 1) * nw] = _dot(xn, w_in_v[j])
        z = _gelu(zp_ref[...])
        u = z[:, :D_MODEL]
        vn = _rms(z[:, D_MODEL:], gv_ref[...])[0].astype(BF16)
        for c in range(tm // CHUNK):
            rows = slice(c * CHUNK, (c + 1) * CHUNK)
            for h in range(A_GROUPS):
                cols = slice(h * CHUNK, (h + 1) * CHUNK)
                s = _dot(ws_ref[h], vn[rows, cols]) + bsb_ref[h]
                gated_v[rows, cols] = (u[rows, cols] * s).astype(BF16)
        h1_ref[...] = xv + _dot(gated_v[...], w_out_v[...])

    return pl.pallas_call(
        body, name="mixer_a_fwd", grid=(nt,),
        in_specs=[_row_spec(tm, D_MODEL), _const_spec((1, D_MODEL)), _const_spec((1, D_MODEL)),
                  _const_spec((A_GROUPS, CHUNK, CHUNK)), _const_spec((A_GROUPS, CHUNK, CHUNK)), ANY, ANY],
        out_specs=[_row_spec(tm, D_MODEL), _row_spec(tm, 2 * D_MODEL)],
        out_shape=[jax.ShapeDtypeStruct((T, D_MODEL), F32), jax.ShapeDtypeStruct((T, 2 * D_MODEL), F32)],
        scratch_shapes=[pltpu.VMEM((N_SHARD, D_MODEL, nw), BF16), pltpu.VMEM((D_MODEL, D_MODEL), BF16),
                        pltpu.VMEM((tm, D_MODEL), BF16), pltpu.SemaphoreType.DMA((2,))],
        compiler_params=_params(),
    )(x, nmix, gv, wsm, bsb, w_in, w_out)


def _mixer_a_bwd(dh, x, zp, nmix, gv, wsm, bsb, tril, w_in, w_out, deps=()):
    T = x.shape[0]
    tm = min(256, T)
    nt = T // tm
    nw = 2 * D_MODEL // N_SHARD

    def body(dh_ref, x_ref, zp_ref, nmix_ref, gv_ref, ws_ref, bsb_ref, tril_ref, w_in_hbm, w_out_hbm,
             dx_ref, gated_ref, dzp_ref, xn_ref, dws_ref, dbs_ref, dgv_ref, dnmix_ref,
             w_in_v, w_out_v, du_v, dvn_v, dbs_v, sem):
        _load_once([(w_in_hbm, w_in_v), (w_out_hbm, w_out_v)], sem)
        _zero_first([dws_ref, dbs_v, dgv_ref, dnmix_ref])
        i = pl.program_id(0)
        dhv = dh_ref[...]
        xv = x_ref[...]
        xn, xh, r = _rms(xv, nmix_ref[...])
        xn_ref[...] = xn.astype(BF16)
        zpv = zp_ref[...]
        z = _gelu(zpv)
        u = z[:, :D_MODEL]
        vn_f, vh, rv = _rms(z[:, D_MODEL:], gv_ref[...])
        vn = vn_f.astype(BF16)
        dgated = _dot_nt(dhv.astype(BF16), w_out_v[...])
        for c in range(tm // CHUNK):
            rows = slice(c * CHUNK, (c + 1) * CHUNK)
            for h in range(A_GROUPS):
                cols = slice(h * CHUNK, (h + 1) * CHUNK)
                vn_h = vn[rows, cols]
                s = _dot(ws_ref[h], vn_h) + bsb_ref[h]
                dgt = dgated[rows, cols]
                u_h = u[rows, cols]
                gated_ref[rows, cols] = (u_h * s).astype(BF16)
                du_v[rows, cols] = dgt * s
                ds = dgt * u_h
                dsb = ds.astype(BF16)
                dws_ref[h] += _dot_nt(dsb, vn_h)
                dbs_v[h] += ds
                dvn_v[rows, cols] = _dot_tn(ws_ref[h], dsb)
        dv, dgv = _rms_bwd(dvn_v[...], vh, rv, gv_ref[...])
        dgv_ref[...] += dgv
        dzu = (du_v[...] * _gelu_grad(zpv[:, :D_MODEL])).astype(BF16)
        dzv = (dv * _gelu_grad(zpv[:, D_MODEL:])).astype(BF16)
        dzp_ref[:, :D_MODEL] = dzu
        dzp_ref[:, D_MODEL:] = dzv
        dxn = _dot_nt(dzu[:, :nw], w_in_v[0]) + _dot_nt(dzu[:, nw:], w_in_v[1])
        dxn += _dot_nt(dzv[:, :nw], w_in_v[2]) + _dot_nt(dzv[:, nw:], w_in_v[3])
        dxx, dn = _rms_bwd(dxn, xh, r, nmix_ref[...])
        dnmix_ref[...] += dn
        dx_ref[...] = dhv + dxx

        @pl.when(i == nt - 1)
        def _():
            for h in range(A_GROUPS):
                dws_ref[h] = dws_ref[h] * tril_ref[...]
                dbs_ref[h] = jnp.broadcast_to(jnp.sum(dbs_v[h], axis=1, keepdims=True), (CHUNK, CHUNK))

    grp = (A_GROUPS, CHUNK, CHUNK)
    body, in_specs, args = _add_deps(
        body, [_row_spec(tm, D_MODEL), _row_spec(tm, D_MODEL), _row_spec(tm, 2 * D_MODEL),
               _const_spec((1, D_MODEL)), _const_spec((1, D_MODEL)), _const_spec(grp), _const_spec(grp),
               _const_spec((CHUNK, CHUNK)), ANY, ANY],
        [dh, x, zp, nmix, gv, wsm, bsb, tril, w_in, w_out], deps)
    return pl.pallas_call(
        body, name="mixer_a_bwd", grid=(nt,), in_specs=in_specs,
        out_specs=[_row_spec(tm, D_MODEL), _row_spec(tm, D_MODEL), _row_spec(tm, 2 * D_MODEL),
                   _row_spec(tm, D_MODEL), _const_spec(grp), _const_spec(grp),
                   _const_spec((1, D_MODEL)), _const_spec((1, D_MODEL))],
        out_shape=[jax.ShapeDtypeStruct((T, D_MODEL), F32), jax.ShapeDtypeStruct((T, D_MODEL), BF16),
                   jax.ShapeDtypeStruct((T, 2 * D_MODEL), BF16), jax.ShapeDtypeStruct((T, D_MODEL), BF16),
                   jax.ShapeDtypeStruct(grp, F32), jax.ShapeDtypeStruct(grp, F32),
                   jax.ShapeDtypeStruct((1, D_MODEL), F32), jax.ShapeDtypeStruct((1, D_MODEL), F32)],
        scratch_shapes=[pltpu.VMEM((N_SHARD, D_MODEL, nw), BF16), pltpu.VMEM((D_MODEL, D_MODEL), BF16),
                        pltpu.VMEM((tm, D_MODEL), F32), pltpu.VMEM((tm, D_MODEL), F32),
                        pltpu.VMEM(grp, F32), pltpu.SemaphoreType.DMA((2,))],
        compiler_params=_params(),
    )(*args)


def _load_ffn_weights(w_up_hbm, w_dn_hbm, layer, w_up_v, w_dn_v, sem):
    _load_once([(w_up_hbm, w_up_v), (w_dn_hbm, w_dn_v)], sem)


def _ffn_fwd(h, nffn, cw, cb, w_up, w_dn, layer):
    T = h.shape[0]
    tm = min(256, T)
    nt = T // tm

    def body(h_ref, n_ref, cw_ref, cb_ref, w_up_hbm, w_dn_hbm, out_ref, hh_ref,
             w_up_v, w_dn_v, carry_v, sem):
        _load_ffn_weights(w_up_hbm, w_dn_hbm, layer, w_up_v, w_dn_v, sem)
        _zero_first([carry_v])
        xv = h_ref[...]
        xf = _rms(xv, n_ref[...])[0].astype(BF16)
        acc = xv
        for j in range(2):
            cs = []
            for blk in (j, j + 2):
                cols = slice(blk * FF_BLK, (blk + 1) * FF_BLK)
                hh = _dot(xf, w_up_v[blk])
                hh_ref[:, cols] = hh.astype(BF16)
                ext = jnp.concatenate([carry_v[blk], hh], axis=0)
                carry_v[blk] = hh[tm - 8:, :]
                s1 = pltpu.roll(ext, 1, 0)[8:]
                s2 = pltpu.roll(ext, 2, 0)[8:]
                cs.append(cb_ref[:, cols] + cw_ref[0:1, cols] * s2 + cw_ref[1:2, cols] * s1
                          + cw_ref[2:3, cols] * hh)
            act = (cs[0] * _sigmoid(cs[0]) * cs[1]).astype(BF16)
            acc = acc + _dot(act, w_dn_v[j * FF_BLK:(j + 1) * FF_BLK, :])
        out_ref[...] = acc

    return pl.pallas_call(
        body, name=f"ffn_fwd{layer}", grid=(nt,),
        in_specs=[_row_spec(tm, D_MODEL), _const_spec((1, D_MODEL)), _const_spec((3, N_FF)),
                  _const_spec((1, N_FF)), ANY, ANY],
        out_specs=[_row_spec(tm, D_MODEL), _row_spec(tm, N_FF)],
        out_shape=[jax.ShapeDtypeStruct((T, D_MODEL), F32), jax.ShapeDtypeStruct((T, N_FF), BF16)],
        scratch_shapes=[pltpu.VMEM((N_SHARD, D_MODEL, FF_BLK), BF16), pltpu.VMEM((D_FF, D_MODEL), BF16),
                        pltpu.VMEM((N_SHARD, 8, FF_BLK), F32), pltpu.SemaphoreType.DMA((2 * N_SHARD,))],
        compiler_params=_params(),
    )(h, nffn, cw, cb, w_up, w_dn)


def _ffn_bwd(dh, h, hh, nffn, cw, cb, w_up, w_dn, layer, deps=()):
    T = h.shape[0]
    tm = min(256, T)
    nt = T // tm
    pv = 16

    def body(dh_ref, h_ref, hh_ref, hhp_ref, n_ref, cw_ref, cb_ref, w_up_hbm, w_dn_hbm,
             dhin_ref, act_ref, dhh_ref, xf_ref, dcw_ref, dcb_ref, dn_ref,
             w_up_v, w_dn_v, carry_v, sem):
        _load_ffn_weights(w_up_hbm, w_dn_hbm, layer, w_up_v, w_dn_v, sem)
        _zero_first([carry_v, dcw_ref, dcb_ref, dn_ref])
        ti = nt - 1 - pl.program_id(0)
        keep_prev = jnp.where(ti > 0, 1.0, 0.0).astype(F32)
        dout = dh_ref[...]
        doutb = dout.astype(BF16)
        xf_f, xh, r = _rms(h_ref[...], n_ref[...])
        xf_ref[...] = xf_f.astype(BF16)
        dxf = jnp.zeros((tm, D_MODEL), F32)
        for j in range(2):
            fw = []
            for blk in (j, j + 2):
                cols = slice(blk * FF_BLK, (blk + 1) * FF_BLK)
                hhv = hh_ref[:, cols].astype(F32)
                prev = hhp_ref[:, cols].astype(F32) * keep_prev
                ext = jnp.concatenate([prev, hhv], axis=0)
                s1 = pltpu.roll(ext, 1, 0)[pv:]
                s2 = pltpu.roll(ext, 2, 0)[pv:]
                cv = (cb_ref[:, cols] + cw_ref[0:1, cols] * s2 + cw_ref[1:2, cols] * s1
                      + cw_ref[2:3, cols] * hhv)
                fw.append((blk, cols, hhv, s1, s2, cv))
            cg, cu = fw[0][5], fw[1][5]
            sg = _sigmoid(cg)
            sil = cg * sg
            act_ref[:, j * FF_BLK:(j + 1) * FF_BLK] = (sil * cu).astype(BF16)
            dact = _dot_nt(doutb, w_dn_v[j * FF_BLK:(j + 1) * FF_BLK, :])
            dcs = (dact * cu * (sg * (1.0 + cg * (1.0 - sg))), dact * sil)
            for (blk, cols, hhv, s1, s2, _), dc in zip(fw, dcs):
                dcb_ref[:, cols] += jnp.sum(dc, axis=0, keepdims=True)
                dcw_ref[0:1, cols] += jnp.sum(dc * s2, axis=0, keepdims=True)
                dcw_ref[1:2, cols] += jnp.sum(dc * s1, axis=0, keepdims=True)
                dcw_ref[2:3, cols] += jnp.sum(dc * hhv, axis=0, keepdims=True)
                ext = jnp.concatenate([dc, carry_v[blk]], axis=0)
                carry_v[blk] = dc[:8, :]
                n = tm + 8
                a1 = pltpu.roll(ext, n - 1, 0)[:tm]
                a2 = pltpu.roll(ext, n - 2, 0)[:tm]
                dhh = (cw_ref[2:3, cols] * dc + cw_ref[1:2, cols] * a1 + cw_ref[0:1, cols] * a2).astype(BF16)
                dhh_ref[:, cols] = dhh
                dxf = dxf + _dot_nt(dhh, w_up_v[blk])
        dxx, dn = _rms_bwd(dxf, xh, r, n_ref[...])
        dn_ref[...] += dn
        dhin_ref[...] = dout + dxx

    rev = functools.partial(_row_spec, rev_nt=nt)
    prev_spec = pl.BlockSpec((pv, N_FF), lambda i: (jnp.maximum((nt - 1 - i) * (tm // pv) - 1, 0), 0))
    body, in_specs, args = _add_deps(
        body, [rev(tm, D_MODEL), rev(tm, D_MODEL), rev(tm, N_FF), prev_spec,
               _const_spec((1, D_MODEL)), _const_spec((3, N_FF)), _const_spec((1, N_FF)), ANY, ANY],
        [dh, h, hh, hh, nffn, cw, cb, w_up, w_dn], deps)
    return pl.pallas_call(
        body, name=f"ffn_bwd{layer}", grid=(nt,), in_specs=in_specs,
        out_specs=[rev(tm, D_MODEL), rev(tm, D_FF), rev(tm, N_FF), rev(tm, D_MODEL),
                   _const_spec((3, N_FF)), _const_spec((1, N_FF)), _const_spec((1, D_MODEL))],
        out_shape=[jax.ShapeDtypeStruct((T, D_MODEL), F32), jax.ShapeDtypeStruct((T, D_FF), BF16),
                   jax.ShapeDtypeStruct((T, N_FF), BF16), jax.ShapeDtypeStruct((T, D_MODEL), BF16),
                   jax.ShapeDtypeStruct((3, N_FF), F32), jax.ShapeDtypeStruct((1, N_FF), F32),
                   jax.ShapeDtypeStruct((1, D_MODEL), F32)],
        scratch_shapes=[pltpu.VMEM((N_SHARD, D_MODEL, FF_BLK), BF16), pltpu.VMEM((D_FF, D_MODEL), BF16),
                        pltpu.VMEM((N_SHARD, 8, FF_BLK), F32), pltpu.SemaphoreType.DMA((2 * N_SHARD,))],
        compiler_params=_params(),
    )(*args)


def _load_ple_weights(w_pin_hbm, w_gate_hbm, layer, w_pin_v, w_gate_v, sem, extra=()):
    _load_once([(w_pin_hbm, w_pin_v), (w_gate_hbm, w_gate_v)] + list(extra), sem)


def _ple_fwd_kv(h, p, nple, bg, nkv, w_pin, w_gate, w_kv):
    T = h.shape[0]
    tm = min(512, T)
    nt = T // tm
    pw = D_MODEL // N_SHARD

    def body(h_ref, p_ref, n_ref, bg_ref, nkv_ref, w_pin_hbm, w_gate_hbm, w_kv_hbm,
             out_ref, pe_ref, a_ref, kv_ref, w_pin_v, w_gate_v, w_kv_v, sem):
        _load_ple_weights(w_pin_hbm, w_gate_hbm, 0, w_pin_v, w_gate_v, sem, [(w_kv_hbm, w_kv_v)])
        xv = h_ref[...]
        xg = _rms(xv, n_ref[...])[0].astype(BF16)
        a = _dot(xg, w_gate_v[...]) + bg_ref[...]
        a_ref[...] = a
        pb = p_ref[...].astype(BF16)
        for j in range(N_SHARD):
            pe_ref[:, j * pw:(j + 1) * pw] = _dot(pb, w_pin_v[j])
        hn = xv + pe_ref[...] * _sigmoid(a)
        out_ref[...] = hn
        kvn = _rms(hn, nkv_ref[...])[0].astype(BF16)
        kv_ref[...] = _dot(kvn, w_kv_v[...]).astype(BF16)

    vec = _const_spec((1, D_MODEL))
    return pl.pallas_call(
        body, name="ple_fwd0", grid=(nt,),
        in_specs=[_row_spec(tm, D_MODEL), _row_spec(tm, PLE_DIM), vec, vec, vec, ANY, ANY, ANY],
        out_specs=[_row_spec(tm, D_MODEL), _row_spec(tm, D_MODEL), _row_spec(tm, D_MODEL),
                   _row_spec(tm, 2 * KV_DIM)],
        out_shape=[jax.ShapeDtypeStruct((T, D_MODEL), F32), jax.ShapeDtypeStruct((T, D_MODEL), F32),
                   jax.ShapeDtypeStruct((T, D_MODEL), F32), jax.ShapeDtypeStruct((T, 2 * KV_DIM), BF16)],
        scratch_shapes=[pltpu.VMEM((N_SHARD, PLE_DIM, pw), BF16), pltpu.VMEM((D_MODEL, D_MODEL), BF16),
                        pltpu.VMEM((D_MODEL, 2 * KV_DIM), BF16), pltpu.SemaphoreType.DMA((2 * N_SHARD + 1,))],
        compiler_params=_params(),
    )(h, p, nple, bg, nkv, w_pin, w_gate, w_kv)


def _ple_fwd_final(h, p, tgt, nple, bg, nfin, w_pin, w_gate):
    T = h.shape[0]
    tm = min(512, T)
    nt = T // tm
    pw = D_MODEL // N_SHARD

    def body(h_ref, p_ref, t_ref, n_ref, bg_ref, nf_ref, w_pin_hbm, w_gate_hbm,
             dh_ref, pe_ref, a_ref, loss_ref, dnf_ref, w_pin_v, w_gate_v, sem):
        _load_ple_weights(w_pin_hbm, w_gate_hbm, 1, w_pin_v, w_gate_v, sem)
        _zero_first([loss_ref, dnf_ref])
        xv = h_ref[...]
        xg = _rms(xv, n_ref[...])[0].astype(BF16)
        a = _dot(xg, w_gate_v[...]) + bg_ref[...]
        a_ref[...] = a
        pb = p_ref[...].astype(BF16)
        for j in range(N_SHARD):
            pe_ref[:, j * pw:(j + 1) * pw] = _dot(pb, w_pin_v[j])
        hn = xv + pe_ref[...] * _sigmoid(a)
        y, yh, r = _rms(hn, nf_ref[...])
        diff = y - t_ref[...]
        loss_ref[...] += 0.5 * jnp.sum(jnp.mean(diff * diff, axis=-1, keepdims=True))
        dy = diff * (1.0 / D_MODEL)
        dhn, dnf = _rms_bwd(dy, yh, r, nf_ref[...])
        dnf_ref[...] += dnf
        dh_ref[...] = dhn

    vec = _const_spec((1, D_MODEL))
    return pl.pallas_call(
        body, name="ple_fwd1", grid=(nt,),
        in_specs=[_row_spec(tm, D_MODEL), _row_spec(tm, PLE_DIM), _row_spec(tm, D_MODEL), vec, vec, vec, ANY, ANY],
        out_specs=[_row_spec(tm, D_MODEL), _row_spec(tm, D_MODEL), _row_spec(tm, D_MODEL),
                   _const_spec((8, 128)), vec],
        out_shape=[jax.ShapeDtypeStruct((T, D_MODEL), F32), jax.ShapeDtypeStruct((T, D_MODEL), F32),
                   jax.ShapeDtypeStruct((T, D_MODEL), F32), jax.ShapeDtypeStruct((8, 128), F32),
                   jax.ShapeDtypeStruct((1, D_MODEL), F32)],
        scratch_shapes=[pltpu.VMEM((N_SHARD, PLE_DIM, pw), BF16), pltpu.VMEM((D_MODEL, D_MODEL), BF16),
                        pltpu.SemaphoreType.DMA((2 * N_SHARD,))],
        compiler_params=_params(),
    )(h, p, tgt, nple, bg, nfin, w_pin, w_gate)


def _ple_bwd(dh, hb, pe, a, nple, w_gate, layer, kv_args=None):
    T = hb.shape[0]
    tm = min(512, T)
    nt = T // tm
    with_kv = kv_args is not None
    rows = D_MODEL // N_SHARD

    def body(*refs):
        if with_kv:
            (dh_ref, hb_ref, pe_ref, a_ref, n_ref, w_gate_hbm, hc_ref, dkv_ref, nkv_ref, w_kv_hbm,
             dhb_ref, dpe_ref, da_ref, xg_ref, dbg_ref, dn_ref, kvn_ref, dnkv_ref,
             w_gate_v, w_kv_v, sem) = refs
        else:
            (dh_ref, hb_ref, pe_ref, a_ref, n_ref, w_gate_hbm,
             dhb_ref, dpe_ref, da_ref, xg_ref, dbg_ref, dn_ref, w_gate_v, sem) = refs
        pairs = [(w_gate_hbm, w_gate_v)]
        if with_kv:
            pairs.append((w_kv_hbm, w_kv_v))
        _load_once(pairs, sem)
        _zero_first([dbg_ref, dn_ref] + ([dnkv_ref] if with_kv else []))
        do = dh_ref[...]
        if with_kv:
            dkvn = _dot_nt(dkv_ref[...].astype(BF16), w_kv_v[...])
            kvn, kh, kr = _rms(hc_ref[...], nkv_ref[...])
            kvn_ref[...] = kvn.astype(BF16)
            dk, dnkv = _rms_bwd(dkvn, kh, kr, nkv_ref[...])
            dnkv_ref[...] += dnkv
            do = do + dk
        gate = _sigmoid(a_ref[...])
        dpe_ref[...] = (do * gate).astype(BF16)
        da = do * pe_ref[...] * (gate * (1.0 - gate))
        dab = da.astype(BF16)
        da_ref[...] = dab
        dbg_ref[...] += jnp.sum(da, axis=0, keepdims=True)
        dxg = _dot_nt(dab, w_gate_v[...])
        xg, xh, r = _rms(hb_ref[...], n_ref[...])
        xg_ref[...] = xg.astype(BF16)
        dxx, dn = _rms_bwd(dxg, xh, r, n_ref[...])
        dn_ref[...] += dn
        dhb_ref[...] = do + dxx

    vec = _const_spec((1, D_MODEL))
    row = _row_spec(tm, D_MODEL)
    in_specs = [row, row, row, row, vec, ANY]
    args = [dh, hb, pe, a, nple, w_gate]
    out_specs = [row, row, row, row, vec, vec]
    out_shape = [jax.ShapeDtypeStruct((T, D_MODEL), F32), jax.ShapeDtypeStruct((T, D_MODEL), BF16),
                 jax.ShapeDtypeStruct((T, D_MODEL), BF16), jax.ShapeDtypeStruct((T, D_MODEL), BF16),
                 jax.ShapeDtypeStruct((1, D_MODEL), F32), jax.ShapeDtypeStruct((1, D_MODEL), F32)]
    scratch = [pltpu.VMEM((D_MODEL, D_MODEL), BF16)]
    if with_kv:
        hc, dkv, nkv, w_kv = kv_args
        in_specs += [row, _row_spec(tm, 2 * KV_DIM), vec, ANY]
        args += [hc, dkv, nkv, w_kv]
        out_specs += [row, vec]
        out_shape += [jax.ShapeDtypeStruct((T, D_MODEL), BF16), jax.ShapeDtypeStruct((1, D_MODEL), F32)]
        scratch.append(pltpu.VMEM((D_MODEL, 2 * KV_DIM), BF16))
    scratch.append(pltpu.SemaphoreType.DMA((N_SHARD + 1,)))
    return pl.pallas_call(
        body, name=f"ple_bwd{layer}", grid=(nt,), in_specs=in_specs, out_specs=out_specs,
        out_shape=out_shape, scratch_shapes=scratch, compiler_params=_params(),
    )(*args)


def _band_masks(is_first):
    ii = lax.broadcasted_iota(jnp.int32, (BLOCK, 2 * BLOCK), 0)
    jj = lax.broadcasted_iota(jnp.int32, (BLOCK, 2 * BLOCK), 1)
    dist = ii + BLOCK - jj
    valid = (dist >= 0) & (dist < BLOCK) & ((jj >= BLOCK) | jnp.logical_not(is_first))
    return dist.astype(F32), valid


def _attn_fwd(h, nmix, kv, sinks, w_q, w_o):
    T = h.shape[0]
    tm = min(512, T)
    nt = T // tm
    nb = tm // BLOCK

    def body(h_ref, n_ref, kv_ref, kvp_ref, sink_ref, w_q_hbm, w_o_hbm,
             out_ref, q_ref, ao_ref, lse_ref, w_q_v, w_o_v, kvs_v, sem):
        _load_once([(w_q_hbm, w_q_v), (w_o_hbm, w_o_v)], sem)
        ti = pl.program_id(0)
        xv = h_ref[...]
        xn = _rms(xv, n_ref[...])[0].astype(BF16)
        q_ref[...] = (_dot(xn, w_q_v[...]) * (HEAD_DIM ** -0.5)).astype(BF16)
        kvs_v[0:BLOCK, :] = kvp_ref[...]
        kvs_v[BLOCK:, :] = kv_ref[...]
        lane = lax.broadcasted_iota(jnp.int32, (BLOCK, 128), 1)

        def blk_body(b, carry):
            r0 = pl.multiple_of(b * BLOCK, BLOCK)
            distf, valid = _band_masks(jnp.logical_and(ti == 0, b == 0))
            qb = q_ref[pl.ds(r0, BLOCK), :]
            band = kvs_v[pl.ds(r0, 2 * BLOCK), :]
            lse_mat = jnp.zeros((BLOCK, 128), F32)
            outs = []
            for hq in range(N_Q_HEADS):
                kh = hq // GQA_GROUP
                k_h = band[:, kh * HEAD_DIM:(kh + 1) * HEAD_DIM]
                v_h = band[:, KV_DIM + kh * HEAD_DIM:KV_DIM + (kh + 1) * HEAD_DIM]
                s = _dot_nt(qb[:, hq * HEAD_DIM:(hq + 1) * HEAD_DIM], k_h) - _SLOPES[hq] * distf
                s = jnp.where(valid, s, NEG)
                sink = sink_ref[hq]
                m = jnp.maximum(jnp.max(s, axis=1, keepdims=True), sink)
                e = jnp.exp(s - m)
                den = jnp.sum(e, axis=1, keepdims=True) + jnp.exp(sink - m)
                outs.append(_dot((e / den).astype(BF16), v_h))
                lse_mat = jnp.where(lane == hq, m + jnp.log(den), lse_mat)
            ao_ref[pl.ds(r0, BLOCK), :] = jnp.concatenate(outs, axis=1).astype(BF16)
            lse_ref[pl.ds(r0, BLOCK), :] = lse_mat
            return carry

        lax.fori_loop(0, nb, blk_body, 0)
        out_ref[...] = xv + _dot(ao_ref[...], w_o_v[...])

    row = _row_spec(tm, D_MODEL)
    prev_spec = pl.BlockSpec((BLOCK, 2 * KV_DIM), lambda i: (jnp.maximum(i * nb - 1, 0), 0))
    return pl.pallas_call(
        body, name="attn_fwd", grid=(nt,),
        in_specs=[row, _const_spec((1, D_MODEL)), _row_spec(tm, 2 * KV_DIM), prev_spec, SMEM, ANY, ANY],
        out_specs=[row, row, row, _row_spec(tm, 128)],
        out_shape=[jax.ShapeDtypeStruct((T, D_MODEL), F32), jax.ShapeDtypeStruct((T, D_MODEL), BF16),
                   jax.ShapeDtypeStruct((T, D_MODEL), BF16), jax.ShapeDtypeStruct((T, 128), F32)],
        scratch_shapes=[pltpu.VMEM((D_MODEL, D_MODEL), BF16), pltpu.VMEM((D_MODEL, D_MODEL), BF16),
                        pltpu.VMEM((tm + BLOCK, 2 * KV_DIM), BF16), pltpu.SemaphoreType.DMA((2,))],
        compiler_params=_params(),
    )(h, nmix, kv, kv, sinks, w_q, w_o)


def _attn_bwd(dh, h, q, kv, ao, lse, nmix, sinks, w_q, w_o):
    T = h.shape[0]
    tm = min(512, T)
    nt = T // tm
    nb = tm // BLOCK

    def body(dh_ref, h_ref, q_ref, kv_ref, kvp_ref, ao_ref, lse_ref, n_ref, sink_ref, w_q_hbm, w_o_hbm,
             dhin_ref, dq_ref, xn_ref, dkv_ref, dsink_ref, dn_ref,
             w_q_v, w_o_v, kvs_v, dao_v, dq_v, dkv_v, carry_v, sem):
        _load_once([(w_q_hbm, w_q_v), (w_o_hbm, w_o_v)], sem)
        _zero_first([carry_v, dsink_ref, dn_ref])
        ti = nt - 1 - pl.program_id(0)
        dout = dh_ref[...]
        dao_v[...] = _dot_nt(dout.astype(BF16), w_o_v[...])
        kvs_v[0:BLOCK, :] = kvp_ref[...]
        kvs_v[BLOCK:, :] = kv_ref[...]
        dkv_v[0:tm, :] = jnp.zeros((tm, 2 * KV_DIM), F32)
        dkv_v[tm:, :] = carry_v[...]
        lane = lax.broadcasted_iota(jnp.int32, (BLOCK, 128), 1)
        lane8 = lax.broadcasted_iota(jnp.int32, (8, 128), 1)

        def blk_body(b, dsk):
            r0 = pl.multiple_of(b * BLOCK, BLOCK)
            distf, valid = _band_masks(jnp.logical_and(ti == 0, b == 0))
            qb = q_ref[pl.ds(r0, BLOCK), :]
            band = kvs_v[pl.ds(r0, 2 * BLOCK), :]
            aob = ao_ref[pl.ds(r0, BLOCK), :].astype(F32)
            daob = dao_v[pl.ds(r0, BLOCK), :]
            lse_mat = lse_ref[pl.ds(r0, BLOCK), :]
            dqs = []
            dks = []
            dvs = []
            for kh in range(N_KV_HEADS):
                k_h = band[:, kh * HEAD_DIM:(kh + 1) * HEAD_DIM]
                v_h = band[:, KV_DIM + kh * HEAD_DIM:KV_DIM + (kh + 1) * HEAD_DIM]
                dk = jnp.zeros((2 * BLOCK, HEAD_DIM), F32)
                dv = jnp.zeros((2 * BLOCK, HEAD_DIM), F32)
                for g in range(GQA_GROUP):
                    hq = kh * GQA_GROUP + g
                    hc = slice(hq * HEAD_DIM, (hq + 1) * HEAD_DIM)
                    q_h = qb[:, hc]
                    s = _dot_nt(q_h, k_h) - _SLOPES[hq] * distf
                    s = jnp.where(valid, s, NEG)
                    lse = jnp.sum(jnp.where(lane == hq, lse_mat, 0.0), axis=1, keepdims=True)
                    pr = jnp.exp(s - lse)
                    dao_h = daob[:, hc]
                    dd = jnp.sum(dao_h * aob[:, hc], axis=1, keepdims=True)
                    dao_hb = dao_h.astype(BF16)
                    dp = _dot_nt(dao_hb, v_h)
                    dsb = (pr * (dp - dd)).astype(BF16)
                    dqs.append(_dot(dsb, k_h) * (HEAD_DIM ** -0.5))
                    dk = dk + _dot_tn(dsb, q_h)
                    dv = dv + _dot_tn(pr.astype(BF16), dao_hb)
                    dsv = -jnp.sum(jnp.exp(sink_ref[hq] - lse) * dd)
                    dsk = dsk + jnp.where(lane8 == hq, dsv, 0.0)
                dks.append(dk)
                dvs.append(dv)
            dq_v[pl.ds(r0, BLOCK), :] = jnp.concatenate(dqs, axis=1)
            dkv_v[pl.ds(r0, 2 * BLOCK), :] += jnp.concatenate(dks + dvs, axis=1)
            return dsk

        dsk = lax.fori_loop(0, nb, blk_body, jnp.zeros((8, 128), F32))
        dsink_ref[...] += dsk
        dqb = dq_v[...].astype(BF16)
        dq_ref[...] = dqb
        dxn = _dot_nt(dqb, w_q_v[...])
        xn, xh, r = _rms(h_ref[...], n_ref[...])
        xn_ref[...] = xn.astype(BF16)
        dxx, dn = _rms_bwd(dxn, xh, r, n_ref[...])
        dn_ref[...] += dn
        dhin_ref[...] = dout + dxx
        dkv_ref[...] = dkv_v[BLOCK:, :]
        carry_v[...] = dkv_v[0:BLOCK, :]

    rev = functools.partial(_row_spec, rev_nt=nt)
    row = rev(tm, D_MODEL)
    prev_spec = pl.BlockSpec((BLOCK, 2 * KV_DIM), lambda i: (jnp.maximum((nt - 1 - i) * nb - 1, 0), 0))
    return pl.pallas_call(
        body, name="attn_bwd", grid=(nt,),
        in_specs=[row, row, row, rev(tm, 2 * KV_DIM), prev_spec, row, rev(tm, 128),
                  _const_spec((1, D_MODEL)), SMEM, ANY, ANY],
        out_specs=[row, row, row, rev(tm, 2 * KV_DIM), _const_spec((8, 128)), _const_spec((1, D_MODEL))],
        out_shape=[jax.ShapeDtypeStruct((T, D_MODEL), F32), jax.ShapeDtypeStruct((T, D_MODEL), BF16),
                   jax.ShapeDtypeStruct((T, D_MODEL), BF16), jax.ShapeDtypeStruct((T, 2 * KV_DIM), F32),
                   jax.ShapeDtypeStruct((8, 128), F32), jax.ShapeDtypeStruct((1, D_MODEL), F32)],
        scratch_shapes=[pltpu.VMEM((D_MODEL, D_MODEL), BF16), pltpu.VMEM((D_MODEL, D_MODEL), BF16),
                        pltpu.VMEM((tm + BLOCK, 2 * KV_DIM), BF16), pltpu.VMEM((tm, D_MODEL), F32),
                        pltpu.VMEM((tm, D_MODEL), F32), pltpu.VMEM((tm + BLOCK, 2 * KV_DIM), F32),
                        pltpu.VMEM((BLOCK, 2 * KV_DIM), F32), pltpu.SemaphoreType.DMA((2,))],
        compiler_params=_params(),
    )(dh, h, q, kv, kv, ao, lse, nmix, sinks, w_q, w_o)


def _wgrad(a, b, bn, col_sharded, name, layer=0, n_layers=1, stacked=None, deps=()):
    T, K = a.shape
    N = b.shape[1]
    tt = min(1024, T)
    nn, ntt = N // bn, T // tt
    kr = K // N_SHARD

    def body(a_ref, b_ref, *rest):
        o_ref = rest[-1]

        @pl.when(pl.program_id(1) == 0)
        def _():
            o_ref[...] = jnp.zeros(o_ref.shape, F32)
        d = _dot_tn(a_ref[...].astype(BF16), b_ref[...].astype(BF16))
        if col_sharded:
            o_ref[...] += d
        else:
            for j in range(N_SHARD):
                o_ref[j] += d[j * kr:(j + 1) * kr]

    if col_sharded:
        assert nn == N_SHARD
        out_spec = pl.BlockSpec((None, None, K, bn), lambda n, t: (n, layer, 0, 0))
        out_shape = jax.ShapeDtypeStruct((N_SHARD, n_layers, K, bn), F32)
    else:
        out_spec = pl.BlockSpec((N_SHARD, None, kr, bn), lambda n, t: (0, layer, 0, n))
        out_shape = jax.ShapeDtypeStruct((N_SHARD, n_layers, kr, N), F32)
    in_specs = [pl.BlockSpec((tt, K), lambda n, t: (t, 0)), pl.BlockSpec((tt, bn), lambda n, t: (t, n))]
    args = [a, b]
    aliases = {}
    if stacked is not None:
        in_specs.append(ANY)
        args.append(stacked)
        aliases = {2 + len(deps): 0}
    body, in_specs, args = _add_deps(body, in_specs, args, deps)
    return pl.pallas_call(
        body, name=name, grid=(nn, ntt), in_specs=in_specs, out_specs=out_spec, out_shape=out_shape,
        input_output_aliases=aliases,
        compiler_params=pltpu.CompilerParams(dimension_semantics=("arbitrary",) * 2, vmem_limit_bytes=VMEM_LIMIT),
    )(*args)


def _mesh_pos():
    return lax.axis_index("x"), lax.axis_index("y"), lax.axis_index("c")


def _other_chips(x, y):
    return [(1 - x, y), (x, 1 - y), (1 - x, 1 - y)]


HBM_SPEC = pl.BlockSpec(memory_space=pltpu.HBM)
SEM_SPEC = pl.BlockSpec(memory_space=pltpu.SEMAPHORE)


def _split_call(name, bufs, waits=(), starts=(), after=()):
    n, nw, ns, na = len(bufs), len(waits), len(starts), len(after)

    def body(*refs):
        brefs = refs[:n]
        wsems = [(refs[n + 2 * k], refs[n + 2 * k + 1]) for k in range(nw)]
        o = n + 2 * nw + na
        ssems = [(refs[o + 2 * k], refs[o + 2 * k + 1]) for k in range(ns)]
        for (ss, rs), (_, _, fn) in zip(wsems, waits):
            for sending, arriving in fn(brefs, ss, rs):
                sending.wait_send()
                arriving.wait_recv()
        for (ss, rs), (_, fn) in zip(ssems, starts):
            for sending, _ in fn(brefs, ss, rs):
                sending.start()
        if ns:
            token = refs[o + 2 * ns + n]
            token[...] = jnp.zeros(token.shape, token.dtype)

    out_shape, out_specs = [], []
    for cnt, _ in starts:
        out_shape += [pltpu.SemaphoreType.DMA((cnt,)), pltpu.SemaphoreType.DMA((cnt,))]
        out_specs += [SEM_SPEC, SEM_SPEC]
    out_shape += [pltpu.HBM(b.shape, b.dtype) for b in bufs]
    out_specs += [HBM_SPEC] * n
    if ns:
        out_shape.append(jax.ShapeDtypeStruct((8, 128), F32))
        out_specs.append(pl.BlockSpec(memory_space=pltpu.VMEM))
    args = [pltpu.with_memory_space_constraint(b, pltpu.HBM) for b in bufs]
    for ss, rs, _ in waits:
        args += [ss, rs]
    args += list(after)
    res = pl.pallas_call(
        body, name=name, out_shape=tuple(out_shape),
        in_specs=[HBM_SPEC] * n + [SEM_SPEC] * (2 * nw) + [ANY] * na, out_specs=tuple(out_specs),
        input_output_aliases={i: 2 * ns + i for i in range(n)},
        compiler_params=pltpu.CompilerParams(has_side_effects=pltpu.SideEffectType.DATAFLOW_SIDE_EFFECTING),
    )(*args)
    sems = [(res[2 * k], res[2 * k + 1]) for k in range(ns)]
    return list(res[2 * ns:2 * ns + n]), sems, (res[2 * ns + n] if ns else None)


def _cast_place(items, name):
    n = len(items)
    mats = [a.shape[-2:] for a, _, _ in items]

    def body(*refs):
        ins, outs, scr, sem = refs[:n], refs[n:2 * n], refs[2 * n:3 * n], refs[3 * n]
        x, y, _ = _mesh_pos()
        cps = []
        for t in range(n):
            scr[t][...] = ins[t][...].astype(scr[t].dtype)
            cp = pltpu.make_async_copy(scr[t], outs[t].at[2 * x + y], sem.at[t])
            cp.start()
            cps.append(cp)
        for cp in cps:
            cp.wait()

    def spec(idx, shape):
        return pl.BlockSpec((None,) * len(idx) + tuple(shape), lambda i: tuple(idx) + (0, 0))

    return pl.pallas_call(
        body, name=name, grid=(1,),
        in_specs=[spec(idx, mat) for (_, idx, _), mat in zip(items, mats)], out_specs=[ANY] * n,
        out_shape=[jax.ShapeDtypeStruct((N_SHARD,) + tuple(mat), dt) for (_, _, dt), mat in zip(items, mats)],
        scratch_shapes=[pltpu.VMEM(tuple(mat), dt) for (_, _, dt), mat in zip(items, mats)]
        + [pltpu.SemaphoreType.DMA((n,))],
        compiler_params=_params(),
    )(*[a for a, _, _ in items])


def _gather_ici(idx):
    def fn(bufs, ss, rs):
        x, y, c = _mesh_pos()
        pairs = []
        for k, t in enumerate(idx):
            half = bufs[t].shape[1] // 2
            mine = bufs[t].at[2 * x + y, pl.ds(c * half, half), :]
            for j, (cx, cy) in enumerate(_other_chips(x, y)):
                theirs = bufs[t].at[2 * cx + cy, pl.ds(c * half, half), :]
                sem = dict(send_sem=ss.at[3 * k + j], recv_sem=rs.at[3 * k + j],
                           device_id=(cx, cy, c), device_id_type=MESH)
                pairs.append((pltpu.make_async_remote_copy(src_ref=mine, dst_ref=mine, **sem),
                              pltpu.make_async_remote_copy(src_ref=mine, dst_ref=theirs, **sem)))
        return pairs
    return fn


def _gather_d2d(idx):
    def fn(bufs, ss, rs):
        x, y, c = _mesh_pos()
        pairs = []
        for k, t in enumerate(idx):
            half = bufs[t].shape[1] // 2
            for j, (cx, cy) in enumerate(_other_chips(x, y)):
                got = bufs[t].at[2 * cx + cy, pl.ds(c * half, half), :]
                theirs = bufs[t].at[2 * cx + cy, pl.ds((1 - c) * half, half), :]
                sem = dict(send_sem=ss.at[3 * k + j], recv_sem=rs.at[3 * k + j],
                           device_id=(x, y, 1 - c), device_id_type=MESH)
                pairs.append((pltpu.make_async_remote_copy(src_ref=got, dst_ref=got, **sem),
                              pltpu.make_async_remote_copy(src_ref=got, dst_ref=theirs, **sem)))
        return pairs
    return fn


def _alloc(shapes, name):
    def body(*refs):
        pass

    return pl.pallas_call(body, name=name, out_specs=[ANY] * len(shapes),
                          out_shape=[jax.ShapeDtypeStruct(s, d) for s, d in shapes])()


def _send_to_sibling(n):
    def fn(bufs, ss, rs):
        x, y, c = _mesh_pos()
        pairs = []
        for t in range(n):
            src = bufs[t]
            if len(src.shape) == 3:
                half = src.shape[1] // 2
                src = src.at[:, pl.ds((1 - c) * half, half), :]
            cp = pltpu.make_async_remote_copy(src_ref=src, dst_ref=bufs[n + t], send_sem=ss.at[t],
                                              recv_sem=rs.at[t], device_id=(x, y, 1 - c), device_id_type=MESH)
            pairs.append((cp, cp))
        return pairs
    return fn


def _send_to_chips(n):
    def fn(bufs, ss, rs):
        x, y, c = _mesh_pos()
        pairs = []
        for j, (cx, cy) in enumerate(_other_chips(x, y)):
            for t in range(n):
                src = bufs[t].at[j] if len(bufs[t].shape) == 3 else bufs[t]
                cp = pltpu.make_async_remote_copy(src_ref=src, dst_ref=bufs[n + t].at[j], send_sem=ss.at[3 * t + j],
                                                  recv_sem=rs.at[3 * t + j], device_id=(cx, cy, c),
                                                  device_id_type=MESH)
                pairs.append((cp, cp))
        return pairs
    return fn


class _Exchange:
    def __init__(self, name, srcs, land_shapes, fn, n_sems):
        self.name, self.fn = name, fn
        lands = _alloc(land_shapes, name + "_alloc")
        self.n = len(srcs)
        self.bufs, sems, self.token = _split_call(name + "_start", list(srcs) + list(lands),
                                                  starts=[(n_sems, fn)])
        self.sems = sems[0]

    def finish(self, after=()):
        bufs, _, _ = _split_call(self.name + "_wait", self.bufs, waits=[(*self.sems, self.fn)], after=after)
        return bufs[:self.n], bufs[self.n:]


def _row_block(rows, cols, mult=8, limit=3 * 512 * 1024, itemsize=4):
    best = None
    for br in range(mult, rows + 1, mult):
        if rows % br == 0 and br * cols * itemsize <= limit:
            best = br
    assert best is not None, (rows, cols)
    return best


def _chip_partial(g, s, ids, name):
    _, half, cols = s.shape
    br = _row_block(half, cols, mult=16)
    nr = half // br

    def body(ids_ref, g_ref, s_ref, o_ref):
        o_ref[...] = (g_ref[...] + s_ref[...]).astype(BF16)

    return pl.pallas_call(
        body, name=name,
        grid_spec=pltpu.PrefetchScalarGridSpec(
            num_scalar_prefetch=1, grid=(3, nr),
            in_specs=[pl.BlockSpec((None, br, cols), lambda j, r, ids_ref: (ids_ref[2 + j], ids_ref[0] * nr + r, 0)),
                      pl.BlockSpec((None, br, cols), lambda j, r, ids_ref: (ids_ref[2 + j], r, 0))],
            out_specs=pl.BlockSpec((None, br, cols), lambda j, r, ids_ref: (j, r, 0))),
        out_shape=jax.ShapeDtypeStruct((3, half, cols), BF16),
        compiler_params=pltpu.CompilerParams(dimension_semantics=("arbitrary", "arbitrary")),
    )(ids, g, s)


def _chip_sum(g, s, q, ids, name):
    _, half, cols = s.shape
    br = _row_block(half, cols, mult=16)
    nr = half // br

    def body(ids_ref, g_ref, s_ref, q_ref, o_ref):
        own = g_ref[...] + s_ref[...]
        o_ref[...] = (own + q_ref[2].astype(F32)) + (q_ref[0].astype(F32) + q_ref[1].astype(F32))

    return pl.pallas_call(
        body, name=name,
        grid_spec=pltpu.PrefetchScalarGridSpec(
            num_scalar_prefetch=1, grid=(nr,),
            in_specs=[pl.BlockSpec((None, br, cols), lambda r, ids_ref: (ids_ref[1], ids_ref[0] * nr + r, 0)),
                      pl.BlockSpec((None, br, cols), lambda r, ids_ref: (ids_ref[1], r, 0)),
                      pl.BlockSpec((3, br, cols), lambda r, ids_ref: (0, r, 0))],
            out_specs=pl.BlockSpec((br, cols), lambda r, ids_ref: (r, 0))),
        out_shape=jax.ShapeDtypeStruct((half, cols), F32),
        compiler_params=pltpu.CompilerParams(dimension_semantics=("arbitrary",)),
    )(ids, g, s, q)


def _small_sum(part, recv):
    def body(p_ref, q_ref, o_ref):
        o_ref[...] = (p_ref[...] + q_ref[2]) + (q_ref[0] + q_ref[1])

    return pl.pallas_call(body, name="chip_sum_small", out_shape=jax.ShapeDtypeStruct(part.shape, F32))(part, recv)


def _adamw_math(w, g, m, v):
    mn = ADAM_B1 * m + (1.0 - ADAM_B1) * g
    vn = ADAM_B2 * v + (1.0 - ADAM_B2) * (g * g)
    m_hat = mn / (1.0 - ADAM_B1 ** ADAM_STEP)
    v_hat = vn / (1.0 - ADAM_B2 ** ADAM_STEP)
    return -ADAM_LR * (m_hat / (jnp.sqrt(v_hat) + ADAM_EPS) + ADAM_WD * w), mn, vn


def _adamw(w, g, m, v, name):
    R, C = w.shape
    br = _row_block(R, C)

    def body(w_ref, g_ref, m_ref, v_ref, d_ref, mo_ref, vo_ref):
        d_ref[...], mo_ref[...], vo_ref[...] = _adamw_math(w_ref[...], g_ref[...], m_ref[...], v_ref[...])

    spec = pl.BlockSpec((br, C), lambda i: (i, 0))
    return pl.pallas_call(
        body, name=name, grid=(R // br,), in_specs=[spec] * 4, out_specs=[spec] * 3,
        out_shape=[jax.ShapeDtypeStruct((R, C), F32)] * 3, compiler_params=_params(),
    )(w, g, m, v)


def _adamw_halves(w, own, sib, m, v, ids, name, layer=0, n_layers=1, stacked=None):
    C = w.shape[1]
    R = w.shape[0] // n_layers
    half = R // 2
    br = _row_block(half, C)
    nh = half // br
    base = layer * 2 * nh

    def body(ids_ref, w_ref, own_ref, sib_ref, m_ref, v_ref, *rest):
        g_ref, d_ref, mo_ref, vo_ref = rest[-4:]
        is_own = (pl.program_id(0) // nh) == ids_ref[0]
        g = jnp.where(is_own, own_ref[...], sib_ref[...])
        g_ref[...] = g
        d_ref[...], mo_ref[...], vo_ref[...] = _adamw_math(w_ref[...], g, m_ref[...], v_ref[...])

    full = pl.BlockSpec((br, C), lambda r, ids_ref: (base + r, 0))
    own_spec = pl.BlockSpec((br, C), lambda r, ids_ref: (jnp.clip(r - ids_ref[0] * nh, 0, nh - 1), 0))
    sib_spec = pl.BlockSpec((br, C), lambda r, ids_ref: (jnp.clip(r - (1 - ids_ref[0]) * nh, 0, nh - 1), 0))
    in_specs = [full, own_spec, sib_spec, full, full]
    args = [ids, w, own, sib, m, v]
    aliases = {}
    if stacked is not None:
        in_specs += [ANY] * 4
        args += list(stacked)
        aliases = {6 + k: k for k in range(4)}
    return pl.pallas_call(
        body, name=name,
        grid_spec=pltpu.PrefetchScalarGridSpec(
            num_scalar_prefetch=1, grid=(2 * nh,), in_specs=in_specs, out_specs=[full] * 4),
        out_shape=[jax.ShapeDtypeStruct(w.shape, F32)] * 4, input_output_aliases=aliases,
        compiler_params=_params(),
    )(*args)


_PACK_UNIT = 1024


def _pack(arrs):
    flat = []
    for a in arrs:
        f = a.reshape(-1).astype(F32)
        pad = (-f.shape[0]) % _PACK_UNIT
        if pad:
            f = jnp.concatenate([f, jnp.zeros((pad,), F32)])
        flat.append(f)
    return jnp.concatenate(flat).reshape(-1, 128)


def _unpack(packed, shapes):
    flat = packed.reshape(-1)
    out, off = [], 0
    for shp in shapes:
        size = int(np.prod(shp))
        out.append(flat[off:off + size].reshape(shp))
        off += size + ((-size) % _PACK_UNIT)
    return out


def kernel(x, p, norm_mix, norm_ffn, norm_ple, norm_kv, norm_final, a_w_in, a_norm_v, a_w_s, a_b_s, a_w_out, w_kv, b_w_q, b_sinks, b_w_o, f_w_up, f_conv_w, f_conv_b, f_w_down, ple_w_in, ple_w_gate, ple_b_gate, loss_target, m_norm_mix, m_norm_ffn, m_norm_ple, m_norm_kv, m_norm_final, m_a_w_in, m_a_norm_v, m_a_w_s, m_a_b_s, m_a_w_out, m_w_kv, m_b_w_q, m_b_sinks, m_b_w_o, m_f_w_up, m_f_conv_w, m_f_conv_b, m_f_w_down, m_ple_w_in, m_ple_w_gate, m_ple_b_gate, v_norm_mix, v_norm_ffn, v_norm_ple, v_norm_kv, v_norm_final, v_a_w_in, v_a_norm_v, v_a_w_s, v_a_b_s, v_a_w_out, v_w_kv, v_b_w_q, v_b_sinks, v_b_w_o, v_f_w_up, v_f_conv_w, v_f_conv_b, v_f_w_down, v_ple_w_in, v_ple_w_gate, v_ple_b_gate):
    given = dict(locals())

    small_shard = _pack([a_norm_v, f_conv_w])
    pad_rows = (-small_shard.shape[0]) % 16
    if pad_rows:
        small_shard = jnp.concatenate([small_shard, jnp.zeros((pad_rows, 128), F32)])
    groups = [
        [(a_w_in, (0,), BF16), (a_w_out, (0,), BF16), (small_shard, (), F32)],
        [(f_w_up, (0,), BF16), (f_w_down, (0,), BF16)],
        [(ple_w_in, (0,), BF16), (ple_w_gate, (0,), BF16), (w_kv, (), BF16), (b_w_q, (0,), BF16),
         (b_w_o, (0,), BF16), (f_w_up, (1,), BF16), (f_w_down, (1,), BF16), (ple_w_in, (1,), BF16),
         (ple_w_gate, (1,), BF16)],
    ]
    lands, spans, start = [], [], 0
    for gi, items in enumerate(groups):
        lands += _cast_place(items, f"cast_place_g{gi}")
        spans.append(list(range(start, start + len(items))))
        start += len(items)
    lands, ici_sems, _ = _split_call("gather_start", lands,
                                     starts=[(3 * len(sp), _gather_ici(sp)) for sp in spans])

    def finish_group(gi, after):
        sp = spans[gi]
        local = list(range(len(sp)))
        bufs = [lands[t] for t in sp]
        bufs, d2d_sems, _ = _split_call(f"gather_pass_g{gi}", bufs, waits=[(*ici_sems[gi], _gather_ici(local))],
                                        starts=[(3 * len(sp), _gather_d2d(local))], after=after)
        bufs, _, _ = _split_call(f"gather_done_g{gi}", bufs, waits=[(*d2d_sems[0], _gather_d2d(local))])
        return bufs

    def stage0():
        b_in, b_out, b_small = finish_group(0, ())
        small_full = b_small.reshape(N_SHARD, -1)
        gv_full = small_full[:, :256].reshape(1, D_MODEL)
        cw_full = small_full[:, _PACK_UNIT:_PACK_UNIT + 2 * 3 * FF_BLK].reshape(N_SHARD, 2, 3, FF_BLK)
        cw_full = jnp.transpose(cw_full, (1, 2, 0, 3)).reshape(2, 3, N_FF)
        return gv_full, cw_full, b_in, b_out.reshape(D_MODEL, D_MODEL)

    def stage1(after):
        b_up, b_dn = finish_group(1, after)
        return b_up, b_dn.reshape(D_FF, D_MODEL)

    def stage2(after):
        pin0, gate0, kv_w, wq, wo, up1, dn1, pin1, gate1 = finish_group(2, after)
        sq = lambda a: a.reshape(D_MODEL, -1)
        return dict(w_pin=[pin0, pin1], w_gate=[sq(gate0), sq(gate1)], w_kv=sq(kv_w), w_q=sq(wq), w_o=sq(wo),
                    w_up1=up1, w_dn1=dn1.reshape(D_FF, D_MODEL))

    loss_acc, dx, (out_g, out_d, out_m, out_v) = _local_step(
        x[0], p[0, 0], p[1, 0], loss_target[0], norm_mix, norm_ffn, norm_ple, norm_kv, norm_final, a_w_s, a_b_s,
        b_sinks, f_conv_b, ple_b_gate, stage0, stage1, stage2, _Reducer(given))
    weight_names = ['norm_mix', 'norm_ffn', 'norm_ple', 'norm_kv', 'norm_final', 'a_w_in', 'a_norm_v', 'a_w_s',
                    'a_b_s', 'a_w_out', 'w_kv', 'b_w_q', 'b_sinks', 'b_w_o', 'f_w_up', 'f_conv_w', 'f_conv_b',
                    'f_w_down', 'ple_w_in', 'ple_w_gate', 'ple_b_gate']
    loss = lax.psum(loss_acc[0, 0], ("x", "y", "c"))
    return (loss, dx.reshape(x.shape), *[out_g[k] for k in weight_names], *[out_d[k] for k in weight_names],
            *[out_m[k] for k in weight_names], *[out_v[k] for k in weight_names])


def _local_step(xs, p0, p1, tgt, norm_mix, norm_ffn, norm_ple, norm_kv, norm_final, a_w_s, a_b_s, b_sinks,
                f_conv_b, ple_b_gate, stage0, stage1, stage2, sched):
    tril = jnp.tril(jnp.ones((CHUNK, CHUNK), F32))
    wsm = (a_w_s[0] * tril[None]).astype(BF16)
    bsb = jnp.broadcast_to(a_b_s[0][:, :, None], (A_GROUPS, CHUNK, CHUNK))
    sinks = b_sinks[0]
    row = lambda a: a.reshape(1, -1)

    gv_full, cw_full, w_in, w_out = stage0()
    h1, zp = _mixer_a_fwd(xs, row(norm_mix[0]), gv_full, wsm, bsb, w_in, w_out)
    w_up0, w_dn0 = stage1((h1,))
    h2, hh0 = _ffn_fwd(h1, row(norm_ffn[0]), cw_full[0], row(f_conv_b[0]), w_up0, w_dn0, 0)
    rest = stage2((h2,))
    w_pin, w_gate, w_kv_f, w_q, w_o = rest['w_pin'], rest['w_gate'], rest['w_kv'], rest['w_q'], rest['w_o']
    w_up = [w_up0, rest['w_up1']]
    w_dn = [w_dn0, rest['w_dn1']]
    h3, pe0, a0, kv = _ple_fwd_kv(h2, p0, row(norm_ple[0]), row(ple_b_gate[0]), row(norm_kv), w_pin[0], w_gate[0], w_kv_f)
    h4, q, ao, lse = _attn_fwd(h3, row(norm_mix[1]), kv, sinks, w_q, w_o)
    h5, hh1 = _ffn_fwd(h4, row(norm_ffn[1]), cw_full[1], row(f_conv_b[1]), w_up[1], w_dn[1], 1)
    dh6, pe1, a1, loss_acc, dn_final = _ple_fwd_final(
        h5, p1, tgt, row(norm_ple[1]), row(ple_b_gate[1]), row(norm_final), w_pin[1], w_gate[1])

    dh5, dpe1, da1, xg1, dbg1, dnple1 = _ple_bwd(dh6, h5, pe1, a1, row(norm_ple[1]), w_gate[1], 1)
    pw = D_MODEL // N_SHARD
    hd = D_MODEL // 2

    def wgrad(a, b, bn, col_sharded, name, deps=()):
        g = _wgrad(a, b, bn, col_sharded, name, deps=deps)
        return g.reshape(N_SHARD, -1, g.shape[-1])

    early = {('ple_w_in', 1): wgrad(p1, dpe1, pw, True, "wgrad_ple_in1"),
             ('ple_w_gate', 1): wgrad(xg1, da1, hd, False, "wgrad_ple_gate1")}
    dh4, act1, dhh1, xf1, dcw1, dcb1, dnffn1 = _ffn_bwd(
        dh5, h4, hh1, row(norm_ffn[1]), cw_full[1], row(f_conv_b[1]), w_up[1], w_dn[1], 1)
    early['f_w_down', 1] = wgrad(act1, dh5, hd, False, "wgrad_ffn_down1")
    early['f_w_up', 1] = wgrad(xf1, dhh1, FF_BLK, True, "wgrad_ffn_up1")
    dh3a, dq, xn1, dkv, dsink, dnmix1 = _attn_bwd(dh4, h3, q, kv, ao, lse, row(norm_mix[1]), sinks, w_q, w_o)
    early['b_w_o', 0] = wgrad(ao, dh4, hd, False, "wgrad_attn_o")
    early['b_w_q', 0] = wgrad(xn1, dq, hd, False, "wgrad_attn_q")
    dh2, dpe0, da0, xg0, dbg0, dnple0, kvn, dnkv = _ple_bwd(
        dh3a, h2, pe0, a0, row(norm_ple[0]), w_gate[0], 0, kv_args=(h3, dkv, row(norm_kv), w_kv_f))
    early['w_kv', 0] = wgrad(kvn, dkv, 2 * KV_DIM, False, "wgrad_kv")
    early['ple_w_in', 0] = wgrad(p0, dpe0, pw, True, "wgrad_ple_in0")
    early['ple_w_gate', 0] = wgrad(xg0, da0, hd, False, "wgrad_ple_gate0")
    deps = sched.early_ready(early)
    dh1, act0, dhh0, xf0, dcw0, dcb0, dnffn0 = _ffn_bwd(
        dh2, h1, hh0, row(norm_ffn[0]), cw_full[0], row(f_conv_b[0]), w_up[0], w_dn[0], 0, deps=deps)
    deps = sched.after_ffn_bwd0((dh1,))
    g_dn0 = wgrad(act0, dh2, hd, False, "wgrad_ffn_down0", deps=deps)
    g_up0 = wgrad(xf0, dhh0, FF_BLK, True, "wgrad_ffn_up0")
    deps = sched.ffn0_ready({('f_w_down', 0): g_dn0, ('f_w_up', 0): g_up0})
    dx, gated, dzp, xn0, dws, dbs, dgv, dnmix0 = _mixer_a_bwd(
        dh1, xs, zp, row(norm_mix[0]), gv_full, wsm, bsb, tril, w_in, w_out, deps=deps)
    deps = sched.after_mixer_bwd((dx,))
    g_win = wgrad(xn0, dzp, 2 * pw, True, "wgrad_a_in", deps=deps)
    g_wout = wgrad(gated, dh1, hd, False, "wgrad_a_out")

    small_grads = {
        'norm_mix': jnp.concatenate([dnmix0, dnmix1]), 'norm_ffn': jnp.concatenate([dnffn0, dnffn1]),
        'norm_ple': jnp.concatenate([dnple0, dnple1]), 'norm_kv': dnkv, 'norm_final': dn_final,
        'a_norm_v': dgv, 'a_w_s': dws, 'a_b_s': dbs[:, :, 0], 'b_sinks': dsink[0, :N_Q_HEADS],
        'f_conv_w': jnp.stack([dcw0, dcw1]), 'f_conv_b': jnp.concatenate([dcb0, dcb1]),
        'ple_b_gate': jnp.concatenate([dbg0, dbg1]),
    }
    outs = sched.finish({('a_w_in', 0): g_win, ('a_w_out', 0): g_wout}, small_grads, (g_wout,))
    return loss_acc, dx, outs


_SMALL_SHAPES = {
    'norm_mix': (2, D_MODEL), 'norm_ffn': (2, D_MODEL), 'norm_ple': (2, D_MODEL), 'norm_kv': (D_MODEL,),
    'norm_final': (D_MODEL,), 'a_norm_v': (1, D_MODEL), 'a_w_s': (1, A_GROUPS, CHUNK, CHUNK),
    'a_b_s': (1, A_GROUPS, CHUNK), 'b_sinks': (1, N_Q_HEADS), 'f_conv_w': (2, 3, N_FF),
    'f_conv_b': (2, N_FF), 'ple_b_gate': (2, D_MODEL),
}


class _Reducer:
    def __init__(self, given):
        self.given = given
        cx, cy, cc = _mesh_pos()
        self.shard = 2 * cx + cy
        s = self.shard
        self.ids = jnp.stack([cc, s, s ^ 2, s ^ 1, s ^ 3]).astype(jnp.int32)
        self.out = [{}, {}, {}, {}]
        self.stacked = {}

    def _send(self, tag, grads, small=None):
        keys = list(grads)
        srcs = [grads[k] for k in keys]
        shapes = [((N_SHARD, g.shape[1] // 2, g.shape[2]), F32) for g in srcs]
        if small is not None:
            srcs.append(small)
            shapes.append((small.shape, F32))
        return keys, _Exchange(f"send_{tag}", srcs, shapes, _send_to_sibling(len(srcs)), len(srcs))

    def _exchange(self, tag, keys, send, after, with_small=False):
        srcs, lands = send.finish(after)
        n = len(keys)
        parts = [_chip_partial(g, s, self.ids, f"chip_partial_{k[0]}{k[1]}")
                 for k, g, s in zip(keys, srcs[:n], lands[:n])]
        shapes = [(p.shape, BF16) for p in parts]
        if with_small:
            parts.append(_small_add(srcs[n], lands[n]))
            shapes.append(((3,) + parts[-1].shape, F32))
        exch = _Exchange(f"exch_{tag}", parts, shapes, _send_to_chips(len(parts)), 3 * len(parts))
        return (keys, srcs[:n], lands[:n], exch)

    def _swap(self, tag, state, after, with_small=False):
        keys, grads, sib, exch = state
        parts, recv = exch.finish(after)
        n = len(keys)
        own = [_chip_sum(g, s, q, self.ids, f"chip_sum_{k[0]}{k[1]}") for k, g, s, q in zip(keys, grads, sib, recv[:n])]
        small_red = _small_sum(parts[n], recv[n]) if with_small else None
        return keys, _Exchange(f"swap_{tag}", own, [(o.shape, F32) for o in own], _send_to_sibling(n), n), small_red

    def _adamw(self, keys, swap, after):
        own, sib = swap.finish(after)
        last = None
        for (name, layer), o, s in zip(keys, own, sib):
            w = self.given[name]
            n_layers = w.shape[0] if w.ndim == 3 else 1
            c2 = w.shape[-1]
            res = _adamw_halves(w.reshape(-1, c2), o, s, self.given['m_' + name].reshape(-1, c2),
                                self.given['v_' + name].reshape(-1, c2), self.ids, f"adamw_{name}{layer}",
                                layer, n_layers, self.stacked.get(name))
            self.stacked[name] = res
            if layer == 0:
                for dst, r in zip(self.out, res):
                    dst[name] = r.reshape(w.shape)
            last = res[0]
        return last

    def early_ready(self, grads):
        self.e_keys, self.e_send = self._send("e", grads)
        return (self.e_send.token,)

    def after_ffn_bwd0(self, after):
        self.e_state = self._exchange("e", self.e_keys, self.e_send, after)
        return (self.e_state[3].token,)

    def ffn0_ready(self, grads):
        self.f_keys, self.f_send = self._send("f", grads)
        _, self.e_swap, _ = self._swap("e", self.e_state, tuple(grads.values())[-1:])
        return (self.f_send.token, self.e_swap.token)

    def after_mixer_bwd(self, after):
        self.f_state = self._exchange("f", self.f_keys, self.f_send, after)
        self._adamw(self.e_keys, self.e_swap, (self.f_state[3].token,))
        return (self.f_state[3].token,)

    def finish(self, grads, small_grads, after):
        small_names = list(_SMALL_SHAPES)
        small_g = _pack([small_grads[k] for k in small_names])
        _, f_swap, _ = self._swap("f", self.f_state, after)
        a_keys, a_send = self._send("a", grads, small_g)
        a_state = self._exchange("a", a_keys, a_send, (), with_small=True)
        f_done = self._adamw(self.f_keys, f_swap, (a_state[3].token,))
        _, a_swap, small_red = self._swap("a", a_state, (f_done,), with_small=True)
        self._adamw(a_keys, a_swap, ())

        given, shard = self.given, self.shard
        out_g, out_d, out_m, out_v = self.out
        full_small = dict(zip(small_names, _unpack(small_red, [_SMALL_SHAPES[k] for k in small_names])))
        local_small = dict(full_small)
        local_small['a_norm_v'] = lax.dynamic_slice_in_dim(full_small['a_norm_v'], shard * 256, 256, axis=1)
        local_small['f_conv_w'] = lax.dynamic_slice_in_dim(full_small['f_conv_w'], shard * FF_BLK, FF_BLK, axis=2)
        sg = _pack([local_small[k] for k in small_names])
        sw = _pack([given[k] for k in small_names])
        sm = _pack([given['m_' + k] for k in small_names])
        sv = _pack([given['v_' + k] for k in small_names])
        sd, smn, svn = _adamw(sw, sg, sm, sv, "adamw_small")
        local_shapes = [given[k].shape for k in small_names]
        for dst, packed in ((out_d, sd), (out_m, smn), (out_v, svn)):
            dst.update(zip(small_names, _unpack(packed, local_shapes)))
        for k in small_names:
            out_g[k] = local_small[k].reshape(given[k].shape)
        return self.out


def _small_add(a, b):
    def body(a_ref, b_ref, o_ref):
        o_ref[...] = a_ref[...] + b_ref[...]

    return pl.pallas_call(body, name="chip_partial_small", out_shape=jax.ShapeDtypeStruct(a.shape, F32))(a, b)
```

```python
import functools
import math

import numpy as np
import jax
import jax.numpy as jnp
from jax import lax
from jax.experimental import pallas as pl
from jax.experimental.pallas import tpu as pltpu

F32 = jnp.float32
BF16 = jnp.bfloat16

D_MODEL = 1024
CHUNK = 128
A_GROUPS = 8
HEAD_DIM = 64
N_Q_HEADS = 16
N_KV_HEADS = 4
GQA_GROUP = N_Q_HEADS // N_KV_HEADS
KV_DIM = N_KV_HEADS * HEAD_DIM
BLOCK = 128
D_FF = 2816
N_FF = 2 * D_FF
FF_BLK = N_FF // 4
PLE_DIM = 256
EPS = 1e-6
NEG = -1e30
N_SHARD = 4

ADAM_LR = 0.001
ADAM_B1 = 0.9
ADAM_B2 = 0.999
ADAM_EPS = 1e-08
ADAM_WD = 0.01
ADAM_STEP = 10

VMEM_LIMIT = 60 * 1024 * 1024
MESH = pl.DeviceIdType.MESH
ANY = pl.BlockSpec(memory_space=pl.ANY)
SMEM = pl.BlockSpec(memory_space=pltpu.SMEM)

_SLOPES = [float(np.float32(2.0 ** (-8.0 * (h + 1) / N_Q_HEADS))) for h in range(N_Q_HEADS)]


def _dot(a, b):
    return jnp.dot(a, b, preferred_element_type=F32)


def _dot_nt(a, b):
    return lax.dot_general(a, b, (((1,), (1,)), ((), ())), preferred_element_type=F32)


def _dot_tn(a, b):
    return lax.dot_general(a, b, (((0,), (0,)), ((), ())), preferred_element_type=F32)


def _rms(x, g):
    r = lax.rsqrt(jnp.mean(x * x, axis=-1, keepdims=True) + EPS)
    xh = x * r
    return xh * g, xh, r


def _rms_bwd(dy, xh, r, g):
    dxh = dy * g
    dg = jnp.sum(dy * xh, axis=0, keepdims=True)
    dx = r * (dxh - xh * jnp.mean(dxh * xh, axis=-1, keepdims=True))
    return dx, dg


_GELU_C = math.sqrt(2.0 / math.pi)


def _gelu(x):
    t = jnp.tanh(_GELU_C * (x + 0.044715 * (x * x * x)))
    return 0.5 * x * (1.0 + t)


def _gelu_grad(x):
    x2 = x * x
    t = jnp.tanh(_GELU_C * (x + 0.044715 * (x2 * x)))
    return 0.5 * (1.0 + t) + 0.5 * x * (1.0 - t * t) * (_GELU_C * (1.0 + 3.0 * 0.044715 * x2))


def _sigmoid(x):
    return 0.5 * jnp.tanh(0.5 * x) + 0.5


def _load_once(pairs, sem):
    @pl.when(pl.program_id(0) == 0)
    def _():
        cps = [pltpu.make_async_copy(s, d, sem.at[i]) for i, (s, d) in enumerate(pairs)]
        for cp in cps:
            cp.start()
        for cp in cps:
            cp.wait()


def _params(n_axes=1, vmem=VMEM_LIMIT):
    return pltpu.CompilerParams(dimension_semantics=("arbitrary",) * n_axes, vmem_limit_bytes=vmem)


def _row_spec(tm, n, rev_nt=None):
    if rev_nt is None:
        return pl.BlockSpec((tm, n), lambda i: (i, 0))
    return pl.BlockSpec((tm, n), lambda i: (rev_nt - 1 - i, 0))


def _const_spec(shape):
    nd = len(shape)
    return pl.BlockSpec(shape, lambda i: (0,) * nd)


def _add_deps(body, in_specs, args, deps):
    nd = len(deps)
    if nd == 0:
        return body, list(in_specs), list(args)

    def wrapped(*refs):
        return body(*refs[nd:])

    return wrapped, [ANY] * nd + list(in_specs), list(deps) + list(args)


def _zero_first(refs):
    @pl.when(pl.program_id(0) == 0)
    def _():
        for r in refs:
            r[...] = jnp.zeros(r.shape, r.dtype)


def _mixer_a_fwd(x, nmix, gv, wsm, bsb, w_in, w_out):
    T = x.shape[0]
    tm = min(512, T)
    nt = T // tm
    nw = 2 * D_MODEL // N_SHARD

    def body(x_ref, nmix_ref, gv_ref, ws_ref, bsb_ref, w_in_hbm, w_out_hbm,
             h1_ref, zp_ref, w_in_v, w_out_v, gated_v, sem):
        _load_once([(w_in_hbm, w_in_v), (w_out_hbm, w_out_v)], sem)
        xv = x_ref[...]
        xn = _rms(xv, nmix_ref[...])[0].astype(BF16)
        for j in range(N_SHARD):
            zp_ref[:, j * nw:(j + 1) * nw] = _dot(xn, w_in_v[j])
        z = _gelu(zp_ref[...])
        u = z[:, :D_MODEL]
        vn = _rms(z[:, D_MODEL:], gv_ref[...])[0].astype(BF16)
        for c in range(tm // CHUNK):
            rows = slice(c * CHUNK, (c + 1) * CHUNK)
            for h in range(A_GROUPS):
                cols = slice(h * CHUNK, (h + 1) * CHUNK)
                s = _dot(ws_ref[h], vn[rows, cols]) + bsb_ref[h]
                gated_v[rows, cols] = (u[rows, cols] * s).astype(BF16)
        h1_ref[...] = xv + _dot(gated_v[...], w_out_v[...])

    return pl.pallas_call(
        body, name="mixer_a_fwd", grid=(nt,),
        in_specs=[_row_spec(tm, D_MODEL), _const_spec((1, D_MODEL)), _const_spec((1, D_MODEL)),
                  _const_spec((A_GROUPS, CHUNK, CHUNK)), _const_spec((A_GROUPS, CHUNK, CHUNK)), ANY, ANY],
        out_specs=[_row_spec(tm, D_MODEL), _row_spec(tm, 2 * D_MODEL)],
        out_shape=[jax.ShapeDtypeStruct((T, D_MODEL), F32), jax.ShapeDtypeStruct((T, 2 * D_MODEL), F32)],
        scratch_shapes=[pltpu.VMEM((N_SHARD, D_MODEL, nw), BF16), pltpu.VMEM((D_MODEL, D_MODEL), BF16),
                        pltpu.VMEM((tm, D_MODEL), BF16), pltpu.SemaphoreType.DMA((2,))],
        compiler_params=_params(),
    )(x, nmix, gv, wsm, bsb, w_in, w_out)


def _mixer_a_bwd(dh, x, zp, nmix, gv, wsm, bsb, tril, w_in, w_out, deps=()):
    T = x.shape[0]
    tm = min(256, T)
    nt = T // tm
    nw = 2 * D_MODEL // N_SHARD

    def body(dh_ref, x_ref, zp_ref, nmix_ref, gv_ref, ws_ref, bsb_ref, tril_ref, w_in_hbm, w_out_hbm,
             dx_ref, dwin_ref, dwout_ref, dws_ref, dbs_ref, dgv_ref, dnmix_ref,
             w_in_v, w_out_v, du_v, dvn_v, dbs_v, gated_ref, sem):
        _load_once([(w_in_hbm, w_in_v), (w_out_hbm, w_out_v)], sem)
        _zero_first([dws_ref, dbs_v, dgv_ref, dnmix_ref, dwin_ref, dwout_ref])
        i = pl.program_id(0)
        dhv = dh_ref[...]
        dhb = dhv.astype(BF16)
        xv = x_ref[...]
        xn, xh, r = _rms(xv, nmix_ref[...])
        xnb = xn.astype(BF16)
        zpv = zp_ref[...]
        z = _gelu(zpv)
        u = z[:, :D_MODEL]
        vn_f, vh, rv = _rms(z[:, D_MODEL:], gv_ref[...])
        vn = vn_f.astype(BF16)
        dgated = _dot_nt(dhb, w_out_v[...])
        for c in range(tm // CHUNK):
            rows = slice(c * CHUNK, (c + 1) * CHUNK)
            for h in range(A_GROUPS):
                cols = slice(h * CHUNK, (h + 1) * CHUNK)
                vn_h = vn[rows, cols]
                s = _dot(ws_ref[h], vn_h) + bsb_ref[h]
                dgt = dgated[rows, cols]
                u_h = u[rows, cols]
                gated_ref[rows, cols] = (u_h * s).astype(BF16)
                du_v[rows, cols] = dgt * s
                ds = dgt * u_h
                dsb = ds.astype(BF16)
                dws_ref[h] += _dot_nt(dsb, vn_h)
                dbs_v[h] += ds
                dvn_v[rows, cols] = _dot_tn(ws_ref[h], dsb)
        dwout_ref[...] += _dot_tn(gated_ref[...], dhb)
        dv, dgv = _rms_bwd(dvn_v[...], vh, rv, gv_ref[...])
        dgv_ref[...] += dgv
        dzu = (du_v[...] * _gelu_grad(zpv[:, :D_MODEL])).astype(BF16)
        dzv = (dv * _gelu_grad(zpv[:, D_MODEL:])).astype(BF16)
        dzs = (dzu[:, :nw], dzu[:, nw:], dzv[:, :nw], dzv[:, nw:])
        dxn = jnp.zeros((tm, D_MODEL), F32)
        for j in range(N_SHARD):
            dxn = dxn + _dot_nt(dzs[j], w_in_v[j])
            dwin_ref[j] += _dot_tn(xnb, dzs[j])
        dxx, dn = _rms_bwd(dxn, xh, r, nmix_ref[...])
        dnmix_ref[...] += dn
        dx_ref[...] = dhv + dxx

        @pl.when(i == nt - 1)
        def _():
            for h in range(A_GROUPS):
                dws_ref[h] = dws_ref[h] * tril_ref[...]
                dbs_ref[h] = jnp.broadcast_to(jnp.sum(dbs_v[h], axis=1, keepdims=True), (CHUNK, CHUNK))

    grp = (A_GROUPS, CHUNK, CHUNK)
    body, in_specs, args = _add_deps(
        body, [_row_spec(tm, D_MODEL), _row_spec(tm, D_MODEL), _row_spec(tm, 2 * D_MODEL),
               _const_spec((1, D_MODEL)), _const_spec((1, D_MODEL)), _const_spec(grp), _const_spec(grp),
               _const_spec((CHUNK, CHUNK)), ANY, ANY],
        [dh, x, zp, nmix, gv, wsm, bsb, tril, w_in, w_out], deps)
    return pl.pallas_call(
        body, name="mixer_a_bwd", grid=(nt,), in_specs=in_specs,
        out_specs=[_row_spec(tm, D_MODEL), _const_spec((N_SHARD, D_MODEL, nw)), _const_spec((D_MODEL, D_MODEL)),
                   _const_spec(grp), _const_spec(grp), _const_spec((1, D_MODEL)), _const_spec((1, D_MODEL))],
        out_shape=[jax.ShapeDtypeStruct((T, D_MODEL), F32), jax.ShapeDtypeStruct((N_SHARD, D_MODEL, nw), F32),
                   jax.ShapeDtypeStruct((D_MODEL, D_MODEL), F32),
                   jax.ShapeDtypeStruct(grp, F32), jax.ShapeDtypeStruct(grp, F32),
                   jax.ShapeDtypeStruct((1, D_MODEL), F32), jax.ShapeDtypeStruct((1, D_MODEL), F32)],
        scratch_shapes=[pltpu.VMEM((N_SHARD, D_MODEL, nw), BF16), pltpu.VMEM((D_MODEL, D_MODEL), BF16),
                        pltpu.VMEM((tm, D_MODEL), F32), pltpu.VMEM((tm, D_MODEL), F32),
                        pltpu.VMEM(grp, F32), pltpu.VMEM((tm, D_MODEL), BF16), pltpu.SemaphoreType.DMA((2,))],
        compiler_params=_params(),
    )(*args)


def _load_ffn_weights(w_up_hbm, w_dn_hbm, layer, w_up_v, w_dn_v, sem):
    _load_once([(w_up_hbm, w_up_v), (w_dn_hbm, w_dn_v)], sem)


def _ffn_fwd(h, nffn, cw, cb, w_up, w_dn, layer):
    T = h.shape[0]
    tm = min(256, T)
    nt = T // tm

    def body(h_ref, n_ref, cw_ref, cb_ref, w_up_hbm, w_dn_hbm, out_ref, hh_ref, c_ref,
             w_up_v, w_dn_v, carry_v, sem):
        _load_ffn_weights(w_up_hbm, w_dn_hbm, layer, w_up_v, w_dn_v, sem)
        _zero_first([carry_v])
        xv = h_ref[...]
        xf = _rms(xv, n_ref[...])[0].astype(BF16)
        acc = xv
        for j in range(2):
            cs = []
            for blk in (j, j + 2):
                cols = slice(blk * FF_BLK, (blk + 1) * FF_BLK)
                hh = _dot(xf, w_up_v[blk])
                hh_ref[:, cols] = hh.astype(BF16)
                ext = jnp.concatenate([carry_v[blk], hh], axis=0)
                carry_v[blk] = hh[tm - 8:, :]
                s1 = pltpu.roll(ext, 1, 0)[8:]
                s2 = pltpu.roll(ext, 2, 0)[8:]
                cv = (cb_ref[:, cols] + cw_ref[0:1, cols] * s2 + cw_ref[1:2, cols] * s1
                      + cw_ref[2:3, cols] * hh)
                c_ref[:, cols] = cv.astype(BF16)
                cs.append(cv)
            act = (cs[0] * _sigmoid(cs[0]) * cs[1]).astype(BF16)
            acc = acc + _dot(act, w_dn_v[j * FF_BLK:(j + 1) * FF_BLK, :])
        out_ref[...] = acc

    return pl.pallas_call(
        body, name=f"ffn_fwd{layer}", grid=(nt,),
        in_specs=[_row_spec(tm, D_MODEL), _const_spec((1, D_MODEL)), _const_spec((3, N_FF)),
                  _const_spec((1, N_FF)), ANY, ANY],
        out_specs=[_row_spec(tm, D_MODEL), _row_spec(tm, N_FF), _row_spec(tm, N_FF)],
        out_shape=[jax.ShapeDtypeStruct((T, D_MODEL), F32), jax.ShapeDtypeStruct((T, N_FF), BF16),
                   jax.ShapeDtypeStruct((T, N_FF), BF16)],
        scratch_shapes=[pltpu.VMEM((N_SHARD, D_MODEL, FF_BLK), BF16), pltpu.VMEM((D_FF, D_MODEL), BF16),
                        pltpu.VMEM((N_SHARD, 8, FF_BLK), F32), pltpu.SemaphoreType.DMA((2 * N_SHARD,))],
        compiler_params=_params(),
    )(h, nffn, cw, cb, w_up, w_dn)


def _ffn_bwd(dh, h, hh, c, nffn, cw, w_up, w_dn, layer, deps=()):
    T = h.shape[0]
    tm = min(256, T)
    nt = T // tm

    def body(dh_ref, h_ref, hh_ref, c_ref, n_ref, cw_ref, w_up_hbm, w_dn_hbm,
             dhin_ref, act_ref, dhh_ref, xf_ref, dcw_ref, dcb_ref, dn_ref,
             w_up_v, w_dn_v, carry_v, sem):
        _load_ffn_weights(w_up_hbm, w_dn_hbm, layer, w_up_v, w_dn_v, sem)
        _zero_first([carry_v, dcw_ref, dcb_ref, dn_ref])
        dout = dh_ref[...]
        doutb = dout.astype(BF16)
        xf_f, xh, r = _rms(h_ref[...], n_ref[...])
        xf_ref[...] = xf_f.astype(BF16)
        dxf = jnp.zeros((tm, D_MODEL), F32)
        for j in range(2):
            blks = (j, j + 2)
            cg = c_ref[:, j * FF_BLK:(j + 1) * FF_BLK].astype(F32)
            cu = c_ref[:, (j + 2) * FF_BLK:(j + 3) * FF_BLK].astype(F32)
            sg = _sigmoid(cg)
            sil = cg * sg
            act_ref[:, j * FF_BLK:(j + 1) * FF_BLK] = (sil * cu).astype(BF16)
            dact = _dot_nt(doutb, w_dn_v[j * FF_BLK:(j + 1) * FF_BLK, :])
            dcs = (dact * cu * (sg * (1.0 + cg * (1.0 - sg))), dact * sil)
            for blk, dc in zip(blks, dcs):
                cols = slice(blk * FF_BLK, (blk + 1) * FF_BLK)
                hhv = hh_ref[:, cols].astype(F32)
                ext = jnp.concatenate([dc, carry_v[blk]], axis=0)
                carry_v[blk] = dc[:8, :]
                n = tm + 8
                a1 = pltpu.roll(ext, n - 1, 0)[:tm]
                a2 = pltpu.roll(ext, n - 2, 0)[:tm]
                dcb_ref[:, cols] += jnp.sum(dc, axis=0, keepdims=True)
                dcw_ref[0:1, cols] += jnp.sum(a2 * hhv, axis=0, keepdims=True)
                dcw_ref[1:2, cols] += jnp.sum(a1 * hhv, axis=0, keepdims=True)
                dcw_ref[2:3, cols] += jnp.sum(dc * hhv, axis=0, keepdims=True)
                dhh = (cw_ref[2:3, cols] * dc + cw_ref[1:2, cols] * a1 + cw_ref[0:1, cols] * a2).astype(BF16)
                dhh_ref[:, cols] = dhh
                dxf = dxf + _dot_nt(dhh, w_up_v[blk])
        dxx, dn = _rms_bwd(dxf, xh, r, n_ref[...])
        dn_ref[...] += dn
        dhin_ref[...] = dout + dxx

    rev = functools.partial(_row_spec, rev_nt=nt)
    body, in_specs, args = _add_deps(
        body, [rev(tm, D_MODEL), rev(tm, D_MODEL), rev(tm, N_FF), rev(tm, N_FF),
               _const_spec((1, D_MODEL)), _const_spec((3, N_FF)), ANY, ANY],
        [dh, h, hh, c, nffn, cw, w_up, w_dn], deps)
    return pl.pallas_call(
        body, name=f"ffn_bwd{layer}", grid=(nt,), in_specs=in_specs,
        out_specs=[rev(tm, D_MODEL), rev(tm, D_FF), rev(tm, N_FF), rev(tm, D_MODEL),
                   _const_spec((3, N_FF)), _const_spec((1, N_FF)), _const_spec((1, D_MODEL))],
        out_shape=[jax.ShapeDtypeStruct((T, D_MODEL), F32), jax.ShapeDtypeStruct((T, D_FF), BF16),
                   jax.ShapeDtypeStruct((T, N_FF), BF16), jax.ShapeDtypeStruct((T, D_MODEL), BF16),
                   jax.ShapeDtypeStruct((3, N_FF), F32), jax.ShapeDtypeStruct((1, N_FF), F32),
                   jax.ShapeDtypeStruct((1, D_MODEL), F32)],
        scratch_shapes=[pltpu.VMEM((N_SHARD, D_MODEL, FF_BLK), BF16), pltpu.VMEM((D_FF, D_MODEL), BF16),
                        pltpu.VMEM((N_SHARD, 8, FF_BLK), F32), pltpu.SemaphoreType.DMA((2 * N_SHARD,))],
        compiler_params=_params(),
    )(*args)


def _load_ple_weights(w_pin_hbm, w_gate_hbm, layer, w_pin_v, w_gate_v, sem, extra=()):
    _load_once([(w_pin_hbm, w_pin_v), (w_gate_hbm, w_gate_v)] + list(extra), sem)


def _ple_fwd_kv(h, p, nple, bg, nkv, w_pin, w_gate, w_kv):
    T = h.shape[0]
    tm = min(512, T)
    nt = T // tm
    pw = D_MODEL // N_SHARD

    def body(h_ref, p_ref, n_ref, bg_ref, nkv_ref, w_pin_hbm, w_gate_hbm, w_kv_hbm,
             out_ref, pe_ref, a_ref, kv_ref, w_pin_v, w_gate_v, w_kv_v, sem):
        _load_ple_weights(w_pin_hbm, w_gate_hbm, 0, w_pin_v, w_gate_v, sem, [(w_kv_hbm, w_kv_v)])
        xv = h_ref[...]
        xg = _rms(xv, n_ref[...])[0].astype(BF16)
        a = _dot(xg, w_gate_v[...]) + bg_ref[...]
        a_ref[...] = a
        pb = p_ref[...].astype(BF16)
        for j in range(N_SHARD):
            pe_ref[:, j * pw:(j + 1) * pw] = _dot(pb, w_pin_v[j])
        hn = xv + pe_ref[...] * _sigmoid(a)
        out_ref[...] = hn
        kvn = _rms(hn, nkv_ref[...])[0].astype(BF16)
        kv_ref[...] = _dot(kvn, w_kv_v[...]).astype(BF16)

    vec = _const_spec((1, D_MODEL))
    return pl.pallas_call(
        body, name="ple_fwd0", grid=(nt,),
        in_specs=[_row_spec(tm, D_MODEL), _row_spec(tm, PLE_DIM), vec, vec, vec, ANY, ANY, ANY],
        out_specs=[_row_spec(tm, D_MODEL), _row_spec(tm, D_MODEL), _row_spec(tm, D_MODEL),
                   _row_spec(tm, 2 * KV_DIM)],
        out_shape=[jax.ShapeDtypeStruct((T, D_MODEL), F32), jax.ShapeDtypeStruct((T, D_MODEL), F32),
                   jax.ShapeDtypeStruct((T, D_MODEL), F32), jax.ShapeDtypeStruct((T, 2 * KV_DIM), BF16)],
        scratch_shapes=[pltpu.VMEM((N_SHARD, PLE_DIM, pw), BF16), pltpu.VMEM((D_MODEL, D_MODEL), BF16),
                        pltpu.VMEM((D_MODEL, 2 * KV_DIM), BF16), pltpu.SemaphoreType.DMA((2 * N_SHARD + 1,))],
        compiler_params=_params(),
    )(h, p, nple, bg, nkv, w_pin, w_gate, w_kv)


def _ple_fwd_final(h, p, tgt, nple, bg, nfin, w_pin, w_gate):
    T = h.shape[0]
    tm = min(512, T)
    nt = T // tm
    pw = D_MODEL // N_SHARD

    def body(h_ref, p_ref, t_ref, n_ref, bg_ref, nf_ref, w_pin_hbm, w_gate_hbm,
             dh_ref, pe_ref, a_ref, loss_ref, dnf_ref, w_pin_v, w_gate_v, sem):
        _load_ple_weights(w_pin_hbm, w_gate_hbm, 1, w_pin_v, w_gate_v, sem)
        _zero_first([loss_ref, dnf_ref])
        xv = h_ref[...]
        xg = _rms(xv, n_ref[...])[0].astype(BF16)
        a = _dot(xg, w_gate_v[...]) + bg_ref[...]
        a_ref[...] = a
        pb = p_ref[...].astype(BF16)
        for j in range(N_SHARD):
            pe_ref[:, j * pw:(j + 1) * pw] = _dot(pb, w_pin_v[j])
        hn = xv + pe_ref[...] * _sigmoid(a)
        y, yh, r = _rms(hn, nf_ref[...])
        diff = y - t_ref[...]
        loss_ref[...] += 0.5 * jnp.sum(jnp.mean(diff * diff, axis=-1, keepdims=True))
        dy = diff * (1.0 / D_MODEL)
        dhn, dnf = _rms_bwd(dy, yh, r, nf_ref[...])
        dnf_ref[...] += dnf
        dh_ref[...] = dhn

    vec = _const_spec((1, D_MODEL))
    return pl.pallas_call(
        body, name="ple_fwd1", grid=(nt,),
        in_specs=[_row_spec(tm, D_MODEL), _row_spec(tm, PLE_DIM), _row_spec(tm, D_MODEL), vec, vec, vec, ANY, ANY],
        out_specs=[_row_spec(tm, D_MODEL), _row_spec(tm, D_MODEL), _row_spec(tm, D_MODEL),
                   _const_spec((8, 128)), vec],
        out_shape=[jax.ShapeDtypeStruct((T, D_MODEL), F32), jax.ShapeDtypeStruct((T, D_MODEL), F32),
                   jax.ShapeDtypeStruct((T, D_MODEL), F32), jax.ShapeDtypeStruct((8, 128), F32),
                   jax.ShapeDtypeStruct((1, D_MODEL), F32)],
        scratch_shapes=[pltpu.VMEM((N_SHARD, PLE_DIM, pw), BF16), pltpu.VMEM((D_MODEL, D_MODEL), BF16),
                        pltpu.SemaphoreType.DMA((2 * N_SHARD,))],
        compiler_params=_params(),
    )(h, p, tgt, nple, bg, nfin, w_pin, w_gate)


def _ple_bwd(dh, hb, pe, a, p, nple, w_gate, layer, kv_args=None):
    T = hb.shape[0]
    tm = min(512, T)
    nt = T // tm
    with_kv = kv_args is not None
    pw = D_MODEL // N_SHARD

    def body(*refs):
        if with_kv:
            (dh_ref, hb_ref, pe_ref, a_ref, p_ref, n_ref, w_gate_hbm, hc_ref, dkv_ref, nkv_ref, w_kv_hbm,
             dhb_ref, dwpin_ref, dwgate_ref, dbg_ref, dn_ref, dwkv_ref, dnkv_ref,
             w_gate_v, w_kv_v, sem) = refs
        else:
            (dh_ref, hb_ref, pe_ref, a_ref, p_ref, n_ref, w_gate_hbm,
             dhb_ref, dwpin_ref, dwgate_ref, dbg_ref, dn_ref, w_gate_v, sem) = refs
        pairs = [(w_gate_hbm, w_gate_v)]
        if with_kv:
            pairs.append((w_kv_hbm, w_kv_v))
        _load_once(pairs, sem)
        _zero_first([dwpin_ref, dwgate_ref, dbg_ref, dn_ref] + ([dwkv_ref, dnkv_ref] if with_kv else []))
        do = dh_ref[...]
        if with_kv:
            dkvb = dkv_ref[...].astype(BF16)
            dkvn = _dot_nt(dkvb, w_kv_v[...])
            kvn, kh, kr = _rms(hc_ref[...], nkv_ref[...])
            dwkv_ref[...] += _dot_tn(kvn.astype(BF16), dkvb)
            dk, dnkv = _rms_bwd(dkvn, kh, kr, nkv_ref[...])
            dnkv_ref[...] += dnkv
            do = do + dk
        gate = _sigmoid(a_ref[...])
        dpe = (do * gate).astype(BF16)
        pb = p_ref[...].astype(BF16)
        for j in range(N_SHARD):
            dwpin_ref[j] += _dot_tn(pb, dpe[:, j * pw:(j + 1) * pw])
        da = do * pe_ref[...] * (gate * (1.0 - gate))
        dab = da.astype(BF16)
        dbg_ref[...] += jnp.sum(da, axis=0, keepdims=True)
        dxg = _dot_nt(dab, w_gate_v[...])
        xg, xh, r = _rms(hb_ref[...], n_ref[...])
        dwgate_ref[...] += _dot_tn(xg.astype(BF16), dab)
        dxx, dn = _rms_bwd(dxg, xh, r, n_ref[...])
        dn_ref[...] += dn
        dhb_ref[...] = do + dxx

    vec = _const_spec((1, D_MODEL))
    row = _row_spec(tm, D_MODEL)
    in_specs = [row, row, row, row, _row_spec(tm, PLE_DIM), vec, ANY]
    args = [dh, hb, pe, a, p, nple, w_gate]
    out_specs = [row, _const_spec((N_SHARD, PLE_DIM, pw)), _const_spec((D_MODEL, D_MODEL)), vec, vec]
    out_shape = [jax.ShapeDtypeStruct((T, D_MODEL), F32), jax.ShapeDtypeStruct((N_SHARD, PLE_DIM, pw), F32),
                 jax.ShapeDtypeStruct((D_MODEL, D_MODEL), F32),
                 jax.ShapeDtypeStruct((1, D_MODEL), F32), jax.ShapeDtypeStruct((1, D_MODEL), F32)]
    scratch = [pltpu.VMEM((D_MODEL, D_MODEL), BF16)]
    if with_kv:
        hc, dkv, nkv, w_kv = kv_args
        in_specs += [row, _row_spec(tm, 2 * KV_DIM), vec, ANY]
        args += [hc, dkv, nkv, w_kv]
        out_specs += [_const_spec((D_MODEL, 2 * KV_DIM)), vec]
        out_shape += [jax.ShapeDtypeStruct((D_MODEL, 2 * KV_DIM), F32), jax.ShapeDtypeStruct((1, D_MODEL), F32)]
        scratch.append(pltpu.VMEM((D_MODEL, 2 * KV_DIM), BF16))
    scratch.append(pltpu.SemaphoreType.DMA((N_SHARD + 1,)))
    return pl.pallas_call(
        body, name=f"ple_bwd{layer}", grid=(nt,), in_specs=in_specs, out_specs=out_specs,
        out_shape=out_shape, scratch_shapes=scratch, compiler_params=_params(),
    )(*args)


def _band_masks(is_first):
    ii = lax.broadcasted_iota(jnp.int32, (BLOCK, 2 * BLOCK), 0)
    jj = lax.broadcasted_iota(jnp.int32, (BLOCK, 2 * BLOCK), 1)
    dist = ii + BLOCK - jj
    valid = (dist >= 0) & (dist < BLOCK) & ((jj >= BLOCK) | jnp.logical_not(is_first))
    return dist.astype(F32), valid


def _attn_fwd(h, nmix, kv, sinks, w_q, w_o):
    T = h.shape[0]
    tm = min(512, T)
    nt = T // tm
    nb = tm // BLOCK

    def body(h_ref, n_ref, kv_ref, kvp_ref, sink_ref, w_q_hbm, w_o_hbm,
             out_ref, q_ref, ao_ref, lse_ref, w_q_v, w_o_v, kvs_v, sem):
        _load_once([(w_q_hbm, w_q_v), (w_o_hbm, w_o_v)], sem)
        ti = pl.program_id(0)
        xv = h_ref[...]
        xn = _rms(xv, n_ref[...])[0].astype(BF16)
        q_ref[...] = (_dot(xn, w_q_v[...]) * (HEAD_DIM ** -0.5)).astype(BF16)
        kvs_v[0:BLOCK, :] = kvp_ref[...]
        kvs_v[BLOCK:, :] = kv_ref[...]
        lane = lax.broadcasted_iota(jnp.int32, (BLOCK, 128), 1)

        def blk_body(b, carry):
            r0 = pl.multiple_of(b * BLOCK, BLOCK)
            distf, valid = _band_masks(jnp.logical_and(ti == 0, b == 0))
            qb = q_ref[pl.ds(r0, BLOCK), :]
            band = kvs_v[pl.ds(r0, 2 * BLOCK), :]
            lse_mat = jnp.zeros((BLOCK, 128), F32)
            outs = []
            for hq in range(N_Q_HEADS):
                kh = hq // GQA_GROUP
                k_h = band[:, kh * HEAD_DIM:(kh + 1) * HEAD_DIM]
                v_h = band[:, KV_DIM + kh * HEAD_DIM:KV_DIM + (kh + 1) * HEAD_DIM]
                s = _dot_nt(qb[:, hq * HEAD_DIM:(hq + 1) * HEAD_DIM], k_h) - _SLOPES[hq] * distf
                s = jnp.where(valid, s, NEG)
                sink = sink_ref[hq]
                m = jnp.maximum(jnp.max(s, axis=1, keepdims=True), sink)
                e = jnp.exp(s - m)
                den = jnp.sum(e, axis=1, keepdims=True) + jnp.exp(sink - m)
                outs.append(_dot((e / den).astype(BF16), v_h))
                lse_mat = jnp.where(lane == hq, m + jnp.log(den), lse_mat)
            ao_ref[pl.ds(r0, BLOCK), :] = jnp.concatenate(outs, axis=1).astype(BF16)
            lse_ref[pl.ds(r0, BLOCK), :] = lse_mat
            return carry

        lax.fori_loop(0, nb, blk_body, 0)
        out_ref[...] = xv + _dot(ao_ref[...], w_o_v[...])

    row = _row_spec(tm, D_MODEL)
    prev_spec = pl.BlockSpec((BLOCK, 2 * KV_DIM), lambda i: (jnp.maximum(i * nb - 1, 0), 0))
    return pl.pallas_call(
        body, name="attn_fwd", grid=(nt,),
        in_specs=[row, _const_spec((1, D_MODEL)), _row_spec(tm, 2 * KV_DIM), prev_spec, SMEM, ANY, ANY],
        out_specs=[row, row, row, _row_spec(tm, 128)],
        out_shape=[jax.ShapeDtypeStruct((T, D_MODEL), F32), jax.ShapeDtypeStruct((T, D_MODEL), BF16),
                   jax.ShapeDtypeStruct((T, D_MODEL), BF16), jax.ShapeDtypeStruct((T, 128), F32)],
        scratch_shapes=[pltpu.VMEM((D_MODEL, D_MODEL), BF16), pltpu.VMEM((D_MODEL, D_MODEL), BF16),
                        pltpu.VMEM((tm + BLOCK, 2 * KV_DIM), BF16), pltpu.SemaphoreType.DMA((2,))],
        compiler_params=_params(),
    )(h, nmix, kv, kv, sinks, w_q, w_o)


def _attn_bwd(dh, h, q, kv, ao, lse, nmix, sinks, w_q, w_o):
    T = h.shape[0]
    tm = min(512, T)
    nt = T // tm
    nb = tm // BLOCK

    def body(dh_ref, h_ref, q_ref, kv_ref, kvp_ref, ao_ref, lse_ref, n_ref, sink_ref, w_q_hbm, w_o_hbm,
             dhin_ref, dwq_ref, dwo_ref, dkv_ref, dsink_ref, dn_ref,
             w_q_v, w_o_v, kvs_v, dao_v, dq_v, dkv_v, carry_v, sem):
        _load_once([(w_q_hbm, w_q_v), (w_o_hbm, w_o_v)], sem)
        _zero_first([carry_v, dsink_ref, dn_ref, dwq_ref, dwo_ref])
        ti = nt - 1 - pl.program_id(0)
        dout = dh_ref[...]
        doutb = dout.astype(BF16)
        dao_v[...] = _dot_nt(doutb, w_o_v[...])
        dwo_ref[...] += _dot_tn(ao_ref[...], doutb)
        kvs_v[0:BLOCK, :] = kvp_ref[...]
        kvs_v[BLOCK:, :] = kv_ref[...]
        dkv_v[0:tm, :] = jnp.zeros((tm, 2 * KV_DIM), F32)
        dkv_v[tm:, :] = carry_v[...]
        lane = lax.broadcasted_iota(jnp.int32, (BLOCK, 128), 1)
        lane8 = lax.broadcasted_iota(jnp.int32, (8, 128), 1)

        def blk_body(b, dsk):
            r0 = pl.multiple_of(b * BLOCK, BLOCK)
            distf, valid = _band_masks(jnp.logical_and(ti == 0, b == 0))
            qb = q_ref[pl.ds(r0, BLOCK), :]
            band = kvs_v[pl.ds(r0, 2 * BLOCK), :]
            aob = ao_ref[pl.ds(r0, BLOCK), :].astype(F32)
            daob = dao_v[pl.ds(r0, BLOCK), :]
            lse_mat = lse_ref[pl.ds(r0, BLOCK), :]
            dqs = []
            dks = []
            dvs = []
            for kh in range(N_KV_HEADS):
                k_h = band[:, kh * HEAD_DIM:(kh + 1) * HEAD_DIM]
                v_h = band[:, KV_DIM + kh * HEAD_DIM:KV_DIM + (kh + 1) * HEAD_DIM]
                dk = jnp.zeros((2 * BLOCK, HEAD_DIM), F32)
                dv = jnp.zeros((2 * BLOCK, HEAD_DIM), F32)
                for g in range(GQA_GROUP):
                    hq = kh * GQA_GROUP + g
                    hc = slice(hq * HEAD_DIM, (hq + 1) * HEAD_DIM)
                    q_h = qb[:, hc]
                    s = _dot_nt(q_h, k_h) - _SLOPES[hq] * distf
                    s = jnp.where(valid, s, NEG)
                    lse = jnp.sum(jnp.where(lane == hq, lse_mat, 0.0), axis=1, keepdims=True)
                    pr = jnp.exp(s - lse)
                    dao_h = daob[:, hc]
                    dd = jnp.sum(dao_h * aob[:, hc], axis=1, keepdims=True)
                    dao_hb = dao_h.astype(BF16)
                    dp = _dot_nt(dao_hb, v_h)
                    dsb = (pr * (dp - dd)).astype(BF16)
                    dqs.append(_dot(dsb, k_h) * (HEAD_DIM ** -0.5))
                    dk = dk + _dot_tn(dsb, q_h)
                    dv = dv + _dot_tn(pr.astype(BF16), dao_hb)
                    dsv = -jnp.sum(jnp.exp(sink_ref[hq] - lse) * dd)
                    dsk = dsk + jnp.where(lane8 == hq, dsv, 0.0)
                dks.append(dk)
                dvs.append(dv)
            dq_v[pl.ds(r0, BLOCK), :] = jnp.concatenate(dqs, axis=1)
            dkv_v[pl.ds(r0, 2 * BLOCK), :] += jnp.concatenate(dks + dvs, axis=1)
            return dsk

        dsk = lax.fori_loop(0, nb, blk_body, jnp.zeros((8, 128), F32))
        dsink_ref[...] += dsk
        dqb = dq_v[...].astype(BF16)
        dxn = _dot_nt(dqb, w_q_v[...])
        xn, xh, r = _rms(h_ref[...], n_ref[...])
        dwq_ref[...] += _dot_tn(xn.astype(BF16), dqb)
        dxx, dn = _rms_bwd(dxn, xh, r, n_ref[...])
        dn_ref[...] += dn
        dhin_ref[...] = dout + dxx
        dkv_ref[...] = dkv_v[BLOCK:, :]
        carry_v[...] = dkv_v[0:BLOCK, :]

    rev = functools.partial(_row_spec, rev_nt=nt)
    row = rev(tm, D_MODEL)
    prev_spec = pl.BlockSpec((BLOCK, 2 * KV_DIM), lambda i: (jnp.maximum((nt - 1 - i) * nb - 1, 0), 0))
    return pl.pallas_call(
        body, name="attn_bwd", grid=(nt,),
        in_specs=[row, row, row, rev(tm, 2 * KV_DIM), prev_spec, row, rev(tm, 128),
                  _const_spec((1, D_MODEL)), SMEM, ANY, ANY],
        out_specs=[row, _const_spec((D_MODEL, D_MODEL)), _const_spec((D_MODEL, D_MODEL)), rev(tm, 2 * KV_DIM),
                   _const_spec((8, 128)), _const_spec((1, D_MODEL))],
        out_shape=[jax.ShapeDtypeStruct((T, D_MODEL), F32), jax.ShapeDtypeStruct((D_MODEL, D_MODEL), F32),
                   jax.ShapeDtypeStruct((D_MODEL, D_MODEL), F32), jax.ShapeDtypeStruct((T, 2 * KV_DIM), F32),
                   jax.ShapeDtypeStruct((8, 128), F32), jax.ShapeDtypeStruct((1, D_MODEL), F32)],
        scratch_shapes=[pltpu.VMEM((D_MODEL, D_MODEL), BF16), pltpu.VMEM((D_MODEL, D_MODEL), BF16),
                        pltpu.VMEM((tm + BLOCK, 2 * KV_DIM), BF16), pltpu.VMEM((tm, D_MODEL), F32),
                        pltpu.VMEM((tm, D_MODEL), F32), pltpu.VMEM((tm + BLOCK, 2 * KV_DIM), F32),
                        pltpu.VMEM((BLOCK, 2 * KV_DIM), F32), pltpu.SemaphoreType.DMA((2,))],
        compiler_params=_params(),
    )(dh, h, q, kv, kv, ao, lse, nmix, sinks, w_q, w_o)


def _wgrad(a, b, bn, col_sharded, name, layer=0, n_layers=1, stacked=None, deps=()):
    T, K = a.shape
    N = b.shape[1]
    tt = min(1024, T)
    nn, ntt = N // bn, T // tt
    kr = K // N_SHARD

    def body(a_ref, b_ref, *rest):
        o_ref = rest[-1]

        @pl.when(pl.program_id(1) == 0)
        def _():
            o_ref[...] = jnp.zeros(o_ref.shape, F32)
        d = _dot_tn(a_ref[...].astype(BF16), b_ref[...].astype(BF16))
        if col_sharded:
            o_ref[...] += d
        else:
            for j in range(N_SHARD):
                o_ref[j] += d[j * kr:(j + 1) * kr]

    if col_sharded:
        assert nn == N_SHARD
        out_spec = pl.BlockSpec((None, None, K, bn), lambda n, t: (n, layer, 0, 0))
        out_shape = jax.ShapeDtypeStruct((N_SHARD, n_layers, K, bn), F32)
    else:
        out_spec = pl.BlockSpec((N_SHARD, None, kr, bn), lambda n, t: (0, layer, 0, n))
        out_shape = jax.ShapeDtypeStruct((N_SHARD, n_layers, kr, N), F32)
    in_specs = [pl.BlockSpec((tt, K), lambda n, t: (t, 0)), pl.BlockSpec((tt, bn), lambda n, t: (t, n))]
    args = [a, b]
    aliases = {}
    if stacked is not None:
        in_specs.append(ANY)
        args.append(stacked)
        aliases = {2 + len(deps): 0}
    body, in_specs, args = _add_deps(body, in_specs, args, deps)
    return pl.pallas_call(
        body, name=name, grid=(nn, ntt), in_specs=in_specs, out_specs=out_spec, out_shape=out_shape,
        input_output_aliases=aliases,
        compiler_params=pltpu.CompilerParams(dimension_semantics=("arbitrary",) * 2, vmem_limit_bytes=VMEM_LIMIT),
    )(*args)


def _mesh_pos():
    return lax.axis_index("x"), lax.axis_index("y"), lax.axis_index("c")


def _other_chips(x, y):
    return [(1 - x, y), (x, 1 - y), (1 - x, 1 - y)]


HBM_SPEC = pl.BlockSpec(memory_space=pltpu.HBM)
SEM_SPEC = pl.BlockSpec(memory_space=pltpu.SEMAPHORE)


def _split_call(name, bufs, waits=(), starts=(), after=()):
    n, nw, ns, na = len(bufs), len(waits), len(starts), len(after)

    def body(*refs):
        brefs = refs[:n]
        wsems = [(refs[n + 2 * k], refs[n + 2 * k + 1]) for k in range(nw)]
        o = n + 2 * nw + na
        ssems = [(refs[o + 2 * k], refs[o + 2 * k + 1]) for k in range(ns)]
        for (ss, rs), (_, _, fn) in zip(wsems, waits):
            for sending, arriving in fn(brefs, ss, rs):
                sending.wait_send()
                arriving.wait_recv()
        for (ss, rs), (_, fn) in zip(ssems, starts):
            for sending, _ in fn(brefs, ss, rs):
                sending.start()
        if ns:
            token = refs[o + 2 * ns + n]
            token[...] = jnp.zeros(token.shape, token.dtype)

    out_shape, out_specs = [], []
    for cnt, _ in starts:
        out_shape += [pltpu.SemaphoreType.DMA((cnt,)), pltpu.SemaphoreType.DMA((cnt,))]
        out_specs += [SEM_SPEC, SEM_SPEC]
    out_shape += [pltpu.HBM(b.shape, b.dtype) for b in bufs]
    out_specs += [HBM_SPEC] * n
    if ns:
        out_shape.append(jax.ShapeDtypeStruct((8, 128), F32))
        out_specs.append(pl.BlockSpec(memory_space=pltpu.VMEM))
    args = [pltpu.with_memory_space_constraint(b, pltpu.HBM) for b in bufs]
    for ss, rs, _ in waits:
        args += [ss, rs]
    args += list(after)
    res = pl.pallas_call(
        body, name=name, out_shape=tuple(out_shape),
        in_specs=[HBM_SPEC] * n + [SEM_SPEC] * (2 * nw) + [ANY] * na, out_specs=tuple(out_specs),
        input_output_aliases={i: 2 * ns + i for i in range(n)},
        compiler_params=pltpu.CompilerParams(has_side_effects=pltpu.SideEffectType.DATAFLOW_SIDE_EFFECTING),
    )(*args)
    sems = [(res[2 * k], res[2 * k + 1]) for k in range(ns)]
    return list(res[2 * ns:2 * ns + n]), sems, (res[2 * ns + n] if ns else None)


def _cast_place(items, name):
    n = len(items)
    mats = [a.shape[-2:] for a, _, _ in items]

    def body(*refs):
        ins, outs, scr, sem = refs[:n], refs[n:2 * n], refs[2 * n:3 * n], refs[3 * n]
        x, y, _ = _mesh_pos()
        cps = []
        for t in range(n):
            scr[t][...] = ins[t][...].astype(scr[t].dtype)
            cp = pltpu.make_async_copy(scr[t], outs[t].at[2 * x + y], sem.at[t])
            cp.start()
            cps.append(cp)
        for cp in cps:
            cp.wait()

    def spec(idx, shape):
        return pl.BlockSpec((None,) * len(idx) + tuple(shape), lambda i: tuple(idx) + (0, 0))

    return pl.pallas_call(
        body, name=name, grid=(1,),
        in_specs=[spec(idx, mat) for (_, idx, _), mat in zip(items, mats)], out_specs=[ANY] * n,
        out_shape=[jax.ShapeDtypeStruct((N_SHARD,) + tuple(mat), dt) for (_, _, dt), mat in zip(items, mats)],
        scratch_shapes=[pltpu.VMEM(tuple(mat), dt) for (_, _, dt), mat in zip(items, mats)]
        + [pltpu.SemaphoreType.DMA((n,))],
        compiler_params=_params(),
    )(*[a for a, _, _ in items])


def _gather_ici(idx):
    def fn(bufs, ss, rs):
        x, y, c = _mesh_pos()
        pairs = []
        for k, t in enumerate(idx):
            half = bufs[t].shape[1] // 2
            mine = bufs[t].at[2 * x + y, pl.ds(c * half, half), :]
            for j, (cx, cy) in enumerate(_other_chips(x, y)):
                theirs = bufs[t].at[2 * cx + cy, pl.ds(c * half, half), :]
                sem = dict(send_sem=ss.at[3 * k + j], recv_sem=rs.at[3 * k + j],
                           device_id=(cx, cy, c), device_id_type=MESH)
                pairs.append((pltpu.make_async_remote_copy(src_ref=mine, dst_ref=mine, **sem),
                              pltpu.make_async_remote_copy(src_ref=mine, dst_ref=theirs, **sem)))
        return pairs
    return fn


def _gather_d2d(idx):
    def fn(bufs, ss, rs):
        x, y, c = _mesh_pos()
        pairs = []
        for k, t in enumerate(idx):
            half = bufs[t].shape[1] // 2
            for j, (cx, cy) in enumerate(_other_chips(x, y)):
                got = bufs[t].at[2 * cx + cy, pl.ds(c * half, half), :]
                theirs = bufs[t].at[2 * cx + cy, pl.ds((1 - c) * half, half), :]
                sem = dict(send_sem=ss.at[3 * k + j], recv_sem=rs.at[3 * k + j],
                           device_id=(x, y, 1 - c), device_id_type=MESH)
                pairs.append((pltpu.make_async_remote_copy(src_ref=got, dst_ref=got, **sem),
                              pltpu.make_async_remote_copy(src_ref=got, dst_ref=theirs, **sem)))
        return pairs
    return fn


def _alloc(shapes, name):
    def body(*refs):
        pass

    return pl.pallas_call(body, name=name, out_specs=[ANY] * len(shapes),
                          out_shape=[jax.ShapeDtypeStruct(s, d) for s, d in shapes])()


def _send_to_sibling(n):
    def fn(bufs, ss, rs):
        x, y, c = _mesh_pos()
        pairs = []
        for t in range(n):
            src = bufs[t]
            if len(src.shape) == 3:
                half = src.shape[1] // 2
                src = src.at[:, pl.ds((1 - c) * half, half), :]
            cp = pltpu.make_async_remote_copy(src_ref=src, dst_ref=bufs[n + t], send_sem=ss.at[t],
                                              recv_sem=rs.at[t], device_id=(x, y, 1 - c), device_id_type=MESH)
            pairs.append((cp, cp))
        return pairs
    return fn


def _send_to_chips(n):
    def fn(bufs, ss, rs):
        x, y, c = _mesh_pos()
        pairs = []
        for j, (cx, cy) in enumerate(_other_chips(x, y)):
            for t in range(n):
                src = bufs[t].at[j] if len(bufs[t].shape) == 3 else bufs[t]
                cp = pltpu.make_async_remote_copy(src_ref=src, dst_ref=bufs[n + t].at[j], send_sem=ss.at[3 * t + j],
                                                  recv_sem=rs.at[3 * t + j], device_id=(cx, cy, c),
                                                  device_id_type=MESH)
                pairs.append((cp, cp))
        return pairs
    return fn


class _Exchange:
    def __init__(self, name, srcs, land_shapes, fn, n_sems):
        self.name, self.fn = name, fn
        lands = _alloc(land_shapes, name + "_alloc")
        self.n = len(srcs)
        self.bufs, sems, self.token = _split_call(name + "_start", list(srcs) + list(lands),
                                                  starts=[(n_sems, fn)])
        self.sems = sems[0]

    def finish(self, after=()):
        bufs, _, _ = _split_call(self.name + "_wait", self.bufs, waits=[(*self.sems, self.fn)], after=after)
        return bufs[:self.n], bufs[self.n:]


def _row_block(rows, cols, mult=8, limit=3 * 512 * 1024, itemsize=4):
    best = None
    for br in range(mult, rows + 1, mult):
        if rows % br == 0 and br * cols * itemsize <= limit:
            best = br
    assert best is not None, (rows, cols)
    return best


def _chip_partial(g, s, ids, name):
    _, half, cols = s.shape
    br = _row_block(half, cols, mult=16)
    nr = half // br

    def body(ids_ref, g_ref, s_ref, o_ref):
        o_ref[...] = (g_ref[...] + s_ref[...]).astype(BF16)

    return pl.pallas_call(
        body, name=name,
        grid_spec=pltpu.PrefetchScalarGridSpec(
            num_scalar_prefetch=1, grid=(3, nr),
            in_specs=[pl.BlockSpec((None, br, cols), lambda j, r, ids_ref: (ids_ref[2 + j], ids_ref[0] * nr + r, 0)),
                      pl.BlockSpec((None, br, cols), lambda j, r, ids_ref: (ids_ref[2 + j], r, 0))],
            out_specs=pl.BlockSpec((None, br, cols), lambda j, r, ids_ref: (j, r, 0))),
        out_shape=jax.ShapeDtypeStruct((3, half, cols), BF16),
        compiler_params=pltpu.CompilerParams(dimension_semantics=("arbitrary", "arbitrary")),
    )(ids, g, s)


def _chip_sum(g, s, q, ids, name):
    _, half, cols = s.shape
    br = _row_block(half, cols, mult=16)
    nr = half // br

    def body(ids_ref, g_ref, s_ref, q_ref, o_ref):
        own = g_ref[...] + s_ref[...]
        o_ref[...] = (own + q_ref[2].astype(F32)) + (q_ref[0].astype(F32) + q_ref[1].astype(F32))

    return pl.pallas_call(
        body, name=name,
        grid_spec=pltpu.PrefetchScalarGridSpec(
            num_scalar_prefetch=1, grid=(nr,),
            in_specs=[pl.BlockSpec((None, br, cols), lambda r, ids_ref: (ids_ref[1], ids_ref[0] * nr + r, 0)),
                      pl.BlockSpec((None, br, cols), lambda r, ids_ref: (ids_ref[1], r, 0)),
                      pl.BlockSpec((3, br, cols), lambda r, ids_ref: (0, r, 0))],
            out_specs=pl.BlockSpec((br, cols), lambda r, ids_ref: (r, 0))),
        out_shape=jax.ShapeDtypeStruct((half, cols), F32),
        compiler_params=pltpu.CompilerParams(dimension_semantics=("arbitrary",)),
    )(ids, g, s, q)


def _small_sum(part, recv):
    def body(p_ref, q_ref, o_ref):
        o_ref[...] = (p_ref[...] + q_ref[2]) + (q_ref[0] + q_ref[1])

    return pl.pallas_call(body, name="chip_sum_small", out_shape=jax.ShapeDtypeStruct(part.shape, F32))(part, recv)


def _adamw_math(w, g, m, v):
    mn = ADAM_B1 * m + (1.0 - ADAM_B1) * g
    vn = ADAM_B2 * v + (1.0 - ADAM_B2) * (g * g)
    m_hat = mn / (1.0 - ADAM_B1 ** ADAM_STEP)
    v_hat = vn / (1.0 - ADAM_B2 ** ADAM_STEP)
    return -ADAM_LR * (m_hat / (jnp.sqrt(v_hat) + ADAM_EPS) + ADAM_WD * w), mn, vn


def _adamw(w, g, m, v, name):
    R, C = w.shape
    br = _row_block(R, C)

    def body(w_ref, g_ref, m_ref, v_ref, d_ref, mo_ref, vo_ref):
        d_ref[...], mo_ref[...], vo_ref[...] = _adamw_math(w_ref[...], g_ref[...], m_ref[...], v_ref[...])

    spec = pl.BlockSpec((br, C), lambda i: (i, 0))
    return pl.pallas_call(
        body, name=name, grid=(R // br,), in_specs=[spec] * 4, out_specs=[spec] * 3,
        out_shape=[jax.ShapeDtypeStruct((R, C), F32)] * 3, compiler_params=_params(),
    )(w, g, m, v)


def _adamw_halves(w, own, sib, m, v, ids, name, layer=0, n_layers=1, stacked=None):
    C = w.shape[1]
    R = w.shape[0] // n_layers
    half = R // 2
    br = _row_block(half, C)
    nh = half // br
    base = layer * 2 * nh

    def body(ids_ref, w_ref, own_ref, sib_ref, m_ref, v_ref, *rest):
        g_ref, d_ref, mo_ref, vo_ref = rest[-4:]
        is_own = (pl.program_id(0) // nh) == ids_ref[0]
        g = jnp.where(is_own, own_ref[...], sib_ref[...])
        g_ref[...] = g
        d_ref[...], mo_ref[...], vo_ref[...] = _adamw_math(w_ref[...], g, m_ref[...], v_ref[...])

    full = pl.BlockSpec((br, C), lambda r, ids_ref: (base + r, 0))
    own_spec = pl.BlockSpec((br, C), lambda r, ids_ref: (jnp.clip(r - ids_ref[0] * nh, 0, nh - 1), 0))
    sib_spec = pl.BlockSpec((br, C), lambda r, ids_ref: (jnp.clip(r - (1 - ids_ref[0]) * nh, 0, nh - 1), 0))
    in_specs = [full, own_spec, sib_spec, full, full]
    args = [ids, w, own, sib, m, v]
    aliases = {}
    if stacked is not None:
        in_specs += [ANY] * 4
        args += list(stacked)
        aliases = {6 + k: k for k in range(4)}
    return pl.pallas_call(
        body, name=name,
        grid_spec=pltpu.PrefetchScalarGridSpec(
            num_scalar_prefetch=1, grid=(2 * nh,), in_specs=in_specs, out_specs=[full] * 4),
        out_shape=[jax.ShapeDtypeStruct(w.shape, F32)] * 4, input_output_aliases=aliases,
        compiler_params=_params(),
    )(*args)


_PACK_UNIT = 1024


def _pack(arrs):
    flat = []
    for a in arrs:
        f = a.reshape(-1).astype(F32)
        pad = (-f.shape[0]) % _PACK_UNIT
        if pad:
            f = jnp.concatenate([f, jnp.zeros((pad,), F32)])
        flat.append(f)
    return jnp.concatenate(flat).reshape(-1, 128)


def _unpack(packed, shapes):
    flat = packed.reshape(-1)
    out, off = [], 0
    for shp in shapes:
        size = int(np.prod(shp))
        out.append(flat[off:off + size].reshape(shp))
        off += size + ((-size) % _PACK_UNIT)
    return out


def kernel(x, p, norm_mix, norm_ffn, norm_ple, norm_kv, norm_final, a_w_in, a_norm_v, a_w_s, a_b_s, a_w_out, w_kv, b_w_q, b_sinks, b_w_o, f_w_up, f_conv_w, f_conv_b, f_w_down, ple_w_in, ple_w_gate, ple_b_gate, loss_target, m_norm_mix, m_norm_ffn, m_norm_ple, m_norm_kv, m_norm_final, m_a_w_in, m_a_norm_v, m_a_w_s, m_a_b_s, m_a_w_out, m_w_kv, m_b_w_q, m_b_sinks, m_b_w_o, m_f_w_up, m_f_conv_w, m_f_conv_b, m_f_w_down, m_ple_w_in, m_ple_w_gate, m_ple_b_gate, v_norm_mix, v_norm_ffn, v_norm_ple, v_norm_kv, v_norm_final, v_a_w_in, v_a_norm_v, v_a_w_s, v_a_b_s, v_a_w_out, v_w_kv, v_b_w_q, v_b_sinks, v_b_w_o, v_f_w_up, v_f_conv_w, v_f_conv_b, v_f_w_down, v_ple_w_in, v_ple_w_gate, v_ple_b_gate):
    given = dict(locals())

    small_shard = _pack([a_norm_v, f_conv_w])
    pad_rows = (-small_shard.shape[0]) % 16
    if pad_rows:
        small_shard = jnp.concatenate([small_shard, jnp.zeros((pad_rows, 128), F32)])
    groups = [
        [(a_w_in, (0,), BF16), (a_w_out, (0,), BF16), (small_shard, (), F32)],
        [(f_w_up, (0,), BF16), (f_w_down, (0,), BF16)],
        [(ple_w_in, (0,), BF16), (ple_w_gate, (0,), BF16), (w_kv, (), BF16), (b_w_q, (0,), BF16),
         (b_w_o, (0,), BF16), (f_w_up, (1,), BF16), (f_w_down, (1,), BF16), (ple_w_in, (1,), BF16),
         (ple_w_gate, (1,), BF16)],
    ]
    lands, spans, start = [], [], 0
    for gi, items in enumerate(groups):
        lands += _cast_place(items, f"cast_place_g{gi}")
        spans.append(list(range(start, start + len(items))))
        start += len(items)
    lands, ici_sems, _ = _split_call("gather_start", lands,
                                     starts=[(3 * len(sp), _gather_ici(sp)) for sp in spans])

    def finish_group(gi, after):
        sp = spans[gi]
        local = list(range(len(sp)))
        bufs = [lands[t] for t in sp]
        bufs, d2d_sems, _ = _split_call(f"gather_pass_g{gi}", bufs, waits=[(*ici_sems[gi], _gather_ici(local))],
                                        starts=[(3 * len(sp), _gather_d2d(local))], after=after)
        bufs, _, _ = _split_call(f"gather_done_g{gi}", bufs, waits=[(*d2d_sems[0], _gather_d2d(local))])
        return bufs

    def stage0():
        b_in, b_out, b_small = finish_group(0, ())
        small_full = b_small.reshape(N_SHARD, -1)
        gv_full = small_full[:, :256].reshape(1, D_MODEL)
        cw_full = small_full[:, _PACK_UNIT:_PACK_UNIT + 2 * 3 * FF_BLK].reshape(N_SHARD, 2, 3, FF_BLK)
        cw_full = jnp.transpose(cw_full, (1, 2, 0, 3)).reshape(2, 3, N_FF)
        return gv_full, cw_full, b_in, b_out.reshape(D_MODEL, D_MODEL)

    def stage1(after):
        b_up, b_dn = finish_group(1, after)
        return b_up, b_dn.reshape(D_FF, D_MODEL)

    def stage2(after):
        pin0, gate0, kv_w, wq, wo, up1, dn1, pin1, gate1 = finish_group(2, after)
        sq = lambda a: a.reshape(D_MODEL, -1)
        return dict(w_pin=[pin0, pin1], w_gate=[sq(gate0), sq(gate1)], w_kv=sq(kv_w), w_q=sq(wq), w_o=sq(wo),
                    w_up1=up1, w_dn1=dn1.reshape(D_FF, D_MODEL))

    loss_acc, dx, (out_g, out_d, out_m, out_v) = _local_step(
        x[0], p[0, 0], p[1, 0], loss_target[0], norm_mix, norm_ffn, norm_ple, norm_kv, norm_final, a_w_s, a_b_s,
        b_sinks, f_conv_b, ple_b_gate, stage0, stage1, stage2, _Reducer(given))
    weight_names = ['norm_mix', 'norm_ffn', 'norm_ple', 'norm_kv', 'norm_final', 'a_w_in', 'a_norm_v', 'a_w_s',
                    'a_b_s', 'a_w_out', 'w_kv', 'b_w_q', 'b_sinks', 'b_w_o', 'f_w_up', 'f_conv_w', 'f_conv_b',
                    'f_w_down', 'ple_w_in', 'ple_w_gate', 'ple_b_gate']
    loss = lax.psum(loss_acc[0, 0], ("x", "y", "c"))
    return (loss, dx.reshape(x.shape), *[out_g[k] for k in weight_names], *[out_d[k] for k in weight_names],
            *[out_m[k] for k in weight_names], *[out_v[k] for k in weight_names])


def _local_step(xs, p0, p1, tgt, norm_mix, norm_ffn, norm_ple, norm_kv, norm_final, a_w_s, a_b_s, b_sinks,
                f_conv_b, ple_b_gate, stage0, stage1, stage2, sched):
    tril = jnp.tril(jnp.ones((CHUNK, CHUNK), F32))
    wsm = (a_w_s[0] * tril[None]).astype(BF16)
    bsb = jnp.broadcast_to(a_b_s[0][:, :, None], (A_GROUPS, CHUNK, CHUNK))
    sinks = b_sinks[0]
    row = lambda a: a.reshape(1, -1)

    gv_full, cw_full, w_in, w_out = stage0()
    h1, zp = _mixer_a_fwd(xs, row(norm_mix[0]), gv_full, wsm, bsb, w_in, w_out)
    w_up0, w_dn0 = stage1((h1,))
    h2, hh0, c0 = _ffn_fwd(h1, row(norm_ffn[0]), cw_full[0], row(f_conv_b[0]), w_up0, w_dn0, 0)
    rest = stage2((h2,))
    w_pin, w_gate, w_kv_f, w_q, w_o = rest['w_pin'], rest['w_gate'], rest['w_kv'], rest['w_q'], rest['w_o']
    w_up = [w_up0, rest['w_up1']]
    w_dn = [w_dn0, rest['w_dn1']]
    h3, pe0, a0, kv = _ple_fwd_kv(h2, p0, row(norm_ple[0]), row(ple_b_gate[0]), row(norm_kv), w_pin[0], w_gate[0], w_kv_f)
    h4, q, ao, lse = _attn_fwd(h3, row(norm_mix[1]), kv, sinks, w_q, w_o)
    h5, hh1, c1 = _ffn_fwd(h4, row(norm_ffn[1]), cw_full[1], row(f_conv_b[1]), w_up[1], w_dn[1], 1)
    dh6, pe1, a1, loss_acc, dn_final = _ple_fwd_final(
        h5, p1, tgt, row(norm_ple[1]), row(ple_b_gate[1]), row(norm_final), w_pin[1], w_gate[1])

    hd = D_MODEL // 2

    def pieces(g):
        return g.reshape(N_SHARD, -1, g.shape[-1])

    def wgrad(a, b, bn, col_sharded, name, deps=()):
        return pieces(_wgrad(a, b, bn, col_sharded, name, deps=deps))

    dh5, g_pin1, g_gate1, dbg1, dnple1 = _ple_bwd(dh6, h5, pe1, a1, p1, row(norm_ple[1]), w_gate[1], 1)
    early = {('ple_w_in', 1): g_pin1, ('ple_w_gate', 1): pieces(g_gate1)}
    dh4, act1, dhh1, xf1, dcw1, dcb1, dnffn1 = _ffn_bwd(
        dh5, h4, hh1, c1, row(norm_ffn[1]), cw_full[1], w_up[1], w_dn[1], 1)
    early['f_w_down', 1] = wgrad(act1, dh5, hd, False, "wgrad_ffn_down1")
    early['f_w_up', 1] = wgrad(xf1, dhh1, FF_BLK, True, "wgrad_ffn_up1")
    dh3a, g_wq, g_wo, dkv, dsink, dnmix1 = _attn_bwd(dh4, h3, q, kv, ao, lse, row(norm_mix[1]), sinks, w_q, w_o)
    early['b_w_o', 0] = pieces(g_wo)
    early['b_w_q', 0] = pieces(g_wq)
    dh2, g_pin0, g_gate0, dbg0, dnple0, g_wkv, dnkv = _ple_bwd(
        dh3a, h2, pe0, a0, p0, row(norm_ple[0]), w_gate[0], 0, kv_args=(h3, dkv, row(norm_kv), w_kv_f))
    early['w_kv', 0] = pieces(g_wkv)
    early['ple_w_in', 0] = g_pin0
    early['ple_w_gate', 0] = pieces(g_gate0)
    deps = sched.early_ready(early)
    dh1, act0, dhh0, xf0, dcw0, dcb0, dnffn0 = _ffn_bwd(
        dh2, h1, hh0, c0, row(norm_ffn[0]), cw_full[0], w_up[0], w_dn[0], 0, deps=deps)
    deps = sched.after_ffn_bwd0((dh1,))
    g_dn0 = wgrad(act0, dh2, hd, False, "wgrad_ffn_down0", deps=deps)
    g_up0 = wgrad(xf0, dhh0, FF_BLK, True, "wgrad_ffn_up0")
    deps = sched.ffn0_ready({('f_w_down', 0): g_dn0, ('f_w_up', 0): g_up0})
    dx, g_win, g_wout, dws, dbs, dgv, dnmix0 = _mixer_a_bwd(
        dh1, xs, zp, row(norm_mix[0]), gv_full, wsm, bsb, tril, w_in, w_out, deps=deps)
    sched.after_mixer_bwd((dx,))
    g_wout = pieces(g_wout)

    small_grads = {
        'norm_mix': jnp.concatenate([dnmix0, dnmix1]), 'norm_ffn': jnp.concatenate([dnffn0, dnffn1]),
        'norm_ple': jnp.concatenate([dnple0, dnple1]), 'norm_kv': dnkv, 'norm_final': dn_final,
        'a_norm_v': dgv, 'a_w_s': dws, 'a_b_s': dbs[:, :, 0], 'b_sinks': dsink[0, :N_Q_HEADS],
        'f_conv_w': jnp.stack([dcw0, dcw1]), 'f_conv_b': jnp.concatenate([dcb0, dcb1]),
        'ple_b_gate': jnp.concatenate([dbg0, dbg1]),
    }
    outs = sched.finish({('a_w_in', 0): g_win, ('a_w_out', 0): g_wout}, small_grads, (g_wout,))
    return loss_acc, dx, outs


_SMALL_SHAPES = {
    'norm_mix': (2, D_MODEL), 'norm_ffn': (2, D_MODEL), 'norm_ple': (2, D_MODEL), 'norm_kv': (D_MODEL,),
    'norm_final': (D_MODEL,), 'a_norm_v': (1, D_MODEL), 'a_w_s': (1, A_GROUPS, CHUNK, CHUNK),
    'a_b_s': (1, A_GROUPS, CHUNK), 'b_sinks': (1, N_Q_HEADS), 'f_conv_w': (2, 3, N_FF),
    'f_conv_b': (2, N_FF), 'ple_b_gate': (2, D_MODEL),
}


class _Reducer:
    def __init__(self, given):
        self.given = given
        cx, cy, cc = _mesh_pos()
        self.shard = 2 * cx + cy
        s = self.shard
        self.ids = jnp.stack([cc, s, s ^ 2, s ^ 1, s ^ 3]).astype(jnp.int32)
        self.out = [{}, {}, {}, {}]
        self.stacked = {}

    def _send(self, tag, grads, small=None):
        keys = list(grads)
        srcs = [grads[k] for k in keys]
        shapes = [((N_SHARD, g.shape[1] // 2, g.shape[2]), F32) for g in srcs]
        if small is not None:
            srcs.append(small)
            shapes.append((small.shape, F32))
        return keys, _Exchange(f"send_{tag}", srcs, shapes, _send_to_sibling(len(srcs)), len(srcs))

    def _exchange(self, tag, keys, send, after, with_small=False):
        srcs, lands = send.finish(after)
        n = len(keys)
        parts = [_chip_partial(g, s, self.ids, f"chip_partial_{k[0]}{k[1]}")
                 for k, g, s in zip(keys, srcs[:n], lands[:n])]
        shapes = [(p.shape, BF16) for p in parts]
        if with_small:
            parts.append(_small_add(srcs[n], lands[n]))
            shapes.append(((3,) + parts[-1].shape, F32))
        exch = _Exchange(f"exch_{tag}", parts, shapes, _send_to_chips(len(parts)), 3 * len(parts))
        return (keys, srcs[:n], lands[:n], exch)

    def _swap(self, tag, state, after, with_small=False):
        keys, grads, sib, exch = state
        parts, recv = exch.finish(after)
        n = len(keys)
        own = [_chip_sum(g, s, q, self.ids, f"chip_sum_{k[0]}{k[1]}") for k, g, s, q in zip(keys, grads, sib, recv[:n])]
        small_red = _small_sum(parts[n], recv[n]) if with_small else None
        return keys, _Exchange(f"swap_{tag}", own, [(o.shape, F32) for o in own], _send_to_sibling(n), n), small_red

    def _adamw(self, keys, swap, after):
        own, sib = swap.finish(after)
        last = None
        for (name, layer), o, s in zip(keys, own, sib):
            w = self.given[name]
            n_layers = w.shape[0] if w.ndim == 3 else 1
            c2 = w.shape[-1]
            res = _adamw_halves(w.reshape(-1, c2), o, s, self.given['m_' + name].reshape(-1, c2),
                                self.given['v_' + name].reshape(-1, c2), self.ids, f"adamw_{name}{layer}",
                                layer, n_layers, self.stacked.get(name))
            self.stacked[name] = res
            if layer == 0:
                for dst, r in zip(self.out, res):
                    dst[name] = r.reshape(w.shape)
            last = res[0]
        return last

    def early_ready(self, grads):
        self.e_keys, self.e_send = self._send("e", grads)
        return (self.e_send.token,)

    def after_ffn_bwd0(self, after):
        self.e_state = self._exchange("e", self.e_keys, self.e_send, after)
        return (self.e_state[3].token,)

    def ffn0_ready(self, grads):
        self.f_keys, self.f_send = self._send("f", grads)
        _, self.e_swap, _ = self._swap("e", self.e_state, tuple(grads.values())[-1:])
        return (self.f_send.token, self.e_swap.token)

    def after_mixer_bwd(self, after):
        self.f_state = self._exchange("f", self.f_keys, self.f_send, after)
        self._adamw(self.e_keys, self.e_swap, (self.f_state[3].token,))
        return (self.f_state[3].token,)

    def finish(self, grads, small_grads, after):
        small_names = list(_SMALL_SHAPES)
        small_g = _pack([small_grads[k] for k in small_names])
        _, f_swap, _ = self._swap("f", self.f_state, after)
        a_keys, a_send = self._send("a", grads, small_g)
        a_state = self._exchange("a", a_keys, a_send, (), with_small=True)
        f_done = self._adamw(self.f_keys, f_swap, (a_state[3].token,))
        _, a_swap, small_red = self._swap("a", a_state, (f_done,), with_small=True)
        self._adamw(a_keys, a_swap, ())

        given, shard = self.given, self.shard
        out_g, out_d, out_m, out_v = self.out
        full_small = dict(zip(small_names, _unpack(small_red, [_SMALL_SHAPES[k] for k in small_names])))
        local_small = dict(full_small)
        local_small['a_norm_v'] = lax.dynamic_slice_in_dim(full_small['a_norm_v'], shard * 256, 256, axis=1)
        local_small['f_conv_w'] = lax.dynamic_slice_in_dim(full_small['f_conv_w'], shard * FF_BLK, FF_BLK, axis=2)
        sg = _pack([local_small[k] for k in small_names])
        sw = _pack([given[k] for k in small_names])
        sm = _pack([given['m_' + k] for k in small_names])
        sv = _pack([given['v_' + k] for k in small_names])
        sd, smn, svn = _adamw(sw, sg, sm, sv, "adamw_small")
        local_shapes = [given[k].shape for k in small_names]
        for dst, packed in ((out_d, sd), (out_m, smn), (out_v, svn)):
            dst.update(zip(small_names, _unpack(packed, local_shapes)))
        for k in small_names:
            out_g[k] = local_small[k].reshape(given[k].shape)
        return self.out


def _small_add(a, b):
    def body(a_ref, b_ref, o_ref):
        o_ref[...] = a_ref[...] + b_ref[...]

    return pl.pallas_call(body, name="chip_partial_small", out_shape=jax.ShapeDtypeStruct(a.shape, F32))(a, b)
```

```python
import functools
import math

import numpy as np
import jax
import jax.numpy as jnp
from jax import lax
from jax.experimental import pallas as pl
from jax.experimental.pallas import tpu as pltpu

F32 = jnp.float32
BF16 = jnp.bfloat16

D_MODEL = 1024
CHUNK = 128
A_GROUPS = 8
HEAD_DIM = 64
N_Q_HEADS = 16
N_KV_HEADS = 4
GQA_GROUP = N_Q_HEADS // N_KV_HEADS
KV_DIM = N_KV_HEADS * HEAD_DIM
BLOCK = 128
D_FF = 2816
N_FF = 2 * D_FF
FF_BLK = N_FF // 4
PLE_DIM = 256
EPS = 1e-6
NEG = -1e30
N_SHARD = 4

ADAM_LR = 0.001
ADAM_B1 = 0.9
ADAM_B2 = 0.999
ADAM_EPS = 1e-08
ADAM_WD = 0.01
ADAM_STEP = 10

VMEM_LIMIT = 60 * 1024 * 1024
MESH = pl.DeviceIdType.MESH
ANY = pl.BlockSpec(memory_space=pl.ANY)
SMEM = pl.BlockSpec(memory_space=pltpu.SMEM)

_SLOPES = [float(np.float32(2.0 ** (-8.0 * (h + 1) / N_Q_HEADS))) for h in range(N_Q_HEADS)]


def _dot(a, b):
    return jnp.dot(a, b, preferred_element_type=F32)


def _dot_nt(a, b):
    return lax.dot_general(a, b, (((1,), (1,)), ((), ())), preferred_element_type=F32)


def _dot_tn(a, b):
    return lax.dot_general(a, b, (((0,), (0,)), ((), ())), preferred_element_type=F32)


def _rms(x, g):
    r = lax.rsqrt(jnp.mean(x * x, axis=-1, keepdims=True) + EPS)
    xh = x * r
    return xh * g, xh, r


def _rms_bwd(dy, xh, r, g):
    dxh = dy * g
    dg = jnp.sum(dy * xh, axis=0, keepdims=True)
    dx = r * (dxh - xh * jnp.mean(dxh * xh, axis=-1, keepdims=True))
    return dx, dg


_GELU_C = math.sqrt(2.0 / math.pi)


def _gelu(x):
    t = jnp.tanh(_GELU_C * (x + 0.044715 * (x * x * x)))
    return 0.5 * x * (1.0 + t)


def _gelu_grad(x):
    x2 = x * x
    t = jnp.tanh(_GELU_C * (x + 0.044715 * (x2 * x)))
    return 0.5 * (1.0 + t) + 0.5 * x * (1.0 - t * t) * (_GELU_C * (1.0 + 3.0 * 0.044715 * x2))


def _sigmoid(x):
    return 0.5 * jnp.tanh(0.5 * x) + 0.5


def _load_once(pairs, sem):
    @pl.when(pl.program_id(0) == 0)
    def _():
        cps = [pltpu.make_async_copy(s, d, sem.at[i]) for i, (s, d) in enumerate(pairs)]
        for cp in cps:
            cp.start()
        for cp in cps:
            cp.wait()


def _params(n_axes=1, vmem=VMEM_LIMIT):
    return pltpu.CompilerParams(dimension_semantics=("arbitrary",) * n_axes, vmem_limit_bytes=vmem)


def _row_spec(tm, n, rev_nt=None):
    if rev_nt is None:
        return pl.BlockSpec((tm, n), lambda i: (i, 0))
    return pl.BlockSpec((tm, n), lambda i: (rev_nt - 1 - i, 0))


def _const_spec(shape):
    nd = len(shape)
    return pl.BlockSpec(shape, lambda i: (0,) * nd)


def _add_deps(body, in_specs, args, deps):
    nd = len(deps)
    if nd == 0:
        return body, list(in_specs), list(args)

    def wrapped(*refs):
        return body(*refs[nd:])

    return wrapped, [ANY] * nd + list(in_specs), list(deps) + list(args)


def _zero_first(refs):
    @pl.when(pl.program_id(0) == 0)
    def _():
        for r in refs:
            r[...] = jnp.zeros(r.shape, r.dtype)


def _mixer_a_fwd(x, nmix, gv, wsm, bsb, w_in, w_out):
    T = x.shape[0]
    tm = min(512, T)
    nt = T // tm
    nw = 2 * D_MODEL // N_SHARD

    def body(x_ref, nmix_ref, gv_ref, ws_ref, bsb_ref, w_in_hbm, w_out_hbm,
             h1_ref, zp_ref, w_in_v, w_out_v, gated_v, sem):
        _load_once([(w_in_hbm, w_in_v), (w_out_hbm, w_out_v)], sem)
        xv = x_ref[...]
        xn = _rms(xv, nmix_ref[...])[0].astype(BF16)
        for j in range(N_SHARD):
            zp_ref[:, j * nw:(j + 1) * nw] = _dot(xn, w_in_v[j])
        z = _gelu(zp_ref[...])
        u = z[:, :D_MODEL]
        vn = _rms(z[:, D_MODEL:], gv_ref[...])[0].astype(BF16)
        for c in range(tm // CHUNK):
            rows = slice(c * CHUNK, (c + 1) * CHUNK)
            for h in range(A_GROUPS):
                cols = slice(h * CHUNK, (h + 1) * CHUNK)
                s = _dot(ws_ref[h], vn[rows, cols]) + bsb_ref[h]
                gated_v[rows, cols] = (u[rows, cols] * s).astype(BF16)
        h1_ref[...] = xv + _dot(gated_v[...], w_out_v[...])

    return pl.pallas_call(
        body, name="mixer_a_fwd", grid=(nt,),
        in_specs=[_row_spec(tm, D_MODEL), _const_spec((1, D_MODEL)), _const_spec((1, D_MODEL)),
                  _const_spec((A_GROUPS, CHUNK, CHUNK)), _const_spec((A_GROUPS, CHUNK, CHUNK)), ANY, ANY],
        out_specs=[_row_spec(tm, D_MODEL), _row_spec(tm, 2 * D_MODEL)],
        out_shape=[jax.ShapeDtypeStruct((T, D_MODEL), F32), jax.ShapeDtypeStruct((T, 2 * D_MODEL), F32)],
        scratch_shapes=[pltpu.VMEM((N_SHARD, D_MODEL, nw), BF16), pltpu.VMEM((D_MODEL, D_MODEL), BF16),
                        pltpu.VMEM((tm, D_MODEL), BF16), pltpu.SemaphoreType.DMA((2,))],
        compiler_params=_params(),
    )(x, nmix, gv, wsm, bsb, w_in, w_out)


def _mixer_a_bwd(dh, x, zp, nmix, gv, wsm, bsb, tril, w_in, w_out, deps=()):
    T = x.shape[0]
    tm = min(256, T)
    nt = T // tm
    nw = 2 * D_MODEL // N_SHARD

    def body(dh_ref, x_ref, zp_ref, nmix_ref, gv_ref, ws_ref, bsb_ref, tril_ref, w_in_hbm, w_out_hbm,
             dx_ref, dwin_ref, dwout_ref, dws_ref, dbs_ref, dgv_ref, dnmix_ref,
             w_in_v, w_out_v, du_v, dvn_v, dbs_v, gated_ref, sem):
        _load_once([(w_in_hbm, w_in_v), (w_out_hbm, w_out_v)], sem)
        _zero_first([dws_ref, dbs_v, dgv_ref, dnmix_ref, dwin_ref, dwout_ref])
        i = pl.program_id(0)
        dhv = dh_ref[...]
        dhb = dhv.astype(BF16)
        xv = x_ref[...]
        xn, xh, r = _rms(xv, nmix_ref[...])
        xnb = xn.astype(BF16)
        zpv = zp_ref[...]
        z = _gelu(zpv)
        u = z[:, :D_MODEL]
        vn_f, vh, rv = _rms(z[:, D_MODEL:], gv_ref[...])
        vn = vn_f.astype(BF16)
        dgated = _dot_nt(dhb, w_out_v[...])
        for c in range(tm // CHUNK):
            rows = slice(c * CHUNK, (c + 1) * CHUNK)
            for h in range(A_GROUPS):
                cols = slice(h * CHUNK, (h + 1) * CHUNK)
                vn_h = vn[rows, cols]
                s = _dot(ws_ref[h], vn_h) + bsb_ref[h]
                dgt = dgated[rows, cols]
                u_h = u[rows, cols]
                gated_ref[rows, cols] = (u_h * s).astype(BF16)
                du_v[rows, cols] = dgt * s
                ds = dgt * u_h
                dsb = ds.astype(BF16)
                dws_ref[h] += _dot_nt(dsb, vn_h)
                dbs_v[h] += ds
                dvn_v[rows, cols] = _dot_tn(ws_ref[h], dsb)
        dwout_ref[...] += _dot_tn(gated_ref[...], dhb)
        dv, dgv = _rms_bwd(dvn_v[...], vh, rv, gv_ref[...])
        dgv_ref[...] += dgv
        dzu = (du_v[...] * _gelu_grad(zpv[:, :D_MODEL])).astype(BF16)
        dzv = (dv * _gelu_grad(zpv[:, D_MODEL:])).astype(BF16)
        dzs = (dzu[:, :nw], dzu[:, nw:], dzv[:, :nw], dzv[:, nw:])
        dxn = jnp.zeros((tm, D_MODEL), F32)
        for j in range(N_SHARD):
            dxn = dxn + _dot_nt(dzs[j], w_in_v[j])
            dwin_ref[j] += _dot_tn(xnb, dzs[j])
        dxx, dn = _rms_bwd(dxn, xh, r, nmix_ref[...])
        dnmix_ref[...] += dn
        dx_ref[...] = dhv + dxx

        @pl.when(i == nt - 1)
        def _():
            for h in range(A_GROUPS):
                dws_ref[h] = dws_ref[h] * tril_ref[...]
                dbs_ref[h] = jnp.broadcast_to(jnp.sum(dbs_v[h], axis=1, keepdims=True), (CHUNK, CHUNK))

    grp = (A_GROUPS, CHUNK, CHUNK)
    body, in_specs, args = _add_deps(
        body, [_row_spec(tm, D_MODEL), _row_spec(tm, D_MODEL), _row_spec(tm, 2 * D_MODEL),
               _const_spec((1, D_MODEL)), _const_spec((1, D_MODEL)), _const_spec(grp), _const_spec(grp),
               _const_spec((CHUNK, CHUNK)), ANY, ANY],
        [dh, x, zp, nmix, gv, wsm, bsb, tril, w_in, w_out], deps)
    return pl.pallas_call(
        body, name="mixer_a_bwd", grid=(nt,), in_specs=in_specs,
        out_specs=[_row_spec(tm, D_MODEL), _const_spec((N_SHARD, D_MODEL, nw)), _const_spec((D_MODEL, D_MODEL)),
                   _const_spec(grp), _const_spec(grp), _const_spec((1, D_MODEL)), _const_spec((1, D_MODEL))],
        out_shape=[jax.ShapeDtypeStruct((T, D_MODEL), F32), jax.ShapeDtypeStruct((N_SHARD, D_MODEL, nw), F32),
                   jax.ShapeDtypeStruct((D_MODEL, D_MODEL), F32),
                   jax.ShapeDtypeStruct(grp, F32), jax.ShapeDtypeStruct(grp, F32),
                   jax.ShapeDtypeStruct((1, D_MODEL), F32), jax.ShapeDtypeStruct((1, D_MODEL), F32)],
        scratch_shapes=[pltpu.VMEM((N_SHARD, D_MODEL, nw), BF16), pltpu.VMEM((D_MODEL, D_MODEL), BF16),
                        pltpu.VMEM((tm, D_MODEL), F32), pltpu.VMEM((tm, D_MODEL), F32),
                        pltpu.VMEM(grp, F32), pltpu.VMEM((tm, D_MODEL), BF16), pltpu.SemaphoreType.DMA((2,))],
        compiler_params=_params(),
    )(*args)


def _load_ffn_weights(w_up_hbm, w_dn_hbm, layer, w_up_v, w_dn_v, sem):
    _load_once([(w_up_hbm, w_up_v), (w_dn_hbm, w_dn_v)], sem)


def _ffn_fwd(h, nffn, cw, cb, w_up, w_dn, layer):
    T = h.shape[0]
    tm = min(256, T)
    nt = T // tm

    def body(h_ref, n_ref, cw_ref, cb_ref, w_up_hbm, w_dn_hbm, out_ref, hh_ref, c_ref,
             w_up_v, w_dn_v, carry_v, sem):
        _load_ffn_weights(w_up_hbm, w_dn_hbm, layer, w_up_v, w_dn_v, sem)
        _zero_first([carry_v])
        xv = h_ref[...]
        xf = _rms(xv, n_ref[...])[0].astype(BF16)
        acc = xv
        for j in range(2):
            cs = []
            for blk in (j, j + 2):
                cols = slice(blk * FF_BLK, (blk + 1) * FF_BLK)
                hh = _dot(xf, w_up_v[blk])
                hh_ref[:, cols] = hh.astype(BF16)
                ext = jnp.concatenate([carry_v[blk], hh], axis=0)
                carry_v[blk] = hh[tm - 8:, :]
                s1 = pltpu.roll(ext, 1, 0)[8:]
                s2 = pltpu.roll(ext, 2, 0)[8:]
                cv = (cb_ref[:, cols] + cw_ref[0:1, cols] * s2 + cw_ref[1:2, cols] * s1
                      + cw_ref[2:3, cols] * hh)
                c_ref[:, cols] = cv.astype(BF16)
                cs.append(cv)
            act = (cs[0] * _sigmoid(cs[0]) * cs[1]).astype(BF16)
            acc = acc + _dot(act, w_dn_v[j * FF_BLK:(j + 1) * FF_BLK, :])
        out_ref[...] = acc

    return pl.pallas_call(
        body, name=f"ffn_fwd{layer}", grid=(nt,),
        in_specs=[_row_spec(tm, D_MODEL), _const_spec((1, D_MODEL)), _const_spec((3, N_FF)),
                  _const_spec((1, N_FF)), ANY, ANY],
        out_specs=[_row_spec(tm, D_MODEL), _row_spec(tm, N_FF), _row_spec(tm, N_FF)],
        out_shape=[jax.ShapeDtypeStruct((T, D_MODEL), F32), jax.ShapeDtypeStruct((T, N_FF), BF16),
                   jax.ShapeDtypeStruct((T, N_FF), BF16)],
        scratch_shapes=[pltpu.VMEM((N_SHARD, D_MODEL, FF_BLK), BF16), pltpu.VMEM((D_FF, D_MODEL), BF16),
                        pltpu.VMEM((N_SHARD, 8, FF_BLK), F32), pltpu.SemaphoreType.DMA((2 * N_SHARD,))],
        compiler_params=_params(),
    )(h, nffn, cw, cb, w_up, w_dn)


def _wgrad(a, b, bn, col_sharded, name, deps=()):
    T, K = a.shape
    N = b.shape[1]
    tt = min(1024, T)
    nn, ntt = N // bn, T // tt
    kr = K // N_SHARD

    def body(a_ref, b_ref, o_ref):
        @pl.when(pl.program_id(1) == 0)
        def _():
            o_ref[...] = jnp.zeros(o_ref.shape, F32)
        d = _dot_tn(a_ref[...].astype(BF16), b_ref[...].astype(BF16))
        if col_sharded:
            o_ref[...] += d
        else:
            for j in range(N_SHARD):
                o_ref[j] += d[j * kr:(j + 1) * kr]

    if col_sharded:
        assert nn == N_SHARD
        out_spec = pl.BlockSpec((None, K, bn), lambda n, t: (n, 0, 0))
        out_shape = jax.ShapeDtypeStruct((N_SHARD, K, bn), F32)
    else:
        out_spec = pl.BlockSpec((N_SHARD, kr, bn), lambda n, t: (0, 0, n))
        out_shape = jax.ShapeDtypeStruct((N_SHARD, kr, N), F32)
    body, in_specs, args = _add_deps(
        body, [pl.BlockSpec((tt, K), lambda n, t: (t, 0)), pl.BlockSpec((tt, bn), lambda n, t: (t, n))],
        [a, b], deps)
    return pl.pallas_call(
        body, name=name, grid=(nn, ntt), in_specs=in_specs, out_specs=out_spec, out_shape=out_shape,
        compiler_params=pltpu.CompilerParams(dimension_semantics=("arbitrary",) * 2, vmem_limit_bytes=VMEM_LIMIT),
    )(*args)


def _ffn_bwd(dh, h, hh, c, nffn, cw, w_up, w_dn, layer, deps=(), between=None):
    T = h.shape[0]
    tm = min(256, T)
    nt = T // tm

    def body(dh_ref, h_ref, hh_ref, c_ref, n_ref, cw_ref, w_up_hbm, w_dn_hbm,
             dhin_ref, act_ref, dhh_ref, xf_ref, dcw_ref, dcb_ref, dn_ref,
             w_up_v, w_dn_v, carry_v, sem):
        _load_ffn_weights(w_up_hbm, w_dn_hbm, layer, w_up_v, w_dn_v, sem)
        _zero_first([carry_v, dcw_ref, dcb_ref, dn_ref])
        dout = dh_ref[...]
        doutb = dout.astype(BF16)
        xf_f, xh, r = _rms(h_ref[...], n_ref[...])
        xf_ref[...] = xf_f.astype(BF16)
        dxf = jnp.zeros((tm, D_MODEL), F32)
        for j in range(2):
            blks = (j, j + 2)
            cg = c_ref[:, j * FF_BLK:(j + 1) * FF_BLK].astype(F32)
            cu = c_ref[:, (j + 2) * FF_BLK:(j + 3) * FF_BLK].astype(F32)
            sg = _sigmoid(cg)
            sil = cg * sg
            act_ref[:, j * FF_BLK:(j + 1) * FF_BLK] = (sil * cu).astype(BF16)
            dact = _dot_nt(doutb, w_dn_v[j * FF_BLK:(j + 1) * FF_BLK, :])
            dcs = (dact * cu * (sg * (1.0 + cg * (1.0 - sg))), dact * sil)
            for blk, dc in zip(blks, dcs):
                cols = slice(blk * FF_BLK, (blk + 1) * FF_BLK)
                hhv = hh_ref[:, cols].astype(F32)
                ext = jnp.concatenate([dc, carry_v[blk]], axis=0)
                carry_v[blk] = dc[:8, :]
                n = tm + 8
                a1 = pltpu.roll(ext, n - 1, 0)[:tm]
                a2 = pltpu.roll(ext, n - 2, 0)[:tm]
                dcb_ref[:, cols] += jnp.sum(dc, axis=0, keepdims=True)
                dcw_ref[0:1, cols] += jnp.sum(a2 * hhv, axis=0, keepdims=True)
                dcw_ref[1:2, cols] += jnp.sum(a1 * hhv, axis=0, keepdims=True)
                dcw_ref[2:3, cols] += jnp.sum(dc * hhv, axis=0, keepdims=True)
                dhh = (cw_ref[2:3, cols] * dc + cw_ref[1:2, cols] * a1 + cw_ref[0:1, cols] * a2).astype(BF16)
                dhh_ref[:, cols] = dhh
                dxf = dxf + _dot_nt(dhh, w_up_v[blk])
        dxx, dn = _rms_bwd(dxf, xh, r, n_ref[...])
        dn_ref[...] += dn
        dhin_ref[...] = dout + dxx

    rev = functools.partial(_row_spec, rev_nt=nt)
    body, in_specs, args = _add_deps(
        body, [rev(tm, D_MODEL), rev(tm, D_MODEL), rev(tm, N_FF), rev(tm, N_FF),
               _const_spec((1, D_MODEL)), _const_spec((3, N_FF)), ANY, ANY],
        [dh, h, hh, c, nffn, cw, w_up, w_dn], deps)
    dhin, act, dhh, xf, dcw, dcb, dn = pl.pallas_call(
        body, name=f"ffn_bwd{layer}", grid=(nt,), in_specs=in_specs,
        out_specs=[rev(tm, D_MODEL), rev(tm, D_FF), rev(tm, N_FF), rev(tm, D_MODEL),
                   _const_spec((3, N_FF)), _const_spec((1, N_FF)), _const_spec((1, D_MODEL))],
        out_shape=[jax.ShapeDtypeStruct((T, D_MODEL), F32), jax.ShapeDtypeStruct((T, D_FF), BF16),
                   jax.ShapeDtypeStruct((T, N_FF), BF16), jax.ShapeDtypeStruct((T, D_MODEL), BF16),
                   jax.ShapeDtypeStruct((3, N_FF), F32), jax.ShapeDtypeStruct((1, N_FF), F32),
                   jax.ShapeDtypeStruct((1, D_MODEL), F32)],
        scratch_shapes=[pltpu.VMEM((N_SHARD, D_MODEL, FF_BLK), BF16), pltpu.VMEM((D_FF, D_MODEL), BF16),
                        pltpu.VMEM((N_SHARD, 8, FF_BLK), F32), pltpu.SemaphoreType.DMA((2 * N_SHARD,))],
        compiler_params=_params(),
    )(*args)
    deps2 = between(dhin) if between is not None else ()
    dwdn = _wgrad(act, dh, D_MODEL // 2, False, f"wgrad_ffn_down{layer}", deps=deps2)
    dwup = _wgrad(xf, dhh, FF_BLK, True, f"wgrad_ffn_up{layer}")
    return dhin, dwup, dwdn, dcw, dcb, dn


def _load_ple_weights(w_pin_hbm, w_gate_hbm, layer, w_pin_v, w_gate_v, sem, extra=()):
    _load_once([(w_pin_hbm, w_pin_v), (w_gate_hbm, w_gate_v)] + list(extra), sem)


def _ple_fwd_kv(h, p, nple, bg, nkv, w_pin, w_gate, w_kv):
    T = h.shape[0]
    tm = min(512, T)
    nt = T // tm
    pw = D_MODEL // N_SHARD

    def body(h_ref, p_ref, n_ref, bg_ref, nkv_ref, w_pin_hbm, w_gate_hbm, w_kv_hbm,
             out_ref, pe_ref, a_ref, kv_ref, w_pin_v, w_gate_v, w_kv_v, sem):
        _load_ple_weights(w_pin_hbm, w_gate_hbm, 0, w_pin_v, w_gate_v, sem, [(w_kv_hbm, w_kv_v)])
        xv = h_ref[...]
        xg = _rms(xv, n_ref[...])[0].astype(BF16)
        a = _dot(xg, w_gate_v[...]) + bg_ref[...]
        a_ref[...] = a
        pb = p_ref[...].astype(BF16)
        for j in range(N_SHARD):
            pe_ref[:, j * pw:(j + 1) * pw] = _dot(pb, w_pin_v[j])
        hn = xv + pe_ref[...] * _sigmoid(a)
        out_ref[...] = hn
        kvn = _rms(hn, nkv_ref[...])[0].astype(BF16)
        kv_ref[...] = _dot(kvn, w_kv_v[...]).astype(BF16)

    vec = _const_spec((1, D_MODEL))
    return pl.pallas_call(
        body, name="ple_fwd0", grid=(nt,),
        in_specs=[_row_spec(tm, D_MODEL), _row_spec(tm, PLE_DIM), vec, vec, vec, ANY, ANY, ANY],
        out_specs=[_row_spec(tm, D_MODEL), _row_spec(tm, D_MODEL), _row_spec(tm, D_MODEL),
                   _row_spec(tm, 2 * KV_DIM)],
        out_shape=[jax.ShapeDtypeStruct((T, D_MODEL), F32), jax.ShapeDtypeStruct((T, D_MODEL), F32),
                   jax.ShapeDtypeStruct((T, D_MODEL), F32), jax.ShapeDtypeStruct((T, 2 * KV_DIM), BF16)],
        scratch_shapes=[pltpu.VMEM((N_SHARD, PLE_DIM, pw), BF16), pltpu.VMEM((D_MODEL, D_MODEL), BF16),
                        pltpu.VMEM((D_MODEL, 2 * KV_DIM), BF16), pltpu.SemaphoreType.DMA((2 * N_SHARD + 1,))],
        compiler_params=_params(),
    )(h, p, nple, bg, nkv, w_pin, w_gate, w_kv)


def _ple_fwd_final(h, p, tgt, nple, bg, nfin, w_pin, w_gate):
    T = h.shape[0]
    tm = min(512, T)
    nt = T // tm
    pw = D_MODEL // N_SHARD

    def body(h_ref, p_ref, t_ref, n_ref, bg_ref, nf_ref, w_pin_hbm, w_gate_hbm,
             dh_ref, pe_ref, a_ref, loss_ref, dnf_ref, w_pin_v, w_gate_v, sem):
        _load_ple_weights(w_pin_hbm, w_gate_hbm, 1, w_pin_v, w_gate_v, sem)
        _zero_first([loss_ref, dnf_ref])
        xv = h_ref[...]
        xg = _rms(xv, n_ref[...])[0].astype(BF16)
        a = _dot(xg, w_gate_v[...]) + bg_ref[...]
        a_ref[...] = a
        pb = p_ref[...].astype(BF16)
        for j in range(N_SHARD):
            pe_ref[:, j * pw:(j + 1) * pw] = _dot(pb, w_pin_v[j])
        hn = xv + pe_ref[...] * _sigmoid(a)
        y, yh, r = _rms(hn, nf_ref[...])
        diff = y - t_ref[...]
        loss_ref[...] += 0.5 * jnp.sum(jnp.mean(diff * diff, axis=-1, keepdims=True))
        dy = diff * (1.0 / D_MODEL)
        dhn, dnf = _rms_bwd(dy, yh, r, nf_ref[...])
        dnf_ref[...] += dnf
        dh_ref[...] = dhn

    vec = _const_spec((1, D_MODEL))
    return pl.pallas_call(
        body, name="ple_fwd1", grid=(nt,),
        in_specs=[_row_spec(tm, D_MODEL), _row_spec(tm, PLE_DIM), _row_spec(tm, D_MODEL), vec, vec, vec, ANY, ANY],
        out_specs=[_row_spec(tm, D_MODEL), _row_spec(tm, D_MODEL), _row_spec(tm, D_MODEL),
                   _const_spec((8, 128)), vec],
        out_shape=[jax.ShapeDtypeStruct((T, D_MODEL), F32), jax.ShapeDtypeStruct((T, D_MODEL), F32),
                   jax.ShapeDtypeStruct((T, D_MODEL), F32), jax.ShapeDtypeStruct((8, 128), F32),
                   jax.ShapeDtypeStruct((1, D_MODEL), F32)],
        scratch_shapes=[pltpu.VMEM((N_SHARD, PLE_DIM, pw), BF16), pltpu.VMEM((D_MODEL, D_MODEL), BF16),
                        pltpu.SemaphoreType.DMA((2 * N_SHARD,))],
        compiler_params=_params(),
    )(h, p, tgt, nple, bg, nfin, w_pin, w_gate)


def _ple_bwd(dh, hb, pe, a, p, nple, w_gate, layer, kv_args=None):
    T = hb.shape[0]
    tm = min(512, T)
    nt = T // tm
    with_kv = kv_args is not None
    pw = D_MODEL // N_SHARD

    def body(*refs):
        if with_kv:
            (dh_ref, hb_ref, pe_ref, a_ref, p_ref, n_ref, w_gate_hbm, hc_ref, dkv_ref, nkv_ref, w_kv_hbm,
             dhb_ref, dwpin_ref, dwgate_ref, dbg_ref, dn_ref, dwkv_ref, dnkv_ref,
             w_gate_v, w_kv_v, sem) = refs
        else:
            (dh_ref, hb_ref, pe_ref, a_ref, p_ref, n_ref, w_gate_hbm,
             dhb_ref, dwpin_ref, dwgate_ref, dbg_ref, dn_ref, w_gate_v, sem) = refs
        pairs = [(w_gate_hbm, w_gate_v)]
        if with_kv:
            pairs.append((w_kv_hbm, w_kv_v))
        _load_once(pairs, sem)
        _zero_first([dwpin_ref, dwgate_ref, dbg_ref, dn_ref] + ([dwkv_ref, dnkv_ref] if with_kv else []))
        do = dh_ref[...]
        if with_kv:
            dkvb = dkv_ref[...].astype(BF16)
            dkvn = _dot_nt(dkvb, w_kv_v[...])
            kvn, kh, kr = _rms(hc_ref[...], nkv_ref[...])
            dwkv_ref[...] += _dot_tn(kvn.astype(BF16), dkvb)
            dk, dnkv = _rms_bwd(dkvn, kh, kr, nkv_ref[...])
            dnkv_ref[...] += dnkv
            do = do + dk
        gate = _sigmoid(a_ref[...])
        dpe = (do * gate).astype(BF16)
        pb = p_ref[...].astype(BF16)
        for j in range(N_SHARD):
            dwpin_ref[j] += _dot_tn(pb, dpe[:, j * pw:(j + 1) * pw])
        da = do * pe_ref[...] * (gate * (1.0 - gate))
        dab = da.astype(BF16)
        dbg_ref[...] += jnp.sum(da, axis=0, keepdims=True)
        dxg = _dot_nt(dab, w_gate_v[...])
        xg, xh, r = _rms(hb_ref[...], n_ref[...])
        dwgate_ref[...] += _dot_tn(xg.astype(BF16), dab)
        dxx, dn = _rms_bwd(dxg, xh, r, n_ref[...])
        dn_ref[...] += dn
        dhb_ref[...] = do + dxx

    vec = _const_spec((1, D_MODEL))
    row = _row_spec(tm, D_MODEL)
    in_specs = [row, row, row, row, _row_spec(tm, PLE_DIM), vec, ANY]
    args = [dh, hb, pe, a, p, nple, w_gate]
    out_specs = [row, _const_spec((N_SHARD, PLE_DIM, pw)), _const_spec((D_MODEL, D_MODEL)), vec, vec]
    out_shape = [jax.ShapeDtypeStruct((T, D_MODEL), F32), jax.ShapeDtypeStruct((N_SHARD, PLE_DIM, pw), F32),
                 jax.ShapeDtypeStruct((D_MODEL, D_MODEL), F32),
                 jax.ShapeDtypeStruct((1, D_MODEL), F32), jax.ShapeDtypeStruct((1, D_MODEL), F32)]
    scratch = [pltpu.VMEM((D_MODEL, D_MODEL), BF16)]
    if with_kv:
        hc, dkv, nkv, w_kv = kv_args
        in_specs += [row, _row_spec(tm, 2 * KV_DIM), vec, ANY]
        args += [hc, dkv, nkv, w_kv]
        out_specs += [_const_spec((D_MODEL, 2 * KV_DIM)), vec]
        out_shape += [jax.ShapeDtypeStruct((D_MODEL, 2 * KV_DIM), F32), jax.ShapeDtypeStruct((1, D_MODEL), F32)]
        scratch.append(pltpu.VMEM((D_MODEL, 2 * KV_DIM), BF16))
    scratch.append(pltpu.SemaphoreType.DMA((N_SHARD + 1,)))
    return pl.pallas_call(
        body, name=f"ple_bwd{layer}", grid=(nt,), in_specs=in_specs, out_specs=out_specs,
        out_shape=out_shape, scratch_shapes=scratch, compiler_params=_params(),
    )(*args)


def _band_masks(is_first):
    ii = lax.broadcasted_iota(jnp.int32, (BLOCK, 2 * BLOCK), 0)
    jj = lax.broadcasted_iota(jnp.int32, (BLOCK, 2 * BLOCK), 1)
    dist = ii + BLOCK - jj
    valid = (dist >= 0) & (dist < BLOCK) & ((jj >= BLOCK) | jnp.logical_not(is_first))
    return dist.astype(F32), valid


def _attn_fwd(h, nmix, kv, sinks, w_q, w_o):
    T = h.shape[0]
    tm = min(512, T)
    nt = T // tm
    nb = tm // BLOCK

    def body(h_ref, n_ref, kv_ref, kvp_ref, sink_ref, w_q_hbm, w_o_hbm,
             out_ref, q_ref, ao_ref, lse_ref, w_q_v, w_o_v, kvs_v, sem):
        _load_once([(w_q_hbm, w_q_v), (w_o_hbm, w_o_v)], sem)
        ti = pl.program_id(0)
        xv = h_ref[...]
        xn = _rms(xv, n_ref[...])[0].astype(BF16)
        q_ref[...] = (_dot(xn, w_q_v[...]) * (HEAD_DIM ** -0.5)).astype(BF16)
        kvs_v[0:BLOCK, :] = kvp_ref[...]
        kvs_v[BLOCK:, :] = kv_ref[...]
        lane = lax.broadcasted_iota(jnp.int32, (BLOCK, 128), 1)

        def blk_body(b, carry):
            r0 = pl.multiple_of(b * BLOCK, BLOCK)
            distf, valid = _band_masks(jnp.logical_and(ti == 0, b == 0))
            qb = q_ref[pl.ds(r0, BLOCK), :]
            band = kvs_v[pl.ds(r0, 2 * BLOCK), :]
            lse_mat = jnp.zeros((BLOCK, 128), F32)
            outs = []
            for hq in range(N_Q_HEADS):
                kh = hq // GQA_GROUP
                k_h = band[:, kh * HEAD_DIM:(kh + 1) * HEAD_DIM]
                v_h = band[:, KV_DIM + kh * HEAD_DIM:KV_DIM + (kh + 1) * HEAD_DIM]
                s = _dot_nt(qb[:, hq * HEAD_DIM:(hq + 1) * HEAD_DIM], k_h) - _SLOPES[hq] * distf
                s = jnp.where(valid, s, NEG)
                sink = sink_ref[hq]
                m = jnp.maximum(jnp.max(s, axis=1, keepdims=True), sink)
                e = jnp.exp(s - m)
                den = jnp.sum(e, axis=1, keepdims=True) + jnp.exp(sink - m)
                outs.append(_dot((e / den).astype(BF16), v_h))
                lse_mat = jnp.where(lane == hq, m + jnp.log(den), lse_mat)
            ao_ref[pl.ds(r0, BLOCK), :] = jnp.concatenate(outs, axis=1).astype(BF16)
            lse_ref[pl.ds(r0, BLOCK), :] = lse_mat
            return carry

        lax.fori_loop(0, nb, blk_body, 0)
        out_ref[...] = xv + _dot(ao_ref[...], w_o_v[...])

    row = _row_spec(tm, D_MODEL)
    prev_spec = pl.BlockSpec((BLOCK, 2 * KV_DIM), lambda i: (jnp.maximum(i * nb - 1, 0), 0))
    return pl.pallas_call(
        body, name="attn_fwd", grid=(nt,),
        in_specs=[row, _const_spec((1, D_MODEL)), _row_spec(tm, 2 * KV_DIM), prev_spec, SMEM, ANY, ANY],
        out_specs=[row, row, row, _row_spec(tm, 128)],
        out_shape=[jax.ShapeDtypeStruct((T, D_MODEL), F32), jax.ShapeDtypeStruct((T, D_MODEL), BF16),
                   jax.ShapeDtypeStruct((T, D_MODEL), BF16), jax.ShapeDtypeStruct((T, 128), F32)],
        scratch_shapes=[pltpu.VMEM((D_MODEL, D_MODEL), BF16), pltpu.VMEM((D_MODEL, D_MODEL), BF16),
                        pltpu.VMEM((tm + BLOCK, 2 * KV_DIM), BF16), pltpu.SemaphoreType.DMA((2,))],
        compiler_params=_params(),
    )(h, nmix, kv, kv, sinks, w_q, w_o)


def _attn_bwd(dh, h, q, kv, ao, lse, nmix, sinks, w_q, w_o):
    T = h.shape[0]
    tm = min(512, T)
    nt = T // tm
    nb = tm // BLOCK

    def body(dh_ref, h_ref, q_ref, kv_ref, kvp_ref, ao_ref, lse_ref, n_ref, sink_ref, w_q_hbm, w_o_hbm,
             dhin_ref, dwq_ref, dwo_ref, dkv_ref, dsink_ref, dn_ref,
             w_q_v, w_o_v, kvs_v, dao_v, dq_v, dkv_v, carry_v, sem):
        _load_once([(w_q_hbm, w_q_v), (w_o_hbm, w_o_v)], sem)
        _zero_first([carry_v, dsink_ref, dn_ref, dwq_ref, dwo_ref])
        ti = nt - 1 - pl.program_id(0)
        dout = dh_ref[...]
        doutb = dout.astype(BF16)
        dao_v[...] = _dot_nt(doutb, w_o_v[...])
        dwo_ref[...] += _dot_tn(ao_ref[...], doutb)
        kvs_v[0:BLOCK, :] = kvp_ref[...]
        kvs_v[BLOCK:, :] = kv_ref[...]
        dkv_v[0:tm, :] = jnp.zeros((tm, 2 * KV_DIM), F32)
        dkv_v[tm:, :] = carry_v[...]
        lane = lax.broadcasted_iota(jnp.int32, (BLOCK, 128), 1)
        lane8 = lax.broadcasted_iota(jnp.int32, (8, 128), 1)

        def blk_body(b, dsk):
            r0 = pl.multiple_of(b * BLOCK, BLOCK)
            distf, valid = _band_masks(jnp.logical_and(ti == 0, b == 0))
            qb = q_ref[pl.ds(r0, BLOCK), :]
            band = kvs_v[pl.ds(r0, 2 * BLOCK), :]
            aob = ao_ref[pl.ds(r0, BLOCK), :].astype(F32)
            daob = dao_v[pl.ds(r0, BLOCK), :]
            lse_mat = lse_ref[pl.ds(r0, BLOCK), :]
            dqs = []
            dks = []
            dvs = []
            for kh in range(N_KV_HEADS):
                k_h = band[:, kh * HEAD_DIM:(kh + 1) * HEAD_DIM]
                v_h = band[:, KV_DIM + kh * HEAD_DIM:KV_DIM + (kh + 1) * HEAD_DIM]
                dk = jnp.zeros((2 * BLOCK, HEAD_DIM), F32)
                dv = jnp.zeros((2 * BLOCK, HEAD_DIM), F32)
                for g in range(GQA_GROUP):
                    hq = kh * GQA_GROUP + g
                    hc = slice(hq * HEAD_DIM, (hq + 1) * HEAD_DIM)
                    q_h = qb[:, hc]
                    s = _dot_nt(q_h, k_h) - _SLOPES[hq] * distf
                    s = jnp.where(valid, s, NEG)
                    lse = jnp.sum(jnp.where(lane == hq, lse_mat, 0.0), axis=1, keepdims=True)
                    pr = jnp.exp(s - lse)
                    dao_h = daob[:, hc]
                    dd = jnp.sum(dao_h * aob[:, hc], axis=1, keepdims=True)
                    dao_hb = dao_h.astype(BF16)
                    dp = _dot_nt(dao_hb, v_h)
                    dsb = (pr * (dp - dd)).astype(BF16)
                    dqs.append(_dot(dsb, k_h) * (HEAD_DIM ** -0.5))
                    dk = dk + _dot_tn(dsb, q_h)
                    dv = dv + _dot_tn(pr.astype(BF16), dao_hb)
                    dsv = -jnp.sum(jnp.exp(sink_ref[hq] - lse) * dd)
                    dsk = dsk + jnp.where(lane8 == hq, dsv, 0.0)
                dks.append(dk)
                dvs.append(dv)
            dq_v[pl.ds(r0, BLOCK), :] = jnp.concatenate(dqs, axis=1)
            dkv_v[pl.ds(r0, 2 * BLOCK), :] += jnp.concatenate(dks + dvs, axis=1)
            return dsk

        dsk = lax.fori_loop(0, nb, blk_body, jnp.zeros((8, 128), F32))
        dsink_ref[...] += dsk
        dqb = dq_v[...].astype(BF16)
        dxn = _dot_nt(dqb, w_q_v[...])
        xn, xh, r = _rms(h_ref[...], n_ref[...])
        dwq_ref[...] += _dot_tn(xn.astype(BF16), dqb)
        dxx, dn = _rms_bwd(dxn, xh, r, n_ref[...])
        dn_ref[...] += dn
        dhin_ref[...] = dout + dxx
        dkv_ref[...] = dkv_v[BLOCK:, :]
        carry_v[...] = dkv_v[0:BLOCK, :]

    rev = functools.partial(_row_spec, rev_nt=nt)
    row = rev(tm, D_MODEL)
    prev_spec = pl.BlockSpec((BLOCK, 2 * KV_DIM), lambda i: (jnp.maximum((nt - 1 - i) * nb - 1, 0), 0))
    return pl.pallas_call(
        body, name="attn_bwd", grid=(nt,),
        in_specs=[row, row, row, rev(tm, 2 * KV_DIM), prev_spec, row, rev(tm, 128),
                  _const_spec((1, D_MODEL)), SMEM, ANY, ANY],
        out_specs=[row, _const_spec((D_MODEL, D_MODEL)), _const_spec((D_MODEL, D_MODEL)), rev(tm, 2 * KV_DIM),
                   _const_spec((8, 128)), _const_spec((1, D_MODEL))],
        out_shape=[jax.ShapeDtypeStruct((T, D_MODEL), F32), jax.ShapeDtypeStruct((D_MODEL, D_MODEL), F32),
                   jax.ShapeDtypeStruct((D_MODEL, D_MODEL), F32), jax.ShapeDtypeStruct((T, 2 * KV_DIM), F32),
                   jax.ShapeDtypeStruct((8, 128), F32), jax.ShapeDtypeStruct((1, D_MODEL), F32)],
        scratch_shapes=[pltpu.VMEM((D_MODEL, D_MODEL), BF16), pltpu.VMEM((D_MODEL, D_MODEL), BF16),
                        pltpu.VMEM((tm + BLOCK, 2 * KV_DIM), BF16), pltpu.VMEM((tm, D_MODEL), F32),
                        pltpu.VMEM((tm, D_MODEL), F32), pltpu.VMEM((tm + BLOCK, 2 * KV_DIM), F32),
                        pltpu.VMEM((BLOCK, 2 * KV_DIM), F32), pltpu.SemaphoreType.DMA((2,))],
        compiler_params=_params(),
    )(dh, h, q, kv, kv, ao, lse, nmix, sinks, w_q, w_o)


def _mesh_pos():
    return lax.axis_index("x"), lax.axis_index("y"), lax.axis_index("c")


def _other_chips(x, y):
    return [(1 - x, y), (x, 1 - y), (1 - x, 1 - y)]


HBM_SPEC = pl.BlockSpec(memory_space=pltpu.HBM)
SEM_SPEC = pl.BlockSpec(memory_space=pltpu.SEMAPHORE)


def _split_call(name, bufs, waits=(), starts=(), after=()):
    n, nw, ns, na = len(bufs), len(waits), len(starts), len(after)

    def body(*refs):
        brefs = refs[:n]
        wsems = [(refs[n + 2 * k], refs[n + 2 * k + 1]) for k in range(nw)]
        o = n + 2 * nw + na
        ssems = [(refs[o + 2 * k], refs[o + 2 * k + 1]) for k in range(ns)]
        for (ss, rs), (_, _, fn) in zip(wsems, waits):
            for sending, arriving in fn(brefs, ss, rs):
                sending.wait_send()
                arriving.wait_recv()
        for (ss, rs), (_, fn) in zip(ssems, starts):
            for sending, _ in fn(brefs, ss, rs):
                sending.start()
        if ns:
            token = refs[o + 2 * ns + n]
            token[...] = jnp.zeros(token.shape, token.dtype)

    out_shape, out_specs = [], []
    for cnt, _ in starts:
        out_shape += [pltpu.SemaphoreType.DMA((cnt,)), pltpu.SemaphoreType.DMA((cnt,))]
        out_specs += [SEM_SPEC, SEM_SPEC]
    out_shape += [pltpu.HBM(b.shape, b.dtype) for b in bufs]
    out_specs += [HBM_SPEC] * n
    if ns:
        out_shape.append(jax.ShapeDtypeStruct((8, 128), F32))
        out_specs.append(pl.BlockSpec(memory_space=pltpu.VMEM))
    args = [pltpu.with_memory_space_constraint(b, pltpu.HBM) for b in bufs]
    for ss, rs, _ in waits:
        args += [ss, rs]
    args += list(after)
    res = pl.pallas_call(
        body, name=name, out_shape=tuple(out_shape),
        in_specs=[HBM_SPEC] * n + [SEM_SPEC] * (2 * nw) + [ANY] * na, out_specs=tuple(out_specs),
        input_output_aliases={i: 2 * ns + i for i in range(n)},
        compiler_params=pltpu.CompilerParams(has_side_effects=pltpu.SideEffectType.DATAFLOW_SIDE_EFFECTING),
    )(*args)
    sems = [(res[2 * k], res[2 * k + 1]) for k in range(ns)]
    return list(res[2 * ns:2 * ns + n]), sems, (res[2 * ns + n] if ns else None)


def _cast_place(items, name):
    n = len(items)
    mats = [a.shape[-2:] for a, _, _ in items]

    def body(*refs):
        ins, outs, scr, sem = refs[:n], refs[n:2 * n], refs[2 * n:3 * n], refs[3 * n]
        x, y, _ = _mesh_pos()
        cps = []
        for t in range(n):
            scr[t][...] = ins[t][...].astype(scr[t].dtype)
            cp = pltpu.make_async_copy(scr[t], outs[t].at[2 * x + y], sem.at[t])
            cp.start()
            cps.append(cp)
        for cp in cps:
            cp.wait()

    def spec(idx, shape):
        return pl.BlockSpec((None,) * len(idx) + tuple(shape), lambda i: tuple(idx) + (0, 0))

    return pl.pallas_call(
        body, name=name, grid=(1,),
        in_specs=[spec(idx, mat) for (_, idx, _), mat in zip(items, mats)], out_specs=[ANY] * n,
        out_shape=[jax.ShapeDtypeStruct((N_SHARD,) + tuple(mat), dt) for (_, _, dt), mat in zip(items, mats)],
        scratch_shapes=[pltpu.VMEM(tuple(mat), dt) for (_, _, dt), mat in zip(items, mats)]
        + [pltpu.SemaphoreType.DMA((n,))],
        compiler_params=_params(),
    )(*[a for a, _, _ in items])


def _gather_ici(idx):
    def fn(bufs, ss, rs):
        x, y, c = _mesh_pos()
        pairs = []
        for k, t in enumerate(idx):
            half = bufs[t].shape[1] // 2
            mine = bufs[t].at[2 * x + y, pl.ds(c * half, half), :]
            for j, (cx, cy) in enumerate(_other_chips(x, y)):
                theirs = bufs[t].at[2 * cx + cy, pl.ds(c * half, half), :]
                sem = dict(send_sem=ss.at[3 * k + j], recv_sem=rs.at[3 * k + j],
                           device_id=(cx, cy, c), device_id_type=MESH)
                pairs.append((pltpu.make_async_remote_copy(src_ref=mine, dst_ref=mine, **sem),
                              pltpu.make_async_remote_copy(src_ref=mine, dst_ref=theirs, **sem)))
        return pairs
    return fn


def _gather_d2d(idx):
    def fn(bufs, ss, rs):
        x, y, c = _mesh_pos()
        pairs = []
        for k, t in enumerate(idx):
            half = bufs[t].shape[1] // 2
            for j, (cx, cy) in enumerate(_other_chips(x, y)):
                got = bufs[t].at[2 * cx + cy, pl.ds(c * half, half), :]
                theirs = bufs[t].at[2 * cx + cy, pl.ds((1 - c) * half, half), :]
                sem = dict(send_sem=ss.at[3 * k + j], recv_sem=rs.at[3 * k + j],
                           device_id=(x, y, 1 - c), device_id_type=MESH)
                pairs.append((pltpu.make_async_remote_copy(src_ref=got, dst_ref=got, **sem),
                              pltpu.make_async_remote_copy(src_ref=got, dst_ref=theirs, **sem)))
        return pairs
    return fn


def _alloc(shapes, name):
    def body(*refs):
        pass

    return pl.pallas_call(body, name=name, out_specs=[ANY] * len(shapes),
                          out_shape=[jax.ShapeDtypeStruct(s, d) for s, d in shapes])()


def _send_to_sibling(n):
    def fn(bufs, ss, rs):
        x, y, c = _mesh_pos()
        pairs = []
        for t in range(n):
            src = bufs[t]
            if len(src.shape) == 3:
                half = src.shape[1] // 2
                src = src.at[:, pl.ds((1 - c) * half, half), :]
            cp = pltpu.make_async_remote_copy(src_ref=src, dst_ref=bufs[n + t], send_sem=ss.at[t],
                                              recv_sem=rs.at[t], device_id=(x, y, 1 - c), device_id_type=MESH)
            pairs.append((cp, cp))
        return pairs
    return fn


def _send_to_chips(n):
    def fn(bufs, ss, rs):
        x, y, c = _mesh_pos()
        pairs = []
        for j, (cx, cy) in enumerate(_other_chips(x, y)):
            for t in range(n):
                src = bufs[t].at[j] if len(bufs[t].shape) == 3 else bufs[t]
                cp = pltpu.make_async_remote_copy(src_ref=src, dst_ref=bufs[n + t].at[j], send_sem=ss.at[3 * t + j],
                                                  recv_sem=rs.at[3 * t + j], device_id=(cx, cy, c),
                                                  device_id_type=MESH)
                pairs.append((cp, cp))
        return pairs
    return fn


class _Exchange:
    def __init__(self, name, srcs, land_shapes, fn, n_sems):
        self.name, self.fn = name, fn
        lands = _alloc(land_shapes, name + "_alloc")
        self.n = len(srcs)
        self.bufs, sems, self.token = _split_call(name + "_start", list(srcs) + list(lands),
                                                  starts=[(n_sems, fn)])
        self.sems = sems[0]

    def finish(self, after=()):
        bufs, _, _ = _split_call(self.name + "_wait", self.bufs, waits=[(*self.sems, self.fn)], after=after)
        return bufs[:self.n], bufs[self.n:]


def _row_block(rows, cols, mult=8, limit=3 * 512 * 1024, itemsize=4):
    best = None
    for br in range(mult, rows + 1, mult):
        if rows % br == 0 and br * cols * itemsize <= limit:
            best = br
    assert best is not None, (rows, cols)
    return best


def _chip_partial(g, s, ids, name):
    _, half, cols = s.shape
    br = _row_block(half, cols, mult=16)
    nr = half // br

    def body(ids_ref, g_ref, s_ref, o_ref):
        o_ref[...] = (g_ref[...] + s_ref[...]).astype(BF16)

    return pl.pallas_call(
        body, name=name,
        grid_spec=pltpu.PrefetchScalarGridSpec(
            num_scalar_prefetch=1, grid=(3, nr),
            in_specs=[pl.BlockSpec((None, br, cols), lambda j, r, ids_ref: (ids_ref[2 + j], ids_ref[0] * nr + r, 0)),
                      pl.BlockSpec((None, br, cols), lambda j, r, ids_ref: (ids_ref[2 + j], r, 0))],
            out_specs=pl.BlockSpec((None, br, cols), lambda j, r, ids_ref: (j, r, 0))),
        out_shape=jax.ShapeDtypeStruct((3, half, cols), BF16),
        compiler_params=pltpu.CompilerParams(dimension_semantics=("arbitrary", "arbitrary")),
    )(ids, g, s)


def _chip_sum(g, s, q, ids, name):
    _, half, cols = s.shape
    br = _row_block(half, cols, mult=16)
    nr = half // br

    def body(ids_ref, g_ref, s_ref, q_ref, o_ref):
        own = g_ref[...] + s_ref[...]
        o_ref[...] = (own + q_ref[2].astype(F32)) + (q_ref[0].astype(F32) + q_ref[1].astype(F32))

    return pl.pallas_call(
        body, name=name,
        grid_spec=pltpu.PrefetchScalarGridSpec(
            num_scalar_prefetch=1, grid=(nr,),
            in_specs=[pl.BlockSpec((None, br, cols), lambda r, ids_ref: (ids_ref[1], ids_ref[0] * nr + r, 0)),
                      pl.BlockSpec((None, br, cols), lambda r, ids_ref: (ids_ref[1], r, 0)),
                      pl.BlockSpec((3, br, cols), lambda r, ids_ref: (0, r, 0))],
            out_specs=pl.BlockSpec((br, cols), lambda r, ids_ref: (r, 0))),
        out_shape=jax.ShapeDtypeStruct((half, cols), F32),
        compiler_params=pltpu.CompilerParams(dimension_semantics=("arbitrary",)),
    )(ids, g, s, q)


def _small_sum(part, recv):
    def body(p_ref, q_ref, o_ref):
        o_ref[...] = (p_ref[...] + q_ref[2]) + (q_ref[0] + q_ref[1])

    return pl.pallas_call(body, name="chip_sum_small", out_shape=jax.ShapeDtypeStruct(part.shape, F32))(part, recv)


def _adamw_math(w, g, m, v):
    mn = ADAM_B1 * m + (1.0 - ADAM_B1) * g
    vn = ADAM_B2 * v + (1.0 - ADAM_B2) * (g * g)
    m_hat = mn / (1.0 - ADAM_B1 ** ADAM_STEP)
    v_hat = vn / (1.0 - ADAM_B2 ** ADAM_STEP)
    return -ADAM_LR * (m_hat / (jnp.sqrt(v_hat) + ADAM_EPS) + ADAM_WD * w), mn, vn


def _adamw(w, g, m, v, name):
    R, C = w.shape
    br = _row_block(R, C)

    def body(w_ref, g_ref, m_ref, v_ref, d_ref, mo_ref, vo_ref):
        d_ref[...], mo_ref[...], vo_ref[...] = _adamw_math(w_ref[...], g_ref[...], m_ref[...], v_ref[...])

    spec = pl.BlockSpec((br, C), lambda i: (i, 0))
    return pl.pallas_call(
        body, name=name, grid=(R // br,), in_specs=[spec] * 4, out_specs=[spec] * 3,
        out_shape=[jax.ShapeDtypeStruct((R, C), F32)] * 3, compiler_params=_params(),
    )(w, g, m, v)


def _adamw_halves(w, own, sib, m, v, ids, name, layer=0, n_layers=1, stacked=None):
    C = w.shape[1]
    R = w.shape[0] // n_layers
    half = R // 2
    br = _row_block(half, C)
    nh = half // br
    base = layer * 2 * nh

    def body(ids_ref, w_ref, own_ref, sib_ref, m_ref, v_ref, *rest):
        g_ref, d_ref, mo_ref, vo_ref = rest[-4:]
        is_own = (pl.program_id(0) // nh) == ids_ref[0]
        g = jnp.where(is_own, own_ref[...], sib_ref[...])
        g_ref[...] = g
        d_ref[...], mo_ref[...], vo_ref[...] = _adamw_math(w_ref[...], g, m_ref[...], v_ref[...])

    full = pl.BlockSpec((br, C), lambda r, ids_ref: (base + r, 0))
    own_spec = pl.BlockSpec((br, C), lambda r, ids_ref: (jnp.clip(r - ids_ref[0] * nh, 0, nh - 1), 0))
    sib_spec = pl.BlockSpec((br, C), lambda r, ids_ref: (jnp.clip(r - (1 - ids_ref[0]) * nh, 0, nh - 1), 0))
    in_specs = [full, own_spec, sib_spec, full, full]
    args = [ids, w, own, sib, m, v]
    aliases = {}
    if stacked is not None:
        in_specs += [ANY] * 4
        args += list(stacked)
        aliases = {6 + k: k for k in range(4)}
    return pl.pallas_call(
        body, name=name,
        grid_spec=pltpu.PrefetchScalarGridSpec(
            num_scalar_prefetch=1, grid=(2 * nh,), in_specs=in_specs, out_specs=[full] * 4),
        out_shape=[jax.ShapeDtypeStruct(w.shape, F32)] * 4, input_output_aliases=aliases,
        compiler_params=_params(),
    )(*args)


_PACK_UNIT = 1024


def _pack(arrs):
    flat = []
    for a in arrs:
        f = a.reshape(-1).astype(F32)
        pad = (-f.shape[0]) % _PACK_UNIT
        if pad:
            f = jnp.concatenate([f, jnp.zeros((pad,), F32)])
        flat.append(f)
    return jnp.concatenate(flat).reshape(-1, 128)


def _unpack(packed, shapes):
    flat = packed.reshape(-1)
    out, off = [], 0
    for shp in shapes:
        size = int(np.prod(shp))
        out.append(flat[off:off + size].reshape(shp))
        off += size + ((-size) % _PACK_UNIT)
    return out


def kernel(x, p, norm_mix, norm_ffn, norm_ple, norm_kv, norm_final, a_w_in, a_norm_v, a_w_s, a_b_s, a_w_out, w_kv, b_w_q, b_sinks, b_w_o, f_w_up, f_conv_w, f_conv_b, f_w_down, ple_w_in, ple_w_gate, ple_b_gate, loss_target, m_norm_mix, m_norm_ffn, m_norm_ple, m_norm_kv, m_norm_final, m_a_w_in, m_a_norm_v, m_a_w_s, m_a_b_s, m_a_w_out, m_w_kv, m_b_w_q, m_b_sinks, m_b_w_o, m_f_w_up, m_f_conv_w, m_f_conv_b, m_f_w_down, m_ple_w_in, m_ple_w_gate, m_ple_b_gate, v_norm_mix, v_norm_ffn, v_norm_ple, v_norm_kv, v_norm_final, v_a_w_in, v_a_norm_v, v_a_w_s, v_a_b_s, v_a_w_out, v_w_kv, v_b_w_q, v_b_sinks, v_b_w_o, v_f_w_up, v_f_conv_w, v_f_conv_b, v_f_w_down, v_ple_w_in, v_ple_w_gate, v_ple_b_gate):
    given = dict(locals())

    small_shard = _pack([a_norm_v, f_conv_w])
    pad_rows = (-small_shard.shape[0]) % 16
    if pad_rows:
        small_shard = jnp.concatenate([small_shard, jnp.zeros((pad_rows, 128), F32)])
    groups = [
        [(a_w_in, (0,), BF16), (a_w_out, (0,), BF16), (small_shard, (), F32)],
        [(f_w_up, (0,), BF16), (f_w_down, (0,), BF16)],
        [(ple_w_in, (0,), BF16), (ple_w_gate, (0,), BF16), (w_kv, (), BF16), (b_w_q, (0,), BF16),
         (b_w_o, (0,), BF16), (f_w_up, (1,), BF16), (f_w_down, (1,), BF16), (ple_w_in, (1,), BF16),
         (ple_w_gate, (1,), BF16)],
    ]
    lands, spans, start = [], [], 0
    for gi, items in enumerate(groups):
        lands += _cast_place(items, f"cast_place_g{gi}")
        spans.append(list(range(start, start + len(items))))
        start += len(items)
    lands, ici_sems, _ = _split_call("gather_start", lands,
                                     starts=[(3 * len(sp), _gather_ici(sp)) for sp in spans])

    def finish_group(gi, after):
        sp = spans[gi]
        local = list(range(len(sp)))
        bufs = [lands[t] for t in sp]
        bufs, d2d_sems, _ = _split_call(f"gather_pass_g{gi}", bufs, waits=[(*ici_sems[gi], _gather_ici(local))],
                                        starts=[(3 * len(sp), _gather_d2d(local))], after=after)
        bufs, _, _ = _split_call(f"gather_done_g{gi}", bufs, waits=[(*d2d_sems[0], _gather_d2d(local))])
        return bufs

    def stage0():
        b_in, b_out, b_small = finish_group(0, ())
        small_full = b_small.reshape(N_SHARD, -1)
        gv_full = small_full[:, :256].reshape(1, D_MODEL)
        cw_full = small_full[:, _PACK_UNIT:_PACK_UNIT + 2 * 3 * FF_BLK].reshape(N_SHARD, 2, 3, FF_BLK)
        cw_full = jnp.transpose(cw_full, (1, 2, 0, 3)).reshape(2, 3, N_FF)
        return gv_full, cw_full, b_in, b_out.reshape(D_MODEL, D_MODEL)

    def stage1(after):
        b_up, b_dn = finish_group(1, after)
        return b_up, b_dn.reshape(D_FF, D_MODEL)

    def stage2(after):
        pin0, gate0, kv_w, wq, wo, up1, dn1, pin1, gate1 = finish_group(2, after)
        sq = lambda a: a.reshape(D_MODEL, -1)
        return dict(w_pin=[pin0, pin1], w_gate=[sq(gate0), sq(gate1)], w_kv=sq(kv_w), w_q=sq(wq), w_o=sq(wo),
                    w_up1=up1, w_dn1=dn1.reshape(D_FF, D_MODEL))

    loss_acc, dx, (out_g, out_d, out_m, out_v) = _local_step(
        x[0], p[0, 0], p[1, 0], loss_target[0], norm_mix, norm_ffn, norm_ple, norm_kv, norm_final, a_w_s, a_b_s,
        b_sinks, f_conv_b, ple_b_gate, stage0, stage1, stage2, _Reducer(given))
    weight_names = ['norm_mix', 'norm_ffn', 'norm_ple', 'norm_kv', 'norm_final', 'a_w_in', 'a_norm_v', 'a_w_s',
                    'a_b_s', 'a_w_out', 'w_kv', 'b_w_q', 'b_sinks', 'b_w_o', 'f_w_up', 'f_conv_w', 'f_conv_b',
                    'f_w_down', 'ple_w_in', 'ple_w_gate', 'ple_b_gate']
    loss = lax.psum(loss_acc[0, 0], ("x", "y", "c"))
    return (loss, dx.reshape(x.shape), *[out_g[k] for k in weight_names], *[out_d[k] for k in weight_names],
            *[out_m[k] for k in weight_names], *[out_v[k] for k in weight_names])


def _local_step(xs, p0, p1, tgt, norm_mix, norm_ffn, norm_ple, norm_kv, norm_final, a_w_s, a_b_s, b_sinks,
                f_conv_b, ple_b_gate, stage0, stage1, stage2, sched):
    tril = jnp.tril(jnp.ones((CHUNK, CHUNK), F32))
    wsm = (a_w_s[0] * tril[None]).astype(BF16)
    bsb = jnp.broadcast_to(a_b_s[0][:, :, None], (A_GROUPS, CHUNK, CHUNK))
    sinks = b_sinks[0]
    row = lambda a: a.reshape(1, -1)

    gv_full, cw_full, w_in, w_out = stage0()
    h1, zp = _mixer_a_fwd(xs, row(norm_mix[0]), gv_full, wsm, bsb, w_in, w_out)
    w_up0, w_dn0 = stage1((h1,))
    h2, hh0, c0 = _ffn_fwd(h1, row(norm_ffn[0]), cw_full[0], row(f_conv_b[0]), w_up0, w_dn0, 0)
    rest = stage2((h2,))
    w_pin, w_gate, w_kv_f, w_q, w_o = rest['w_pin'], rest['w_gate'], rest['w_kv'], rest['w_q'], rest['w_o']
    w_up = [w_up0, rest['w_up1']]
    w_dn = [w_dn0, rest['w_dn1']]
    h3, pe0, a0, kv = _ple_fwd_kv(h2, p0, row(norm_ple[0]), row(ple_b_gate[0]), row(norm_kv), w_pin[0], w_gate[0], w_kv_f)
    h4, q, ao, lse = _attn_fwd(h3, row(norm_mix[1]), kv, sinks, w_q, w_o)
    h5, hh1, c1 = _ffn_fwd(h4, row(norm_ffn[1]), cw_full[1], row(f_conv_b[1]), w_up[1], w_dn[1], 1)
    dh6, pe1, a1, loss_acc, dn_final = _ple_fwd_final(
        h5, p1, tgt, row(norm_ple[1]), row(ple_b_gate[1]), row(norm_final), w_pin[1], w_gate[1])

    def pieces(g):
        return g.reshape(N_SHARD, -1, g.shape[-1])

    dh5, g_pin1, g_gate1, dbg1, dnple1 = _ple_bwd(dh6, h5, pe1, a1, p1, row(norm_ple[1]), w_gate[1], 1)
    early = {('ple_w_in', 1): g_pin1, ('ple_w_gate', 1): pieces(g_gate1)}
    dh4, g_up1, g_dn1, dcw1, dcb1, dnffn1 = _ffn_bwd(
        dh5, h4, hh1, c1, row(norm_ffn[1]), cw_full[1], w_up[1], w_dn[1], 1)
    early['f_w_down', 1] = pieces(g_dn1)
    early['f_w_up', 1] = g_up1
    dh3a, g_wq, g_wo, dkv, dsink, dnmix1 = _attn_bwd(dh4, h3, q, kv, ao, lse, row(norm_mix[1]), sinks, w_q, w_o)
    early['b_w_o', 0] = pieces(g_wo)
    early['b_w_q', 0] = pieces(g_wq)
    dh2, g_pin0, g_gate0, dbg0, dnple0, g_wkv, dnkv = _ple_bwd(
        dh3a, h2, pe0, a0, p0, row(norm_ple[0]), w_gate[0], 0, kv_args=(h3, dkv, row(norm_kv), w_kv_f))
    early['w_kv', 0] = pieces(g_wkv)
    early['ple_w_in', 0] = g_pin0
    early['ple_w_gate', 0] = pieces(g_gate0)
    deps = sched.early_ready(early)
    dh1, g_up0, g_dn0, dcw0, dcb0, dnffn0 = _ffn_bwd(
        dh2, h1, hh0, c0, row(norm_ffn[0]), cw_full[0], w_up[0], w_dn[0], 0, deps=deps,
        between=lambda part: sched.after_ffn_half((part,)))
    deps = sched.ffn0_ready({('f_w_down', 0): pieces(g_dn0), ('f_w_up', 0): g_up0})
    dx, g_win, g_wout, dws, dbs, dgv, dnmix0 = _mixer_a_bwd(
        dh1, xs, zp, row(norm_mix[0]), gv_full, wsm, bsb, tril, w_in, w_out, deps=deps)
    g_wout = pieces(g_wout)

    small_grads = {
        'norm_mix': jnp.concatenate([dnmix0, dnmix1]), 'norm_ffn': jnp.concatenate([dnffn0, dnffn1]),
        'norm_ple': jnp.concatenate([dnple0, dnple1]), 'norm_kv': dnkv, 'norm_final': dn_final,
        'a_norm_v': dgv, 'a_w_s': dws, 'a_b_s': dbs[:, :, 0], 'b_sinks': dsink[0, :N_Q_HEADS],
        'f_conv_w': jnp.stack([dcw0, dcw1]), 'f_conv_b': jnp.concatenate([dcb0, dcb1]),
        'ple_b_gate': jnp.concatenate([dbg0, dbg1]),
    }
    outs = sched.finish({('a_w_in', 0): g_win, ('a_w_out', 0): g_wout}, small_grads, (dx,))
    return loss_acc, dx, outs


_SMALL_SHAPES = {
    'norm_mix': (2, D_MODEL), 'norm_ffn': (2, D_MODEL), 'norm_ple': (2, D_MODEL), 'norm_kv': (D_MODEL,),
    'norm_final': (D_MODEL,), 'a_norm_v': (1, D_MODEL), 'a_w_s': (1, A_GROUPS, CHUNK, CHUNK),
    'a_b_s': (1, A_GROUPS, CHUNK), 'b_sinks': (1, N_Q_HEADS), 'f_conv_w': (2, 3, N_FF),
    'f_conv_b': (2, N_FF), 'ple_b_gate': (2, D_MODEL),
}


class _Reducer:
    def __init__(self, given):
        self.given = given
        cx, cy, cc = _mesh_pos()
        self.shard = 2 * cx + cy
        s = self.shard
        self.ids = jnp.stack([cc, s, s ^ 2, s ^ 1, s ^ 3]).astype(jnp.int32)
        self.out = [{}, {}, {}, {}]
        self.stacked = {}

    def _send(self, tag, grads, small=None):
        keys = list(grads)
        srcs = [grads[k] for k in keys]
        shapes = [((N_SHARD, g.shape[1] // 2, g.shape[2]), F32) for g in srcs]
        if small is not None:
            srcs.append(small)
            shapes.append((small.shape, F32))
        return keys, _Exchange(f"send_{tag}", srcs, shapes, _send_to_sibling(len(srcs)), len(srcs))

    def _exchange(self, tag, keys, send, after, with_small=False):
        srcs, lands = send.finish(after)
        n = len(keys)
        parts = [_chip_partial(g, s, self.ids, f"chip_partial_{k[0]}{k[1]}")
                 for k, g, s in zip(keys, srcs[:n], lands[:n])]
        shapes = [(p.shape, BF16) for p in parts]
        if with_small:
            parts.append(_small_add(srcs[n], lands[n]))
            shapes.append(((3,) + parts[-1].shape, F32))
        exch = _Exchange(f"exch_{tag}", parts, shapes, _send_to_chips(len(parts)), 3 * len(parts))
        return (keys, srcs[:n], lands[:n], exch)

    def _swap(self, tag, state, after, with_small=False):
        keys, grads, sib, exch = state
        parts, recv = exch.finish(after)
        n = len(keys)
        own = [_chip_sum(g, s, q, self.ids, f"chip_sum_{k[0]}{k[1]}") for k, g, s, q in zip(keys, grads, sib, recv[:n])]
        small_red = _small_sum(parts[n], recv[n]) if with_small else None
        return keys, _Exchange(f"swap_{tag}", own, [(o.shape, F32) for o in own], _send_to_sibling(n), n), small_red

    def _adamw(self, keys, swap, after):
        own, sib = swap.finish(after)
        last = None
        for (name, layer), o, s in zip(keys, own, sib):
            w = self.given[name]
            n_layers = w.shape[0] if w.ndim == 3 else 1
            c2 = w.shape[-1]
            res = _adamw_halves(w.reshape(-1, c2), o, s, self.given['m_' + name].reshape(-1, c2),
                                self.given['v_' + name].reshape(-1, c2), self.ids, f"adamw_{name}{layer}",
                                layer, n_layers, self.stacked.get(name))
            self.stacked[name] = res
            if layer == 0:
                for dst, r in zip(self.out, res):
                    dst[name] = r.reshape(w.shape)
            last = res[0]
        return last

    def early_ready(self, grads):
        self.e_keys, self.e_send = self._send("e", grads)
        return (self.e_send.token,)

    def after_ffn_half(self, after):
        self.e_state = self._exchange("e", self.e_keys, self.e_send, after)
        return (self.e_state[3].token,)

    def ffn0_ready(self, grads):
        _, self.e_swap, _ = self._swap("e", self.e_state, tuple(grads.values())[-1:])
        f_keys, f_send = self._send("f", grads)
        self.f_state = self._exchange("f", f_keys, f_send, ())
        return (self.f_state[3].token, self.e_swap.token)

    def finish(self, grads, small_grads, after):
        small_names = list(_SMALL_SHAPES)
        small_g = _pack([small_grads[k] for k in small_names])
        a_keys, a_send = self._send("a", grads, small_g)
        a_state = self._exchange("a", a_keys, a_send, after, with_small=True)
        e_done = self._adamw(self.e_keys, self.e_swap, (a_state[3].token,))
        f_keys, f_swap, _ = self._swap("f", self.f_state, (e_done,))
        f_done = self._adamw(f_keys, f_swap, ())
        _, a_swap, small_red = self._swap("a", a_state, (f_done,), with_small=True)
        self._adamw(a_keys, a_swap, ())

        given, shard = self.given, self.shard
        out_g, out_d, out_m, out_v = self.out
        full_small = dict(zip(small_names, _unpack(small_red, [_SMALL_SHAPES[k] for k in small_names])))
        local_small = dict(full_small)
        local_small['a_norm_v'] = lax.dynamic_slice_in_dim(full_small['a_norm_v'], shard * 256, 256, axis=1)
        local_small['f_conv_w'] = lax.dynamic_slice_in_dim(full_small['f_conv_w'], shard * FF_BLK, FF_BLK, axis=2)
        sg = _pack([local_small[k] for k in small_names])
        sw = _pack([given[k] for k in small_names])
        sm = _pack([given['m_' + k] for k in small_names])
        sv = _pack([given['v_' + k] for k in small_names])
        sd, smn, svn = _adamw(sw, sg, sm, sv, "adamw_small")
        local_shapes = [given[k].shape for k in small_names]
        for dst, packed in ((out_d, sd), (out_m, smn), (out_v, svn)):
            dst.update(zip(small_names, _unpack(packed, local_shapes)))
        for k in small_names:
            out_g[k] = local_small[k].reshape(given[k].shape)
        return self.out


def _small_add(a, b):
    def body(a_ref, b_ref, o_ref):
        o_ref[...] = a_ref[...] + b_ref[...]

    return pl.pallas_call(body, name="chip_partial_small", out_shape=jax.ShapeDtypeStruct(a.shape, F32))(a, b)
```

```python
import functools
import math

import numpy as np
import jax
import jax.numpy as jnp
from jax import lax
from jax.experimental import pallas as pl
from jax.experimental.pallas import tpu as pltpu

F32 = jnp.float32
BF16 = jnp.bfloat16

D_MODEL = 1024
CHUNK = 128
A_GROUPS = 8
HEAD_DIM = 64
N_Q_HEADS = 16
N_KV_HEADS = 4
GQA_GROUP = N_Q_HEADS // N_KV_HEADS
KV_DIM = N_KV_HEADS * HEAD_DIM
BLOCK = 128
D_FF = 2816
N_FF = 2 * D_FF
FF_BLK = N_FF // 4
PLE_DIM = 256
EPS = 1e-6
NEG = -1e30
N_SHARD = 4

ADAM_LR = 0.001
ADAM_B1 = 0.9
ADAM_B2 = 0.999
ADAM_EPS = 1e-08
ADAM_WD = 0.01
ADAM_STEP = 10

VMEM_LIMIT = 60 * 1024 * 1024
MESH = pl.DeviceIdType.MESH
ANY = pl.BlockSpec(memory_space=pl.ANY)
SMEM = pl.BlockSpec(memory_space=pltpu.SMEM)

_SLOPES = [float(np.float32(2.0 ** (-8.0 * (h + 1) / N_Q_HEADS))) for h in range(N_Q_HEADS)]


def _dot(a, b):
    return jnp.dot(a, b, preferred_element_type=F32)


def _dot_nt(a, b):
    return lax.dot_general(a, b, (((1,), (1,)), ((), ())), preferred_element_type=F32)


def _dot_tn(a, b):
    return lax.dot_general(a, b, (((0,), (0,)), ((), ())), preferred_element_type=F32)


def _rms(x, g):
    r = lax.rsqrt(jnp.mean(x * x, axis=-1, keepdims=True) + EPS)
    xh = x * r
    return xh * g, xh, r


def _rms_bwd(dy, xh, r, g):
    dxh = dy * g
    dg = jnp.sum(dy * xh, axis=0, keepdims=True)
    dx = r * (dxh - xh * jnp.mean(dxh * xh, axis=-1, keepdims=True))
    return dx, dg


_GELU_C = math.sqrt(2.0 / math.pi)


def _gelu(x):
    t = jnp.tanh(_GELU_C * (x + 0.044715 * (x * x * x)))
    return 0.5 * x * (1.0 + t)


def _gelu_grad(x):
    x2 = x * x
    t = jnp.tanh(_GELU_C * (x + 0.044715 * (x2 * x)))
    return 0.5 * (1.0 + t) + 0.5 * x * (1.0 - t * t) * (_GELU_C * (1.0 + 3.0 * 0.044715 * x2))


def _sigmoid(x):
    return 0.5 * jnp.tanh(0.5 * x) + 0.5


def _load_once(pairs, sem):
    @pl.when(pl.program_id(0) == 0)
    def _():
        cps = [pltpu.make_async_copy(s, d, sem.at[i]) for i, (s, d) in enumerate(pairs)]
        for cp in cps:
            cp.start()
        for cp in cps:
            cp.wait()


def _params(n_axes=1, vmem=VMEM_LIMIT):
    return pltpu.CompilerParams(dimension_semantics=("arbitrary",) * n_axes, vmem_limit_bytes=vmem)


def _row_spec(tm, n, rev_nt=None):
    if rev_nt is None:
        return pl.BlockSpec((tm, n), lambda i: (i, 0))
    return pl.BlockSpec((tm, n), lambda i: (rev_nt - 1 - i, 0))


def _const_spec(shape):
    nd = len(shape)
    return pl.BlockSpec(shape, lambda i: (0,) * nd)


def _add_deps(body, in_specs, args, deps):
    nd = len(deps)
    if nd == 0:
        return body, list(in_specs), list(args)

    def wrapped(*refs):
        return body(*refs[nd:])

    return wrapped, [ANY] * nd + list(in_specs), list(deps) + list(args)


def _zero_first(refs):
    @pl.when(pl.program_id(0) == 0)
    def _():
        for r in refs:
            r[...] = jnp.zeros(r.shape, r.dtype)


def _mixer_a_fwd(x, nmix, gv, wsm, bsb, w_in, w_out):
    T = x.shape[0]
    tm = min(512, T)
    nt = T // tm
    nw = 2 * D_MODEL // N_SHARD

    def body(x_ref, nmix_ref, gv_ref, ws_ref, bsb_ref, w_in_hbm, w_out_hbm,
             h1_ref, zp_ref, w_in_v, w_out_v, gated_v, sem):
        _load_once([(w_in_hbm, w_in_v), (w_out_hbm, w_out_v)], sem)
        xv = x_ref[...]
        xn = _rms(xv, nmix_ref[...])[0].astype(BF16)
        for j in range(N_SHARD):
            zp_ref[:, j * nw:(j + 1) * nw] = _dot(xn, w_in_v[j])
        z = _gelu(zp_ref[...])
        u = z[:, :D_MODEL]
        vn = _rms(z[:, D_MODEL:], gv_ref[...])[0].astype(BF16)
        for c in range(tm // CHUNK):
            rows = slice(c * CHUNK, (c + 1) * CHUNK)
            for h in range(A_GROUPS):
                cols = slice(h * CHUNK, (h + 1) * CHUNK)
                s = _dot(ws_ref[h], vn[rows, cols]) + bsb_ref[h]
                gated_v[rows, cols] = (u[rows, cols] * s).astype(BF16)
        h1_ref[...] = xv + _dot(gated_v[...], w_out_v[...])

    return pl.pallas_call(
        body, name="mixer_a_fwd", grid=(nt,),
        in_specs=[_row_spec(tm, D_MODEL), _const_spec((1, D_MODEL)), _const_spec((1, D_MODEL)),
                  _const_spec((A_GROUPS, CHUNK, CHUNK)), _const_spec((A_GROUPS, CHUNK, CHUNK)), ANY, ANY],
        out_specs=[_row_spec(tm, D_MODEL), _row_spec(tm, 2 * D_MODEL)],
        out_shape=[jax.ShapeDtypeStruct((T, D_MODEL), F32), jax.ShapeDtypeStruct((T, 2 * D_MODEL), F32)],
        scratch_shapes=[pltpu.VMEM((N_SHARD, D_MODEL, nw), BF16), pltpu.VMEM((D_MODEL, D_MODEL), BF16),
                        pltpu.VMEM((tm, D_MODEL), BF16), pltpu.SemaphoreType.DMA((2,))],
        compiler_params=_params(),
    )(x, nmix, gv, wsm, bsb, w_in, w_out)


def _mixer_a_bwd(dh, x, zp, nmix, gv, wsm, bsb, tril, w_in, w_out, deps=()):
    T = x.shape[0]
    tm = min(256, T)
    nt = T // tm
    nw = 2 * D_MODEL // N_SHARD

    def body(dh_ref, x_ref, zp_ref, nmix_ref, gv_ref, ws_ref, bsb_ref, tril_ref, w_in_hbm, w_out_hbm,
             dx_ref, dwin_ref, dwout_ref, dws_ref, dbs_ref, dgv_ref, dnmix_ref,
             w_in_v, w_out_v, du_v, dvn_v, dbs_v, gated_ref, sem):
        _load_once([(w_in_hbm, w_in_v), (w_out_hbm, w_out_v)], sem)
        _zero_first([dws_ref, dbs_v, dgv_ref, dnmix_ref, dwin_ref, dwout_ref])
        i = pl.program_id(0)
        dhv = dh_ref[...]
        dhb = dhv.astype(BF16)
        xv = x_ref[...]
        xn, xh, r = _rms(xv, nmix_ref[...])
        xnb = xn.astype(BF16)
        zpv = zp_ref[...]
        z = _gelu(zpv)
        u = z[:, :D_MODEL]
        vn_f, vh, rv = _rms(z[:, D_MODEL:], gv_ref[...])
        vn = vn_f.astype(BF16)
        dgated = _dot_nt(dhb, w_out_v[...])
        for c in range(tm // CHUNK):
            rows = slice(c * CHUNK, (c + 1) * CHUNK)
            for h in range(A_GROUPS):
                cols = slice(h * CHUNK, (h + 1) * CHUNK)
                vn_h = vn[rows, cols]
                s = _dot(ws_ref[h], vn_h) + bsb_ref[h]
                dgt = dgated[rows, cols]
                u_h = u[rows, cols]
                gated_ref[rows, cols] = (u_h * s).astype(BF16)
                du_v[rows, cols] = dgt * s
                ds = dgt * u_h
                dsb = ds.astype(BF16)
                dws_ref[h] += _dot_nt(dsb, vn_h)
                dbs_v[h] += ds
                dvn_v[rows, cols] = _dot_tn(ws_ref[h], dsb)
        dwout_ref[...] += _dot_tn(gated_ref[...], dhb)
        dv, dgv = _rms_bwd(dvn_v[...], vh, rv, gv_ref[...])
        dgv_ref[...] += dgv
        dzu = (du_v[...] * _gelu_grad(zpv[:, :D_MODEL])).astype(BF16)
        dzv = (dv * _gelu_grad(zpv[:, D_MODEL:])).astype(BF16)
        dzs = (dzu[:, :nw], dzu[:, nw:], dzv[:, :nw], dzv[:, nw:])
        dxn = jnp.zeros((tm, D_MODEL), F32)
        for j in range(N_SHARD):
            dxn = dxn + _dot_nt(dzs[j], w_in_v[j])
            dwin_ref[j] += _dot_tn(xnb, dzs[j])
        dxx, dn = _rms_bwd(dxn, xh, r, nmix_ref[...])
        dnmix_ref[...] += dn
        dx_ref[...] = dhv + dxx

        @pl.when(i == nt - 1)
        def _():
            for h in range(A_GROUPS):
                dws_ref[h] = dws_ref[h] * tril_ref[...]
                dbs_ref[h] = jnp.broadcast_to(jnp.sum(dbs_v[h], axis=1, keepdims=True), (CHUNK, CHUNK))

    grp = (A_GROUPS, CHUNK, CHUNK)
    body, in_specs, args = _add_deps(
        body, [_row_spec(tm, D_MODEL), _row_spec(tm, D_MODEL), _row_spec(tm, 2 * D_MODEL),
               _const_spec((1, D_MODEL)), _const_spec((1, D_MODEL)), _const_spec(grp), _const_spec(grp),
               _const_spec((CHUNK, CHUNK)), ANY, ANY],
        [dh, x, zp, nmix, gv, wsm, bsb, tril, w_in, w_out], deps)
    return pl.pallas_call(
        body, name="mixer_a_bwd", grid=(nt,), in_specs=in_specs,
        out_specs=[_row_spec(tm, D_MODEL), _const_spec((N_SHARD, D_MODEL, nw)), _const_spec((D_MODEL, D_MODEL)),
                   _const_spec(grp), _const_spec(grp), _const_spec((1, D_MODEL)), _const_spec((1, D_MODEL))],
        out_shape=[jax.ShapeDtypeStruct((T, D_MODEL), F32), jax.ShapeDtypeStruct((N_SHARD, D_MODEL, nw), F32),
                   jax.ShapeDtypeStruct((D_MODEL, D_MODEL), F32),
                   jax.ShapeDtypeStruct(grp, F32), jax.ShapeDtypeStruct(grp, F32),
                   jax.ShapeDtypeStruct((1, D_MODEL), F32), jax.ShapeDtypeStruct((1, D_MODEL), F32)],
        scratch_shapes=[pltpu.VMEM((N_SHARD, D_MODEL, nw), BF16), pltpu.VMEM((D_MODEL, D_MODEL), BF16),
                        pltpu.VMEM((tm, D_MODEL), F32), pltpu.VMEM((tm, D_MODEL), F32),
                        pltpu.VMEM(grp, F32), pltpu.VMEM((tm, D_MODEL), BF16), pltpu.SemaphoreType.DMA((2,))],
        compiler_params=_params(),
    )(*args)


def _load_ffn_weights(w_up_hbm, w_dn_hbm, layer, w_up_v, w_dn_v, sem):
    _load_once([(w_up_hbm, w_up_v), (w_dn_hbm, w_dn_v)], sem)


def _ffn_fwd(h, nffn, cw, cb, w_up, w_dn, layer):
    T = h.shape[0]
    tm = min(256, T)
    nt = T // tm

    def body(h_ref, n_ref, cw_ref, cb_ref, w_up_hbm, w_dn_hbm, out_ref, hh_ref, c_ref,
             w_up_v, w_dn_v, carry_v, sem):
        _load_ffn_weights(w_up_hbm, w_dn_hbm, layer, w_up_v, w_dn_v, sem)
        _zero_first([carry_v])
        xv = h_ref[...]
        xf = _rms(xv, n_ref[...])[0].astype(BF16)
        acc = xv
        for j in range(2):
            cs = []
            for blk in (j, j + 2):
                cols = slice(blk * FF_BLK, (blk + 1) * FF_BLK)
                hh = _dot(xf, w_up_v[blk])
                hh_ref[:, cols] = hh.astype(BF16)
                ext = jnp.concatenate([carry_v[blk], hh], axis=0)
                carry_v[blk] = hh[tm - 8:, :]
                s1 = pltpu.roll(ext, 1, 0)[8:]
                s2 = pltpu.roll(ext, 2, 0)[8:]
                cv = (cb_ref[:, cols] + cw_ref[0:1, cols] * s2 + cw_ref[1:2, cols] * s1
                      + cw_ref[2:3, cols] * hh)
                c_ref[:, cols] = cv.astype(BF16)
                cs.append(cv)
            act = (cs[0] * _sigmoid(cs[0]) * cs[1]).astype(BF16)
            acc = acc + _dot(act, w_dn_v[j * FF_BLK:(j + 1) * FF_BLK, :])
        out_ref[...] = acc

    return pl.pallas_call(
        body, name=f"ffn_fwd{layer}", grid=(nt,),
        in_specs=[_row_spec(tm, D_MODEL), _const_spec((1, D_MODEL)), _const_spec((3, N_FF)),
                  _const_spec((1, N_FF)), ANY, ANY],
        out_specs=[_row_spec(tm, D_MODEL), _row_spec(tm, N_FF), _row_spec(tm, N_FF)],
        out_shape=[jax.ShapeDtypeStruct((T, D_MODEL), F32), jax.ShapeDtypeStruct((T, N_FF), BF16),
                   jax.ShapeDtypeStruct((T, N_FF), BF16)],
        scratch_shapes=[pltpu.VMEM((N_SHARD, D_MODEL, FF_BLK), BF16), pltpu.VMEM((D_FF, D_MODEL), BF16),
                        pltpu.VMEM((N_SHARD, 8, FF_BLK), F32), pltpu.SemaphoreType.DMA((2 * N_SHARD,))],
        compiler_params=_params(),
    )(h, nffn, cw, cb, w_up, w_dn)


def _wgrad(a, b, bn, col_sharded, name, deps=()):
    T, K = a.shape
    N = b.shape[1]
    tt = min(2048, T)
    nn, ntt = N // bn, T // tt
    kr = K // N_SHARD

    def body(a_ref, b_ref, o_ref):
        @pl.when(pl.program_id(1) == 0)
        def _():
            o_ref[...] = jnp.zeros(o_ref.shape, F32)
        d = _dot_tn(a_ref[...].astype(BF16), b_ref[...].astype(BF16))
        if col_sharded:
            o_ref[...] += d
        else:
            for j in range(N_SHARD):
                o_ref[j] += d[j * kr:(j + 1) * kr]

    if col_sharded:
        assert nn == N_SHARD
        out_spec = pl.BlockSpec((None, K, bn), lambda n, t: (n, 0, 0))
        out_shape = jax.ShapeDtypeStruct((N_SHARD, K, bn), F32)
    else:
        out_spec = pl.BlockSpec((N_SHARD, kr, bn), lambda n, t: (0, 0, n))
        out_shape = jax.ShapeDtypeStruct((N_SHARD, kr, N), F32)
    body, in_specs, args = _add_deps(
        body, [pl.BlockSpec((tt, K), lambda n, t: (t, 0)), pl.BlockSpec((tt, bn), lambda n, t: (t, n))],
        [a, b], deps)
    return pl.pallas_call(
        body, name=name, grid=(nn, ntt), in_specs=in_specs, out_specs=out_spec, out_shape=out_shape,
        compiler_params=pltpu.CompilerParams(dimension_semantics=("arbitrary",) * 2, vmem_limit_bytes=VMEM_LIMIT),
    )(*args)


def _ffn_bwd(dh, h, hh, c, nffn, cw, w_up, w_dn, layer, deps=(), between=None):
    T = h.shape[0]
    tm = min(256, T)
    nt = T // tm

    def body(dh_ref, h_ref, hh_ref, c_ref, n_ref, cw_ref, w_up_hbm, w_dn_hbm,
             dhin_ref, act_ref, dhh_ref, xf_ref, dcw_ref, dcb_ref, dn_ref,
             w_up_v, w_dn_v, carry_v, sem):
        _load_ffn_weights(w_up_hbm, w_dn_hbm, layer, w_up_v, w_dn_v, sem)
        _zero_first([carry_v, dcw_ref, dcb_ref, dn_ref])
        dout = dh_ref[...]
        doutb = dout.astype(BF16)
        xf_f, xh, r = _rms(h_ref[...], n_ref[...])
        xf_ref[...] = xf_f.astype(BF16)
        dxf = jnp.zeros((tm, D_MODEL), F32)
        for j in range(2):
            blks = (j, j + 2)
            cg = c_ref[:, j * FF_BLK:(j + 1) * FF_BLK].astype(F32)
            cu = c_ref[:, (j + 2) * FF_BLK:(j + 3) * FF_BLK].astype(F32)
            sg = _sigmoid(cg)
            sil = cg * sg
            act_ref[:, j * FF_BLK:(j + 1) * FF_BLK] = (sil * cu).astype(BF16)
            dact = _dot_nt(doutb, w_dn_v[j * FF_BLK:(j + 1) * FF_BLK, :])
            dcs = (dact * cu * (sg * (1.0 + cg * (1.0 - sg))), dact * sil)
            for blk, dc in zip(blks, dcs):
                cols = slice(blk * FF_BLK, (blk + 1) * FF_BLK)
                hhv = hh_ref[:, cols].astype(F32)
                ext = jnp.concatenate([dc, carry_v[blk]], axis=0)
                carry_v[blk] = dc[:8, :]
                n = tm + 8
                a1 = pltpu.roll(ext, n - 1, 0)[:tm]
                a2 = pltpu.roll(ext, n - 2, 0)[:tm]
                dcb_ref[:, cols] += jnp.sum(dc, axis=0, keepdims=True)
                dcw_ref[0:1, cols] += jnp.sum(a2 * hhv, axis=0, keepdims=True)
                dcw_ref[1:2, cols] += jnp.sum(a1 * hhv, axis=0, keepdims=True)
                dcw_ref[2:3, cols] += jnp.sum(dc * hhv, axis=0, keepdims=True)
                dhh = (cw_ref[2:3, cols] * dc + cw_ref[1:2, cols] * a1 + cw_ref[0:1, cols] * a2).astype(BF16)
                dhh_ref[:, cols] = dhh
                dxf = dxf + _dot_nt(dhh, w_up_v[blk])
        dxx, dn = _rms_bwd(dxf, xh, r, n_ref[...])
        dn_ref[...] += dn
        dhin_ref[...] = dout + dxx

    rev = functools.partial(_row_spec, rev_nt=nt)
    body, in_specs, args = _add_deps(
        body, [rev(tm, D_MODEL), rev(tm, D_MODEL), rev(tm, N_FF), rev(tm, N_FF),
               _const_spec((1, D_MODEL)), _const_spec((3, N_FF)), ANY, ANY],
        [dh, h, hh, c, nffn, cw, w_up, w_dn], deps)
    dhin, act, dhh, xf, dcw, dcb, dn = pl.pallas_call(
        body, name=f"ffn_bwd{layer}", grid=(nt,), in_specs=in_specs,
        out_specs=[rev(tm, D_MODEL), rev(tm, D_FF), rev(tm, N_FF), rev(tm, D_MODEL),
                   _const_spec((3, N_FF)), _const_spec((1, N_FF)), _const_spec((1, D_MODEL))],
        out_shape=[jax.ShapeDtypeStruct((T, D_MODEL), F32), jax.ShapeDtypeStruct((T, D_FF), BF16),
                   jax.ShapeDtypeStruct((T, N_FF), BF16), jax.ShapeDtypeStruct((T, D_MODEL), BF16),
                   jax.ShapeDtypeStruct((3, N_FF), F32), jax.ShapeDtypeStruct((1, N_FF), F32),
                   jax.ShapeDtypeStruct((1, D_MODEL), F32)],
        scratch_shapes=[pltpu.VMEM((N_SHARD, D_MODEL, FF_BLK), BF16), pltpu.VMEM((D_FF, D_MODEL), BF16),
                        pltpu.VMEM((N_SHARD, 8, FF_BLK), F32), pltpu.SemaphoreType.DMA((2 * N_SHARD,))],
        compiler_params=_params(),
    )(*args)
    deps2 = between(dhin) if between is not None else ()
    dwdn = _wgrad(act, dh, D_MODEL // 2, False, f"wgrad_ffn_down{layer}", deps=deps2)
    dwup = _wgrad(xf, dhh, FF_BLK, True, f"wgrad_ffn_up{layer}")
    return dhin, dwup, dwdn, dcw, dcb, dn


def _load_ple_weights(w_pin_hbm, w_gate_hbm, layer, w_pin_v, w_gate_v, sem, extra=()):
    _load_once([(w_pin_hbm, w_pin_v), (w_gate_hbm, w_gate_v)] + list(extra), sem)


def _ple_fwd_kv(h, p, nple, bg, nkv, w_pin, w_gate, w_kv):
    T = h.shape[0]
    tm = min(512, T)
    nt = T // tm
    pw = D_MODEL // N_SHARD

    def body(h_ref, p_ref, n_ref, bg_ref, nkv_ref, w_pin_hbm, w_gate_hbm, w_kv_hbm,
             out_ref, pe_ref, a_ref, kv_ref, w_pin_v, w_gate_v, w_kv_v, sem):
        _load_ple_weights(w_pin_hbm, w_gate_hbm, 0, w_pin_v, w_gate_v, sem, [(w_kv_hbm, w_kv_v)])
        xv = h_ref[...]
        xg = _rms(xv, n_ref[...])[0].astype(BF16)
        a = _dot(xg, w_gate_v[...]) + bg_ref[...]
        a_ref[...] = a
        pb = p_ref[...].astype(BF16)
        for j in range(N_SHARD):
            pe_ref[:, j * pw:(j + 1) * pw] = _dot(pb, w_pin_v[j])
        hn = xv + pe_ref[...] * _sigmoid(a)
        out_ref[...] = hn
        kvn = _rms(hn, nkv_ref[...])[0].astype(BF16)
        kv_ref[...] = _dot(kvn, w_kv_v[...]).astype(BF16)

    vec = _const_spec((1, D_MODEL))
    return pl.pallas_call(
        body, name="ple_fwd0", grid=(nt,),
        in_specs=[_row_spec(tm, D_MODEL), _row_spec(tm, PLE_DIM), vec, vec, vec, ANY, ANY, ANY],
        out_specs=[_row_spec(tm, D_MODEL), _row_spec(tm, D_MODEL), _row_spec(tm, D_MODEL),
                   _row_spec(tm, 2 * KV_DIM)],
        out_shape=[jax.ShapeDtypeStruct((T, D_MODEL), F32), jax.ShapeDtypeStruct((T, D_MODEL), F32),
                   jax.ShapeDtypeStruct((T, D_MODEL), F32), jax.ShapeDtypeStruct((T, 2 * KV_DIM), BF16)],
        scratch_shapes=[pltpu.VMEM((N_SHARD, PLE_DIM, pw), BF16), pltpu.VMEM((D_MODEL, D_MODEL), BF16),
                        pltpu.VMEM((D_MODEL, 2 * KV_DIM), BF16), pltpu.SemaphoreType.DMA((2 * N_SHARD + 1,))],
        compiler_params=_params(),
    )(h, p, nple, bg, nkv, w_pin, w_gate, w_kv)


def _ple_fwd_final(h, p, tgt, nple, bg, nfin, w_pin, w_gate):
    T = h.shape[0]
    tm = min(512, T)
    nt = T // tm
    pw = D_MODEL // N_SHARD

    def body(h_ref, p_ref, t_ref, n_ref, bg_ref, nf_ref, w_pin_hbm, w_gate_hbm,
             dh_ref, pe_ref, a_ref, loss_ref, dnf_ref, w_pin_v, w_gate_v, sem):
        _load_ple_weights(w_pin_hbm, w_gate_hbm, 1, w_pin_v, w_gate_v, sem)
        _zero_first([loss_ref, dnf_ref])
        xv = h_ref[...]
        xg = _rms(xv, n_ref[...])[0].astype(BF16)
        a = _dot(xg, w_gate_v[...]) + bg_ref[...]
        a_ref[...] = a
        pb = p_ref[...].astype(BF16)
        for j in range(N_SHARD):
            pe_ref[:, j * pw:(j + 1) * pw] = _dot(pb, w_pin_v[j])
        hn = xv + pe_ref[...] * _sigmoid(a)
        y, yh, r = _rms(hn, nf_ref[...])
        diff = y - t_ref[...]
        loss_ref[...] += 0.5 * jnp.sum(jnp.mean(diff * diff, axis=-1, keepdims=True))
        dy = diff * (1.0 / D_MODEL)
        dhn, dnf = _rms_bwd(dy, yh, r, nf_ref[...])
        dnf_ref[...] += dnf
        dh_ref[...] = dhn

    vec = _const_spec((1, D_MODEL))
    return pl.pallas_call(
        body, name="ple_fwd1", grid=(nt,),
        in_specs=[_row_spec(tm, D_MODEL), _row_spec(tm, PLE_DIM), _row_spec(tm, D_MODEL), vec, vec, vec, ANY, ANY],
        out_specs=[_row_spec(tm, D_MODEL), _row_spec(tm, D_MODEL), _row_spec(tm, D_MODEL),
                   _const_spec((8, 128)), vec],
        out_shape=[jax.ShapeDtypeStruct((T, D_MODEL), F32), jax.ShapeDtypeStruct((T, D_MODEL), F32),
                   jax.ShapeDtypeStruct((T, D_MODEL), F32), jax.ShapeDtypeStruct((8, 128), F32),
                   jax.ShapeDtypeStruct((1, D_MODEL), F32)],
        scratch_shapes=[pltpu.VMEM((N_SHARD, PLE_DIM, pw), BF16), pltpu.VMEM((D_MODEL, D_MODEL), BF16),
                        pltpu.SemaphoreType.DMA((2 * N_SHARD,))],
        compiler_params=_params(),
    )(h, p, tgt, nple, bg, nfin, w_pin, w_gate)


def _ple_bwd(dh, hb, pe, a, p, nple, w_gate, layer, kv_args=None):
    T = hb.shape[0]
    tm = min(512, T)
    nt = T // tm
    with_kv = kv_args is not None
    pw = D_MODEL // N_SHARD

    def body(*refs):
        if with_kv:
            (dh_ref, hb_ref, pe_ref, a_ref, p_ref, n_ref, w_gate_hbm, hc_ref, dkv_ref, nkv_ref, w_kv_hbm,
             dhb_ref, dwpin_ref, dwgate_ref, dbg_ref, dn_ref, dwkv_ref, dnkv_ref,
             w_gate_v, w_kv_v, sem) = refs
        else:
            (dh_ref, hb_ref, pe_ref, a_ref, p_ref, n_ref, w_gate_hbm,
             dhb_ref, dwpin_ref, dwgate_ref, dbg_ref, dn_ref, w_gate_v, sem) = refs
        pairs = [(w_gate_hbm, w_gate_v)]
        if with_kv:
            pairs.append((w_kv_hbm, w_kv_v))
        _load_once(pairs, sem)
        _zero_first([dwpin_ref, dwgate_ref, dbg_ref, dn_ref] + ([dwkv_ref, dnkv_ref] if with_kv else []))
        do = dh_ref[...]
        if with_kv:
            dkvb = dkv_ref[...].astype(BF16)
            dkvn = _dot_nt(dkvb, w_kv_v[...])
            kvn, kh, kr = _rms(hc_ref[...], nkv_ref[...])
            dwkv_ref[...] += _dot_tn(kvn.astype(BF16), dkvb)
            dk, dnkv = _rms_bwd(dkvn, kh, kr, nkv_ref[...])
            dnkv_ref[...] += dnkv
            do = do + dk
        gate = _sigmoid(a_ref[...])
        dpe = (do * gate).astype(BF16)
        pb = p_ref[...].astype(BF16)
        for j in range(N_SHARD):
            dwpin_ref[j] += _dot_tn(pb, dpe[:, j * pw:(j + 1) * pw])
        da = do * pe_ref[...] * (gate * (1.0 - gate))
        dab = da.astype(BF16)
        dbg_ref[...] += jnp.sum(da, axis=0, keepdims=True)
        dxg = _dot_nt(dab, w_gate_v[...])
        xg, xh, r = _rms(hb_ref[...], n_ref[...])
        dwgate_ref[...] += _dot_tn(xg.astype(BF16), dab)
        dxx, dn = _rms_bwd(dxg, xh, r, n_ref[...])
        dn_ref[...] += dn
        dhb_ref[...] = do + dxx

    vec = _const_spec((1, D_MODEL))
    row = _row_spec(tm, D_MODEL)
    in_specs = [row, row, row, row, _row_spec(tm, PLE_DIM), vec, ANY]
    args = [dh, hb, pe, a, p, nple, w_gate]
    out_specs = [row, _const_spec((N_SHARD, PLE_DIM, pw)), _const_spec((D_MODEL, D_MODEL)), vec, vec]
    out_shape = [jax.ShapeDtypeStruct((T, D_MODEL), F32), jax.ShapeDtypeStruct((N_SHARD, PLE_DIM, pw), F32),
                 jax.ShapeDtypeStruct((D_MODEL, D_MODEL), F32),
                 jax.ShapeDtypeStruct((1, D_MODEL), F32), jax.ShapeDtypeStruct((1, D_MODEL), F32)]
    scratch = [pltpu.VMEM((D_MODEL, D_MODEL), BF16)]
    if with_kv:
        hc, dkv, nkv, w_kv = kv_args
        in_specs += [row, _row_spec(tm, 2 * KV_DIM), vec, ANY]
        args += [hc, dkv, nkv, w_kv]
        out_specs += [_const_spec((D_MODEL, 2 * KV_DIM)), vec]
        out_shape += [jax.ShapeDtypeStruct((D_MODEL, 2 * KV_DIM), F32), jax.ShapeDtypeStruct((1, D_MODEL), F32)]
        scratch.append(pltpu.VMEM((D_MODEL, 2 * KV_DIM), BF16))
    scratch.append(pltpu.SemaphoreType.DMA((N_SHARD + 1,)))
    return pl.pallas_call(
        body, name=f"ple_bwd{layer}", grid=(nt,), in_specs=in_specs, out_specs=out_specs,
        out_shape=out_shape, scratch_shapes=scratch, compiler_params=_params(),
    )(*args)


GROUP_ROWS = GQA_GROUP * BLOCK


def _band_bias():
    ii = lax.broadcasted_iota(jnp.int32, (GROUP_ROWS, 2 * BLOCK), 0)
    jj = lax.broadcasted_iota(jnp.int32, (GROUP_ROWS, 2 * BLOCK), 1)
    dist = (ii & (BLOCK - 1)) + BLOCK - jj
    inband = (dist >= 0) & (dist < BLOCK)
    distf = dist.astype(F32)
    grp = ii >> 7
    bias = []
    for kh in range(N_KV_HEADS):
        sl = _SLOPES[kh * GQA_GROUP:(kh + 1) * GQA_GROUP]
        slope = jnp.where(grp == 0, sl[0], jnp.where(grp == 1, sl[1], jnp.where(grp == 2, sl[2], sl[3])))
        bias.append(jnp.where(inband, slope * distf, -NEG))
    return bias, jj


def _stack_heads(x, kh):
    return jnp.concatenate([x[:, (kh * GQA_GROUP + g) * HEAD_DIM:(kh * GQA_GROUP + g + 1) * HEAD_DIM]
                            for g in range(GQA_GROUP)], axis=0)


def _sink_column(sink_ref, kh):
    grp = lax.broadcasted_iota(jnp.int32, (GROUP_ROWS, 1), 0) >> 7
    s = [sink_ref[kh * GQA_GROUP + g] for g in range(GQA_GROUP)]
    return jnp.where(grp == 0, s[0], jnp.where(grp == 1, s[1], jnp.where(grp == 2, s[2], s[3])))


def _attn_fwd(h, nmix, kv, sinks, w_q, w_o):
    T = h.shape[0]
    tm = min(512, T)
    nt = T // tm
    nb = tm // BLOCK

    def body(h_ref, n_ref, kv_ref, kvp_ref, sink_ref, w_q_hbm, w_o_hbm,
             out_ref, q_ref, ao_ref, lse_ref, w_q_v, w_o_v, kvs_v, sem):
        _load_once([(w_q_hbm, w_q_v), (w_o_hbm, w_o_v)], sem)
        ti = pl.program_id(0)
        xv = h_ref[...]
        xn = _rms(xv, n_ref[...])[0].astype(BF16)
        q_ref[...] = (_dot(xn, w_q_v[...]) * (HEAD_DIM ** -0.5)).astype(BF16)
        kvs_v[0:BLOCK, :] = kvp_ref[...]
        kvs_v[BLOCK:, :] = kv_ref[...]
        lane = lax.broadcasted_iota(jnp.int32, (BLOCK, 128), 1)
        bias, jj = _band_bias()

        def blk_body(b, carry):
            r0 = pl.multiple_of(b * BLOCK, BLOCK)
            no_prev = jnp.logical_and(jnp.logical_and(ti == 0, b == 0), jj < BLOCK)
            qb = q_ref[pl.ds(r0, BLOCK), :]
            band = kvs_v[pl.ds(r0, 2 * BLOCK), :]
            lse_mat = jnp.zeros((BLOCK, 128), F32)
            outs = []
            for kh in range(N_KV_HEADS):
                k_h = band[:, kh * HEAD_DIM:(kh + 1) * HEAD_DIM]
                v_h = band[:, KV_DIM + kh * HEAD_DIM:KV_DIM + (kh + 1) * HEAD_DIM]
                s = _dot_nt(_stack_heads(qb, kh), k_h) - bias[kh]
                s = jnp.where(no_prev, NEG, s)
                sink = _sink_column(sink_ref, kh)
                m = jnp.maximum(jnp.max(s, axis=1, keepdims=True), sink)
                e = jnp.exp(s - m)
                den = jnp.sum(e, axis=1, keepdims=True) + jnp.exp(sink - m)
                o = _dot((e / den).astype(BF16), v_h)
                lse = m + jnp.log(den)
                for g in range(GQA_GROUP):
                    rows = slice(g * BLOCK, (g + 1) * BLOCK)
                    outs.append(o[rows])
                    lse_mat = jnp.where(lane == kh * GQA_GROUP + g, lse[rows], lse_mat)
            ao_ref[pl.ds(r0, BLOCK), :] = jnp.concatenate(outs, axis=1).astype(BF16)
            lse_ref[pl.ds(r0, BLOCK), :] = lse_mat
            return carry

        lax.fori_loop(0, nb, blk_body, 0)
        out_ref[...] = xv + _dot(ao_ref[...], w_o_v[...])

    row = _row_spec(tm, D_MODEL)
    prev_spec = pl.BlockSpec((BLOCK, 2 * KV_DIM), lambda i: (jnp.maximum(i * nb - 1, 0), 0))
    return pl.pallas_call(
        body, name="attn_fwd", grid=(nt,),
        in_specs=[row, _const_spec((1, D_MODEL)), _row_spec(tm, 2 * KV_DIM), prev_spec, SMEM, ANY, ANY],
        out_specs=[row, row, row, _row_spec(tm, 128)],
        out_shape=[jax.ShapeDtypeStruct((T, D_MODEL), F32), jax.ShapeDtypeStruct((T, D_MODEL), BF16),
                   jax.ShapeDtypeStruct((T, D_MODEL), BF16), jax.ShapeDtypeStruct((T, 128), F32)],
        scratch_shapes=[pltpu.VMEM((D_MODEL, D_MODEL), BF16), pltpu.VMEM((D_MODEL, D_MODEL), BF16),
                        pltpu.VMEM((tm + BLOCK, 2 * KV_DIM), BF16), pltpu.SemaphoreType.DMA((2,))],
        compiler_params=_params(),
    )(h, nmix, kv, kv, sinks, w_q, w_o)


def _attn_bwd(dh, h, q, kv, ao, lse, nmix, sinks, w_q, w_o):
    T = h.shape[0]
    tm = min(512, T)
    nt = T // tm
    nb = tm // BLOCK

    def body(dh_ref, h_ref, q_ref, kv_ref, kvp_ref, ao_ref, lse_ref, n_ref, sink_ref, w_q_hbm, w_o_hbm,
             dhin_ref, dwq_ref, dwo_ref, dkv_ref, dsink_ref, dn_ref,
             w_q_v, w_o_v, kvs_v, dao_v, dq_v, dkv_v, carry_v, sem):
        _load_once([(w_q_hbm, w_q_v), (w_o_hbm, w_o_v)], sem)
        _zero_first([carry_v, dsink_ref, dn_ref, dwq_ref, dwo_ref])
        ti = nt - 1 - pl.program_id(0)
        dout = dh_ref[...]
        doutb = dout.astype(BF16)
        dao_v[...] = _dot_nt(doutb, w_o_v[...])
        dwo_ref[...] += _dot_tn(ao_ref[...], doutb)
        kvs_v[0:BLOCK, :] = kvp_ref[...]
        kvs_v[BLOCK:, :] = kv_ref[...]
        dkv_v[0:tm, :] = jnp.zeros((tm, 2 * KV_DIM), F32)
        dkv_v[tm:, :] = carry_v[...]
        lane = lax.broadcasted_iota(jnp.int32, (BLOCK, 128), 1)
        lane8 = lax.broadcasted_iota(jnp.int32, (8, 128), 1)
        bias, jj = _band_bias()

        def blk_body(b, dsk):
            r0 = pl.multiple_of(b * BLOCK, BLOCK)
            no_prev = jnp.logical_and(jnp.logical_and(ti == 0, b == 0), jj < BLOCK)
            qb = q_ref[pl.ds(r0, BLOCK), :]
            band = kvs_v[pl.ds(r0, 2 * BLOCK), :]
            aob = ao_ref[pl.ds(r0, BLOCK), :].astype(F32)
            daob = dao_v[pl.ds(r0, BLOCK), :]
            lse_mat = lse_ref[pl.ds(r0, BLOCK), :]
            dqs = []
            dks = []
            dvs = []
            for kh in range(N_KV_HEADS):
                k_h = band[:, kh * HEAD_DIM:(kh + 1) * HEAD_DIM]
                v_h = band[:, KV_DIM + kh * HEAD_DIM:KV_DIM + (kh + 1) * HEAD_DIM]
                q_g = _stack_heads(qb, kh)
                dao_g = _stack_heads(daob, kh)
                s = _dot_nt(q_g, k_h) - bias[kh]
                s = jnp.where(no_prev, NEG, s)
                lse = jnp.concatenate(
                    [jnp.sum(jnp.where(lane == kh * GQA_GROUP + g, lse_mat, 0.0), axis=1, keepdims=True)
                     for g in range(GQA_GROUP)], axis=0)
                pr = jnp.exp(s - lse)
                dd = jnp.sum(dao_g * _stack_heads(aob, kh), axis=1, keepdims=True)
                dao_gb = dao_g.astype(BF16)
                dp = _dot_nt(dao_gb, v_h)
                dsb = (pr * (dp - dd)).astype(BF16)
                dq_g = _dot(dsb, k_h) * (HEAD_DIM ** -0.5)
                dks.append(_dot_tn(dsb, q_g))
                dvs.append(_dot_tn(pr.astype(BF16), dao_gb))
                psink = jnp.exp(_sink_column(sink_ref, kh) - lse) * dd
                for g in range(GQA_GROUP):
                    rows = slice(g * BLOCK, (g + 1) * BLOCK)
                    dqs.append(dq_g[rows])
                    dsk = dsk - jnp.where(lane8 == kh * GQA_GROUP + g, jnp.sum(psink[rows]), 0.0)
            dq_v[pl.ds(r0, BLOCK), :] = jnp.concatenate(dqs, axis=1)
            dkv_v[pl.ds(r0, 2 * BLOCK), :] += jnp.concatenate(dks + dvs, axis=1)
            return dsk

        dsk = lax.fori_loop(0, nb, blk_body, jnp.zeros((8, 128), F32))
        dsink_ref[...] += dsk
        dqb = dq_v[...].astype(BF16)
        dxn = _dot_nt(dqb, w_q_v[...])
        xn, xh, r = _rms(h_ref[...], n_ref[...])
        dwq_ref[...] += _dot_tn(xn.astype(BF16), dqb)
        dxx, dn = _rms_bwd(dxn, xh, r, n_ref[...])
        dn_ref[...] += dn
        dhin_ref[...] = dout + dxx
        dkv_ref[...] = dkv_v[BLOCK:, :]
        carry_v[...] = dkv_v[0:BLOCK, :]

    rev = functools.partial(_row_spec, rev_nt=nt)
    row = rev(tm, D_MODEL)
    prev_spec = pl.BlockSpec((BLOCK, 2 * KV_DIM), lambda i: (jnp.maximum((nt - 1 - i) * nb - 1, 0), 0))
    return pl.pallas_call(
        body, name="attn_bwd", grid=(nt,),
        in_specs=[row, row, row, rev(tm, 2 * KV_DIM), prev_spec, row, rev(tm, 128),
                  _const_spec((1, D_MODEL)), SMEM, ANY, ANY],
        out_specs=[row, _const_spec((D_MODEL, D_MODEL)), _const_spec((D_MODEL, D_MODEL)), rev(tm, 2 * KV_DIM),
                   _const_spec((8, 128)), _const_spec((1, D_MODEL))],
        out_shape=[jax.ShapeDtypeStruct((T, D_MODEL), F32), jax.ShapeDtypeStruct((D_MODEL, D_MODEL), F32),
                   jax.ShapeDtypeStruct((D_MODEL, D_MODEL), F32), jax.ShapeDtypeStruct((T, 2 * KV_DIM), F32),
                   jax.ShapeDtypeStruct((8, 128), F32), jax.ShapeDtypeStruct((1, D_MODEL), F32)],
        scratch_shapes=[pltpu.VMEM((D_MODEL, D_MODEL), BF16), pltpu.VMEM((D_MODEL, D_MODEL), BF16),
                        pltpu.VMEM((tm + BLOCK, 2 * KV_DIM), BF16), pltpu.VMEM((tm, D_MODEL), F32),
                        pltpu.VMEM((tm, D_MODEL), F32), pltpu.VMEM((tm + BLOCK, 2 * KV_DIM), F32),
                        pltpu.VMEM((BLOCK, 2 * KV_DIM), F32), pltpu.SemaphoreType.DMA((2,))],
        compiler_params=_params(),
    )(dh, h, q, kv, kv, ao, lse, nmix, sinks, w_q, w_o)


def _mesh_pos():
    return lax.axis_index("x"), lax.axis_index("y"), lax.axis_index("c")


def _other_chips(x, y):
    return [(1 - x, y), (x, 1 - y), (1 - x, 1 - y)]


HBM_SPEC = pl.BlockSpec(memory_space=pltpu.HBM)
SEM_SPEC = pl.BlockSpec(memory_space=pltpu.SEMAPHORE)


def _split_call(name, bufs, waits=(), starts=(), after=()):
    n, nw, ns, na = len(bufs), len(waits), len(starts), len(after)

    def body(*refs):
        brefs = refs[:n]
        wsems = [(refs[n + 2 * k], refs[n + 2 * k + 1]) for k in range(nw)]
        o = n + 2 * nw + na
        ssems = [(refs[o + 2 * k], refs[o + 2 * k + 1]) for k in range(ns)]
        for (ss, rs), (_, _, fn) in zip(wsems, waits):
            for sending, arriving in fn(brefs, ss, rs):
                sending.wait_send()
                arriving.wait_recv()
        for (ss, rs), (_, fn) in zip(ssems, starts):
            for sending, _ in fn(brefs, ss, rs):
                sending.start()
        if ns:
            token = refs[o + 2 * ns + n]
            token[...] = jnp.zeros(token.shape, token.dtype)

    out_shape, out_specs = [], []
    for cnt, _ in starts:
        out_shape += [pltpu.SemaphoreType.DMA((cnt,)), pltpu.SemaphoreType.DMA((cnt,))]
        out_specs += [SEM_SPEC, SEM_SPEC]
    out_shape += [pltpu.HBM(b.shape, b.dtype) for b in bufs]
    out_specs += [HBM_SPEC] * n
    if ns:
        out_shape.append(jax.ShapeDtypeStruct((8, 128), F32))
        out_specs.append(pl.BlockSpec(memory_space=pltpu.VMEM))
    args = [pltpu.with_memory_space_constraint(b, pltpu.HBM) for b in bufs]
    for ss, rs, _ in waits:
        args += [ss, rs]
    args += list(after)
    res = pl.pallas_call(
        body, name=name, out_shape=tuple(out_shape),
        in_specs=[HBM_SPEC] * n + [SEM_SPEC] * (2 * nw) + [ANY] * na, out_specs=tuple(out_specs),
        input_output_aliases={i: 2 * ns + i for i in range(n)},
        compiler_params=pltpu.CompilerParams(has_side_effects=pltpu.SideEffectType.DATAFLOW_SIDE_EFFECTING),
    )(*args)
    sems = [(res[2 * k], res[2 * k + 1]) for k in range(ns)]
    return list(res[2 * ns:2 * ns + n]), sems, (res[2 * ns + n] if ns else None)


def _cast_place(items, name):
    n = len(items)
    mats = [a.shape[-2:] for a, _, _ in items]

    def body(*refs):
        ins, outs, scr, sem = refs[:n], refs[n:2 * n], refs[2 * n:3 * n], refs[3 * n]
        x, y, _ = _mesh_pos()
        cps = []
        for t in range(n):
            scr[t][...] = ins[t][...].astype(scr[t].dtype)
            cp = pltpu.make_async_copy(scr[t], outs[t].at[2 * x + y], sem.at[t])
            cp.start()
            cps.append(cp)
        for cp in cps:
            cp.wait()

    def spec(idx, shape):
        return pl.BlockSpec((None,) * len(idx) + tuple(shape), lambda i: tuple(idx) + (0, 0))

    return pl.pallas_call(
        body, name=name, grid=(1,),
        in_specs=[spec(idx, mat) for (_, idx, _), mat in zip(items, mats)], out_specs=[ANY] * n,
        out_shape=[jax.ShapeDtypeStruct((N_SHARD,) + tuple(mat), dt) for (_, _, dt), mat in zip(items, mats)],
        scratch_shapes=[pltpu.VMEM(tuple(mat), dt) for (_, _, dt), mat in zip(items, mats)]
        + [pltpu.SemaphoreType.DMA((n,))],
        compiler_params=_params(),
    )(*[a for a, _, _ in items])


def _gather_ici(idx):
    def fn(bufs, ss, rs):
        x, y, c = _mesh_pos()
        pairs = []
        for k, t in enumerate(idx):
            half = bufs[t].shape[1] // 2
            mine = bufs[t].at[2 * x + y, pl.ds(c * half, half), :]
            for j, (cx, cy) in enumerate(_other_chips(x, y)):
                theirs = bufs[t].at[2 * cx + cy, pl.ds(c * half, half), :]
                sem = dict(send_sem=ss.at[3 * k + j], recv_sem=rs.at[3 * k + j],
                           device_id=(cx, cy, c), device_id_type=MESH)
                pairs.append((pltpu.make_async_remote_copy(src_ref=mine, dst_ref=mine, **sem),
                              pltpu.make_async_remote_copy(src_ref=mine, dst_ref=theirs, **sem)))
        return pairs
    return fn


def _gather_d2d(idx):
    def fn(bufs, ss, rs):
        x, y, c = _mesh_pos()
        pairs = []
        for k, t in enumerate(idx):
            half = bufs[t].shape[1] // 2
            for j, (cx, cy) in enumerate(_other_chips(x, y)):
                got = bufs[t].at[2 * cx + cy, pl.ds(c * half, half), :]
                theirs = bufs[t].at[2 * cx + cy, pl.ds((1 - c) * half, half), :]
                sem = dict(send_sem=ss.at[3 * k + j], recv_sem=rs.at[3 * k + j],
                           device_id=(x, y, 1 - c), device_id_type=MESH)
                pairs.append((pltpu.make_async_remote_copy(src_ref=got, dst_ref=got, **sem),
                              pltpu.make_async_remote_copy(src_ref=got, dst_ref=theirs, **sem)))
        return pairs
    return fn


def _alloc(shapes, name):
    def body(*refs):
        pass

    return pl.pallas_call(body, name=name, out_specs=[ANY] * len(shapes),
                          out_shape=[jax.ShapeDtypeStruct(s, d) for s, d in shapes])()


def _send_to_sibling(n):
    def fn(bufs, ss, rs):
        x, y, c = _mesh_pos()
        pairs = []
        for t in range(n):
            src = bufs[t]
            if len(src.shape) == 3:
                half = src.shape[1] // 2
                src = src.at[:, pl.ds((1 - c) * half, half), :]
            cp = pltpu.make_async_remote_copy(src_ref=src, dst_ref=bufs[n + t], send_sem=ss.at[t],
                                              recv_sem=rs.at[t], device_id=(x, y, 1 - c), device_id_type=MESH)
            pairs.append((cp, cp))
        return pairs
    return fn


def _send_to_chips(n):
    def fn(bufs, ss, rs):
        x, y, c = _mesh_pos()
        pairs = []
        for j, (cx, cy) in enumerate(_other_chips(x, y)):
            for t in range(n):
                src = bufs[t].at[j] if len(bufs[t].shape) == 3 else bufs[t]
                cp = pltpu.make_async_remote_copy(src_ref=src, dst_ref=bufs[n + t].at[j], send_sem=ss.at[3 * t + j],
                                                  recv_sem=rs.at[3 * t + j], device_id=(cx, cy, c),
                                                  device_id_type=MESH)
                pairs.append((cp, cp))
        return pairs
    return fn


class _Exchange:
    def __init__(self, name, srcs, land_shapes, fn, n_sems):
        self.name, self.fn = name, fn
        lands = _alloc(land_shapes, name + "_alloc")
        self.n = len(srcs)
        self.bufs, sems, self.token = _split_call(name + "_start", list(srcs) + list(lands),
                                                  starts=[(n_sems, fn)])
        self.sems = sems[0]

    def finish(self, after=()):
        bufs, _, _ = _split_call(self.name + "_wait", self.bufs, waits=[(*self.sems, self.fn)], after=after)
        return bufs[:self.n], bufs[self.n:]


def _row_block(rows, cols, mult=8, limit=3 * 512 * 1024, itemsize=4):
    best = None
    for br in range(mult, rows + 1, mult):
        if rows % br == 0 and br * cols * itemsize <= limit:
            best = br
    assert best is not None, (rows, cols)
    return best


def _chip_partial(g, s, ids, name):
    _, half, cols = s.shape
    br = _row_block(half, cols, mult=16)
    nr = half // br

    def body(ids_ref, g_ref, s_ref, o_ref):
        o_ref[...] = (g_ref[...] + s_ref[...]).astype(BF16)

    return pl.pallas_call(
        body, name=name,
        grid_spec=pltpu.PrefetchScalarGridSpec(
            num_scalar_prefetch=1, grid=(3, nr),
            in_specs=[pl.BlockSpec((None, br, cols), lambda j, r, ids_ref: (ids_ref[2 + j], ids_ref[0] * nr + r, 0)),
                      pl.BlockSpec((None, br, cols), lambda j, r, ids_ref: (ids_ref[2 + j], r, 0))],
            out_specs=pl.BlockSpec((None, br, cols), lambda j, r, ids_ref: (j, r, 0))),
        out_shape=jax.ShapeDtypeStruct((3, half, cols), BF16),
        compiler_params=pltpu.CompilerParams(dimension_semantics=("arbitrary", "arbitrary")),
    )(ids, g, s)


def _chip_sum(g, s, q, ids, name):
    _, half, cols = s.shape
    br = _row_block(half, cols, mult=16)
    nr = half // br

    def body(ids_ref, g_ref, s_ref, q_ref, o_ref):
        own = g_ref[...] + s_ref[...]
        o_ref[...] = (own + q_ref[2].astype(F32)) + (q_ref[0].astype(F32) + q_ref[1].astype(F32))

    return pl.pallas_call(
        body, name=name,
        grid_spec=pltpu.PrefetchScalarGridSpec(
            num_scalar_prefetch=1, grid=(nr,),
            in_specs=[pl.BlockSpec((None, br, cols), lambda r, ids_ref: (ids_ref[1], ids_ref[0] * nr + r, 0)),
                      pl.BlockSpec((None, br, cols), lambda r, ids_ref: (ids_ref[1], r, 0)),
                      pl.BlockSpec((3, br, cols), lambda r, ids_ref: (0, r, 0))],
            out_specs=pl.BlockSpec((br, cols), lambda r, ids_ref: (r, 0))),
        out_shape=jax.ShapeDtypeStruct((half, cols), F32),
        compiler_params=pltpu.CompilerParams(dimension_semantics=("arbitrary",)),
    )(ids, g, s, q)


def _small_sum(part, recv):
    def body(p_ref, q_ref, o_ref):
        o_ref[...] = (p_ref[...] + q_ref[2]) + (q_ref[0] + q_ref[1])

    return pl.pallas_call(body, name="chip_sum_small", out_shape=jax.ShapeDtypeStruct(part.shape, F32))(part, recv)


def _adamw_math(w, g, m, v):
    mn = ADAM_B1 * m + (1.0 - ADAM_B1) * g
    vn = ADAM_B2 * v + (1.0 - ADAM_B2) * (g * g)
    m_hat = mn / (1.0 - ADAM_B1 ** ADAM_STEP)
    v_hat = vn / (1.0 - ADAM_B2 ** ADAM_STEP)
    return -ADAM_LR * (m_hat / (jnp.sqrt(v_hat) + ADAM_EPS) + ADAM_WD * w), mn, vn


def _adamw(w, g, m, v, name):
    R, C = w.shape
    br = _row_block(R, C)

    def body(w_ref, g_ref, m_ref, v_ref, d_ref, mo_ref, vo_ref):
        d_ref[...], mo_ref[...], vo_ref[...] = _adamw_math(w_ref[...], g_ref[...], m_ref[...], v_ref[...])

    spec = pl.BlockSpec((br, C), lambda i: (i, 0))
    return pl.pallas_call(
        body, name=name, grid=(R // br,), in_specs=[spec] * 4, out_specs=[spec] * 3,
        out_shape=[jax.ShapeDtypeStruct((R, C), F32)] * 3, compiler_params=_params(),
    )(w, g, m, v)


def _adamw_halves(w, own, sib, m, v, ids, name, layer=0, n_layers=1, stacked=None):
    C = w.shape[1]
    R = w.shape[0] // n_layers
    half = R // 2
    br = _row_block(half, C)
    nh = half // br
    base = layer * 2 * nh

    def body(ids_ref, w_ref, own_ref, sib_ref, m_ref, v_ref, *rest):
        g_ref, d_ref, mo_ref, vo_ref = rest[-4:]
        is_own = (pl.program_id(0) // nh) == ids_ref[0]
        g = jnp.where(is_own, own_ref[...], sib_ref[...])
        g_ref[...] = g
        d_ref[...], mo_ref[...], vo_ref[...] = _adamw_math(w_ref[...], g, m_ref[...], v_ref[...])

    full = pl.BlockSpec((br, C), lambda r, ids_ref: (base + r, 0))
    own_spec = pl.BlockSpec((br, C), lambda r, ids_ref: (jnp.clip(r - ids_ref[0] * nh, 0, nh - 1), 0))
    sib_spec = pl.BlockSpec((br, C), lambda r, ids_ref: (jnp.clip(r - (1 - ids_ref[0]) * nh, 0, nh - 1), 0))
    in_specs = [full, own_spec, sib_spec, full, full]
    args = [ids, w, own, sib, m, v]
    aliases = {}
    if stacked is not None:
        in_specs += [ANY] * 4
        args += list(stacked)
        aliases = {6 + k: k for k in range(4)}
    return pl.pallas_call(
        body, name=name,
        grid_spec=pltpu.PrefetchScalarGridSpec(
            num_scalar_prefetch=1, grid=(2 * nh,), in_specs=in_specs, out_specs=[full] * 4),
        out_shape=[jax.ShapeDtypeStruct(w.shape, F32)] * 4, input_output_aliases=aliases,
        compiler_params=_params(),
    )(*args)


_PACK_UNIT = 1024


def _pack(arrs):
    flat = []
    for a in arrs:
        f = a.reshape(-1).astype(F32)
        pad = (-f.shape[0]) % _PACK_UNIT
        if pad:
            f = jnp.concatenate([f, jnp.zeros((pad,), F32)])
        flat.append(f)
    return jnp.concatenate(flat).reshape(-1, 128)


def _unpack(packed, shapes):
    flat = packed.reshape(-1)
    out, off = [], 0
    for shp in shapes:
        size = int(np.prod(shp))
        out.append(flat[off:off + size].reshape(shp))
        off += size + ((-size) % _PACK_UNIT)
    return out


def kernel(x, p, norm_mix, norm_ffn, norm_ple, norm_kv, norm_final, a_w_in, a_norm_v, a_w_s, a_b_s, a_w_out, w_kv, b_w_q, b_sinks, b_w_o, f_w_up, f_conv_w, f_conv_b, f_w_down, ple_w_in, ple_w_gate, ple_b_gate, loss_target, m_norm_mix, m_norm_ffn, m_norm_ple, m_norm_kv, m_norm_final, m_a_w_in, m_a_norm_v, m_a_w_s, m_a_b_s, m_a_w_out, m_w_kv, m_b_w_q, m_b_sinks, m_b_w_o, m_f_w_up, m_f_conv_w, m_f_conv_b, m_f_w_down, m_ple_w_in, m_ple_w_gate, m_ple_b_gate, v_norm_mix, v_norm_ffn, v_norm_ple, v_norm_kv, v_norm_final, v_a_w_in, v_a_norm_v, v_a_w_s, v_a_b_s, v_a_w_out, v_w_kv, v_b_w_q, v_b_sinks, v_b_w_o, v_f_w_up, v_f_conv_w, v_f_conv_b, v_f_w_down, v_ple_w_in, v_ple_w_gate, v_ple_b_gate):
    given = dict(locals())

    small_shard = _pack([a_norm_v, f_conv_w])
    pad_rows = (-small_shard.shape[0]) % 16
    if pad_rows:
        small_shard = jnp.concatenate([small_shard, jnp.zeros((pad_rows, 128), F32)])
    groups = [
        [(a_w_in, (0,), BF16), (a_w_out, (0,), BF16), (small_shard, (), F32)],
        [(f_w_up, (0,), BF16), (f_w_down, (0,), BF16)],
        [(ple_w_in, (0,), BF16), (ple_w_gate, (0,), BF16), (w_kv, (), BF16), (b_w_q, (0,), BF16),
         (b_w_o, (0,), BF16), (f_w_up, (1,), BF16), (f_w_down, (1,), BF16), (ple_w_in, (1,), BF16),
         (ple_w_gate, (1,), BF16)],
    ]
    lands, spans, start = [], [], 0
    for gi, items in enumerate(groups):
        lands += _cast_place(items, f"cast_place_g{gi}")
        spans.append(list(range(start, start + len(items))))
        start += len(items)
    lands, ici_sems, _ = _split_call("gather_start", lands,
                                     starts=[(3 * len(sp), _gather_ici(sp)) for sp in spans])

    def finish_group(gi, after):
        sp = spans[gi]
        local = list(range(len(sp)))
        bufs = [lands[t] for t in sp]
        bufs, d2d_sems, _ = _split_call(f"gather_pass_g{gi}", bufs, waits=[(*ici_sems[gi], _gather_ici(local))],
                                        starts=[(3 * len(sp), _gather_d2d(local))], after=after)
        bufs, _, _ = _split_call(f"gather_done_g{gi}", bufs, waits=[(*d2d_sems[0], _gather_d2d(local))])
        return bufs

    def stage0():
        b_in, b_out, b_small = finish_group(0, ())
        small_full = b_small.reshape(N_SHARD, -1)
        gv_full = small_full[:, :256].reshape(1, D_MODEL)
        cw_full = small_full[:, _PACK_UNIT:_PACK_UNIT + 2 * 3 * FF_BLK].reshape(N_SHARD, 2, 3, FF_BLK)
        cw_full = jnp.transpose(cw_full, (1, 2, 0, 3)).reshape(2, 3, N_FF)
        return gv_full, cw_full, b_in, b_out.reshape(D_MODEL, D_MODEL)

    def stage1(after):
        b_up, b_dn = finish_group(1, after)
        return b_up, b_dn.reshape(D_FF, D_MODEL)

    def stage2(after):
        pin0, gate0, kv_w, wq, wo, up1, dn1, pin1, gate1 = finish_group(2, after)
        sq = lambda a: a.reshape(D_MODEL, -1)
        return dict(w_pin=[pin0, pin1], w_gate=[sq(gate0), sq(gate1)], w_kv=sq(kv_w), w_q=sq(wq), w_o=sq(wo),
                    w_up1=up1, w_dn1=dn1.reshape(D_FF, D_MODEL))

    loss_acc, dx, (out_g, out_d, out_m, out_v) = _local_step(
        x[0], p[0, 0], p[1, 0], loss_target[0], norm_mix, norm_ffn, norm_ple, norm_kv, norm_final, a_w_s, a_b_s,
        b_sinks, f_conv_b, ple_b_gate, stage0, stage1, stage2, _Reducer(given))
    weight_names = ['norm_mix', 'norm_ffn', 'norm_ple', 'norm_kv', 'norm_final', 'a_w_in', 'a_norm_v', 'a_w_s',
                    'a_b_s', 'a_w_out', 'w_kv', 'b_w_q', 'b_sinks', 'b_w_o', 'f_w_up', 'f_conv_w', 'f_conv_b',
                    'f_w_down', 'ple_w_in', 'ple_w_gate', 'ple_b_gate']
    loss = lax.psum(loss_acc[0, 0], ("x", "y", "c"))
    return (loss, dx.reshape(x.shape), *[out_g[k] for k in weight_names], *[out_d[k] for k in weight_names],
            *[out_m[k] for k in weight_names], *[out_v[k] for k in weight_names])


def _local_step(xs, p0, p1, tgt, norm_mix, norm_ffn, norm_ple, norm_kv, norm_final, a_w_s, a_b_s, b_sinks,
                f_conv_b, ple_b_gate, stage0, stage1, stage2, sched):
    tril = jnp.tril(jnp.ones((CHUNK, CHUNK), F32))
    wsm = (a_w_s[0] * tril[None]).astype(BF16)
    bsb = jnp.broadcast_to(a_b_s[0][:, :, None], (A_GROUPS, CHUNK, CHUNK))
    sinks = b_sinks[0]
    row = lambda a: a.reshape(1, -1)

    gv_full, cw_full, w_in, w_out = stage0()
    h1, zp = _mixer_a_fwd(xs, row(norm_mix[0]), gv_full, wsm, bsb, w_in, w_out)
    w_up0, w_dn0 = stage1((h1,))
    h2, hh0, c0 = _ffn_fwd(h1, row(norm_ffn[0]), cw_full[0], row(f_conv_b[0]), w_up0, w_dn0, 0)
    rest = stage2((h2,))
    w_pin, w_gate, w_kv_f, w_q, w_o = rest['w_pin'], rest['w_gate'], rest['w_kv'], rest['w_q'], rest['w_o']
    w_up = [w_up0, rest['w_up1']]
    w_dn = [w_dn0, rest['w_dn1']]
    h3, pe0, a0, kv = _ple_fwd_kv(h2, p0, row(norm_ple[0]), row(ple_b_gate[0]), row(norm_kv), w_pin[0], w_gate[0], w_kv_f)
    h4, q, ao, lse = _attn_fwd(h3, row(norm_mix[1]), kv, sinks, w_q, w_o)
    h5, hh1, c1 = _ffn_fwd(h4, row(norm_ffn[1]), cw_full[1], row(f_conv_b[1]), w_up[1], w_dn[1], 1)
    dh6, pe1, a1, loss_acc, dn_final = _ple_fwd_final(
        h5, p1, tgt, row(norm_ple[1]), row(ple_b_gate[1]), row(norm_final), w_pin[1], w_gate[1])

    def pieces(g):
        return g.reshape(N_SHARD, -1, g.shape[-1])

    dh5, g_pin1, g_gate1, dbg1, dnple1 = _ple_bwd(dh6, h5, pe1, a1, p1, row(norm_ple[1]), w_gate[1], 1)
    early = {('ple_w_in', 1): g_pin1, ('ple_w_gate', 1): pieces(g_gate1)}
    dh4, g_up1, g_dn1, dcw1, dcb1, dnffn1 = _ffn_bwd(
        dh5, h4, hh1, c1, row(norm_ffn[1]), cw_full[1], w_up[1], w_dn[1], 1)
    early['f_w_down', 1] = pieces(g_dn1)
    early['f_w_up', 1] = g_up1
    dh3a, g_wq, g_wo, dkv, dsink, dnmix1 = _attn_bwd(dh4, h3, q, kv, ao, lse, row(norm_mix[1]), sinks, w_q, w_o)
    early['b_w_o', 0] = pieces(g_wo)
    early['b_w_q', 0] = pieces(g_wq)
    dh2, g_pin0, g_gate0, dbg0, dnple0, g_wkv, dnkv = _ple_bwd(
        dh3a, h2, pe0, a0, p0, row(norm_ple[0]), w_gate[0], 0, kv_args=(h3, dkv, row(norm_kv), w_kv_f))
    early['w_kv', 0] = pieces(g_wkv)
    early['ple_w_in', 0] = g_pin0
    early['ple_w_gate', 0] = pieces(g_gate0)
    deps = sched.early_ready(early)
    dh1, g_up0, g_dn0, dcw0, dcb0, dnffn0 = _ffn_bwd(
        dh2, h1, hh0, c0, row(norm_ffn[0]), cw_full[0], w_up[0], w_dn[0], 0, deps=deps,
        between=lambda part: sched.after_ffn_half((part,)))
    deps = sched.ffn0_ready({('f_w_down', 0): pieces(g_dn0), ('f_w_up', 0): g_up0})
    dx, g_win, g_wout, dws, dbs, dgv, dnmix0 = _mixer_a_bwd(
        dh1, xs, zp, row(norm_mix[0]), gv_full, wsm, bsb, tril, w_in, w_out, deps=deps)
    g_wout = pieces(g_wout)

    small_grads = {
        'norm_mix': jnp.concatenate([dnmix0, dnmix1]), 'norm_ffn': jnp.concatenate([dnffn0, dnffn1]),
        'norm_ple': jnp.concatenate([dnple0, dnple1]), 'norm_kv': dnkv, 'norm_final': dn_final,
        'a_norm_v': dgv, 'a_w_s': dws, 'a_b_s': dbs[:, :, 0], 'b_sinks': dsink[0, :N_Q_HEADS],
        'f_conv_w': jnp.stack([dcw0, dcw1]), 'f_conv_b': jnp.concatenate([dcb0, dcb1]),
        'ple_b_gate': jnp.concatenate([dbg0, dbg1]),
    }
    outs = sched.finish({('a_w_in', 0): g_win, ('a_w_out', 0): g_wout}, small_grads, (dx,))
    return loss_acc, dx, outs


_SMALL_SHAPES = {
    'norm_mix': (2, D_MODEL), 'norm_ffn': (2, D_MODEL), 'norm_ple': (2, D_MODEL), 'norm_kv': (D_MODEL,),
    'norm_final': (D_MODEL,), 'a_norm_v': (1, D_MODEL), 'a_w_s': (1, A_GROUPS, CHUNK, CHUNK),
    'a_b_s': (1, A_GROUPS, CHUNK), 'b_sinks': (1, N_Q_HEADS), 'f_conv_w': (2, 3, N_FF),
    'f_conv_b': (2, N_FF), 'ple_b_gate': (2, D_MODEL),
}


class _Reducer:
    def __init__(self, given):
        self.given = given
        cx, cy, cc = _mesh_pos()
        self.shard = 2 * cx + cy
        s = self.shard
        self.ids = jnp.stack([cc, s, s ^ 2, s ^ 1, s ^ 3]).astype(jnp.int32)
        self.out = [{}, {}, {}, {}]
        self.stacked = {}

    def _send(self, tag, grads, small=None):
        keys = list(grads)
        srcs = [grads[k] for k in keys]
        shapes = [((N_SHARD, g.shape[1] // 2, g.shape[2]), F32) for g in srcs]
        if small is not None:
            srcs.append(small)
            shapes.append((small.shape, F32))
        return keys, _Exchange(f"send_{tag}", srcs, shapes, _send_to_sibling(len(srcs)), len(srcs))

    def _exchange(self, tag, keys, send, after, with_small=False):
        srcs, lands = send.finish(after)
        n = len(keys)
        parts = [_chip_partial(g, s, self.ids, f"chip_partial_{k[0]}{k[1]}")
                 for k, g, s in zip(keys, srcs[:n], lands[:n])]
        shapes = [(p.shape, BF16) for p in parts]
        if with_small:
            parts.append(_small_add(srcs[n], lands[n]))
            shapes.append(((3,) + parts[-1].shape, F32))
        exch = _Exchange(f"exch_{tag}", parts, shapes, _send_to_chips(len(parts)), 3 * len(parts))
        return (keys, srcs[:n], lands[:n], exch)

    def _swap(self, tag, state, after, with_small=False):
        keys, grads, sib, exch = state
        parts, recv = exch.finish(after)
        n = len(keys)
        own = [_chip_sum(g, s, q, self.ids, f"chip_sum_{k[0]}{k[1]}") for k, g, s, q in zip(keys, grads, sib, recv[:n])]
        small_red = _small_sum(parts[n], recv[n]) if with_small else None
        return keys, _Exchange(f"swap_{tag}", own, [(o.shape, F32) for o in own], _send_to_sibling(n), n), small_red

    def _adamw(self, keys, swap, after):
        own, sib = swap.finish(after)
        last = None
        for (name, layer), o, s in zip(keys, own, sib):
            w = self.given[name]
            n_layers = w.shape[0] if w.ndim == 3 else 1
            c2 = w.shape[-1]
            res = _adamw_halves(w.reshape(-1, c2), o, s, self.given['m_' + name].reshape(-1, c2),
                                self.given['v_' + name].reshape(-1, c2), self.ids, f"adamw_{name}{layer}",
                                layer, n_layers, self.stacked.get(name))
            self.stacked[name] = res
            if layer == 0:
                for dst, r in zip(self.out, res):
                    dst[name] = r.reshape(w.shape)
            last = res[0]
        return last

    def early_ready(self, grads):
        self.e_keys, self.e_send = self._send("e", grads)
        return (self.e_send.token,)

    def after_ffn_half(self, after):
        self.e_state = self._exchange("e", self.e_keys, self.e_send, after)
        return (self.e_state[3].token,)

    def ffn0_ready(self, grads):
        _, self.e_swap, _ = self._swap("e", self.e_state, tuple(grads.values())[-1:])
        f_keys, f_send = self._send("f", grads)
        self.f_state = self._exchange("f", f_keys, f_send, ())
        return (self.f_state[3].token, self.e_swap.token)

    def finish(self, grads, small_grads, after):
        small_names = list(_SMALL_SHAPES)
        small_g = _pack([small_grads[k] for k in small_names])
        a_keys, a_send = self._send("a", grads, small_g)
        a_state = self._exchange("a", a_keys, a_send, after, with_small=True)
        e_done = self._adamw(self.e_keys, self.e_swap, (a_state[3].token,))
        f_keys, f_swap, _ = self._swap("f", self.f_state, (e_done,))
        f_done = self._adamw(f_keys, f_swap, ())
        _, a_swap, small_red = self._swap("a", a_state, (f_done,), with_small=True)
        self._adamw(a_keys, a_swap, ())

        given, shard = self.given, self.shard
        out_g, out_d, out_m, out_v = self.out
        full_small = dict(zip(small_names, _unpack(small_red, [_SMALL_SHAPES[k] for k in small_names])))
        local_small = dict(full_small)
        local_small['a_norm_v'] = lax.dynamic_slice_in_dim(full_small['a_norm_v'], shard * 256, 256, axis=1)
        local_small['f_conv_w'] = lax.dynamic_slice_in_dim(full_small['f_conv_w'], shard * FF_BLK, FF_BLK, axis=2)
        sg = _pack([local_small[k] for k in small_names])
        sw = _pack([given[k] for k in small_names])
        sm = _pack([given['m_' + k] for k in small_names])
        sv = _pack([given['v_' + k] for k in small_names])
        sd, smn, svn = _adamw(sw, sg, sm, sv, "adamw_small")
        local_shapes = [given[k].shape for k in small_names]
        for dst, packed in ((out_d, sd), (out_m, smn), (out_v, svn)):
            dst.update(zip(small_names, _unpack(packed, local_shapes)))
        for k in small_names:
            out_g[k] = local_small[k].reshape(given[k].shape)
        return self.out


def _small_add(a, b):
    def body(a_ref, b_ref, o_ref):
        o_ref[...] = a_ref[...] + b_ref[...]

    return pl.pallas_call(body, name="chip_partial_small", out_shape=jax.ShapeDtypeStruct(a.shape, F32))(a, b)
```

```python
import functools
import math

import numpy as np
import jax
import jax.numpy as jnp
from jax import lax
from jax.experimental import pallas as pl
from jax.experimental.pallas import tpu as pltpu

F32 = jnp.float32
BF16 = jnp.bfloat16

D_MODEL = 1024
CHUNK = 128
A_GROUPS = 8
HEAD_DIM = 64
N_Q_HEADS = 16
N_KV_HEADS = 4
GQA_GROUP = N_Q_HEADS // N_KV_HEADS
KV_DIM = N_KV_HEADS * HEAD_DIM
BLOCK = 128
D_FF = 2816
N_FF = 2 * D_FF
FF_BLK = N_FF // 4
PLE_DIM = 256
EPS = 1e-6
NEG = -1e30
N_SHARD = 4

ADAM_LR = 0.001
ADAM_B1 = 0.9
ADAM_B2 = 0.999
ADAM_EPS = 1e-08
ADAM_WD = 0.01
ADAM_STEP = 10

VMEM_LIMIT = 60 * 1024 * 1024
MESH = pl.DeviceIdType.MESH
ANY = pl.BlockSpec(memory_space=pl.ANY)
SMEM = pl.BlockSpec(memory_space=pltpu.SMEM)

_SLOPES = [float(np.float32(2.0 ** (-8.0 * (h + 1) / N_Q_HEADS))) for h in range(N_Q_HEADS)]


def _dot(a, b):
    return jnp.dot(a, b, preferred_element_type=F32)


def _dot_nt(a, b):
    return lax.dot_general(a, b, (((1,), (1,)), ((), ())), preferred_element_type=F32)


def _dot_tn(a, b):
    return lax.dot_general(a, b, (((0,), (0,)), ((), ())), preferred_element_type=F32)


def _rms(x, g):
    r = lax.rsqrt(jnp.mean(x * x, axis=-1, keepdims=True) + EPS)
    xh = x * r
    return xh * g, xh, r


def _rms_bwd(dy, xh, r, g):
    dxh = dy * g
    dg = jnp.sum(dy * xh, axis=0, keepdims=True)
    dx = r * (dxh - xh * jnp.mean(dxh * xh, axis=-1, keepdims=True))
    return dx, dg


_GELU_C = math.sqrt(2.0 / math.pi)


def _gelu(x):
    t = jnp.tanh(_GELU_C * (x + 0.044715 * (x * x * x)))
    return 0.5 * x * (1.0 + t)


def _gelu_grad(x):
    x2 = x * x
    t = jnp.tanh(_GELU_C * (x + 0.044715 * (x2 * x)))
    return 0.5 * (1.0 + t) + 0.5 * x * (1.0 - t * t) * (_GELU_C * (1.0 + 3.0 * 0.044715 * x2))


def _sigmoid(x):
    return 0.5 * jnp.tanh(0.5 * x) + 0.5


def _load_once(pairs, sem):
    @pl.when(pl.program_id(0) == 0)
    def _():
        cps = [pltpu.make_async_copy(s, d, sem.at[i]) for i, (s, d) in enumerate(pairs)]
        for cp in cps:
            cp.start()
        for cp in cps:
            cp.wait()


def _params(n_axes=1, vmem=VMEM_LIMIT):
    return pltpu.CompilerParams(dimension_semantics=("arbitrary",) * n_axes, vmem_limit_bytes=vmem)


def _row_spec(tm, n, rev_nt=None):
    if rev_nt is None:
        return pl.BlockSpec((tm, n), lambda i: (i, 0))
    return pl.BlockSpec((tm, n), lambda i: (rev_nt - 1 - i, 0))


def _const_spec(shape):
    nd = len(shape)
    return pl.BlockSpec(shape, lambda i: (0,) * nd)


def _add_deps(body, in_specs, args, deps):
    nd = len(deps)
    if nd == 0:
        return body, list(in_specs), list(args)

    def wrapped(*refs):
        return body(*refs[nd:])

    return wrapped, [ANY] * nd + list(in_specs), list(deps) + list(args)


def _zero_first(refs):
    @pl.when(pl.program_id(0) == 0)
    def _():
        for r in refs:
            r[...] = jnp.zeros(r.shape, r.dtype)


def _mixer_a_fwd(x, nmix, gv, wsm, bsb, w_in, w_out):
    T = x.shape[0]
    tm = min(512, T)
    nt = T // tm
    nw = 2 * D_MODEL // N_SHARD

    def body(x_ref, nmix_ref, gv_ref, ws_ref, bsb_ref, w_in_hbm, w_out_hbm,
             h1_ref, zp_ref, w_in_v, w_out_v, gated_v, sem):
        _load_once([(w_in_hbm, w_in_v), (w_out_hbm, w_out_v)], sem)
        xv = x_ref[...]
        xn = _rms(xv, nmix_ref[...])[0].astype(BF16)
        for j in range(N_SHARD):
            zp_ref[:, j * nw:(j + 1) * nw] = _dot(xn, w_in_v[j])
        z = _gelu(zp_ref[...])
        u = z[:, :D_MODEL]
        vn = _rms(z[:, D_MODEL:], gv_ref[...])[0].astype(BF16)
        for c in range(tm // CHUNK):
            rows = slice(c * CHUNK, (c + 1) * CHUNK)
            for h in range(A_GROUPS):
                cols = slice(h * CHUNK, (h + 1) * CHUNK)
                s = _dot(ws_ref[h], vn[rows, cols]) + bsb_ref[h]
                gated_v[rows, cols] = (u[rows, cols] * s).astype(BF16)
        h1_ref[...] = xv + _dot(gated_v[...], w_out_v[...])

    return pl.pallas_call(
        body, name="mixer_a_fwd", grid=(nt,),
        in_specs=[_row_spec(tm, D_MODEL), _const_spec((1, D_MODEL)), _const_spec((1, D_MODEL)),
                  _const_spec((A_GROUPS, CHUNK, CHUNK)), _const_spec((A_GROUPS, CHUNK, CHUNK)), ANY, ANY],
        out_specs=[_row_spec(tm, D_MODEL), _row_spec(tm, 2 * D_MODEL)],
        out_shape=[jax.ShapeDtypeStruct((T, D_MODEL), F32), jax.ShapeDtypeStruct((T, 2 * D_MODEL), F32)],
        scratch_shapes=[pltpu.VMEM((N_SHARD, D_MODEL, nw), BF16), pltpu.VMEM((D_MODEL, D_MODEL), BF16),
                        pltpu.VMEM((tm, D_MODEL), BF16), pltpu.SemaphoreType.DMA((2,))],
        compiler_params=_params(),
    )(x, nmix, gv, wsm, bsb, w_in, w_out)


def _mixer_a_bwd(dh, x, zp, nmix, gv, wsm, bsb, tril, w_in, w_out, deps=()):
    T = x.shape[0]
    tm = min(256, T)
    nt = T // tm
    nw = 2 * D_MODEL // N_SHARD

    def body(dh_ref, x_ref, zp_ref, nmix_ref, gv_ref, ws_ref, bsb_ref, tril_ref, w_in_hbm, w_out_hbm,
             dx_ref, dwin_ref, dwout_ref, dws_ref, dbs_ref, dgv_ref, dnmix_ref,
             w_in_v, w_out_v, du_v, dvn_v, dbs_v, gated_ref, sem):
        _load_once([(w_in_hbm, w_in_v), (w_out_hbm, w_out_v)], sem)
        _zero_first([dws_ref, dbs_v, dgv_ref, dnmix_ref, dwin_ref, dwout_ref])
        i = pl.program_id(0)
        dhv = dh_ref[...]
        dhb = dhv.astype(BF16)
        xv = x_ref[...]
        xn, xh, r = _rms(xv, nmix_ref[...])
        xnb = xn.astype(BF16)
        zpv = zp_ref[...]
        z = _gelu(zpv)
        u = z[:, :D_MODEL]
        vn_f, vh, rv = _rms(z[:, D_MODEL:], gv_ref[...])
        vn = vn_f.astype(BF16)
        dgated = _dot_nt(dhb, w_out_v[...])
        for c in range(tm // CHUNK):
            rows = slice(c * CHUNK, (c + 1) * CHUNK)
            for h in range(A_GROUPS):
                cols = slice(h * CHUNK, (h + 1) * CHUNK)
                vn_h = vn[rows, cols]
                s = _dot(ws_ref[h], vn_h) + bsb_ref[h]
                dgt = dgated[rows, cols]
                u_h = u[rows, cols]
                gated_ref[rows, cols] = (u_h * s).astype(BF16)
                du_v[rows, cols] = dgt * s
                ds = dgt * u_h
                dsb = ds.astype(BF16)
                dws_ref[h] += _dot_nt(dsb, vn_h)
                dbs_v[h] += ds
                dvn_v[rows, cols] = _dot_tn(ws_ref[h], dsb)
        dwout_ref[...] += _dot_tn(gated_ref[...], dhb)
        dv, dgv = _rms_bwd(dvn_v[...], vh, rv, gv_ref[...])
        dgv_ref[...] += dgv
        dzu = (du_v[...] * _gelu_grad(zpv[:, :D_MODEL])).astype(BF16)
        dzv = (dv * _gelu_grad(zpv[:, D_MODEL:])).astype(BF16)
        dzs = (dzu[:, :nw], dzu[:, nw:], dzv[:, :nw], dzv[:, nw:])
        dxn = jnp.zeros((tm, D_MODEL), F32)
        for j in range(N_SHARD):
            dxn = dxn + _dot_nt(dzs[j], w_in_v[j])
            dwin_ref[j] += _dot_tn(xnb, dzs[j])
        dxx, dn = _rms_bwd(dxn, xh, r, nmix_ref[...])
        dnmix_ref[...] += dn
        dx_ref[...] = dhv + dxx

        @pl.when(i == nt - 1)
        def _():
            for h in range(A_GROUPS):
                dws_ref[h] = dws_ref[h] * tril_ref[...]
                dbs_ref[h] = jnp.broadcast_to(jnp.sum(dbs_v[h], axis=1, keepdims=True), (CHUNK, CHUNK))

    grp = (A_GROUPS, CHUNK, CHUNK)
    body, in_specs, args = _add_deps(
        body, [_row_spec(tm, D_MODEL), _row_spec(tm, D_MODEL), _row_spec(tm, 2 * D_MODEL),
               _const_spec((1, D_MODEL)), _const_spec((1, D_MODEL)), _const_spec(grp), _const_spec(grp),
               _const_spec((CHUNK, CHUNK)), ANY, ANY],
        [dh, x, zp, nmix, gv, wsm, bsb, tril, w_in, w_out], deps)
    return pl.pallas_call(
        body, name="mixer_a_bwd", grid=(nt,), in_specs=in_specs,
        out_specs=[_row_spec(tm, D_MODEL), _const_spec((N_SHARD, D_MODEL, nw)), _const_spec((D_MODEL, D_MODEL)),
                   _const_spec(grp), _const_spec(grp), _const_spec((1, D_MODEL)), _const_spec((1, D_MODEL))],
        out_shape=[jax.ShapeDtypeStruct((T, D_MODEL), F32), jax.ShapeDtypeStruct((N_SHARD, D_MODEL, nw), F32),
                   jax.ShapeDtypeStruct((D_MODEL, D_MODEL), F32),
                   jax.ShapeDtypeStruct(grp, F32), jax.ShapeDtypeStruct(grp, F32),
                   jax.ShapeDtypeStruct((1, D_MODEL), F32), jax.ShapeDtypeStruct((1, D_MODEL), F32)],
        scratch_shapes=[pltpu.VMEM((N_SHARD, D_MODEL, nw), BF16), pltpu.VMEM((D_MODEL, D_MODEL), BF16),
                        pltpu.VMEM((tm, D_MODEL), F32), pltpu.VMEM((tm, D_MODEL), F32),
                        pltpu.VMEM(grp, F32), pltpu.VMEM((tm, D_MODEL), BF16), pltpu.SemaphoreType.DMA((2,))],
        compiler_params=_params(),
    )(*args)


def _load_ffn_weights(w_up_hbm, w_dn_hbm, layer, w_up_v, w_dn_v, sem):
    _load_once([(w_up_hbm, w_up_v), (w_dn_hbm, w_dn_v)], sem)


def _ffn_fwd(h, nffn, cw, cb, w_up, w_dn, layer):
    T = h.shape[0]
    tm = min(256, T)
    nt = T // tm

    def body(h_ref, n_ref, cw_ref, cb_ref, w_up_hbm, w_dn_hbm, out_ref, hh_ref, c_ref,
             w_up_v, w_dn_v, carry_v, sem):
        _load_ffn_weights(w_up_hbm, w_dn_hbm, layer, w_up_v, w_dn_v, sem)
        _zero_first([carry_v])
        xv = h_ref[...]
        xf = _rms(xv, n_ref[...])[0].astype(BF16)
        acc = xv
        for j in range(2):
            cs = []
            for blk in (j, j + 2):
                cols = slice(blk * FF_BLK, (blk + 1) * FF_BLK)
                hh = _dot(xf, w_up_v[blk])
                hh_ref[:, cols] = hh.astype(BF16)
                ext = jnp.concatenate([carry_v[blk], hh], axis=0)
                carry_v[blk] = hh[tm - 8:, :]
                s1 = pltpu.roll(ext, 1, 0)[8:]
                s2 = pltpu.roll(ext, 2, 0)[8:]
                cv = (cb_ref[:, cols] + cw_ref[0:1, cols] * s2 + cw_ref[1:2, cols] * s1
                      + cw_ref[2:3, cols] * hh)
                c_ref[:, cols] = cv.astype(BF16)
                cs.append(cv)
            act = (cs[0] * _sigmoid(cs[0]) * cs[1]).astype(BF16)
            acc = acc + _dot(act, w_dn_v[j * FF_BLK:(j + 1) * FF_BLK, :])
        out_ref[...] = acc

    return pl.pallas_call(
        body, name=f"ffn_fwd{layer}", grid=(nt,),
        in_specs=[_row_spec(tm, D_MODEL), _const_spec((1, D_MODEL)), _const_spec((3, N_FF)),
                  _const_spec((1, N_FF)), ANY, ANY],
        out_specs=[_row_spec(tm, D_MODEL), _row_spec(tm, N_FF), _row_spec(tm, N_FF)],
        out_shape=[jax.ShapeDtypeStruct((T, D_MODEL), F32), jax.ShapeDtypeStruct((T, N_FF), BF16),
                   jax.ShapeDtypeStruct((T, N_FF), BF16)],
        scratch_shapes=[pltpu.VMEM((N_SHARD, D_MODEL, FF_BLK), BF16), pltpu.VMEM((D_FF, D_MODEL), BF16),
                        pltpu.VMEM((N_SHARD, 8, FF_BLK), F32), pltpu.SemaphoreType.DMA((2 * N_SHARD,))],
        compiler_params=_params(),
    )(h, nffn, cw, cb, w_up, w_dn)


def _wgrad(a, b, bn, col_sharded, name, deps=()):
    T, K = a.shape
    N = b.shape[1]
    tt = min(2048, T)
    nn, ntt = N // bn, T // tt
    kr = K // N_SHARD

    def body(a_ref, b_ref, o_ref):
        @pl.when(pl.program_id(1) == 0)
        def _():
            o_ref[...] = jnp.zeros(o_ref.shape, F32)
        d = _dot_tn(a_ref[...].astype(BF16), b_ref[...].astype(BF16))
        if col_sharded:
            o_ref[...] += d
        else:
            for j in range(N_SHARD):
                o_ref[j] += d[j * kr:(j + 1) * kr]

    if col_sharded:
        assert nn == N_SHARD
        out_spec = pl.BlockSpec((None, K, bn), lambda n, t: (n, 0, 0))
        out_shape = jax.ShapeDtypeStruct((N_SHARD, K, bn), F32)
    else:
        out_spec = pl.BlockSpec((N_SHARD, kr, bn), lambda n, t: (0, 0, n))
        out_shape = jax.ShapeDtypeStruct((N_SHARD, kr, N), F32)
    body, in_specs, args = _add_deps(
        body, [pl.BlockSpec((tt, K), lambda n, t: (t, 0)), pl.BlockSpec((tt, bn), lambda n, t: (t, n))],
        [a, b], deps)
    return pl.pallas_call(
        body, name=name, grid=(nn, ntt), in_specs=in_specs, out_specs=out_spec, out_shape=out_shape,
        compiler_params=pltpu.CompilerParams(dimension_semantics=("arbitrary",) * 2, vmem_limit_bytes=VMEM_LIMIT),
    )(*args)


def _ffn_bwd(dh, h, hh, c, nffn, cw, w_up, w_dn, layer, deps=(), between=None):
    T = h.shape[0]
    tm = min(256, T)
    nt = T // tm

    def body(dh_ref, h_ref, hh_ref, c_ref, n_ref, cw_ref, w_up_hbm, w_dn_hbm,
             dhin_ref, act_ref, dhh_ref, xf_ref, dcw_ref, dcb_ref, dn_ref,
             w_up_v, w_dn_v, carry_v, sem):
        _load_ffn_weights(w_up_hbm, w_dn_hbm, layer, w_up_v, w_dn_v, sem)
        _zero_first([carry_v, dcw_ref, dcb_ref, dn_ref])
        dout = dh_ref[...]
        doutb = dout.astype(BF16)
        xf_f, xh, r = _rms(h_ref[...], n_ref[...])
        xf_ref[...] = xf_f.astype(BF16)
        dxf = jnp.zeros((tm, D_MODEL), F32)
        for j in range(2):
            blks = (j, j + 2)
            cg = c_ref[:, j * FF_BLK:(j + 1) * FF_BLK].astype(F32)
            cu = c_ref[:, (j + 2) * FF_BLK:(j + 3) * FF_BLK].astype(F32)
            sg = _sigmoid(cg)
            sil = cg * sg
            act_ref[:, j * FF_BLK:(j + 1) * FF_BLK] = (sil * cu).astype(BF16)
            dact = _dot_nt(doutb, w_dn_v[j * FF_BLK:(j + 1) * FF_BLK, :])
            dcs = (dact * cu * (sg * (1.0 + cg * (1.0 - sg))), dact * sil)
            for blk, dc in zip(blks, dcs):
                cols = slice(blk * FF_BLK, (blk + 1) * FF_BLK)
                hhv = hh_ref[:, cols].astype(F32)
                ext = jnp.concatenate([dc, carry_v[blk]], axis=0)
                carry_v[blk] = dc[:8, :]
                n = tm + 8
                a1 = pltpu.roll(ext, n - 1, 0)[:tm]
                a2 = pltpu.roll(ext, n - 2, 0)[:tm]
                dcb_ref[:, cols] += jnp.sum(dc, axis=0, keepdims=True)
                dcw_ref[0:1, cols] += jnp.sum(a2 * hhv, axis=0, keepdims=True)
                dcw_ref[1:2, cols] += jnp.sum(a1 * hhv, axis=0, keepdims=True)
                dcw_ref[2:3, cols] += jnp.sum(dc * hhv, axis=0, keepdims=True)
                dhh = (cw_ref[2:3, cols] * dc + cw_ref[1:2, cols] * a1 + cw_ref[0:1, cols] * a2).astype(BF16)
                dhh_ref[:, cols] = dhh
                dxf = dxf + _dot_nt(dhh, w_up_v[blk])
        dxx, dn = _rms_bwd(dxf, xh, r, n_ref[...])
        dn_ref[...] += dn
        dhin_ref[...] = dout + dxx

    rev = functools.partial(_row_spec, rev_nt=nt)
    body, in_specs, args = _add_deps(
        body, [rev(tm, D_MODEL), rev(tm, D_MODEL), rev(tm, N_FF), rev(tm, N_FF),
               _const_spec((1, D_MODEL)), _const_spec((3, N_FF)), ANY, ANY],
        [dh, h, hh, c, nffn, cw, w_up, w_dn], deps)
    dhin, act, dhh, xf, dcw, dcb, dn = pl.pallas_call(
        body, name=f"ffn_bwd{layer}", grid=(nt,), in_specs=in_specs,
        out_specs=[rev(tm, D_MODEL), rev(tm, D_FF), rev(tm, N_FF), rev(tm, D_MODEL),
                   _const_spec((3, N_FF)), _const_spec((1, N_FF)), _const_spec((1, D_MODEL))],
        out_shape=[jax.ShapeDtypeStruct((T, D_MODEL), F32), jax.ShapeDtypeStruct((T, D_FF), BF16),
                   jax.ShapeDtypeStruct((T, N_FF), BF16), jax.ShapeDtypeStruct((T, D_MODEL), BF16),
                   jax.ShapeDtypeStruct((3, N_FF), F32), jax.ShapeDtypeStruct((1, N_FF), F32),
                   jax.ShapeDtypeStruct((1, D_MODEL), F32)],
        scratch_shapes=[pltpu.VMEM((N_SHARD, D_MODEL, FF_BLK), BF16), pltpu.VMEM((D_FF, D_MODEL), BF16),
                        pltpu.VMEM((N_SHARD, 8, FF_BLK), F32), pltpu.SemaphoreType.DMA((2 * N_SHARD,))],
        compiler_params=_params(),
    )(*args)
    deps2 = between(dhin) if between is not None else ()
    dwdn = _wgrad(act, dh, D_MODEL // 2, False, f"wgrad_ffn_down{layer}", deps=deps2)
    dwup = _wgrad(xf, dhh, FF_BLK, True, f"wgrad_ffn_up{layer}")
    return dhin, dwup, dwdn, dcw, dcb, dn


def _load_ple_weights(w_pin_hbm, w_gate_hbm, layer, w_pin_v, w_gate_v, sem, extra=()):
    _load_once([(w_pin_hbm, w_pin_v), (w_gate_hbm, w_gate_v)] + list(extra), sem)


def _ple_fwd_kv(h, p, nple, bg, nkv, w_pin, w_gate, w_kv):
    T = h.shape[0]
    tm = min(512, T)
    nt = T // tm
    pw = D_MODEL // N_SHARD

    def body(h_ref, p_ref, n_ref, bg_ref, nkv_ref, w_pin_hbm, w_gate_hbm, w_kv_hbm,
             out_ref, pe_ref, a_ref, kv_ref, w_pin_v, w_gate_v, w_kv_v, sem):
        _load_ple_weights(w_pin_hbm, w_gate_hbm, 0, w_pin_v, w_gate_v, sem, [(w_kv_hbm, w_kv_v)])
        xv = h_ref[...]
        xg = _rms(xv, n_ref[...])[0].astype(BF16)
        a = _dot(xg, w_gate_v[...]) + bg_ref[...]
        a_ref[...] = a
        pb = p_ref[...].astype(BF16)
        for j in range(N_SHARD):
            pe_ref[:, j * pw:(j + 1) * pw] = _dot(pb, w_pin_v[j])
        hn = xv + pe_ref[...] * _sigmoid(a)
        out_ref[...] = hn
        kvn = _rms(hn, nkv_ref[...])[0].astype(BF16)
        kv_ref[...] = _dot(kvn, w_kv_v[...]).astype(BF16)

    vec = _const_spec((1, D_MODEL))
    return pl.pallas_call(
        body, name="ple_fwd0", grid=(nt,),
        in_specs=[_row_spec(tm, D_MODEL), _row_spec(tm, PLE_DIM), vec, vec, vec, ANY, ANY, ANY],
        out_specs=[_row_spec(tm, D_MODEL), _row_spec(tm, D_MODEL), _row_spec(tm, D_MODEL),
                   _row_spec(tm, 2 * KV_DIM)],
        out_shape=[jax.ShapeDtypeStruct((T, D_MODEL), F32), jax.ShapeDtypeStruct((T, D_MODEL), F32),
                   jax.ShapeDtypeStruct((T, D_MODEL), F32), jax.ShapeDtypeStruct((T, 2 * KV_DIM), BF16)],
        scratch_shapes=[pltpu.VMEM((N_SHARD, PLE_DIM, pw), BF16), pltpu.VMEM((D_MODEL, D_MODEL), BF16),
                        pltpu.VMEM((D_MODEL, 2 * KV_DIM), BF16), pltpu.SemaphoreType.DMA((2 * N_SHARD + 1,))],
        compiler_params=_params(),
    )(h, p, nple, bg, nkv, w_pin, w_gate, w_kv)


def _ple_fwd_final(h, p, tgt, nple, bg, nfin, w_pin, w_gate):
    T = h.shape[0]
    tm = min(512, T)
    nt = T // tm
    pw = D_MODEL // N_SHARD

    def body(h_ref, p_ref, t_ref, n_ref, bg_ref, nf_ref, w_pin_hbm, w_gate_hbm,
             dh_ref, pe_ref, a_ref, loss_ref, dnf_ref, w_pin_v, w_gate_v, sem):
        _load_ple_weights(w_pin_hbm, w_gate_hbm, 1, w_pin_v, w_gate_v, sem)
        _zero_first([loss_ref, dnf_ref])
        xv = h_ref[...]
        xg = _rms(xv, n_ref[...])[0].astype(BF16)
        a = _dot(xg, w_gate_v[...]) + bg_ref[...]
        a_ref[...] = a
        pb = p_ref[...].astype(BF16)
        for j in range(N_SHARD):
            pe_ref[:, j * pw:(j + 1) * pw] = _dot(pb, w_pin_v[j])
        hn = xv + pe_ref[...] * _sigmoid(a)
        y, yh, r = _rms(hn, nf_ref[...])
        diff = y - t_ref[...]
        loss_ref[...] += 0.5 * jnp.sum(jnp.mean(diff * diff, axis=-1, keepdims=True))
        dy = diff * (1.0 / D_MODEL)
        dhn, dnf = _rms_bwd(dy, yh, r, nf_ref[...])
        dnf_ref[...] += dnf
        dh_ref[...] = dhn

    vec = _const_spec((1, D_MODEL))
    return pl.pallas_call(
        body, name="ple_fwd1", grid=(nt,),
        in_specs=[_row_spec(tm, D_MODEL), _row_spec(tm, PLE_DIM), _row_spec(tm, D_MODEL), vec, vec, vec, ANY, ANY],
        out_specs=[_row_spec(tm, D_MODEL), _row_spec(tm, D_MODEL), _row_spec(tm, D_MODEL),
                   _const_spec((8, 128)), vec],
        out_shape=[jax.ShapeDtypeStruct((T, D_MODEL), F32), jax.ShapeDtypeStruct((T, D_MODEL), F32),
                   jax.ShapeDtypeStruct((T, D_MODEL), F32), jax.ShapeDtypeStruct((8, 128), F32),
                   jax.ShapeDtypeStruct((1, D_MODEL), F32)],
        scratch_shapes=[pltpu.VMEM((N_SHARD, PLE_DIM, pw), BF16), pltpu.VMEM((D_MODEL, D_MODEL), BF16),
                        pltpu.SemaphoreType.DMA((2 * N_SHARD,))],
        compiler_params=_params(),
    )(h, p, tgt, nple, bg, nfin, w_pin, w_gate)


def _ple_bwd(dh, hb, pe, a, p, nple, w_gate, layer, kv_args=None):
    T = hb.shape[0]
    tm = min(512, T)
    nt = T // tm
    with_kv = kv_args is not None
    pw = D_MODEL // N_SHARD

    def body(*refs):
        if with_kv:
            (dh_ref, hb_ref, pe_ref, a_ref, p_ref, n_ref, w_gate_hbm, hc_ref, dkv_ref, nkv_ref, w_kv_hbm,
             dhb_ref, dwpin_ref, dwgate_ref, dbg_ref, dn_ref, dwkv_ref, dnkv_ref,
             w_gate_v, w_kv_v, sem) = refs
        else:
            (dh_ref, hb_ref, pe_ref, a_ref, p_ref, n_ref, w_gate_hbm,
             dhb_ref, dwpin_ref, dwgate_ref, dbg_ref, dn_ref, w_gate_v, sem) = refs
        pairs = [(w_gate_hbm, w_gate_v)]
        if with_kv:
            pairs.append((w_kv_hbm, w_kv_v))
        _load_once(pairs, sem)
        _zero_first([dwpin_ref, dwgate_ref, dbg_ref, dn_ref] + ([dwkv_ref, dnkv_ref] if with_kv else []))
        do = dh_ref[...]
        if with_kv:
            dkvb = dkv_ref[...].astype(BF16)
            dkvn = _dot_nt(dkvb, w_kv_v[...])
            kvn, kh, kr = _rms(hc_ref[...], nkv_ref[...])
            dwkv_ref[...] += _dot_tn(kvn.astype(BF16), dkvb)
            dk, dnkv = _rms_bwd(dkvn, kh, kr, nkv_ref[...])
            dnkv_ref[...] += dnkv
            do = do + dk
        gate = _sigmoid(a_ref[...])
        dpe = (do * gate).astype(BF16)
        pb = p_ref[...].astype(BF16)
        for j in range(N_SHARD):
            dwpin_ref[j] += _dot_tn(pb, dpe[:, j * pw:(j + 1) * pw])
        da = do * pe_ref[...] * (gate * (1.0 - gate))
        dab = da.astype(BF16)
        dbg_ref[...] += jnp.sum(da, axis=0, keepdims=True)
        dxg = _dot_nt(dab, w_gate_v[...])
        xg, xh, r = _rms(hb_ref[...], n_ref[...])
        dwgate_ref[...] += _dot_tn(xg.astype(BF16), dab)
        dxx, dn = _rms_bwd(dxg, xh, r, n_ref[...])
        dn_ref[...] += dn
        dhb_ref[...] = do + dxx

    vec = _const_spec((1, D_MODEL))
    row = _row_spec(tm, D_MODEL)
    in_specs = [row, row, row, row, _row_spec(tm, PLE_DIM), vec, ANY]
    args = [dh, hb, pe, a, p, nple, w_gate]
    out_specs = [row, _const_spec((N_SHARD, PLE_DIM, pw)), _const_spec((D_MODEL, D_MODEL)), vec, vec]
    out_shape = [jax.ShapeDtypeStruct((T, D_MODEL), F32), jax.ShapeDtypeStruct((N_SHARD, PLE_DIM, pw), F32),
                 jax.ShapeDtypeStruct((D_MODEL, D_MODEL), F32),
                 jax.ShapeDtypeStruct((1, D_MODEL), F32), jax.ShapeDtypeStruct((1, D_MODEL), F32)]
    scratch = [pltpu.VMEM((D_MODEL, D_MODEL), BF16)]
    if with_kv:
        hc, dkv, nkv, w_kv = kv_args
        in_specs += [row, _row_spec(tm, 2 * KV_DIM), vec, ANY]
        args += [hc, dkv, nkv, w_kv]
        out_specs += [_const_spec((D_MODEL, 2 * KV_DIM)), vec]
        out_shape += [jax.ShapeDtypeStruct((D_MODEL, 2 * KV_DIM), F32), jax.ShapeDtypeStruct((1, D_MODEL), F32)]
        scratch.append(pltpu.VMEM((D_MODEL, 2 * KV_DIM), BF16))
    scratch.append(pltpu.SemaphoreType.DMA((N_SHARD + 1,)))
    return pl.pallas_call(
        body, name=f"ple_bwd{layer}", grid=(nt,), in_specs=in_specs, out_specs=out_specs,
        out_shape=out_shape, scratch_shapes=scratch, compiler_params=_params(),
    )(*args)


GROUP_ROWS = GQA_GROUP * BLOCK


def _band_bias():
    ii = lax.broadcasted_iota(jnp.int32, (GROUP_ROWS, 2 * BLOCK), 0)
    jj = lax.broadcasted_iota(jnp.int32, (GROUP_ROWS, 2 * BLOCK), 1)
    dist = (ii & (BLOCK - 1)) + BLOCK - jj
    inband = (dist >= 0) & (dist < BLOCK)
    distf = dist.astype(F32)
    grp = ii >> 7
    bias = []
    for kh in range(N_KV_HEADS):
        sl = _SLOPES[kh * GQA_GROUP:(kh + 1) * GQA_GROUP]
        slope = jnp.where(grp == 0, sl[0], jnp.where(grp == 1, sl[1], jnp.where(grp == 2, sl[2], sl[3])))
        bias.append(jnp.where(inband, slope * distf, -NEG))
    return bias, jj


def _stack_heads(x, kh):
    return jnp.concatenate([x[:, (kh * GQA_GROUP + g) * HEAD_DIM:(kh * GQA_GROUP + g + 1) * HEAD_DIM]
                            for g in range(GQA_GROUP)], axis=0)


def _sink_column(sink_ref, kh):
    grp = lax.broadcasted_iota(jnp.int32, (GROUP_ROWS, 1), 0) >> 7
    s = [sink_ref[kh * GQA_GROUP + g] for g in range(GQA_GROUP)]
    return jnp.where(grp == 0, s[0], jnp.where(grp == 1, s[1], jnp.where(grp == 2, s[2], s[3])))


def _attn_fwd(h, nmix, kv, sinks, w_q, w_o):
    T = h.shape[0]
    tm = min(512, T)
    nt = T // tm
    nb = tm // BLOCK

    def body(h_ref, n_ref, kv_ref, kvp_ref, sink_ref, w_q_hbm, w_o_hbm,
             out_ref, q_ref, ao_ref, lse_ref, w_q_v, w_o_v, kvs_v, sem):
        _load_once([(w_q_hbm, w_q_v), (w_o_hbm, w_o_v)], sem)
        ti = pl.program_id(0)
        xv = h_ref[...]
        xn = _rms(xv, n_ref[...])[0].astype(BF16)
        q_ref[...] = (_dot(xn, w_q_v[...]) * (HEAD_DIM ** -0.5)).astype(BF16)
        kvs_v[0:BLOCK, :] = kvp_ref[...]
        kvs_v[BLOCK:, :] = kv_ref[...]
        lane = lax.broadcasted_iota(jnp.int32, (BLOCK, 128), 1)
        ii = lax.broadcasted_iota(jnp.int32, (BLOCK, 2 * BLOCK), 0)
        jj = lax.broadcasted_iota(jnp.int32, (BLOCK, 2 * BLOCK), 1)
        dist = ii + BLOCK - jj
        inband = (dist >= 0) & (dist < BLOCK)
        distf = dist.astype(F32)

        def blk_body(b, carry):
            r0 = pl.multiple_of(b * BLOCK, BLOCK)
            valid = inband & ((jj >= BLOCK) | jnp.logical_not(jnp.logical_and(ti == 0, b == 0)))
            qb = q_ref[pl.ds(r0, BLOCK), :]
            band = kvs_v[pl.ds(r0, 2 * BLOCK), :]
            lse_mat = jnp.zeros((BLOCK, 128), F32)
            outs = []
            for hq in range(N_Q_HEADS):
                kh = hq // GQA_GROUP
                k_h = band[:, kh * HEAD_DIM:(kh + 1) * HEAD_DIM]
                v_h = band[:, KV_DIM + kh * HEAD_DIM:KV_DIM + (kh + 1) * HEAD_DIM]
                s = _dot_nt(qb[:, hq * HEAD_DIM:(hq + 1) * HEAD_DIM], k_h) - _SLOPES[hq] * distf
                s = jnp.where(valid, s, NEG)
                sink = sink_ref[hq]
                m = jnp.maximum(jnp.max(s, axis=1, keepdims=True), sink)
                e = jnp.exp(s - m)
                den = jnp.sum(e, axis=1, keepdims=True) + jnp.exp(sink - m)
                outs.append(_dot((e / den).astype(BF16), v_h))
                lse_mat = jnp.where(lane == hq, m + jnp.log(den), lse_mat)
            ao_ref[pl.ds(r0, BLOCK), :] = jnp.concatenate(outs, axis=1).astype(BF16)
            lse_ref[pl.ds(r0, BLOCK), :] = lse_mat
            return carry

        lax.fori_loop(0, nb, blk_body, 0)
        out_ref[...] = xv + _dot(ao_ref[...], w_o_v[...])

    row = _row_spec(tm, D_MODEL)
    prev_spec = pl.BlockSpec((BLOCK, 2 * KV_DIM), lambda i: (jnp.maximum(i * nb - 1, 0), 0))
    return pl.pallas_call(
        body, name="attn_fwd", grid=(nt,),
        in_specs=[row, _const_spec((1, D_MODEL)), _row_spec(tm, 2 * KV_DIM), prev_spec, SMEM, ANY, ANY],
        out_specs=[row, row, row, _row_spec(tm, 128)],
        out_shape=[jax.ShapeDtypeStruct((T, D_MODEL), F32), jax.ShapeDtypeStruct((T, D_MODEL), BF16),
                   jax.ShapeDtypeStruct((T, D_MODEL), BF16), jax.ShapeDtypeStruct((T, 128), F32)],
        scratch_shapes=[pltpu.VMEM((D_MODEL, D_MODEL), BF16), pltpu.VMEM((D_MODEL, D_MODEL), BF16),
                        pltpu.VMEM((tm + BLOCK, 2 * KV_DIM), BF16), pltpu.SemaphoreType.DMA((2,))],
        compiler_params=_params(),
    )(h, nmix, kv, kv, sinks, w_q, w_o)


def _attn_bwd(dh, h, q, kv, ao, lse, nmix, sinks, w_q, w_o):
    T = h.shape[0]
    tm = min(512, T)
    nt = T // tm
    nb = tm // BLOCK

    def body(dh_ref, h_ref, q_ref, kv_ref, kvp_ref, ao_ref, lse_ref, n_ref, sink_ref, w_q_hbm, w_o_hbm,
             dhin_ref, dwq_ref, dwo_ref, dkv_ref, dsink_ref, dn_ref,
             w_q_v, w_o_v, kvs_v, dao_v, dq_v, dkv_v, carry_v, sem):
        _load_once([(w_q_hbm, w_q_v), (w_o_hbm, w_o_v)], sem)
        _zero_first([carry_v, dsink_ref, dn_ref, dwq_ref, dwo_ref])
        ti = nt - 1 - pl.program_id(0)
        dout = dh_ref[...]
        doutb = dout.astype(BF16)
        dao_v[...] = _dot_nt(doutb, w_o_v[...])
        dwo_ref[...] += _dot_tn(ao_ref[...], doutb)
        kvs_v[0:BLOCK, :] = kvp_ref[...]
        kvs_v[BLOCK:, :] = kv_ref[...]
        dkv_v[0:tm, :] = jnp.zeros((tm, 2 * KV_DIM), F32)
        dkv_v[tm:, :] = carry_v[...]
        lane = lax.broadcasted_iota(jnp.int32, (BLOCK, 128), 1)
        lane8 = lax.broadcasted_iota(jnp.int32, (8, 128), 1)
        bias, jj = _band_bias()

        def blk_body(b, dsk):
            r0 = pl.multiple_of(b * BLOCK, BLOCK)
            no_prev = jnp.logical_and(jnp.logical_and(ti == 0, b == 0), jj < BLOCK)
            qb = q_ref[pl.ds(r0, BLOCK), :]
            band = kvs_v[pl.ds(r0, 2 * BLOCK), :]
            aob = ao_ref[pl.ds(r0, BLOCK), :].astype(F32)
            daob = dao_v[pl.ds(r0, BLOCK), :]
            lse_mat = lse_ref[pl.ds(r0, BLOCK), :]
            dqs = []
            dks = []
            dvs = []
            for kh in range(N_KV_HEADS):
                k_h = band[:, kh * HEAD_DIM:(kh + 1) * HEAD_DIM]
                v_h = band[:, KV_DIM + kh * HEAD_DIM:KV_DIM + (kh + 1) * HEAD_DIM]
                q_g = _stack_heads(qb, kh)
                dao_g = _stack_heads(daob, kh)
                s = _dot_nt(q_g, k_h) - bias[kh]
                s = jnp.where(no_prev, NEG, s)
                lse = jnp.concatenate(
                    [jnp.sum(jnp.where(lane == kh * GQA_GROUP + g, lse_mat, 0.0), axis=1, keepdims=True)
                     for g in range(GQA_GROUP)], axis=0)
                pr = jnp.exp(s - lse)
                dd = jnp.sum(dao_g * _stack_heads(aob, kh), axis=1, keepdims=True)
                dao_gb = dao_g.astype(BF16)
                dp = _dot_nt(dao_gb, v_h)
                dsb = (pr * (dp - dd)).astype(BF16)
                dq_g = _dot(dsb, k_h) * (HEAD_DIM ** -0.5)
                dks.append(_dot_tn(dsb, q_g))
                dvs.append(_dot_tn(pr.astype(BF16), dao_gb))
                psink = jnp.exp(_sink_column(sink_ref, kh) - lse) * dd
                for g in range(GQA_GROUP):
                    rows = slice(g * BLOCK, (g + 1) * BLOCK)
                    dqs.append(dq_g[rows])
                    dsk = dsk - jnp.where(lane8 == kh * GQA_GROUP + g, jnp.sum(psink[rows]), 0.0)
            dq_v[pl.ds(r0, BLOCK), :] = jnp.concatenate(dqs, axis=1)
            dkv_v[pl.ds(r0, 2 * BLOCK), :] += jnp.concatenate(dks + dvs, axis=1)
            return dsk

        dsk = lax.fori_loop(0, nb, blk_body, jnp.zeros((8, 128), F32))
        dsink_ref[...] += dsk
        dqb = dq_v[...].astype(BF16)
        dxn = _dot_nt(dqb, w_q_v[...])
        xn, xh, r = _rms(h_ref[...], n_ref[...])
        dwq_ref[...] += _dot_tn(xn.astype(BF16), dqb)
        dxx, dn = _rms_bwd(dxn, xh, r, n_ref[...])
        dn_ref[...] += dn
        dhin_ref[...] = dout + dxx
        dkv_ref[...] = dkv_v[BLOCK:, :]
        carry_v[...] = dkv_v[0:BLOCK, :]

    rev = functools.partial(_row_spec, rev_nt=nt)
    row = rev(tm, D_MODEL)
    prev_spec = pl.BlockSpec((BLOCK, 2 * KV_DIM), lambda i: (jnp.maximum((nt - 1 - i) * nb - 1, 0), 0))
    return pl.pallas_call(
        body, name="attn_bwd", grid=(nt,),
        in_specs=[row, row, row, rev(tm, 2 * KV_DIM), prev_spec, row, rev(tm, 128),
                  _const_spec((1, D_MODEL)), SMEM, ANY, ANY],
        out_specs=[row, _const_spec((D_MODEL, D_MODEL)), _const_spec((D_MODEL, D_MODEL)), rev(tm, 2 * KV_DIM),
                   _const_spec((8, 128)), _const_spec((1, D_MODEL))],
        out_shape=[jax.ShapeDtypeStruct((T, D_MODEL), F32), jax.ShapeDtypeStruct((D_MODEL, D_MODEL), F32),
                   jax.ShapeDtypeStruct((D_MODEL, D_MODEL), F32), jax.ShapeDtypeStruct((T, 2 * KV_DIM), F32),
                   jax.ShapeDtypeStruct((8, 128), F32), jax.ShapeDtypeStruct((1, D_MODEL), F32)],
        scratch_shapes=[pltpu.VMEM((D_MODEL, D_MODEL), BF16), pltpu.VMEM((D_MODEL, D_MODEL), BF16),
                        pltpu.VMEM((tm + BLOCK, 2 * KV_DIM), BF16), pltpu.VMEM((tm, D_MODEL), F32),
                        pltpu.VMEM((tm, D_MODEL), F32), pltpu.VMEM((tm + BLOCK, 2 * KV_DIM), F32),
                        pltpu.VMEM((BLOCK, 2 * KV_DIM), F32), pltpu.SemaphoreType.DMA((2,))],
        compiler_params=_params(),
    )(dh, h, q, kv, kv, ao, lse, nmix, sinks, w_q, w_o)


def _mesh_pos():
    return lax.axis_index("x"), lax.axis_index("y"), lax.axis_index("c")


def _other_chips(x, y):
    return [(1 - x, y), (x, 1 - y), (1 - x, 1 - y)]


HBM_SPEC = pl.BlockSpec(memory_space=pltpu.HBM)
SEM_SPEC = pl.BlockSpec(memory_space=pltpu.SEMAPHORE)


def _split_call(name, bufs, waits=(), starts=(), after=()):
    n, nw, ns, na = len(bufs), len(waits), len(starts), len(after)

    def body(*refs):
        brefs = refs[:n]
        wsems = [(refs[n + 2 * k], refs[n + 2 * k + 1]) for k in range(nw)]
        o = n + 2 * nw + na
        ssems = [(refs[o + 2 * k], refs[o + 2 * k + 1]) for k in range(ns)]
        for (ss, rs), (_, _, fn) in zip(wsems, waits):
            for sending, arriving in fn(brefs, ss, rs):
                sending.wait_send()
                arriving.wait_recv()
        for (ss, rs), (_, fn) in zip(ssems, starts):
            for sending, _ in fn(brefs, ss, rs):
                sending.start()
        if ns:
            token = refs[o + 2 * ns + n]
            token[...] = jnp.zeros(token.shape, token.dtype)

    out_shape, out_specs = [], []
    for cnt, _ in starts:
        out_shape += [pltpu.SemaphoreType.DMA((cnt,)), pltpu.SemaphoreType.DMA((cnt,))]
        out_specs += [SEM_SPEC, SEM_SPEC]
    out_shape += [pltpu.HBM(b.shape, b.dtype) for b in bufs]
    out_specs += [HBM_SPEC] * n
    if ns:
        out_shape.append(jax.ShapeDtypeStruct((8, 128), F32))
        out_specs.append(pl.BlockSpec(memory_space=pltpu.VMEM))
    args = [pltpu.with_memory_space_constraint(b, pltpu.HBM) for b in bufs]
    for ss, rs, _ in waits:
        args += [ss, rs]
    args += list(after)
    res = pl.pallas_call(
        body, name=name, out_shape=tuple(out_shape),
        in_specs=[HBM_SPEC] * n + [SEM_SPEC] * (2 * nw) + [ANY] * na, out_specs=tuple(out_specs),
        input_output_aliases={i: 2 * ns + i for i in range(n)},
        compiler_params=pltpu.CompilerParams(has_side_effects=pltpu.SideEffectType.DATAFLOW_SIDE_EFFECTING),
    )(*args)
    sems = [(res[2 * k], res[2 * k + 1]) for k in range(ns)]
    return list(res[2 * ns:2 * ns + n]), sems, (res[2 * ns + n] if ns else None)


def _cast_place(items, name):
    n = len(items)
    mats = [a.shape[-2:] for a, _, _ in items]

    def body(*refs):
        ins, outs, scr, sem = refs[:n], refs[n:2 * n], refs[2 * n:3 * n], refs[3 * n]
        x, y, _ = _mesh_pos()
        cps = []
        for t in range(n):
            scr[t][...] = ins[t][...].astype(scr[t].dtype)
            cp = pltpu.make_async_copy(scr[t], outs[t].at[2 * x + y], sem.at[t])
            cp.start()
            cps.append(cp)
        for cp in cps:
            cp.wait()

    def spec(idx, shape):
        return pl.BlockSpec((None,) * len(idx) + tuple(shape), lambda i: tuple(idx) + (0, 0))

    return pl.pallas_call(
        body, name=name, grid=(1,),
        in_specs=[spec(idx, mat) for (_, idx, _), mat in zip(items, mats)], out_specs=[ANY] * n,
        out_shape=[jax.ShapeDtypeStruct((N_SHARD,) + tuple(mat), dt) for (_, _, dt), mat in zip(items, mats)],
        scratch_shapes=[pltpu.VMEM(tuple(mat), dt) for (_, _, dt), mat in zip(items, mats)]
        + [pltpu.SemaphoreType.DMA((n,))],
        compiler_params=_params(),
    )(*[a for a, _, _ in items])


def _gather_ici(idx):
    def fn(bufs, ss, rs):
        x, y, c = _mesh_pos()
        pairs = []
        for k, t in enumerate(idx):
            half = bufs[t].shape[1] // 2
            mine = bufs[t].at[2 * x + y, pl.ds(c * half, half), :]
            for j, (cx, cy) in enumerate(_other_chips(x, y)):
                theirs = bufs[t].at[2 * cx + cy, pl.ds(c * half, half), :]
                sem = dict(send_sem=ss.at[3 * k + j], recv_sem=rs.at[3 * k + j],
                           device_id=(cx, cy, c), device_id_type=MESH)
                pairs.append((pltpu.make_async_remote_copy(src_ref=mine, dst_ref=mine, **sem),
                              pltpu.make_async_remote_copy(src_ref=mine, dst_ref=theirs, **sem)))
        return pairs
    return fn


def _gather_d2d(idx):
    def fn(bufs, ss, rs):
        x, y, c = _mesh_pos()
        pairs = []
        for k, t in enumerate(idx):
            half = bufs[t].shape[1] // 2
            for j, (cx, cy) in enumerate(_other_chips(x, y)):
                got = bufs[t].at[2 * cx + cy, pl.ds(c * half, half), :]
                theirs = bufs[t].at[2 * cx + cy, pl.ds((1 - c) * half, half), :]
                sem = dict(send_sem=ss.at[3 * k + j], recv_sem=rs.at[3 * k + j],
                           device_id=(x, y, 1 - c), device_id_type=MESH)
                pairs.append((pltpu.make_async_remote_copy(src_ref=got, dst_ref=got, **sem),
                              pltpu.make_async_remote_copy(src_ref=got, dst_ref=theirs, **sem)))
        return pairs
    return fn


def _alloc(shapes, name):
    def body(*refs):
        pass

    return pl.pallas_call(body, name=name, out_specs=[ANY] * len(shapes),
                          out_shape=[jax.ShapeDtypeStruct(s, d) for s, d in shapes])()


def _send_to_sibling(n):
    def fn(bufs, ss, rs):
        x, y, c = _mesh_pos()
        pairs = []
        for t in range(n):
            src = bufs[t]
            if len(src.shape) == 3:
                half = src.shape[1] // 2
                src = src.at[:, pl.ds((1 - c) * half, half), :]
            cp = pltpu.make_async_remote_copy(src_ref=src, dst_ref=bufs[n + t], send_sem=ss.at[t],
                                              recv_sem=rs.at[t], device_id=(x, y, 1 - c), device_id_type=MESH)
            pairs.append((cp, cp))
        return pairs
    return fn


def _send_to_chips(n):
    def fn(bufs, ss, rs):
        x, y, c = _mesh_pos()
        pairs = []
        for j, (cx, cy) in enumerate(_other_chips(x, y)):
            for t in range(n):
                src = bufs[t].at[j] if len(bufs[t].shape) == 3 else bufs[t]
                cp = pltpu.make_async_remote_copy(src_ref=src, dst_ref=bufs[n + t].at[j], send_sem=ss.at[3 * t + j],
                                                  recv_sem=rs.at[3 * t + j], device_id=(cx, cy, c),
                                                  device_id_type=MESH)
                pairs.append((cp, cp))
        return pairs
    return fn


class _Exchange:
    def __init__(self, name, srcs, land_shapes, fn, n_sems):
        self.name, self.fn = name, fn
        lands = _alloc(land_shapes, name + "_alloc")
        self.n = len(srcs)
        self.bufs, sems, self.token = _split_call(name + "_start", list(srcs) + list(lands),
                                                  starts=[(n_sems, fn)])
        self.sems = sems[0]

    def finish(self, after=()):
        bufs, _, _ = _split_call(self.name + "_wait", self.bufs, waits=[(*self.sems, self.fn)], after=after)
        return bufs[:self.n], bufs[self.n:]


def _row_block(rows, cols, mult=8, limit=3 * 512 * 1024, itemsize=4):
    best = None
    for br in range(mult, rows + 1, mult):
        if rows % br == 0 and br * cols * itemsize <= limit:
            best = br
    assert best is not None, (rows, cols)
    return best


def _chip_partial(g, s, ids, name):
    _, half, cols = s.shape
    br = _row_block(half, cols, mult=16)
    nr = half // br

    def body(ids_ref, g_ref, s_ref, o_ref):
        o_ref[...] = (g_ref[...] + s_ref[...]).astype(BF16)

    return pl.pallas_call(
        body, name=name,
        grid_spec=pltpu.PrefetchScalarGridSpec(
            num_scalar_prefetch=1, grid=(3, nr),
            in_specs=[pl.BlockSpec((None, br, cols), lambda j, r, ids_ref: (ids_ref[2 + j], ids_ref[0] * nr + r, 0)),
                      pl.BlockSpec((None, br, cols), lambda j, r, ids_ref: (ids_ref[2 + j], r, 0))],
            out_specs=pl.BlockSpec((None, br, cols), lambda j, r, ids_ref: (j, r, 0))),
        out_shape=jax.ShapeDtypeStruct((3, half, cols), BF16),
        compiler_params=pltpu.CompilerParams(dimension_semantics=("arbitrary", "arbitrary")),
    )(ids, g, s)


def _chip_sum(g, s, q, ids, name):
    _, half, cols = s.shape
    br = _row_block(half, cols, mult=16)
    nr = half // br

    def body(ids_ref, g_ref, s_ref, q_ref, o_ref):
        own = g_ref[...] + s_ref[...]
        o_ref[...] = (own + q_ref[2].astype(F32)) + (q_ref[0].astype(F32) + q_ref[1].astype(F32))

    return pl.pallas_call(
        body, name=name,
        grid_spec=pltpu.PrefetchScalarGridSpec(
            num_scalar_prefetch=1, grid=(nr,),
            in_specs=[pl.BlockSpec((None, br, cols), lambda r, ids_ref: (ids_ref[1], ids_ref[0] * nr + r, 0)),
                      pl.BlockSpec((None, br, cols), lambda r, ids_ref: (ids_ref[1], r, 0)),
                      pl.BlockSpec((3, br, cols), lambda r, ids_ref: (0, r, 0))],
            out_specs=pl.BlockSpec((br, cols), lambda r, ids_ref: (r, 0))),
        out_shape=jax.ShapeDtypeStruct((half, cols), F32),
        compiler_params=pltpu.CompilerParams(dimension_semantics=("arbitrary",)),
    )(ids, g, s, q)


def _small_sum(part, recv):
    def body(p_ref, q_ref, o_ref):
        o_ref[...] = (p_ref[...] + q_ref[2]) + (q_ref[0] + q_ref[1])

    return pl.pallas_call(body, name="chip_sum_small", out_shape=jax.ShapeDtypeStruct(part.shape, F32))(part, recv)


def _adamw_math(w, g, m, v):
    mn = ADAM_B1 * m + (1.0 - ADAM_B1) * g
    vn = ADAM_B2 * v + (1.0 - ADAM_B2) * (g * g)
    m_hat = mn / (1.0 - ADAM_B1 ** ADAM_STEP)
    v_hat = vn / (1.0 - ADAM_B2 ** ADAM_STEP)
    return -ADAM_LR * (m_hat / (jnp.sqrt(v_hat) + ADAM_EPS) + ADAM_WD * w), mn, vn


def _adamw_halves(w, own, sib, m, v, ids, name, layer=0, n_layers=1, stacked=None):
    C = w.shape[1]
    R = w.shape[0] // n_layers
    half = R // 2
    br = _row_block(half, C)
    nh = half // br
    base = layer * 2 * nh

    def body(ids_ref, w_ref, own_ref, sib_ref, m_ref, v_ref, *rest):
        g_ref, d_ref, mo_ref, vo_ref = rest[-4:]
        is_own = (pl.program_id(0) // nh) == ids_ref[0]
        g = jnp.where(is_own, own_ref[...], sib_ref[...])
        g_ref[...] = g
        d_ref[...], mo_ref[...], vo_ref[...] = _adamw_math(w_ref[...], g, m_ref[...], v_ref[...])

    full = pl.BlockSpec((br, C), lambda r, ids_ref: (base + r, 0))
    own_spec = pl.BlockSpec((br, C), lambda r, ids_ref: (jnp.clip(r - ids_ref[0] * nh, 0, nh - 1), 0))
    sib_spec = pl.BlockSpec((br, C), lambda r, ids_ref: (jnp.clip(r - (1 - ids_ref[0]) * nh, 0, nh - 1), 0))
    in_specs = [full, own_spec, sib_spec, full, full]
    args = [ids, w, own, sib, m, v]
    aliases = {}
    if stacked is not None:
        in_specs += [ANY] * 4
        args += list(stacked)
        aliases = {6 + k: k for k in range(4)}
    return pl.pallas_call(
        body, name=name,
        grid_spec=pltpu.PrefetchScalarGridSpec(
            num_scalar_prefetch=1, grid=(2 * nh,), in_specs=in_specs, out_specs=[full] * 4),
        out_shape=[jax.ShapeDtypeStruct(w.shape, F32)] * 4, input_output_aliases=aliases,
        compiler_params=_params(),
    )(*args)


_PACK_UNIT = 1024


def _pack(arrs):
    flat = []
    for a in arrs:
        f = a.reshape(-1).astype(F32)
        pad = (-f.shape[0]) % _PACK_UNIT
        if pad:
            f = jnp.concatenate([f, jnp.zeros((pad,), F32)])
        flat.append(f)
    return jnp.concatenate(flat).reshape(-1, 128)


def kernel(x, p, norm_mix, norm_ffn, norm_ple, norm_kv, norm_final, a_w_in, a_norm_v, a_w_s, a_b_s, a_w_out, w_kv, b_w_q, b_sinks, b_w_o, f_w_up, f_conv_w, f_conv_b, f_w_down, ple_w_in, ple_w_gate, ple_b_gate, loss_target, m_norm_mix, m_norm_ffn, m_norm_ple, m_norm_kv, m_norm_final, m_a_w_in, m_a_norm_v, m_a_w_s, m_a_b_s, m_a_w_out, m_w_kv, m_b_w_q, m_b_sinks, m_b_w_o, m_f_w_up, m_f_conv_w, m_f_conv_b, m_f_w_down, m_ple_w_in, m_ple_w_gate, m_ple_b_gate, v_norm_mix, v_norm_ffn, v_norm_ple, v_norm_kv, v_norm_final, v_a_w_in, v_a_norm_v, v_a_w_s, v_a_b_s, v_a_w_out, v_w_kv, v_b_w_q, v_b_sinks, v_b_w_o, v_f_w_up, v_f_conv_w, v_f_conv_b, v_f_w_down, v_ple_w_in, v_ple_w_gate, v_ple_b_gate):
    given = dict(locals())

    small_shard = _pack([a_norm_v, f_conv_w])
    pad_rows = (-small_shard.shape[0]) % 16
    if pad_rows:
        small_shard = jnp.concatenate([small_shard, jnp.zeros((pad_rows, 128), F32)])
    groups = [
        [(a_w_in, (0,), BF16), (a_w_out, (0,), BF16), (small_shard, (), F32)],
        [(f_w_up, (0,), BF16), (f_w_down, (0,), BF16)],
        [(ple_w_in, (0,), BF16), (ple_w_gate, (0,), BF16), (w_kv, (), BF16), (b_w_q, (0,), BF16),
         (b_w_o, (0,), BF16), (f_w_up, (1,), BF16), (f_w_down, (1,), BF16), (ple_w_in, (1,), BF16),
         (ple_w_gate, (1,), BF16)],
    ]
    lands, spans, start = [], [], 0
    for gi, items in enumerate(groups):
        lands += _cast_place(items, f"cast_place_g{gi}")
        spans.append(list(range(start, start + len(items))))
        start += len(items)
    lands, ici_sems, _ = _split_call("gather_start", lands,
                                     starts=[(3 * len(sp), _gather_ici(sp)) for sp in spans])

    def finish_group(gi, after):
        sp = spans[gi]
        local = list(range(len(sp)))
        bufs = [lands[t] for t in sp]
        bufs, d2d_sems, _ = _split_call(f"gather_pass_g{gi}", bufs, waits=[(*ici_sems[gi], _gather_ici(local))],
                                        starts=[(3 * len(sp), _gather_d2d(local))], after=after)
        bufs, _, _ = _split_call(f"gather_done_g{gi}", bufs, waits=[(*d2d_sems[0], _gather_d2d(local))])
        return bufs

    def stage0():
        b_in, b_out, b_small = finish_group(0, ())
        small_full = b_small.reshape(N_SHARD, -1)
        gv_full = small_full[:, :256].reshape(1, D_MODEL)
        cw_full = small_full[:, _PACK_UNIT:_PACK_UNIT + 2 * 3 * FF_BLK].reshape(N_SHARD, 2, 3, FF_BLK)
        cw_full = jnp.transpose(cw_full, (1, 2, 0, 3)).reshape(2, 3, N_FF)
        return gv_full, cw_full, b_in, b_out.reshape(D_MODEL, D_MODEL)

    def stage1(after):
        b_up, b_dn = finish_group(1, after)
        return b_up, b_dn.reshape(D_FF, D_MODEL)

    def stage2(after):
        pin0, gate0, kv_w, wq, wo, up1, dn1, pin1, gate1 = finish_group(2, after)
        sq = lambda a: a.reshape(D_MODEL, -1)
        return dict(w_pin=[pin0, pin1], w_gate=[sq(gate0), sq(gate1)], w_kv=sq(kv_w), w_q=sq(wq), w_o=sq(wo),
                    w_up1=up1, w_dn1=dn1.reshape(D_FF, D_MODEL))

    dx, (loss, (out_g, out_d, out_m, out_v)) = _local_step(
        x[0], p[0, 0], p[1, 0], loss_target[0], norm_mix, norm_ffn, norm_ple, norm_kv, norm_final, a_w_s, a_b_s,
        b_sinks, f_conv_b, ple_b_gate, stage0, stage1, stage2, _Reducer(given))
    weight_names = ['norm_mix', 'norm_ffn', 'norm_ple', 'norm_kv', 'norm_final', 'a_w_in', 'a_norm_v', 'a_w_s',
                    'a_b_s', 'a_w_out', 'w_kv', 'b_w_q', 'b_sinks', 'b_w_o', 'f_w_up', 'f_conv_w', 'f_conv_b',
                    'f_w_down', 'ple_w_in', 'ple_w_gate', 'ple_b_gate']
    return (loss, dx.reshape(x.shape), *[out_g[k] for k in weight_names], *[out_d[k] for k in weight_names],
            *[out_m[k] for k in weight_names], *[out_v[k] for k in weight_names])


def _local_step(xs, p0, p1, tgt, norm_mix, norm_ffn, norm_ple, norm_kv, norm_final, a_w_s, a_b_s, b_sinks,
                f_conv_b, ple_b_gate, stage0, stage1, stage2, sched):
    tril = jnp.tril(jnp.ones((CHUNK, CHUNK), F32))
    wsm = (a_w_s[0] * tril[None]).astype(BF16)
    bsb = jnp.broadcast_to(a_b_s[0][:, :, None], (A_GROUPS, CHUNK, CHUNK))
    sinks = b_sinks[0]
    row = lambda a: a.reshape(1, -1)

    gv_full, cw_full, w_in, w_out = stage0()
    h1, zp = _mixer_a_fwd(xs, row(norm_mix[0]), gv_full, wsm, bsb, w_in, w_out)
    w_up0, w_dn0 = stage1((h1,))
    h2, hh0, c0 = _ffn_fwd(h1, row(norm_ffn[0]), cw_full[0], row(f_conv_b[0]), w_up0, w_dn0, 0)
    rest = stage2((h2,))
    w_pin, w_gate, w_kv_f, w_q, w_o = rest['w_pin'], rest['w_gate'], rest['w_kv'], rest['w_q'], rest['w_o']
    w_up = [w_up0, rest['w_up1']]
    w_dn = [w_dn0, rest['w_dn1']]
    h3, pe0, a0, kv = _ple_fwd_kv(h2, p0, row(norm_ple[0]), row(ple_b_gate[0]), row(norm_kv), w_pin[0], w_gate[0], w_kv_f)
    h4, q, ao, lse = _attn_fwd(h3, row(norm_mix[1]), kv, sinks, w_q, w_o)
    h5, hh1, c1 = _ffn_fwd(h4, row(norm_ffn[1]), cw_full[1], row(f_conv_b[1]), w_up[1], w_dn[1], 1)
    dh6, pe1, a1, loss_acc, dn_final = _ple_fwd_final(
        h5, p1, tgt, row(norm_ple[1]), row(ple_b_gate[1]), row(norm_final), w_pin[1], w_gate[1])

    def pieces(g):
        return g.reshape(N_SHARD, -1, g.shape[-1])

    dh5, g_pin1, g_gate1, dbg1, dnple1 = _ple_bwd(dh6, h5, pe1, a1, p1, row(norm_ple[1]), w_gate[1], 1)
    early = {('ple_w_in', 1): g_pin1, ('ple_w_gate', 1): pieces(g_gate1)}
    dh4, g_up1, g_dn1, dcw1, dcb1, dnffn1 = _ffn_bwd(
        dh5, h4, hh1, c1, row(norm_ffn[1]), cw_full[1], w_up[1], w_dn[1], 1)
    early['f_w_down', 1] = pieces(g_dn1)
    early['f_w_up', 1] = g_up1
    dh3a, g_wq, g_wo, dkv, dsink, dnmix1 = _attn_bwd(dh4, h3, q, kv, ao, lse, row(norm_mix[1]), sinks, w_q, w_o)
    early['b_w_o', 0] = pieces(g_wo)
    early['b_w_q', 0] = pieces(g_wq)
    dh2, g_pin0, g_gate0, dbg0, dnple0, g_wkv, dnkv = _ple_bwd(
        dh3a, h2, pe0, a0, p0, row(norm_ple[0]), w_gate[0], 0, kv_args=(h3, dkv, row(norm_kv), w_kv_f))
    early['w_kv', 0] = pieces(g_wkv)
    early['ple_w_in', 0] = g_pin0
    early['ple_w_gate', 0] = pieces(g_gate0)
    deps = sched.early_ready(early)
    dh1, g_up0, g_dn0, dcw0, dcb0, dnffn0 = _ffn_bwd(
        dh2, h1, hh0, c0, row(norm_ffn[0]), cw_full[0], w_up[0], w_dn[0], 0, deps=deps,
        between=lambda part: sched.after_ffn_half((part,)))
    deps = sched.ffn0_ready({('f_w_down', 0): pieces(g_dn0), ('f_w_up', 0): g_up0})
    dx, g_win, g_wout, dws, dbs, dgv, dnmix0 = _mixer_a_bwd(
        dh1, xs, zp, row(norm_mix[0]), gv_full, wsm, bsb, tril, w_in, w_out, deps=deps)
    g_wout = pieces(g_wout)

    small_grads = {
        'norm_mix': jnp.concatenate([dnmix0, dnmix1]), 'norm_ffn': jnp.concatenate([dnffn0, dnffn1]),
        'norm_ple': jnp.concatenate([dnple0, dnple1]), 'norm_kv': dnkv, 'norm_final': dn_final,
        'a_norm_v': dgv, 'a_w_s': dws.reshape(A_GROUPS * CHUNK, CHUNK), 'a_b_s': dbs[:, :, 0],
        'b_sinks': dsink[0:1, :], 'f_conv_w': jnp.concatenate([dcw0, dcw1]),
        'f_conv_b': jnp.concatenate([dcb0, dcb1]), 'ple_b_gate': jnp.concatenate([dbg0, dbg1]),
        'loss': loss_acc,
    }
    outs = sched.finish({('a_w_in', 0): g_win, ('a_w_out', 0): g_wout}, small_grads, (dx,))
    return dx, outs


class _Reducer:
    def __init__(self, given):
        self.given = given
        cx, cy, cc = _mesh_pos()
        self.shard = 2 * cx + cy
        s = self.shard
        self.ids = jnp.stack([cc, s, s ^ 2, s ^ 1, s ^ 3]).astype(jnp.int32)
        self.out = [{}, {}, {}, {}]
        self.stacked = {}

    def _send(self, tag, grads, small=()):
        keys = list(grads)
        srcs = [grads[k] for k in keys] + list(small)
        shapes = [((N_SHARD, g.shape[1] // 2, g.shape[2]), F32) for g in srcs[:len(keys)]]
        shapes += [(s.shape, F32) for s in small]
        return keys, _Exchange(f"send_{tag}", srcs, shapes, _send_to_sibling(len(srcs)), len(srcs))

    def _exchange(self, tag, keys, send, after):
        srcs, lands = send.finish(after)
        n = len(keys)
        parts = [_chip_partial(g, s, self.ids, f"chip_partial_{k[0]}{k[1]}")
                 for k, g, s in zip(keys, srcs[:n], lands[:n])]
        shapes = [(p.shape, BF16) for p in parts]
        if len(srcs) > n:
            small = _small_add(srcs[n:], lands[n:])
            parts += small
            shapes += [((3,) + s.shape, F32) for s in small]
        exch = _Exchange(f"exch_{tag}", parts, shapes, _send_to_chips(len(parts)), 3 * len(parts))
        return (keys, srcs[:n], lands[:n], exch)

    def _swap(self, tag, state, after):
        keys, grads, sib, exch = state
        parts, recv = exch.finish(after)
        n = len(keys)
        own = [_chip_sum(g, s, q, self.ids, f"chip_sum_{k[0]}{k[1]}") for k, g, s, q in zip(keys, grads, sib, recv[:n])]
        small_red = _small_sum(parts[n:], recv[n:]) if len(parts) > n else None
        return keys, _Exchange(f"swap_{tag}", own, [(o.shape, F32) for o in own], _send_to_sibling(n), n), small_red

    def _adamw(self, keys, swap, after):
        own, sib = swap.finish(after)
        last = None
        for (name, layer), o, s in zip(keys, own, sib):
            w = self.given[name]
            n_layers = w.shape[0] if w.ndim == 3 else 1
            c2 = w.shape[-1]
            res = _adamw_halves(w.reshape(-1, c2), o, s, self.given['m_' + name].reshape(-1, c2),
                                self.given['v_' + name].reshape(-1, c2), self.ids, f"adamw_{name}{layer}",
                                layer, n_layers, self.stacked.get(name))
            self.stacked[name] = res
            if layer == 0:
                for dst, r in zip(self.out, res):
                    dst[name] = r.reshape(w.shape)
            last = res[0]
        return last

    def early_ready(self, grads):
        self.e_keys, self.e_send = self._send("e", grads)
        return (self.e_send.token,)

    def after_ffn_half(self, after):
        self.e_state = self._exchange("e", self.e_keys, self.e_send, after)
        return (self.e_state[3].token,)

    def ffn0_ready(self, grads):
        _, self.e_swap, _ = self._swap("e", self.e_state, tuple(grads.values())[-1:])
        f_keys, f_send = self._send("f", grads)
        self.f_state = self._exchange("f", f_keys, f_send, ())
        return (self.f_state[3].token, self.e_swap.token)

    def finish(self, grads, small_grads, after):
        small_names = list(small_grads)
        a_keys, a_send = self._send("a", grads, [small_grads[k] for k in small_names])
        a_state = self._exchange("a", a_keys, a_send, after)
        e_done = self._adamw(self.e_keys, self.e_swap, (a_state[3].token,))
        f_keys, f_swap, _ = self._swap("f", self.f_state, (e_done,))
        f_done = self._adamw(f_keys, f_swap, ())
        _, a_swap, small_red = self._swap("a", a_state, (f_done,))
        self._adamw(a_keys, a_swap, ())

        given = self.given
        reduced = dict(zip(small_names, small_red))
        loss = reduced.pop('loss')[0, 0]
        names = list(reduced)
        items = []
        for k in names:
            g = reduced[k]
            cols = g.shape[1] // N_SHARD if k in ('a_norm_v', 'f_conv_w') else g.shape[1]
            view = lambda a: _lane_pad(a.reshape(g.shape[0], -1), cols)
            items.append((view(given[k]), g, view(given['m_' + k]), view(given['v_' + k])))
        res = _adamw_small(items, self.ids)
        for k, four in zip(names, res):
            width = given[k].size // four[0].shape[0]
            for dst, r in zip(self.out, four):
                dst[k] = r[:, :width].reshape(given[k].shape)
        return loss, self.out


def _lane_pad(a, cols):
    return a if a.shape[1] == cols else jnp.pad(a, ((0, 0), (0, cols - a.shape[1])))


def _small_add(a_list, b_list):
    n = len(a_list)

    def body(*refs):
        for t in range(n):
            refs[2 * n + t][...] = refs[t][...] + refs[n + t][...]

    return pl.pallas_call(body, name="chip_partial_small",
                          out_shape=[jax.ShapeDtypeStruct(a.shape, F32) for a in a_list])(*a_list, *b_list)


def _small_sum(parts, recvs):
    n = len(parts)

    def body(*refs):
        for t in range(n):
            q = refs[n + t]
            refs[2 * n + t][...] = (refs[t][...] + q[2]) + (q[0] + q[1])

    return pl.pallas_call(body, name="chip_sum_small",
                          out_shape=[jax.ShapeDtypeStruct(p.shape, F32) for p in parts])(*parts, *recvs)


def _adamw_small(items, ids):
    n = len(items)

    def body(ids_ref, *refs):
        for t in range(n):
            w_ref, g_ref, m_ref, v_ref = refs[4 * t:4 * t + 4]
            g_out, d_ref, mo_ref, vo_ref = refs[4 * n + 4 * t:4 * n + 4 * t + 4]
            g = g_ref[...]
            g_out[...] = g
            d_ref[...], mo_ref[...], vo_ref[...] = _adamw_math(w_ref[...], g, m_ref[...], v_ref[...])

    in_specs, out_specs, out_shape, args = [], [], [], []
    for w, g, m, v in items:
        full = pl.BlockSpec(w.shape, lambda i, ids_ref: (0, 0))
        g_spec = full if g.shape == w.shape else pl.BlockSpec(w.shape, lambda i, ids_ref: (0, ids_ref[1]))
        in_specs += [full, g_spec, full, full]
        out_specs += [full] * 4
        out_shape += [jax.ShapeDtypeStruct(w.shape, F32)] * 4
        args += [w, g, m, v]
    res = pl.pallas_call(
        body, name="adamw_small",
        grid_spec=pltpu.PrefetchScalarGridSpec(num_scalar_prefetch=1, grid=(1,), in_specs=in_specs,
                                               out_specs=out_specs),
        out_shape=out_shape, compiler_params=_params(),
    )(ids, *args)
    return [res[4 * t:4 * t + 4] for t in range(n)]
```

```python
import functools
import math

import numpy as np
import jax
import jax.numpy as jnp
from jax import lax
from jax.experimental import pallas as pl
from jax.experimental.pallas import tpu as pltpu

F32 = jnp.float32
BF16 = jnp.bfloat16

D_MODEL = 1024
CHUNK = 128
A_GROUPS = 8
HEAD_DIM = 64
N_Q_HEADS = 16
N_KV_HEADS = 4
GQA_GROUP = N_Q_HEADS // N_KV_HEADS
KV_DIM = N_KV_HEADS * HEAD_DIM
BLOCK = 128
D_FF = 2816
N_FF = 2 * D_FF
FF_BLK = N_FF // 4
PLE_DIM = 256
EPS = 1e-6
NEG = -1e30
N_SHARD = 4

ADAM_LR = 0.001
ADAM_B1 = 0.9
ADAM_B2 = 0.999
ADAM_EPS = 1e-08
ADAM_WD = 0.01
ADAM_STEP = 10

VMEM_LIMIT = 60 * 1024 * 1024
MESH = pl.DeviceIdType.MESH
ANY = pl.BlockSpec(memory_space=pl.ANY)
SMEM = pl.BlockSpec(memory_space=pltpu.SMEM)

_SLOPES = [float(np.float32(2.0 ** (-8.0 * (h + 1) / N_Q_HEADS))) for h in range(N_Q_HEADS)]


def _dot(a, b):
    return jnp.dot(a, b, preferred_element_type=F32)


def _dot_nt(a, b):
    return lax.dot_general(a, b, (((1,), (1,)), ((), ())), preferred_element_type=F32)


def _dot_tn(a, b):
    return lax.dot_general(a, b, (((0,), (0,)), ((), ())), preferred_element_type=F32)


def _rms(x, g):
    r = lax.rsqrt(jnp.mean(x * x, axis=-1, keepdims=True) + EPS)
    xh = x * r
    return xh * g, xh, r


def _rms_bwd(dy, xh, r, g):
    dxh = dy * g
    dg = jnp.sum(dy * xh, axis=0, keepdims=True)
    dx = r * (dxh - xh * jnp.mean(dxh * xh, axis=-1, keepdims=True))
    return dx, dg


_GELU_C = math.sqrt(2.0 / math.pi)


def _gelu(x):
    t = jnp.tanh(_GELU_C * (x + 0.044715 * (x * x * x)))
    return 0.5 * x * (1.0 + t)


def _gelu_grad(x):
    x2 = x * x
    t = jnp.tanh(_GELU_C * (x + 0.044715 * (x2 * x)))
    return 0.5 * (1.0 + t) + 0.5 * x * (1.0 - t * t) * (_GELU_C * (1.0 + 3.0 * 0.044715 * x2))


def _sigmoid(x):
    return 0.5 * jnp.tanh(0.5 * x) + 0.5


def _load_once(pairs, sem):
    @pl.when(pl.program_id(0) == 0)
    def _():
        cps = [pltpu.make_async_copy(s, d, sem.at[i]) for i, (s, d) in enumerate(pairs)]
        for cp in cps:
            cp.start()
        for cp in cps:
            cp.wait()


def _params(n_axes=1, vmem=VMEM_LIMIT):
    return pltpu.CompilerParams(dimension_semantics=("arbitrary",) * n_axes, vmem_limit_bytes=vmem)


def _row_spec(tm, n, rev_nt=None):
    if rev_nt is None:
        return pl.BlockSpec((tm, n), lambda i: (i, 0))
    return pl.BlockSpec((tm, n), lambda i: (rev_nt - 1 - i, 0))


def _const_spec(shape):
    nd = len(shape)
    return pl.BlockSpec(shape, lambda i: (0,) * nd)


def _add_deps(body, in_specs, args, deps):
    nd = len(deps)
    if nd == 0:
        return body, list(in_specs), list(args)

    def wrapped(*refs):
        return body(*refs[nd:])

    return wrapped, [ANY] * nd + list(in_specs), list(deps) + list(args)


def _zero_first(refs):
    @pl.when(pl.program_id(0) == 0)
    def _():
        for r in refs:
            r[...] = jnp.zeros(r.shape, r.dtype)


def _mixer_a_fwd(x, nmix, gv, wsm, bsb, w_in, w_out):
    T = x.shape[0]
    tm = min(512, T)
    nt = T // tm
    nw = 2 * D_MODEL // N_SHARD

    def body(x_ref, nmix_ref, gv_ref, ws_ref, bsb_ref, w_in_hbm, w_out_hbm,
             h1_ref, zp_ref, w_in_v, w_out_v, gated_v, sem):
        _load_once([(w_in_hbm, w_in_v), (w_out_hbm, w_out_v)], sem)
        xv = x_ref[...]
        xn = _rms(xv, nmix_ref[...])[0].astype(BF16)
        for j in range(N_SHARD):
            zp_ref[:, j * nw:(j + 1) * nw] = _dot(xn, w_in_v[j])
        z = _gelu(zp_ref[...])
        u = z[:, :D_MODEL]
        vn = _rms(z[:, D_MODEL:], gv_ref[...])[0].astype(BF16)
        for c in range(tm // CHUNK):
            rows = slice(c * CHUNK, (c + 1) * CHUNK)
            for h in range(A_GROUPS):
                cols = slice(h * CHUNK, (h + 1) * CHUNK)
                s = _dot(ws_ref[h], vn[rows, cols]) + bsb_ref[h]
                gated_v[rows, cols] = (u[rows, cols] * s).astype(BF16)
        h1_ref[...] = xv + _dot(gated_v[...], w_out_v[...])

    return pl.pallas_call(
        body, name="mixer_a_fwd", grid=(nt,),
        in_specs=[_row_spec(tm, D_MODEL), _const_spec((1, D_MODEL)), _const_spec((1, D_MODEL)),
                  _const_spec((A_GROUPS, CHUNK, CHUNK)), _const_spec((A_GROUPS, CHUNK, CHUNK)), ANY, ANY],
        out_specs=[_row_spec(tm, D_MODEL), _row_spec(tm, 2 * D_MODEL)],
        out_shape=[jax.ShapeDtypeStruct((T, D_MODEL), F32), jax.ShapeDtypeStruct((T, 2 * D_MODEL), F32)],
        scratch_shapes=[pltpu.VMEM((N_SHARD, D_MODEL, nw), BF16), pltpu.VMEM((D_MODEL, D_MODEL), BF16),
                        pltpu.VMEM((tm, D_MODEL), BF16), pltpu.SemaphoreType.DMA((2,))],
        compiler_params=_params(),
    )(x, nmix, gv, wsm, bsb, w_in, w_out)


def _mixer_a_bwd(dh, x, zp, nmix, gv, wsm, bsb, tril, w_in, w_out, deps=()):
    T = x.shape[0]
    tm = min(256, T)
    nt = T // tm
    nw = 2 * D_MODEL // N_SHARD

    def body(dh_ref, x_ref, zp_ref, nmix_ref, gv_ref, ws_ref, bsb_ref, tril_ref, w_in_hbm, w_out_hbm,
             dx_ref, dwin_ref, dwout_ref, dws_ref, dbs_ref, dgv_ref, dnmix_ref,
             w_in_v, w_out_v, du_v, dvn_v, dbs_v, gated_ref, sem):
        _load_once([(w_in_hbm, w_in_v), (w_out_hbm, w_out_v)], sem)
        _zero_first([dws_ref, dbs_v, dgv_ref, dnmix_ref, dwin_ref, dwout_ref])
        i = pl.program_id(0)
        dhv = dh_ref[...]
        dhb = dhv.astype(BF16)
        xv = x_ref[...]
        xn, xh, r = _rms(xv, nmix_ref[...])
        xnb = xn.astype(BF16)
        zpv = zp_ref[...]
        z = _gelu(zpv)
        u = z[:, :D_MODEL]
        vn_f, vh, rv = _rms(z[:, D_MODEL:], gv_ref[...])
        vn = vn_f.astype(BF16)
        dgated = _dot_nt(dhb, w_out_v[...])
        for c in range(tm // CHUNK):
            rows = slice(c * CHUNK, (c + 1) * CHUNK)
            for h in range(A_GROUPS):
                cols = slice(h * CHUNK, (h + 1) * CHUNK)
                vn_h = vn[rows, cols]
                s = _dot(ws_ref[h], vn_h) + bsb_ref[h]
                dgt = dgated[rows, cols]
                u_h = u[rows, cols]
                gated_ref[rows, cols] = (u_h * s).astype(BF16)
                du_v[rows, cols] = dgt * s
                ds = dgt * u_h
                dsb = ds.astype(BF16)
                dws_ref[h] += _dot_nt(dsb, vn_h)
                dbs_v[h] += ds
                dvn_v[rows, cols] = _dot_tn(ws_ref[h], dsb)
        dwout_ref[...] += _dot_tn(gated_ref[...], dhb)
        dv, dgv = _rms_bwd(dvn_v[...], vh, rv, gv_ref[...])
        dgv_ref[...] += dgv
        dzu = (du_v[...] * _gelu_grad(zpv[:, :D_MODEL])).astype(BF16)
        dzv = (dv * _gelu_grad(zpv[:, D_MODEL:])).astype(BF16)
        dzs = (dzu[:, :nw], dzu[:, nw:], dzv[:, :nw], dzv[:, nw:])
        dxn = jnp.zeros((tm, D_MODEL), F32)
        for j in range(N_SHARD):
            dxn = dxn + _dot_nt(dzs[j], w_in_v[j])
            dwin_ref[j] += _dot_tn(xnb, dzs[j])
        dxx, dn = _rms_bwd(dxn, xh, r, nmix_ref[...])
        dnmix_ref[...] += dn
        dx_ref[...] = dhv + dxx

        @pl.when(i == nt - 1)
        def _():
            for h in range(A_GROUPS):
                dws_ref[h] = dws_ref[h] * tril_ref[...]
                dbs_ref[h] = jnp.broadcast_to(jnp.sum(dbs_v[h], axis=1, keepdims=True), (CHUNK, CHUNK))

    grp = (A_GROUPS, CHUNK, CHUNK)
    body, in_specs, args = _add_deps(
        body, [_row_spec(tm, D_MODEL), _row_spec(tm, D_MODEL), _row_spec(tm, 2 * D_MODEL),
               _const_spec((1, D_MODEL)), _const_spec((1, D_MODEL)), _const_spec(grp), _const_spec(grp),
               _const_spec((CHUNK, CHUNK)), ANY, ANY],
        [dh, x, zp, nmix, gv, wsm, bsb, tril, w_in, w_out], deps)
    return pl.pallas_call(
        body, name="mixer_a_bwd", grid=(nt,), in_specs=in_specs,
        out_specs=[_row_spec(tm, D_MODEL), _const_spec((N_SHARD, D_MODEL, nw)), _const_spec((D_MODEL, D_MODEL)),
                   _const_spec(grp), _const_spec(grp), _const_spec((1, D_MODEL)), _const_spec((1, D_MODEL))],
        out_shape=[jax.ShapeDtypeStruct((T, D_MODEL), F32), jax.ShapeDtypeStruct((N_SHARD, D_MODEL, nw), F32),
                   jax.ShapeDtypeStruct((D_MODEL, D_MODEL), F32),
                   jax.ShapeDtypeStruct(grp, F32), jax.ShapeDtypeStruct(grp, F32),
                   jax.ShapeDtypeStruct((1, D_MODEL), F32), jax.ShapeDtypeStruct((1, D_MODEL), F32)],
        scratch_shapes=[pltpu.VMEM((N_SHARD, D_MODEL, nw), BF16), pltpu.VMEM((D_MODEL, D_MODEL), BF16),
                        pltpu.VMEM((tm, D_MODEL), F32), pltpu.VMEM((tm, D_MODEL), F32),
                        pltpu.VMEM(grp, F32), pltpu.VMEM((tm, D_MODEL), BF16), pltpu.SemaphoreType.DMA((2,))],
        compiler_params=_params(),
    )(*args)


def _load_ffn_weights(w_up_hbm, w_dn_hbm, layer, w_up_v, w_dn_v, sem):
    _load_once([(w_up_hbm, w_up_v), (w_dn_hbm, w_dn_v)], sem)


def _ffn_fwd(h, nffn, cw, cb, w_up, w_dn, layer):
    T = h.shape[0]
    tm = min(256, T)
    nt = T // tm

    def body(h_ref, n_ref, cw_ref, cb_ref, w_up_hbm, w_dn_hbm, out_ref, hh_ref, c_ref,
             w_up_v, w_dn_v, carry_v, sem):
        _load_ffn_weights(w_up_hbm, w_dn_hbm, layer, w_up_v, w_dn_v, sem)
        _zero_first([carry_v])
        xv = h_ref[...]
        xf = _rms(xv, n_ref[...])[0].astype(BF16)
        acc = xv
        for j in range(2):
            cs = []
            for blk in (j, j + 2):
                cols = slice(blk * FF_BLK, (blk + 1) * FF_BLK)
                hh = _dot(xf, w_up_v[blk])
                hh_ref[:, cols] = hh.astype(BF16)
                ext = jnp.concatenate([carry_v[blk], hh], axis=0)
                carry_v[blk] = hh[tm - 8:, :]
                s1 = pltpu.roll(ext, 1, 0)[8:]
                s2 = pltpu.roll(ext, 2, 0)[8:]
                cv = (cb_ref[:, cols] + cw_ref[0:1, cols] * s2 + cw_ref[1:2, cols] * s1
                      + cw_ref[2:3, cols] * hh)
                c_ref[:, cols] = cv.astype(BF16)
                cs.append(cv)
            act = (cs[0] * _sigmoid(cs[0]) * cs[1]).astype(BF16)
            acc = acc + _dot(act, w_dn_v[j * FF_BLK:(j + 1) * FF_BLK, :])
        out_ref[...] = acc

    return pl.pallas_call(
        body, name=f"ffn_fwd{layer}", grid=(nt,),
        in_specs=[_row_spec(tm, D_MODEL), _const_spec((1, D_MODEL)), _const_spec((3, N_FF)),
                  _const_spec((1, N_FF)), ANY, ANY],
        out_specs=[_row_spec(tm, D_MODEL), _row_spec(tm, N_FF), _row_spec(tm, N_FF)],
        out_shape=[jax.ShapeDtypeStruct((T, D_MODEL), F32), jax.ShapeDtypeStruct((T, N_FF), BF16),
                   jax.ShapeDtypeStruct((T, N_FF), BF16)],
        scratch_shapes=[pltpu.VMEM((N_SHARD, D_MODEL, FF_BLK), BF16), pltpu.VMEM((D_FF, D_MODEL), BF16),
                        pltpu.VMEM((N_SHARD, 8, FF_BLK), F32), pltpu.SemaphoreType.DMA((2 * N_SHARD,))],
        compiler_params=_params(),
    )(h, nffn, cw, cb, w_up, w_dn)


def _wgrad(a, b, bn, col_sharded, name, deps=()):
    T, K = a.shape
    N = b.shape[1]
    tt = min(2048, T)
    nn, ntt = N // bn, T // tt
    kr = K // N_SHARD

    def body(a_ref, b_ref, o_ref):
        @pl.when(pl.program_id(1) == 0)
        def _():
            o_ref[...] = jnp.zeros(o_ref.shape, F32)
        d = _dot_tn(a_ref[...].astype(BF16), b_ref[...].astype(BF16))
        if col_sharded:
            o_ref[...] += d
        else:
            for j in range(N_SHARD):
                o_ref[j] += d[j * kr:(j + 1) * kr]

    if col_sharded:
        assert nn == N_SHARD
        out_spec = pl.BlockSpec((None, K, bn), lambda n, t: (n, 0, 0))
        out_shape = jax.ShapeDtypeStruct((N_SHARD, K, bn), F32)
    else:
        out_spec = pl.BlockSpec((N_SHARD, kr, bn), lambda n, t: (0, 0, n))
        out_shape = jax.ShapeDtypeStruct((N_SHARD, kr, N), F32)
    body, in_specs, args = _add_deps(
        body, [pl.BlockSpec((tt, K), lambda n, t: (t, 0)), pl.BlockSpec((tt, bn), lambda n, t: (t, n))],
        [a, b], deps)
    return pl.pallas_call(
        body, name=name, grid=(nn, ntt), in_specs=in_specs, out_specs=out_spec, out_shape=out_shape,
        compiler_params=pltpu.CompilerParams(dimension_semantics=("arbitrary",) * 2, vmem_limit_bytes=VMEM_LIMIT),
    )(*args)


def _ffn_bwd(dh, h, hh, c, nffn, cw, w_up, w_dn, layer, deps=(), between=None):
    T = h.shape[0]
    tm = min(256, T)
    nt = T // tm

    def body(dh_ref, h_ref, hh_ref, c_ref, n_ref, cw_ref, w_up_hbm, w_dn_hbm,
             dhin_ref, act_ref, dhh_ref, xf_ref, dcw_ref, dcb_ref, dn_ref,
             w_up_v, w_dn_v, carry_v, sem):
        _load_ffn_weights(w_up_hbm, w_dn_hbm, layer, w_up_v, w_dn_v, sem)
        _zero_first([carry_v, dcw_ref, dcb_ref, dn_ref])
        dout = dh_ref[...]
        doutb = dout.astype(BF16)
        xf_f, xh, r = _rms(h_ref[...], n_ref[...])
        xf_ref[...] = xf_f.astype(BF16)
        dxf = jnp.zeros((tm, D_MODEL), F32)
        for j in range(2):
            blks = (j, j + 2)
            cg = c_ref[:, j * FF_BLK:(j + 1) * FF_BLK].astype(F32)
            cu = c_ref[:, (j + 2) * FF_BLK:(j + 3) * FF_BLK].astype(F32)
            sg = _sigmoid(cg)
            sil = cg * sg
            act_ref[:, j * FF_BLK:(j + 1) * FF_BLK] = (sil * cu).astype(BF16)
            dact = _dot_nt(doutb, w_dn_v[j * FF_BLK:(j + 1) * FF_BLK, :])
            dcs = (dact * cu * (sg * (1.0 + cg * (1.0 - sg))), dact * sil)
            for blk, dc in zip(blks, dcs):
                cols = slice(blk * FF_BLK, (blk + 1) * FF_BLK)
                hhv = hh_ref[:, cols].astype(F32)
                ext = jnp.concatenate([dc, carry_v[blk]], axis=0)
                carry_v[blk] = dc[:8, :]
                n = tm + 8
                a1 = pltpu.roll(ext, n - 1, 0)[:tm]
                a2 = pltpu.roll(ext, n - 2, 0)[:tm]
                dcb_ref[:, cols] += jnp.sum(dc, axis=0, keepdims=True)
                dcw_ref[0:1, cols] += jnp.sum(a2 * hhv, axis=0, keepdims=True)
                dcw_ref[1:2, cols] += jnp.sum(a1 * hhv, axis=0, keepdims=True)
                dcw_ref[2:3, cols] += jnp.sum(dc * hhv, axis=0, keepdims=True)
                dhh = (cw_ref[2:3, cols] * dc + cw_ref[1:2, cols] * a1 + cw_ref[0:1, cols] * a2).astype(BF16)
                dhh_ref[:, cols] = dhh
                dxf = dxf + _dot_nt(dhh, w_up_v[blk])
        dxx, dn = _rms_bwd(dxf, xh, r, n_ref[...])
        dn_ref[...] += dn
        dhin_ref[...] = dout + dxx

    rev = functools.partial(_row_spec, rev_nt=nt)
    body, in_specs, args = _add_deps(
        body, [rev(tm, D_MODEL), rev(tm, D_MODEL), rev(tm, N_FF), rev(tm, N_FF),
               _const_spec((1, D_MODEL)), _const_spec((3, N_FF)), ANY, ANY],
        [dh, h, hh, c, nffn, cw, w_up, w_dn], deps)
    dhin, act, dhh, xf, dcw, dcb, dn = pl.pallas_call(
        body, name=f"ffn_bwd{layer}", grid=(nt,), in_specs=in_specs,
        out_specs=[rev(tm, D_MODEL), rev(tm, D_FF), rev(tm, N_FF), rev(tm, D_MODEL),
                   _const_spec((3, N_FF)), _const_spec((1, N_FF)), _const_spec((1, D_MODEL))],
        out_shape=[jax.ShapeDtypeStruct((T, D_MODEL), F32), jax.ShapeDtypeStruct((T, D_FF), BF16),
                   jax.ShapeDtypeStruct((T, N_FF), BF16), jax.ShapeDtypeStruct((T, D_MODEL), BF16),
                   jax.ShapeDtypeStruct((3, N_FF), F32), jax.ShapeDtypeStruct((1, N_FF), F32),
                   jax.ShapeDtypeStruct((1, D_MODEL), F32)],
        scratch_shapes=[pltpu.VMEM((N_SHARD, D_MODEL, FF_BLK), BF16), pltpu.VMEM((D_FF, D_MODEL), BF16),
                        pltpu.VMEM((N_SHARD, 8, FF_BLK), F32), pltpu.SemaphoreType.DMA((2 * N_SHARD,))],
        compiler_params=_params(),
    )(*args)
    deps2 = between(dhin) if between is not None else ()
    dwdn = _wgrad(act, dh, D_MODEL // 2, False, f"wgrad_ffn_down{layer}", deps=deps2)
    dwup = _wgrad(xf, dhh, FF_BLK, True, f"wgrad_ffn_up{layer}", deps=deps2)
    return dhin, dwup, dwdn, dcw, dcb, dn


def _load_ple_weights(w_pin_hbm, w_gate_hbm, layer, w_pin_v, w_gate_v, sem, extra=()):
    _load_once([(w_pin_hbm, w_pin_v), (w_gate_hbm, w_gate_v)] + list(extra), sem)


def _ple_fwd_kv(h, p, nple, bg, nkv, w_pin, w_gate, w_kv):
    T = h.shape[0]
    tm = min(512, T)
    nt = T // tm
    pw = D_MODEL // N_SHARD

    def body(h_ref, p_ref, n_ref, bg_ref, nkv_ref, w_pin_hbm, w_gate_hbm, w_kv_hbm,
             out_ref, pe_ref, a_ref, kv_ref, w_pin_v, w_gate_v, w_kv_v, sem):
        _load_ple_weights(w_pin_hbm, w_gate_hbm, 0, w_pin_v, w_gate_v, sem, [(w_kv_hbm, w_kv_v)])
        xv = h_ref[...]
        xg = _rms(xv, n_ref[...])[0].astype(BF16)
        a = _dot(xg, w_gate_v[...]) + bg_ref[...]
        a_ref[...] = a
        pb = p_ref[...].astype(BF16)
        for j in range(N_SHARD):
            pe_ref[:, j * pw:(j + 1) * pw] = _dot(pb, w_pin_v[j])
        hn = xv + pe_ref[...] * _sigmoid(a)
        out_ref[...] = hn
        kvn = _rms(hn, nkv_ref[...])[0].astype(BF16)
        kv_ref[...] = _dot(kvn, w_kv_v[...]).astype(BF16)

    vec = _const_spec((1, D_MODEL))
    return pl.pallas_call(
        body, name="ple_fwd0", grid=(nt,),
        in_specs=[_row_spec(tm, D_MODEL), _row_spec(tm, PLE_DIM), vec, vec, vec, ANY, ANY, ANY],
        out_specs=[_row_spec(tm, D_MODEL), _row_spec(tm, D_MODEL), _row_spec(tm, D_MODEL),
                   _row_spec(tm, 2 * KV_DIM)],
        out_shape=[jax.ShapeDtypeStruct((T, D_MODEL), F32), jax.ShapeDtypeStruct((T, D_MODEL), F32),
                   jax.ShapeDtypeStruct((T, D_MODEL), F32), jax.ShapeDtypeStruct((T, 2 * KV_DIM), BF16)],
        scratch_shapes=[pltpu.VMEM((N_SHARD, PLE_DIM, pw), BF16), pltpu.VMEM((D_MODEL, D_MODEL), BF16),
                        pltpu.VMEM((D_MODEL, 2 * KV_DIM), BF16), pltpu.SemaphoreType.DMA((2 * N_SHARD + 1,))],
        compiler_params=_params(),
    )(h, p, nple, bg, nkv, w_pin, w_gate, w_kv)


def _ple_fwd_final(h, p, tgt, nple, bg, nfin, w_pin, w_gate):
    T = h.shape[0]
    tm = min(512, T)
    nt = T // tm
    pw = D_MODEL // N_SHARD

    def body(h_ref, p_ref, t_ref, n_ref, bg_ref, nf_ref, w_pin_hbm, w_gate_hbm,
             dh_ref, pe_ref, a_ref, loss_ref, dnf_ref, w_pin_v, w_gate_v, sem):
        _load_ple_weights(w_pin_hbm, w_gate_hbm, 1, w_pin_v, w_gate_v, sem)
        _zero_first([loss_ref, dnf_ref])
        xv = h_ref[...]
        xg = _rms(xv, n_ref[...])[0].astype(BF16)
        a = _dot(xg, w_gate_v[...]) + bg_ref[...]
        a_ref[...] = a
        pb = p_ref[...].astype(BF16)
        for j in range(N_SHARD):
            pe_ref[:, j * pw:(j + 1) * pw] = _dot(pb, w_pin_v[j])
        hn = xv + pe_ref[...] * _sigmoid(a)
        y, yh, r = _rms(hn, nf_ref[...])
        diff = y - t_ref[...]
        loss_ref[...] += 0.5 * jnp.sum(jnp.mean(diff * diff, axis=-1, keepdims=True))
        dy = diff * (1.0 / D_MODEL)
        dhn, dnf = _rms_bwd(dy, yh, r, nf_ref[...])
        dnf_ref[...] += dnf
        dh_ref[...] = dhn

    vec = _const_spec((1, D_MODEL))
    return pl.pallas_call(
        body, name="ple_fwd1", grid=(nt,),
        in_specs=[_row_spec(tm, D_MODEL), _row_spec(tm, PLE_DIM), _row_spec(tm, D_MODEL), vec, vec, vec, ANY, ANY],
        out_specs=[_row_spec(tm, D_MODEL), _row_spec(tm, D_MODEL), _row_spec(tm, D_MODEL),
                   _const_spec((8, 128)), vec],
        out_shape=[jax.ShapeDtypeStruct((T, D_MODEL), F32), jax.ShapeDtypeStruct((T, D_MODEL), F32),
                   jax.ShapeDtypeStruct((T, D_MODEL), F32), jax.ShapeDtypeStruct((8, 128), F32),
                   jax.ShapeDtypeStruct((1, D_MODEL), F32)],
        scratch_shapes=[pltpu.VMEM((N_SHARD, PLE_DIM, pw), BF16), pltpu.VMEM((D_MODEL, D_MODEL), BF16),
                        pltpu.SemaphoreType.DMA((2 * N_SHARD,))],
        compiler_params=_params(),
    )(h, p, tgt, nple, bg, nfin, w_pin, w_gate)


def _ple_bwd(dh, hb, pe, a, p, nple, w_gate, layer, kv_args=None):
    T = hb.shape[0]
    tm = min(512, T)
    nt = T // tm
    with_kv = kv_args is not None
    pw = D_MODEL // N_SHARD

    def body(*refs):
        if with_kv:
            (dh_ref, hb_ref, pe_ref, a_ref, p_ref, n_ref, w_gate_hbm, hc_ref, dkv_ref, nkv_ref, w_kv_hbm,
             dhb_ref, dwpin_ref, dwgate_ref, dbg_ref, dn_ref, dwkv_ref, dnkv_ref,
             w_gate_v, w_kv_v, sem) = refs
        else:
            (dh_ref, hb_ref, pe_ref, a_ref, p_ref, n_ref, w_gate_hbm,
             dhb_ref, dwpin_ref, dwgate_ref, dbg_ref, dn_ref, w_gate_v, sem) = refs
        pairs = [(w_gate_hbm, w_gate_v)]
        if with_kv:
            pairs.append((w_kv_hbm, w_kv_v))
        _load_once(pairs, sem)
        _zero_first([dwpin_ref, dwgate_ref, dbg_ref, dn_ref] + ([dwkv_ref, dnkv_ref] if with_kv else []))
        do = dh_ref[...]
        if with_kv:
            dkvb = dkv_ref[...].astype(BF16)
            dkvn = _dot_nt(dkvb, w_kv_v[...])
            kvn, kh, kr = _rms(hc_ref[...], nkv_ref[...])
            dwkv_ref[...] += _dot_tn(kvn.astype(BF16), dkvb)
            dk, dnkv = _rms_bwd(dkvn, kh, kr, nkv_ref[...])
            dnkv_ref[...] += dnkv
            do = do + dk
        gate = _sigmoid(a_ref[...])
        dpe = (do * gate).astype(BF16)
        pb = p_ref[...].astype(BF16)
        for j in range(N_SHARD):
            dwpin_ref[j] += _dot_tn(pb, dpe[:, j * pw:(j + 1) * pw])
        da = do * pe_ref[...] * (gate * (1.0 - gate))
        dab = da.astype(BF16)
        dbg_ref[...] += jnp.sum(da, axis=0, keepdims=True)
        dxg = _dot_nt(dab, w_gate_v[...])
        xg, xh, r = _rms(hb_ref[...], n_ref[...])
        dwgate_ref[...] += _dot_tn(xg.astype(BF16), dab)
        dxx, dn = _rms_bwd(dxg, xh, r, n_ref[...])
        dn_ref[...] += dn
        dhb_ref[...] = do + dxx

    vec = _const_spec((1, D_MODEL))
    row = _row_spec(tm, D_MODEL)
    in_specs = [row, row, row, row, _row_spec(tm, PLE_DIM), vec, ANY]
    args = [dh, hb, pe, a, p, nple, w_gate]
    out_specs = [row, _const_spec((N_SHARD, PLE_DIM, pw)), _const_spec((D_MODEL, D_MODEL)), vec, vec]
    out_shape = [jax.ShapeDtypeStruct((T, D_MODEL), F32), jax.ShapeDtypeStruct((N_SHARD, PLE_DIM, pw), F32),
                 jax.ShapeDtypeStruct((D_MODEL, D_MODEL), F32),
                 jax.ShapeDtypeStruct((1, D_MODEL), F32), jax.ShapeDtypeStruct((1, D_MODEL), F32)]
    scratch = [pltpu.VMEM((D_MODEL, D_MODEL), BF16)]
    if with_kv:
        hc, dkv, nkv, w_kv = kv_args
        in_specs += [row, _row_spec(tm, 2 * KV_DIM), vec, ANY]
        args += [hc, dkv, nkv, w_kv]
        out_specs += [_const_spec((D_MODEL, 2 * KV_DIM)), vec]
        out_shape += [jax.ShapeDtypeStruct((D_MODEL, 2 * KV_DIM), F32), jax.ShapeDtypeStruct((1, D_MODEL), F32)]
        scratch.append(pltpu.VMEM((D_MODEL, 2 * KV_DIM), BF16))
    scratch.append(pltpu.SemaphoreType.DMA((N_SHARD + 1,)))
    return pl.pallas_call(
        body, name=f"ple_bwd{layer}", grid=(nt,), in_specs=in_specs, out_specs=out_specs,
        out_shape=out_shape, scratch_shapes=scratch, compiler_params=_params(),
    )(*args)


GROUP_ROWS = GQA_GROUP * BLOCK


def _band_bias():
    ii = lax.broadcasted_iota(jnp.int32, (GROUP_ROWS, 2 * BLOCK), 0)
    jj = lax.broadcasted_iota(jnp.int32, (GROUP_ROWS, 2 * BLOCK), 1)
    dist = (ii & (BLOCK - 1)) + BLOCK - jj
    inband = (dist >= 0) & (dist < BLOCK)
    distf = dist.astype(F32)
    grp = ii >> 7
    bias = []
    for kh in range(N_KV_HEADS):
        sl = _SLOPES[kh * GQA_GROUP:(kh + 1) * GQA_GROUP]
        slope = jnp.where(grp == 0, sl[0], jnp.where(grp == 1, sl[1], jnp.where(grp == 2, sl[2], sl[3])))
        bias.append(jnp.where(inband, slope * distf, -NEG))
    return bias, jj


def _stack_heads(x, kh):
    return jnp.concatenate([x[:, (kh * GQA_GROUP + g) * HEAD_DIM:(kh * GQA_GROUP + g + 1) * HEAD_DIM]
                            for g in range(GQA_GROUP)], axis=0)


def _sink_column(sink_ref, kh):
    grp = lax.broadcasted_iota(jnp.int32, (GROUP_ROWS, 1), 0) >> 7
    s = [sink_ref[kh * GQA_GROUP + g] for g in range(GQA_GROUP)]
    return jnp.where(grp == 0, s[0], jnp.where(grp == 1, s[1], jnp.where(grp == 2, s[2], s[3])))


def _attn_fwd(h, nmix, kv, sinks, w_q, w_o):
    T = h.shape[0]
    tm = min(512, T)
    nt = T // tm
    nb = tm // BLOCK

    def body(h_ref, n_ref, kv_ref, kvp_ref, sink_ref, w_q_hbm, w_o_hbm,
             out_ref, q_ref, ao_ref, lse_ref, w_q_v, w_o_v, kvs_v, sem):
        _load_once([(w_q_hbm, w_q_v), (w_o_hbm, w_o_v)], sem)
        ti = pl.program_id(0)
        xv = h_ref[...]
        xn = _rms(xv, n_ref[...])[0].astype(BF16)
        q_ref[...] = (_dot(xn, w_q_v[...]) * (HEAD_DIM ** -0.5)).astype(BF16)
        kvs_v[0:BLOCK, :] = kvp_ref[...]
        kvs_v[BLOCK:, :] = kv_ref[...]
        lane = lax.broadcasted_iota(jnp.int32, (BLOCK, 128), 1)
        ii = lax.broadcasted_iota(jnp.int32, (BLOCK, 2 * BLOCK), 0)
        jj = lax.broadcasted_iota(jnp.int32, (BLOCK, 2 * BLOCK), 1)
        dist = ii + BLOCK - jj
        inband = (dist >= 0) & (dist < BLOCK)
        distf = dist.astype(F32)

        def blk_body(b, carry):
            r0 = pl.multiple_of(b * BLOCK, BLOCK)
            valid = inband & ((jj >= BLOCK) | jnp.logical_not(jnp.logical_and(ti == 0, b == 0)))
            qb = q_ref[pl.ds(r0, BLOCK), :]
            band = kvs_v[pl.ds(r0, 2 * BLOCK), :]
            lse_mat = jnp.zeros((BLOCK, 128), F32)
            outs = []
            for hq in range(N_Q_HEADS):
                kh = hq // GQA_GROUP
                k_h = band[:, kh * HEAD_DIM:(kh + 1) * HEAD_DIM]
                v_h = band[:, KV_DIM + kh * HEAD_DIM:KV_DIM + (kh + 1) * HEAD_DIM]
                s = _dot_nt(qb[:, hq * HEAD_DIM:(hq + 1) * HEAD_DIM], k_h) - _SLOPES[hq] * distf
                s = jnp.where(valid, s, NEG)
                sink = sink_ref[hq]
                m = jnp.maximum(jnp.max(s, axis=1, keepdims=True), sink)
                e = jnp.exp(s - m)
                den = jnp.sum(e, axis=1, keepdims=True) + jnp.exp(sink - m)
                outs.append(_dot((e / den).astype(BF16), v_h))
                lse_mat = jnp.where(lane == hq, m + jnp.log(den), lse_mat)
            ao_ref[pl.ds(r0, BLOCK), :] = jnp.concatenate(outs, axis=1).astype(BF16)
            lse_ref[pl.ds(r0, BLOCK), :] = lse_mat
            return carry

        lax.fori_loop(0, nb, blk_body, 0)
        out_ref[...] = xv + _dot(ao_ref[...], w_o_v[...])

    row = _row_spec(tm, D_MODEL)
    prev_spec = pl.BlockSpec((BLOCK, 2 * KV_DIM), lambda i: (jnp.maximum(i * nb - 1, 0), 0))
    return pl.pallas_call(
        body, name="attn_fwd", grid=(nt,),
        in_specs=[row, _const_spec((1, D_MODEL)), _row_spec(tm, 2 * KV_DIM), prev_spec, SMEM, ANY, ANY],
        out_specs=[row, row, row, _row_spec(tm, 128)],
        out_shape=[jax.ShapeDtypeStruct((T, D_MODEL), F32), jax.ShapeDtypeStruct((T, D_MODEL), BF16),
                   jax.ShapeDtypeStruct((T, D_MODEL), BF16), jax.ShapeDtypeStruct((T, 128), F32)],
        scratch_shapes=[pltpu.VMEM((D_MODEL, D_MODEL), BF16), pltpu.VMEM((D_MODEL, D_MODEL), BF16),
                        pltpu.VMEM((tm + BLOCK, 2 * KV_DIM), BF16), pltpu.SemaphoreType.DMA((2,))],
        compiler_params=_params(),
    )(h, nmix, kv, kv, sinks, w_q, w_o)


def _attn_bwd(dh, h, q, kv, ao, lse, nmix, sinks, w_q, w_o):
    T = h.shape[0]
    tm = min(512, T)
    nt = T // tm
    nb = tm // BLOCK

    def body(dh_ref, h_ref, q_ref, kv_ref, kvp_ref, ao_ref, lse_ref, n_ref, sink_ref, w_q_hbm, w_o_hbm,
             dhin_ref, dwq_ref, dwo_ref, dkv_ref, dsink_ref, dn_ref,
             w_q_v, w_o_v, kvs_v, dao_v, dq_v, dkv_v, carry_v, sem):
        _load_once([(w_q_hbm, w_q_v), (w_o_hbm, w_o_v)], sem)
        _zero_first([carry_v, dsink_ref, dn_ref, dwq_ref, dwo_ref])
        ti = nt - 1 - pl.program_id(0)
        dout = dh_ref[...]
        doutb = dout.astype(BF16)
        dao_v[...] = _dot_nt(doutb, w_o_v[...])
        dwo_ref[...] += _dot_tn(ao_ref[...], doutb)
        kvs_v[0:BLOCK, :] = kvp_ref[...]
        kvs_v[BLOCK:, :] = kv_ref[...]
        dkv_v[0:tm, :] = jnp.zeros((tm, 2 * KV_DIM), F32)
        dkv_v[tm:, :] = carry_v[...]
        lane = lax.broadcasted_iota(jnp.int32, (BLOCK, 128), 1)
        lane8 = lax.broadcasted_iota(jnp.int32, (8, 128), 1)
        bias, jj = _band_bias()

        def blk_body(b, dsk):
            r0 = pl.multiple_of(b * BLOCK, BLOCK)
            no_prev = jnp.logical_and(jnp.logical_and(ti == 0, b == 0), jj < BLOCK)
            qb = q_ref[pl.ds(r0, BLOCK), :]
            band = kvs_v[pl.ds(r0, 2 * BLOCK), :]
            aob = ao_ref[pl.ds(r0, BLOCK), :].astype(F32)
            daob = dao_v[pl.ds(r0, BLOCK), :]
            lse_mat = lse_ref[pl.ds(r0, BLOCK), :]
            dqs = []
            dks = []
            dvs = []
            for kh in range(N_KV_HEADS):
                k_h = band[:, kh * HEAD_DIM:(kh + 1) * HEAD_DIM]
                v_h = band[:, KV_DIM + kh * HEAD_DIM:KV_DIM + (kh + 1) * HEAD_DIM]
                q_g = _stack_heads(qb, kh)
                dao_g = _stack_heads(daob, kh)
                s = _dot_nt(q_g, k_h) - bias[kh]
                s = jnp.where(no_prev, NEG, s)
                lse = jnp.concatenate(
                    [jnp.sum(jnp.where(lane == kh * GQA_GROUP + g, lse_mat, 0.0), axis=1, keepdims=True)
                     for g in range(GQA_GROUP)], axis=0)
                pr = jnp.exp(s - lse)
                dd = jnp.sum(dao_g * _stack_heads(aob, kh), axis=1, keepdims=True)
                dao_gb = dao_g.astype(BF16)
                dp = _dot_nt(dao_gb, v_h)
                dsb = (pr * (dp - dd)).astype(BF16)
                dq_g = _dot(dsb, k_h) * (HEAD_DIM ** -0.5)
                dks.append(_dot_tn(dsb, q_g))
                dvs.append(_dot_tn(pr.astype(BF16), dao_gb))
                psink = jnp.exp(_sink_column(sink_ref, kh) - lse) * dd
                for g in range(GQA_GROUP):
                    rows = slice(g * BLOCK, (g + 1) * BLOCK)
                    dqs.append(dq_g[rows])
                    dsk = dsk - jnp.where(lane8 == kh * GQA_GROUP + g, jnp.sum(psink[rows]), 0.0)
            dq_v[pl.ds(r0, BLOCK), :] = jnp.concatenate(dqs, axis=1)
            dkv_v[pl.ds(r0, 2 * BLOCK), :] += jnp.concatenate(dks + dvs, axis=1)
            return dsk

        dsk = lax.fori_loop(0, nb, blk_body, jnp.zeros((8, 128), F32))
        dsink_ref[...] += dsk
        dqb = dq_v[...].astype(BF16)
        dxn = _dot_nt(dqb, w_q_v[...])
        xn, xh, r = _rms(h_ref[...], n_ref[...])
        dwq_ref[...] += _dot_tn(xn.astype(BF16), dqb)
        dxx, dn = _rms_bwd(dxn, xh, r, n_ref[...])
        dn_ref[...] += dn
        dhin_ref[...] = dout + dxx
        dkv_ref[...] = dkv_v[BLOCK:, :]
        carry_v[...] = dkv_v[0:BLOCK, :]

    rev = functools.partial(_row_spec, rev_nt=nt)
    row = rev(tm, D_MODEL)
    prev_spec = pl.BlockSpec((BLOCK, 2 * KV_DIM), lambda i: (jnp.maximum((nt - 1 - i) * nb - 1, 0), 0))
    return pl.pallas_call(
        body, name="attn_bwd", grid=(nt,),
        in_specs=[row, row, row, rev(tm, 2 * KV_DIM), prev_spec, row, rev(tm, 128),
                  _const_spec((1, D_MODEL)), SMEM, ANY, ANY],
        out_specs=[row, _const_spec((D_MODEL, D_MODEL)), _const_spec((D_MODEL, D_MODEL)), rev(tm, 2 * KV_DIM),
                   _const_spec((8, 128)), _const_spec((1, D_MODEL))],
        out_shape=[jax.ShapeDtypeStruct((T, D_MODEL), F32), jax.ShapeDtypeStruct((D_MODEL, D_MODEL), F32),
                   jax.ShapeDtypeStruct((D_MODEL, D_MODEL), F32), jax.ShapeDtypeStruct((T, 2 * KV_DIM), F32),
                   jax.ShapeDtypeStruct((8, 128), F32), jax.ShapeDtypeStruct((1, D_MODEL), F32)],
        scratch_shapes=[pltpu.VMEM((D_MODEL, D_MODEL), BF16), pltpu.VMEM((D_MODEL, D_MODEL), BF16),
                        pltpu.VMEM((tm + BLOCK, 2 * KV_DIM), BF16), pltpu.VMEM((tm, D_MODEL), F32),
                        pltpu.VMEM((tm, D_MODEL), F32), pltpu.VMEM((tm + BLOCK, 2 * KV_DIM), F32),
                        pltpu.VMEM((BLOCK, 2 * KV_DIM), F32), pltpu.SemaphoreType.DMA((2,))],
        compiler_params=_params(),
    )(dh, h, q, kv, kv, ao, lse, nmix, sinks, w_q, w_o)


def _mesh_pos():
    return lax.axis_index("x"), lax.axis_index("y"), lax.axis_index("c")


def _other_chips(x, y):
    return [(1 - x, y), (x, 1 - y), (1 - x, 1 - y)]


HBM_SPEC = pl.BlockSpec(memory_space=pltpu.HBM)
SEM_SPEC = pl.BlockSpec(memory_space=pltpu.SEMAPHORE)


def _split_call(name, bufs, waits=(), starts=(), after=()):
    n, nw, ns, na = len(bufs), len(waits), len(starts), len(after)

    def body(*refs):
        brefs = refs[:n]
        wsems = [(refs[n + 2 * k], refs[n + 2 * k + 1]) for k in range(nw)]
        o = n + 2 * nw + na
        ssems = [(refs[o + 2 * k], refs[o + 2 * k + 1]) for k in range(ns)]
        for (ss, rs), (_, _, fn) in zip(wsems, waits):
            for sending, arriving in fn(brefs, ss, rs):
                sending.wait_send()
                arriving.wait_recv()
        for (ss, rs), (_, fn) in zip(ssems, starts):
            for sending, _ in fn(brefs, ss, rs):
                sending.start()
        if ns:
            token = refs[o + 2 * ns + n]
            token[...] = jnp.zeros(token.shape, token.dtype)

    out_shape, out_specs = [], []
    for cnt, _ in starts:
        out_shape += [pltpu.SemaphoreType.DMA((cnt,)), pltpu.SemaphoreType.DMA((cnt,))]
        out_specs += [SEM_SPEC, SEM_SPEC]
    out_shape += [pltpu.HBM(b.shape, b.dtype) for b in bufs]
    out_specs += [HBM_SPEC] * n
    if ns:
        out_shape.append(jax.ShapeDtypeStruct((8, 128), F32))
        out_specs.append(pl.BlockSpec(memory_space=pltpu.VMEM))
    args = [pltpu.with_memory_space_constraint(b, pltpu.HBM) for b in bufs]
    for ss, rs, _ in waits:
        args += [ss, rs]
    args += list(after)
    res = pl.pallas_call(
        body, name=name, out_shape=tuple(out_shape),
        in_specs=[HBM_SPEC] * n + [SEM_SPEC] * (2 * nw) + [ANY] * na, out_specs=tuple(out_specs),
        input_output_aliases={i: 2 * ns + i for i in range(n)},
        compiler_params=pltpu.CompilerParams(has_side_effects=pltpu.SideEffectType.DATAFLOW_SIDE_EFFECTING),
    )(*args)
    sems = [(res[2 * k], res[2 * k + 1]) for k in range(ns)]
    return list(res[2 * ns:2 * ns + n]), sems, (res[2 * ns + n] if ns else None)


def _cast_place(items, name):
    n = len(items)
    mats = [a.shape[-2:] for a, _, _ in items]

    def body(*refs):
        ins, outs, scr, sem = refs[:n], refs[n:2 * n], refs[2 * n:3 * n], refs[3 * n]
        x, y, _ = _mesh_pos()
        cps = []
        for t in range(n):
            scr[t][...] = ins[t][...].astype(scr[t].dtype)
            cp = pltpu.make_async_copy(scr[t], outs[t].at[2 * x + y], sem.at[t])
            cp.start()
            cps.append(cp)
        for cp in cps:
            cp.wait()

    def spec(idx, shape):
        return pl.BlockSpec((None,) * len(idx) + tuple(shape), lambda i: tuple(idx) + (0, 0))

    return pl.pallas_call(
        body, name=name, grid=(1,),
        in_specs=[spec(idx, mat) for (_, idx, _), mat in zip(items, mats)], out_specs=[ANY] * n,
        out_shape=[jax.ShapeDtypeStruct((N_SHARD,) + tuple(mat), dt) for (_, _, dt), mat in zip(items, mats)],
        scratch_shapes=[pltpu.VMEM(tuple(mat), dt) for (_, _, dt), mat in zip(items, mats)]
        + [pltpu.SemaphoreType.DMA((n,))],
        compiler_params=_params(),
    )(*[a for a, _, _ in items])


def _gather_ici(idx):
    def fn(bufs, ss, rs):
        x, y, c = _mesh_pos()
        pairs = []
        for k, t in enumerate(idx):
            half = bufs[t].shape[1] // 2
            mine = bufs[t].at[2 * x + y, pl.ds(c * half, half), :]
            for j, (cx, cy) in enumerate(_other_chips(x, y)):
                theirs = bufs[t].at[2 * cx + cy, pl.ds(c * half, half), :]
                sem = dict(send_sem=ss.at[3 * k + j], recv_sem=rs.at[3 * k + j],
                           device_id=(cx, cy, c), device_id_type=MESH)
                pairs.append((pltpu.make_async_remote_copy(src_ref=mine, dst_ref=mine, **sem),
                              pltpu.make_async_remote_copy(src_ref=mine, dst_ref=theirs, **sem)))
        return pairs
    return fn


def _gather_d2d(idx):
    def fn(bufs, ss, rs):
        x, y, c = _mesh_pos()
        pairs = []
        for k, t in enumerate(idx):
            half = bufs[t].shape[1] // 2
            for j, (cx, cy) in enumerate(_other_chips(x, y)):
                got = bufs[t].at[2 * cx + cy, pl.ds(c * half, half), :]
                theirs = bufs[t].at[2 * cx + cy, pl.ds((1 - c) * half, half), :]
                sem = dict(send_sem=ss.at[3 * k + j], recv_sem=rs.at[3 * k + j],
                           device_id=(x, y, 1 - c), device_id_type=MESH)
                pairs.append((pltpu.make_async_remote_copy(src_ref=got, dst_ref=got, **sem),
                              pltpu.make_async_remote_copy(src_ref=got, dst_ref=theirs, **sem)))
        return pairs
    return fn


def _alloc(shapes, name):
    def body(*refs):
        pass

    return pl.pallas_call(body, name=name, out_specs=[ANY] * len(shapes),
                          out_shape=[jax.ShapeDtypeStruct(s, d) for s, d in shapes])()


def _send_to_sibling(n):
    def fn(bufs, ss, rs):
        x, y, c = _mesh_pos()
        pairs = []
        for t in range(n):
            src = bufs[t]
            if len(src.shape) == 3:
                half = src.shape[1] // 2
                src = src.at[:, pl.ds((1 - c) * half, half), :]
            cp = pltpu.make_async_remote_copy(src_ref=src, dst_ref=bufs[n + t], send_sem=ss.at[t],
                                              recv_sem=rs.at[t], device_id=(x, y, 1 - c), device_id_type=MESH)
            pairs.append((cp, cp))
        return pairs
    return fn


def _send_to_chips(n):
    def fn(bufs, ss, rs):
        x, y, c = _mesh_pos()
        pairs = []
        for j, (cx, cy) in enumerate(_other_chips(x, y)):
            for t in range(n):
                src = bufs[t].at[j] if len(bufs[t].shape) == 3 else bufs[t]
                cp = pltpu.make_async_remote_copy(src_ref=src, dst_ref=bufs[n + t].at[j], send_sem=ss.at[3 * t + j],
                                                  recv_sem=rs.at[3 * t + j], device_id=(cx, cy, c),
                                                  device_id_type=MESH)
                pairs.append((cp, cp))
        return pairs
    return fn


class _Exchange:
    def __init__(self, name, srcs, land_shapes, fn, n_sems):
        self.name, self.fn = name, fn
        lands = _alloc(land_shapes, name + "_alloc")
        self.n = len(srcs)
        self.bufs, sems, self.token = _split_call(name + "_start", list(srcs) + list(lands),
                                                  starts=[(n_sems, fn)])
        self.sems = sems[0]

    def finish(self, after=()):
        bufs, _, _ = _split_call(self.name + "_wait", self.bufs, waits=[(*self.sems, self.fn)], after=after)
        return bufs[:self.n], bufs[self.n:]


def _row_block(rows, cols, mult=8, limit=3 * 512 * 1024, itemsize=4):
    best = None
    for br in range(mult, rows + 1, mult):
        if rows % br == 0 and br * cols * itemsize <= limit:
            best = br
    assert best is not None, (rows, cols)
    return best


_GROUP_BLOCK_BYTES = 512 * 1024


def _group_plan(ss):
    plan = []
    for s in ss:
        half, cols = s.shape[-2:]
        br = _row_block(half, cols, mult=16, limit=_GROUP_BLOCK_BYTES)
        plan.append((br, half // br))
    return plan, max(nr for _, nr in plan)


def _chip_partial(gs, ss, ids, name):
    n = len(gs)
    plan, steps = _group_plan(ss)

    def body(ids_ref, *refs):
        for t in range(n):
            refs[2 * n + t][...] = (refs[t][...] + refs[n + t][...]).astype(BF16)

    g_specs, s_specs, o_specs = [], [], []
    for (br, nr), s in zip(plan, ss):
        blk = (None, br, s.shape[2])
        g_specs.append(pl.BlockSpec(
            blk, lambda j, r, ids_ref, nr=nr: (ids_ref[2 + j], ids_ref[0] * nr + jnp.minimum(r, nr - 1), 0)))
        s_specs.append(pl.BlockSpec(blk, lambda j, r, ids_ref, nr=nr: (ids_ref[2 + j], jnp.minimum(r, nr - 1), 0)))
        o_specs.append(pl.BlockSpec(blk, lambda j, r, ids_ref, nr=nr: (j, jnp.minimum(r, nr - 1), 0)))
    return pl.pallas_call(
        body, name=name,
        grid_spec=pltpu.PrefetchScalarGridSpec(num_scalar_prefetch=1, grid=(3, steps),
                                               in_specs=g_specs + s_specs, out_specs=o_specs),
        out_shape=[jax.ShapeDtypeStruct((3,) + s.shape[1:], BF16) for s in ss],
        compiler_params=pltpu.CompilerParams(dimension_semantics=("arbitrary", "arbitrary"),
                                             vmem_limit_bytes=VMEM_LIMIT),
    )(ids, *gs, *ss)


def _chip_sum(gs, ss, qs, ids, name):
    n = len(gs)
    plan, steps = _group_plan(ss)

    def body(ids_ref, *refs):
        for t in range(n):
            q_ref = refs[2 * n + t]
            own = refs[t][...] + refs[n + t][...]
            refs[3 * n + t][...] = (own + q_ref[2].astype(F32)) + (q_ref[0].astype(F32) + q_ref[1].astype(F32))

    g_specs, s_specs, q_specs, o_specs = [], [], [], []
    for (br, nr), s in zip(plan, ss):
        cols = s.shape[2]
        g_specs.append(pl.BlockSpec(
            (None, br, cols), lambda r, ids_ref, nr=nr: (ids_ref[1], ids_ref[0] * nr + jnp.minimum(r, nr - 1), 0)))
        s_specs.append(pl.BlockSpec((None, br, cols), lambda r, ids_ref, nr=nr: (ids_ref[1], jnp.minimum(r, nr - 1), 0)))
        q_specs.append(pl.BlockSpec((3, br, cols), lambda r, ids_ref, nr=nr: (0, jnp.minimum(r, nr - 1), 0)))
        o_specs.append(pl.BlockSpec((br, cols), lambda r, ids_ref, nr=nr: (jnp.minimum(r, nr - 1), 0)))
    return pl.pallas_call(
        body, name=name,
        grid_spec=pltpu.PrefetchScalarGridSpec(num_scalar_prefetch=1, grid=(steps,),
                                               in_specs=g_specs + s_specs + q_specs, out_specs=o_specs),
        out_shape=[jax.ShapeDtypeStruct(s.shape[1:], F32) for s in ss],
        compiler_params=pltpu.CompilerParams(dimension_semantics=("arbitrary",), vmem_limit_bytes=VMEM_LIMIT),
    )(ids, *gs, *ss, *qs)


def _adamw_math(w, g, m, v):
    mn = ADAM_B1 * m + (1.0 - ADAM_B1) * g
    vn = ADAM_B2 * v + (1.0 - ADAM_B2) * (g * g)
    m_hat = mn / (1.0 - ADAM_B1 ** ADAM_STEP)
    v_hat = vn / (1.0 - ADAM_B2 ** ADAM_STEP)
    return -ADAM_LR * (m_hat / (jnp.sqrt(v_hat) + ADAM_EPS) + ADAM_WD * w), mn, vn


def _adamw_halves(w, own, sib, m, v, ids, name, layer=0, n_layers=1, stacked=None):
    C = w.shape[1]
    R = w.shape[0] // n_layers
    half = R // 2
    br = _row_block(half, C)
    nh = half // br
    base = layer * 2 * nh

    def body(ids_ref, w_ref, own_ref, sib_ref, m_ref, v_ref, *rest):
        g_ref, d_ref, mo_ref, vo_ref = rest[-4:]
        is_own = (pl.program_id(0) // nh) == ids_ref[0]
        g = jnp.where(is_own, own_ref[...], sib_ref[...])
        g_ref[...] = g
        d_ref[...], mo_ref[...], vo_ref[...] = _adamw_math(w_ref[...], g, m_ref[...], v_ref[...])

    full = pl.BlockSpec((br, C), lambda r, ids_ref: (base + r, 0))
    own_spec = pl.BlockSpec((br, C), lambda r, ids_ref: (jnp.clip(r - ids_ref[0] * nh, 0, nh - 1), 0))
    sib_spec = pl.BlockSpec((br, C), lambda r, ids_ref: (jnp.clip(r - (1 - ids_ref[0]) * nh, 0, nh - 1), 0))
    in_specs = [full, own_spec, sib_spec, full, full]
    args = [ids, w, own, sib, m, v]
    aliases = {}
    if stacked is not None:
        in_specs += [ANY] * 4
        args += list(stacked)
        aliases = {6 + k: k for k in range(4)}
    return pl.pallas_call(
        body, name=name,
        grid_spec=pltpu.PrefetchScalarGridSpec(
            num_scalar_prefetch=1, grid=(2 * nh,), in_specs=in_specs, out_specs=[full] * 4),
        out_shape=[jax.ShapeDtypeStruct(w.shape, F32)] * 4, input_output_aliases=aliases,
        compiler_params=_params(),
    )(*args)


_PACK_UNIT = 1024


def _pack(arrs):
    flat = []
    for a in arrs:
        f = a.reshape(-1).astype(F32)
        pad = (-f.shape[0]) % _PACK_UNIT
        if pad:
            f = jnp.concatenate([f, jnp.zeros((pad,), F32)])
        flat.append(f)
    return jnp.concatenate(flat).reshape(-1, 128)


def kernel(x, p, norm_mix, norm_ffn, norm_ple, norm_kv, norm_final, a_w_in, a_norm_v, a_w_s, a_b_s, a_w_out, w_kv, b_w_q, b_sinks, b_w_o, f_w_up, f_conv_w, f_conv_b, f_w_down, ple_w_in, ple_w_gate, ple_b_gate, loss_target, m_norm_mix, m_norm_ffn, m_norm_ple, m_norm_kv, m_norm_final, m_a_w_in, m_a_norm_v, m_a_w_s, m_a_b_s, m_a_w_out, m_w_kv, m_b_w_q, m_b_sinks, m_b_w_o, m_f_w_up, m_f_conv_w, m_f_conv_b, m_f_w_down, m_ple_w_in, m_ple_w_gate, m_ple_b_gate, v_norm_mix, v_norm_ffn, v_norm_ple, v_norm_kv, v_norm_final, v_a_w_in, v_a_norm_v, v_a_w_s, v_a_b_s, v_a_w_out, v_w_kv, v_b_w_q, v_b_sinks, v_b_w_o, v_f_w_up, v_f_conv_w, v_f_conv_b, v_f_w_down, v_ple_w_in, v_ple_w_gate, v_ple_b_gate):
    given = dict(locals())

    small_shard = _pack([a_norm_v, f_conv_w])
    pad_rows = (-small_shard.shape[0]) % 16
    if pad_rows:
        small_shard = jnp.concatenate([small_shard, jnp.zeros((pad_rows, 128), F32)])
    groups = [
        [(a_w_in, (0,), BF16), (a_w_out, (0,), BF16), (small_shard, (), F32)],
        [(f_w_up, (0,), BF16), (f_w_down, (0,), BF16)],
        [(ple_w_in, (0,), BF16), (ple_w_gate, (0,), BF16), (w_kv, (), BF16), (b_w_q, (0,), BF16),
         (b_w_o, (0,), BF16), (f_w_up, (1,), BF16), (f_w_down, (1,), BF16), (ple_w_in, (1,), BF16),
         (ple_w_gate, (1,), BF16)],
    ]
    lands, spans, start = [], [], 0
    for gi, items in enumerate(groups):
        lands += _cast_place(items, f"cast_place_g{gi}")
        spans.append(list(range(start, start + len(items))))
        start += len(items)
    lands, ici_sems, _ = _split_call("gather_start", lands,
                                     starts=[(3 * len(sp), _gather_ici(sp)) for sp in spans])

    def finish_group(gi, after):
        sp = spans[gi]
        local = list(range(len(sp)))
        bufs = [lands[t] for t in sp]
        bufs, d2d_sems, _ = _split_call(f"gather_pass_g{gi}", bufs, waits=[(*ici_sems[gi], _gather_ici(local))],
                                        starts=[(3 * len(sp), _gather_d2d(local))], after=after)
        bufs, _, _ = _split_call(f"gather_done_g{gi}", bufs, waits=[(*d2d_sems[0], _gather_d2d(local))])
        return bufs

    def stage0():
        b_in, b_out, b_small = finish_group(0, ())
        small_full = b_small.reshape(N_SHARD, -1)
        gv_full = small_full[:, :256].reshape(1, D_MODEL)
        cw_full = small_full[:, _PACK_UNIT:_PACK_UNIT + 2 * 3 * FF_BLK].reshape(N_SHARD, 2, 3, FF_BLK)
        cw_full = jnp.transpose(cw_full, (1, 2, 0, 3)).reshape(2, 3, N_FF)
        return gv_full, cw_full, b_in, b_out.reshape(D_MODEL, D_MODEL)

    def stage1(after):
        b_up, b_dn = finish_group(1, after)
        return b_up, b_dn.reshape(D_FF, D_MODEL)

    def stage2(after):
        pin0, gate0, kv_w, wq, wo, up1, dn1, pin1, gate1 = finish_group(2, after)
        sq = lambda a: a.reshape(D_MODEL, -1)
        return dict(w_pin=[pin0, pin1], w_gate=[sq(gate0), sq(gate1)], w_kv=sq(kv_w), w_q=sq(wq), w_o=sq(wo),
                    w_up1=up1, w_dn1=dn1.reshape(D_FF, D_MODEL))

    dx, (loss, (out_g, out_d, out_m, out_v)) = _local_step(
        x[0], p[0, 0], p[1, 0], loss_target[0], norm_mix, norm_ffn, norm_ple, norm_kv, norm_final, a_w_s, a_b_s,
        b_sinks, f_conv_b, ple_b_gate, stage0, stage1, stage2, _Reducer(given))
    weight_names = ['norm_mix', 'norm_ffn', 'norm_ple', 'norm_kv', 'norm_final', 'a_w_in', 'a_norm_v', 'a_w_s',
                    'a_b_s', 'a_w_out', 'w_kv', 'b_w_q', 'b_sinks', 'b_w_o', 'f_w_up', 'f_conv_w', 'f_conv_b',
                    'f_w_down', 'ple_w_in', 'ple_w_gate', 'ple_b_gate']
    return (loss, dx.reshape(x.shape), *[out_g[k] for k in weight_names], *[out_d[k] for k in weight_names],
            *[out_m[k] for k in weight_names], *[out_v[k] for k in weight_names])


def _local_step(xs, p0, p1, tgt, norm_mix, norm_ffn, norm_ple, norm_kv, norm_final, a_w_s, a_b_s, b_sinks,
                f_conv_b, ple_b_gate, stage0, stage1, stage2, sched):
    tril = jnp.tril(jnp.ones((CHUNK, CHUNK), F32))
    wsm = (a_w_s[0] * tril[None]).astype(BF16)
    bsb = jnp.broadcast_to(a_b_s[0][:, :, None], (A_GROUPS, CHUNK, CHUNK))
    sinks = b_sinks[0]
    row = lambda a: a.reshape(1, -1)

    gv_full, cw_full, w_in, w_out = stage0()
    h1, zp = _mixer_a_fwd(xs, row(norm_mix[0]), gv_full, wsm, bsb, w_in, w_out)
    w_up0, w_dn0 = stage1((h1,))
    h2, hh0, c0 = _ffn_fwd(h1, row(norm_ffn[0]), cw_full[0], row(f_conv_b[0]), w_up0, w_dn0, 0)
    rest = stage2((h2,))
    w_pin, w_gate, w_kv_f, w_q, w_o = rest['w_pin'], rest['w_gate'], rest['w_kv'], rest['w_q'], rest['w_o']
    w_up = [w_up0, rest['w_up1']]
    w_dn = [w_dn0, rest['w_dn1']]
    h3, pe0, a0, kv = _ple_fwd_kv(h2, p0, row(norm_ple[0]), row(ple_b_gate[0]), row(norm_kv), w_pin[0], w_gate[0], w_kv_f)
    h4, q, ao, lse = _attn_fwd(h3, row(norm_mix[1]), kv, sinks, w_q, w_o)
    h5, hh1, c1 = _ffn_fwd(h4, row(norm_ffn[1]), cw_full[1], row(f_conv_b[1]), w_up[1], w_dn[1], 1)
    dh6, pe1, a1, loss_acc, dn_final = _ple_fwd_final(
        h5, p1, tgt, row(norm_ple[1]), row(ple_b_gate[1]), row(norm_final), w_pin[1], w_gate[1])

    def pieces(g):
        return g.reshape(N_SHARD, -1, g.shape[-1])

    dh5, g_pin1, g_gate1, dbg1, dnple1 = _ple_bwd(dh6, h5, pe1, a1, p1, row(norm_ple[1]), w_gate[1], 1)
    early = {('ple_w_in', 1): g_pin1, ('ple_w_gate', 1): pieces(g_gate1)}
    dh4, g_up1, g_dn1, dcw1, dcb1, dnffn1 = _ffn_bwd(
        dh5, h4, hh1, c1, row(norm_ffn[1]), cw_full[1], w_up[1], w_dn[1], 1)
    early['f_w_down', 1] = pieces(g_dn1)
    early['f_w_up', 1] = g_up1
    dh3a, g_wq, g_wo, dkv, dsink, dnmix1 = _attn_bwd(dh4, h3, q, kv, ao, lse, row(norm_mix[1]), sinks, w_q, w_o)
    early['b_w_o', 0] = pieces(g_wo)
    early['b_w_q', 0] = pieces(g_wq)
    dh2, g_pin0, g_gate0, dbg0, dnple0, g_wkv, dnkv = _ple_bwd(
        dh3a, h2, pe0, a0, p0, row(norm_ple[0]), w_gate[0], 0, kv_args=(h3, dkv, row(norm_kv), w_kv_f))
    early['w_kv', 0] = pieces(g_wkv)
    early['ple_w_in', 0] = g_pin0
    early['ple_w_gate', 0] = pieces(g_gate0)
    deps = sched.early_ready(early)
    dh1, g_up0, g_dn0, dcw0, dcb0, dnffn0 = _ffn_bwd(
        dh2, h1, hh0, c0, row(norm_ffn[0]), cw_full[0], w_up[0], w_dn[0], 0, deps=deps,
        between=lambda part: sched.after_ffn_half((part,)))
    deps = sched.ffn0_ready({('f_w_down', 0): pieces(g_dn0), ('f_w_up', 0): g_up0})
    dx, g_win, g_wout, dws, dbs, dgv, dnmix0 = _mixer_a_bwd(
        dh1, xs, zp, row(norm_mix[0]), gv_full, wsm, bsb, tril, w_in, w_out, deps=deps)
    g_wout = pieces(g_wout)

    small_grads = {
        'norm_mix': jnp.concatenate([dnmix0, dnmix1]), 'norm_ffn': jnp.concatenate([dnffn0, dnffn1]),
        'norm_ple': jnp.concatenate([dnple0, dnple1]), 'norm_kv': dnkv, 'norm_final': dn_final,
        'a_norm_v': dgv, 'a_w_s': dws.reshape(A_GROUPS * CHUNK, CHUNK), 'a_b_s': dbs[:, :, 0],
        'b_sinks': dsink[0:1, :], 'f_conv_w': jnp.concatenate([dcw0, dcw1]),
        'f_conv_b': jnp.concatenate([dcb0, dcb1]), 'ple_b_gate': jnp.concatenate([dbg0, dbg1]),
        'loss': loss_acc,
    }
    outs = sched.finish({('a_w_in', 0): g_win, ('a_w_out', 0): g_wout}, small_grads, (dx,))
    return dx, outs


class _Reducer:
    def __init__(self, given):
        self.given = given
        cx, cy, cc = _mesh_pos()
        self.shard = 2 * cx + cy
        s = self.shard
        self.ids = jnp.stack([cc, s, s ^ 2, s ^ 1, s ^ 3]).astype(jnp.int32)
        self.out = [{}, {}, {}, {}]
        self.stacked = {}

    def _send(self, tag, grads, small=()):
        keys = list(grads)
        srcs = [grads[k] for k in keys] + list(small)
        shapes = [((N_SHARD, g.shape[1] // 2, g.shape[2]), F32) for g in srcs[:len(keys)]]
        shapes += [(s.shape, F32) for s in small]
        return keys, _Exchange(f"send_{tag}", srcs, shapes, _send_to_sibling(len(srcs)), len(srcs))

    def _exchange(self, tag, keys, send, after):
        srcs, lands = send.finish(after)
        n = len(keys)
        parts = _chip_partial(srcs[:n], lands[:n], self.ids, f"chip_partial_{tag}")
        shapes = [(p.shape, BF16) for p in parts]
        if len(srcs) > n:
            small = _small_add(srcs[n:], lands[n:])
            parts += small
            shapes += [((3,) + s.shape, F32) for s in small]
        exch = _Exchange(f"exch_{tag}", parts, shapes, _send_to_chips(len(parts)), 3 * len(parts))
        return (keys, srcs[:n], lands[:n], exch)

    def _swap(self, tag, state, after):
        keys, grads, sib, exch = state
        parts, recv = exch.finish(after)
        n = len(keys)
        own = _chip_sum(grads, sib, recv[:n], self.ids, f"chip_sum_{tag}")
        small_red = _small_sum(parts[n:], recv[n:]) if len(parts) > n else None
        return keys, _Exchange(f"swap_{tag}", own, [(o.shape, F32) for o in own], _send_to_sibling(n), n), small_red

    def _adamw(self, keys, swap, after):
        own, sib = swap.finish(after)
        last = None
        for (name, layer), o, s in zip(keys, own, sib):
            w = self.given[name]
            n_layers = w.shape[0] if w.ndim == 3 else 1
            c2 = w.shape[-1]
            res = _adamw_halves(w.reshape(-1, c2), o, s, self.given['m_' + name].reshape(-1, c2),
                                self.given['v_' + name].reshape(-1, c2), self.ids, f"adamw_{name}{layer}",
                                layer, n_layers, self.stacked.get(name))
            self.stacked[name] = res
            if layer == 0:
                for dst, r in zip(self.out, res):
                    dst[name] = r.reshape(w.shape)
            last = res[0]
        return last

    def early_ready(self, grads):
        self.e_keys, self.e_send = self._send("e", grads)
        return (self.e_send.token,)

    def after_ffn_half(self, after):
        self.e_state = self._exchange("e", self.e_keys, self.e_send, after)
        return (self.e_state[3].token,)

    def ffn0_ready(self, grads):
        _, self.e_swap, _ = self._swap("e", self.e_state, tuple(grads.values()))
        f_keys, f_send = self._send("f", grads)
        self.f_state = self._exchange("f", f_keys, f_send, ())
        return (self.f_state[3].token, self.e_swap.token)

    def finish(self, grads, small_grads, after):
        small_names = list(small_grads)
        a_keys, a_send = self._send("a", grads, [small_grads[k] for k in small_names])
        a_state = self._exchange("a", a_keys, a_send, after)
        e_done = self._adamw(self.e_keys, self.e_swap, (a_state[3].token,))
        f_keys, f_swap, _ = self._swap("f", self.f_state, (e_done,))
        f_done = self._adamw(f_keys, f_swap, ())
        _, a_swap, small_red = self._swap("a", a_state, (f_done,))
        self._adamw(a_keys, a_swap, ())

        given = self.given
        reduced = dict(zip(small_names, small_red))
        loss = reduced.pop('loss')[0, 0]
        names = list(reduced)
        items = []
        for k in names:
            g = reduced[k]
            cols = g.shape[1] // N_SHARD if k in ('a_norm_v', 'f_conv_w') else g.shape[1]
            view = lambda a: _lane_pad(a.reshape(g.shape[0], -1), cols)
            items.append((view(given[k]), g, view(given['m_' + k]), view(given['v_' + k])))
        res = _adamw_small(items, self.ids)
        for k, four in zip(names, res):
            width = given[k].size // four[0].shape[0]
            for dst, r in zip(self.out, four):
                dst[k] = r[:, :width].reshape(given[k].shape)
        return loss, self.out


def _lane_pad(a, cols):
    return a if a.shape[1] == cols else jnp.pad(a, ((0, 0), (0, cols - a.shape[1])))


def _small_add(a_list, b_list):
    n = len(a_list)

    def body(*refs):
        for t in range(n):
            refs[2 * n + t][...] = refs[t][...] + refs[n + t][...]

    return pl.pallas_call(body, name="chip_partial_small",
                          out_shape=[jax.ShapeDtypeStruct(a.shape, F32) for a in a_list])(*a_list, *b_list)


def _small_sum(parts, recvs):
    n = len(parts)

    def body(*refs):
        for t in range(n):
            q = refs[n + t]
            refs[2 * n + t][...] = (refs[t][...] + q[2]) + (q[0] + q[1])

    return pl.pallas_call(body, name="chip_sum_small",
                          out_shape=[jax.ShapeDtypeStruct(p.shape, F32) for p in parts])(*parts, *recvs)


def _adamw_small(items, ids):
    n = len(items)

    def body(ids_ref, *refs):
        for t in range(n):
            w_ref, g_ref, m_ref, v_ref = refs[4 * t:4 * t + 4]
            g_out, d_ref, mo_ref, vo_ref = refs[4 * n + 4 * t:4 * n + 4 * t + 4]
            g = g_ref[...]
            g_out[...] = g
            d_ref[...], mo_ref[...], vo_ref[...] = _adamw_math(w_ref[...], g, m_ref[...], v_ref[...])

    in_specs, out_specs, out_shape, args = [], [], [], []
    for w, g, m, v in items:
        full = pl.BlockSpec(w.shape, lambda i, ids_ref: (0, 0))
        g_spec = full if g.shape == w.shape else pl.BlockSpec(w.shape, lambda i, ids_ref: (0, ids_ref[1]))
        in_specs += [full, g_spec, full, full]
        out_specs += [full] * 4
        out_shape += [jax.ShapeDtypeStruct(w.shape, F32)] * 4
        args += [w, g, m, v]
    res = pl.pallas_call(
        body, name="adamw_small",
        grid_spec=pltpu.PrefetchScalarGridSpec(num_scalar_prefetch=1, grid=(1,), in_specs=in_specs,
                                               out_specs=out_specs),
        out_shape=out_shape, compiler_params=_params(),
    )(ids, *args)
    return [res[4 * t:4 * t + 4] for t in range(n)]
```

```python
import functools
import math

import numpy as np
import jax
import jax.numpy as jnp
from jax import lax
from jax.experimental import pallas as pl
from jax.experimental.pallas import tpu as pltpu

F32 = jnp.float32
BF16 = jnp.bfloat16

D_MODEL = 1024
CHUNK = 128
A_GROUPS = 8
HEAD_DIM = 64
N_Q_HEADS = 16
N_KV_HEADS = 4
GQA_GROUP = N_Q_HEADS // N_KV_HEADS
KV_DIM = N_KV_HEADS * HEAD_DIM
BLOCK = 128
D_FF = 2816
N_FF = 2 * D_FF
FF_BLK = N_FF // 4
PLE_DIM = 256
EPS = 1e-6
NEG = -1e30
N_SHARD = 4

ADAM_LR = 0.001
ADAM_B1 = 0.9
ADAM_B2 = 0.999
ADAM_EPS = 1e-08
ADAM_WD = 0.01
ADAM_STEP = 10

VMEM_LIMIT = 60 * 1024 * 1024
MESH = pl.DeviceIdType.MESH
ANY = pl.BlockSpec(memory_space=pl.ANY)
SMEM = pl.BlockSpec(memory_space=pltpu.SMEM)

_SLOPES = [float(np.float32(2.0 ** (-8.0 * (h + 1) / N_Q_HEADS))) for h in range(N_Q_HEADS)]


def _dot(a, b):
    return jnp.dot(a, b, preferred_element_type=F32)


def _dot_nt(a, b):
    return lax.dot_general(a, b, (((1,), (1,)), ((), ())), preferred_element_type=F32)


def _dot_tn(a, b):
    return lax.dot_general(a, b, (((0,), (0,)), ((), ())), preferred_element_type=F32)


def _rms(x, g):
    r = lax.rsqrt(jnp.mean(x * x, axis=-1, keepdims=True) + EPS)
    xh = x * r
    return xh * g, xh, r


def _rms_bwd(dy, xh, r, g):
    dxh = dy * g
    dg = jnp.sum(dy * xh, axis=0, keepdims=True)
    dx = r * (dxh - xh * jnp.mean(dxh * xh, axis=-1, keepdims=True))
    return dx, dg


_GELU_C = math.sqrt(2.0 / math.pi)


def _gelu(x):
    t = jnp.tanh(_GELU_C * (x + 0.044715 * (x * x * x)))
    return 0.5 * x * (1.0 + t)


def _gelu_grad(x):
    x2 = x * x
    t = jnp.tanh(_GELU_C * (x + 0.044715 * (x2 * x)))
    return 0.5 * (1.0 + t) + 0.5 * x * (1.0 - t * t) * (_GELU_C * (1.0 + 3.0 * 0.044715 * x2))


def _sigmoid(x):
    return 0.5 * jnp.tanh(0.5 * x) + 0.5


def _load_once(pairs, sem):
    @pl.when(pl.program_id(0) == 0)
    def _():
        cps = [pltpu.make_async_copy(s, d, sem.at[i]) for i, (s, d) in enumerate(pairs)]
        for cp in cps:
            cp.start()
        for cp in cps:
            cp.wait()


def _params(n_axes=1, vmem=VMEM_LIMIT):
    return pltpu.CompilerParams(dimension_semantics=("arbitrary",) * n_axes, vmem_limit_bytes=vmem)


def _row_spec(tm, n, rev_nt=None):
    if rev_nt is None:
        return pl.BlockSpec((tm, n), lambda i: (i, 0))
    return pl.BlockSpec((tm, n), lambda i: (rev_nt - 1 - i, 0))


def _const_spec(shape):
    nd = len(shape)
    return pl.BlockSpec(shape, lambda i: (0,) * nd)


def _add_deps(body, in_specs, args, deps):
    nd = len(deps)
    if nd == 0:
        return body, list(in_specs), list(args)

    def wrapped(*refs):
        return body(*refs[nd:])

    return wrapped, [ANY] * nd + list(in_specs), list(deps) + list(args)


def _zero_first(refs):
    @pl.when(pl.program_id(0) == 0)
    def _():
        for r in refs:
            r[...] = jnp.zeros(r.shape, r.dtype)


def _mixer_a_fwd(x, nmix, gv, wsm, bsb, w_in, w_out):
    T = x.shape[0]
    tm = min(512, T)
    nt = T // tm
    nw = 2 * D_MODEL // N_SHARD

    def body(x_ref, nmix_ref, gv_ref, ws_ref, bsb_ref, w_in_hbm, w_out_hbm,
             h1_ref, zp_ref, w_in_v, w_out_v, gated_v, sem):
        _load_once([(w_in_hbm, w_in_v), (w_out_hbm, w_out_v)], sem)
        xv = x_ref[...]
        xn = _rms(xv, nmix_ref[...])[0].astype(BF16)
        for j in range(N_SHARD):
            zp_ref[:, j * nw:(j + 1) * nw] = _dot(xn, w_in_v[j])
        z = _gelu(zp_ref[...])
        u = z[:, :D_MODEL]
        vn = _rms(z[:, D_MODEL:], gv_ref[...])[0].astype(BF16)
        for c in range(tm // CHUNK):
            rows = slice(c * CHUNK, (c + 1) * CHUNK)
            for h in range(A_GROUPS):
                cols = slice(h * CHUNK, (h + 1) * CHUNK)
                s = _dot(ws_ref[h], vn[rows, cols]) + bsb_ref[h]
                gated_v[rows, cols] = (u[rows, cols] * s).astype(BF16)
        h1_ref[...] = xv + _dot(gated_v[...], w_out_v[...])

    return pl.pallas_call(
        body, name="mixer_a_fwd", grid=(nt,),
        in_specs=[_row_spec(tm, D_MODEL), _const_spec((1, D_MODEL)), _const_spec((1, D_MODEL)),
                  _const_spec((A_GROUPS, CHUNK, CHUNK)), _const_spec((A_GROUPS, CHUNK, CHUNK)), ANY, ANY],
        out_specs=[_row_spec(tm, D_MODEL), _row_spec(tm, 2 * D_MODEL)],
        out_shape=[jax.ShapeDtypeStruct((T, D_MODEL), F32), jax.ShapeDtypeStruct((T, 2 * D_MODEL), F32)],
        scratch_shapes=[pltpu.VMEM((N_SHARD, D_MODEL, nw), BF16), pltpu.VMEM((D_MODEL, D_MODEL), BF16),
                        pltpu.VMEM((tm, D_MODEL), BF16), pltpu.SemaphoreType.DMA((2,))],
        compiler_params=_params(),
    )(x, nmix, gv, wsm, bsb, w_in, w_out)


def _mixer_a_bwd(dh, x, zp, nmix, gv, wsm, bsb, tril, w_in, w_out, deps=()):
    T = x.shape[0]
    tm = min(256, T)
    nt = T // tm
    nw = 2 * D_MODEL // N_SHARD

    def body(dh_ref, x_ref, zp_ref, nmix_ref, gv_ref, ws_ref, bsb_ref, tril_ref, w_in_hbm, w_out_hbm,
             dx_ref, dwin_ref, dwout_ref, dws_ref, dbs_ref, dgv_ref, dnmix_ref,
             w_in_v, w_out_v, du_v, dvn_v, dbs_v, gated_ref, sem):
        _load_once([(w_in_hbm, w_in_v), (w_out_hbm, w_out_v)], sem)
        _zero_first([dws_ref, dbs_v, dgv_ref, dnmix_ref, dwin_ref, dwout_ref])
        i = pl.program_id(0)
        dhv = dh_ref[...]
        dhb = dhv.astype(BF16)
        xv = x_ref[...]
        xn, xh, r = _rms(xv, nmix_ref[...])
        xnb = xn.astype(BF16)
        zpv = zp_ref[...]
        z = _gelu(zpv)
        u = z[:, :D_MODEL]
        vn_f, vh, rv = _rms(z[:, D_MODEL:], gv_ref[...])
        vn = vn_f.astype(BF16)
        dgated = _dot_nt(dhb, w_out_v[...])
        for c in range(tm // CHUNK):
            rows = slice(c * CHUNK, (c + 1) * CHUNK)
            for h in range(A_GROUPS):
                cols = slice(h * CHUNK, (h + 1) * CHUNK)
                vn_h = vn[rows, cols]
                s = _dot(ws_ref[h], vn_h) + bsb_ref[h]
                dgt = dgated[rows, cols]
                u_h = u[rows, cols]
                gated_ref[rows, cols] = (u_h * s).astype(BF16)
                du_v[rows, cols] = dgt * s
                ds = dgt * u_h
                dsb = ds.astype(BF16)
                dws_ref[h] += _dot_nt(dsb, vn_h)
                dbs_v[h] += ds
                dvn_v[rows, cols] = _dot_tn(ws_ref[h], dsb)
        dwout_ref[...] += _dot_tn(gated_ref[...], dhb)
        dv, dgv = _rms_bwd(dvn_v[...], vh, rv, gv_ref[...])
        dgv_ref[...] += dgv
        dzu = (du_v[...] * _gelu_grad(zpv[:, :D_MODEL])).astype(BF16)
        dzv = (dv * _gelu_grad(zpv[:, D_MODEL:])).astype(BF16)
        dzs = (dzu[:, :nw], dzu[:, nw:], dzv[:, :nw], dzv[:, nw:])
        dxn = jnp.zeros((tm, D_MODEL), F32)
        for j in range(N_SHARD):
            dxn = dxn + _dot_nt(dzs[j], w_in_v[j])
            dwin_ref[j] += _dot_tn(xnb, dzs[j])
        dxx, dn = _rms_bwd(dxn, xh, r, nmix_ref[...])
        dnmix_ref[...] += dn
        dx_ref[...] = dhv + dxx

        @pl.when(i == nt - 1)
        def _():
            for h in range(A_GROUPS):
                dws_ref[h] = dws_ref[h] * tril_ref[...]
                dbs_ref[h] = jnp.broadcast_to(jnp.sum(dbs_v[h], axis=1, keepdims=True), (CHUNK, CHUNK))

    grp = (A_GROUPS, CHUNK, CHUNK)
    body, in_specs, args = _add_deps(
        body, [_row_spec(tm, D_MODEL), _row_spec(tm, D_MODEL), _row_spec(tm, 2 * D_MODEL),
               _const_spec((1, D_MODEL)), _const_spec((1, D_MODEL)), _const_spec(grp), _const_spec(grp),
               _const_spec((CHUNK, CHUNK)), ANY, ANY],
        [dh, x, zp, nmix, gv, wsm, bsb, tril, w_in, w_out], deps)
    return pl.pallas_call(
        body, name="mixer_a_bwd", grid=(nt,), in_specs=in_specs,
        out_specs=[_row_spec(tm, D_MODEL), _const_spec((N_SHARD, D_MODEL, nw)), _const_spec((D_MODEL, D_MODEL)),
                   _const_spec(grp), _const_spec(grp), _const_spec((1, D_MODEL)), _const_spec((1, D_MODEL))],
        out_shape=[jax.ShapeDtypeStruct((T, D_MODEL), F32), jax.ShapeDtypeStruct((N_SHARD, D_MODEL, nw), F32),
                   jax.ShapeDtypeStruct((D_MODEL, D_MODEL), F32),
                   jax.ShapeDtypeStruct(grp, F32), jax.ShapeDtypeStruct(grp, F32),
                   jax.ShapeDtypeStruct((1, D_MODEL), F32), jax.ShapeDtypeStruct((1, D_MODEL), F32)],
        scratch_shapes=[pltpu.VMEM((N_SHARD, D_MODEL, nw), BF16), pltpu.VMEM((D_MODEL, D_MODEL), BF16),
                        pltpu.VMEM((tm, D_MODEL), F32), pltpu.VMEM((tm, D_MODEL), F32),
                        pltpu.VMEM(grp, F32), pltpu.VMEM((tm, D_MODEL), BF16), pltpu.SemaphoreType.DMA((2,))],
        compiler_params=_params(),
    )(*args)


def _load_ffn_weights(w_up_hbm, w_dn_hbm, layer, w_up_v, w_dn_v, sem):
    _load_once([(w_up_hbm, w_up_v), (w_dn_hbm, w_dn_v)], sem)


def _ffn_fwd(h, nffn, cw, cb, w_up, w_dn, layer):
    T = h.shape[0]
    tm = min(256, T)
    nt = T // tm

    def body(h_ref, n_ref, cw_ref, cb_ref, w_up_hbm, w_dn_hbm, out_ref, hh_ref, c_ref,
             w_up_v, w_dn_v, carry_v, sem):
        _load_ffn_weights(w_up_hbm, w_dn_hbm, layer, w_up_v, w_dn_v, sem)
        _zero_first([carry_v])
        xv = h_ref[...]
        xf = _rms(xv, n_ref[...])[0].astype(BF16)
        acc = xv
        for j in range(2):
            cs = []
            for blk in (j, j + 2):
                cols = slice(blk * FF_BLK, (blk + 1) * FF_BLK)
                hh = _dot(xf, w_up_v[blk])
                hh_ref[:, cols] = hh.astype(BF16)
                ext = jnp.concatenate([carry_v[blk], hh], axis=0)
                carry_v[blk] = hh[tm - 8:, :]
                s1 = pltpu.roll(ext, 1, 0)[8:]
                s2 = pltpu.roll(ext, 2, 0)[8:]
                cv = (cb_ref[:, cols] + cw_ref[0:1, cols] * s2 + cw_ref[1:2, cols] * s1
                      + cw_ref[2:3, cols] * hh)
                c_ref[:, cols] = cv.astype(BF16)
                cs.append(cv)
            act = (cs[0] * _sigmoid(cs[0]) * cs[1]).astype(BF16)
            acc = acc + _dot(act, w_dn_v[j * FF_BLK:(j + 1) * FF_BLK, :])
        out_ref[...] = acc

    return pl.pallas_call(
        body, name=f"ffn_fwd{layer}", grid=(nt,),
        in_specs=[_row_spec(tm, D_MODEL), _const_spec((1, D_MODEL)), _const_spec((3, N_FF)),
                  _const_spec((1, N_FF)), ANY, ANY],
        out_specs=[_row_spec(tm, D_MODEL), _row_spec(tm, N_FF), _row_spec(tm, N_FF)],
        out_shape=[jax.ShapeDtypeStruct((T, D_MODEL), F32), jax.ShapeDtypeStruct((T, N_FF), BF16),
                   jax.ShapeDtypeStruct((T, N_FF), BF16)],
        scratch_shapes=[pltpu.VMEM((N_SHARD, D_MODEL, FF_BLK), BF16), pltpu.VMEM((D_FF, D_MODEL), BF16),
                        pltpu.VMEM((N_SHARD, 8, FF_BLK), F32), pltpu.SemaphoreType.DMA((2 * N_SHARD,))],
        compiler_params=_params(),
    )(h, nffn, cw, cb, w_up, w_dn)


def _wgrad(a, b, bn, col_sharded, name, deps=()):
    T, K = a.shape
    N = b.shape[1]
    tt = min(2048, T)
    nn, ntt = N // bn, T // tt
    kr = K // N_SHARD

    def body(a_ref, b_ref, o_ref):
        @pl.when(pl.program_id(1) == 0)
        def _():
            o_ref[...] = jnp.zeros(o_ref.shape, F32)
        d = _dot_tn(a_ref[...].astype(BF16), b_ref[...].astype(BF16))
        if col_sharded:
            o_ref[...] += d
        else:
            for j in range(N_SHARD):
                o_ref[j] += d[j * kr:(j + 1) * kr]

    if col_sharded:
        assert nn == N_SHARD
        out_spec = pl.BlockSpec((None, K, bn), lambda n, t: (n, 0, 0))
        out_shape = jax.ShapeDtypeStruct((N_SHARD, K, bn), F32)
    else:
        out_spec = pl.BlockSpec((N_SHARD, kr, bn), lambda n, t: (0, 0, n))
        out_shape = jax.ShapeDtypeStruct((N_SHARD, kr, N), F32)
    body, in_specs, args = _add_deps(
        body, [pl.BlockSpec((tt, K), lambda n, t: (t, 0)), pl.BlockSpec((tt, bn), lambda n, t: (t, n))],
        [a, b], deps)
    return pl.pallas_call(
        body, name=name, grid=(nn, ntt), in_specs=in_specs, out_specs=out_spec, out_shape=out_shape,
        compiler_params=pltpu.CompilerParams(dimension_semantics=("arbitrary",) * 2, vmem_limit_bytes=VMEM_LIMIT),
    )(*args)


def _ffn_bwd(dh, h, hh, c, nffn, cw, w_up, w_dn, layer, deps=(), between=None):
    T = h.shape[0]
    tm = min(256, T)
    nt = T // tm

    def body(dh_ref, h_ref, hh_ref, c_ref, n_ref, cw_ref, w_up_hbm, w_dn_hbm,
             dhin_ref, act_ref, dhh_ref, xf_ref, dcw_ref, dcb_ref, dn_ref,
             w_up_v, w_dn_v, carry_v, sem):
        _load_ffn_weights(w_up_hbm, w_dn_hbm, layer, w_up_v, w_dn_v, sem)
        _zero_first([carry_v, dcw_ref, dcb_ref, dn_ref])
        dout = dh_ref[...]
        doutb = dout.astype(BF16)
        xf_f, xh, r = _rms(h_ref[...], n_ref[...])
        xf_ref[...] = xf_f.astype(BF16)
        dxf = jnp.zeros((tm, D_MODEL), F32)
        for j in range(2):
            blks = (j, j + 2)
            cg = c_ref[:, j * FF_BLK:(j + 1) * FF_BLK].astype(F32)
            cu = c_ref[:, (j + 2) * FF_BLK:(j + 3) * FF_BLK].astype(F32)
            sg = _sigmoid(cg)
            sil = cg * sg
            act_ref[:, j * FF_BLK:(j + 1) * FF_BLK] = (sil * cu).astype(BF16)
            dact = _dot_nt(doutb, w_dn_v[j * FF_BLK:(j + 1) * FF_BLK, :])
            dcs = (dact * cu * (sg * (1.0 + cg * (1.0 - sg))), dact * sil)
            for blk, dc in zip(blks, dcs):
                cols = slice(blk * FF_BLK, (blk + 1) * FF_BLK)
                hhv = hh_ref[:, cols].astype(F32)
                ext = jnp.concatenate([dc, carry_v[blk]], axis=0)
                carry_v[blk] = dc[:8, :]
                n = tm + 8
                a1 = pltpu.roll(ext, n - 1, 0)[:tm]
                a2 = pltpu.roll(ext, n - 2, 0)[:tm]
                dcb_ref[:, cols] += jnp.sum(dc, axis=0, keepdims=True)
                dcw_ref[0:1, cols] += jnp.sum(a2 * hhv, axis=0, keepdims=True)
                dcw_ref[1:2, cols] += jnp.sum(a1 * hhv, axis=0, keepdims=True)
                dcw_ref[2:3, cols] += jnp.sum(dc * hhv, axis=0, keepdims=True)
                dhh = (cw_ref[2:3, cols] * dc + cw_ref[1:2, cols] * a1 + cw_ref[0:1, cols] * a2).astype(BF16)
                dhh_ref[:, cols] = dhh
                dxf = dxf + _dot_nt(dhh, w_up_v[blk])
        dxx, dn = _rms_bwd(dxf, xh, r, n_ref[...])
        dn_ref[...] += dn
        dhin_ref[...] = dout + dxx

    rev = functools.partial(_row_spec, rev_nt=nt)
    body, in_specs, args = _add_deps(
        body, [rev(tm, D_MODEL), rev(tm, D_MODEL), rev(tm, N_FF), rev(tm, N_FF),
               _const_spec((1, D_MODEL)), _const_spec((3, N_FF)), ANY, ANY],
        [dh, h, hh, c, nffn, cw, w_up, w_dn], deps)
    dhin, act, dhh, xf, dcw, dcb, dn = pl.pallas_call(
        body, name=f"ffn_bwd{layer}", grid=(nt,), in_specs=in_specs,
        out_specs=[rev(tm, D_MODEL), rev(tm, D_FF), rev(tm, N_FF), rev(tm, D_MODEL),
                   _const_spec((3, N_FF)), _const_spec((1, N_FF)), _const_spec((1, D_MODEL))],
        out_shape=[jax.ShapeDtypeStruct((T, D_MODEL), F32), jax.ShapeDtypeStruct((T, D_FF), BF16),
                   jax.ShapeDtypeStruct((T, N_FF), BF16), jax.ShapeDtypeStruct((T, D_MODEL), BF16),
                   jax.ShapeDtypeStruct((3, N_FF), F32), jax.ShapeDtypeStruct((1, N_FF), F32),
                   jax.ShapeDtypeStruct((1, D_MODEL), F32)],
        scratch_shapes=[pltpu.VMEM((N_SHARD, D_MODEL, FF_BLK), BF16), pltpu.VMEM((D_FF, D_MODEL), BF16),
                        pltpu.VMEM((N_SHARD, 8, FF_BLK), F32), pltpu.SemaphoreType.DMA((2 * N_SHARD,))],
        compiler_params=_params(),
    )(*args)
    deps2 = between(dhin) if between is not None else ()
    dwdn = _wgrad(act, dh, D_MODEL // 2, False, f"wgrad_ffn_down{layer}", deps=deps2)
    dwup = _wgrad(xf, dhh, FF_BLK, True, f"wgrad_ffn_up{layer}", deps=deps2)
    return dhin, dwup, dwdn, dcw, dcb, dn


def _load_ple_weights(w_pin_hbm, w_gate_hbm, layer, w_pin_v, w_gate_v, sem, extra=()):
    _load_once([(w_pin_hbm, w_pin_v), (w_gate_hbm, w_gate_v)] + list(extra), sem)


def _ple_fwd_kv(h, p, nple, bg, nkv, w_pin, w_gate, w_kv):
    T = h.shape[0]
    tm = min(512, T)
    nt = T // tm
    pw = D_MODEL // N_SHARD

    def body(h_ref, p_ref, n_ref, bg_ref, nkv_ref, w_pin_hbm, w_gate_hbm, w_kv_hbm,
             out_ref, pe_ref, a_ref, kv_ref, w_pin_v, w_gate_v, w_kv_v, sem):
        _load_ple_weights(w_pin_hbm, w_gate_hbm, 0, w_pin_v, w_gate_v, sem, [(w_kv_hbm, w_kv_v)])
        xv = h_ref[...]
        xg = _rms(xv, n_ref[...])[0].astype(BF16)
        a = _dot(xg, w_gate_v[...]) + bg_ref[...]
        a_ref[...] = a
        pb = p_ref[...].astype(BF16)
        for j in range(N_SHARD):
            pe_ref[:, j * pw:(j + 1) * pw] = _dot(pb, w_pin_v[j])
        hn = xv + pe_ref[...] * _sigmoid(a)
        out_ref[...] = hn
        kvn = _rms(hn, nkv_ref[...])[0].astype(BF16)
        kv_ref[...] = _dot(kvn, w_kv_v[...]).astype(BF16)

    vec = _const_spec((1, D_MODEL))
    return pl.pallas_call(
        body, name="ple_fwd0", grid=(nt,),
        in_specs=[_row_spec(tm, D_MODEL), _row_spec(tm, PLE_DIM), vec, vec, vec, ANY, ANY, ANY],
        out_specs=[_row_spec(tm, D_MODEL), _row_spec(tm, D_MODEL), _row_spec(tm, D_MODEL),
                   _row_spec(tm, 2 * KV_DIM)],
        out_shape=[jax.ShapeDtypeStruct((T, D_MODEL), F32), jax.ShapeDtypeStruct((T, D_MODEL), F32),
                   jax.ShapeDtypeStruct((T, D_MODEL), F32), jax.ShapeDtypeStruct((T, 2 * KV_DIM), BF16)],
        scratch_shapes=[pltpu.VMEM((N_SHARD, PLE_DIM, pw), BF16), pltpu.VMEM((D_MODEL, D_MODEL), BF16),
                        pltpu.VMEM((D_MODEL, 2 * KV_DIM), BF16), pltpu.SemaphoreType.DMA((2 * N_SHARD + 1,))],
        compiler_params=_params(),
    )(h, p, nple, bg, nkv, w_pin, w_gate, w_kv)


def _ple_fwd_final(h, p, tgt, nple, bg, nfin, w_pin, w_gate):
    T = h.shape[0]
    tm = min(512, T)
    nt = T // tm
    pw = D_MODEL // N_SHARD

    def body(h_ref, p_ref, t_ref, n_ref, bg_ref, nf_ref, w_pin_hbm, w_gate_hbm,
             dh_ref, pe_ref, a_ref, loss_ref, dnf_ref, w_pin_v, w_gate_v, sem):
        _load_ple_weights(w_pin_hbm, w_gate_hbm, 1, w_pin_v, w_gate_v, sem)
        _zero_first([loss_ref, dnf_ref])
        xv = h_ref[...]
        xg = _rms(xv, n_ref[...])[0].astype(BF16)
        a = _dot(xg, w_gate_v[...]) + bg_ref[...]
        a_ref[...] = a
        pb = p_ref[...].astype(BF16)
        for j in range(N_SHARD):
            pe_ref[:, j * pw:(j + 1) * pw] = _dot(pb, w_pin_v[j])
        hn = xv + pe_ref[...] * _sigmoid(a)
        y, yh, r = _rms(hn, nf_ref[...])
        diff = y - t_ref[...]
        loss_ref[...] += 0.5 * jnp.sum(jnp.mean(diff * diff, axis=-1, keepdims=True))
        dy = diff * (1.0 / D_MODEL)
        dhn, dnf = _rms_bwd(dy, yh, r, nf_ref[...])
        dnf_ref[...] += dnf
        dh_ref[...] = dhn

    vec = _const_spec((1, D_MODEL))
    return pl.pallas_call(
        body, name="ple_fwd1", grid=(nt,),
        in_specs=[_row_spec(tm, D_MODEL), _row_spec(tm, PLE_DIM), _row_spec(tm, D_MODEL), vec, vec, vec, ANY, ANY],
        out_specs=[_row_spec(tm, D_MODEL), _row_spec(tm, D_MODEL), _row_spec(tm, D_MODEL),
                   _const_spec((8, 128)), vec],
        out_shape=[jax.ShapeDtypeStruct((T, D_MODEL), F32), jax.ShapeDtypeStruct((T, D_MODEL), F32),
                   jax.ShapeDtypeStruct((T, D_MODEL), F32), jax.ShapeDtypeStruct((8, 128), F32),
                   jax.ShapeDtypeStruct((1, D_MODEL), F32)],
        scratch_shapes=[pltpu.VMEM((N_SHARD, PLE_DIM, pw), BF16), pltpu.VMEM((D_MODEL, D_MODEL), BF16),
                        pltpu.SemaphoreType.DMA((2 * N_SHARD,))],
        compiler_params=_params(),
    )(h, p, tgt, nple, bg, nfin, w_pin, w_gate)


def _ple_bwd(dh, hb, pe, a, p, nple, w_gate, layer, kv_args=None):
    T = hb.shape[0]
    tm = min(512, T)
    nt = T // tm
    with_kv = kv_args is not None
    pw = D_MODEL // N_SHARD

    def body(*refs):
        if with_kv:
            (dh_ref, hb_ref, pe_ref, a_ref, p_ref, n_ref, w_gate_hbm, hc_ref, dkv_ref, nkv_ref, w_kv_hbm,
             dhb_ref, dwpin_ref, dwgate_ref, dbg_ref, dn_ref, dwkv_ref, dnkv_ref,
             w_gate_v, w_kv_v, sem) = refs
        else:
            (dh_ref, hb_ref, pe_ref, a_ref, p_ref, n_ref, w_gate_hbm,
             dhb_ref, dwpin_ref, dwgate_ref, dbg_ref, dn_ref, w_gate_v, sem) = refs
        pairs = [(w_gate_hbm, w_gate_v)]
        if with_kv:
            pairs.append((w_kv_hbm, w_kv_v))
        _load_once(pairs, sem)
        _zero_first([dwpin_ref, dwgate_ref, dbg_ref, dn_ref] + ([dwkv_ref, dnkv_ref] if with_kv else []))
        do = dh_ref[...]
        if with_kv:
            dkvb = dkv_ref[...].astype(BF16)
            dkvn = _dot_nt(dkvb, w_kv_v[...])
            kvn, kh, kr = _rms(hc_ref[...], nkv_ref[...])
            dwkv_ref[...] += _dot_tn(kvn.astype(BF16), dkvb)
            dk, dnkv = _rms_bwd(dkvn, kh, kr, nkv_ref[...])
            dnkv_ref[...] += dnkv
            do = do + dk
        gate = _sigmoid(a_ref[...])
        dpe = (do * gate).astype(BF16)
        pb = p_ref[...].astype(BF16)
        for j in range(N_SHARD):
            dwpin_ref[j] += _dot_tn(pb, dpe[:, j * pw:(j + 1) * pw])
        da = do * pe_ref[...] * (gate * (1.0 - gate))
        dab = da.astype(BF16)
        dbg_ref[...] += jnp.sum(da, axis=0, keepdims=True)
        dxg = _dot_nt(dab, w_gate_v[...])
        xg, xh, r = _rms(hb_ref[...], n_ref[...])
        dwgate_ref[...] += _dot_tn(xg.astype(BF16), dab)
        dxx, dn = _rms_bwd(dxg, xh, r, n_ref[...])
        dn_ref[...] += dn
        dhb_ref[...] = do + dxx

    vec = _const_spec((1, D_MODEL))
    row = _row_spec(tm, D_MODEL)
    in_specs = [row, row, row, row, _row_spec(tm, PLE_DIM), vec, ANY]
    args = [dh, hb, pe, a, p, nple, w_gate]
    out_specs = [row, _const_spec((N_SHARD, PLE_DIM, pw)), _const_spec((D_MODEL, D_MODEL)), vec, vec]
    out_shape = [jax.ShapeDtypeStruct((T, D_MODEL), F32), jax.ShapeDtypeStruct((N_SHARD, PLE_DIM, pw), F32),
                 jax.ShapeDtypeStruct((D_MODEL, D_MODEL), F32),
                 jax.ShapeDtypeStruct((1, D_MODEL), F32), jax.ShapeDtypeStruct((1, D_MODEL), F32)]
    scratch = [pltpu.VMEM((D_MODEL, D_MODEL), BF16)]
    if with_kv:
        hc, dkv, nkv, w_kv = kv_args
        in_specs += [row, _row_spec(tm, 2 * KV_DIM), vec, ANY]
        args += [hc, dkv, nkv, w_kv]
        out_specs += [_const_spec((D_MODEL, 2 * KV_DIM)), vec]
        out_shape += [jax.ShapeDtypeStruct((D_MODEL, 2 * KV_DIM), F32), jax.ShapeDtypeStruct((1, D_MODEL), F32)]
        scratch.append(pltpu.VMEM((D_MODEL, 2 * KV_DIM), BF16))
    scratch.append(pltpu.SemaphoreType.DMA((N_SHARD + 1,)))
    return pl.pallas_call(
        body, name=f"ple_bwd{layer}", grid=(nt,), in_specs=in_specs, out_specs=out_specs,
        out_shape=out_shape, scratch_shapes=scratch, compiler_params=_params(),
    )(*args)


GROUP_ROWS = GQA_GROUP * BLOCK


def _stack_heads(x, kh):
    return jnp.concatenate([x[:, (kh * GQA_GROUP + g) * HEAD_DIM:(kh * GQA_GROUP + g + 1) * HEAD_DIM]
                            for g in range(GQA_GROUP)], axis=0)


def _attn_fwd(h, nmix, kv, sinks, w_q, w_o):
    T = h.shape[0]
    tm = min(512, T)
    nt = T // tm
    nb = tm // BLOCK

    def body(h_ref, n_ref, kv_ref, kvp_ref, sink_ref, w_q_hbm, w_o_hbm,
             out_ref, q_ref, ao_ref, p_ref, w_q_v, w_o_v, kvs_v, sem):
        _load_once([(w_q_hbm, w_q_v), (w_o_hbm, w_o_v)], sem)
        ti = pl.program_id(0)
        xv = h_ref[...]
        xn = _rms(xv, n_ref[...])[0].astype(BF16)
        q_ref[...] = (_dot(xn, w_q_v[...]) * (HEAD_DIM ** -0.5)).astype(BF16)
        kvs_v[0:BLOCK, :] = kvp_ref[...]
        kvs_v[BLOCK:, :] = kv_ref[...]
        ii = lax.broadcasted_iota(jnp.int32, (BLOCK, 2 * BLOCK), 0)
        jj = lax.broadcasted_iota(jnp.int32, (BLOCK, 2 * BLOCK), 1)
        dist = ii + BLOCK - jj
        inband = (dist >= 0) & (dist < BLOCK)
        distf = dist.astype(F32)

        def blk_body(b, carry):
            r0 = pl.multiple_of(b * BLOCK, BLOCK)
            valid = inband & ((jj >= BLOCK) | jnp.logical_not(jnp.logical_and(ti == 0, b == 0)))
            qb = q_ref[pl.ds(r0, BLOCK), :]
            band = kvs_v[pl.ds(r0, 2 * BLOCK), :]
            outs = []
            for hq in range(N_Q_HEADS):
                kh, g = divmod(hq, GQA_GROUP)
                k_h = band[:, kh * HEAD_DIM:(kh + 1) * HEAD_DIM]
                v_h = band[:, KV_DIM + kh * HEAD_DIM:KV_DIM + (kh + 1) * HEAD_DIM]
                s = _dot_nt(qb[:, hq * HEAD_DIM:(hq + 1) * HEAD_DIM], k_h) - _SLOPES[hq] * distf
                s = jnp.where(valid, s, NEG)
                sink = sink_ref[hq]
                m = jnp.maximum(jnp.max(s, axis=1, keepdims=True), sink)
                e = jnp.exp(s - m)
                den = jnp.sum(e, axis=1, keepdims=True) + jnp.exp(sink - m)
                pb = (e / den).astype(BF16)
                p_ref[b, kh, g * BLOCK:(g + 1) * BLOCK, :] = pb
                outs.append(_dot(pb, v_h))
            ao_ref[pl.ds(r0, BLOCK), :] = jnp.concatenate(outs, axis=1).astype(BF16)
            return carry

        lax.fori_loop(0, nb, blk_body, 0)
        out_ref[...] = xv + _dot(ao_ref[...], w_o_v[...])

    row = _row_spec(tm, D_MODEL)
    prev_spec = pl.BlockSpec((BLOCK, 2 * KV_DIM), lambda i: (jnp.maximum(i * nb - 1, 0), 0))
    return pl.pallas_call(
        body, name="attn_fwd", grid=(nt,),
        in_specs=[row, _const_spec((1, D_MODEL)), _row_spec(tm, 2 * KV_DIM), prev_spec, SMEM, ANY, ANY],
        out_specs=[row, row, row, pl.BlockSpec((nb, N_KV_HEADS, GROUP_ROWS, 2 * BLOCK), lambda i: (i, 0, 0, 0))],
        out_shape=[jax.ShapeDtypeStruct((T, D_MODEL), F32), jax.ShapeDtypeStruct((T, D_MODEL), BF16),
                   jax.ShapeDtypeStruct((T, D_MODEL), BF16),
                   jax.ShapeDtypeStruct((T // BLOCK, N_KV_HEADS, GROUP_ROWS, 2 * BLOCK), BF16)],
        scratch_shapes=[pltpu.VMEM((D_MODEL, D_MODEL), BF16), pltpu.VMEM((D_MODEL, D_MODEL), BF16),
                        pltpu.VMEM((tm + BLOCK, 2 * KV_DIM), BF16), pltpu.SemaphoreType.DMA((2,))],
        compiler_params=_params(),
    )(h, nmix, kv, kv, sinks, w_q, w_o)


def _attn_bwd(dh, h, q, kv, ao, p, nmix, w_q, w_o):
    T = h.shape[0]
    tm = min(512, T)
    nt = T // tm
    nb = tm // BLOCK

    def body(dh_ref, h_ref, q_ref, kv_ref, kvp_ref, ao_ref, p_ref, n_ref, w_q_hbm, w_o_hbm,
             dhin_ref, dwq_ref, dwo_ref, dkv_ref, dsink_ref, dn_ref,
             w_q_v, w_o_v, kvs_v, dao_v, dq_v, dkv_v, carry_v, sem):
        _load_once([(w_q_hbm, w_q_v), (w_o_hbm, w_o_v)], sem)
        _zero_first([carry_v, dsink_ref, dn_ref, dwq_ref, dwo_ref])
        dout = dh_ref[...]
        doutb = dout.astype(BF16)
        dao_v[...] = _dot_nt(doutb, w_o_v[...])
        dwo_ref[...] += _dot_tn(ao_ref[...], doutb)
        kvs_v[0:BLOCK, :] = kvp_ref[...]
        kvs_v[BLOCK:, :] = kv_ref[...]
        dkv_v[0:tm, :] = jnp.zeros((tm, 2 * KV_DIM), F32)
        dkv_v[tm:, :] = carry_v[...]
        lane8 = lax.broadcasted_iota(jnp.int32, (8, 128), 1)

        def blk_body(b, dsk):
            r0 = pl.multiple_of(b * BLOCK, BLOCK)
            qb = q_ref[pl.ds(r0, BLOCK), :]
            band = kvs_v[pl.ds(r0, 2 * BLOCK), :]
            aob = ao_ref[pl.ds(r0, BLOCK), :].astype(F32)
            daob = dao_v[pl.ds(r0, BLOCK), :]
            dqs = []
            dks = []
            dvs = []
            for kh in range(N_KV_HEADS):
                k_h = band[:, kh * HEAD_DIM:(kh + 1) * HEAD_DIM]
                v_h = band[:, KV_DIM + kh * HEAD_DIM:KV_DIM + (kh + 1) * HEAD_DIM]
                q_g = _stack_heads(qb, kh)
                dao_g = _stack_heads(daob, kh)
                prb = p_ref[b, kh]
                pr = prb.astype(F32)
                dd = jnp.sum(dao_g * _stack_heads(aob, kh), axis=1, keepdims=True)
                dao_gb = dao_g.astype(BF16)
                dp = _dot_nt(dao_gb, v_h)
                pd = pr * dd
                dsb = (pr * dp - pd).astype(BF16)
                dq_g = _dot(dsb, k_h) * (HEAD_DIM ** -0.5)
                dks.append(_dot_tn(dsb, q_g))
                dvs.append(_dot_tn(prb, dao_gb))
                for g in range(GQA_GROUP):
                    rows = slice(g * BLOCK, (g + 1) * BLOCK)
                    dqs.append(dq_g[rows])
                    dsv = jnp.sum(pd[rows]) - jnp.sum(dd[rows])
                    dsk = dsk + jnp.where(lane8 == kh * GQA_GROUP + g, dsv, 0.0)
            dq_v[pl.ds(r0, BLOCK), :] = jnp.concatenate(dqs, axis=1)
            dkv_v[pl.ds(r0, 2 * BLOCK), :] += jnp.concatenate(dks + dvs, axis=1)
            return dsk

        dsk = lax.fori_loop(0, nb, blk_body, jnp.zeros((8, 128), F32))
        dsink_ref[...] += dsk
        dqb = dq_v[...].astype(BF16)
        dxn = _dot_nt(dqb, w_q_v[...])
        xn, xh, r = _rms(h_ref[...], n_ref[...])
        dwq_ref[...] += _dot_tn(xn.astype(BF16), dqb)
        dxx, dn = _rms_bwd(dxn, xh, r, n_ref[...])
        dn_ref[...] += dn
        dhin_ref[...] = dout + dxx
        dkv_ref[...] = dkv_v[BLOCK:, :]
        carry_v[...] = dkv_v[0:BLOCK, :]

    rev = functools.partial(_row_spec, rev_nt=nt)
    row = rev(tm, D_MODEL)
    prev_spec = pl.BlockSpec((BLOCK, 2 * KV_DIM), lambda i: (jnp.maximum((nt - 1 - i) * nb - 1, 0), 0))
    return pl.pallas_call(
        body, name="attn_bwd", grid=(nt,),
        in_specs=[row, row, row, rev(tm, 2 * KV_DIM), prev_spec, row,
                  pl.BlockSpec((nb, N_KV_HEADS, GROUP_ROWS, 2 * BLOCK), lambda i: (nt - 1 - i, 0, 0, 0)),
                  _const_spec((1, D_MODEL)), ANY, ANY],
        out_specs=[row, _const_spec((D_MODEL, D_MODEL)), _const_spec((D_MODEL, D_MODEL)), rev(tm, 2 * KV_DIM),
                   _const_spec((8, 128)), _const_spec((1, D_MODEL))],
        out_shape=[jax.ShapeDtypeStruct((T, D_MODEL), F32), jax.ShapeDtypeStruct((D_MODEL, D_MODEL), F32),
                   jax.ShapeDtypeStruct((D_MODEL, D_MODEL), F32), jax.ShapeDtypeStruct((T, 2 * KV_DIM), F32),
                   jax.ShapeDtypeStruct((8, 128), F32), jax.ShapeDtypeStruct((1, D_MODEL), F32)],
        scratch_shapes=[pltpu.VMEM((D_MODEL, D_MODEL), BF16), pltpu.VMEM((D_MODEL, D_MODEL), BF16),
                        pltpu.VMEM((tm + BLOCK, 2 * KV_DIM), BF16), pltpu.VMEM((tm, D_MODEL), F32),
                        pltpu.VMEM((tm, D_MODEL), F32), pltpu.VMEM((tm + BLOCK, 2 * KV_DIM), F32),
                        pltpu.VMEM((BLOCK, 2 * KV_DIM), F32), pltpu.SemaphoreType.DMA((2,))],
        compiler_params=_params(),
    )(dh, h, q, kv, kv, ao, p, nmix, w_q, w_o)


def _mesh_pos():
    return lax.axis_index("x"), lax.axis_index("y"), lax.axis_index("c")


def _other_chips(x, y):
    return [(1 - x, y), (x, 1 - y), (1 - x, 1 - y)]


HBM_SPEC = pl.BlockSpec(memory_space=pltpu.HBM)
SEM_SPEC = pl.BlockSpec(memory_space=pltpu.SEMAPHORE)


def _split_call(name, bufs, waits=(), starts=(), after=()):
    n, nw, ns, na = len(bufs), len(waits), len(starts), len(after)

    def body(*refs):
        brefs = refs[:n]
        wsems = [(refs[n + 2 * k], refs[n + 2 * k + 1]) for k in range(nw)]
        o = n + 2 * nw + na
        ssems = [(refs[o + 2 * k], refs[o + 2 * k + 1]) for k in range(ns)]
        for (ss, rs), (_, _, fn) in zip(wsems, waits):
            for sending, arriving in fn(brefs, ss, rs):
                sending.wait_send()
                arriving.wait_recv()
        for (ss, rs), (_, fn) in zip(ssems, starts):
            for sending, _ in fn(brefs, ss, rs):
                sending.start()
        if ns:
            token = refs[o + 2 * ns + n]
            token[...] = jnp.zeros(token.shape, token.dtype)

    out_shape, out_specs = [], []
    for cnt, _ in starts:
        out_shape += [pltpu.SemaphoreType.DMA((cnt,)), pltpu.SemaphoreType.DMA((cnt,))]
        out_specs += [SEM_SPEC, SEM_SPEC]
    out_shape += [pltpu.HBM(b.shape, b.dtype) for b in bufs]
    out_specs += [HBM_SPEC] * n
    if ns:
        out_shape.append(jax.ShapeDtypeStruct((8, 128), F32))
        out_specs.append(pl.BlockSpec(memory_space=pltpu.VMEM))
    args = [pltpu.with_memory_space_constraint(b, pltpu.HBM) for b in bufs]
    for ss, rs, _ in waits:
        args += [ss, rs]
    args += list(after)
    res = pl.pallas_call(
        body, name=name, out_shape=tuple(out_shape),
        in_specs=[HBM_SPEC] * n + [SEM_SPEC] * (2 * nw) + [ANY] * na, out_specs=tuple(out_specs),
        input_output_aliases={i: 2 * ns + i for i in range(n)},
        compiler_params=pltpu.CompilerParams(has_side_effects=pltpu.SideEffectType.DATAFLOW_SIDE_EFFECTING),
    )(*args)
    sems = [(res[2 * k], res[2 * k + 1]) for k in range(ns)]
    return list(res[2 * ns:2 * ns + n]), sems, (res[2 * ns + n] if ns else None)


def _cast_place(items, name):
    n = len(items)
    mats = [a.shape[-2:] for a, _, _ in items]

    def body(*refs):
        ins, outs, scr, sem = refs[:n], refs[n:2 * n], refs[2 * n:3 * n], refs[3 * n]
        x, y, _ = _mesh_pos()
        cps = []
        for t in range(n):
            scr[t][...] = ins[t][...].astype(scr[t].dtype)
            cp = pltpu.make_async_copy(scr[t], outs[t].at[2 * x + y], sem.at[t])
            cp.start()
            cps.append(cp)
        for cp in cps:
            cp.wait()

    def spec(idx, shape):
        return pl.BlockSpec((None,) * len(idx) + tuple(shape), lambda i: tuple(idx) + (0, 0))

    return pl.pallas_call(
        body, name=name, grid=(1,),
        in_specs=[spec(idx, mat) for (_, idx, _), mat in zip(items, mats)], out_specs=[ANY] * n,
        out_shape=[jax.ShapeDtypeStruct((N_SHARD,) + tuple(mat), dt) for (_, _, dt), mat in zip(items, mats)],
        scratch_shapes=[pltpu.VMEM(tuple(mat), dt) for (_, _, dt), mat in zip(items, mats)]
        + [pltpu.SemaphoreType.DMA((n,))],
        compiler_params=_params(),
    )(*[a for a, _, _ in items])


def _gather_ici(idx):
    def fn(bufs, ss, rs):
        x, y, c = _mesh_pos()
        pairs = []
        for k, t in enumerate(idx):
            half = bufs[t].shape[1] // 2
            mine = bufs[t].at[2 * x + y, pl.ds(c * half, half), :]
            for j, (cx, cy) in enumerate(_other_chips(x, y)):
                theirs = bufs[t].at[2 * cx + cy, pl.ds(c * half, half), :]
                sem = dict(send_sem=ss.at[3 * k + j], recv_sem=rs.at[3 * k + j],
                           device_id=(cx, cy, c), device_id_type=MESH)
                pairs.append((pltpu.make_async_remote_copy(src_ref=mine, dst_ref=mine, **sem),
                              pltpu.make_async_remote_copy(src_ref=mine, dst_ref=theirs, **sem)))
        return pairs
    return fn


def _gather_d2d(idx):
    def fn(bufs, ss, rs):
        x, y, c = _mesh_pos()
        pairs = []
        for k, t in enumerate(idx):
            half = bufs[t].shape[1] // 2
            for j, (cx, cy) in enumerate(_other_chips(x, y)):
                got = bufs[t].at[2 * cx + cy, pl.ds(c * half, half), :]
                theirs = bufs[t].at[2 * cx + cy, pl.ds((1 - c) * half, half), :]
                sem = dict(send_sem=ss.at[3 * k + j], recv_sem=rs.at[3 * k + j],
                           device_id=(x, y, 1 - c), device_id_type=MESH)
                pairs.append((pltpu.make_async_remote_copy(src_ref=got, dst_ref=got, **sem),
                              pltpu.make_async_remote_copy(src_ref=got, dst_ref=theirs, **sem)))
        return pairs
    return fn


def _alloc(shapes, name):
    def body(*refs):
        pass

    return pl.pallas_call(body, name=name, out_specs=[ANY] * len(shapes),
                          out_shape=[jax.ShapeDtypeStruct(s, d) for s, d in shapes])()


def _send_to_sibling(n):
    def fn(bufs, ss, rs):
        x, y, c = _mesh_pos()
        pairs = []
        for t in range(n):
            src = bufs[t]
            if len(src.shape) == 3:
                half = src.shape[1] // 2
                src = src.at[:, pl.ds((1 - c) * half, half), :]
            cp = pltpu.make_async_remote_copy(src_ref=src, dst_ref=bufs[n + t], send_sem=ss.at[t],
                                              recv_sem=rs.at[t], device_id=(x, y, 1 - c), device_id_type=MESH)
            pairs.append((cp, cp))
        return pairs
    return fn


def _send_to_chips(n):
    def fn(bufs, ss, rs):
        x, y, c = _mesh_pos()
        pairs = []
        for j, (cx, cy) in enumerate(_other_chips(x, y)):
            for t in range(n):
                src = bufs[t].at[j] if len(bufs[t].shape) == 3 else bufs[t]
                cp = pltpu.make_async_remote_copy(src_ref=src, dst_ref=bufs[n + t].at[j], send_sem=ss.at[3 * t + j],
                                                  recv_sem=rs.at[3 * t + j], device_id=(cx, cy, c),
                                                  device_id_type=MESH)
                pairs.append((cp, cp))
        return pairs
    return fn


class _Exchange:
    def __init__(self, name, srcs, land_shapes, fn, n_sems):
        self.name, self.fn = name, fn
        lands = _alloc(land_shapes, name + "_alloc")
        self.n = len(srcs)
        self.bufs, sems, self.token = _split_call(name + "_start", list(srcs) + list(lands),
                                                  starts=[(n_sems, fn)])
        self.sems = sems[0]

    def finish(self, after=()):
        bufs, _, _ = _split_call(self.name + "_wait", self.bufs, waits=[(*self.sems, self.fn)], after=after)
        return bufs[:self.n], bufs[self.n:]


def _row_block(rows, cols, mult=8, limit=3 * 512 * 1024, itemsize=4):
    best = None
    for br in range(mult, rows + 1, mult):
        if rows % br == 0 and br * cols * itemsize <= limit:
            best = br
    assert best is not None, (rows, cols)
    return best


_GROUP_BLOCK_BYTES = 512 * 1024


def _group_plan(ss):
    plan = []
    for s in ss:
        half, cols = s.shape[-2:]
        br = _row_block(half, cols, mult=16, limit=_GROUP_BLOCK_BYTES)
        plan.append((br, half // br))
    return plan, max(nr for _, nr in plan)


def _chip_partial(gs, ss, ids, name):
    n = len(gs)
    plan, steps = _group_plan(ss)

    def body(ids_ref, *refs):
        for t in range(n):
            refs[2 * n + t][...] = (refs[t][...] + refs[n + t][...]).astype(BF16)

    g_specs, s_specs, o_specs = [], [], []
    for (br, nr), s in zip(plan, ss):
        blk = (None, br, s.shape[2])
        g_specs.append(pl.BlockSpec(
            blk, lambda j, r, ids_ref, nr=nr: (ids_ref[2 + j], ids_ref[0] * nr + jnp.minimum(r, nr - 1), 0)))
        s_specs.append(pl.BlockSpec(blk, lambda j, r, ids_ref, nr=nr: (ids_ref[2 + j], jnp.minimum(r, nr - 1), 0)))
        o_specs.append(pl.BlockSpec(blk, lambda j, r, ids_ref, nr=nr: (j, jnp.minimum(r, nr - 1), 0)))
    return pl.pallas_call(
        body, name=name,
        grid_spec=pltpu.PrefetchScalarGridSpec(num_scalar_prefetch=1, grid=(3, steps),
                                               in_specs=g_specs + s_specs, out_specs=o_specs),
        out_shape=[jax.ShapeDtypeStruct((3,) + s.shape[1:], BF16) for s in ss],
        compiler_params=pltpu.CompilerParams(dimension_semantics=("arbitrary", "arbitrary"),
                                             vmem_limit_bytes=VMEM_LIMIT),
    )(ids, *gs, *ss)


def _chip_sum(gs, ss, qs, ids, name):
    n = len(gs)
    plan, steps = _group_plan(ss)

    def body(ids_ref, *refs):
        for t in range(n):
            q_ref = refs[2 * n + t]
            own = refs[t][...] + refs[n + t][...]
            refs[3 * n + t][...] = (own + q_ref[2].astype(F32)) + (q_ref[0].astype(F32) + q_ref[1].astype(F32))

    g_specs, s_specs, q_specs, o_specs = [], [], [], []
    for (br, nr), s in zip(plan, ss):
        cols = s.shape[2]
        g_specs.append(pl.BlockSpec(
            (None, br, cols), lambda r, ids_ref, nr=nr: (ids_ref[1], ids_ref[0] * nr + jnp.minimum(r, nr - 1), 0)))
        s_specs.append(pl.BlockSpec((None, br, cols), lambda r, ids_ref, nr=nr: (ids_ref[1], jnp.minimum(r, nr - 1), 0)))
        q_specs.append(pl.BlockSpec((3, br, cols), lambda r, ids_ref, nr=nr: (0, jnp.minimum(r, nr - 1), 0)))
        o_specs.append(pl.BlockSpec((br, cols), lambda r, ids_ref, nr=nr: (jnp.minimum(r, nr - 1), 0)))
    return pl.pallas_call(
        body, name=name,
        grid_spec=pltpu.PrefetchScalarGridSpec(num_scalar_prefetch=1, grid=(steps,),
                                               in_specs=g_specs + s_specs + q_specs, out_specs=o_specs),
        out_shape=[jax.ShapeDtypeStruct(s.shape[1:], F32) for s in ss],
        compiler_params=pltpu.CompilerParams(dimension_semantics=("arbitrary",), vmem_limit_bytes=VMEM_LIMIT),
    )(ids, *gs, *ss, *qs)


def _adamw_math(w, g, m, v):
    mn = ADAM_B1 * m + (1.0 - ADAM_B1) * g
    vn = ADAM_B2 * v + (1.0 - ADAM_B2) * (g * g)
    m_hat = mn / (1.0 - ADAM_B1 ** ADAM_STEP)
    v_hat = vn / (1.0 - ADAM_B2 ** ADAM_STEP)
    return -ADAM_LR * (m_hat / (jnp.sqrt(v_hat) + ADAM_EPS) + ADAM_WD * w), mn, vn


def _adamw_halves(w, own, sib, m, v, ids, name, layer=0, n_layers=1, stacked=None):
    C = w.shape[1]
    R = w.shape[0] // n_layers
    half = R // 2
    br = _row_block(half, C)
    nh = half // br
    base = layer * 2 * nh

    def body(ids_ref, w_ref, own_ref, sib_ref, m_ref, v_ref, *rest):
        g_ref, d_ref, mo_ref, vo_ref = rest[-4:]
        is_own = (pl.program_id(0) // nh) == ids_ref[0]
        g = jnp.where(is_own, own_ref[...], sib_ref[...])
        g_ref[...] = g
        d_ref[...], mo_ref[...], vo_ref[...] = _adamw_math(w_ref[...], g, m_ref[...], v_ref[...])

    full = pl.BlockSpec((br, C), lambda r, ids_ref: (base + r, 0))
    own_spec = pl.BlockSpec((br, C), lambda r, ids_ref: (jnp.clip(r - ids_ref[0] * nh, 0, nh - 1), 0))
    sib_spec = pl.BlockSpec((br, C), lambda r, ids_ref: (jnp.clip(r - (1 - ids_ref[0]) * nh, 0, nh - 1), 0))
    in_specs = [full, own_spec, sib_spec, full, full]
    args = [ids, w, own, sib, m, v]
    aliases = {}
    if stacked is not None:
        in_specs += [ANY] * 4
        args += list(stacked)
        aliases = {6 + k: k for k in range(4)}
    return pl.pallas_call(
        body, name=name,
        grid_spec=pltpu.PrefetchScalarGridSpec(
            num_scalar_prefetch=1, grid=(2 * nh,), in_specs=in_specs, out_specs=[full] * 4),
        out_shape=[jax.ShapeDtypeStruct(w.shape, F32)] * 4, input_output_aliases=aliases,
        compiler_params=_params(),
    )(*args)


_PACK_UNIT = 1024


def _pack(arrs):
    flat = []
    for a in arrs:
        f = a.reshape(-1).astype(F32)
        pad = (-f.shape[0]) % _PACK_UNIT
        if pad:
            f = jnp.concatenate([f, jnp.zeros((pad,), F32)])
        flat.append(f)
    return jnp.concatenate(flat).reshape(-1, 128)


def kernel(x, p, norm_mix, norm_ffn, norm_ple, norm_kv, norm_final, a_w_in, a_norm_v, a_w_s, a_b_s, a_w_out, w_kv, b_w_q, b_sinks, b_w_o, f_w_up, f_conv_w, f_conv_b, f_w_down, ple_w_in, ple_w_gate, ple_b_gate, loss_target, m_norm_mix, m_norm_ffn, m_norm_ple, m_norm_kv, m_norm_final, m_a_w_in, m_a_norm_v, m_a_w_s, m_a_b_s, m_a_w_out, m_w_kv, m_b_w_q, m_b_sinks, m_b_w_o, m_f_w_up, m_f_conv_w, m_f_conv_b, m_f_w_down, m_ple_w_in, m_ple_w_gate, m_ple_b_gate, v_norm_mix, v_norm_ffn, v_norm_ple, v_norm_kv, v_norm_final, v_a_w_in, v_a_norm_v, v_a_w_s, v_a_b_s, v_a_w_out, v_w_kv, v_b_w_q, v_b_sinks, v_b_w_o, v_f_w_up, v_f_conv_w, v_f_conv_b, v_f_w_down, v_ple_w_in, v_ple_w_gate, v_ple_b_gate):
    given = dict(locals())

    small_shard = _pack([a_norm_v, f_conv_w])
    pad_rows = (-small_shard.shape[0]) % 16
    if pad_rows:
        small_shard = jnp.concatenate([small_shard, jnp.zeros((pad_rows, 128), F32)])
    groups = [
        [(a_w_in, (0,), BF16), (a_w_out, (0,), BF16), (small_shard, (), F32)],
        [(f_w_up, (0,), BF16), (f_w_down, (0,), BF16)],
        [(ple_w_in, (0,), BF16), (ple_w_gate, (0,), BF16), (w_kv, (), BF16), (b_w_q, (0,), BF16),
         (b_w_o, (0,), BF16), (f_w_up, (1,), BF16), (f_w_down, (1,), BF16), (ple_w_in, (1,), BF16),
         (ple_w_gate, (1,), BF16)],
    ]
    lands, spans, start = [], [], 0
    for gi, items in enumerate(groups):
        lands += _cast_place(items, f"cast_place_g{gi}")
        spans.append(list(range(start, start + len(items))))
        start += len(items)
    lands, ici_sems, _ = _split_call("gather_start", lands,
                                     starts=[(3 * len(sp), _gather_ici(sp)) for sp in spans])

    def finish_group(gi, after):
        sp = spans[gi]
        local = list(range(len(sp)))
        bufs = [lands[t] for t in sp]
        bufs, d2d_sems, _ = _split_call(f"gather_pass_g{gi}", bufs, waits=[(*ici_sems[gi], _gather_ici(local))],
                                        starts=[(3 * len(sp), _gather_d2d(local))], after=after)
        bufs, _, _ = _split_call(f"gather_done_g{gi}", bufs, waits=[(*d2d_sems[0], _gather_d2d(local))])
        return bufs

    def stage0():
        b_in, b_out, b_small = finish_group(0, ())
        small_full = b_small.reshape(N_SHARD, -1)
        gv_full = small_full[:, :256].reshape(1, D_MODEL)
        cw_full = small_full[:, _PACK_UNIT:_PACK_UNIT + 2 * 3 * FF_BLK].reshape(N_SHARD, 2, 3, FF_BLK)
        cw_full = jnp.transpose(cw_full, (1, 2, 0, 3)).reshape(2, 3, N_FF)
        return gv_full, cw_full, b_in, b_out.reshape(D_MODEL, D_MODEL)

    def stage1(after):
        b_up, b_dn = finish_group(1, after)
        return b_up, b_dn.reshape(D_FF, D_MODEL)

    def stage2(after):
        pin0, gate0, kv_w, wq, wo, up1, dn1, pin1, gate1 = finish_group(2, after)
        sq = lambda a: a.reshape(D_MODEL, -1)
        return dict(w_pin=[pin0, pin1], w_gate=[sq(gate0), sq(gate1)], w_kv=sq(kv_w), w_q=sq(wq), w_o=sq(wo),
                    w_up1=up1, w_dn1=dn1.reshape(D_FF, D_MODEL))

    dx, (loss, (out_g, out_d, out_m, out_v)) = _local_step(
        x[0], p[0, 0], p[1, 0], loss_target[0], norm_mix, norm_ffn, norm_ple, norm_kv, norm_final, a_w_s, a_b_s,
        b_sinks, f_conv_b, ple_b_gate, stage0, stage1, stage2, _Reducer(given))
    weight_names = ['norm_mix', 'norm_ffn', 'norm_ple', 'norm_kv', 'norm_final', 'a_w_in', 'a_norm_v', 'a_w_s',
                    'a_b_s', 'a_w_out', 'w_kv', 'b_w_q', 'b_sinks', 'b_w_o', 'f_w_up', 'f_conv_w', 'f_conv_b',
                    'f_w_down', 'ple_w_in', 'ple_w_gate', 'ple_b_gate']
    return (loss, dx.reshape(x.shape), *[out_g[k] for k in weight_names], *[out_d[k] for k in weight_names],
            *[out_m[k] for k in weight_names], *[out_v[k] for k in weight_names])


def _local_step(xs, p0, p1, tgt, norm_mix, norm_ffn, norm_ple, norm_kv, norm_final, a_w_s, a_b_s, b_sinks,
                f_conv_b, ple_b_gate, stage0, stage1, stage2, sched):
    tril = jnp.tril(jnp.ones((CHUNK, CHUNK), F32))
    wsm = (a_w_s[0] * tril[None]).astype(BF16)
    bsb = jnp.broadcast_to(a_b_s[0][:, :, None], (A_GROUPS, CHUNK, CHUNK))
    sinks = b_sinks[0]
    row = lambda a: a.reshape(1, -1)

    gv_full, cw_full, w_in, w_out = stage0()
    h1, zp = _mixer_a_fwd(xs, row(norm_mix[0]), gv_full, wsm, bsb, w_in, w_out)
    w_up0, w_dn0 = stage1((h1,))
    h2, hh0, c0 = _ffn_fwd(h1, row(norm_ffn[0]), cw_full[0], row(f_conv_b[0]), w_up0, w_dn0, 0)
    rest = stage2((h2,))
    w_pin, w_gate, w_kv_f, w_q, w_o = rest['w_pin'], rest['w_gate'], rest['w_kv'], rest['w_q'], rest['w_o']
    w_up = [w_up0, rest['w_up1']]
    w_dn = [w_dn0, rest['w_dn1']]
    h3, pe0, a0, kv = _ple_fwd_kv(h2, p0, row(norm_ple[0]), row(ple_b_gate[0]), row(norm_kv), w_pin[0], w_gate[0], w_kv_f)
    h4, q, ao, probs = _attn_fwd(h3, row(norm_mix[1]), kv, sinks, w_q, w_o)
    h5, hh1, c1 = _ffn_fwd(h4, row(norm_ffn[1]), cw_full[1], row(f_conv_b[1]), w_up[1], w_dn[1], 1)
    dh6, pe1, a1, loss_acc, dn_final = _ple_fwd_final(
        h5, p1, tgt, row(norm_ple[1]), row(ple_b_gate[1]), row(norm_final), w_pin[1], w_gate[1])

    def pieces(g):
        return g.reshape(N_SHARD, -1, g.shape[-1])

    dh5, g_pin1, g_gate1, dbg1, dnple1 = _ple_bwd(dh6, h5, pe1, a1, p1, row(norm_ple[1]), w_gate[1], 1)
    early = {('ple_w_in', 1): g_pin1, ('ple_w_gate', 1): pieces(g_gate1)}
    dh4, g_up1, g_dn1, dcw1, dcb1, dnffn1 = _ffn_bwd(
        dh5, h4, hh1, c1, row(norm_ffn[1]), cw_full[1], w_up[1], w_dn[1], 1)
    early['f_w_down', 1] = pieces(g_dn1)
    early['f_w_up', 1] = g_up1
    dh3a, g_wq, g_wo, dkv, dsink, dnmix1 = _attn_bwd(dh4, h3, q, kv, ao, probs, row(norm_mix[1]), w_q, w_o)
    early['b_w_o', 0] = pieces(g_wo)
    early['b_w_q', 0] = pieces(g_wq)
    dh2, g_pin0, g_gate0, dbg0, dnple0, g_wkv, dnkv = _ple_bwd(
        dh3a, h2, pe0, a0, p0, row(norm_ple[0]), w_gate[0], 0, kv_args=(h3, dkv, row(norm_kv), w_kv_f))
    early['w_kv', 0] = pieces(g_wkv)
    early['ple_w_in', 0] = g_pin0
    early['ple_w_gate', 0] = pieces(g_gate0)
    deps = sched.early_ready(early)
    dh1, g_up0, g_dn0, dcw0, dcb0, dnffn0 = _ffn_bwd(
        dh2, h1, hh0, c0, row(norm_ffn[0]), cw_full[0], w_up[0], w_dn[0], 0, deps=deps,
        between=lambda part: sched.after_ffn_half((part,)))
    deps = sched.ffn0_ready({('f_w_down', 0): pieces(g_dn0), ('f_w_up', 0): g_up0})
    dx, g_win, g_wout, dws, dbs, dgv, dnmix0 = _mixer_a_bwd(
        dh1, xs, zp, row(norm_mix[0]), gv_full, wsm, bsb, tril, w_in, w_out, deps=deps)
    g_wout = pieces(g_wout)

    small_grads = {
        'norm_mix': jnp.concatenate([dnmix0, dnmix1]), 'norm_ffn': jnp.concatenate([dnffn0, dnffn1]),
        'norm_ple': jnp.concatenate([dnple0, dnple1]), 'norm_kv': dnkv, 'norm_final': dn_final,
        'a_norm_v': dgv, 'a_w_s': dws.reshape(A_GROUPS * CHUNK, CHUNK), 'a_b_s': dbs[:, :, 0],
        'b_sinks': dsink[0:1, :], 'f_conv_w': jnp.concatenate([dcw0, dcw1]),
        'f_conv_b': jnp.concatenate([dcb0, dcb1]), 'ple_b_gate': jnp.concatenate([dbg0, dbg1]),
        'loss': loss_acc,
    }
    outs = sched.finish({('a_w_in', 0): g_win, ('a_w_out', 0): g_wout}, small_grads, (dx,))
    return dx, outs


class _Reducer:
    def __init__(self, given):
        self.given = given
        cx, cy, cc = _mesh_pos()
        self.shard = 2 * cx + cy
        s = self.shard
        self.ids = jnp.stack([cc, s, s ^ 2, s ^ 1, s ^ 3]).astype(jnp.int32)
        self.out = [{}, {}, {}, {}]
        self.stacked = {}

    def _send(self, tag, grads, small=()):
        keys = list(grads)
        srcs = [grads[k] for k in keys] + list(small)
        shapes = [((N_SHARD, g.shape[1] // 2, g.shape[2]), F32) for g in srcs[:len(keys)]]
        shapes += [(s.shape, F32) for s in small]
        return keys, _Exchange(f"send_{tag}", srcs, shapes, _send_to_sibling(len(srcs)), len(srcs))

    def _exchange(self, tag, keys, send, after):
        srcs, lands = send.finish(after)
        n = len(keys)
        parts = _chip_partial(srcs[:n], lands[:n], self.ids, f"chip_partial_{tag}")
        shapes = [(p.shape, BF16) for p in parts]
        if len(srcs) > n:
            small = _small_add(srcs[n:], lands[n:])
            parts += small
            shapes += [((3,) + s.shape, F32) for s in small]
        exch = _Exchange(f"exch_{tag}", parts, shapes, _send_to_chips(len(parts)), 3 * len(parts))
        return (keys, srcs[:n], lands[:n], exch)

    def _swap(self, tag, state, after):
        keys, grads, sib, exch = state
        parts, recv = exch.finish(after)
        n = len(keys)
        own = _chip_sum(grads, sib, recv[:n], self.ids, f"chip_sum_{tag}")
        small_red = _small_sum(parts[n:], recv[n:]) if len(parts) > n else None
        return keys, _Exchange(f"swap_{tag}", own, [(o.shape, F32) for o in own], _send_to_sibling(n), n), small_red

    def _adamw(self, keys, swap, after):
        own, sib = swap.finish(after)
        last = None
        for (name, layer), o, s in zip(keys, own, sib):
            w = self.given[name]
            n_layers = w.shape[0] if w.ndim == 3 else 1
            c2 = w.shape[-1]
            res = _adamw_halves(w.reshape(-1, c2), o, s, self.given['m_' + name].reshape(-1, c2),
                                self.given['v_' + name].reshape(-1, c2), self.ids, f"adamw_{name}{layer}",
                                layer, n_layers, self.stacked.get(name))
            self.stacked[name] = res
            if layer == 0:
                for dst, r in zip(self.out, res):
                    dst[name] = r.reshape(w.shape)
            last = res[0]
        return last

    def early_ready(self, grads):
        self.e_keys, self.e_send = self._send("e", grads)
        return (self.e_send.token,)

    def after_ffn_half(self, after):
        self.e_state = self._exchange("e", self.e_keys, self.e_send, after)
        return (self.e_state[3].token,)

    def ffn0_ready(self, grads):
        _, self.e_swap, _ = self._swap("e", self.e_state, tuple(grads.values()))
        f_keys, f_send = self._send("f", grads)
        self.f_state = self._exchange("f", f_keys, f_send, ())
        return (self.f_state[3].token, self.e_swap.token)

    def finish(self, grads, small_grads, after):
        small_names = list(small_grads)
        a_keys, a_send = self._send("a", grads, [small_grads[k] for k in small_names])
        a_state = self._exchange("a", a_keys, a_send, after)
        e_done = self._adamw(self.e_keys, self.e_swap, (a_state[3].token,))
        f_keys, f_swap, _ = self._swap("f", self.f_state, (e_done,))
        f_done = self._adamw(f_keys, f_swap, ())
        _, a_swap, small_red = self._swap("a", a_state, (f_done,))
        self._adamw(a_keys, a_swap, ())

        given = self.given
        reduced = dict(zip(small_names, small_red))
        loss = reduced.pop('loss')[0, 0]
        names = list(reduced)
        items = []
        for k in names:
            g = reduced[k]
            cols = g.shape[1] // N_SHARD if k in ('a_norm_v', 'f_conv_w') else g.shape[1]
            view = lambda a: _lane_pad(a.reshape(g.shape[0], -1), cols)
            items.append((view(given[k]), g, view(given['m_' + k]), view(given['v_' + k])))
        res = _adamw_small(items, self.ids)
        for k, four in zip(names, res):
            width = given[k].size // four[0].shape[0]
            for dst, r in zip(self.out, four):
                dst[k] = r[:, :width].reshape(given[k].shape)
        return loss, self.out


def _lane_pad(a, cols):
    return a if a.shape[1] == cols else jnp.pad(a, ((0, 0), (0, cols - a.shape[1])))


def _small_add(a_list, b_list):
    n = len(a_list)

    def body(*refs):
        for t in range(n):
            refs[2 * n + t][...] = refs[t][...] + refs[n + t][...]

    return pl.pallas_call(body, name="chip_partial_small",
                          out_shape=[jax.ShapeDtypeStruct(a.shape, F32) for a in a_list])(*a_list, *b_list)


def _small_sum(parts, recvs):
    n = len(parts)

    def body(*refs):
        for t in range(n):
            q = refs[n + t]
            refs[2 * n + t][...] = (refs[t][...] + q[2]) + (q[0] + q[1])

    return pl.pallas_call(body, name="chip_sum_small",
                          out_shape=[jax.ShapeDtypeStruct(p.shape, F32) for p in parts])(*parts, *recvs)


def _adamw_small(items, ids):
    n = len(items)

    def body(ids_ref, *refs):
        for t in range(n):
            w_ref, g_ref, m_ref, v_ref = refs[4 * t:4 * t + 4]
            g_out, d_ref, mo_ref, vo_ref = refs[4 * n + 4 * t:4 * n + 4 * t + 4]
            g = g_ref[...]
            g_out[...] = g
            d_ref[...], mo_ref[...], vo_ref[...] = _adamw_math(w_ref[...], g, m_ref[...], v_ref[...])

    in_specs, out_specs, out_shape, args = [], [], [], []
    for w, g, m, v in items:
        full = pl.BlockSpec(w.shape, lambda i, ids_ref: (0, 0))
        g_spec = full if g.shape == w.shape else pl.BlockSpec(w.shape, lambda i, ids_ref: (0, ids_ref[1]))
        in_specs += [full, g_spec, full, full]
        out_specs += [full] * 4
        out_shape += [jax.ShapeDtypeStruct(w.shape, F32)] * 4
        args += [w, g, m, v]
    res = pl.pallas_call(
        body, name="adamw_small",
        grid_spec=pltpu.PrefetchScalarGridSpec(num_scalar_prefetch=1, grid=(1,), in_specs=in_specs,
                                               out_specs=out_specs),
        out_shape=out_shape, compiler_params=_params(),
    )(ids, *args)
    return [res[4 * t:4 * t + 4] for t in range(n)]
```

```python
import functools
import math

import numpy as np
import jax
import jax.numpy as jnp
from jax import lax
from jax.experimental import pallas as pl
from jax.experimental.pallas import tpu as pltpu

F32 = jnp.float32
BF16 = jnp.bfloat16

D_MODEL = 1024
CHUNK = 128
A_GROUPS = 8
HEAD_DIM = 64
N_Q_HEADS = 16
N_KV_HEADS = 4
GQA_GROUP = N_Q_HEADS // N_KV_HEADS
KV_DIM = N_KV_HEADS * HEAD_DIM
BLOCK = 128
D_FF = 2816
N_FF = 2 * D_FF
FF_BLK = N_FF // 4
PLE_DIM = 256
EPS = 1e-6
NEG = -1e30
N_SHARD = 4

ADAM_LR = 0.001
ADAM_B1 = 0.9
ADAM_B2 = 0.999
ADAM_EPS = 1e-08
ADAM_WD = 0.01
ADAM_STEP = 10

VMEM_LIMIT = 60 * 1024 * 1024
MESH = pl.DeviceIdType.MESH
ANY = pl.BlockSpec(memory_space=pl.ANY)
SMEM = pl.BlockSpec(memory_space=pltpu.SMEM)

_SLOPES = [float(np.float32(2.0 ** (-8.0 * (h + 1) / N_Q_HEADS))) for h in range(N_Q_HEADS)]


def _dot(a, b):
    return jnp.dot(a, b, preferred_element_type=F32)


def _dot_nt(a, b):
    return lax.dot_general(a, b, (((1,), (1,)), ((), ())), preferred_element_type=F32)


def _dot_tn(a, b):
    return lax.dot_general(a, b, (((0,), (0,)), ((), ())), preferred_element_type=F32)


def _rms(x, g):
    r = lax.rsqrt(jnp.mean(x * x, axis=-1, keepdims=True) + EPS)
    xh = x * r
    return xh * g, xh, r


def _rms_bwd(dy, xh, r, g):
    dxh = dy * g
    dg = jnp.sum(dy * xh, axis=0, keepdims=True)
    dx = r * (dxh - xh * jnp.mean(dxh * xh, axis=-1, keepdims=True))
    return dx, dg


_GELU_C = math.sqrt(2.0 / math.pi)


def _gelu(x):
    t = jnp.tanh(_GELU_C * (x + 0.044715 * (x * x * x)))
    return 0.5 * x * (1.0 + t)


def _gelu_grad(x):
    x2 = x * x
    t = jnp.tanh(_GELU_C * (x + 0.044715 * (x2 * x)))
    return 0.5 * (1.0 + t) + 0.5 * x * (1.0 - t * t) * (_GELU_C * (1.0 + 3.0 * 0.044715 * x2))


def _sigmoid(x):
    return 0.5 * jnp.tanh(0.5 * x) + 0.5


def _load_once(pairs, sem):
    @pl.when(pl.program_id(0) == 0)
    def _():
        cps = [pltpu.make_async_copy(s, d, sem.at[i]) for i, (s, d) in enumerate(pairs)]
        for cp in cps:
            cp.start()
        for cp in cps:
            cp.wait()


def _params(n_axes=1, vmem=VMEM_LIMIT):
    return pltpu.CompilerParams(dimension_semantics=("arbitrary",) * n_axes, vmem_limit_bytes=vmem)


def _row_spec(tm, n, rev_nt=None):
    if rev_nt is None:
        return pl.BlockSpec((tm, n), lambda i: (i, 0))
    return pl.BlockSpec((tm, n), lambda i: (rev_nt - 1 - i, 0))


def _const_spec(shape):
    nd = len(shape)
    return pl.BlockSpec(shape, lambda i: (0,) * nd)


def _add_deps(body, in_specs, args, deps):
    nd = len(deps)
    if nd == 0:
        return body, list(in_specs), list(args)

    def wrapped(*refs):
        return body(*refs[nd:])

    return wrapped, [ANY] * nd + list(in_specs), list(deps) + list(args)


def _zero_first(refs):
    @pl.when(pl.program_id(0) == 0)
    def _():
        for r in refs:
            r[...] = jnp.zeros(r.shape, r.dtype)


def _mixer_a_fwd(x, nmix, gv, wsm, bsb, w_in, w_out):
    T = x.shape[0]
    tm = min(512, T)
    nt = T // tm
    nw = 2 * D_MODEL // N_SHARD

    def body(x_ref, nmix_ref, gv_ref, ws_ref, bsb_ref, w_in_hbm, w_out_hbm,
             h1_ref, zp_ref, w_in_v, w_out_v, gated_v, sem):
        _load_once([(w_in_hbm, w_in_v), (w_out_hbm, w_out_v)], sem)
        xv = x_ref[...]
        xn = _rms(xv, nmix_ref[...])[0].astype(BF16)
        for j in range(N_SHARD):
            zp_ref[:, j * nw:(j + 1) * nw] = _dot(xn, w_in_v[j])
        z = _gelu(zp_ref[...])
        u = z[:, :D_MODEL]
        vn = _rms(z[:, D_MODEL:], gv_ref[...])[0].astype(BF16)
        for c in range(tm // CHUNK):
            rows = slice(c * CHUNK, (c + 1) * CHUNK)
            for h in range(A_GROUPS):
                cols = slice(h * CHUNK, (h + 1) * CHUNK)
                s = _dot(ws_ref[h], vn[rows, cols]) + bsb_ref[h]
                gated_v[rows, cols] = (u[rows, cols] * s).astype(BF16)
        h1_ref[...] = xv + _dot(gated_v[...], w_out_v[...])

    return pl.pallas_call(
        body, name="mixer_a_fwd", grid=(nt,),
        in_specs=[_row_spec(tm, D_MODEL), _const_spec((1, D_MODEL)), _const_spec((1, D_MODEL)),
                  _const_spec((A_GROUPS, CHUNK, CHUNK)), _const_spec((A_GROUPS, CHUNK, CHUNK)), ANY, ANY],
        out_specs=[_row_spec(tm, D_MODEL), _row_spec(tm, 2 * D_MODEL)],
        out_shape=[jax.ShapeDtypeStruct((T, D_MODEL), F32), jax.ShapeDtypeStruct((T, 2 * D_MODEL), F32)],
        scratch_shapes=[pltpu.VMEM((N_SHARD, D_MODEL, nw), BF16), pltpu.VMEM((D_MODEL, D_MODEL), BF16),
                        pltpu.VMEM((tm, D_MODEL), BF16), pltpu.SemaphoreType.DMA((2,))],
        compiler_params=_params(),
    )(x, nmix, gv, wsm, bsb, w_in, w_out)


def _mixer_a_bwd(dh, x, zp, nmix, gv, wsm, bsb, tril, w_in, w_out, deps=()):
    T = x.shape[0]
    tm = min(256, T)
    nt = T // tm
    nw = 2 * D_MODEL // N_SHARD

    def body(dh_ref, x_ref, zp_ref, nmix_ref, gv_ref, ws_ref, bsb_ref, tril_ref, w_in_hbm, w_out_hbm,
             dx_ref, dwin_ref, dwout_ref, dws_ref, dbs_ref, dgv_ref, dnmix_ref,
             w_in_v, w_out_v, du_v, dvn_v, dbs_v, gated_ref, sem):
        _load_once([(w_in_hbm, w_in_v), (w_out_hbm, w_out_v)], sem)
        _zero_first([dws_ref, dbs_v, dgv_ref, dnmix_ref, dwin_ref, dwout_ref])
        i = pl.program_id(0)
        dhv = dh_ref[...]
        dhb = dhv.astype(BF16)
        xv = x_ref[...]
        xn, xh, r = _rms(xv, nmix_ref[...])
        xnb = xn.astype(BF16)
        zpv = zp_ref[...]
        z = _gelu(zpv)
        u = z[:, :D_MODEL]
        vn_f, vh, rv = _rms(z[:, D_MODEL:], gv_ref[...])
        vn = vn_f.astype(BF16)
        dgated = _dot_nt(dhb, w_out_v[...])
        for c in range(tm // CHUNK):
            rows = slice(c * CHUNK, (c + 1) * CHUNK)
            for h in range(A_GROUPS):
                cols = slice(h * CHUNK, (h + 1) * CHUNK)
                vn_h = vn[rows, cols]
                s = _dot(ws_ref[h], vn_h) + bsb_ref[h]
                dgt = dgated[rows, cols]
                u_h = u[rows, cols]
                gated_ref[rows, cols] = (u_h * s).astype(BF16)
                du_v[rows, cols] = dgt * s
                ds = dgt * u_h
                dsb = ds.astype(BF16)
                dws_ref[h] += _dot_nt(dsb, vn_h)
                dbs_v[h] += ds
                dvn_v[rows, cols] = _dot_tn(ws_ref[h], dsb)
        dwout_ref[...] += _dot_tn(gated_ref[...], dhb)
        dv, dgv = _rms_bwd(dvn_v[...], vh, rv, gv_ref[...])
        dgv_ref[...] += dgv
        dzu = (du_v[...] * _gelu_grad(zpv[:, :D_MODEL])).astype(BF16)
        dzv = (dv * _gelu_grad(zpv[:, D_MODEL:])).astype(BF16)
        dzs = (dzu[:, :nw], dzu[:, nw:], dzv[:, :nw], dzv[:, nw:])
        dxn = jnp.zeros((tm, D_MODEL), F32)
        for j in range(N_SHARD):
            dxn = dxn + _dot_nt(dzs[j], w_in_v[j])
            dwin_ref[j] += _dot_tn(xnb, dzs[j])
        dxx, dn = _rms_bwd(dxn, xh, r, nmix_ref[...])
        dnmix_ref[...] += dn
        dx_ref[...] = dhv + dxx

        @pl.when(i == nt - 1)
        def _():
            for h in range(A_GROUPS):
                dws_ref[h] = dws_ref[h] * tril_ref[...]
                dbs_ref[h] = jnp.broadcast_to(jnp.sum(dbs_v[h], axis=1, keepdims=True), (CHUNK, CHUNK))

    grp = (A_GROUPS, CHUNK, CHUNK)
    body, in_specs, args = _add_deps(
        body, [_row_spec(tm, D_MODEL), _row_spec(tm, D_MODEL), _row_spec(tm, 2 * D_MODEL),
               _const_spec((1, D_MODEL)), _const_spec((1, D_MODEL)), _const_spec(grp), _const_spec(grp),
               _const_spec((CHUNK, CHUNK)), ANY, ANY],
        [dh, x, zp, nmix, gv, wsm, bsb, tril, w_in, w_out], deps)
    return pl.pallas_call(
        body, name="mixer_a_bwd", grid=(nt,), in_specs=in_specs,
        out_specs=[_row_spec(tm, D_MODEL), _const_spec((N_SHARD, D_MODEL, nw)), _const_spec((D_MODEL, D_MODEL)),
                   _const_spec(grp), _const_spec(grp), _const_spec((1, D_MODEL)), _const_spec((1, D_MODEL))],
        out_shape=[jax.ShapeDtypeStruct((T, D_MODEL), F32), jax.ShapeDtypeStruct((N_SHARD, D_MODEL, nw), F32),
                   jax.ShapeDtypeStruct((D_MODEL, D_MODEL), F32),
                   jax.ShapeDtypeStruct(grp, F32), jax.ShapeDtypeStruct(grp, F32),
                   jax.ShapeDtypeStruct((1, D_MODEL), F32), jax.ShapeDtypeStruct((1, D_MODEL), F32)],
        scratch_shapes=[pltpu.VMEM((N_SHARD, D_MODEL, nw), BF16), pltpu.VMEM((D_MODEL, D_MODEL), BF16),
                        pltpu.VMEM((tm, D_MODEL), F32), pltpu.VMEM((tm, D_MODEL), F32),
                        pltpu.VMEM(grp, F32), pltpu.VMEM((tm, D_MODEL), BF16), pltpu.SemaphoreType.DMA((2,))],
        compiler_params=_params(),
    )(*args)


def _load_ffn_weights(w_up_hbm, w_dn_hbm, layer, w_up_v, w_dn_v, sem):
    _load_once([(w_up_hbm, w_up_v), (w_dn_hbm, w_dn_v)], sem)


def _ffn_fwd(h, nffn, cw, cb, w_up, w_dn, layer):
    T = h.shape[0]
    tm = min(256, T)
    nt = T // tm

    def body(h_ref, n_ref, cw_ref, cb_ref, w_up_hbm, w_dn_hbm, out_ref, hh_ref, c_ref,
             w_up_v, w_dn_v, carry_v, sem):
        _load_ffn_weights(w_up_hbm, w_dn_hbm, layer, w_up_v, w_dn_v, sem)
        _zero_first([carry_v])
        xv = h_ref[...]
        xf = _rms(xv, n_ref[...])[0].astype(BF16)
        acc = xv
        for j in range(2):
            cs = []
            for blk in (j, j + 2):
                cols = slice(blk * FF_BLK, (blk + 1) * FF_BLK)
                hh = _dot(xf, w_up_v[blk])
                hh_ref[:, cols] = hh.astype(BF16)
                ext = jnp.concatenate([carry_v[blk], hh], axis=0)
                carry_v[blk] = hh[tm - 8:, :]
                s1 = pltpu.roll(ext, 1, 0)[8:]
                s2 = pltpu.roll(ext, 2, 0)[8:]
                cv = (cb_ref[:, cols] + cw_ref[0:1, cols] * s2 + cw_ref[1:2, cols] * s1
                      + cw_ref[2:3, cols] * hh)
                c_ref[:, cols] = cv.astype(BF16)
                cs.append(cv)
            act = (cs[0] * _sigmoid(cs[0]) * cs[1]).astype(BF16)
            acc = acc + _dot(act, w_dn_v[j * FF_BLK:(j + 1) * FF_BLK, :])
        out_ref[...] = acc

    return pl.pallas_call(
        body, name=f"ffn_fwd{layer}", grid=(nt,),
        in_specs=[_row_spec(tm, D_MODEL), _const_spec((1, D_MODEL)), _const_spec((3, N_FF)),
                  _const_spec((1, N_FF)), ANY, ANY],
        out_specs=[_row_spec(tm, D_MODEL), _row_spec(tm, N_FF), _row_spec(tm, N_FF)],
        out_shape=[jax.ShapeDtypeStruct((T, D_MODEL), F32), jax.ShapeDtypeStruct((T, N_FF), BF16),
                   jax.ShapeDtypeStruct((T, N_FF), BF16)],
        scratch_shapes=[pltpu.VMEM((N_SHARD, D_MODEL, FF_BLK), BF16), pltpu.VMEM((D_FF, D_MODEL), BF16),
                        pltpu.VMEM((N_SHARD, 8, FF_BLK), F32), pltpu.SemaphoreType.DMA((2 * N_SHARD,))],
        compiler_params=_params(),
    )(h, nffn, cw, cb, w_up, w_dn)


def _wgrad(a, b, bn, col_sharded, name, deps=()):
    T, K = a.shape
    N = b.shape[1]
    tt = min(2048, T)
    nn, ntt = N // bn, T // tt
    kr = K // N_SHARD

    def body(a_ref, b_ref, o_ref):
        @pl.when(pl.program_id(1) == 0)
        def _():
            o_ref[...] = jnp.zeros(o_ref.shape, F32)
        d = _dot_tn(a_ref[...].astype(BF16), b_ref[...].astype(BF16))
        if col_sharded:
            o_ref[...] += d
        else:
            for j in range(N_SHARD):
                o_ref[j] += d[j * kr:(j + 1) * kr]

    if col_sharded:
        assert nn == N_SHARD
        out_spec = pl.BlockSpec((None, K, bn), lambda n, t: (n, 0, 0))
        out_shape = jax.ShapeDtypeStruct((N_SHARD, K, bn), F32)
    else:
        out_spec = pl.BlockSpec((N_SHARD, kr, bn), lambda n, t: (0, 0, n))
        out_shape = jax.ShapeDtypeStruct((N_SHARD, kr, N), F32)
    body, in_specs, args = _add_deps(
        body, [pl.BlockSpec((tt, K), lambda n, t: (t, 0)), pl.BlockSpec((tt, bn), lambda n, t: (t, n))],
        [a, b], deps)
    return pl.pallas_call(
        body, name=name, grid=(nn, ntt), in_specs=in_specs, out_specs=out_spec, out_shape=out_shape,
        compiler_params=pltpu.CompilerParams(dimension_semantics=("arbitrary",) * 2, vmem_limit_bytes=VMEM_LIMIT),
    )(*args)


def _ffn_bwd(dh, h, hh, c, nffn, cw, w_up, w_dn, layer, deps=(), between=None):
    T = h.shape[0]
    tm = min(256, T)
    nt = T // tm

    def body(dh_ref, h_ref, hh_ref, c_ref, n_ref, cw_ref, w_up_hbm, w_dn_hbm,
             dhin_ref, act_ref, dhh_ref, xf_ref, dcw_ref, dcb_ref, dn_ref,
             w_up_v, w_dn_v, carry_v, sem):
        _load_ffn_weights(w_up_hbm, w_dn_hbm, layer, w_up_v, w_dn_v, sem)
        _zero_first([carry_v, dcw_ref, dcb_ref, dn_ref])
        dout = dh_ref[...]
        doutb = dout.astype(BF16)
        xf_f, xh, r = _rms(h_ref[...], n_ref[...])
        xf_ref[...] = xf_f.astype(BF16)
        dxf = jnp.zeros((tm, D_MODEL), F32)
        for j in range(2):
            blks = (j, j + 2)
            cg = c_ref[:, j * FF_BLK:(j + 1) * FF_BLK].astype(F32)
            cu = c_ref[:, (j + 2) * FF_BLK:(j + 3) * FF_BLK].astype(F32)
            sg = _sigmoid(cg)
            sil = cg * sg
            act_ref[:, j * FF_BLK:(j + 1) * FF_BLK] = (sil * cu).astype(BF16)
            dact = _dot_nt(doutb, w_dn_v[j * FF_BLK:(j + 1) * FF_BLK, :])
            dcs = (dact * cu * (sg * (1.0 + cg * (1.0 - sg))), dact * sil)
            for blk, dc in zip(blks, dcs):
                cols = slice(blk * FF_BLK, (blk + 1) * FF_BLK)
                hhv = hh_ref[:, cols].astype(F32)
                ext = jnp.concatenate([dc, carry_v[blk]], axis=0)
                carry_v[blk] = dc[:8, :]
                n = tm + 8
                a1 = pltpu.roll(ext, n - 1, 0)[:tm]
                a2 = pltpu.roll(ext, n - 2, 0)[:tm]
                dcb_ref[:, cols] += jnp.sum(dc, axis=0, keepdims=True)
                dcw_ref[0:1, cols] += jnp.sum(a2 * hhv, axis=0, keepdims=True)
                dcw_ref[1:2, cols] += jnp.sum(a1 * hhv, axis=0, keepdims=True)
                dcw_ref[2:3, cols] += jnp.sum(dc * hhv, axis=0, keepdims=True)
                dhh = (cw_ref[2:3, cols] * dc + cw_ref[1:2, cols] * a1 + cw_ref[0:1, cols] * a2).astype(BF16)
                dhh_ref[:, cols] = dhh
                dxf = dxf + _dot_nt(dhh, w_up_v[blk])
        dxx, dn = _rms_bwd(dxf, xh, r, n_ref[...])
        dn_ref[...] += dn
        dhin_ref[...] = dout + dxx

    rev = functools.partial(_row_spec, rev_nt=nt)
    body, in_specs, args = _add_deps(
        body, [rev(tm, D_MODEL), rev(tm, D_MODEL), rev(tm, N_FF), rev(tm, N_FF),
               _const_spec((1, D_MODEL)), _const_spec((3, N_FF)), ANY, ANY],
        [dh, h, hh, c, nffn, cw, w_up, w_dn], deps)
    dhin, act, dhh, xf, dcw, dcb, dn = pl.pallas_call(
        body, name=f"ffn_bwd{layer}", grid=(nt,), in_specs=in_specs,
        out_specs=[rev(tm, D_MODEL), rev(tm, D_FF), rev(tm, N_FF), rev(tm, D_MODEL),
                   _const_spec((3, N_FF)), _const_spec((1, N_FF)), _const_spec((1, D_MODEL))],
        out_shape=[jax.ShapeDtypeStruct((T, D_MODEL), F32), jax.ShapeDtypeStruct((T, D_FF), BF16),
                   jax.ShapeDtypeStruct((T, N_FF), BF16), jax.ShapeDtypeStruct((T, D_MODEL), BF16),
                   jax.ShapeDtypeStruct((3, N_FF), F32), jax.ShapeDtypeStruct((1, N_FF), F32),
                   jax.ShapeDtypeStruct((1, D_MODEL), F32)],
        scratch_shapes=[pltpu.VMEM((N_SHARD, D_MODEL, FF_BLK), BF16), pltpu.VMEM((D_FF, D_MODEL), BF16),
                        pltpu.VMEM((N_SHARD, 8, FF_BLK), F32), pltpu.SemaphoreType.DMA((2 * N_SHARD,))],
        compiler_params=_params(),
    )(*args)
    deps2 = between(dhin) if between is not None else ()
    dwdn = _wgrad(act, dh, D_MODEL // 2, False, f"wgrad_ffn_down{layer}", deps=deps2)
    dwup = _wgrad(xf, dhh, FF_BLK, True, f"wgrad_ffn_up{layer}", deps=deps2)
    return dhin, dwup, dwdn, dcw, dcb, dn


def _load_ple_weights(w_pin_hbm, w_gate_hbm, layer, w_pin_v, w_gate_v, sem, extra=()):
    _load_once([(w_pin_hbm, w_pin_v), (w_gate_hbm, w_gate_v)] + list(extra), sem)


def _ple_terms(xv, p_ref, n_ref, bg_ref, w_pin_v, w_gate_v, pe_v):
    pw = D_MODEL // N_SHARD
    xg, xh, r = _rms(xv, n_ref[...])
    xgb = xg.astype(BF16)
    gate = _sigmoid(_dot(xgb, w_gate_v[...]) + bg_ref[...])
    pb = p_ref[...].astype(BF16)
    for j in range(N_SHARD):
        pe_v[:, j * pw:(j + 1) * pw] = _dot(pb, w_pin_v[j])
    pe = pe_v[...]
    return pe * gate, pe, gate, xgb, xh, r


def _ple_fwd_kv(h, p, nple, bg, nkv, w_pin, w_gate, w_kv):
    T = h.shape[0]
    tm = min(512, T)
    nt = T // tm
    pw = D_MODEL // N_SHARD

    def body(h_ref, p_ref, n_ref, bg_ref, nkv_ref, w_pin_hbm, w_gate_hbm, w_kv_hbm,
             out_ref, kv_ref, w_pin_v, w_gate_v, w_kv_v, pe_v, sem):
        _load_ple_weights(w_pin_hbm, w_gate_hbm, 0, w_pin_v, w_gate_v, sem, [(w_kv_hbm, w_kv_v)])
        xv = h_ref[...]
        hn = xv + _ple_terms(xv, p_ref, n_ref, bg_ref, w_pin_v, w_gate_v, pe_v)[0]
        out_ref[...] = hn
        kvn = _rms(hn, nkv_ref[...])[0].astype(BF16)
        kv_ref[...] = _dot(kvn, w_kv_v[...]).astype(BF16)

    vec = _const_spec((1, D_MODEL))
    return pl.pallas_call(
        body, name="ple_fwd0", grid=(nt,),
        in_specs=[_row_spec(tm, D_MODEL), _row_spec(tm, PLE_DIM), vec, vec, vec, ANY, ANY, ANY],
        out_specs=[_row_spec(tm, D_MODEL), _row_spec(tm, 2 * KV_DIM)],
        out_shape=[jax.ShapeDtypeStruct((T, D_MODEL), F32), jax.ShapeDtypeStruct((T, 2 * KV_DIM), BF16)],
        scratch_shapes=[pltpu.VMEM((N_SHARD, PLE_DIM, pw), BF16), pltpu.VMEM((D_MODEL, D_MODEL), BF16),
                        pltpu.VMEM((D_MODEL, 2 * KV_DIM), BF16), pltpu.VMEM((tm, D_MODEL), F32),
                        pltpu.SemaphoreType.DMA((2 * N_SHARD + 1,))],
        compiler_params=_params(),
    )(h, p, nple, bg, nkv, w_pin, w_gate, w_kv)


def _ple_fwd_final(h, p, tgt, nple, bg, nfin, w_pin, w_gate):
    T = h.shape[0]
    tm = min(512, T)
    nt = T // tm
    pw = D_MODEL // N_SHARD

    def body(h_ref, p_ref, t_ref, n_ref, bg_ref, nf_ref, w_pin_hbm, w_gate_hbm,
             dh_ref, loss_ref, dnf_ref, w_pin_v, w_gate_v, pe_v, sem):
        _load_ple_weights(w_pin_hbm, w_gate_hbm, 1, w_pin_v, w_gate_v, sem)
        _zero_first([loss_ref, dnf_ref])
        xv = h_ref[...]
        hn = xv + _ple_terms(xv, p_ref, n_ref, bg_ref, w_pin_v, w_gate_v, pe_v)[0]
        y, yh, r = _rms(hn, nf_ref[...])
        diff = y - t_ref[...]
        loss_ref[...] += 0.5 * jnp.sum(jnp.mean(diff * diff, axis=-1, keepdims=True))
        dy = diff * (1.0 / D_MODEL)
        dhn, dnf = _rms_bwd(dy, yh, r, nf_ref[...])
        dnf_ref[...] += dnf
        dh_ref[...] = dhn

    vec = _const_spec((1, D_MODEL))
    return pl.pallas_call(
        body, name="ple_fwd1", grid=(nt,),
        in_specs=[_row_spec(tm, D_MODEL), _row_spec(tm, PLE_DIM), _row_spec(tm, D_MODEL), vec, vec, vec, ANY, ANY],
        out_specs=[_row_spec(tm, D_MODEL), _const_spec((8, 128)), vec],
        out_shape=[jax.ShapeDtypeStruct((T, D_MODEL), F32), jax.ShapeDtypeStruct((8, 128), F32),
                   jax.ShapeDtypeStruct((1, D_MODEL), F32)],
        scratch_shapes=[pltpu.VMEM((N_SHARD, PLE_DIM, pw), BF16), pltpu.VMEM((D_MODEL, D_MODEL), BF16),
                        pltpu.VMEM((tm, D_MODEL), F32), pltpu.SemaphoreType.DMA((2 * N_SHARD,))],
        compiler_params=_params(),
    )(h, p, tgt, nple, bg, nfin, w_pin, w_gate)


def _ple_bwd(dh, hb, p, nple, bg, w_pin, w_gate, layer, kv_args=None):
    T = hb.shape[0]
    tm = min(512, T)
    nt = T // tm
    with_kv = kv_args is not None
    pw = D_MODEL // N_SHARD

    def body(*refs):
        if with_kv:
            (dh_ref, hb_ref, p_ref, n_ref, bg_ref, w_pin_hbm, w_gate_hbm, hc_ref, dkv_ref, nkv_ref, w_kv_hbm,
             dhb_ref, dwpin_ref, dwgate_ref, dbg_ref, dn_ref, dwkv_ref, dnkv_ref,
             w_pin_v, w_gate_v, pe_v, w_kv_v, sem) = refs
        else:
            (dh_ref, hb_ref, p_ref, n_ref, bg_ref, w_pin_hbm, w_gate_hbm,
             dhb_ref, dwpin_ref, dwgate_ref, dbg_ref, dn_ref, w_pin_v, w_gate_v, pe_v, sem) = refs
        pairs = [(w_pin_hbm, w_pin_v), (w_gate_hbm, w_gate_v)]
        if with_kv:
            pairs.append((w_kv_hbm, w_kv_v))
        _load_once(pairs, sem)
        _zero_first([dwpin_ref, dwgate_ref, dbg_ref, dn_ref] + ([dwkv_ref, dnkv_ref] if with_kv else []))
        do = dh_ref[...]
        if with_kv:
            dkvb = dkv_ref[...].astype(BF16)
            dkvn = _dot_nt(dkvb, w_kv_v[...])
            kvn, kh, kr = _rms(hc_ref[...], nkv_ref[...])
            dwkv_ref[...] += _dot_tn(kvn.astype(BF16), dkvb)
            dk, dnkv = _rms_bwd(dkvn, kh, kr, nkv_ref[...])
            dnkv_ref[...] += dnkv
            do = do + dk
        _, pe, gate, xgb, xh, r = _ple_terms(hb_ref[...], p_ref, n_ref, bg_ref, w_pin_v, w_gate_v, pe_v)
        dpe = (do * gate).astype(BF16)
        pb = p_ref[...].astype(BF16)
        for j in range(N_SHARD):
            dwpin_ref[j] += _dot_tn(pb, dpe[:, j * pw:(j + 1) * pw])
        da = do * pe * (gate * (1.0 - gate))
        dab = da.astype(BF16)
        dbg_ref[...] += jnp.sum(da, axis=0, keepdims=True)
        dxg = _dot_nt(dab, w_gate_v[...])
        dwgate_ref[...] += _dot_tn(xgb, dab)
        dxx, dn = _rms_bwd(dxg, xh, r, n_ref[...])
        dn_ref[...] += dn
        dhb_ref[...] = do + dxx

    vec = _const_spec((1, D_MODEL))
    row = _row_spec(tm, D_MODEL)
    in_specs = [row, row, _row_spec(tm, PLE_DIM), vec, vec, ANY, ANY]
    args = [dh, hb, p, nple, bg, w_pin, w_gate]
    out_specs = [row, _const_spec((N_SHARD, PLE_DIM, pw)), _const_spec((D_MODEL, D_MODEL)), vec, vec]
    out_shape = [jax.ShapeDtypeStruct((T, D_MODEL), F32), jax.ShapeDtypeStruct((N_SHARD, PLE_DIM, pw), F32),
                 jax.ShapeDtypeStruct((D_MODEL, D_MODEL), F32),
                 jax.ShapeDtypeStruct((1, D_MODEL), F32), jax.ShapeDtypeStruct((1, D_MODEL), F32)]
    scratch = [pltpu.VMEM((N_SHARD, PLE_DIM, pw), BF16), pltpu.VMEM((D_MODEL, D_MODEL), BF16),
               pltpu.VMEM((tm, D_MODEL), F32)]
    if with_kv:
        hc, dkv, nkv, w_kv = kv_args
        in_specs += [row, _row_spec(tm, 2 * KV_DIM), vec, ANY]
        args += [hc, dkv, nkv, w_kv]
        out_specs += [_const_spec((D_MODEL, 2 * KV_DIM)), vec]
        out_shape += [jax.ShapeDtypeStruct((D_MODEL, 2 * KV_DIM), F32), jax.ShapeDtypeStruct((1, D_MODEL), F32)]
        scratch.append(pltpu.VMEM((D_MODEL, 2 * KV_DIM), BF16))
    scratch.append(pltpu.SemaphoreType.DMA((3,)))
    return pl.pallas_call(
        body, name=f"ple_bwd{layer}", grid=(nt,), in_specs=in_specs, out_specs=out_specs,
        out_shape=out_shape, scratch_shapes=scratch, compiler_params=_params(),
    )(*args)


GROUP_ROWS = GQA_GROUP * BLOCK


def _stack_heads(x, kh):
    return jnp.concatenate([x[:, (kh * GQA_GROUP + g) * HEAD_DIM:(kh * GQA_GROUP + g + 1) * HEAD_DIM]
                            for g in range(GQA_GROUP)], axis=0)


def _attn_fwd(h, nmix, kv, sinks, w_q, w_o):
    T = h.shape[0]
    tm = min(512, T)
    nt = T // tm
    nb = tm // BLOCK

    def body(h_ref, n_ref, kv_ref, kvp_ref, sink_ref, w_q_hbm, w_o_hbm,
             out_ref, q_ref, ao_ref, p_ref, psink_ref, w_q_v, w_o_v, kvs_v, sem):
        _load_once([(w_q_hbm, w_q_v), (w_o_hbm, w_o_v)], sem)
        ti = pl.program_id(0)
        xv = h_ref[...]
        xn = _rms(xv, n_ref[...])[0].astype(BF16)
        q_ref[...] = (_dot(xn, w_q_v[...]) * (HEAD_DIM ** -0.5)).astype(BF16)
        kvs_v[0:BLOCK, :] = kvp_ref[...]
        kvs_v[BLOCK:, :] = kv_ref[...]
        lane = lax.broadcasted_iota(jnp.int32, (BLOCK, 128), 1)
        ii = lax.broadcasted_iota(jnp.int32, (BLOCK, 2 * BLOCK), 0)
        jj = lax.broadcasted_iota(jnp.int32, (BLOCK, 2 * BLOCK), 1)
        dist = ii + BLOCK - jj
        inband = (dist >= 0) & (dist < BLOCK)
        distf = dist.astype(F32)

        def blk_body(b, carry):
            r0 = pl.multiple_of(b * BLOCK, BLOCK)
            valid = inband & ((jj >= BLOCK) | jnp.logical_not(jnp.logical_and(ti == 0, b == 0)))
            qb = q_ref[pl.ds(r0, BLOCK), :]
            band = kvs_v[pl.ds(r0, 2 * BLOCK), :]
            psink_mat = jnp.zeros((BLOCK, 128), F32)
            outs = []
            for hq in range(N_Q_HEADS):
                kh, g = divmod(hq, GQA_GROUP)
                k_h = band[:, kh * HEAD_DIM:(kh + 1) * HEAD_DIM]
                v_h = band[:, KV_DIM + kh * HEAD_DIM:KV_DIM + (kh + 1) * HEAD_DIM]
                s = _dot_nt(qb[:, hq * HEAD_DIM:(hq + 1) * HEAD_DIM], k_h) - _SLOPES[hq] * distf
                s = jnp.where(valid, s, NEG)
                sink = sink_ref[hq]
                m = jnp.maximum(jnp.max(s, axis=1, keepdims=True), sink)
                e = jnp.exp(s - m)
                esink = jnp.exp(sink - m)
                den = jnp.sum(e, axis=1, keepdims=True) + esink
                pb = (e / den).astype(BF16)
                p_ref[b, kh, g * BLOCK:(g + 1) * BLOCK, :] = pb
                outs.append(_dot(pb, v_h))
                psink_mat = jnp.where(lane == hq, esink / den, psink_mat)
            ao_ref[pl.ds(r0, BLOCK), :] = jnp.concatenate(outs, axis=1).astype(BF16)
            psink_ref[pl.ds(r0, BLOCK), :] = psink_mat
            return carry

        lax.fori_loop(0, nb, blk_body, 0)
        out_ref[...] = xv + _dot(ao_ref[...], w_o_v[...])

    row = _row_spec(tm, D_MODEL)
    prev_spec = pl.BlockSpec((BLOCK, 2 * KV_DIM), lambda i: (jnp.maximum(i * nb - 1, 0), 0))
    return pl.pallas_call(
        body, name="attn_fwd", grid=(nt,),
        in_specs=[row, _const_spec((1, D_MODEL)), _row_spec(tm, 2 * KV_DIM), prev_spec, SMEM, ANY, ANY],
        out_specs=[row, row, row, pl.BlockSpec((nb, N_KV_HEADS, GROUP_ROWS, 2 * BLOCK), lambda i: (i, 0, 0, 0)),
                   _row_spec(tm, 128)],
        out_shape=[jax.ShapeDtypeStruct((T, D_MODEL), F32), jax.ShapeDtypeStruct((T, D_MODEL), BF16),
                   jax.ShapeDtypeStruct((T, D_MODEL), BF16),
                   jax.ShapeDtypeStruct((T // BLOCK, N_KV_HEADS, GROUP_ROWS, 2 * BLOCK), BF16),
                   jax.ShapeDtypeStruct((T, 128), F32)],
        scratch_shapes=[pltpu.VMEM((D_MODEL, D_MODEL), BF16), pltpu.VMEM((D_MODEL, D_MODEL), BF16),
                        pltpu.VMEM((tm + BLOCK, 2 * KV_DIM), BF16), pltpu.SemaphoreType.DMA((2,))],
        compiler_params=_params(),
    )(h, nmix, kv, kv, sinks, w_q, w_o)


def _attn_bwd(dh, h, q, kv, ao, p, psink, nmix, w_q, w_o):
    T = h.shape[0]
    tm = min(512, T)
    nt = T // tm
    nb = tm // BLOCK

    def body(dh_ref, h_ref, q_ref, kv_ref, kvp_ref, ao_ref, p_ref, psink_ref, n_ref, w_q_hbm, w_o_hbm,
             dhin_ref, dwq_ref, dwo_ref, dkv_ref, dsink_ref, dn_ref,
             w_q_v, w_o_v, kvs_v, dao_v, dq_v, dkv_v, carry_v, sem):
        _load_once([(w_q_hbm, w_q_v), (w_o_hbm, w_o_v)], sem)
        _zero_first([carry_v, dsink_ref, dn_ref, dwq_ref, dwo_ref])
        dout = dh_ref[...]
        doutb = dout.astype(BF16)
        dao_v[...] = _dot_nt(doutb, w_o_v[...])
        dwo_ref[...] += _dot_tn(ao_ref[...], doutb)
        kvs_v[0:BLOCK, :] = kvp_ref[...]
        kvs_v[BLOCK:, :] = kv_ref[...]
        dkv_v[0:tm, :] = jnp.zeros((tm, 2 * KV_DIM), F32)
        dkv_v[tm:, :] = carry_v[...]
        seg = (lax.broadcasted_iota(jnp.int32, (D_MODEL, 128), 0) // HEAD_DIM
               == lax.broadcasted_iota(jnp.int32, (D_MODEL, 128), 1)).astype(BF16)

        def blk_body(b, dsk):
            r0 = pl.multiple_of(b * BLOCK, BLOCK)
            qb = q_ref[pl.ds(r0, BLOCK), :]
            band = kvs_v[pl.ds(r0, 2 * BLOCK), :]
            aob = ao_ref[pl.ds(r0, BLOCK), :].astype(F32)
            daob = dao_v[pl.ds(r0, BLOCK), :]
            prod = daob * aob
            head = prod.astype(BF16)
            tail = (prod - head.astype(F32)).astype(BF16)
            dsk = dsk + psink_ref[pl.ds(r0, BLOCK), :] * (_dot(head, seg) + _dot(tail, seg))
            dqs = []
            dks = []
            dvs = []
            for kh in range(N_KV_HEADS):
                k_h = band[:, kh * HEAD_DIM:(kh + 1) * HEAD_DIM]
                v_h = band[:, KV_DIM + kh * HEAD_DIM:KV_DIM + (kh + 1) * HEAD_DIM]
                q_g = _stack_heads(qb, kh)
                dao_g = _stack_heads(daob, kh)
                prb = p_ref[b, kh]
                pr = prb.astype(F32)
                dd = jnp.sum(dao_g * _stack_heads(aob, kh), axis=1, keepdims=True)
                dao_gb = dao_g.astype(BF16)
                dp = _dot_nt(dao_gb, v_h)
                dsb = (pr * (dp - dd)).astype(BF16)
                dq_g = _dot(dsb, k_h) * (HEAD_DIM ** -0.5)
                dks.append(_dot_tn(dsb, q_g))
                dvs.append(_dot_tn(prb, dao_gb))
                for g in range(GQA_GROUP):
                    dqs.append(dq_g[g * BLOCK:(g + 1) * BLOCK])
            dq_v[pl.ds(r0, BLOCK), :] = jnp.concatenate(dqs, axis=1)
            dkv_v[pl.ds(r0, 2 * BLOCK), :] += jnp.concatenate(dks + dvs, axis=1)
            return dsk

        dsk = lax.fori_loop(0, nb, blk_body, jnp.zeros((BLOCK, 128), F32))
        dsink_ref[...] -= jnp.sum(dsk, axis=0, keepdims=True)
        dqb = dq_v[...].astype(BF16)
        dxn = _dot_nt(dqb, w_q_v[...])
        xn, xh, r = _rms(h_ref[...], n_ref[...])
        dwq_ref[...] += _dot_tn(xn.astype(BF16), dqb)
        dxx, dn = _rms_bwd(dxn, xh, r, n_ref[...])
        dn_ref[...] += dn
        dhin_ref[...] = dout + dxx
        dkv_ref[...] = dkv_v[BLOCK:, :]
        carry_v[...] = dkv_v[0:BLOCK, :]

    rev = functools.partial(_row_spec, rev_nt=nt)
    row = rev(tm, D_MODEL)
    prev_spec = pl.BlockSpec((BLOCK, 2 * KV_DIM), lambda i: (jnp.maximum((nt - 1 - i) * nb - 1, 0), 0))
    return pl.pallas_call(
        body, name="attn_bwd", grid=(nt,),
        in_specs=[row, row, row, rev(tm, 2 * KV_DIM), prev_spec, row,
                  pl.BlockSpec((nb, N_KV_HEADS, GROUP_ROWS, 2 * BLOCK), lambda i: (nt - 1 - i, 0, 0, 0)),
                  rev(tm, 128), _const_spec((1, D_MODEL)), ANY, ANY],
        out_specs=[row, _const_spec((D_MODEL, D_MODEL)), _const_spec((D_MODEL, D_MODEL)), rev(tm, 2 * KV_DIM),
                   _const_spec((8, 128)), _const_spec((1, D_MODEL))],
        out_shape=[jax.ShapeDtypeStruct((T, D_MODEL), F32), jax.ShapeDtypeStruct((D_MODEL, D_MODEL), F32),
                   jax.ShapeDtypeStruct((D_MODEL, D_MODEL), F32), jax.ShapeDtypeStruct((T, 2 * KV_DIM), F32),
                   jax.ShapeDtypeStruct((8, 128), F32), jax.ShapeDtypeStruct((1, D_MODEL), F32)],
        scratch_shapes=[pltpu.VMEM((D_MODEL, D_MODEL), BF16), pltpu.VMEM((D_MODEL, D_MODEL), BF16),
                        pltpu.VMEM((tm + BLOCK, 2 * KV_DIM), BF16), pltpu.VMEM((tm, D_MODEL), F32),
                        pltpu.VMEM((tm, D_MODEL), F32), pltpu.VMEM((tm + BLOCK, 2 * KV_DIM), F32),
                        pltpu.VMEM((BLOCK, 2 * KV_DIM), F32), pltpu.SemaphoreType.DMA((2,))],
        compiler_params=_params(),
    )(dh, h, q, kv, kv, ao, p, psink, nmix, w_q, w_o)


def _mesh_pos():
    return lax.axis_index("x"), lax.axis_index("y"), lax.axis_index("c")


def _other_chips(x, y):
    return [(1 - x, y), (x, 1 - y), (1 - x, 1 - y)]


HBM_SPEC = pl.BlockSpec(memory_space=pltpu.HBM)
SEM_SPEC = pl.BlockSpec(memory_space=pltpu.SEMAPHORE)


def _split_call(name, bufs, waits=(), starts=(), after=()):
    n, nw, ns, na = len(bufs), len(waits), len(starts), len(after)

    def body(*refs):
        brefs = refs[:n]
        wsems = [(refs[n + 2 * k], refs[n + 2 * k + 1]) for k in range(nw)]
        o = n + 2 * nw + na
        ssems = [(refs[o + 2 * k], refs[o + 2 * k + 1]) for k in range(ns)]
        for (ss, rs), (_, _, fn) in zip(wsems, waits):
            for sending, arriving in fn(brefs, ss, rs):
                sending.wait_send()
                arriving.wait_recv()
        for (ss, rs), (_, fn) in zip(ssems, starts):
            for sending, _ in fn(brefs, ss, rs):
                sending.start()
        if ns:
            token = refs[o + 2 * ns + n]
            token[...] = jnp.zeros(token.shape, token.dtype)

    out_shape, out_specs = [], []
    for cnt, _ in starts:
        out_shape += [pltpu.SemaphoreType.DMA((cnt,)), pltpu.SemaphoreType.DMA((cnt,))]
        out_specs += [SEM_SPEC, SEM_SPEC]
    out_shape += [pltpu.HBM(b.shape, b.dtype) for b in bufs]
    out_specs += [HBM_SPEC] * n
    if ns:
        out_shape.append(jax.ShapeDtypeStruct((8, 128), F32))
        out_specs.append(pl.BlockSpec(memory_space=pltpu.VMEM))
    args = [pltpu.with_memory_space_constraint(b, pltpu.HBM) for b in bufs]
    for ss, rs, _ in waits:
        args += [ss, rs]
    args += list(after)
    res = pl.pallas_call(
        body, name=name, out_shape=tuple(out_shape),
        in_specs=[HBM_SPEC] * n + [SEM_SPEC] * (2 * nw) + [ANY] * na, out_specs=tuple(out_specs),
        input_output_aliases={i: 2 * ns + i for i in range(n)},
        compiler_params=pltpu.CompilerParams(has_side_effects=pltpu.SideEffectType.DATAFLOW_SIDE_EFFECTING),
    )(*args)
    sems = [(res[2 * k], res[2 * k + 1]) for k in range(ns)]
    return list(res[2 * ns:2 * ns + n]), sems, (res[2 * ns + n] if ns else None)


def _cast_place(items, name):
    n = len(items)
    mats = [a.shape[-2:] for a, _, _ in items]

    def body(*refs):
        ins, outs, scr, sem = refs[:n], refs[n:2 * n], refs[2 * n:3 * n], refs[3 * n]
        x, y, _ = _mesh_pos()
        cps = []
        for t in range(n):
            scr[t][...] = ins[t][...].astype(scr[t].dtype)
            cp = pltpu.make_async_copy(scr[t], outs[t].at[2 * x + y], sem.at[t])
            cp.start()
            cps.append(cp)
        for cp in cps:
            cp.wait()

    def spec(idx, shape):
        return pl.BlockSpec((None,) * len(idx) + tuple(shape), lambda i: tuple(idx) + (0, 0))

    return pl.pallas_call(
        body, name=name, grid=(1,),
        in_specs=[spec(idx, mat) for (_, idx, _), mat in zip(items, mats)], out_specs=[ANY] * n,
        out_shape=[jax.ShapeDtypeStruct((N_SHARD,) + tuple(mat), dt) for (_, _, dt), mat in zip(items, mats)],
        scratch_shapes=[pltpu.VMEM(tuple(mat), dt) for (_, _, dt), mat in zip(items, mats)]
        + [pltpu.SemaphoreType.DMA((n,))],
        compiler_params=_params(),
    )(*[a for a, _, _ in items])


def _gather_ici(idx):
    def fn(bufs, ss, rs):
        x, y, c = _mesh_pos()
        pairs = []
        for k, t in enumerate(idx):
            half = bufs[t].shape[1] // 2
            mine = bufs[t].at[2 * x + y, pl.ds(c * half, half), :]
            for j, (cx, cy) in enumerate(_other_chips(x, y)):
                theirs = bufs[t].at[2 * cx + cy, pl.ds(c * half, half), :]
                sem = dict(send_sem=ss.at[3 * k + j], recv_sem=rs.at[3 * k + j],
                           device_id=(cx, cy, c), device_id_type=MESH)
                pairs.append((pltpu.make_async_remote_copy(src_ref=mine, dst_ref=mine, **sem),
                              pltpu.make_async_remote_copy(src_ref=mine, dst_ref=theirs, **sem)))
        return pairs
    return fn


def _gather_d2d(idx):
    def fn(bufs, ss, rs):
        x, y, c = _mesh_pos()
        pairs = []
        for k, t in enumerate(idx):
            half = bufs[t].shape[1] // 2
            for j, (cx, cy) in enumerate(_other_chips(x, y)):
                got = bufs[t].at[2 * cx + cy, pl.ds(c * half, half), :]
                theirs = bufs[t].at[2 * cx + cy, pl.ds((1 - c) * half, half), :]
                sem = dict(send_sem=ss.at[3 * k + j], recv_sem=rs.at[3 * k + j],
                           device_id=(x, y, 1 - c), device_id_type=MESH)
                pairs.append((pltpu.make_async_remote_copy(src_ref=got, dst_ref=got, **sem),
                              pltpu.make_async_remote_copy(src_ref=got, dst_ref=theirs, **sem)))
        return pairs
    return fn


def _alloc(shapes, name):
    def body(*refs):
        pass

    return pl.pallas_call(body, name=name, out_specs=[ANY] * len(shapes),
                          out_shape=[jax.ShapeDtypeStruct(s, d) for s, d in shapes])()


def _send_to_sibling(n):
    def fn(bufs, ss, rs):
        x, y, c = _mesh_pos()
        pairs = []
        for t in range(n):
            src = bufs[t]
            if len(src.shape) == 3:
                half = src.shape[1] // 2
                src = src.at[:, pl.ds((1 - c) * half, half), :]
            cp = pltpu.make_async_remote_copy(src_ref=src, dst_ref=bufs[n + t], send_sem=ss.at[t],
                                              recv_sem=rs.at[t], device_id=(x, y, 1 - c), device_id_type=MESH)
            pairs.append((cp, cp))
        return pairs
    return fn


def _send_to_chips(n):
    def fn(bufs, ss, rs):
        x, y, c = _mesh_pos()
        pairs = []
        for j, (cx, cy) in enumerate(_other_chips(x, y)):
            for t in range(n):
                src = bufs[t].at[j] if len(bufs[t].shape) == 3 else bufs[t]
                cp = pltpu.make_async_remote_copy(src_ref=src, dst_ref=bufs[n + t].at[j], send_sem=ss.at[3 * t + j],
                                                  recv_sem=rs.at[3 * t + j], device_id=(cx, cy, c),
                                                  device_id_type=MESH)
                pairs.append((cp, cp))
        return pairs
    return fn


class _Exchange:
    def __init__(self, name, srcs, land_shapes, fn, n_sems):
        self.name, self.fn = name, fn
        lands = _alloc(land_shapes, name + "_alloc")
        self.n = len(srcs)
        self.bufs, sems, self.token = _split_call(name + "_start", list(srcs) + list(lands),
                                                  starts=[(n_sems, fn)])
        self.sems = sems[0]

    def finish(self, after=()):
        bufs, _, _ = _split_call(self.name + "_wait", self.bufs, waits=[(*self.sems, self.fn)], after=after)
        return bufs[:self.n], bufs[self.n:]


def _row_block(rows, cols, mult=8, limit=3 * 512 * 1024, itemsize=4):
    best = None
    for br in range(mult, rows + 1, mult):
        if rows % br == 0 and br * cols * itemsize <= limit:
            best = br
    assert best is not None, (rows, cols)
    return best


_GROUP_BLOCK_BYTES = 512 * 1024


def _group_plan(ss):
    plan = []
    for s in ss:
        half, cols = s.shape[-2:]
        br = _row_block(half, cols, mult=16, limit=_GROUP_BLOCK_BYTES)
        plan.append((br, half // br))
    return plan, max(nr for _, nr in plan)


def _chip_partial(gs, ss, ids, name):
    n = len(gs)
    plan, steps = _group_plan(ss)

    def body(ids_ref, *refs):
        for t in range(n):
            refs[2 * n + t][...] = (refs[t][...] + refs[n + t][...]).astype(BF16)

    g_specs, s_specs, o_specs = [], [], []
    for (br, nr), s in zip(plan, ss):
        blk = (None, br, s.shape[2])
        g_specs.append(pl.BlockSpec(
            blk, lambda j, r, ids_ref, nr=nr: (ids_ref[2 + j], ids_ref[0] * nr + jnp.minimum(r, nr - 1), 0)))
        s_specs.append(pl.BlockSpec(blk, lambda j, r, ids_ref, nr=nr: (ids_ref[2 + j], jnp.minimum(r, nr - 1), 0)))
        o_specs.append(pl.BlockSpec(blk, lambda j, r, ids_ref, nr=nr: (j, jnp.minimum(r, nr - 1), 0)))
    return pl.pallas_call(
        body, name=name,
        grid_spec=pltpu.PrefetchScalarGridSpec(num_scalar_prefetch=1, grid=(3, steps),
                                               in_specs=g_specs + s_specs, out_specs=o_specs),
        out_shape=[jax.ShapeDtypeStruct((3,) + s.shape[1:], BF16) for s in ss],
        compiler_params=pltpu.CompilerParams(dimension_semantics=("arbitrary", "arbitrary"),
                                             vmem_limit_bytes=VMEM_LIMIT),
    )(ids, *gs, *ss)


def _chip_sum(gs, ss, qs, ids, name):
    n = len(gs)
    plan, steps = _group_plan(ss)

    def body(ids_ref, *refs):
        for t in range(n):
            q_ref = refs[2 * n + t]
            own = refs[t][...] + refs[n + t][...]
            refs[3 * n + t][...] = (own + q_ref[2].astype(F32)) + (q_ref[0].astype(F32) + q_ref[1].astype(F32))

    g_specs, s_specs, q_specs, o_specs = [], [], [], []
    for (br, nr), s in zip(plan, ss):
        cols = s.shape[2]
        g_specs.append(pl.BlockSpec(
            (None, br, cols), lambda r, ids_ref, nr=nr: (ids_ref[1], ids_ref[0] * nr + jnp.minimum(r, nr - 1), 0)))
        s_specs.append(pl.BlockSpec((None, br, cols), lambda r, ids_ref, nr=nr: (ids_ref[1], jnp.minimum(r, nr - 1), 0)))
        q_specs.append(pl.BlockSpec((3, br, cols), lambda r, ids_ref, nr=nr: (0, jnp.minimum(r, nr - 1), 0)))
        o_specs.append(pl.BlockSpec((br, cols), lambda r, ids_ref, nr=nr: (jnp.minimum(r, nr - 1), 0)))
    return pl.pallas_call(
        body, name=name,
        grid_spec=pltpu.PrefetchScalarGridSpec(num_scalar_prefetch=1, grid=(steps,),
                                               in_specs=g_specs + s_specs + q_specs, out_specs=o_specs),
        out_shape=[jax.ShapeDtypeStruct(s.shape[1:], F32) for s in ss],
        compiler_params=pltpu.CompilerParams(dimension_semantics=("arbitrary",), vmem_limit_bytes=VMEM_LIMIT),
    )(ids, *gs, *ss, *qs)


def _adamw_math(w, g, m, v):
    mn = ADAM_B1 * m + (1.0 - ADAM_B1) * g
    vn = ADAM_B2 * v + (1.0 - ADAM_B2) * (g * g)
    m_hat = mn / (1.0 - ADAM_B1 ** ADAM_STEP)
    v_hat = vn / (1.0 - ADAM_B2 ** ADAM_STEP)
    return -ADAM_LR * (m_hat / (jnp.sqrt(v_hat) + ADAM_EPS) + ADAM_WD * w), mn, vn


def _adamw_halves(w, own, sib, m, v, ids, name, layer=0, n_layers=1, stacked=None):
    C = w.shape[1]
    R = w.shape[0] // n_layers
    half = R // 2
    br = _row_block(half, C)
    nh = half // br
    base = layer * 2 * nh

    def body(ids_ref, w_ref, own_ref, sib_ref, m_ref, v_ref, *rest):
        g_ref, d_ref, mo_ref, vo_ref = rest[-4:]
        is_own = (pl.program_id(0) // nh) == ids_ref[0]
        g = jnp.where(is_own, own_ref[...], sib_ref[...])
        g_ref[...] = g
        d_ref[...], mo_ref[...], vo_ref[...] = _adamw_math(w_ref[...], g, m_ref[...], v_ref[...])

    full = pl.BlockSpec((br, C), lambda r, ids_ref: (base + r, 0))
    own_spec = pl.BlockSpec((br, C), lambda r, ids_ref: (jnp.clip(r - ids_ref[0] * nh, 0, nh - 1), 0))
    sib_spec = pl.BlockSpec((br, C), lambda r, ids_ref: (jnp.clip(r - (1 - ids_ref[0]) * nh, 0, nh - 1), 0))
    in_specs = [full, own_spec, sib_spec, full, full]
    args = [ids, w, own, sib, m, v]
    aliases = {}
    if stacked is not None:
        in_specs += [ANY] * 4
        args += list(stacked)
        aliases = {6 + k: k for k in range(4)}
    return pl.pallas_call(
        body, name=name,
        grid_spec=pltpu.PrefetchScalarGridSpec(
            num_scalar_prefetch=1, grid=(2 * nh,), in_specs=in_specs, out_specs=[full] * 4),
        out_shape=[jax.ShapeDtypeStruct(w.shape, F32)] * 4, input_output_aliases=aliases,
        compiler_params=_params(),
    )(*args)


_PACK_UNIT = 1024


def _pack(arrs):
    flat = []
    for a in arrs:
        f = a.reshape(-1).astype(F32)
        pad = (-f.shape[0]) % _PACK_UNIT
        if pad:
            f = jnp.concatenate([f, jnp.zeros((pad,), F32)])
        flat.append(f)
    return jnp.concatenate(flat).reshape(-1, 128)


def kernel(x, p, norm_mix, norm_ffn, norm_ple, norm_kv, norm_final, a_w_in, a_norm_v, a_w_s, a_b_s, a_w_out, w_kv, b_w_q, b_sinks, b_w_o, f_w_up, f_conv_w, f_conv_b, f_w_down, ple_w_in, ple_w_gate, ple_b_gate, loss_target, m_norm_mix, m_norm_ffn, m_norm_ple, m_norm_kv, m_norm_final, m_a_w_in, m_a_norm_v, m_a_w_s, m_a_b_s, m_a_w_out, m_w_kv, m_b_w_q, m_b_sinks, m_b_w_o, m_f_w_up, m_f_conv_w, m_f_conv_b, m_f_w_down, m_ple_w_in, m_ple_w_gate, m_ple_b_gate, v_norm_mix, v_norm_ffn, v_norm_ple, v_norm_kv, v_norm_final, v_a_w_in, v_a_norm_v, v_a_w_s, v_a_b_s, v_a_w_out, v_w_kv, v_b_w_q, v_b_sinks, v_b_w_o, v_f_w_up, v_f_conv_w, v_f_conv_b, v_f_w_down, v_ple_w_in, v_ple_w_gate, v_ple_b_gate):
    given = dict(locals())

    small_shard = _pack([a_norm_v, f_conv_w])
    pad_rows = (-small_shard.shape[0]) % 16
    if pad_rows:
        small_shard = jnp.concatenate([small_shard, jnp.zeros((pad_rows, 128), F32)])
    groups = [
        [(a_w_in, (0,), BF16), (a_w_out, (0,), BF16), (small_shard, (), F32)],
        [(f_w_up, (0,), BF16), (f_w_down, (0,), BF16)],
        [(ple_w_in, (0,), BF16), (ple_w_gate, (0,), BF16), (w_kv, (), BF16), (b_w_q, (0,), BF16),
         (b_w_o, (0,), BF16), (f_w_up, (1,), BF16), (f_w_down, (1,), BF16), (ple_w_in, (1,), BF16),
         (ple_w_gate, (1,), BF16)],
    ]
    lands, spans, start = [], [], 0
    for gi, items in enumerate(groups):
        lands += _cast_place(items, f"cast_place_g{gi}")
        spans.append(list(range(start, start + len(items))))
        start += len(items)
    lands, ici_sems, _ = _split_call("gather_start", lands,
                                     starts=[(3 * len(sp), _gather_ici(sp)) for sp in spans])

    def finish_group(gi, after):
        sp = spans[gi]
        local = list(range(len(sp)))
        bufs = [lands[t] for t in sp]
        bufs, d2d_sems, _ = _split_call(f"gather_pass_g{gi}", bufs, waits=[(*ici_sems[gi], _gather_ici(local))],
                                        starts=[(3 * len(sp), _gather_d2d(local))], after=after)
        bufs, _, _ = _split_call(f"gather_done_g{gi}", bufs, waits=[(*d2d_sems[0], _gather_d2d(local))])
        return bufs

    def stage0():
        b_in, b_out, b_small = finish_group(0, ())
        small_full = b_small.reshape(N_SHARD, -1)
        gv_full = small_full[:, :256].reshape(1, D_MODEL)
        cw_full = small_full[:, _PACK_UNIT:_PACK_UNIT + 2 * 3 * FF_BLK].reshape(N_SHARD, 2, 3, FF_BLK)
        cw_full = jnp.transpose(cw_full, (1, 2, 0, 3)).reshape(2, 3, N_FF)
        return gv_full, cw_full, b_in, b_out.reshape(D_MODEL, D_MODEL)

    def stage1(after):
        b_up, b_dn = finish_group(1, after)
        return b_up, b_dn.reshape(D_FF, D_MODEL)

    def stage2(after):
        pin0, gate0, kv_w, wq, wo, up1, dn1, pin1, gate1 = finish_group(2, after)
        sq = lambda a: a.reshape(D_MODEL, -1)
        return dict(w_pin=[pin0, pin1], w_gate=[sq(gate0), sq(gate1)], w_kv=sq(kv_w), w_q=sq(wq), w_o=sq(wo),
                    w_up1=up1, w_dn1=dn1.reshape(D_FF, D_MODEL))

    dx, (loss, (out_g, out_d, out_m, out_v)) = _local_step(
        x[0], p[0, 0], p[1, 0], loss_target[0], norm_mix, norm_ffn, norm_ple, norm_kv, norm_final, a_w_s, a_b_s,
        b_sinks, f_conv_b, ple_b_gate, stage0, stage1, stage2, _Reducer(given))
    weight_names = ['norm_mix', 'norm_ffn', 'norm_ple', 'norm_kv', 'norm_final', 'a_w_in', 'a_norm_v', 'a_w_s',
                    'a_b_s', 'a_w_out', 'w_kv', 'b_w_q', 'b_sinks', 'b_w_o', 'f_w_up', 'f_conv_w', 'f_conv_b',
                    'f_w_down', 'ple_w_in', 'ple_w_gate', 'ple_b_gate']
    return (loss, dx.reshape(x.shape), *[out_g[k] for k in weight_names], *[out_d[k] for k in weight_names],
            *[out_m[k] for k in weight_names], *[out_v[k] for k in weight_names])


def _local_step(xs, p0, p1, tgt, norm_mix, norm_ffn, norm_ple, norm_kv, norm_final, a_w_s, a_b_s, b_sinks,
                f_conv_b, ple_b_gate, stage0, stage1, stage2, sched):
    tril = jnp.tril(jnp.ones((CHUNK, CHUNK), F32))
    wsm = (a_w_s[0] * tril[None]).astype(BF16)
    bsb = jnp.broadcast_to(a_b_s[0][:, :, None], (A_GROUPS, CHUNK, CHUNK))
    sinks = b_sinks[0]
    row = lambda a: a.reshape(1, -1)

    gv_full, cw_full, w_in, w_out = stage0()
    h1, zp = _mixer_a_fwd(xs, row(norm_mix[0]), gv_full, wsm, bsb, w_in, w_out)
    w_up0, w_dn0 = stage1((h1,))
    h2, hh0, c0 = _ffn_fwd(h1, row(norm_ffn[0]), cw_full[0], row(f_conv_b[0]), w_up0, w_dn0, 0)
    rest = stage2((h2,))
    w_pin, w_gate, w_kv_f, w_q, w_o = rest['w_pin'], rest['w_gate'], rest['w_kv'], rest['w_q'], rest['w_o']
    w_up = [w_up0, rest['w_up1']]
    w_dn = [w_dn0, rest['w_dn1']]
    h3, kv = _ple_fwd_kv(h2, p0, row(norm_ple[0]), row(ple_b_gate[0]), row(norm_kv), w_pin[0], w_gate[0], w_kv_f)
    h4, q, ao, probs, psink = _attn_fwd(h3, row(norm_mix[1]), kv, sinks, w_q, w_o)
    h5, hh1, c1 = _ffn_fwd(h4, row(norm_ffn[1]), cw_full[1], row(f_conv_b[1]), w_up[1], w_dn[1], 1)
    dh6, loss_acc, dn_final = _ple_fwd_final(
        h5, p1, tgt, row(norm_ple[1]), row(ple_b_gate[1]), row(norm_final), w_pin[1], w_gate[1])

    def pieces(g):
        return g.reshape(N_SHARD, -1, g.shape[-1])

    dh5, g_pin1, g_gate1, dbg1, dnple1 = _ple_bwd(dh6, h5, p1, row(norm_ple[1]), row(ple_b_gate[1]), w_pin[1], w_gate[1], 1)
    early = {('ple_w_in', 1): g_pin1, ('ple_w_gate', 1): pieces(g_gate1)}
    dh4, g_up1, g_dn1, dcw1, dcb1, dnffn1 = _ffn_bwd(
        dh5, h4, hh1, c1, row(norm_ffn[1]), cw_full[1], w_up[1], w_dn[1], 1)
    early['f_w_down', 1] = pieces(g_dn1)
    early['f_w_up', 1] = g_up1
    dh3a, g_wq, g_wo, dkv, dsink, dnmix1 = _attn_bwd(dh4, h3, q, kv, ao, probs, psink, row(norm_mix[1]), w_q, w_o)
    early['b_w_o', 0] = pieces(g_wo)
    early['b_w_q', 0] = pieces(g_wq)
    dh2, g_pin0, g_gate0, dbg0, dnple0, g_wkv, dnkv = _ple_bwd(
        dh3a, h2, p0, row(norm_ple[0]), row(ple_b_gate[0]), w_pin[0], w_gate[0], 0,
        kv_args=(h3, dkv, row(norm_kv), w_kv_f))
    early['w_kv', 0] = pieces(g_wkv)
    early['ple_w_in', 0] = g_pin0
    early['ple_w_gate', 0] = pieces(g_gate0)
    deps = sched.early_ready(early)
    dh1, g_up0, g_dn0, dcw0, dcb0, dnffn0 = _ffn_bwd(
        dh2, h1, hh0, c0, row(norm_ffn[0]), cw_full[0], w_up[0], w_dn[0], 0, deps=deps,
        between=lambda part: sched.after_ffn_half((part,)))
    deps = sched.ffn0_ready({('f_w_down', 0): pieces(g_dn0), ('f_w_up', 0): g_up0})
    dx, g_win, g_wout, dws, dbs, dgv, dnmix0 = _mixer_a_bwd(
        dh1, xs, zp, row(norm_mix[0]), gv_full, wsm, bsb, tril, w_in, w_out, deps=deps)
    g_wout = pieces(g_wout)

    small_grads = {
        'norm_mix': jnp.concatenate([dnmix0, dnmix1]), 'norm_ffn': jnp.concatenate([dnffn0, dnffn1]),
        'norm_ple': jnp.concatenate([dnple0, dnple1]), 'norm_kv': dnkv, 'norm_final': dn_final,
        'a_norm_v': dgv, 'a_w_s': dws.reshape(A_GROUPS * CHUNK, CHUNK), 'a_b_s': dbs[:, :, 0],
        'b_sinks': dsink[0:1, :], 'f_conv_w': jnp.concatenate([dcw0, dcw1]),
        'f_conv_b': jnp.concatenate([dcb0, dcb1]), 'ple_b_gate': jnp.concatenate([dbg0, dbg1]),
        'loss': loss_acc,
    }
    outs = sched.finish({('a_w_in', 0): g_win, ('a_w_out', 0): g_wout}, small_grads, (dx,))
    return dx, outs


class _Reducer:
    def __init__(self, given):
        self.given = given
        cx, cy, cc = _mesh_pos()
        self.shard = 2 * cx + cy
        s = self.shard
        self.ids = jnp.stack([cc, s, s ^ 2, s ^ 1, s ^ 3]).astype(jnp.int32)
        self.out = [{}, {}, {}, {}]
        self.stacked = {}

    def _send(self, tag, grads, small=()):
        keys = list(grads)
        srcs = [grads[k] for k in keys] + list(small)
        shapes = [((N_SHARD, g.shape[1] // 2, g.shape[2]), F32) for g in srcs[:len(keys)]]
        shapes += [(s.shape, F32) for s in small]
        return keys, _Exchange(f"send_{tag}", srcs, shapes, _send_to_sibling(len(srcs)), len(srcs))

    def _exchange(self, tag, keys, send, after):
        srcs, lands = send.finish(after)
        n = len(keys)
        parts = _chip_partial(srcs[:n], lands[:n], self.ids, f"chip_partial_{tag}")
        shapes = [(p.shape, BF16) for p in parts]
        if len(srcs) > n:
            small = _small_add(srcs[n:], lands[n:])
            parts += small
            shapes += [((3,) + s.shape, F32) for s in small]
        exch = _Exchange(f"exch_{tag}", parts, shapes, _send_to_chips(len(parts)), 3 * len(parts))
        return (keys, srcs[:n], lands[:n], exch)

    def _swap(self, tag, state, after):
        keys, grads, sib, exch = state
        parts, recv = exch.finish(after)
        n = len(keys)
        own = _chip_sum(grads, sib, recv[:n], self.ids, f"chip_sum_{tag}")
        small_red = _small_sum(parts[n:], recv[n:]) if len(parts) > n else None
        return keys, _Exchange(f"swap_{tag}", own, [(o.shape, F32) for o in own], _send_to_sibling(n), n), small_red

    def _adamw(self, keys, swap, after):
        own, sib = swap.finish(after)
        last = None
        for (name, layer), o, s in zip(keys, own, sib):
            w = self.given[name]
            n_layers = w.shape[0] if w.ndim == 3 else 1
            c2 = w.shape[-1]
            res = _adamw_halves(w.reshape(-1, c2), o, s, self.given['m_' + name].reshape(-1, c2),
                                self.given['v_' + name].reshape(-1, c2), self.ids, f"adamw_{name}{layer}",
                                layer, n_layers, self.stacked.get(name))
            self.stacked[name] = res
            if layer == 0:
                for dst, r in zip(self.out, res):
                    dst[name] = r.reshape(w.shape)
            last = res[0]
        return last

    def early_ready(self, grads):
        self.e_keys, self.e_send = self._send("e", grads)
        return (self.e_send.token,)

    def after_ffn_half(self, after):
        self.e_state = self._exchange("e", self.e_keys, self.e_send, after)
        return (self.e_state[3].token,)

    def ffn0_ready(self, grads):
        _, self.e_swap, _ = self._swap("e", self.e_state, tuple(grads.values()))
        f_keys, f_send = self._send("f", grads)
        self.f_state = self._exchange("f", f_keys, f_send, ())
        return (self.f_state[3].token, self.e_swap.token)

    def finish(self, grads, small_grads, after):
        small_names = list(small_grads)
        a_keys, a_send = self._send("a", grads, [small_grads[k] for k in small_names])
        a_state = self._exchange("a", a_keys, a_send, after)
        e_done = self._adamw(self.e_keys, self.e_swap, (a_state[3].token,))
        f_keys, f_swap, _ = self._swap("f", self.f_state, (e_done,))
        f_done = self._adamw(f_keys, f_swap, ())
        _, a_swap, small_red = self._swap("a", a_state, (f_done,))
        self._adamw(a_keys, a_swap, ())

        given = self.given
        reduced = dict(zip(small_names, small_red))
        loss = reduced.pop('loss')[0, 0]
        names = list(reduced)
        items = []
        for k in names:
            g = reduced[k]
            cols = g.shape[1] // N_SHARD if k in ('a_norm_v', 'f_conv_w') else g.shape[1]
            view = lambda a: _lane_pad(a.reshape(g.shape[0], -1), cols)
            items.append((view(given[k]), g, view(given['m_' + k]), view(given['v_' + k])))
        res = _adamw_small(items, self.ids)
        for k, four in zip(names, res):
            width = given[k].size // four[0].shape[0]
            for dst, r in zip(self.out, four):
                dst[k] = r[:, :width].reshape(given[k].shape)
        return loss, self.out


def _lane_pad(a, cols):
    return a if a.shape[1] == cols else jnp.pad(a, ((0, 0), (0, cols - a.shape[1])))


def _small_add(a_list, b_list):
    n = len(a_list)

    def body(*refs):
        for t in range(n):
            refs[2 * n + t][...] = refs[t][...] + refs[n + t][...]

    return pl.pallas_call(body, name="chip_partial_small",
                          out_shape=[jax.ShapeDtypeStruct(a.shape, F32) for a in a_list])(*a_list, *b_list)


def _small_sum(parts, recvs):
    n = len(parts)

    def body(*refs):
        for t in range(n):
            q = refs[n + t]
            refs[2 * n + t][...] = (refs[t][...] + q[2]) + (q[0] + q[1])

    return pl.pallas_call(body, name="chip_sum_small",
                          out_shape=[jax.ShapeDtypeStruct(p.shape, F32) for p in parts])(*parts, *recvs)


def _adamw_small(items, ids):
    n = len(items)

    def body(ids_ref, *refs):
        for t in range(n):
            w_ref, g_ref, m_ref, v_ref = refs[4 * t:4 * t + 4]
            g_out, d_ref, mo_ref, vo_ref = refs[4 * n + 4 * t:4 * n + 4 * t + 4]
            g = g_ref[...]
            g_out[...] = g
            d_ref[...], mo_ref[...], vo_ref[...] = _adamw_math(w_ref[...], g, m_ref[...], v_ref[...])

    in_specs, out_specs, out_shape, args = [], [], [], []
    for w, g, m, v in items:
        full = pl.BlockSpec(w.shape, lambda i, ids_ref: (0, 0))
        g_spec = full if g.shape == w.shape else pl.BlockSpec(w.shape, lambda i, ids_ref: (0, ids_ref[1]))
        in_specs += [full, g_spec, full, full]
        out_specs += [full] * 4
        out_shape += [jax.ShapeDtypeStruct(w.shape, F32)] * 4
        args += [w, g, m, v]
    res = pl.pallas_call(
        body, name="adamw_small",
        grid_spec=pltpu.PrefetchScalarGridSpec(num_scalar_prefetch=1, grid=(1,), in_specs=in_specs,
                                               out_specs=out_specs),
        out_shape=out_shape, compiler_params=_params(),
    )(ids, *args)
    return [res[4 * t:4 * t + 4] for t in range(n)]
```

```python
import functools
import math

import numpy as np
import jax
import jax.numpy as jnp
from jax import lax
from jax.experimental import pallas as pl
from jax.experimental.pallas import tpu as pltpu

F32 = jnp.float32
BF16 = jnp.bfloat16

D_MODEL = 1024
CHUNK = 128
A_GROUPS = 8
HEAD_DIM = 64
N_Q_HEADS = 16
N_KV_HEADS = 4
GQA_GROUP = N_Q_HEADS // N_KV_HEADS
KV_DIM = N_KV_HEADS * HEAD_DIM
BLOCK = 128
D_FF = 2816
N_FF = 2 * D_FF
FF_BLK = N_FF // 4
PLE_DIM = 256
EPS = 1e-6
NEG = -1e30
N_SHARD = 4

ADAM_LR = 0.001
ADAM_B1 = 0.9
ADAM_B2 = 0.999
ADAM_EPS = 1e-08
ADAM_WD = 0.01
ADAM_STEP = 10

VMEM_LIMIT = 60 * 1024 * 1024
MESH = pl.DeviceIdType.MESH
ANY = pl.BlockSpec(memory_space=pl.ANY)
SMEM = pl.BlockSpec(memory_space=pltpu.SMEM)

_SLOPES = [float(np.float32(2.0 ** (-8.0 * (h + 1) / N_Q_HEADS))) for h in range(N_Q_HEADS)]


def _dot(a, b):
    return jnp.dot(a, b, preferred_element_type=F32)


def _dot_nt(a, b):
    return lax.dot_general(a, b, (((1,), (1,)), ((), ())), preferred_element_type=F32)


def _dot_tn(a, b):
    return lax.dot_general(a, b, (((0,), (0,)), ((), ())), preferred_element_type=F32)


def _rms(x, g):
    r = lax.rsqrt(jnp.mean(x * x, axis=-1, keepdims=True) + EPS)
    xh = x * r
    return xh * g, xh, r


def _rms_bwd(dy, xh, r, g):
    dxh = dy * g
    dg = jnp.sum(dy * xh, axis=0, keepdims=True)
    dx = r * (dxh - xh * jnp.mean(dxh * xh, axis=-1, keepdims=True))
    return dx, dg


_GELU_C = math.sqrt(2.0 / math.pi)


def _gelu(x):
    t = jnp.tanh(_GELU_C * (x + 0.044715 * (x * x * x)))
    return 0.5 * x * (1.0 + t)


def _gelu_grad(x):
    x2 = x * x
    t = jnp.tanh(_GELU_C * (x + 0.044715 * (x2 * x)))
    return 0.5 * (1.0 + t) + 0.5 * x * (1.0 - t * t) * (_GELU_C * (1.0 + 3.0 * 0.044715 * x2))


def _sigmoid(x):
    return 0.5 * jnp.tanh(0.5 * x) + 0.5


def _load_once(pairs, sem):
    @pl.when(pl.program_id(0) == 0)
    def _():
        cps = [pltpu.make_async_copy(s, d, sem.at[i]) for i, (s, d) in enumerate(pairs)]
        for cp in cps:
            cp.start()
        for cp in cps:
            cp.wait()


def _params(n_axes=1, vmem=VMEM_LIMIT):
    return pltpu.CompilerParams(dimension_semantics=("arbitrary",) * n_axes, vmem_limit_bytes=vmem)


def _row_spec(tm, n, rev_nt=None):
    if rev_nt is None:
        return pl.BlockSpec((tm, n), lambda i: (i, 0))
    return pl.BlockSpec((tm, n), lambda i: (rev_nt - 1 - i, 0))


def _const_spec(shape):
    nd = len(shape)
    return pl.BlockSpec(shape, lambda i: (0,) * nd)


def _add_deps(body, in_specs, args, deps):
    nd = len(deps)
    if nd == 0:
        return body, list(in_specs), list(args)

    def wrapped(*refs):
        return body(*refs[nd:])

    return wrapped, [ANY] * nd + list(in_specs), list(deps) + list(args)


def _zero_first(refs):
    @pl.when(pl.program_id(0) == 0)
    def _():
        for r in refs:
            r[...] = jnp.zeros(r.shape, r.dtype)


def _mixer_a_fwd(x, nmix, gv, wsm, bsb, w_in, w_out):
    T = x.shape[0]
    tm = min(512, T)
    nt = T // tm
    nw = 2 * D_MODEL // N_SHARD

    def body(x_ref, nmix_ref, gv_ref, ws_ref, bsb_ref, w_in_hbm, w_out_hbm,
             h1_ref, zp_ref, w_in_v, w_out_v, gated_v, sem):
        _load_once([(w_in_hbm, w_in_v), (w_out_hbm, w_out_v)], sem)
        xv = x_ref[...]
        xn = _rms(xv, nmix_ref[...])[0].astype(BF16)
        for j in range(N_SHARD):
            zp_ref[:, j * nw:(j + 1) * nw] = _dot(xn, w_in_v[j])
        z = _gelu(zp_ref[...])
        u = z[:, :D_MODEL]
        vn = _rms(z[:, D_MODEL:], gv_ref[...])[0].astype(BF16)
        for c in range(tm // CHUNK):
            rows = slice(c * CHUNK, (c + 1) * CHUNK)
            for h in range(A_GROUPS):
                cols = slice(h * CHUNK, (h + 1) * CHUNK)
                s = _dot(ws_ref[h], vn[rows, cols]) + bsb_ref[h]
                gated_v[rows, cols] = (u[rows, cols] * s).astype(BF16)
        h1_ref[...] = xv + _dot(gated_v[...], w_out_v[...])

    return pl.pallas_call(
        body, name="mixer_a_fwd", grid=(nt,),
        in_specs=[_row_spec(tm, D_MODEL), _const_spec((1, D_MODEL)), _const_spec((1, D_MODEL)),
                  _const_spec((A_GROUPS, CHUNK, CHUNK)), _const_spec((A_GROUPS, CHUNK, CHUNK)), ANY, ANY],
        out_specs=[_row_spec(tm, D_MODEL), _row_spec(tm, 2 * D_MODEL)],
        out_shape=[jax.ShapeDtypeStruct((T, D_MODEL), F32), jax.ShapeDtypeStruct((T, 2 * D_MODEL), F32)],
        scratch_shapes=[pltpu.VMEM((N_SHARD, D_MODEL, nw), BF16), pltpu.VMEM((D_MODEL, D_MODEL), BF16),
                        pltpu.VMEM((tm, D_MODEL), BF16), pltpu.SemaphoreType.DMA((2,))],
        compiler_params=_params(),
    )(x, nmix, gv, wsm, bsb, w_in, w_out)


def _mixer_a_bwd(dh, x, zp, nmix, gv, wsm, bsb, tril, w_in, w_out, deps=()):
    T = x.shape[0]
    tm = min(256, T)
    nt = T // tm
    nw = 2 * D_MODEL // N_SHARD

    def body(dh_ref, x_ref, zp_ref, nmix_ref, gv_ref, ws_ref, bsb_ref, tril_ref, w_in_hbm, w_out_hbm,
             dx_ref, dwin_ref, dwout_ref, dws_ref, dbs_ref, dgv_ref, dnmix_ref,
             w_in_v, w_out_v, du_v, dvn_v, dbs_v, gated_ref, sem):
        _load_once([(w_in_hbm, w_in_v), (w_out_hbm, w_out_v)], sem)
        _zero_first([dws_ref, dbs_v, dgv_ref, dnmix_ref, dwin_ref, dwout_ref])
        i = pl.program_id(0)
        dhv = dh_ref[...]
        dhb = dhv.astype(BF16)
        xv = x_ref[...]
        xn, xh, r = _rms(xv, nmix_ref[...])
        xnb = xn.astype(BF16)
        zpv = zp_ref[...]
        z = _gelu(zpv)
        u = z[:, :D_MODEL]
        vn_f, vh, rv = _rms(z[:, D_MODEL:], gv_ref[...])
        vn = vn_f.astype(BF16)
        dgated = _dot_nt(dhb, w_out_v[...])
        for c in range(tm // CHUNK):
            rows = slice(c * CHUNK, (c + 1) * CHUNK)
            for h in range(A_GROUPS):
                cols = slice(h * CHUNK, (h + 1) * CHUNK)
                vn_h = vn[rows, cols]
                s = _dot(ws_ref[h], vn_h) + bsb_ref[h]
                dgt = dgated[rows, cols]
                u_h = u[rows, cols]
                gated_ref[rows, cols] = (u_h * s).astype(BF16)
                du_v[rows, cols] = dgt * s
                ds = dgt * u_h
                dsb = ds.astype(BF16)
                dws_ref[h] += _dot_nt(dsb, vn_h)
                dbs_v[h] += ds
                dvn_v[rows, cols] = _dot_tn(ws_ref[h], dsb)
        dwout_ref[...] += _dot_tn(gated_ref[...], dhb)
        dv, dgv = _rms_bwd(dvn_v[...], vh, rv, gv_ref[...])
        dgv_ref[...] += dgv
        dzu = (du_v[...] * _gelu_grad(zpv[:, :D_MODEL])).astype(BF16)
        dzv = (dv * _gelu_grad(zpv[:, D_MODEL:])).astype(BF16)
        dzs = (dzu[:, :nw], dzu[:, nw:], dzv[:, :nw], dzv[:, nw:])
        dxn = jnp.zeros((tm, D_MODEL), F32)
        for j in range(N_SHARD):
            dxn = dxn + _dot_nt(dzs[j], w_in_v[j])
            dwin_ref[j] += _dot_tn(xnb, dzs[j])
        dxx, dn = _rms_bwd(dxn, xh, r, nmix_ref[...])
        dnmix_ref[...] += dn
        dx_ref[...] = dhv + dxx

        @pl.when(i == nt - 1)
        def _():
            for h in range(A_GROUPS):
                dws_ref[h] = dws_ref[h] * tril_ref[...]
                dbs_ref[h] = jnp.broadcast_to(jnp.sum(dbs_v[h], axis=1, keepdims=True), (CHUNK, CHUNK))

    grp = (A_GROUPS, CHUNK, CHUNK)
    body, in_specs, args = _add_deps(
        body, [_row_spec(tm, D_MODEL), _row_spec(tm, D_MODEL), _row_spec(tm, 2 * D_MODEL),
               _const_spec((1, D_MODEL)), _const_spec((1, D_MODEL)), _const_spec(grp), _const_spec(grp),
               _const_spec((CHUNK, CHUNK)), ANY, ANY],
        [dh, x, zp, nmix, gv, wsm, bsb, tril, w_in, w_out], deps)
    return pl.pallas_call(
        body, name="mixer_a_bwd", grid=(nt,), in_specs=in_specs,
        out_specs=[_row_spec(tm, D_MODEL), _const_spec((N_SHARD, D_MODEL, nw)), _const_spec((D_MODEL, D_MODEL)),
                   _const_spec(grp), _const_spec(grp), _const_spec((1, D_MODEL)), _const_spec((1, D_MODEL))],
        out_shape=[jax.ShapeDtypeStruct((T, D_MODEL), F32), jax.ShapeDtypeStruct((N_SHARD, D_MODEL, nw), F32),
                   jax.ShapeDtypeStruct((D_MODEL, D_MODEL), F32),
                   jax.ShapeDtypeStruct(grp, F32), jax.ShapeDtypeStruct(grp, F32),
                   jax.ShapeDtypeStruct((1, D_MODEL), F32), jax.ShapeDtypeStruct((1, D_MODEL), F32)],
        scratch_shapes=[pltpu.VMEM((N_SHARD, D_MODEL, nw), BF16), pltpu.VMEM((D_MODEL, D_MODEL), BF16),
                        pltpu.VMEM((tm, D_MODEL), F32), pltpu.VMEM((tm, D_MODEL), F32),
                        pltpu.VMEM(grp, F32), pltpu.VMEM((tm, D_MODEL), BF16), pltpu.SemaphoreType.DMA((2,))],
        compiler_params=_params(),
    )(*args)


def _load_ffn_weights(w_up_hbm, w_dn_hbm, layer, w_up_v, w_dn_v, sem):
    _load_once([(w_up_hbm, w_up_v), (w_dn_hbm, w_dn_v)], sem)


def _ffn_fwd(h, nffn, cw, cb, w_up, w_dn, layer):
    T = h.shape[0]
    tm = min(256, T)
    nt = T // tm

    def body(h_ref, n_ref, cw_ref, cb_ref, w_up_hbm, w_dn_hbm, out_ref, hh_ref, c_ref,
             w_up_v, w_dn_v, carry_v, sem):
        _load_ffn_weights(w_up_hbm, w_dn_hbm, layer, w_up_v, w_dn_v, sem)
        _zero_first([carry_v])
        xv = h_ref[...]
        xf = _rms(xv, n_ref[...])[0].astype(BF16)
        acc = xv
        for j in range(2):
            cs = []
            for blk in (j, j + 2):
                cols = slice(blk * FF_BLK, (blk + 1) * FF_BLK)
                hh = _dot(xf, w_up_v[blk])
                hh_ref[:, cols] = hh.astype(BF16)
                ext = jnp.concatenate([carry_v[blk], hh], axis=0)
                carry_v[blk] = hh[tm - 8:, :]
                s1 = pltpu.roll(ext, 1, 0)[8:]
                s2 = pltpu.roll(ext, 2, 0)[8:]
                cv = (cb_ref[:, cols] + cw_ref[0:1, cols] * s2 + cw_ref[1:2, cols] * s1
                      + cw_ref[2:3, cols] * hh)
                c_ref[:, cols] = cv.astype(BF16)
                cs.append(cv)
            act = (cs[0] * _sigmoid(cs[0]) * cs[1]).astype(BF16)
            acc = acc + _dot(act, w_dn_v[j * FF_BLK:(j + 1) * FF_BLK, :])
        out_ref[...] = acc

    return pl.pallas_call(
        body, name=f"ffn_fwd{layer}", grid=(nt,),
        in_specs=[_row_spec(tm, D_MODEL), _const_spec((1, D_MODEL)), _const_spec((3, N_FF)),
                  _const_spec((1, N_FF)), ANY, ANY],
        out_specs=[_row_spec(tm, D_MODEL), _row_spec(tm, N_FF), _row_spec(tm, N_FF)],
        out_shape=[jax.ShapeDtypeStruct((T, D_MODEL), F32), jax.ShapeDtypeStruct((T, N_FF), BF16),
                   jax.ShapeDtypeStruct((T, N_FF), BF16)],
        scratch_shapes=[pltpu.VMEM((N_SHARD, D_MODEL, FF_BLK), BF16), pltpu.VMEM((D_FF, D_MODEL), BF16),
                        pltpu.VMEM((N_SHARD, 8, FF_BLK), F32), pltpu.SemaphoreType.DMA((2 * N_SHARD,))],
        compiler_params=_params(),
    )(h, nffn, cw, cb, w_up, w_dn)


def _wgrad(a, b, bn, col_sharded, name, deps=()):
    T, K = a.shape
    N = b.shape[1]
    tt = min(2048, T)
    nn, ntt = N // bn, T // tt
    kr = K // N_SHARD

    def body(a_ref, b_ref, o_ref):
        @pl.when(pl.program_id(1) == 0)
        def _():
            o_ref[...] = jnp.zeros(o_ref.shape, F32)
        d = _dot_tn(a_ref[...].astype(BF16), b_ref[...].astype(BF16))
        if col_sharded:
            o_ref[...] += d
        else:
            for j in range(N_SHARD):
                o_ref[j] += d[j * kr:(j + 1) * kr]

    if col_sharded:
        assert nn == N_SHARD
        out_spec = pl.BlockSpec((None, K, bn), lambda n, t: (n, 0, 0))
        out_shape = jax.ShapeDtypeStruct((N_SHARD, K, bn), F32)
    else:
        out_spec = pl.BlockSpec((N_SHARD, kr, bn), lambda n, t: (0, 0, n))
        out_shape = jax.ShapeDtypeStruct((N_SHARD, kr, N), F32)
    body, in_specs, args = _add_deps(
        body, [pl.BlockSpec((tt, K), lambda n, t: (t, 0)), pl.BlockSpec((tt, bn), lambda n, t: (t, n))],
        [a, b], deps)
    return pl.pallas_call(
        body, name=name, grid=(nn, ntt), in_specs=in_specs, out_specs=out_spec, out_shape=out_shape,
        compiler_params=pltpu.CompilerParams(dimension_semantics=("arbitrary",) * 2, vmem_limit_bytes=VMEM_LIMIT),
    )(*args)


def _ffn_bwd(dh, h, hh, c, nffn, cw, w_up, w_dn, layer, deps=(), between=None):
    T = h.shape[0]
    tm = min(256, T)
    nt = T // tm

    def body(dh_ref, h_ref, hh_ref, c_ref, n_ref, cw_ref, w_up_hbm, w_dn_hbm,
             dhin_ref, act_ref, dhh_ref, xf_ref, dcw_ref, dcb_ref, dn_ref,
             w_up_v, w_dn_v, carry_v, sem):
        _load_ffn_weights(w_up_hbm, w_dn_hbm, layer, w_up_v, w_dn_v, sem)
        _zero_first([carry_v, dcw_ref, dcb_ref, dn_ref])
        dout = dh_ref[...]
        doutb = dout.astype(BF16)
        xf_f, xh, r = _rms(h_ref[...], n_ref[...])
        xf_ref[...] = xf_f.astype(BF16)
        dxf = jnp.zeros((tm, D_MODEL), F32)
        for j in range(2):
            blks = (j, j + 2)
            cg = c_ref[:, j * FF_BLK:(j + 1) * FF_BLK].astype(F32)
            cu = c_ref[:, (j + 2) * FF_BLK:(j + 3) * FF_BLK].astype(F32)
            sg = _sigmoid(cg)
            sil = cg * sg
            act_ref[:, j * FF_BLK:(j + 1) * FF_BLK] = (sil * cu).astype(BF16)
            dact = _dot_nt(doutb, w_dn_v[j * FF_BLK:(j + 1) * FF_BLK, :])
            dcs = (dact * cu * (sg * (1.0 + cg * (1.0 - sg))), dact * sil)
            for blk, dc in zip(blks, dcs):
                cols = slice(blk * FF_BLK, (blk + 1) * FF_BLK)
                hhv = hh_ref[:, cols].astype(F32)
                ext = jnp.concatenate([dc, carry_v[blk]], axis=0)
                carry_v[blk] = dc[:8, :]
                n = tm + 8
                a1 = pltpu.roll(ext, n - 1, 0)[:tm]
                a2 = pltpu.roll(ext, n - 2, 0)[:tm]
                dcb_ref[:, cols] += jnp.sum(dc, axis=0, keepdims=True)
                dcw_ref[0:1, cols] += jnp.sum(a2 * hhv, axis=0, keepdims=True)
                dcw_ref[1:2, cols] += jnp.sum(a1 * hhv, axis=0, keepdims=True)
                dcw_ref[2:3, cols] += jnp.sum(dc * hhv, axis=0, keepdims=True)
                dhh = (cw_ref[2:3, cols] * dc + cw_ref[1:2, cols] * a1 + cw_ref[0:1, cols] * a2).astype(BF16)
                dhh_ref[:, cols] = dhh
                dxf = dxf + _dot_nt(dhh, w_up_v[blk])
        dxx, dn = _rms_bwd(dxf, xh, r, n_ref[...])
        dn_ref[...] += dn
        dhin_ref[...] = dout + dxx

    rev = functools.partial(_row_spec, rev_nt=nt)
    body, in_specs, args = _add_deps(
        body, [rev(tm, D_MODEL), rev(tm, D_MODEL), rev(tm, N_FF), rev(tm, N_FF),
               _const_spec((1, D_MODEL)), _const_spec((3, N_FF)), ANY, ANY],
        [dh, h, hh, c, nffn, cw, w_up, w_dn], deps)
    dhin, act, dhh, xf, dcw, dcb, dn = pl.pallas_call(
        body, name=f"ffn_bwd{layer}", grid=(nt,), in_specs=in_specs,
        out_specs=[rev(tm, D_MODEL), rev(tm, D_FF), rev(tm, N_FF), rev(tm, D_MODEL),
                   _const_spec((3, N_FF)), _const_spec((1, N_FF)), _const_spec((1, D_MODEL))],
        out_shape=[jax.ShapeDtypeStruct((T, D_MODEL), F32), jax.ShapeDtypeStruct((T, D_FF), BF16),
                   jax.ShapeDtypeStruct((T, N_FF), BF16), jax.ShapeDtypeStruct((T, D_MODEL), BF16),
                   jax.ShapeDtypeStruct((3, N_FF), F32), jax.ShapeDtypeStruct((1, N_FF), F32),
                   jax.ShapeDtypeStruct((1, D_MODEL), F32)],
        scratch_shapes=[pltpu.VMEM((N_SHARD, D_MODEL, FF_BLK), BF16), pltpu.VMEM((D_FF, D_MODEL), BF16),
                        pltpu.VMEM((N_SHARD, 8, FF_BLK), F32), pltpu.SemaphoreType.DMA((2 * N_SHARD,))],
        compiler_params=_params(),
    )(*args)
    deps2 = between(dhin) if between is not None else ()
    dwdn = _wgrad(act, dh, D_MODEL // 2, False, f"wgrad_ffn_down{layer}", deps=deps2)
    dwup = _wgrad(xf, dhh, FF_BLK, True, f"wgrad_ffn_up{layer}", deps=deps2)
    return dhin, dwup, dwdn, dcw, dcb, dn


def _load_ple_weights(w_pin_hbm, w_gate_hbm, layer, w_pin_v, w_gate_v, sem, extra=()):
    _load_once([(w_pin_hbm, w_pin_v), (w_gate_hbm, w_gate_v)] + list(extra), sem)


def _ple_terms(xv, p_ref, n_ref, bg_ref, w_pin_v, w_gate_v, pe_v):
    pw = D_MODEL // N_SHARD
    xg, xh, r = _rms(xv, n_ref[...])
    xgb = xg.astype(BF16)
    gate = _sigmoid(_dot(xgb, w_gate_v[...]) + bg_ref[...])
    pb = p_ref[...].astype(BF16)
    for j in range(N_SHARD):
        pe_v[:, j * pw:(j + 1) * pw] = _dot(pb, w_pin_v[j])
    pe = pe_v[...]
    return pe * gate, pe, gate, xgb, xh, r


def _ple_fwd_kv(h, p, nple, bg, nkv, w_pin, w_gate, w_kv):
    T = h.shape[0]
    tm = min(512, T)
    nt = T // tm
    pw = D_MODEL // N_SHARD

    def body(h_ref, p_ref, n_ref, bg_ref, nkv_ref, w_pin_hbm, w_gate_hbm, w_kv_hbm,
             out_ref, kv_ref, w_pin_v, w_gate_v, w_kv_v, pe_v, sem):
        _load_ple_weights(w_pin_hbm, w_gate_hbm, 0, w_pin_v, w_gate_v, sem, [(w_kv_hbm, w_kv_v)])
        xv = h_ref[...]
        hn = xv + _ple_terms(xv, p_ref, n_ref, bg_ref, w_pin_v, w_gate_v, pe_v)[0]
        out_ref[...] = hn
        kvn = _rms(hn, nkv_ref[...])[0].astype(BF16)
        kv_ref[...] = _dot(kvn, w_kv_v[...]).astype(BF16)

    vec = _const_spec((1, D_MODEL))
    return pl.pallas_call(
        body, name="ple_fwd0", grid=(nt,),
        in_specs=[_row_spec(tm, D_MODEL), _row_spec(tm, PLE_DIM), vec, vec, vec, ANY, ANY, ANY],
        out_specs=[_row_spec(tm, D_MODEL), _row_spec(tm, 2 * KV_DIM)],
        out_shape=[jax.ShapeDtypeStruct((T, D_MODEL), F32), jax.ShapeDtypeStruct((T, 2 * KV_DIM), BF16)],
        scratch_shapes=[pltpu.VMEM((N_SHARD, PLE_DIM, pw), BF16), pltpu.VMEM((D_MODEL, D_MODEL), BF16),
                        pltpu.VMEM((D_MODEL, 2 * KV_DIM), BF16), pltpu.VMEM((tm, D_MODEL), F32),
                        pltpu.SemaphoreType.DMA((2 * N_SHARD + 1,))],
        compiler_params=_params(),
    )(h, p, nple, bg, nkv, w_pin, w_gate, w_kv)


def _ple_fwd_final(h, p, tgt, nple, bg, nfin, w_pin, w_gate):
    T = h.shape[0]
    tm = min(512, T)
    nt = T // tm
    pw = D_MODEL // N_SHARD

    def body(h_ref, p_ref, t_ref, n_ref, bg_ref, nf_ref, w_pin_hbm, w_gate_hbm,
             dh_ref, loss_ref, dnf_ref, w_pin_v, w_gate_v, pe_v, sem):
        _load_ple_weights(w_pin_hbm, w_gate_hbm, 1, w_pin_v, w_gate_v, sem)
        _zero_first([loss_ref, dnf_ref])
        xv = h_ref[...]
        hn = xv + _ple_terms(xv, p_ref, n_ref, bg_ref, w_pin_v, w_gate_v, pe_v)[0]
        y, yh, r = _rms(hn, nf_ref[...])
        diff = y - t_ref[...]
        loss_ref[...] += 0.5 * jnp.sum(jnp.mean(diff * diff, axis=-1, keepdims=True))
        dy = diff * (1.0 / D_MODEL)
        dhn, dnf = _rms_bwd(dy, yh, r, nf_ref[...])
        dnf_ref[...] += dnf
        dh_ref[...] = dhn

    vec = _const_spec((1, D_MODEL))
    return pl.pallas_call(
        body, name="ple_fwd1", grid=(nt,),
        in_specs=[_row_spec(tm, D_MODEL), _row_spec(tm, PLE_DIM), _row_spec(tm, D_MODEL), vec, vec, vec, ANY, ANY],
        out_specs=[_row_spec(tm, D_MODEL), _const_spec((8, 128)), vec],
        out_shape=[jax.ShapeDtypeStruct((T, D_MODEL), F32), jax.ShapeDtypeStruct((8, 128), F32),
                   jax.ShapeDtypeStruct((1, D_MODEL), F32)],
        scratch_shapes=[pltpu.VMEM((N_SHARD, PLE_DIM, pw), BF16), pltpu.VMEM((D_MODEL, D_MODEL), BF16),
                        pltpu.VMEM((tm, D_MODEL), F32), pltpu.SemaphoreType.DMA((2 * N_SHARD,))],
        compiler_params=_params(),
    )(h, p, tgt, nple, bg, nfin, w_pin, w_gate)


def _ple_bwd(dh, hb, p, nple, bg, w_pin, w_gate, layer, kv_args=None):
    T = hb.shape[0]
    tm = min(512, T)
    nt = T // tm
    with_kv = kv_args is not None
    pw = D_MODEL // N_SHARD

    def body(*refs):
        if with_kv:
            (dh_ref, hb_ref, p_ref, n_ref, bg_ref, w_pin_hbm, w_gate_hbm, hc_ref, dkv_ref, nkv_ref, w_kv_hbm,
             dhb_ref, dwpin_ref, dwgate_ref, dbg_ref, dn_ref, dwkv_ref, dnkv_ref,
             w_pin_v, w_gate_v, pe_v, w_kv_v, sem) = refs
        else:
            (dh_ref, hb_ref, p_ref, n_ref, bg_ref, w_pin_hbm, w_gate_hbm,
             dhb_ref, dwpin_ref, dwgate_ref, dbg_ref, dn_ref, w_pin_v, w_gate_v, pe_v, sem) = refs
        pairs = [(w_pin_hbm, w_pin_v), (w_gate_hbm, w_gate_v)]
        if with_kv:
            pairs.append((w_kv_hbm, w_kv_v))
        _load_once(pairs, sem)
        _zero_first([dwpin_ref, dwgate_ref, dbg_ref, dn_ref] + ([dwkv_ref, dnkv_ref] if with_kv else []))
        do = dh_ref[...]
        if with_kv:
            dkvb = dkv_ref[...].astype(BF16)
            dkvn = _dot_nt(dkvb, w_kv_v[...])
            kvn, kh, kr = _rms(hc_ref[...], nkv_ref[...])
            dwkv_ref[...] += _dot_tn(kvn.astype(BF16), dkvb)
            dk, dnkv = _rms_bwd(dkvn, kh, kr, nkv_ref[...])
            dnkv_ref[...] += dnkv
            do = do + dk
        _, pe, gate, xgb, xh, r = _ple_terms(hb_ref[...], p_ref, n_ref, bg_ref, w_pin_v, w_gate_v, pe_v)
        dpe = (do * gate).astype(BF16)
        pb = p_ref[...].astype(BF16)
        for j in range(N_SHARD):
            dwpin_ref[j] += _dot_tn(pb, dpe[:, j * pw:(j + 1) * pw])
        da = do * pe * (gate * (1.0 - gate))
        dab = da.astype(BF16)
        dbg_ref[...] += jnp.sum(da, axis=0, keepdims=True)
        dxg = _dot_nt(dab, w_gate_v[...])
        dwgate_ref[...] += _dot_tn(xgb, dab)
        dxx, dn = _rms_bwd(dxg, xh, r, n_ref[...])
        dn_ref[...] += dn
        dhb_ref[...] = do + dxx

    vec = _const_spec((1, D_MODEL))
    row = _row_spec(tm, D_MODEL)
    in_specs = [row, row, _row_spec(tm, PLE_DIM), vec, vec, ANY, ANY]
    args = [dh, hb, p, nple, bg, w_pin, w_gate]
    out_specs = [row, _const_spec((N_SHARD, PLE_DIM, pw)), _const_spec((D_MODEL, D_MODEL)), vec, vec]
    out_shape = [jax.ShapeDtypeStruct((T, D_MODEL), F32), jax.ShapeDtypeStruct((N_SHARD, PLE_DIM, pw), F32),
                 jax.ShapeDtypeStruct((D_MODEL, D_MODEL), F32),
                 jax.ShapeDtypeStruct((1, D_MODEL), F32), jax.ShapeDtypeStruct((1, D_MODEL), F32)]
    scratch = [pltpu.VMEM((N_SHARD, PLE_DIM, pw), BF16), pltpu.VMEM((D_MODEL, D_MODEL), BF16),
               pltpu.VMEM((tm, D_MODEL), F32)]
    if with_kv:
        hc, dkv, nkv, w_kv = kv_args
        in_specs += [row, _row_spec(tm, 2 * KV_DIM), vec, ANY]
        args += [hc, dkv, nkv, w_kv]
        out_specs += [_const_spec((D_MODEL, 2 * KV_DIM)), vec]
        out_shape += [jax.ShapeDtypeStruct((D_MODEL, 2 * KV_DIM), F32), jax.ShapeDtypeStruct((1, D_MODEL), F32)]
        scratch.append(pltpu.VMEM((D_MODEL, 2 * KV_DIM), BF16))
    scratch.append(pltpu.SemaphoreType.DMA((3,)))
    return pl.pallas_call(
        body, name=f"ple_bwd{layer}", grid=(nt,), in_specs=in_specs, out_specs=out_specs,
        out_shape=out_shape, scratch_shapes=scratch, compiler_params=_params(),
    )(*args)


GROUP_ROWS = GQA_GROUP * BLOCK


def _stack_heads(x, kh):
    return jnp.concatenate([x[:, (kh * GQA_GROUP + g) * HEAD_DIM:(kh * GQA_GROUP + g + 1) * HEAD_DIM]
                            for g in range(GQA_GROUP)], axis=0)


def _attn_fwd(h, nmix, kv, sinks, w_q, w_o):
    T = h.shape[0]
    tm = min(512, T)
    nt = T // tm
    nb = tm // BLOCK

    def body(h_ref, n_ref, kv_ref, kvp_ref, sink_ref, w_q_hbm, w_o_hbm,
             out_ref, q_ref, ao_ref, p_ref, psink_ref, w_q_v, w_o_v, kvs_v, sem):
        _load_once([(w_q_hbm, w_q_v), (w_o_hbm, w_o_v)], sem)
        ti = pl.program_id(0)
        xv = h_ref[...]
        xn = _rms(xv, n_ref[...])[0].astype(BF16)
        q_ref[...] = (_dot(xn, w_q_v[...]) * (HEAD_DIM ** -0.5)).astype(BF16)
        kvs_v[0:BLOCK, :] = kvp_ref[...]
        kvs_v[BLOCK:, :] = kv_ref[...]
        lane = lax.broadcasted_iota(jnp.int32, (BLOCK, 128), 1)
        ii = lax.broadcasted_iota(jnp.int32, (BLOCK, 2 * BLOCK), 0)
        jj = lax.broadcasted_iota(jnp.int32, (BLOCK, 2 * BLOCK), 1)
        dist = ii + BLOCK - jj
        inband = (dist >= 0) & (dist < BLOCK)
        distf = dist.astype(F32)

        def blk_body(b, carry):
            r0 = pl.multiple_of(b * BLOCK, BLOCK)
            valid = inband & ((jj >= BLOCK) | jnp.logical_not(jnp.logical_and(ti == 0, b == 0)))
            qb = q_ref[pl.ds(r0, BLOCK), :]
            band = kvs_v[pl.ds(r0, 2 * BLOCK), :]
            psink_mat = jnp.zeros((BLOCK, 128), F32)
            outs = []
            for hq in range(N_Q_HEADS):
                kh, g = divmod(hq, GQA_GROUP)
                k_h = band[:, kh * HEAD_DIM:(kh + 1) * HEAD_DIM]
                v_h = band[:, KV_DIM + kh * HEAD_DIM:KV_DIM + (kh + 1) * HEAD_DIM]
                s = _dot_nt(qb[:, hq * HEAD_DIM:(hq + 1) * HEAD_DIM], k_h) - _SLOPES[hq] * distf
                s = jnp.where(valid, s, NEG)
                sink = sink_ref[hq]
                m = jnp.maximum(jnp.max(s, axis=1, keepdims=True), sink)
                e = jnp.exp(s - m)
                esink = jnp.exp(sink - m)
                inv = 1.0 / (jnp.sum(e, axis=1, keepdims=True) + esink)
                pb = (e * inv).astype(BF16)
                p_ref[b, kh, g * BLOCK:(g + 1) * BLOCK, :] = pb
                outs.append(_dot(pb, v_h))
                psink_mat = jnp.where(lane == hq, esink * inv, psink_mat)
            ao_ref[pl.ds(r0, BLOCK), :] = jnp.concatenate(outs, axis=1).astype(BF16)
            psink_ref[pl.ds(r0, BLOCK), :] = psink_mat
            return carry

        lax.fori_loop(0, nb, blk_body, 0)
        out_ref[...] = xv + _dot(ao_ref[...], w_o_v[...])

    row = _row_spec(tm, D_MODEL)
    prev_spec = pl.BlockSpec((BLOCK, 2 * KV_DIM), lambda i: (jnp.maximum(i * nb - 1, 0), 0))
    return pl.pallas_call(
        body, name="attn_fwd", grid=(nt,),
        in_specs=[row, _const_spec((1, D_MODEL)), _row_spec(tm, 2 * KV_DIM), prev_spec, SMEM, ANY, ANY],
        out_specs=[row, row, row, pl.BlockSpec((nb, N_KV_HEADS, GROUP_ROWS, 2 * BLOCK), lambda i: (i, 0, 0, 0)),
                   _row_spec(tm, 128)],
        out_shape=[jax.ShapeDtypeStruct((T, D_MODEL), F32), jax.ShapeDtypeStruct((T, D_MODEL), BF16),
                   jax.ShapeDtypeStruct((T, D_MODEL), BF16),
                   jax.ShapeDtypeStruct((T // BLOCK, N_KV_HEADS, GROUP_ROWS, 2 * BLOCK), BF16),
                   jax.ShapeDtypeStruct((T, 128), F32)],
        scratch_shapes=[pltpu.VMEM((D_MODEL, D_MODEL), BF16), pltpu.VMEM((D_MODEL, D_MODEL), BF16),
                        pltpu.VMEM((tm + BLOCK, 2 * KV_DIM), BF16), pltpu.SemaphoreType.DMA((2,))],
        compiler_params=_params(),
    )(h, nmix, kv, kv, sinks, w_q, w_o)


def _attn_bwd(dh, h, q, kv, ao, p, psink, nmix, w_q, w_o):
    T = h.shape[0]
    tm = min(512, T)
    nt = T // tm
    nb = tm // BLOCK

    def body(dh_ref, h_ref, q_ref, kv_ref, kvp_ref, ao_ref, p_ref, psink_ref, n_ref, w_q_hbm, w_o_hbm,
             dhin_ref, dwq_ref, dwo_ref, dkv_ref, dsink_ref, dn_ref,
             w_q_v, w_o_v, kvs_v, dao_v, dq_v, dkv_v, carry_v, sem):
        _load_once([(w_q_hbm, w_q_v), (w_o_hbm, w_o_v)], sem)
        _zero_first([carry_v, dsink_ref, dn_ref, dwq_ref, dwo_ref])
        dout = dh_ref[...]
        doutb = dout.astype(BF16)
        dao_v[...] = _dot_nt(doutb, w_o_v[...])
        dwo_ref[...] += _dot_tn(ao_ref[...], doutb)
        kvs_v[0:BLOCK, :] = kvp_ref[...]
        kvs_v[BLOCK:, :] = kv_ref[...]
        dkv_v[0:tm, :] = jnp.zeros((tm, 2 * KV_DIM), F32)
        dkv_v[tm:, :] = carry_v[...]
        seg = (lax.broadcasted_iota(jnp.int32, (D_MODEL, 128), 0) // HEAD_DIM
               == lax.broadcasted_iota(jnp.int32, (D_MODEL, 128), 1)).astype(BF16)

        def blk_body(b, dsk):
            r0 = pl.multiple_of(b * BLOCK, BLOCK)
            qb = q_ref[pl.ds(r0, BLOCK), :]
            band = kvs_v[pl.ds(r0, 2 * BLOCK), :]
            aob = ao_ref[pl.ds(r0, BLOCK), :].astype(F32)
            daob = dao_v[pl.ds(r0, BLOCK), :]
            prod = daob * aob
            head = prod.astype(BF16)
            tail = (prod - head.astype(F32)).astype(BF16)
            dsk = dsk + psink_ref[pl.ds(r0, BLOCK), :] * (_dot(head, seg) + _dot(tail, seg))
            dqs = []
            dks = []
            dvs = []
            for kh in range(N_KV_HEADS):
                k_h = band[:, kh * HEAD_DIM:(kh + 1) * HEAD_DIM]
                v_h = band[:, KV_DIM + kh * HEAD_DIM:KV_DIM + (kh + 1) * HEAD_DIM]
                q_g = _stack_heads(qb, kh)
                dao_g = _stack_heads(daob, kh)
                prb = p_ref[b, kh]
                pr = prb.astype(F32)
                dd = jnp.sum(dao_g * _stack_heads(aob, kh), axis=1, keepdims=True)
                dao_gb = dao_g.astype(BF16)
                dp = _dot_nt(dao_gb, v_h)
                dsb = (pr * (dp - dd)).astype(BF16)
                dq_g = _dot(dsb, k_h) * (HEAD_DIM ** -0.5)
                dks.append(_dot_tn(dsb, q_g))
                dvs.append(_dot_tn(prb, dao_gb))
                for g in range(GQA_GROUP):
                    dqs.append(dq_g[g * BLOCK:(g + 1) * BLOCK])
            dq_v[pl.ds(r0, BLOCK), :] = jnp.concatenate(dqs, axis=1)
            dkv_v[pl.ds(r0, 2 * BLOCK), :] += jnp.concatenate(dks + dvs, axis=1)
            return dsk

        dsk = lax.fori_loop(0, nb, blk_body, jnp.zeros((BLOCK, 128), F32))
        dsink_ref[...] -= jnp.sum(dsk, axis=0, keepdims=True)
        dqb = dq_v[...].astype(BF16)
        dxn = _dot_nt(dqb, w_q_v[...])
        xn, xh, r = _rms(h_ref[...], n_ref[...])
        dwq_ref[...] += _dot_tn(xn.astype(BF16), dqb)
        dxx, dn = _rms_bwd(dxn, xh, r, n_ref[...])
        dn_ref[...] += dn
        dhin_ref[...] = dout + dxx
        dkv_ref[...] = dkv_v[BLOCK:, :]
        carry_v[...] = dkv_v[0:BLOCK, :]

    rev = functools.partial(_row_spec, rev_nt=nt)
    row = rev(tm, D_MODEL)
    prev_spec = pl.BlockSpec((BLOCK, 2 * KV_DIM), lambda i: (jnp.maximum((nt - 1 - i) * nb - 1, 0), 0))
    return pl.pallas_call(
        body, name="attn_bwd", grid=(nt,),
        in_specs=[row, row, row, rev(tm, 2 * KV_DIM), prev_spec, row,
                  pl.BlockSpec((nb, N_KV_HEADS, GROUP_ROWS, 2 * BLOCK), lambda i: (nt - 1 - i, 0, 0, 0)),
                  rev(tm, 128), _const_spec((1, D_MODEL)), ANY, ANY],
        out_specs=[row, _const_spec((D_MODEL, D_MODEL)), _const_spec((D_MODEL, D_MODEL)), rev(tm, 2 * KV_DIM),
                   _const_spec((8, 128)), _const_spec((1, D_MODEL))],
        out_shape=[jax.ShapeDtypeStruct((T, D_MODEL), F32), jax.ShapeDtypeStruct((D_MODEL, D_MODEL), F32),
                   jax.ShapeDtypeStruct((D_MODEL, D_MODEL), F32), jax.ShapeDtypeStruct((T, 2 * KV_DIM), F32),
                   jax.ShapeDtypeStruct((8, 128), F32), jax.ShapeDtypeStruct((1, D_MODEL), F32)],
        scratch_shapes=[pltpu.VMEM((D_MODEL, D_MODEL), BF16), pltpu.VMEM((D_MODEL, D_MODEL), BF16),
                        pltpu.VMEM((tm + BLOCK, 2 * KV_DIM), BF16), pltpu.VMEM((tm, D_MODEL), F32),
                        pltpu.VMEM((tm, D_MODEL), F32), pltpu.VMEM((tm + BLOCK, 2 * KV_DIM), F32),
                        pltpu.VMEM((BLOCK, 2 * KV_DIM), F32), pltpu.SemaphoreType.DMA((2,))],
        compiler_params=_params(),
    )(dh, h, q, kv, kv, ao, p, psink, nmix, w_q, w_o)


def _mesh_pos():
    return lax.axis_index("x"), lax.axis_index("y"), lax.axis_index("c")


def _other_chips(x, y):
    return [(1 - x, y), (x, 1 - y), (1 - x, 1 - y)]


HBM_SPEC = pl.BlockSpec(memory_space=pltpu.HBM)
SEM_SPEC = pl.BlockSpec(memory_space=pltpu.SEMAPHORE)


def _split_call(name, bufs, waits=(), starts=(), after=()):
    n, nw, ns, na = len(bufs), len(waits), len(starts), len(after)

    def body(*refs):
        brefs = refs[:n]
        wsems = [(refs[n + 2 * k], refs[n + 2 * k + 1]) for k in range(nw)]
        o = n + 2 * nw + na
        ssems = [(refs[o + 2 * k], refs[o + 2 * k + 1]) for k in range(ns)]
        for (ss, rs), (_, _, fn) in zip(wsems, waits):
            for sending, arriving in fn(brefs, ss, rs):
                sending.wait_send()
                arriving.wait_recv()
        for (ss, rs), (_, fn) in zip(ssems, starts):
            for sending, _ in fn(brefs, ss, rs):
                sending.start()
        if ns:
            token = refs[o + 2 * ns + n]
            token[...] = jnp.zeros(token.shape, token.dtype)

    out_shape, out_specs = [], []
    for cnt, _ in starts:
        out_shape += [pltpu.SemaphoreType.DMA((cnt,)), pltpu.SemaphoreType.DMA((cnt,))]
        out_specs += [SEM_SPEC, SEM_SPEC]
    out_shape += [pltpu.HBM(b.shape, b.dtype) for b in bufs]
    out_specs += [HBM_SPEC] * n
    if ns:
        out_shape.append(jax.ShapeDtypeStruct((8, 128), F32))
        out_specs.append(pl.BlockSpec(memory_space=pltpu.VMEM))
    args = [pltpu.with_memory_space_constraint(b, pltpu.HBM) for b in bufs]
    for ss, rs, _ in waits:
        args += [ss, rs]
    args += list(after)
    res = pl.pallas_call(
        body, name=name, out_shape=tuple(out_shape),
        in_specs=[HBM_SPEC] * n + [SEM_SPEC] * (2 * nw) + [ANY] * na, out_specs=tuple(out_specs),
        input_output_aliases={i: 2 * ns + i for i in range(n)},
        compiler_params=pltpu.CompilerParams(has_side_effects=pltpu.SideEffectType.DATAFLOW_SIDE_EFFECTING),
    )(*args)
    sems = [(res[2 * k], res[2 * k + 1]) for k in range(ns)]
    return list(res[2 * ns:2 * ns + n]), sems, (res[2 * ns + n] if ns else None)


def _cast_place(items, name, deps=()):
    n = len(items)
    mats = [a.shape[-2:] for a, _, _ in items]

    def body(*refs):
        ins, outs, scr, sem = refs[:n], refs[n:2 * n], refs[2 * n:3 * n], refs[3 * n]
        x, y, _ = _mesh_pos()
        cps = []
        for t in range(n):
            scr[t][...] = ins[t][...].astype(scr[t].dtype)
            cp = pltpu.make_async_copy(scr[t], outs[t].at[2 * x + y], sem.at[t])
            cp.start()
            cps.append(cp)
        for cp in cps:
            cp.wait()

    def spec(idx, shape):
        return pl.BlockSpec((None,) * len(idx) + tuple(shape), lambda i: tuple(idx) + (0, 0))

    body, in_specs, args = _add_deps(body, [spec(idx, mat) for (_, idx, _), mat in zip(items, mats)],
                                     [a for a, _, _ in items], deps)
    return pl.pallas_call(
        body, name=name, grid=(1,), in_specs=in_specs, out_specs=[ANY] * n,
        out_shape=[jax.ShapeDtypeStruct((N_SHARD,) + tuple(mat), dt) for (_, _, dt), mat in zip(items, mats)],
        scratch_shapes=[pltpu.VMEM(tuple(mat), dt) for (_, _, dt), mat in zip(items, mats)]
        + [pltpu.SemaphoreType.DMA((n,))],
        compiler_params=_params(),
    )(*args)


def _gather_ici(idx):
    def fn(bufs, ss, rs):
        x, y, c = _mesh_pos()
        pairs = []
        for k, t in enumerate(idx):
            half = bufs[t].shape[1] // 2
            mine = bufs[t].at[2 * x + y, pl.ds(c * half, half), :]
            for j, (cx, cy) in enumerate(_other_chips(x, y)):
                theirs = bufs[t].at[2 * cx + cy, pl.ds(c * half, half), :]
                sem = dict(send_sem=ss.at[3 * k + j], recv_sem=rs.at[3 * k + j],
                           device_id=(cx, cy, c), device_id_type=MESH)
                pairs.append((pltpu.make_async_remote_copy(src_ref=mine, dst_ref=mine, **sem),
                              pltpu.make_async_remote_copy(src_ref=mine, dst_ref=theirs, **sem)))
        return pairs
    return fn


def _gather_d2d(idx):
    def fn(bufs, ss, rs):
        x, y, c = _mesh_pos()
        pairs = []
        for k, t in enumerate(idx):
            half = bufs[t].shape[1] // 2
            for j, (cx, cy) in enumerate(_other_chips(x, y)):
                got = bufs[t].at[2 * cx + cy, pl.ds(c * half, half), :]
                theirs = bufs[t].at[2 * cx + cy, pl.ds((1 - c) * half, half), :]
                sem = dict(send_sem=ss.at[3 * k + j], recv_sem=rs.at[3 * k + j],
                           device_id=(x, y, 1 - c), device_id_type=MESH)
                pairs.append((pltpu.make_async_remote_copy(src_ref=got, dst_ref=got, **sem),
                              pltpu.make_async_remote_copy(src_ref=got, dst_ref=theirs, **sem)))
        return pairs
    return fn


def _alloc(shapes, name):
    def body(*refs):
        pass

    return pl.pallas_call(body, name=name, out_specs=[ANY] * len(shapes),
                          out_shape=[jax.ShapeDtypeStruct(s, d) for s, d in shapes])()


def _send_to_sibling(n):
    def fn(bufs, ss, rs):
        x, y, c = _mesh_pos()
        pairs = []
        for t in range(n):
            src = bufs[t]
            if len(src.shape) == 3:
                half = src.shape[1] // 2
                src = src.at[:, pl.ds((1 - c) * half, half), :]
            cp = pltpu.make_async_remote_copy(src_ref=src, dst_ref=bufs[n + t], send_sem=ss.at[t],
                                              recv_sem=rs.at[t], device_id=(x, y, 1 - c), device_id_type=MESH)
            pairs.append((cp, cp))
        return pairs
    return fn


def _send_to_chips(n):
    def fn(bufs, ss, rs):
        x, y, c = _mesh_pos()
        pairs = []
        for j, (cx, cy) in enumerate(_other_chips(x, y)):
            for t in range(n):
                src = bufs[t].at[j] if len(bufs[t].shape) == 3 else bufs[t]
                cp = pltpu.make_async_remote_copy(src_ref=src, dst_ref=bufs[n + t].at[j], send_sem=ss.at[3 * t + j],
                                                  recv_sem=rs.at[3 * t + j], device_id=(cx, cy, c),
                                                  device_id_type=MESH)
                pairs.append((cp, cp))
        return pairs
    return fn


class _Exchange:
    def __init__(self, name, srcs, land_shapes, fn, n_sems):
        self.name, self.fn = name, fn
        lands = _alloc(land_shapes, name + "_alloc")
        self.n = len(srcs)
        self.bufs, sems, self.token = _split_call(name + "_start", list(srcs) + list(lands),
                                                  starts=[(n_sems, fn)])
        self.sems = sems[0]

    def finish(self, after=()):
        bufs, _, _ = _split_call(self.name + "_wait", self.bufs, waits=[(*self.sems, self.fn)], after=after)
        return bufs[:self.n], bufs[self.n:]


def _row_block(rows, cols, mult=8, limit=3 * 512 * 1024, itemsize=4):
    best = None
    for br in range(mult, rows + 1, mult):
        if rows % br == 0 and br * cols * itemsize <= limit:
            best = br
    assert best is not None, (rows, cols)
    return best


_GROUP_BLOCK_BYTES = 1024 * 1024


def _group_plan(ss):
    plan = []
    for s in ss:
        half, cols = s.shape[-2:]
        br = _row_block(half, cols, mult=16, limit=_GROUP_BLOCK_BYTES)
        plan.append((br, half // br))
    return plan, max(nr for _, nr in plan)


def _chip_partial(gs, ss, ids, name):
    n = len(gs)
    plan, steps = _group_plan(ss)

    def body(ids_ref, *refs):
        for t in range(n):
            refs[2 * n + t][...] = (refs[t][...] + refs[n + t][...]).astype(BF16)

    g_specs, s_specs, o_specs = [], [], []
    for (br, nr), s in zip(plan, ss):
        blk = (None, br, s.shape[2])
        g_specs.append(pl.BlockSpec(
            blk, lambda j, r, ids_ref, nr=nr: (ids_ref[2 + j], ids_ref[0] * nr + jnp.minimum(r, nr - 1), 0)))
        s_specs.append(pl.BlockSpec(blk, lambda j, r, ids_ref, nr=nr: (ids_ref[2 + j], jnp.minimum(r, nr - 1), 0)))
        o_specs.append(pl.BlockSpec(blk, lambda j, r, ids_ref, nr=nr: (j, jnp.minimum(r, nr - 1), 0)))
    return pl.pallas_call(
        body, name=name,
        grid_spec=pltpu.PrefetchScalarGridSpec(num_scalar_prefetch=1, grid=(3, steps),
                                               in_specs=g_specs + s_specs, out_specs=o_specs),
        out_shape=[jax.ShapeDtypeStruct((3,) + s.shape[1:], BF16) for s in ss],
        compiler_params=pltpu.CompilerParams(dimension_semantics=("arbitrary", "arbitrary"),
                                             vmem_limit_bytes=VMEM_LIMIT),
    )(ids, *gs, *ss)


def _chip_sum(gs, ss, qs, ids, name):
    n = len(gs)
    plan, steps = _group_plan(ss)

    def body(ids_ref, *refs):
        for t in range(n):
            q_ref = refs[2 * n + t]
            own = refs[t][...] + refs[n + t][...]
            refs[3 * n + t][...] = (own + q_ref[2].astype(F32)) + (q_ref[0].astype(F32) + q_ref[1].astype(F32))

    g_specs, s_specs, q_specs, o_specs = [], [], [], []
    for (br, nr), s in zip(plan, ss):
        cols = s.shape[2]
        g_specs.append(pl.BlockSpec(
            (None, br, cols), lambda r, ids_ref, nr=nr: (ids_ref[1], ids_ref[0] * nr + jnp.minimum(r, nr - 1), 0)))
        s_specs.append(pl.BlockSpec((None, br, cols), lambda r, ids_ref, nr=nr: (ids_ref[1], jnp.minimum(r, nr - 1), 0)))
        q_specs.append(pl.BlockSpec((3, br, cols), lambda r, ids_ref, nr=nr: (0, jnp.minimum(r, nr - 1), 0)))
        o_specs.append(pl.BlockSpec((br, cols), lambda r, ids_ref, nr=nr: (jnp.minimum(r, nr - 1), 0)))
    return pl.pallas_call(
        body, name=name,
        grid_spec=pltpu.PrefetchScalarGridSpec(num_scalar_prefetch=1, grid=(steps,),
                                               in_specs=g_specs + s_specs + q_specs, out_specs=o_specs),
        out_shape=[jax.ShapeDtypeStruct(s.shape[1:], F32) for s in ss],
        compiler_params=pltpu.CompilerParams(dimension_semantics=("arbitrary",), vmem_limit_bytes=VMEM_LIMIT),
    )(ids, *gs, *ss, *qs)


def _adamw_math(w, g, m, v):
    mn = ADAM_B1 * m + (1.0 - ADAM_B1) * g
    vn = ADAM_B2 * v + (1.0 - ADAM_B2) * (g * g)
    m_hat = mn / (1.0 - ADAM_B1 ** ADAM_STEP)
    v_hat = vn / (1.0 - ADAM_B2 ** ADAM_STEP)
    return -ADAM_LR * (m_hat / (jnp.sqrt(v_hat) + ADAM_EPS) + ADAM_WD * w), mn, vn


def _adamw_halves(w, own, sib, m, v, ids, name, layer=0, n_layers=1, stacked=None):
    C = w.shape[1]
    R = w.shape[0] // n_layers
    half = R // 2
    br = _row_block(half, C)
    nh = half // br
    base = layer * 2 * nh

    def body(ids_ref, w_ref, own_ref, sib_ref, m_ref, v_ref, *rest):
        g_ref, d_ref, mo_ref, vo_ref = rest[-4:]
        is_own = (pl.program_id(0) // nh) == ids_ref[0]
        g = jnp.where(is_own, own_ref[...], sib_ref[...])
        g_ref[...] = g
        d_ref[...], mo_ref[...], vo_ref[...] = _adamw_math(w_ref[...], g, m_ref[...], v_ref[...])

    full = pl.BlockSpec((br, C), lambda r, ids_ref: (base + r, 0))
    own_spec = pl.BlockSpec((br, C), lambda r, ids_ref: (jnp.clip(r - ids_ref[0] * nh, 0, nh - 1), 0))
    sib_spec = pl.BlockSpec((br, C), lambda r, ids_ref: (jnp.clip(r - (1 - ids_ref[0]) * nh, 0, nh - 1), 0))
    in_specs = [full, own_spec, sib_spec, full, full]
    args = [ids, w, own, sib, m, v]
    aliases = {}
    if stacked is not None:
        in_specs += [ANY] * 4
        args += list(stacked)
        aliases = {6 + k: k for k in range(4)}
    return pl.pallas_call(
        body, name=name,
        grid_spec=pltpu.PrefetchScalarGridSpec(
            num_scalar_prefetch=1, grid=(2 * nh,), in_specs=in_specs, out_specs=[full] * 4),
        out_shape=[jax.ShapeDtypeStruct(w.shape, F32)] * 4, input_output_aliases=aliases,
        compiler_params=_params(),
    )(*args)


_PACK_UNIT = 1024


def _pack(arrs):
    flat = []
    for a in arrs:
        f = a.reshape(-1).astype(F32)
        pad = (-f.shape[0]) % _PACK_UNIT
        if pad:
            f = jnp.concatenate([f, jnp.zeros((pad,), F32)])
        flat.append(f)
    return jnp.concatenate(flat).reshape(-1, 128)


def kernel(x, p, norm_mix, norm_ffn, norm_ple, norm_kv, norm_final, a_w_in, a_norm_v, a_w_s, a_b_s, a_w_out, w_kv, b_w_q, b_sinks, b_w_o, f_w_up, f_conv_w, f_conv_b, f_w_down, ple_w_in, ple_w_gate, ple_b_gate, loss_target, m_norm_mix, m_norm_ffn, m_norm_ple, m_norm_kv, m_norm_final, m_a_w_in, m_a_norm_v, m_a_w_s, m_a_b_s, m_a_w_out, m_w_kv, m_b_w_q, m_b_sinks, m_b_w_o, m_f_w_up, m_f_conv_w, m_f_conv_b, m_f_w_down, m_ple_w_in, m_ple_w_gate, m_ple_b_gate, v_norm_mix, v_norm_ffn, v_norm_ple, v_norm_kv, v_norm_final, v_a_w_in, v_a_norm_v, v_a_w_s, v_a_b_s, v_a_w_out, v_w_kv, v_b_w_q, v_b_sinks, v_b_w_o, v_f_w_up, v_f_conv_w, v_f_conv_b, v_f_w_down, v_ple_w_in, v_ple_w_gate, v_ple_b_gate):
    given = dict(locals())

    small_shard = _pack([a_norm_v, f_conv_w])
    pad_rows = (-small_shard.shape[0]) % 16
    if pad_rows:
        small_shard = jnp.concatenate([small_shard, jnp.zeros((pad_rows, 128), F32)])
    groups = [
        [(a_w_in, (0,), BF16), (a_w_out, (0,), BF16), (small_shard, (), F32)],
        [(f_w_up, (0,), BF16), (f_w_down, (0,), BF16)],
        [(ple_w_in, (0,), BF16), (ple_w_gate, (0,), BF16), (w_kv, (), BF16), (b_w_q, (0,), BF16),
         (b_w_o, (0,), BF16), (f_w_up, (1,), BF16), (f_w_down, (1,), BF16), (ple_w_in, (1,), BF16),
         (ple_w_gate, (1,), BF16)],
    ]
    first = list(range(len(groups[0])))
    lands0, sems0, token0 = _split_call("gather_start_g0", _cast_place(groups[0], "cast_place_g0"),
                                        starts=[(3 * len(first), _gather_ici(first))])
    rest, spans, start = [], [], 0
    for gi, items in enumerate(groups[1:], 1):
        rest += _cast_place(items, f"cast_place_g{gi}", deps=(token0,))
        spans.append(list(range(start, start + len(items))))
        start += len(items)
    rest, rest_sems, _ = _split_call("gather_start", rest, starts=[(3 * len(sp), _gather_ici(sp)) for sp in spans])
    group_bufs = [lands0] + [[rest[t] for t in sp] for sp in spans]
    ici_sems = sems0 + rest_sems

    def finish_group(gi, after):
        bufs = group_bufs[gi]
        local = list(range(len(bufs)))
        bufs, d2d_sems, _ = _split_call(f"gather_pass_g{gi}", bufs, waits=[(*ici_sems[gi], _gather_ici(local))],
                                        starts=[(3 * len(local), _gather_d2d(local))], after=after)
        bufs, _, _ = _split_call(f"gather_done_g{gi}", bufs, waits=[(*d2d_sems[0], _gather_d2d(local))])
        return bufs

    def stage0():
        b_in, b_out, b_small = finish_group(0, ())
        small_full = b_small.reshape(N_SHARD, -1)
        gv_full = small_full[:, :256].reshape(1, D_MODEL)
        cw_full = small_full[:, _PACK_UNIT:_PACK_UNIT + 2 * 3 * FF_BLK].reshape(N_SHARD, 2, 3, FF_BLK)
        cw_full = jnp.transpose(cw_full, (1, 2, 0, 3)).reshape(2, 3, N_FF)
        return gv_full, cw_full, b_in, b_out.reshape(D_MODEL, D_MODEL)

    def stage1(after):
        b_up, b_dn = finish_group(1, after)
        return b_up, b_dn.reshape(D_FF, D_MODEL)

    def stage2(after):
        pin0, gate0, kv_w, wq, wo, up1, dn1, pin1, gate1 = finish_group(2, after)
        sq = lambda a: a.reshape(D_MODEL, -1)
        return dict(w_pin=[pin0, pin1], w_gate=[sq(gate0), sq(gate1)], w_kv=sq(kv_w), w_q=sq(wq), w_o=sq(wo),
                    w_up1=up1, w_dn1=dn1.reshape(D_FF, D_MODEL))

    dx, (loss, (out_g, out_d, out_m, out_v)) = _local_step(
        x[0], p[0, 0], p[1, 0], loss_target[0], norm_mix, norm_ffn, norm_ple, norm_kv, norm_final, a_w_s, a_b_s,
        b_sinks, f_conv_b, ple_b_gate, stage0, stage1, stage2, _Reducer(given))
    weight_names = ['norm_mix', 'norm_ffn', 'norm_ple', 'norm_kv', 'norm_final', 'a_w_in', 'a_norm_v', 'a_w_s',
                    'a_b_s', 'a_w_out', 'w_kv', 'b_w_q', 'b_sinks', 'b_w_o', 'f_w_up', 'f_conv_w', 'f_conv_b',
                    'f_w_down', 'ple_w_in', 'ple_w_gate', 'ple_b_gate']
    return (loss, dx.reshape(x.shape), *[out_g[k] for k in weight_names], *[out_d[k] for k in weight_names],
            *[out_m[k] for k in weight_names], *[out_v[k] for k in weight_names])


def _local_step(xs, p0, p1, tgt, norm_mix, norm_ffn, norm_ple, norm_kv, norm_final, a_w_s, a_b_s, b_sinks,
                f_conv_b, ple_b_gate, stage0, stage1, stage2, sched):
    tril = jnp.tril(jnp.ones((CHUNK, CHUNK), F32))
    wsm = (a_w_s[0] * tril[None]).astype(BF16)
    bsb = jnp.broadcast_to(a_b_s[0][:, :, None], (A_GROUPS, CHUNK, CHUNK))
    sinks = b_sinks[0]
    row = lambda a: a.reshape(1, -1)

    gv_full, cw_full, w_in, w_out = stage0()
    h1, zp = _mixer_a_fwd(xs, row(norm_mix[0]), gv_full, wsm, bsb, w_in, w_out)
    w_up0, w_dn0 = stage1((h1,))
    h2, hh0, c0 = _ffn_fwd(h1, row(norm_ffn[0]), cw_full[0], row(f_conv_b[0]), w_up0, w_dn0, 0)
    rest = stage2((h2,))
    w_pin, w_gate, w_kv_f, w_q, w_o = rest['w_pin'], rest['w_gate'], rest['w_kv'], rest['w_q'], rest['w_o']
    w_up = [w_up0, rest['w_up1']]
    w_dn = [w_dn0, rest['w_dn1']]
    h3, kv = _ple_fwd_kv(h2, p0, row(norm_ple[0]), row(ple_b_gate[0]), row(norm_kv), w_pin[0], w_gate[0], w_kv_f)
    h4, q, ao, probs, psink = _attn_fwd(h3, row(norm_mix[1]), kv, sinks, w_q, w_o)
    h5, hh1, c1 = _ffn_fwd(h4, row(norm_ffn[1]), cw_full[1], row(f_conv_b[1]), w_up[1], w_dn[1], 1)
    dh6, loss_acc, dn_final = _ple_fwd_final(
        h5, p1, tgt, row(norm_ple[1]), row(ple_b_gate[1]), row(norm_final), w_pin[1], w_gate[1])

    def pieces(g):
        return g.reshape(N_SHARD, -1, g.shape[-1])

    dh5, g_pin1, g_gate1, dbg1, dnple1 = _ple_bwd(dh6, h5, p1, row(norm_ple[1]), row(ple_b_gate[1]), w_pin[1], w_gate[1], 1)
    early = {('ple_w_in', 1): g_pin1, ('ple_w_gate', 1): pieces(g_gate1)}
    dh4, g_up1, g_dn1, dcw1, dcb1, dnffn1 = _ffn_bwd(
        dh5, h4, hh1, c1, row(norm_ffn[1]), cw_full[1], w_up[1], w_dn[1], 1)
    early['f_w_down', 1] = pieces(g_dn1)
    early['f_w_up', 1] = g_up1
    dh3a, g_wq, g_wo, dkv, dsink, dnmix1 = _attn_bwd(dh4, h3, q, kv, ao, probs, psink, row(norm_mix[1]), w_q, w_o)
    early['b_w_o', 0] = pieces(g_wo)
    early['b_w_q', 0] = pieces(g_wq)
    dh2, g_pin0, g_gate0, dbg0, dnple0, g_wkv, dnkv = _ple_bwd(
        dh3a, h2, p0, row(norm_ple[0]), row(ple_b_gate[0]), w_pin[0], w_gate[0], 0,
        kv_args=(h3, dkv, row(norm_kv), w_kv_f))
    early['w_kv', 0] = pieces(g_wkv)
    early['ple_w_in', 0] = g_pin0
    early['ple_w_gate', 0] = pieces(g_gate0)
    deps = sched.early_ready(early)
    dh1, g_up0, g_dn0, dcw0, dcb0, dnffn0 = _ffn_bwd(
        dh2, h1, hh0, c0, row(norm_ffn[0]), cw_full[0], w_up[0], w_dn[0], 0, deps=deps,
        between=lambda part: sched.after_ffn_half((part,)))
    deps = sched.ffn0_ready({('f_w_down', 0): pieces(g_dn0), ('f_w_up', 0): g_up0})
    dx, g_win, g_wout, dws, dbs, dgv, dnmix0 = _mixer_a_bwd(
        dh1, xs, zp, row(norm_mix[0]), gv_full, wsm, bsb, tril, w_in, w_out, deps=deps)
    g_wout = pieces(g_wout)

    small_grads = {
        'norm_mix': jnp.concatenate([dnmix0, dnmix1]), 'norm_ffn': jnp.concatenate([dnffn0, dnffn1]),
        'norm_ple': jnp.concatenate([dnple0, dnple1]), 'norm_kv': dnkv, 'norm_final': dn_final,
        'a_norm_v': dgv, 'a_w_s': dws.reshape(A_GROUPS * CHUNK, CHUNK), 'a_b_s': dbs[:, :, 0],
        'b_sinks': dsink[0:1, :], 'f_conv_w': jnp.concatenate([dcw0, dcw1]),
        'f_conv_b': jnp.concatenate([dcb0, dcb1]), 'ple_b_gate': jnp.concatenate([dbg0, dbg1]),
        'loss': loss_acc,
    }
    outs = sched.finish({('a_w_in', 0): g_win, ('a_w_out', 0): g_wout}, small_grads, (dx,))
    return dx, outs


class _Reducer:
    def __init__(self, given):
        self.given = given
        cx, cy, cc = _mesh_pos()
        self.shard = 2 * cx + cy
        s = self.shard
        self.ids = jnp.stack([cc, s, s ^ 2, s ^ 1, s ^ 3]).astype(jnp.int32)
        self.out = [{}, {}, {}, {}]
        self.stacked = {}

    def _send(self, tag, grads, small=()):
        keys = list(grads)
        srcs = [grads[k] for k in keys] + list(small)
        shapes = [((N_SHARD, g.shape[1] // 2, g.shape[2]), F32) for g in srcs[:len(keys)]]
        shapes += [(s.shape, F32) for s in small]
        return keys, _Exchange(f"send_{tag}", srcs, shapes, _send_to_sibling(len(srcs)), len(srcs))

    def _exchange(self, tag, keys, send, after):
        srcs, lands = send.finish(after)
        n = len(keys)
        parts = _chip_partial(srcs[:n], lands[:n], self.ids, f"chip_partial_{tag}")
        shapes = [(p.shape, BF16) for p in parts]
        if len(srcs) > n:
            small = _small_add(srcs[n:], lands[n:])
            parts += small
            shapes += [((3,) + s.shape, F32) for s in small]
        exch = _Exchange(f"exch_{tag}", parts, shapes, _send_to_chips(len(parts)), 3 * len(parts))
        return (keys, srcs[:n], lands[:n], exch)

    def _swap(self, tag, state, after):
        keys, grads, sib, exch = state
        parts, recv = exch.finish(after)
        n = len(keys)
        own = _chip_sum(grads, sib, recv[:n], self.ids, f"chip_sum_{tag}")
        small_red = _small_sum(parts[n:], recv[n:]) if len(parts) > n else None
        return keys, _Exchange(f"swap_{tag}", own, [(o.shape, F32) for o in own], _send_to_sibling(n), n), small_red

    def _adamw(self, keys, swap, after):
        own, sib = swap.finish(after)
        last = None
        for (name, layer), o, s in zip(keys, own, sib):
            w = self.given[name]
            n_layers = w.shape[0] if w.ndim == 3 else 1
            c2 = w.shape[-1]
            res = _adamw_halves(w.reshape(-1, c2), o, s, self.given['m_' + name].reshape(-1, c2),
                                self.given['v_' + name].reshape(-1, c2), self.ids, f"adamw_{name}{layer}",
                                layer, n_layers, self.stacked.get(name))
            self.stacked[name] = res
            if layer == 0:
                for dst, r in zip(self.out, res):
                    dst[name] = r.reshape(w.shape)
            last = res[0]
        return last

    def early_ready(self, grads):
        self.e_keys, self.e_send = self._send("e", grads)
        return (self.e_send.token,)

    def after_ffn_half(self, after):
        self.e_state = self._exchange("e", self.e_keys, self.e_send, after)
        return (self.e_state[3].token,)

    def ffn0_ready(self, grads):
        _, self.e_swap, _ = self._swap("e", self.e_state, tuple(grads.values()))
        f_keys, f_send = self._send("f", grads)
        self.f_state = self._exchange("f", f_keys, f_send, ())
        return (self.f_state[3].token, self.e_swap.token)

    def finish(self, grads, small_grads, after):
        small_names = list(small_grads)
        a_keys, a_send = self._send("a", grads, [small_grads[k] for k in small_names])
        a_state = self._exchange("a", a_keys, a_send, after)
        e_done = self._adamw(self.e_keys, self.e_swap, (a_state[3].token,))
        f_keys, f_swap, _ = self._swap("f", self.f_state, (e_done,))
        f_done = self._adamw(f_keys, f_swap, ())
        _, a_swap, small_red = self._swap("a", a_state, (f_done,))
        self._adamw(a_keys, a_swap, ())

        given = self.given
        reduced = dict(zip(small_names, small_red))
        loss = reduced.pop('loss')[0, 0]
        names = list(reduced)
        items = []
        for k in names:
            g = reduced[k]
            cols = g.shape[1] // N_SHARD if k in ('a_norm_v', 'f_conv_w') else g.shape[1]
            view = lambda a: _lane_pad(a.reshape(g.shape[0], -1), cols)
            items.append((view(given[k]), g, view(given['m_' + k]), view(given['v_' + k])))
        res = _adamw_small(items, self.ids)
        for k, four in zip(names, res):
            width = given[k].size // four[0].shape[0]
            for dst, r in zip(self.out, four):
                dst[k] = r[:, :width].reshape(given[k].shape)
        return loss, self.out


def _lane_pad(a, cols):
    return a if a.shape[1] == cols else jnp.pad(a, ((0, 0), (0, cols - a.shape[1])))


def _small_add(a_list, b_list):
    n = len(a_list)

    def body(*refs):
        for t in range(n):
            refs[2 * n + t][...] = refs[t][...] + refs[n + t][...]

    return pl.pallas_call(body, name="chip_partial_small",
                          out_shape=[jax.ShapeDtypeStruct(a.shape, F32) for a in a_list])(*a_list, *b_list)


def _small_sum(parts, recvs):
    n = len(parts)

    def body(*refs):
        for t in range(n):
            q = refs[n + t]
            refs[2 * n + t][...] = (refs[t][...] + q[2]) + (q[0] + q[1])

    return pl.pallas_call(body, name="chip_sum_small",
                          out_shape=[jax.ShapeDtypeStruct(p.shape, F32) for p in parts])(*parts, *recvs)


def _adamw_small(items, ids):
    n = len(items)

    def body(ids_ref, *refs):
        for t in range(n):
            w_ref, g_ref, m_ref, v_ref = refs[4 * t:4 * t + 4]
            g_out, d_ref, mo_ref, vo_ref = refs[4 * n + 4 * t:4 * n + 4 * t + 4]
            g = g_ref[...]
            g_out[...] = g
            d_ref[...], mo_ref[...], vo_ref[...] = _adamw_math(w_ref[...], g, m_ref[...], v_ref[...])

    in_specs, out_specs, out_shape, args = [], [], [], []
    for w, g, m, v in items:
        full = pl.BlockSpec(w.shape, lambda i, ids_ref: (0, 0))
        g_spec = full if g.shape == w.shape else pl.BlockSpec(w.shape, lambda i, ids_ref: (0, ids_ref[1]))
        in_specs += [full, g_spec, full, full]
        out_specs += [full] * 4
        out_shape += [jax.ShapeDtypeStruct(w.shape, F32)] * 4
        args += [w, g, m, v]
    res = pl.pallas_call(
        body, name="adamw_small",
        grid_spec=pltpu.PrefetchScalarGridSpec(num_scalar_prefetch=1, grid=(1,), in_specs=in_specs,
                                               out_specs=out_specs),
        out_shape=out_shape, compiler_params=_params(),
    )(ids, *args)
    return [res[4 * t:4 * t + 4] for t in range(n)]
```

```python
import functools
import math

import numpy as np
import jax
import jax.numpy as jnp
from jax import lax
from jax.experimental import pallas as pl
from jax.experimental.pallas import tpu as pltpu

F32 = jnp.float32
BF16 = jnp.bfloat16

D_MODEL = 1024
CHUNK = 128
A_GROUPS = 8
HEAD_DIM = 64
N_Q_HEADS = 16
N_KV_HEADS = 4
GQA_GROUP = N_Q_HEADS // N_KV_HEADS
KV_DIM = N_KV_HEADS * HEAD_DIM
BLOCK = 128
D_FF = 2816
N_FF = 2 * D_FF
FF_BLK = N_FF // 4
PLE_DIM = 256
EPS = 1e-6
NEG = -1e30
N_SHARD = 4

ADAM_LR = 0.001
ADAM_B1 = 0.9
ADAM_B2 = 0.999
ADAM_EPS = 1e-08
ADAM_WD = 0.01
ADAM_STEP = 10

VMEM_LIMIT = 60 * 1024 * 1024
MESH = pl.DeviceIdType.MESH
ANY = pl.BlockSpec(memory_space=pl.ANY)
SMEM = pl.BlockSpec(memory_space=pltpu.SMEM)

_SLOPES = [float(np.float32(2.0 ** (-8.0 * (h + 1) / N_Q_HEADS))) for h in range(N_Q_HEADS)]


def _dot(a, b):
    return jnp.dot(a, b, preferred_element_type=F32)


def _dot_nt(a, b):
    return lax.dot_general(a, b, (((1,), (1,)), ((), ())), preferred_element_type=F32)


def _dot_tn(a, b):
    return lax.dot_general(a, b, (((0,), (0,)), ((), ())), preferred_element_type=F32)


def _rms(x, g):
    r = lax.rsqrt(jnp.mean(x * x, axis=-1, keepdims=True) + EPS)
    xh = x * r
    return xh * g, xh, r


def _rms_bwd(dy, xh, r, g):
    dxh = dy * g
    dg = jnp.sum(dy * xh, axis=0, keepdims=True)
    dx = r * (dxh - xh * jnp.mean(dxh * xh, axis=-1, keepdims=True))
    return dx, dg


_GELU_C = math.sqrt(2.0 / math.pi)


def _gelu(x):
    t = jnp.tanh(_GELU_C * (x + 0.044715 * (x * x * x)))
    return 0.5 * x * (1.0 + t)


def _gelu_grad(x):
    x2 = x * x
    t = jnp.tanh(_GELU_C * (x + 0.044715 * (x2 * x)))
    return 0.5 * (1.0 + t) + 0.5 * x * (1.0 - t * t) * (_GELU_C * (1.0 + 3.0 * 0.044715 * x2))


def _sigmoid(x):
    return 0.5 * jnp.tanh(0.5 * x) + 0.5


def _load_once(pairs, sem):
    @pl.when(pl.program_id(0) == 0)
    def _():
        cps = [pltpu.make_async_copy(s, d, sem.at[i]) for i, (s, d) in enumerate(pairs)]
        for cp in cps:
            cp.start()
        for cp in cps:
            cp.wait()


def _params(n_axes=1, vmem=VMEM_LIMIT):
    return pltpu.CompilerParams(dimension_semantics=("arbitrary",) * n_axes, vmem_limit_bytes=vmem)


def _row_spec(tm, n, rev_nt=None):
    if rev_nt is None:
        return pl.BlockSpec((tm, n), lambda i: (i, 0))
    return pl.BlockSpec((tm, n), lambda i: (rev_nt - 1 - i, 0))


def _const_spec(shape):
    nd = len(shape)
    return pl.BlockSpec(shape, lambda i: (0,) * nd)


def _add_deps(body, in_specs, args, deps):
    nd = len(deps)
    if nd == 0:
        return body, list(in_specs), list(args)

    def wrapped(*refs):
        return body(*refs[nd:])

    return wrapped, [ANY] * nd + list(in_specs), list(deps) + list(args)


def _zero_first(refs):
    @pl.when(pl.program_id(0) == 0)
    def _():
        for r in refs:
            r[...] = jnp.zeros(r.shape, r.dtype)


def _mixer_a_fwd(x, nmix, gv, wsm, bsb, w_in, w_out):
    T = x.shape[0]
    tm = min(512, T)
    nt = T // tm
    nw = 2 * D_MODEL // N_SHARD

    def body(x_ref, nmix_ref, gv_ref, ws_ref, bsb_ref, w_in_hbm, w_out_hbm,
             h1_ref, zp_ref, w_in_v, w_out_v, gated_v, sem):
        _load_once([(w_in_hbm, w_in_v), (w_out_hbm, w_out_v)], sem)
        xv = x_ref[...]
        xn = _rms(xv, nmix_ref[...])[0].astype(BF16)
        for j in range(N_SHARD):
            zp_ref[:, j * nw:(j + 1) * nw] = _dot(xn, w_in_v[j])
        z = _gelu(zp_ref[...])
        u = z[:, :D_MODEL]
        vn = _rms(z[:, D_MODEL:], gv_ref[...])[0].astype(BF16)
        for c in range(tm // CHUNK):
            rows = slice(c * CHUNK, (c + 1) * CHUNK)
            for h in range(A_GROUPS):
                cols = slice(h * CHUNK, (h + 1) * CHUNK)
                s = _dot(ws_ref[h], vn[rows, cols]) + bsb_ref[h]
                gated_v[rows, cols] = (u[rows, cols] * s).astype(BF16)
        h1_ref[...] = xv + _dot(gated_v[...], w_out_v[...])

    return pl.pallas_call(
        body, name="mixer_a_fwd", grid=(nt,),
        in_specs=[_row_spec(tm, D_MODEL), _const_spec((1, D_MODEL)), _const_spec((1, D_MODEL)),
                  _const_spec((A_GROUPS, CHUNK, CHUNK)), _const_spec((A_GROUPS, CHUNK, CHUNK)), ANY, ANY],
        out_specs=[_row_spec(tm, D_MODEL), _row_spec(tm, 2 * D_MODEL)],
        out_shape=[jax.ShapeDtypeStruct((T, D_MODEL), F32), jax.ShapeDtypeStruct((T, 2 * D_MODEL), F32)],
        scratch_shapes=[pltpu.VMEM((N_SHARD, D_MODEL, nw), BF16), pltpu.VMEM((D_MODEL, D_MODEL), BF16),
                        pltpu.VMEM((tm, D_MODEL), BF16), pltpu.SemaphoreType.DMA((2,))],
        compiler_params=_params(),
    )(x, nmix, gv, wsm, bsb, w_in, w_out)


def _mixer_a_bwd(dh, x, zp, nmix, gv, wsm, bsb, tril, w_in, w_out, deps=()):
    T = x.shape[0]
    tm = min(256, T)
    nt = T // tm
    nw = 2 * D_MODEL // N_SHARD

    def body(dh_ref, x_ref, zp_ref, nmix_ref, gv_ref, ws_ref, bsb_ref, tril_ref, w_in_hbm, w_out_hbm,
             dx_ref, dwin_ref, dwout_ref, dws_ref, dbs_ref, dgv_ref, dnmix_ref,
             w_in_v, w_out_v, du_v, dvn_v, dbs_v, gated_ref, sem):
        _load_once([(w_in_hbm, w_in_v), (w_out_hbm, w_out_v)], sem)
        _zero_first([dws_ref, dbs_v, dgv_ref, dnmix_ref, dwin_ref, dwout_ref])
        i = pl.program_id(0)
        dhv = dh_ref[...]
        dhb = dhv.astype(BF16)
        xv = x_ref[...]
        xn, xh, r = _rms(xv, nmix_ref[...])
        xnb = xn.astype(BF16)
        zpv = zp_ref[...]
        z = _gelu(zpv)
        u = z[:, :D_MODEL]
        vn_f, vh, rv = _rms(z[:, D_MODEL:], gv_ref[...])
        vn = vn_f.astype(BF16)
        dgated = _dot_nt(dhb, w_out_v[...])
        for c in range(tm // CHUNK):
            rows = slice(c * CHUNK, (c + 1) * CHUNK)
            for h in range(A_GROUPS):
                cols = slice(h * CHUNK, (h + 1) * CHUNK)
                vn_h = vn[rows, cols]
                s = _dot(ws_ref[h], vn_h) + bsb_ref[h]
                dgt = dgated[rows, cols]
                u_h = u[rows, cols]
                gated_ref[rows, cols] = (u_h * s).astype(BF16)
                du_v[rows, cols] = dgt * s
                ds = dgt * u_h
                dsb = ds.astype(BF16)
                dws_ref[h] += _dot_nt(dsb, vn_h)
                dbs_v[h] += ds
                dvn_v[rows, cols] = _dot_tn(ws_ref[h], dsb)
        dwout_ref[...] += _dot_tn(gated_ref[...], dhb)
        dv, dgv = _rms_bwd(dvn_v[...], vh, rv, gv_ref[...])
        dgv_ref[...] += dgv
        dzu = (du_v[...] * _gelu_grad(zpv[:, :D_MODEL])).astype(BF16)
        dzv = (dv * _gelu_grad(zpv[:, D_MODEL:])).astype(BF16)
        dzs = (dzu[:, :nw], dzu[:, nw:], dzv[:, :nw], dzv[:, nw:])
        dxn = jnp.zeros((tm, D_MODEL), F32)
        for j in range(N_SHARD):
            dxn = dxn + _dot_nt(dzs[j], w_in_v[j])
            dwin_ref[j] += _dot_tn(xnb, dzs[j])
        dxx, dn = _rms_bwd(dxn, xh, r, nmix_ref[...])
        dnmix_ref[...] += dn
        dx_ref[...] = dhv + dxx

        @pl.when(i == nt - 1)
        def _():
            for h in range(A_GROUPS):
                dws_ref[h] = dws_ref[h] * tril_ref[...]
                dbs_ref[h] = jnp.broadcast_to(jnp.sum(dbs_v[h], axis=1, keepdims=True), (CHUNK, CHUNK))

    grp = (A_GROUPS, CHUNK, CHUNK)
    body, in_specs, args = _add_deps(
        body, [_row_spec(tm, D_MODEL), _row_spec(tm, D_MODEL), _row_spec(tm, 2 * D_MODEL),
               _const_spec((1, D_MODEL)), _const_spec((1, D_MODEL)), _const_spec(grp), _const_spec(grp),
               _const_spec((CHUNK, CHUNK)), ANY, ANY],
        [dh, x, zp, nmix, gv, wsm, bsb, tril, w_in, w_out], deps)
    return pl.pallas_call(
        body, name="mixer_a_bwd", grid=(nt,), in_specs=in_specs,
        out_specs=[_row_spec(tm, D_MODEL), _const_spec((N_SHARD, D_MODEL, nw)), _const_spec((D_MODEL, D_MODEL)),
                   _const_spec(grp), _const_spec(grp), _const_spec((1, D_MODEL)), _const_spec((1, D_MODEL))],
        out_shape=[jax.ShapeDtypeStruct((T, D_MODEL), F32), jax.ShapeDtypeStruct((N_SHARD, D_MODEL, nw), F32),
                   jax.ShapeDtypeStruct((D_MODEL, D_MODEL), F32),
                   jax.ShapeDtypeStruct(grp, F32), jax.ShapeDtypeStruct(grp, F32),
                   jax.ShapeDtypeStruct((1, D_MODEL), F32), jax.ShapeDtypeStruct((1, D_MODEL), F32)],
        scratch_shapes=[pltpu.VMEM((N_SHARD, D_MODEL, nw), BF16), pltpu.VMEM((D_MODEL, D_MODEL), BF16),
                        pltpu.VMEM((tm, D_MODEL), F32), pltpu.VMEM((tm, D_MODEL), F32),
                        pltpu.VMEM(grp, F32), pltpu.VMEM((tm, D_MODEL), BF16), pltpu.SemaphoreType.DMA((2,))],
        compiler_params=_params(),
    )(*args)


def _load_ffn_weights(w_up_hbm, w_dn_hbm, layer, w_up_v, w_dn_v, sem):
    _load_once([(w_up_hbm, w_up_v), (w_dn_hbm, w_dn_v)], sem)


def _ffn_fwd(h, nffn, cw, cb, w_up, w_dn, layer):
    T = h.shape[0]
    tm = min(256, T)
    nt = T // tm

    def body(h_ref, n_ref, cw_ref, cb_ref, w_up_hbm, w_dn_hbm, out_ref, hh_ref, c_ref,
             w_up_v, w_dn_v, carry_v, sem):
        _load_ffn_weights(w_up_hbm, w_dn_hbm, layer, w_up_v, w_dn_v, sem)
        _zero_first([carry_v])
        xv = h_ref[...]
        xf = _rms(xv, n_ref[...])[0].astype(BF16)
        acc = xv
        for j in range(2):
            cs = []
            for blk in (j, j + 2):
                cols = slice(blk * FF_BLK, (blk + 1) * FF_BLK)
                hh = _dot(xf, w_up_v[blk])
                hh_ref[:, cols] = hh.astype(BF16)
                ext = jnp.concatenate([carry_v[blk], hh], axis=0)
                carry_v[blk] = hh[tm - 8:, :]
                s1 = pltpu.roll(ext, 1, 0)[8:]
                s2 = pltpu.roll(ext, 2, 0)[8:]
                cv = (cb_ref[:, cols] + cw_ref[0:1, cols] * s2 + cw_ref[1:2, cols] * s1
                      + cw_ref[2:3, cols] * hh)
                c_ref[:, cols] = cv.astype(BF16)
                cs.append(cv)
            act = (cs[0] * _sigmoid(cs[0]) * cs[1]).astype(BF16)
            acc = acc + _dot(act, w_dn_v[j * FF_BLK:(j + 1) * FF_BLK, :])
        out_ref[...] = acc

    return pl.pallas_call(
        body, name=f"ffn_fwd{layer}", grid=(nt,),
        in_specs=[_row_spec(tm, D_MODEL), _const_spec((1, D_MODEL)), _const_spec((3, N_FF)),
                  _const_spec((1, N_FF)), ANY, ANY],
        out_specs=[_row_spec(tm, D_MODEL), _row_spec(tm, N_FF), _row_spec(tm, N_FF)],
        out_shape=[jax.ShapeDtypeStruct((T, D_MODEL), F32), jax.ShapeDtypeStruct((T, N_FF), BF16),
                   jax.ShapeDtypeStruct((T, N_FF), BF16)],
        scratch_shapes=[pltpu.VMEM((N_SHARD, D_MODEL, FF_BLK), BF16), pltpu.VMEM((D_FF, D_MODEL), BF16),
                        pltpu.VMEM((N_SHARD, 8, FF_BLK), F32), pltpu.SemaphoreType.DMA((2 * N_SHARD,))],
        compiler_params=_params(),
    )(h, nffn, cw, cb, w_up, w_dn)


def _wgrad(a, b, bn, col_sharded, name, deps=()):
    T, K = a.shape
    N = b.shape[1]
    tt = min(2048, T)
    nn, ntt = N // bn, T // tt
    kr = K // N_SHARD

    def body(a_ref, b_ref, o_ref):
        @pl.when(pl.program_id(1) == 0)
        def _():
            o_ref[...] = jnp.zeros(o_ref.shape, F32)
        d = _dot_tn(a_ref[...].astype(BF16), b_ref[...].astype(BF16))
        if col_sharded:
            o_ref[...] += d
        else:
            for j in range(N_SHARD):
                o_ref[j] += d[j * kr:(j + 1) * kr]

    if col_sharded:
        assert nn == N_SHARD
        out_spec = pl.BlockSpec((None, K, bn), lambda n, t: (n, 0, 0))
        out_shape = jax.ShapeDtypeStruct((N_SHARD, K, bn), F32)
    else:
        out_spec = pl.BlockSpec((N_SHARD, kr, bn), lambda n, t: (0, 0, n))
        out_shape = jax.ShapeDtypeStruct((N_SHARD, kr, N), F32)
    body, in_specs, args = _add_deps(
        body, [pl.BlockSpec((tt, K), lambda n, t: (t, 0)), pl.BlockSpec((tt, bn), lambda n, t: (t, n))],
        [a, b], deps)
    return pl.pallas_call(
        body, name=name, grid=(nn, ntt), in_specs=in_specs, out_specs=out_spec, out_shape=out_shape,
        compiler_params=pltpu.CompilerParams(dimension_semantics=("arbitrary",) * 2, vmem_limit_bytes=VMEM_LIMIT),
    )(*args)


def _ffn_bwd(dh, h, hh, c, nffn, cw, w_up, w_dn, layer, deps=(), between=None):
    T = h.shape[0]
    tm = min(256, T)
    nt = T // tm

    def body(dh_ref, h_ref, hh_ref, c_ref, n_ref, cw_ref, w_up_hbm, w_dn_hbm,
             dhin_ref, act_ref, dhh_ref, xf_ref, dcw_ref, dcb_ref, dn_ref,
             w_up_v, w_dn_v, carry_v, sem):
        _load_ffn_weights(w_up_hbm, w_dn_hbm, layer, w_up_v, w_dn_v, sem)
        _zero_first([carry_v, dcw_ref, dcb_ref, dn_ref])
        dout = dh_ref[...]
        doutb = dout.astype(BF16)
        xf_f, xh, r = _rms(h_ref[...], n_ref[...])
        xf_ref[...] = xf_f.astype(BF16)
        dxf = jnp.zeros((tm, D_MODEL), F32)
        for j in range(2):
            blks = (j, j + 2)
            cg = c_ref[:, j * FF_BLK:(j + 1) * FF_BLK].astype(F32)
            cu = c_ref[:, (j + 2) * FF_BLK:(j + 3) * FF_BLK].astype(F32)
            sg = _sigmoid(cg)
            sil = cg * sg
            act_ref[:, j * FF_BLK:(j + 1) * FF_BLK] = (sil * cu).astype(BF16)
            dact = _dot_nt(doutb, w_dn_v[j * FF_BLK:(j + 1) * FF_BLK, :])
            dcs = (dact * cu * (sg * (1.0 + cg * (1.0 - sg))), dact * sil)
            for blk, dc in zip(blks, dcs):
                cols = slice(blk * FF_BLK, (blk + 1) * FF_BLK)
                hhv = hh_ref[:, cols].astype(F32)
                ext = jnp.concatenate([dc, carry_v[blk]], axis=0)
                carry_v[blk] = dc[:8, :]
                n = tm + 8
                a1 = pltpu.roll(ext, n - 1, 0)[:tm]
                a2 = pltpu.roll(ext, n - 2, 0)[:tm]
                dcb_ref[:, cols] += jnp.sum(dc, axis=0, keepdims=True)
                dcw_ref[0:1, cols] += jnp.sum(a2 * hhv, axis=0, keepdims=True)
                dcw_ref[1:2, cols] += jnp.sum(a1 * hhv, axis=0, keepdims=True)
                dcw_ref[2:3, cols] += jnp.sum(dc * hhv, axis=0, keepdims=True)
                dhh = (cw_ref[2:3, cols] * dc + cw_ref[1:2, cols] * a1 + cw_ref[0:1, cols] * a2).astype(BF16)
                dhh_ref[:, cols] = dhh
                dxf = dxf + _dot_nt(dhh, w_up_v[blk])
        dxx, dn = _rms_bwd(dxf, xh, r, n_ref[...])
        dn_ref[...] += dn
        dhin_ref[...] = dout + dxx

    rev = functools.partial(_row_spec, rev_nt=nt)
    body, in_specs, args = _add_deps(
        body, [rev(tm, D_MODEL), rev(tm, D_MODEL), rev(tm, N_FF), rev(tm, N_FF),
               _const_spec((1, D_MODEL)), _const_spec((3, N_FF)), ANY, ANY],
        [dh, h, hh, c, nffn, cw, w_up, w_dn], deps)
    dhin, act, dhh, xf, dcw, dcb, dn = pl.pallas_call(
        body, name=f"ffn_bwd{layer}", grid=(nt,), in_specs=in_specs,
        out_specs=[rev(tm, D_MODEL), rev(tm, D_FF), rev(tm, N_FF), rev(tm, D_MODEL),
                   _const_spec((3, N_FF)), _const_spec((1, N_FF)), _const_spec((1, D_MODEL))],
        out_shape=[jax.ShapeDtypeStruct((T, D_MODEL), F32), jax.ShapeDtypeStruct((T, D_FF), BF16),
                   jax.ShapeDtypeStruct((T, N_FF), BF16), jax.ShapeDtypeStruct((T, D_MODEL), BF16),
                   jax.ShapeDtypeStruct((3, N_FF), F32), jax.ShapeDtypeStruct((1, N_FF), F32),
                   jax.ShapeDtypeStruct((1, D_MODEL), F32)],
        scratch_shapes=[pltpu.VMEM((N_SHARD, D_MODEL, FF_BLK), BF16), pltpu.VMEM((D_FF, D_MODEL), BF16),
                        pltpu.VMEM((N_SHARD, 8, FF_BLK), F32), pltpu.SemaphoreType.DMA((2 * N_SHARD,))],
        compiler_params=_params(),
    )(*args)
    deps2 = between(dhin) if between is not None else ()
    dwdn = _wgrad(act, dh, D_MODEL // 2, False, f"wgrad_ffn_down{layer}", deps=deps2)
    dwup = _wgrad(xf, dhh, FF_BLK, True, f"wgrad_ffn_up{layer}", deps=deps2)
    return dhin, dwup, dwdn, dcw, dcb, dn


def _load_ple_weights(w_pin_hbm, w_gate_hbm, layer, w_pin_v, w_gate_v, sem, extra=()):
    _load_once([(w_pin_hbm, w_pin_v), (w_gate_hbm, w_gate_v)] + list(extra), sem)


def _ple_terms(xv, p_ref, n_ref, bg_ref, w_pin_v, w_gate_v, pe_v):
    pw = D_MODEL // N_SHARD
    xg, xh, r = _rms(xv, n_ref[...])
    xgb = xg.astype(BF16)
    gate = _sigmoid(_dot(xgb, w_gate_v[...]) + bg_ref[...])
    pb = p_ref[...].astype(BF16)
    for j in range(N_SHARD):
        pe_v[:, j * pw:(j + 1) * pw] = _dot(pb, w_pin_v[j])
    pe = pe_v[...]
    return pe * gate, pe, gate, xgb, xh, r


def _ple_fwd_kv(h, p, nple, bg, nkv, w_pin, w_gate, w_kv):
    T = h.shape[0]
    tm = min(512, T)
    nt = T // tm
    pw = D_MODEL // N_SHARD

    def body(h_ref, p_ref, n_ref, bg_ref, nkv_ref, w_pin_hbm, w_gate_hbm, w_kv_hbm,
             out_ref, kv_ref, w_pin_v, w_gate_v, w_kv_v, pe_v, sem):
        _load_ple_weights(w_pin_hbm, w_gate_hbm, 0, w_pin_v, w_gate_v, sem, [(w_kv_hbm, w_kv_v)])
        xv = h_ref[...]
        hn = xv + _ple_terms(xv, p_ref, n_ref, bg_ref, w_pin_v, w_gate_v, pe_v)[0]
        out_ref[...] = hn
        kvn = _rms(hn, nkv_ref[...])[0].astype(BF16)
        kv_ref[...] = _dot(kvn, w_kv_v[...]).astype(BF16)

    vec = _const_spec((1, D_MODEL))
    return pl.pallas_call(
        body, name="ple_fwd0", grid=(nt,),
        in_specs=[_row_spec(tm, D_MODEL), _row_spec(tm, PLE_DIM), vec, vec, vec, ANY, ANY, ANY],
        out_specs=[_row_spec(tm, D_MODEL), _row_spec(tm, 2 * KV_DIM)],
        out_shape=[jax.ShapeDtypeStruct((T, D_MODEL), F32), jax.ShapeDtypeStruct((T, 2 * KV_DIM), BF16)],
        scratch_shapes=[pltpu.VMEM((N_SHARD, PLE_DIM, pw), BF16), pltpu.VMEM((D_MODEL, D_MODEL), BF16),
                        pltpu.VMEM((D_MODEL, 2 * KV_DIM), BF16), pltpu.VMEM((tm, D_MODEL), F32),
                        pltpu.SemaphoreType.DMA((2 * N_SHARD + 1,))],
        compiler_params=_params(),
    )(h, p, nple, bg, nkv, w_pin, w_gate, w_kv)


def _ple_fwd_final(h, p, tgt, nple, bg, nfin, w_pin, w_gate):
    T = h.shape[0]
    tm = min(512, T)
    nt = T // tm
    pw = D_MODEL // N_SHARD

    def body(h_ref, p_ref, t_ref, n_ref, bg_ref, nf_ref, w_pin_hbm, w_gate_hbm,
             dh_ref, loss_ref, dnf_ref, w_pin_v, w_gate_v, pe_v, sem):
        _load_ple_weights(w_pin_hbm, w_gate_hbm, 1, w_pin_v, w_gate_v, sem)
        _zero_first([loss_ref, dnf_ref])
        xv = h_ref[...]
        hn = xv + _ple_terms(xv, p_ref, n_ref, bg_ref, w_pin_v, w_gate_v, pe_v)[0]
        y, yh, r = _rms(hn, nf_ref[...])
        diff = y - t_ref[...]
        loss_ref[...] += 0.5 * jnp.sum(jnp.mean(diff * diff, axis=-1, keepdims=True))
        dy = diff * (1.0 / D_MODEL)
        dhn, dnf = _rms_bwd(dy, yh, r, nf_ref[...])
        dnf_ref[...] += dnf
        dh_ref[...] = dhn

    vec = _const_spec((1, D_MODEL))
    return pl.pallas_call(
        body, name="ple_fwd1", grid=(nt,),
        in_specs=[_row_spec(tm, D_MODEL), _row_spec(tm, PLE_DIM), _row_spec(tm, D_MODEL), vec, vec, vec, ANY, ANY],
        out_specs=[_row_spec(tm, D_MODEL), _const_spec((8, 128)), vec],
        out_shape=[jax.ShapeDtypeStruct((T, D_MODEL), F32), jax.ShapeDtypeStruct((8, 128), F32),
                   jax.ShapeDtypeStruct((1, D_MODEL), F32)],
        scratch_shapes=[pltpu.VMEM((N_SHARD, PLE_DIM, pw), BF16), pltpu.VMEM((D_MODEL, D_MODEL), BF16),
                        pltpu.VMEM((tm, D_MODEL), F32), pltpu.SemaphoreType.DMA((2 * N_SHARD,))],
        compiler_params=_params(),
    )(h, p, tgt, nple, bg, nfin, w_pin, w_gate)


def _ple_bwd(dh, hb, p, nple, bg, w_pin, w_gate, layer, kv_args=None):
    T = hb.shape[0]
    tm = min(512, T)
    nt = T // tm
    with_kv = kv_args is not None
    pw = D_MODEL // N_SHARD

    def body(*refs):
        if with_kv:
            (dh_ref, hb_ref, p_ref, n_ref, bg_ref, w_pin_hbm, w_gate_hbm, hc_ref, dkv_ref, nkv_ref, w_kv_hbm,
             dhb_ref, dwpin_ref, dwgate_ref, dbg_ref, dn_ref, dwkv_ref, dnkv_ref,
             w_pin_v, w_gate_v, pe_v, w_kv_v, sem) = refs
        else:
            (dh_ref, hb_ref, p_ref, n_ref, bg_ref, w_pin_hbm, w_gate_hbm,
             dhb_ref, dwpin_ref, dwgate_ref, dbg_ref, dn_ref, w_pin_v, w_gate_v, pe_v, sem) = refs
        pairs = [(w_pin_hbm, w_pin_v), (w_gate_hbm, w_gate_v)]
        if with_kv:
            pairs.append((w_kv_hbm, w_kv_v))
        _load_once(pairs, sem)
        _zero_first([dwpin_ref, dwgate_ref, dbg_ref, dn_ref] + ([dwkv_ref, dnkv_ref] if with_kv else []))
        do = dh_ref[...]
        if with_kv:
            dkvb = dkv_ref[...].astype(BF16)
            dkvn = _dot_nt(dkvb, w_kv_v[...])
            kvn, kh, kr = _rms(hc_ref[...], nkv_ref[...])
            dwkv_ref[...] += _dot_tn(kvn.astype(BF16), dkvb)
            dk, dnkv = _rms_bwd(dkvn, kh, kr, nkv_ref[...])
            dnkv_ref[...] += dnkv
            do = do + dk
        _, pe, gate, xgb, xh, r = _ple_terms(hb_ref[...], p_ref, n_ref, bg_ref, w_pin_v, w_gate_v, pe_v)
        dpe = (do * gate).astype(BF16)
        pb = p_ref[...].astype(BF16)
        for j in range(N_SHARD):
            dwpin_ref[j] += _dot_tn(pb, dpe[:, j * pw:(j + 1) * pw])
        da = do * pe * (gate * (1.0 - gate))
        dab = da.astype(BF16)
        dbg_ref[...] += jnp.sum(da, axis=0, keepdims=True)
        dxg = _dot_nt(dab, w_gate_v[...])
        dwgate_ref[...] += _dot_tn(xgb, dab)
        dxx, dn = _rms_bwd(dxg, xh, r, n_ref[...])
        dn_ref[...] += dn
        dhb_ref[...] = do + dxx

    vec = _const_spec((1, D_MODEL))
    row = _row_spec(tm, D_MODEL)
    in_specs = [row, row, _row_spec(tm, PLE_DIM), vec, vec, ANY, ANY]
    args = [dh, hb, p, nple, bg, w_pin, w_gate]
    out_specs = [row, _const_spec((N_SHARD, PLE_DIM, pw)), _const_spec((D_MODEL, D_MODEL)), vec, vec]
    out_shape = [jax.ShapeDtypeStruct((T, D_MODEL), F32), jax.ShapeDtypeStruct((N_SHARD, PLE_DIM, pw), F32),
                 jax.ShapeDtypeStruct((D_MODEL, D_MODEL), F32),
                 jax.ShapeDtypeStruct((1, D_MODEL), F32), jax.ShapeDtypeStruct((1, D_MODEL), F32)]
    scratch = [pltpu.VMEM((N_SHARD, PLE_DIM, pw), BF16), pltpu.VMEM((D_MODEL, D_MODEL), BF16),
               pltpu.VMEM((tm, D_MODEL), F32)]
    if with_kv:
        hc, dkv, nkv, w_kv = kv_args
        in_specs += [row, _row_spec(tm, 2 * KV_DIM), vec, ANY]
        args += [hc, dkv, nkv, w_kv]
        out_specs += [_const_spec((D_MODEL, 2 * KV_DIM)), vec]
        out_shape += [jax.ShapeDtypeStruct((D_MODEL, 2 * KV_DIM), F32), jax.ShapeDtypeStruct((1, D_MODEL), F32)]
        scratch.append(pltpu.VMEM((D_MODEL, 2 * KV_DIM), BF16))
    scratch.append(pltpu.SemaphoreType.DMA((3,)))
    return pl.pallas_call(
        body, name=f"ple_bwd{layer}", grid=(nt,), in_specs=in_specs, out_specs=out_specs,
        out_shape=out_shape, scratch_shapes=scratch, compiler_params=_params(),
    )(*args)


GROUP_ROWS = GQA_GROUP * BLOCK


def _stack_heads(x, kh):
    return jnp.concatenate([x[:, (kh * GQA_GROUP + g) * HEAD_DIM:(kh * GQA_GROUP + g + 1) * HEAD_DIM]
                            for g in range(GQA_GROUP)], axis=0)


def _attn_fwd(h, nmix, kv, sinks, w_q, w_o):
    T = h.shape[0]
    tm = min(512, T)
    nt = T // tm
    nb = tm // BLOCK

    def body(h_ref, n_ref, kv_ref, kvp_ref, sink_ref, w_q_hbm, w_o_hbm,
             out_ref, q_ref, ao_ref, p_ref, psink_ref, w_q_v, w_o_v, kvs_v, sem):
        _load_once([(w_q_hbm, w_q_v), (w_o_hbm, w_o_v)], sem)
        ti = pl.program_id(0)
        xv = h_ref[...]
        xn = _rms(xv, n_ref[...])[0].astype(BF16)
        q_ref[...] = (_dot(xn, w_q_v[...]) * (HEAD_DIM ** -0.5)).astype(BF16)
        kvs_v[0:BLOCK, :] = kvp_ref[...]
        kvs_v[BLOCK:, :] = kv_ref[...]
        lane = lax.broadcasted_iota(jnp.int32, (BLOCK, 128), 1)
        ii = lax.broadcasted_iota(jnp.int32, (BLOCK, 2 * BLOCK), 0)
        jj = lax.broadcasted_iota(jnp.int32, (BLOCK, 2 * BLOCK), 1)
        dist = ii + BLOCK - jj
        inband = (dist >= 0) & (dist < BLOCK)
        distf = dist.astype(F32)

        def blk_body(b, carry):
            r0 = pl.multiple_of(b * BLOCK, BLOCK)
            valid = inband & ((jj >= BLOCK) | jnp.logical_not(jnp.logical_and(ti == 0, b == 0)))
            qb = q_ref[pl.ds(r0, BLOCK), :]
            band = kvs_v[pl.ds(r0, 2 * BLOCK), :]
            psink_mat = jnp.zeros((BLOCK, 128), F32)
            outs = []
            for hq in range(N_Q_HEADS):
                kh, g = divmod(hq, GQA_GROUP)
                k_h = band[:, kh * HEAD_DIM:(kh + 1) * HEAD_DIM]
                v_h = band[:, KV_DIM + kh * HEAD_DIM:KV_DIM + (kh + 1) * HEAD_DIM]
                s = _dot_nt(qb[:, hq * HEAD_DIM:(hq + 1) * HEAD_DIM], k_h) - _SLOPES[hq] * distf
                s = jnp.where(valid, s, NEG)
                sink = sink_ref[hq]
                m = jnp.maximum(jnp.max(s, axis=1, keepdims=True), sink)
                e = jnp.exp(s - m)
                esink = jnp.exp(sink - m)
                inv = 1.0 / (jnp.sum(e, axis=1, keepdims=True) + esink)
                pb = (e * inv).astype(BF16)
                p_ref[b, kh, g * BLOCK:(g + 1) * BLOCK, :] = pb
                outs.append(_dot(pb, v_h))
                psink_mat = jnp.where(lane == hq, esink * inv, psink_mat)
            ao_ref[pl.ds(r0, BLOCK), :] = jnp.concatenate(outs, axis=1).astype(BF16)
            psink_ref[pl.ds(r0, BLOCK), :] = psink_mat
            return carry

        lax.fori_loop(0, nb, blk_body, 0)
        out_ref[...] = xv + _dot(ao_ref[...], w_o_v[...])

    row = _row_spec(tm, D_MODEL)
    prev_spec = pl.BlockSpec((BLOCK, 2 * KV_DIM), lambda i: (jnp.maximum(i * nb - 1, 0), 0))
    return pl.pallas_call(
        body, name="attn_fwd", grid=(nt,),
        in_specs=[row, _const_spec((1, D_MODEL)), _row_spec(tm, 2 * KV_DIM), prev_spec, SMEM, ANY, ANY],
        out_specs=[row, row, row, pl.BlockSpec((nb, N_KV_HEADS, GROUP_ROWS, 2 * BLOCK), lambda i: (i, 0, 0, 0)),
                   _row_spec(tm, 128)],
        out_shape=[jax.ShapeDtypeStruct((T, D_MODEL), F32), jax.ShapeDtypeStruct((T, D_MODEL), BF16),
                   jax.ShapeDtypeStruct((T, D_MODEL), BF16),
                   jax.ShapeDtypeStruct((T // BLOCK, N_KV_HEADS, GROUP_ROWS, 2 * BLOCK), BF16),
                   jax.ShapeDtypeStruct((T, 128), F32)],
        scratch_shapes=[pltpu.VMEM((D_MODEL, D_MODEL), BF16), pltpu.VMEM((D_MODEL, D_MODEL), BF16),
                        pltpu.VMEM((tm + BLOCK, 2 * KV_DIM), BF16), pltpu.SemaphoreType.DMA((2,))],
        compiler_params=_params(),
    )(h, nmix, kv, kv, sinks, w_q, w_o)


def _attn_bwd(dh, h, q, kv, ao, p, psink, nmix, w_q, w_o):
    T = h.shape[0]
    tm = min(512, T)
    nt = T // tm
    nb = tm // BLOCK

    def body(dh_ref, h_ref, q_ref, kv_ref, kvp_ref, ao_ref, p_ref, psink_ref, n_ref, w_q_hbm, w_o_hbm,
             dhin_ref, dwq_ref, dwo_ref, dkv_ref, dsink_ref, dn_ref,
             w_q_v, w_o_v, kvs_v, dao_v, dq_v, dkv_v, carry_v, sem):
        _load_once([(w_q_hbm, w_q_v), (w_o_hbm, w_o_v)], sem)
        _zero_first([carry_v, dsink_ref, dn_ref, dwq_ref, dwo_ref])
        dout = dh_ref[...]
        doutb = dout.astype(BF16)
        dao_v[...] = _dot_nt(doutb, w_o_v[...])
        dwo_ref[...] += _dot_tn(ao_ref[...], doutb)
        kvs_v[0:BLOCK, :] = kvp_ref[...]
        kvs_v[BLOCK:, :] = kv_ref[...]
        dkv_v[0:tm, :] = jnp.zeros((tm, 2 * KV_DIM), F32)
        dkv_v[tm:, :] = carry_v[...]
        seg = (lax.broadcasted_iota(jnp.int32, (D_MODEL, 128), 0) // HEAD_DIM
               == lax.broadcasted_iota(jnp.int32, (D_MODEL, 128), 1)).astype(BF16)

        def blk_body(b, dsk):
            r0 = pl.multiple_of(b * BLOCK, BLOCK)
            qb = q_ref[pl.ds(r0, BLOCK), :]
            band = kvs_v[pl.ds(r0, 2 * BLOCK), :]
            aob = ao_ref[pl.ds(r0, BLOCK), :].astype(F32)
            daob = dao_v[pl.ds(r0, BLOCK), :]
            prod = daob * aob
            head = prod.astype(BF16)
            tail = (prod - head.astype(F32)).astype(BF16)
            dsk = dsk + psink_ref[pl.ds(r0, BLOCK), :] * (_dot(head, seg) + _dot(tail, seg))
            dqs = []
            dks = []
            dvs = []
            for kh in range(N_KV_HEADS):
                k_h = band[:, kh * HEAD_DIM:(kh + 1) * HEAD_DIM]
                v_h = band[:, KV_DIM + kh * HEAD_DIM:KV_DIM + (kh + 1) * HEAD_DIM]
                q_g = _stack_heads(qb, kh)
                dao_g = _stack_heads(daob, kh)
                prb = p_ref[b, kh]
                pr = prb.astype(F32)
                dd = jnp.sum(dao_g * _stack_heads(aob, kh), axis=1, keepdims=True)
                dao_gb = dao_g.astype(BF16)
                dp = _dot_nt(dao_gb, v_h)
                dsb = (pr * (dp - dd)).astype(BF16)
                dq_g = _dot(dsb, k_h) * (HEAD_DIM ** -0.5)
                dks.append(_dot_tn(dsb, q_g))
                dvs.append(_dot_tn(prb, dao_gb))
                for g in range(GQA_GROUP):
                    dqs.append(dq_g[g * BLOCK:(g + 1) * BLOCK])
            dq_v[pl.ds(r0, BLOCK), :] = jnp.concatenate(dqs, axis=1)
            dkv_v[pl.ds(r0, 2 * BLOCK), :] += jnp.concatenate(dks + dvs, axis=1)
            return dsk

        dsk = lax.fori_loop(0, nb, blk_body, jnp.zeros((BLOCK, 128), F32))
        dsink_ref[...] -= jnp.sum(dsk, axis=0, keepdims=True)
        dqb = dq_v[...].astype(BF16)
        dxn = _dot_nt(dqb, w_q_v[...])
        xn, xh, r = _rms(h_ref[...], n_ref[...])
        dwq_ref[...] += _dot_tn(xn.astype(BF16), dqb)
        dxx, dn = _rms_bwd(dxn, xh, r, n_ref[...])
        dn_ref[...] += dn
        dhin_ref[...] = dout + dxx
        dkv_ref[...] = dkv_v[BLOCK:, :]
        carry_v[...] = dkv_v[0:BLOCK, :]

    rev = functools.partial(_row_spec, rev_nt=nt)
    row = rev(tm, D_MODEL)
    prev_spec = pl.BlockSpec((BLOCK, 2 * KV_DIM), lambda i: (jnp.maximum((nt - 1 - i) * nb - 1, 0), 0))
    return pl.pallas_call(
        body, name="attn_bwd", grid=(nt,),
        in_specs=[row, row, row, rev(tm, 2 * KV_DIM), prev_spec, row,
                  pl.BlockSpec((nb, N_KV_HEADS, GROUP_ROWS, 2 * BLOCK), lambda i: (nt - 1 - i, 0, 0, 0)),
                  rev(tm, 128), _const_spec((1, D_MODEL)), ANY, ANY],
        out_specs=[row, _const_spec((D_MODEL, D_MODEL)), _const_spec((D_MODEL, D_MODEL)), rev(tm, 2 * KV_DIM),
                   _const_spec((8, 128)), _const_spec((1, D_MODEL))],
        out_shape=[jax.ShapeDtypeStruct((T, D_MODEL), F32), jax.ShapeDtypeStruct((D_MODEL, D_MODEL), F32),
                   jax.ShapeDtypeStruct((D_MODEL, D_MODEL), F32), jax.ShapeDtypeStruct((T, 2 * KV_DIM), F32),
                   jax.ShapeDtypeStruct((8, 128), F32), jax.ShapeDtypeStruct((1, D_MODEL), F32)],
        scratch_shapes=[pltpu.VMEM((D_MODEL, D_MODEL), BF16), pltpu.VMEM((D_MODEL, D_MODEL), BF16),
                        pltpu.VMEM((tm + BLOCK, 2 * KV_DIM), BF16), pltpu.VMEM((tm, D_MODEL), F32),
                        pltpu.VMEM((tm, D_MODEL), F32), pltpu.VMEM((tm + BLOCK, 2 * KV_DIM), F32),
                        pltpu.VMEM((BLOCK, 2 * KV_DIM), F32), pltpu.SemaphoreType.DMA((2,))],
        compiler_params=_params(),
    )(dh, h, q, kv, kv, ao, p, psink, nmix, w_q, w_o)


def _mesh_pos():
    return lax.axis_index("x"), lax.axis_index("y"), lax.axis_index("c")


def _other_chips(x, y):
    return [(1 - x, y), (x, 1 - y), (1 - x, 1 - y)]


HBM_SPEC = pl.BlockSpec(memory_space=pltpu.HBM)
SEM_SPEC = pl.BlockSpec(memory_space=pltpu.SEMAPHORE)


def _split_call(name, bufs, waits=(), starts=(), after=()):
    n, nw, ns, na = len(bufs), len(waits), len(starts), len(after)

    def body(*refs):
        brefs = refs[:n]
        wsems = [(refs[n + 2 * k], refs[n + 2 * k + 1]) for k in range(nw)]
        o = n + 2 * nw + na
        ssems = [(refs[o + 2 * k], refs[o + 2 * k + 1]) for k in range(ns)]
        for (ss, rs), (_, _, fn) in zip(wsems, waits):
            for sending, arriving in fn(brefs, ss, rs):
                sending.wait_send()
                arriving.wait_recv()
        for (ss, rs), (_, fn) in zip(ssems, starts):
            for sending, _ in fn(brefs, ss, rs):
                sending.start()
        if ns:
            token = refs[o + 2 * ns + n]
            token[...] = jnp.zeros(token.shape, token.dtype)

    out_shape, out_specs = [], []
    for cnt, _ in starts:
        out_shape += [pltpu.SemaphoreType.DMA((cnt,)), pltpu.SemaphoreType.DMA((cnt,))]
        out_specs += [SEM_SPEC, SEM_SPEC]
    out_shape += [pltpu.HBM(b.shape, b.dtype) for b in bufs]
    out_specs += [HBM_SPEC] * n
    if ns:
        out_shape.append(jax.ShapeDtypeStruct((8, 128), F32))
        out_specs.append(pl.BlockSpec(memory_space=pltpu.VMEM))
    args = [pltpu.with_memory_space_constraint(b, pltpu.HBM) for b in bufs]
    for ss, rs, _ in waits:
        args += [ss, rs]
    args += list(after)
    res = pl.pallas_call(
        body, name=name, out_shape=tuple(out_shape),
        in_specs=[HBM_SPEC] * n + [SEM_SPEC] * (2 * nw) + [ANY] * na, out_specs=tuple(out_specs),
        input_output_aliases={i: 2 * ns + i for i in range(n)},
        compiler_params=pltpu.CompilerParams(has_side_effects=pltpu.SideEffectType.DATAFLOW_SIDE_EFFECTING),
    )(*args)
    sems = [(res[2 * k], res[2 * k + 1]) for k in range(ns)]
    return list(res[2 * ns:2 * ns + n]), sems, (res[2 * ns + n] if ns else None)


def _cast_place(items, name, deps=()):
    n = len(items)
    mats = [a.shape[-2:] for a, _, _ in items]

    def body(*refs):
        ins, outs, scr, sem = refs[:n], refs[n:2 * n], refs[2 * n:3 * n], refs[3 * n]
        x, y, _ = _mesh_pos()
        cps = []
        for t in range(n):
            scr[t][...] = ins[t][...].astype(scr[t].dtype)
            cp = pltpu.make_async_copy(scr[t], outs[t].at[2 * x + y], sem.at[t])
            cp.start()
            cps.append(cp)
        for cp in cps:
            cp.wait()

    def spec(idx, shape):
        return pl.BlockSpec((None,) * len(idx) + tuple(shape), lambda i: tuple(idx) + (0, 0))

    body, in_specs, args = _add_deps(body, [spec(idx, mat) for (_, idx, _), mat in zip(items, mats)],
                                     [a for a, _, _ in items], deps)
    return pl.pallas_call(
        body, name=name, grid=(1,), in_specs=in_specs, out_specs=[ANY] * n,
        out_shape=[jax.ShapeDtypeStruct((N_SHARD,) + tuple(mat), dt) for (_, _, dt), mat in zip(items, mats)],
        scratch_shapes=[pltpu.VMEM(tuple(mat), dt) for (_, _, dt), mat in zip(items, mats)]
        + [pltpu.SemaphoreType.DMA((n,))],
        compiler_params=_params(),
    )(*args)


def _gather_ici(idx):
    def fn(bufs, ss, rs):
        x, y, c = _mesh_pos()
        pairs = []
        for k, t in enumerate(idx):
            half = bufs[t].shape[1] // 2
            mine = bufs[t].at[2 * x + y, pl.ds(c * half, half), :]
            for j, (cx, cy) in enumerate(_other_chips(x, y)):
                theirs = bufs[t].at[2 * cx + cy, pl.ds(c * half, half), :]
                sem = dict(send_sem=ss.at[3 * k + j], recv_sem=rs.at[3 * k + j],
                           device_id=(cx, cy, c), device_id_type=MESH)
                pairs.append((pltpu.make_async_remote_copy(src_ref=mine, dst_ref=mine, **sem),
                              pltpu.make_async_remote_copy(src_ref=mine, dst_ref=theirs, **sem)))
        return pairs
    return fn


def _gather_d2d(idx):
    def fn(bufs, ss, rs):
        x, y, c = _mesh_pos()
        pairs = []
        for k, t in enumerate(idx):
            half = bufs[t].shape[1] // 2
            for j, (cx, cy) in enumerate(_other_chips(x, y)):
                got = bufs[t].at[2 * cx + cy, pl.ds(c * half, half), :]
                theirs = bufs[t].at[2 * cx + cy, pl.ds((1 - c) * half, half), :]
                sem = dict(send_sem=ss.at[3 * k + j], recv_sem=rs.at[3 * k + j],
                           device_id=(x, y, 1 - c), device_id_type=MESH)
                pairs.append((pltpu.make_async_remote_copy(src_ref=got, dst_ref=got, **sem),
                              pltpu.make_async_remote_copy(src_ref=got, dst_ref=theirs, **sem)))
        return pairs
    return fn


def _alloc(shapes, name):
    def body(*refs):
        pass

    return pl.pallas_call(body, name=name, out_specs=[ANY] * len(shapes),
                          out_shape=[jax.ShapeDtypeStruct(s, d) for s, d in shapes])()


def _send_to_sibling(n):
    def fn(bufs, ss, rs):
        x, y, c = _mesh_pos()
        pairs = []
        for t in range(n):
            src = bufs[t]
            if len(src.shape) == 3:
                half = src.shape[1] // 2
                src = src.at[:, pl.ds((1 - c) * half, half), :]
            cp = pltpu.make_async_remote_copy(src_ref=src, dst_ref=bufs[n + t], send_sem=ss.at[t],
                                              recv_sem=rs.at[t], device_id=(x, y, 1 - c), device_id_type=MESH)
            pairs.append((cp, cp))
        return pairs
    return fn


def _send_to_chips(n):
    def fn(bufs, ss, rs):
        x, y, c = _mesh_pos()
        pairs = []
        for j, (cx, cy) in enumerate(_other_chips(x, y)):
            for t in range(n):
                src = bufs[t].at[j] if len(bufs[t].shape) == 3 else bufs[t]
                cp = pltpu.make_async_remote_copy(src_ref=src, dst_ref=bufs[n + t].at[j], send_sem=ss.at[3 * t + j],
                                                  recv_sem=rs.at[3 * t + j], device_id=(cx, cy, c),
                                                  device_id_type=MESH)
                pairs.append((cp, cp))
        return pairs
    return fn


class _Exchange:
    def __init__(self, name, srcs, land_shapes, fn, n_sems):
        self.name, self.fn = name, fn
        lands = _alloc(land_shapes, name + "_alloc")
        self.n = len(srcs)
        self.bufs, sems, self.token = _split_call(name + "_start", list(srcs) + list(lands),
                                                  starts=[(n_sems, fn)])
        self.sems = sems[0]

    def finish(self, after=()):
        bufs, _, _ = _split_call(self.name + "_wait", self.bufs, waits=[(*self.sems, self.fn)], after=after)
        return bufs[:self.n], bufs[self.n:]


def _row_block(rows, cols, mult=8, limit=3 * 512 * 1024, itemsize=4):
    best = None
    for br in range(mult, rows + 1, mult):
        if rows % br == 0 and br * cols * itemsize <= limit:
            best = br
    assert best is not None, (rows, cols)
    return best


_GROUP_BLOCK_BYTES = 1024 * 1024


def _group_plan(ss):
    plan = []
    for s in ss:
        half, cols = s.shape[-2:]
        br = _row_block(half, cols, mult=16, limit=_GROUP_BLOCK_BYTES)
        plan.append((br, half // br))
    return plan, max(nr for _, nr in plan)


def _chip_partial(gs, ss, ids, name):
    n = len(gs)
    plan, steps = _group_plan(ss)

    def body(ids_ref, *refs):
        for t in range(n):
            refs[2 * n + t][...] = (refs[t][...] + refs[n + t][...]).astype(BF16)

    g_specs, s_specs, o_specs = [], [], []
    for (br, nr), s in zip(plan, ss):
        blk = (None, br, s.shape[2])
        g_specs.append(pl.BlockSpec(
            blk, lambda j, r, ids_ref, nr=nr: (ids_ref[2 + j], ids_ref[0] * nr + jnp.minimum(r, nr - 1), 0)))
        s_specs.append(pl.BlockSpec(blk, lambda j, r, ids_ref, nr=nr: (ids_ref[2 + j], jnp.minimum(r, nr - 1), 0)))
        o_specs.append(pl.BlockSpec(blk, lambda j, r, ids_ref, nr=nr: (j, jnp.minimum(r, nr - 1), 0)))
    return pl.pallas_call(
        body, name=name,
        grid_spec=pltpu.PrefetchScalarGridSpec(num_scalar_prefetch=1, grid=(3, steps),
                                               in_specs=g_specs + s_specs, out_specs=o_specs),
        out_shape=[jax.ShapeDtypeStruct((3,) + s.shape[1:], BF16) for s in ss],
        compiler_params=pltpu.CompilerParams(dimension_semantics=("arbitrary", "arbitrary"),
                                             vmem_limit_bytes=VMEM_LIMIT),
    )(ids, *gs, *ss)


def _chip_sum(gs, ss, qs, ids, name):
    n = len(gs)
    plan, steps = _group_plan(ss)

    def body(ids_ref, *refs):
        for t in range(n):
            q_ref = refs[2 * n + t]
            own = refs[t][...] + refs[n + t][...]
            refs[3 * n + t][...] = (own + q_ref[2].astype(F32)) + (q_ref[0].astype(F32) + q_ref[1].astype(F32))

    g_specs, s_specs, q_specs, o_specs = [], [], [], []
    for (br, nr), s in zip(plan, ss):
        cols = s.shape[2]
        g_specs.append(pl.BlockSpec(
            (None, br, cols), lambda r, ids_ref, nr=nr: (ids_ref[1], ids_ref[0] * nr + jnp.minimum(r, nr - 1), 0)))
        s_specs.append(pl.BlockSpec((None, br, cols), lambda r, ids_ref, nr=nr: (ids_ref[1], jnp.minimum(r, nr - 1), 0)))
        q_specs.append(pl.BlockSpec((3, br, cols), lambda r, ids_ref, nr=nr: (0, jnp.minimum(r, nr - 1), 0)))
        o_specs.append(pl.BlockSpec((br, cols), lambda r, ids_ref, nr=nr: (jnp.minimum(r, nr - 1), 0)))
    return pl.pallas_call(
        body, name=name,
        grid_spec=pltpu.PrefetchScalarGridSpec(num_scalar_prefetch=1, grid=(steps,),
                                               in_specs=g_specs + s_specs + q_specs, out_specs=o_specs),
        out_shape=[jax.ShapeDtypeStruct(s.shape[1:], F32) for s in ss],
        compiler_params=pltpu.CompilerParams(dimension_semantics=("arbitrary",), vmem_limit_bytes=VMEM_LIMIT),
    )(ids, *gs, *ss, *qs)


def _adamw_math(w, g, m, v):
    mn = ADAM_B1 * m + (1.0 - ADAM_B1) * g
    vn = ADAM_B2 * v + (1.0 - ADAM_B2) * (g * g)
    m_hat = mn / (1.0 - ADAM_B1 ** ADAM_STEP)
    v_hat = vn / (1.0 - ADAM_B2 ** ADAM_STEP)
    return -ADAM_LR * (m_hat / (jnp.sqrt(v_hat) + ADAM_EPS) + ADAM_WD * w), mn, vn


def _adamw_halves(w, own, sib, m, v, ids, name, layer=0, n_layers=1, stacked=None):
    C = w.shape[1]
    R = w.shape[0] // n_layers
    half = R // 2
    br = _row_block(half, C)
    nh = half // br
    base = layer * 2 * nh

    def body(ids_ref, w_ref, own_ref, sib_ref, m_ref, v_ref, *rest):
        g_ref, d_ref, mo_ref, vo_ref = rest[-4:]
        is_own = (pl.program_id(0) // nh) == ids_ref[0]
        g = jnp.where(is_own, own_ref[...], sib_ref[...])
        g_ref[...] = g
        d_ref[...], mo_ref[...], vo_ref[...] = _adamw_math(w_ref[...], g, m_ref[...], v_ref[...])

    full = pl.BlockSpec((br, C), lambda r, ids_ref: (base + r, 0))
    own_spec = pl.BlockSpec((br, C), lambda r, ids_ref: (jnp.clip(r - ids_ref[0] * nh, 0, nh - 1), 0))
    sib_spec = pl.BlockSpec((br, C), lambda r, ids_ref: (jnp.clip(r - (1 - ids_ref[0]) * nh, 0, nh - 1), 0))
    in_specs = [full, own_spec, sib_spec, full, full]
    args = [ids, w, own, sib, m, v]
    aliases = {}
    if stacked is not None:
        in_specs += [ANY] * 4
        args += list(stacked)
        aliases = {6 + k: k for k in range(4)}
    return pl.pallas_call(
        body, name=name,
        grid_spec=pltpu.PrefetchScalarGridSpec(
            num_scalar_prefetch=1, grid=(2 * nh,), in_specs=in_specs, out_specs=[full] * 4),
        out_shape=[jax.ShapeDtypeStruct(w.shape, F32)] * 4, input_output_aliases=aliases,
        compiler_params=_params(),
    )(*args)


_PACK_UNIT = 1024


def _pack(arrs):
    flat = []
    for a in arrs:
        f = a.reshape(-1).astype(F32)
        pad = (-f.shape[0]) % _PACK_UNIT
        if pad:
            f = jnp.concatenate([f, jnp.zeros((pad,), F32)])
        flat.append(f)
    return jnp.concatenate(flat).reshape(-1, 128)


def kernel(x, p, norm_mix, norm_ffn, norm_ple, norm_kv, norm_final, a_w_in, a_norm_v, a_w_s, a_b_s, a_w_out, w_kv, b_w_q, b_sinks, b_w_o, f_w_up, f_conv_w, f_conv_b, f_w_down, ple_w_in, ple_w_gate, ple_b_gate, loss_target, m_norm_mix, m_norm_ffn, m_norm_ple, m_norm_kv, m_norm_final, m_a_w_in, m_a_norm_v, m_a_w_s, m_a_b_s, m_a_w_out, m_w_kv, m_b_w_q, m_b_sinks, m_b_w_o, m_f_w_up, m_f_conv_w, m_f_conv_b, m_f_w_down, m_ple_w_in, m_ple_w_gate, m_ple_b_gate, v_norm_mix, v_norm_ffn, v_norm_ple, v_norm_kv, v_norm_final, v_a_w_in, v_a_norm_v, v_a_w_s, v_a_b_s, v_a_w_out, v_w_kv, v_b_w_q, v_b_sinks, v_b_w_o, v_f_w_up, v_f_conv_w, v_f_conv_b, v_f_w_down, v_ple_w_in, v_ple_w_gate, v_ple_b_gate):
    given = dict(locals())

    small_shard = _pack([a_norm_v, f_conv_w])
    pad_rows = (-small_shard.shape[0]) % 16
    if pad_rows:
        small_shard = jnp.concatenate([small_shard, jnp.zeros((pad_rows, 128), F32)])
    groups = [
        [(a_w_in, (0,), BF16), (a_w_out, (0,), BF16), (small_shard, (), F32)],
        [(f_w_up, (0,), BF16), (f_w_down, (0,), BF16)],
        [(ple_w_in, (0,), BF16), (ple_w_gate, (0,), BF16), (w_kv, (), BF16), (b_w_q, (0,), BF16),
         (b_w_o, (0,), BF16), (f_w_up, (1,), BF16), (f_w_down, (1,), BF16), (ple_w_in, (1,), BF16),
         (ple_w_gate, (1,), BF16)],
    ]
    first = list(range(len(groups[0])))
    lands0, sems0, token0 = _split_call("gather_start_g0", _cast_place(groups[0], "cast_place_g0"),
                                        starts=[(3 * len(first), _gather_ici(first))])
    rest, spans, start = [], [], 0
    for gi, items in enumerate(groups[1:], 1):
        rest += _cast_place(items, f"cast_place_g{gi}", deps=(token0,))
        spans.append(list(range(start, start + len(items))))
        start += len(items)
    rest, rest_sems, rest_token = _split_call("gather_start", rest,
                                              starts=[(3 * len(sp), _gather_ici(sp)) for sp in spans])
    group_bufs = [lands0] + [[rest[t] for t in sp] for sp in spans]
    ici_sems = sems0 + rest_sems

    def finish_group(gi, after):
        bufs = group_bufs[gi]
        local = list(range(len(bufs)))
        bufs, d2d_sems, _ = _split_call(f"gather_pass_g{gi}", bufs, waits=[(*ici_sems[gi], _gather_ici(local))],
                                        starts=[(3 * len(local), _gather_d2d(local))], after=after)
        bufs, _, _ = _split_call(f"gather_done_g{gi}", bufs, waits=[(*d2d_sems[0], _gather_d2d(local))])
        return bufs

    def stage0():
        b_in, b_out, b_small = finish_group(0, (rest_token,))
        small_full = b_small.reshape(N_SHARD, -1)
        gv_full = small_full[:, :256].reshape(1, D_MODEL)
        cw_full = small_full[:, _PACK_UNIT:_PACK_UNIT + 2 * 3 * FF_BLK].reshape(N_SHARD, 2, 3, FF_BLK)
        cw_full = jnp.transpose(cw_full, (1, 2, 0, 3)).reshape(2, 3, N_FF)
        return gv_full, cw_full, b_in, b_out.reshape(D_MODEL, D_MODEL)

    def stage1(after):
        b_up, b_dn = finish_group(1, after)
        return b_up, b_dn.reshape(D_FF, D_MODEL)

    def stage2(after):
        pin0, gate0, kv_w, wq, wo, up1, dn1, pin1, gate1 = finish_group(2, after)
        sq = lambda a: a.reshape(D_MODEL, -1)
        return dict(w_pin=[pin0, pin1], w_gate=[sq(gate0), sq(gate1)], w_kv=sq(kv_w), w_q=sq(wq), w_o=sq(wo),
                    w_up1=up1, w_dn1=dn1.reshape(D_FF, D_MODEL))

    dx, (loss, (out_g, out_d, out_m, out_v)) = _local_step(
        x[0], p[0, 0], p[1, 0], loss_target[0], norm_mix, norm_ffn, norm_ple, norm_kv, norm_final, a_w_s, a_b_s,
        b_sinks, f_conv_b, ple_b_gate, stage0, stage1, stage2, _Reducer(given))
    weight_names = ['norm_mix', 'norm_ffn', 'norm_ple', 'norm_kv', 'norm_final', 'a_w_in', 'a_norm_v', 'a_w_s',
                    'a_b_s', 'a_w_out', 'w_kv', 'b_w_q', 'b_sinks', 'b_w_o', 'f_w_up', 'f_conv_w', 'f_conv_b',
                    'f_w_down', 'ple_w_in', 'ple_w_gate', 'ple_b_gate']
    return (loss, dx.reshape(x.shape), *[out_g[k] for k in weight_names], *[out_d[k] for k in weight_names],
            *[out_m[k] for k in weight_names], *[out_v[k] for k in weight_names])


def _local_step(xs, p0, p1, tgt, norm_mix, norm_ffn, norm_ple, norm_kv, norm_final, a_w_s, a_b_s, b_sinks,
                f_conv_b, ple_b_gate, stage0, stage1, stage2, sched):
    tril = jnp.tril(jnp.ones((CHUNK, CHUNK), F32))
    wsm = (a_w_s[0] * tril[None]).astype(BF16)
    bsb = jnp.broadcast_to(a_b_s[0][:, :, None], (A_GROUPS, CHUNK, CHUNK))
    sinks = b_sinks[0]
    row = lambda a: a.reshape(1, -1)

    gv_full, cw_full, w_in, w_out = stage0()
    h1, zp = _mixer_a_fwd(xs, row(norm_mix[0]), gv_full, wsm, bsb, w_in, w_out)
    w_up0, w_dn0 = stage1((h1,))
    h2, hh0, c0 = _ffn_fwd(h1, row(norm_ffn[0]), cw_full[0], row(f_conv_b[0]), w_up0, w_dn0, 0)
    rest = stage2((h2,))
    w_pin, w_gate, w_kv_f, w_q, w_o = rest['w_pin'], rest['w_gate'], rest['w_kv'], rest['w_q'], rest['w_o']
    w_up = [w_up0, rest['w_up1']]
    w_dn = [w_dn0, rest['w_dn1']]
    h3, kv = _ple_fwd_kv(h2, p0, row(norm_ple[0]), row(ple_b_gate[0]), row(norm_kv), w_pin[0], w_gate[0], w_kv_f)
    h4, q, ao, probs, psink = _attn_fwd(h3, row(norm_mix[1]), kv, sinks, w_q, w_o)
    h5, hh1, c1 = _ffn_fwd(h4, row(norm_ffn[1]), cw_full[1], row(f_conv_b[1]), w_up[1], w_dn[1], 1)
    dh6, loss_acc, dn_final = _ple_fwd_final(
        h5, p1, tgt, row(norm_ple[1]), row(ple_b_gate[1]), row(norm_final), w_pin[1], w_gate[1])

    def pieces(g):
        return g.reshape(N_SHARD, -1, g.shape[-1])

    dh5, g_pin1, g_gate1, dbg1, dnple1 = _ple_bwd(dh6, h5, p1, row(norm_ple[1]), row(ple_b_gate[1]), w_pin[1], w_gate[1], 1)
    early = {('ple_w_in', 1): g_pin1, ('ple_w_gate', 1): pieces(g_gate1)}
    dh4, g_up1, g_dn1, dcw1, dcb1, dnffn1 = _ffn_bwd(
        dh5, h4, hh1, c1, row(norm_ffn[1]), cw_full[1], w_up[1], w_dn[1], 1)
    early['f_w_down', 1] = pieces(g_dn1)
    early['f_w_up', 1] = g_up1
    dh3a, g_wq, g_wo, dkv, dsink, dnmix1 = _attn_bwd(dh4, h3, q, kv, ao, probs, psink, row(norm_mix[1]), w_q, w_o)
    early['b_w_o', 0] = pieces(g_wo)
    early['b_w_q', 0] = pieces(g_wq)
    dh2, g_pin0, g_gate0, dbg0, dnple0, g_wkv, dnkv = _ple_bwd(
        dh3a, h2, p0, row(norm_ple[0]), row(ple_b_gate[0]), w_pin[0], w_gate[0], 0,
        kv_args=(h3, dkv, row(norm_kv), w_kv_f))
    early['w_kv', 0] = pieces(g_wkv)
    early['ple_w_in', 0] = g_pin0
    early['ple_w_gate', 0] = pieces(g_gate0)
    deps = sched.early_ready(early)
    dh1, g_up0, g_dn0, dcw0, dcb0, dnffn0 = _ffn_bwd(
        dh2, h1, hh0, c0, row(norm_ffn[0]), cw_full[0], w_up[0], w_dn[0], 0, deps=deps,
        between=lambda part: sched.after_ffn_half((part,)))
    deps = sched.ffn0_ready({('f_w_down', 0): pieces(g_dn0), ('f_w_up', 0): g_up0})
    dx, g_win, g_wout, dws, dbs, dgv, dnmix0 = _mixer_a_bwd(
        dh1, xs, zp, row(norm_mix[0]), gv_full, wsm, bsb, tril, w_in, w_out, deps=deps)
    g_wout = pieces(g_wout)

    small_grads = {
        'norm_mix': jnp.concatenate([dnmix0, dnmix1]), 'norm_ffn': jnp.concatenate([dnffn0, dnffn1]),
        'norm_ple': jnp.concatenate([dnple0, dnple1]), 'norm_kv': dnkv, 'norm_final': dn_final,
        'a_norm_v': dgv, 'a_w_s': dws.reshape(A_GROUPS * CHUNK, CHUNK), 'a_b_s': dbs[:, :, 0],
        'b_sinks': dsink[0:1, :], 'f_conv_w': jnp.concatenate([dcw0, dcw1]),
        'f_conv_b': jnp.concatenate([dcb0, dcb1]), 'ple_b_gate': jnp.concatenate([dbg0, dbg1]),
        'loss': loss_acc,
    }
    outs = sched.finish({('a_w_in', 0): g_win, ('a_w_out', 0): g_wout}, small_grads, (dx,))
    return dx, outs


class _Reducer:
    def __init__(self, given):
        self.given = given
        cx, cy, cc = _mesh_pos()
        self.shard = 2 * cx + cy
        s = self.shard
        self.ids = jnp.stack([cc, s, s ^ 2, s ^ 1, s ^ 3]).astype(jnp.int32)
        self.out = [{}, {}, {}, {}]
        self.stacked = {}

    def _send(self, tag, grads, small=()):
        keys = list(grads)
        srcs = [grads[k] for k in keys] + list(small)
        shapes = [((N_SHARD, g.shape[1] // 2, g.shape[2]), F32) for g in srcs[:len(keys)]]
        shapes += [(s.shape, F32) for s in small]
        return keys, _Exchange(f"send_{tag}", srcs, shapes, _send_to_sibling(len(srcs)), len(srcs))

    def _exchange(self, tag, keys, send, after):
        srcs, lands = send.finish(after)
        n = len(keys)
        parts = _chip_partial(srcs[:n], lands[:n], self.ids, f"chip_partial_{tag}")
        shapes = [(p.shape, BF16) for p in parts]
        if len(srcs) > n:
            small = _small_add(srcs[n:], lands[n:])
            parts += small
            shapes += [((3,) + s.shape, F32) for s in small]
        exch = _Exchange(f"exch_{tag}", parts, shapes, _send_to_chips(len(parts)), 3 * len(parts))
        return (keys, srcs[:n], lands[:n], exch)

    def _swap(self, tag, state, after):
        keys, grads, sib, exch = state
        parts, recv = exch.finish(after)
        n = len(keys)
        own = _chip_sum(grads, sib, recv[:n], self.ids, f"chip_sum_{tag}")
        small_red = _small_sum(parts[n:], recv[n:]) if len(parts) > n else None
        return keys, _Exchange(f"swap_{tag}", own, [(o.shape, F32) for o in own], _send_to_sibling(n), n), small_red

    def _adamw(self, keys, swap, after):
        own, sib = swap.finish(after)
        last = None
        for (name, layer), o, s in zip(keys, own, sib):
            w = self.given[name]
            n_layers = w.shape[0] if w.ndim == 3 else 1
            c2 = w.shape[-1]
            res = _adamw_halves(w.reshape(-1, c2), o, s, self.given['m_' + name].reshape(-1, c2),
                                self.given['v_' + name].reshape(-1, c2), self.ids, f"adamw_{name}{layer}",
                                layer, n_layers, self.stacked.get(name))
            self.stacked[name] = res
            if layer == 0:
                for dst, r in zip(self.out, res):
                    dst[name] = r.reshape(w.shape)
            last = res[0]
        return last

    def early_ready(self, grads):
        self.e_keys, self.e_send = self._send("e", grads)
        return (self.e_send.token,)

    def after_ffn_half(self, after):
        self.e_state = self._exchange("e", self.e_keys, self.e_send, after)
        return (self.e_state[3].token,)

    def ffn0_ready(self, grads):
        _, self.e_swap, _ = self._swap("e", self.e_state, tuple(grads.values()))
        f_keys, f_send = self._send("f", grads)
        self.f_state = self._exchange("f", f_keys, f_send, ())
        return (self.f_state[3].token, self.e_swap.token)

    def finish(self, grads, small_grads, after):
        small_names = list(small_grads)
        a_keys, a_send = self._send("a", grads, [small_grads[k] for k in small_names])
        a_state = self._exchange("a", a_keys, a_send, after)
        e_done = self._adamw(self.e_keys, self.e_swap, (a_state[3].token,))
        f_keys, f_swap, _ = self._swap("f", self.f_state, (e_done,))
        f_done = self._adamw(f_keys, f_swap, ())
        _, a_swap, small_red = self._swap("a", a_state, (f_done,))
        self._adamw(a_keys, a_swap, ())

        given = self.given
        reduced = dict(zip(small_names, small_red))
        loss = reduced.pop('loss')[0, 0]
        names = list(reduced)
        items = []
        for k in names:
            g = reduced[k]
            cols = g.shape[1] // N_SHARD if k in ('a_norm_v', 'f_conv_w') else g.shape[1]
            view = lambda a: _lane_pad(a.reshape(g.shape[0], -1), cols)
            items.append((view(given[k]), g, view(given['m_' + k]), view(given['v_' + k])))
        res = _adamw_small(items, self.ids)
        for k, four in zip(names, res):
            width = given[k].size // four[0].shape[0]
            for dst, r in zip(self.out, four):
                dst[k] = r[:, :width].reshape(given[k].shape)
        return loss, self.out


def _lane_pad(a, cols):
    return a if a.shape[1] == cols else jnp.pad(a, ((0, 0), (0, cols - a.shape[1])))


def _small_add(a_list, b_list):
    n = len(a_list)

    def body(*refs):
        for t in range(n):
            refs[2 * n + t][...] = refs[t][...] + refs[n + t][...]

    return pl.pallas_call(body, name="chip_partial_small",
                          out_shape=[jax.ShapeDtypeStruct(a.shape, F32) for a in a_list])(*a_list, *b_list)


def _small_sum(parts, recvs):
    n = len(parts)

    def body(*refs):
        for t in range(n):
            q = refs[n + t]
            refs[2 * n + t][...] = (refs[t][...] + q[2]) + (q[0] + q[1])

    return pl.pallas_call(body, name="chip_sum_small",
                          out_shape=[jax.ShapeDtypeStruct(p.shape, F32) for p in parts])(*parts, *recvs)


def _adamw_small(items, ids):
    n = len(items)

    def body(ids_ref, *refs):
        for t in range(n):
            w_ref, g_ref, m_ref, v_ref = refs[4 * t:4 * t + 4]
            g_out, d_ref, mo_ref, vo_ref = refs[4 * n + 4 * t:4 * n + 4 * t + 4]
            g = g_ref[...]
            g_out[...] = g
            d_ref[...], mo_ref[...], vo_ref[...] = _adamw_math(w_ref[...], g, m_ref[...], v_ref[...])

    in_specs, out_specs, out_shape, args = [], [], [], []
    for w, g, m, v in items:
        full = pl.BlockSpec(w.shape, lambda i, ids_ref: (0, 0))
        g_spec = full if g.shape == w.shape else pl.BlockSpec(w.shape, lambda i, ids_ref: (0, ids_ref[1]))
        in_specs += [full, g_spec, full, full]
        out_specs += [full] * 4
        out_shape += [jax.ShapeDtypeStruct(w.shape, F32)] * 4
        args += [w, g, m, v]
    res = pl.pallas_call(
        body, name="adamw_small",
        grid_spec=pltpu.PrefetchScalarGridSpec(num_scalar_prefetch=1, grid=(1,), in_specs=in_specs,
                                               out_specs=out_specs),
        out_shape=out_shape, compiler_params=_params(),
    )(ids, *args)
    return [res[4 * t:4 * t + 4] for t in range(n)]
```

```python
import functools
import math

import numpy as np
import jax
import jax.numpy as jnp
from jax import lax
from jax.experimental import pallas as pl
from jax.experimental.pallas import tpu as pltpu

F32 = jnp.float32
BF16 = jnp.bfloat16

D_MODEL = 1024
CHUNK = 128
A_GROUPS = 8
HEAD_DIM = 64
N_Q_HEADS = 16
N_KV_HEADS = 4
GQA_GROUP = N_Q_HEADS // N_KV_HEADS
KV_DIM = N_KV_HEADS * HEAD_DIM
BLOCK = 128
D_FF = 2816
N_FF = 2 * D_FF
FF_BLK = N_FF // 4
PLE_DIM = 256
EPS = 1e-6
NEG = -1e30
N_SHARD = 4

ADAM_LR = 0.001
ADAM_B1 = 0.9
ADAM_B2 = 0.999
ADAM_EPS = 1e-08
ADAM_WD = 0.01
ADAM_STEP = 10

VMEM_LIMIT = 60 * 1024 * 1024
MESH = pl.DeviceIdType.MESH
ANY = pl.BlockSpec(memory_space=pl.ANY)
SMEM = pl.BlockSpec(memory_space=pltpu.SMEM)

_SLOPES = [float(np.float32(2.0 ** (-8.0 * (h + 1) / N_Q_HEADS))) for h in range(N_Q_HEADS)]


def _dot(a, b):
    return jnp.dot(a, b, preferred_element_type=F32)


def _dot_nt(a, b):
    return lax.dot_general(a, b, (((1,), (1,)), ((), ())), preferred_element_type=F32)


def _dot_tn(a, b):
    return lax.dot_general(a, b, (((0,), (0,)), ((), ())), preferred_element_type=F32)


def _rms(x, g):
    r = lax.rsqrt(jnp.mean(x * x, axis=-1, keepdims=True) + EPS)
    xh = x * r
    return xh * g, xh, r


def _rms_bwd(dy, xh, r, g):
    dxh = dy * g
    dg = jnp.sum(dy * xh, axis=0, keepdims=True)
    dx = r * (dxh - xh * jnp.mean(dxh * xh, axis=-1, keepdims=True))
    return dx, dg


_GELU_C = math.sqrt(2.0 / math.pi)


def _gelu(x):
    t = jnp.tanh(_GELU_C * (x + 0.044715 * (x * x * x)))
    return 0.5 * x * (1.0 + t)


def _gelu_grad(x):
    x2 = x * x
    t = jnp.tanh(_GELU_C * (x + 0.044715 * (x2 * x)))
    return 0.5 * (1.0 + t) + 0.5 * x * (1.0 - t * t) * (_GELU_C * (1.0 + 3.0 * 0.044715 * x2))


def _sigmoid(x):
    return 0.5 * jnp.tanh(0.5 * x) + 0.5


def _load_once(pairs, sem):
    @pl.when(pl.program_id(0) == 0)
    def _():
        cps = [pltpu.make_async_copy(s, d, sem.at[i]) for i, (s, d) in enumerate(pairs)]
        for cp in cps:
            cp.start()
        for cp in cps:
            cp.wait()


def _params(n_axes=1, vmem=VMEM_LIMIT):
    return pltpu.CompilerParams(dimension_semantics=("arbitrary",) * n_axes, vmem_limit_bytes=vmem)


def _row_spec(tm, n, rev_nt=None):
    if rev_nt is None:
        return pl.BlockSpec((tm, n), lambda i: (i, 0))
    return pl.BlockSpec((tm, n), lambda i: (rev_nt - 1 - i, 0))


def _const_spec(shape):
    nd = len(shape)
    return pl.BlockSpec(shape, lambda i: (0,) * nd)


def _add_deps(body, in_specs, args, deps):
    nd = len(deps)
    if nd == 0:
        return body, list(in_specs), list(args)

    def wrapped(*refs):
        return body(*refs[nd:])

    return wrapped, [ANY] * nd + list(in_specs), list(deps) + list(args)


def _zero_first(refs):
    @pl.when(pl.program_id(0) == 0)
    def _():
        for r in refs:
            r[...] = jnp.zeros(r.shape, r.dtype)


def _mixer_a_fwd(x, nmix, gv, wsm, bsb, w_in, w_out):
    T = x.shape[0]
    tm = min(512, T)
    nt = T // tm
    nw = 2 * D_MODEL // N_SHARD

    def body(x_ref, nmix_ref, gv_ref, ws_ref, bsb_ref, w_in_hbm, w_out_hbm,
             h1_ref, zp_ref, w_in_v, w_out_v, gated_v, sem):
        _load_once([(w_in_hbm, w_in_v), (w_out_hbm, w_out_v)], sem)
        xv = x_ref[...]
        xn = _rms(xv, nmix_ref[...])[0].astype(BF16)
        for j in range(N_SHARD):
            zp_ref[:, j * nw:(j + 1) * nw] = _dot(xn, w_in_v[j])
        z = _gelu(zp_ref[...])
        u = z[:, :D_MODEL]
        vn = _rms(z[:, D_MODEL:], gv_ref[...])[0].astype(BF16)
        for c in range(tm // CHUNK):
            rows = slice(c * CHUNK, (c + 1) * CHUNK)
            for h in range(A_GROUPS):
                cols = slice(h * CHUNK, (h + 1) * CHUNK)
                s = _dot(ws_ref[h], vn[rows, cols]) + bsb_ref[h]
                gated_v[rows, cols] = (u[rows, cols] * s).astype(BF16)
        h1_ref[...] = xv + _dot(gated_v[...], w_out_v[...])

    return pl.pallas_call(
        body, name="mixer_a_fwd", grid=(nt,),
        in_specs=[_row_spec(tm, D_MODEL), _const_spec((1, D_MODEL)), _const_spec((1, D_MODEL)),
                  _const_spec((A_GROUPS, CHUNK, CHUNK)), _const_spec((A_GROUPS, CHUNK, CHUNK)), ANY, ANY],
        out_specs=[_row_spec(tm, D_MODEL), _row_spec(tm, 2 * D_MODEL)],
        out_shape=[jax.ShapeDtypeStruct((T, D_MODEL), F32), jax.ShapeDtypeStruct((T, 2 * D_MODEL), F32)],
        scratch_shapes=[pltpu.VMEM((N_SHARD, D_MODEL, nw), BF16), pltpu.VMEM((D_MODEL, D_MODEL), BF16),
                        pltpu.VMEM((tm, D_MODEL), BF16), pltpu.SemaphoreType.DMA((2,))],
        compiler_params=_params(),
    )(x, nmix, gv, wsm, bsb, w_in, w_out)


def _mixer_a_bwd(dh, x, zp, nmix, gv, wsm, bsb, tril, w_in, w_out, deps=()):
    T = x.shape[0]
    tm = min(256, T)
    nt = T // tm
    nw = 2 * D_MODEL // N_SHARD

    def body(dh_ref, x_ref, zp_ref, nmix_ref, gv_ref, ws_ref, bsb_ref, tril_ref, w_in_hbm, w_out_hbm,
             dx_ref, dwin_ref, dwout_ref, dws_ref, dbs_ref, dgv_ref, dnmix_ref,
             w_in_v, w_out_v, du_v, dvn_v, dbs_v, gated_ref, sem):
        _load_once([(w_in_hbm, w_in_v), (w_out_hbm, w_out_v)], sem)
        _zero_first([dws_ref, dbs_v, dgv_ref, dnmix_ref, dwin_ref, dwout_ref])
        i = pl.program_id(0)
        dhv = dh_ref[...]
        dhb = dhv.astype(BF16)
        xv = x_ref[...]
        xn, xh, r = _rms(xv, nmix_ref[...])
        xnb = xn.astype(BF16)
        zpv = zp_ref[...]
        z = _gelu(zpv)
        u = z[:, :D_MODEL]
        vn_f, vh, rv = _rms(z[:, D_MODEL:], gv_ref[...])
        vn = vn_f.astype(BF16)
        dgated = _dot_nt(dhb, w_out_v[...])
        for c in range(tm // CHUNK):
            rows = slice(c * CHUNK, (c + 1) * CHUNK)
            for h in range(A_GROUPS):
                cols = slice(h * CHUNK, (h + 1) * CHUNK)
                vn_h = vn[rows, cols]
                s = _dot(ws_ref[h], vn_h) + bsb_ref[h]
                dgt = dgated[rows, cols]
                u_h = u[rows, cols]
                gated_ref[rows, cols] = (u_h * s).astype(BF16)
                du_v[rows, cols] = dgt * s
                ds = dgt * u_h
                dsb = ds.astype(BF16)
                dws_ref[h] += _dot_nt(dsb, vn_h)
                dbs_v[h] += ds
                dvn_v[rows, cols] = _dot_tn(ws_ref[h], dsb)
        dwout_ref[...] += _dot_tn(gated_ref[...], dhb)
        dv, dgv = _rms_bwd(dvn_v[...], vh, rv, gv_ref[...])
        dgv_ref[...] += dgv
        dzu = (du_v[...] * _gelu_grad(zpv[:, :D_MODEL])).astype(BF16)
        dzv = (dv * _gelu_grad(zpv[:, D_MODEL:])).astype(BF16)
        dzs = (dzu[:, :nw], dzu[:, nw:], dzv[:, :nw], dzv[:, nw:])
        dxn = jnp.zeros((tm, D_MODEL), F32)
        for j in range(N_SHARD):
            dxn = dxn + _dot_nt(dzs[j], w_in_v[j])
            dwin_ref[j] += _dot_tn(xnb, dzs[j])
        dxx, dn = _rms_bwd(dxn, xh, r, nmix_ref[...])
        dnmix_ref[...] += dn
        dx_ref[...] = dhv + dxx

        @pl.when(i == nt - 1)
        def _():
            for h in range(A_GROUPS):
                dws_ref[h] = dws_ref[h] * tril_ref[...]
                dbs_ref[h] = jnp.broadcast_to(jnp.sum(dbs_v[h], axis=1, keepdims=True), (CHUNK, CHUNK))

    grp = (A_GROUPS, CHUNK, CHUNK)
    body, in_specs, args = _add_deps(
        body, [_row_spec(tm, D_MODEL), _row_spec(tm, D_MODEL), _row_spec(tm, 2 * D_MODEL),
               _const_spec((1, D_MODEL)), _const_spec((1, D_MODEL)), _const_spec(grp), _const_spec(grp),
               _const_spec((CHUNK, CHUNK)), ANY, ANY],
        [dh, x, zp, nmix, gv, wsm, bsb, tril, w_in, w_out], deps)
    return pl.pallas_call(
        body, name="mixer_a_bwd", grid=(nt,), in_specs=in_specs,
        out_specs=[_row_spec(tm, D_MODEL), _const_spec((N_SHARD, D_MODEL, nw)), _const_spec((D_MODEL, D_MODEL)),
                   _const_spec(grp), _const_spec(grp), _const_spec((1, D_MODEL)), _const_spec((1, D_MODEL))],
        out_shape=[jax.ShapeDtypeStruct((T, D_MODEL), F32), jax.ShapeDtypeStruct((N_SHARD, D_MODEL, nw), F32),
                   jax.ShapeDtypeStruct((D_MODEL, D_MODEL), F32),
                   jax.ShapeDtypeStruct(grp, F32), jax.ShapeDtypeStruct(grp, F32),
                   jax.ShapeDtypeStruct((1, D_MODEL), F32), jax.ShapeDtypeStruct((1, D_MODEL), F32)],
        scratch_shapes=[pltpu.VMEM((N_SHARD, D_MODEL, nw), BF16), pltpu.VMEM((D_MODEL, D_MODEL), BF16),
                        pltpu.VMEM((tm, D_MODEL), F32), pltpu.VMEM((tm, D_MODEL), F32),
                        pltpu.VMEM(grp, F32), pltpu.VMEM((tm, D_MODEL), BF16), pltpu.SemaphoreType.DMA((2,))],
        compiler_params=_params(),
    )(*args)


def _load_ffn_weights(w_up_hbm, w_dn_hbm, layer, w_up_v, w_dn_v, sem):
    _load_once([(w_up_hbm, w_up_v), (w_dn_hbm, w_dn_v)], sem)


def _ffn_fwd(h, nffn, cw, cb, w_up, w_dn, layer):
    T = h.shape[0]
    tm = min(256, T)
    nt = T // tm

    def body(h_ref, n_ref, cw_ref, cb_ref, w_up_hbm, w_dn_hbm, out_ref, hh_ref, c_ref,
             w_up_v, w_dn_v, carry_v, sem):
        _load_ffn_weights(w_up_hbm, w_dn_hbm, layer, w_up_v, w_dn_v, sem)
        _zero_first([carry_v])
        xv = h_ref[...]
        xf = _rms(xv, n_ref[...])[0].astype(BF16)
        acc = xv
        for j in range(2):
            cs = []
            for blk in (j, j + 2):
                cols = slice(blk * FF_BLK, (blk + 1) * FF_BLK)
                hh = _dot(xf, w_up_v[blk])
                hh_ref[:, cols] = hh.astype(BF16)
                ext = jnp.concatenate([carry_v[blk], hh], axis=0)
                carry_v[blk] = hh[tm - 8:, :]
                s1 = pltpu.roll(ext, 1, 0)[8:]
                s2 = pltpu.roll(ext, 2, 0)[8:]
                cv = (cb_ref[:, cols] + cw_ref[0:1, cols] * s2 + cw_ref[1:2, cols] * s1
                      + cw_ref[2:3, cols] * hh)
                c_ref[:, cols] = cv.astype(BF16)
                cs.append(cv)
            act = (cs[0] * _sigmoid(cs[0]) * cs[1]).astype(BF16)
            acc = acc + _dot(act, w_dn_v[j * FF_BLK:(j + 1) * FF_BLK, :])
        out_ref[...] = acc

    return pl.pallas_call(
        body, name=f"ffn_fwd{layer}", grid=(nt,),
        in_specs=[_row_spec(tm, D_MODEL), _const_spec((1, D_MODEL)), _const_spec((3, N_FF)),
                  _const_spec((1, N_FF)), ANY, ANY],
        out_specs=[_row_spec(tm, D_MODEL), _row_spec(tm, N_FF), _row_spec(tm, N_FF)],
        out_shape=[jax.ShapeDtypeStruct((T, D_MODEL), F32), jax.ShapeDtypeStruct((T, N_FF), BF16),
                   jax.ShapeDtypeStruct((T, N_FF), BF16)],
        scratch_shapes=[pltpu.VMEM((N_SHARD, D_MODEL, FF_BLK), BF16), pltpu.VMEM((D_FF, D_MODEL), BF16),
                        pltpu.VMEM((N_SHARD, 8, FF_BLK), F32), pltpu.SemaphoreType.DMA((2 * N_SHARD,))],
        compiler_params=_params(),
    )(h, nffn, cw, cb, w_up, w_dn)


def _wgrad(a, b, bn, col_sharded, name, deps=()):
    T, K = a.shape
    N = b.shape[1]
    tt = min(2048, T)
    nn, ntt = N // bn, T // tt
    kr = K // N_SHARD

    def body(a_ref, b_ref, o_ref):
        @pl.when(pl.program_id(1) == 0)
        def _():
            o_ref[...] = jnp.zeros(o_ref.shape, F32)
        d = _dot_tn(a_ref[...].astype(BF16), b_ref[...].astype(BF16))
        if col_sharded:
            o_ref[...] += d
        else:
            for j in range(N_SHARD):
                o_ref[j] += d[j * kr:(j + 1) * kr]

    if col_sharded:
        assert nn == N_SHARD
        out_spec = pl.BlockSpec((None, K, bn), lambda n, t: (n, 0, 0))
        out_shape = jax.ShapeDtypeStruct((N_SHARD, K, bn), F32)
    else:
        out_spec = pl.BlockSpec((N_SHARD, kr, bn), lambda n, t: (0, 0, n))
        out_shape = jax.ShapeDtypeStruct((N_SHARD, kr, N), F32)
    body, in_specs, args = _add_deps(
        body, [pl.BlockSpec((tt, K), lambda n, t: (t, 0)), pl.BlockSpec((tt, bn), lambda n, t: (t, n))],
        [a, b], deps)
    return pl.pallas_call(
        body, name=name, grid=(nn, ntt), in_specs=in_specs, out_specs=out_spec, out_shape=out_shape,
        compiler_params=pltpu.CompilerParams(dimension_semantics=("arbitrary",) * 2, vmem_limit_bytes=VMEM_LIMIT),
    )(*args)


def _ffn_bwd(dh, h, hh, c, nffn, cw, w_up, w_dn, layer, deps=(), between=None):
    T = h.shape[0]
    tm = min(256, T)
    nt = T // tm

    def body(dh_ref, h_ref, hh_ref, c_ref, n_ref, cw_ref, w_up_hbm, w_dn_hbm,
             dhin_ref, act_ref, dhh_ref, xf_ref, dcw_ref, dcb_ref, dn_ref,
             w_up_v, w_dn_v, carry_v, sem):
        _load_ffn_weights(w_up_hbm, w_dn_hbm, layer, w_up_v, w_dn_v, sem)
        _zero_first([carry_v, dcw_ref, dcb_ref, dn_ref])
        dout = dh_ref[...]
        doutb = dout.astype(BF16)
        xf_f, xh, r = _rms(h_ref[...], n_ref[...])
        xf_ref[...] = xf_f.astype(BF16)
        dxf = jnp.zeros((tm, D_MODEL), F32)
        for j in range(2):
            blks = (j, j + 2)
            cg = c_ref[:, j * FF_BLK:(j + 1) * FF_BLK].astype(F32)
            cu = c_ref[:, (j + 2) * FF_BLK:(j + 3) * FF_BLK].astype(F32)
            sg = _sigmoid(cg)
            sil = cg * sg
            act_ref[:, j * FF_BLK:(j + 1) * FF_BLK] = (sil * cu).astype(BF16)
            dact = _dot_nt(doutb, w_dn_v[j * FF_BLK:(j + 1) * FF_BLK, :])
            dcs = (dact * cu * (sg * (1.0 + cg * (1.0 - sg))), dact * sil)
            for blk, dc in zip(blks, dcs):
                cols = slice(blk * FF_BLK, (blk + 1) * FF_BLK)
                hhv = hh_ref[:, cols].astype(F32)
                ext = jnp.concatenate([dc, carry_v[blk]], axis=0)
                carry_v[blk] = dc[:8, :]
                n = tm + 8
                a1 = pltpu.roll(ext, n - 1, 0)[:tm]
                a2 = pltpu.roll(ext, n - 2, 0)[:tm]
                dcb_ref[:, cols] += jnp.sum(dc, axis=0, keepdims=True)
                dcw_ref[0:1, cols] += jnp.sum(a2 * hhv, axis=0, keepdims=True)
                dcw_ref[1:2, cols] += jnp.sum(a1 * hhv, axis=0, keepdims=True)
                dcw_ref[2:3, cols] += jnp.sum(dc * hhv, axis=0, keepdims=True)
                dhh = (cw_ref[2:3, cols] * dc + cw_ref[1:2, cols] * a1 + cw_ref[0:1, cols] * a2).astype(BF16)
                dhh_ref[:, cols] = dhh
                dxf = dxf + _dot_nt(dhh, w_up_v[blk])
        dxx, dn = _rms_bwd(dxf, xh, r, n_ref[...])
        dn_ref[...] += dn
        dhin_ref[...] = dout + dxx

    rev = functools.partial(_row_spec, rev_nt=nt)
    body, in_specs, args = _add_deps(
        body, [rev(tm, D_MODEL), rev(tm, D_MODEL), rev(tm, N_FF), rev(tm, N_FF),
               _const_spec((1, D_MODEL)), _const_spec((3, N_FF)), ANY, ANY],
        [dh, h, hh, c, nffn, cw, w_up, w_dn], deps)
    dhin, act, dhh, xf, dcw, dcb, dn = pl.pallas_call(
        body, name=f"ffn_bwd{layer}", grid=(nt,), in_specs=in_specs,
        out_specs=[rev(tm, D_MODEL), rev(tm, D_FF), rev(tm, N_FF), rev(tm, D_MODEL),
                   _const_spec((3, N_FF)), _const_spec((1, N_FF)), _const_spec((1, D_MODEL))],
        out_shape=[jax.ShapeDtypeStruct((T, D_MODEL), F32), jax.ShapeDtypeStruct((T, D_FF), BF16),
                   jax.ShapeDtypeStruct((T, N_FF), BF16), jax.ShapeDtypeStruct((T, D_MODEL), BF16),
                   jax.ShapeDtypeStruct((3, N_FF), F32), jax.ShapeDtypeStruct((1, N_FF), F32),
                   jax.ShapeDtypeStruct((1, D_MODEL), F32)],
        scratch_shapes=[pltpu.VMEM((N_SHARD, D_MODEL, FF_BLK), BF16), pltpu.VMEM((D_FF, D_MODEL), BF16),
                        pltpu.VMEM((N_SHARD, 8, FF_BLK), F32), pltpu.SemaphoreType.DMA((2 * N_SHARD,))],
        compiler_params=_params(),
    )(*args)
    deps2 = between(dhin) if between is not None else ()
    dwdn = _wgrad(act, dh, D_MODEL // 2, False, f"wgrad_ffn_down{layer}", deps=deps2)
    dwup = _wgrad(xf, dhh, FF_BLK, True, f"wgrad_ffn_up{layer}", deps=deps2)
    return dhin, dwup, dwdn, dcw, dcb, dn


def _load_ple_weights(w_pin_hbm, w_gate_hbm, layer, w_pin_v, w_gate_v, sem, extra=()):
    _load_once([(w_pin_hbm, w_pin_v), (w_gate_hbm, w_gate_v)] + list(extra), sem)


def _p_spec(tm, layer):
    return pl.BlockSpec((None, tm, PLE_DIM), lambda i: (layer, i, 0))


def _ple_terms(xv, p_ref, n_ref, bg_ref, w_pin_v, w_gate_v, pe_v):
    pw = D_MODEL // N_SHARD
    xg, xh, r = _rms(xv, n_ref[...])
    xgb = xg.astype(BF16)
    gate = _sigmoid(_dot(xgb, w_gate_v[...]) + bg_ref[...])
    pb = p_ref[...].astype(BF16)
    for j in range(N_SHARD):
        pe_v[:, j * pw:(j + 1) * pw] = _dot(pb, w_pin_v[j])
    pe = pe_v[...]
    return pe * gate, pe, gate, xgb, xh, r


def _ple_fwd_kv(h, p, nple, bg, nkv, w_pin, w_gate, w_kv):
    T = h.shape[0]
    tm = min(512, T)
    nt = T // tm
    pw = D_MODEL // N_SHARD

    def body(h_ref, p_ref, n_ref, bg_ref, nkv_ref, w_pin_hbm, w_gate_hbm, w_kv_hbm,
             out_ref, kv_ref, w_pin_v, w_gate_v, w_kv_v, pe_v, sem):
        _load_ple_weights(w_pin_hbm, w_gate_hbm, 0, w_pin_v, w_gate_v, sem, [(w_kv_hbm, w_kv_v)])
        xv = h_ref[...]
        hn = xv + _ple_terms(xv, p_ref, n_ref, bg_ref, w_pin_v, w_gate_v, pe_v)[0]
        out_ref[...] = hn
        kvn = _rms(hn, nkv_ref[...])[0].astype(BF16)
        kv_ref[...] = _dot(kvn, w_kv_v[...]).astype(BF16)

    vec = _const_spec((1, D_MODEL))
    return pl.pallas_call(
        body, name="ple_fwd0", grid=(nt,),
        in_specs=[_row_spec(tm, D_MODEL), _p_spec(tm, 0), vec, vec, vec, ANY, ANY, ANY],
        out_specs=[_row_spec(tm, D_MODEL), _row_spec(tm, 2 * KV_DIM)],
        out_shape=[jax.ShapeDtypeStruct((T, D_MODEL), F32), jax.ShapeDtypeStruct((T, 2 * KV_DIM), BF16)],
        scratch_shapes=[pltpu.VMEM((N_SHARD, PLE_DIM, pw), BF16), pltpu.VMEM((D_MODEL, D_MODEL), BF16),
                        pltpu.VMEM((D_MODEL, 2 * KV_DIM), BF16), pltpu.VMEM((tm, D_MODEL), F32),
                        pltpu.SemaphoreType.DMA((2 * N_SHARD + 1,))],
        compiler_params=_params(),
    )(h, p, nple, bg, nkv, w_pin, w_gate, w_kv)


def _ple_fwd_final(h, p, tgt, nple, bg, nfin, w_pin, w_gate):
    T = h.shape[0]
    tm = min(512, T)
    nt = T // tm
    pw = D_MODEL // N_SHARD

    def body(h_ref, p_ref, t_ref, n_ref, bg_ref, nf_ref, w_pin_hbm, w_gate_hbm,
             dh_ref, loss_ref, dnf_ref, w_pin_v, w_gate_v, pe_v, sem):
        _load_ple_weights(w_pin_hbm, w_gate_hbm, 1, w_pin_v, w_gate_v, sem)
        _zero_first([loss_ref, dnf_ref])
        xv = h_ref[...]
        hn = xv + _ple_terms(xv, p_ref, n_ref, bg_ref, w_pin_v, w_gate_v, pe_v)[0]
        y, yh, r = _rms(hn, nf_ref[...])
        diff = y - t_ref[...]
        loss_ref[...] += 0.5 * jnp.sum(jnp.mean(diff * diff, axis=-1, keepdims=True))
        dy = diff * (1.0 / D_MODEL)
        dhn, dnf = _rms_bwd(dy, yh, r, nf_ref[...])
        dnf_ref[...] += dnf
        dh_ref[...] = dhn

    vec = _const_spec((1, D_MODEL))
    return pl.pallas_call(
        body, name="ple_fwd1", grid=(nt,),
        in_specs=[_row_spec(tm, D_MODEL), _p_spec(tm, 1), _row_spec(tm, D_MODEL), vec, vec, vec, ANY, ANY],
        out_specs=[_row_spec(tm, D_MODEL), _const_spec((8, 128)), vec],
        out_shape=[jax.ShapeDtypeStruct((T, D_MODEL), F32), jax.ShapeDtypeStruct((8, 128), F32),
                   jax.ShapeDtypeStruct((1, D_MODEL), F32)],
        scratch_shapes=[pltpu.VMEM((N_SHARD, PLE_DIM, pw), BF16), pltpu.VMEM((D_MODEL, D_MODEL), BF16),
                        pltpu.VMEM((tm, D_MODEL), F32), pltpu.SemaphoreType.DMA((2 * N_SHARD,))],
        compiler_params=_params(),
    )(h, p, tgt, nple, bg, nfin, w_pin, w_gate)


def _ple_bwd(dh, hb, p, nple, bg, w_pin, w_gate, layer, kv_args=None):
    T = hb.shape[0]
    tm = min(512, T)
    nt = T // tm
    with_kv = kv_args is not None
    pw = D_MODEL // N_SHARD

    def body(*refs):
        if with_kv:
            (dh_ref, hb_ref, p_ref, n_ref, bg_ref, w_pin_hbm, w_gate_hbm, hc_ref, dkv_ref, nkv_ref, w_kv_hbm,
             dhb_ref, dwpin_ref, dwgate_ref, dbg_ref, dn_ref, dwkv_ref, dnkv_ref,
             w_pin_v, w_gate_v, pe_v, w_kv_v, sem) = refs
        else:
            (dh_ref, hb_ref, p_ref, n_ref, bg_ref, w_pin_hbm, w_gate_hbm,
             dhb_ref, dwpin_ref, dwgate_ref, dbg_ref, dn_ref, w_pin_v, w_gate_v, pe_v, sem) = refs
        pairs = [(w_pin_hbm, w_pin_v), (w_gate_hbm, w_gate_v)]
        if with_kv:
            pairs.append((w_kv_hbm, w_kv_v))
        _load_once(pairs, sem)
        _zero_first([dwpin_ref, dwgate_ref, dbg_ref, dn_ref] + ([dwkv_ref, dnkv_ref] if with_kv else []))
        do = dh_ref[...]
        if with_kv:
            dkvb = dkv_ref[...].astype(BF16)
            dkvn = _dot_nt(dkvb, w_kv_v[...])
            kvn, kh, kr = _rms(hc_ref[...], nkv_ref[...])
            dwkv_ref[...] += _dot_tn(kvn.astype(BF16), dkvb)
            dk, dnkv = _rms_bwd(dkvn, kh, kr, nkv_ref[...])
            dnkv_ref[...] += dnkv
            do = do + dk
        _, pe, gate, xgb, xh, r = _ple_terms(hb_ref[...], p_ref, n_ref, bg_ref, w_pin_v, w_gate_v, pe_v)
        dpe = (do * gate).astype(BF16)
        pb = p_ref[...].astype(BF16)
        for j in range(N_SHARD):
            dwpin_ref[j] += _dot_tn(pb, dpe[:, j * pw:(j + 1) * pw])
        da = do * pe * (gate * (1.0 - gate))
        dab = da.astype(BF16)
        dbg_ref[...] += jnp.sum(da, axis=0, keepdims=True)
        dxg = _dot_nt(dab, w_gate_v[...])
        dwgate_ref[...] += _dot_tn(xgb, dab)
        dxx, dn = _rms_bwd(dxg, xh, r, n_ref[...])
        dn_ref[...] += dn
        dhb_ref[...] = do + dxx

    vec = _const_spec((1, D_MODEL))
    row = _row_spec(tm, D_MODEL)
    in_specs = [row, row, _p_spec(tm, layer), vec, vec, ANY, ANY]
    args = [dh, hb, p, nple, bg, w_pin, w_gate]
    out_specs = [row, _const_spec((N_SHARD, PLE_DIM, pw)), _const_spec((D_MODEL, D_MODEL)), vec, vec]
    out_shape = [jax.ShapeDtypeStruct((T, D_MODEL), F32), jax.ShapeDtypeStruct((N_SHARD, PLE_DIM, pw), F32),
                 jax.ShapeDtypeStruct((D_MODEL, D_MODEL), F32),
                 jax.ShapeDtypeStruct((1, D_MODEL), F32), jax.ShapeDtypeStruct((1, D_MODEL), F32)]
    scratch = [pltpu.VMEM((N_SHARD, PLE_DIM, pw), BF16), pltpu.VMEM((D_MODEL, D_MODEL), BF16),
               pltpu.VMEM((tm, D_MODEL), F32)]
    if with_kv:
        hc, dkv, nkv, w_kv = kv_args
        in_specs += [row, _row_spec(tm, 2 * KV_DIM), vec, ANY]
        args += [hc, dkv, nkv, w_kv]
        out_specs += [_const_spec((D_MODEL, 2 * KV_DIM)), vec]
        out_shape += [jax.ShapeDtypeStruct((D_MODEL, 2 * KV_DIM), F32), jax.ShapeDtypeStruct((1, D_MODEL), F32)]
        scratch.append(pltpu.VMEM((D_MODEL, 2 * KV_DIM), BF16))
    scratch.append(pltpu.SemaphoreType.DMA((3,)))
    return pl.pallas_call(
        body, name=f"ple_bwd{layer}", grid=(nt,), in_specs=in_specs, out_specs=out_specs,
        out_shape=out_shape, scratch_shapes=scratch, compiler_params=_params(),
    )(*args)


GROUP_ROWS = GQA_GROUP * BLOCK


def _stack_heads(x, kh):
    return jnp.concatenate([x[:, (kh * GQA_GROUP + g) * HEAD_DIM:(kh * GQA_GROUP + g + 1) * HEAD_DIM]
                            for g in range(GQA_GROUP)], axis=0)


def _attn_fwd(h, nmix, kv, sinks, w_q, w_o):
    T = h.shape[0]
    tm = min(512, T)
    nt = T // tm
    nb = tm // BLOCK

    def body(h_ref, n_ref, kv_ref, kvp_ref, sink_ref, w_q_hbm, w_o_hbm,
             out_ref, q_ref, ao_ref, p_ref, psink_ref, w_q_v, w_o_v, kvs_v, sem):
        _load_once([(w_q_hbm, w_q_v), (w_o_hbm, w_o_v)], sem)
        ti = pl.program_id(0)
        xv = h_ref[...]
        xn = _rms(xv, n_ref[...])[0].astype(BF16)
        q_ref[...] = (_dot(xn, w_q_v[...]) * (HEAD_DIM ** -0.5)).astype(BF16)
        kvs_v[0:BLOCK, :] = kvp_ref[...]
        kvs_v[BLOCK:, :] = kv_ref[...]
        lane = lax.broadcasted_iota(jnp.int32, (BLOCK, 128), 1)
        ii = lax.broadcasted_iota(jnp.int32, (BLOCK, 2 * BLOCK), 0)
        jj = lax.broadcasted_iota(jnp.int32, (BLOCK, 2 * BLOCK), 1)
        dist = ii + BLOCK - jj
        inband = (dist >= 0) & (dist < BLOCK)
        distf = dist.astype(F32)

        def blk_body(b, carry):
            r0 = pl.multiple_of(b * BLOCK, BLOCK)
            valid = inband & ((jj >= BLOCK) | jnp.logical_not(jnp.logical_and(ti == 0, b == 0)))
            qb = q_ref[pl.ds(r0, BLOCK), :]
            band = kvs_v[pl.ds(r0, 2 * BLOCK), :]
            psink_mat = jnp.zeros((BLOCK, 128), F32)
            outs = []
            for hq in range(N_Q_HEADS):
                kh, g = divmod(hq, GQA_GROUP)
                k_h = band[:, kh * HEAD_DIM:(kh + 1) * HEAD_DIM]
                v_h = band[:, KV_DIM + kh * HEAD_DIM:KV_DIM + (kh + 1) * HEAD_DIM]
                s = _dot_nt(qb[:, hq * HEAD_DIM:(hq + 1) * HEAD_DIM], k_h) - _SLOPES[hq] * distf
                s = jnp.where(valid, s, NEG)
                sink = sink_ref[hq]
                m = jnp.maximum(jnp.max(s, axis=1, keepdims=True), sink)
                e = jnp.exp(s - m)
                esink = jnp.exp(sink - m)
                inv = 1.0 / (jnp.sum(e, axis=1, keepdims=True) + esink)
                pb = (e * inv).astype(BF16)
                p_ref[b, kh, g * BLOCK:(g + 1) * BLOCK, :] = pb
                outs.append(_dot(pb, v_h))
                psink_mat = jnp.where(lane == hq, esink * inv, psink_mat)
            ao_ref[pl.ds(r0, BLOCK), :] = jnp.concatenate(outs, axis=1).astype(BF16)
            psink_ref[pl.ds(r0, BLOCK), :] = psink_mat
            return carry

        lax.fori_loop(0, nb, blk_body, 0)
        out_ref[...] = xv + _dot(ao_ref[...], w_o_v[...])

    row = _row_spec(tm, D_MODEL)
    prev_spec = pl.BlockSpec((BLOCK, 2 * KV_DIM), lambda i: (jnp.maximum(i * nb - 1, 0), 0))
    return pl.pallas_call(
        body, name="attn_fwd", grid=(nt,),
        in_specs=[row, _const_spec((1, D_MODEL)), _row_spec(tm, 2 * KV_DIM), prev_spec, SMEM, ANY, ANY],
        out_specs=[row, row, row, pl.BlockSpec((nb, N_KV_HEADS, GROUP_ROWS, 2 * BLOCK), lambda i: (i, 0, 0, 0)),
                   _row_spec(tm, 128)],
        out_shape=[jax.ShapeDtypeStruct((T, D_MODEL), F32), jax.ShapeDtypeStruct((T, D_MODEL), BF16),
                   jax.ShapeDtypeStruct((T, D_MODEL), BF16),
                   jax.ShapeDtypeStruct((T // BLOCK, N_KV_HEADS, GROUP_ROWS, 2 * BLOCK), BF16),
                   jax.ShapeDtypeStruct((T, 128), F32)],
        scratch_shapes=[pltpu.VMEM((D_MODEL, D_MODEL), BF16), pltpu.VMEM((D_MODEL, D_MODEL), BF16),
                        pltpu.VMEM((tm + BLOCK, 2 * KV_DIM), BF16), pltpu.SemaphoreType.DMA((2,))],
        compiler_params=_params(),
    )(h, nmix, kv, kv, sinks, w_q, w_o)


def _attn_bwd(dh, h, q, kv, ao, p, psink, nmix, w_q, w_o):
    T = h.shape[0]
    tm = min(512, T)
    nt = T // tm
    nb = tm // BLOCK

    def body(dh_ref, h_ref, q_ref, kv_ref, kvp_ref, ao_ref, p_ref, psink_ref, n_ref, w_q_hbm, w_o_hbm,
             dhin_ref, dwq_ref, dwo_ref, dkv_ref, dsink_ref, dn_ref,
             w_q_v, w_o_v, kvs_v, dao_v, dq_v, dkv_v, carry_v, sem):
        _load_once([(w_q_hbm, w_q_v), (w_o_hbm, w_o_v)], sem)
        _zero_first([carry_v, dsink_ref, dn_ref, dwq_ref, dwo_ref])
        dout = dh_ref[...]
        doutb = dout.astype(BF16)
        dao_v[...] = _dot_nt(doutb, w_o_v[...])
        dwo_ref[...] += _dot_tn(ao_ref[...], doutb)
        kvs_v[0:BLOCK, :] = kvp_ref[...]
        kvs_v[BLOCK:, :] = kv_ref[...]
        dkv_v[0:tm, :] = jnp.zeros((tm, 2 * KV_DIM), F32)
        dkv_v[tm:, :] = carry_v[...]
        seg = (lax.broadcasted_iota(jnp.int32, (D_MODEL, 128), 0) // HEAD_DIM
               == lax.broadcasted_iota(jnp.int32, (D_MODEL, 128), 1)).astype(BF16)

        def blk_body(b, dsk):
            r0 = pl.multiple_of(b * BLOCK, BLOCK)
            qb = q_ref[pl.ds(r0, BLOCK), :]
            band = kvs_v[pl.ds(r0, 2 * BLOCK), :]
            aob = ao_ref[pl.ds(r0, BLOCK), :].astype(F32)
            daob = dao_v[pl.ds(r0, BLOCK), :]
            prod = daob * aob
            head = prod.astype(BF16)
            tail = (prod - head.astype(F32)).astype(BF16)
            dsk = dsk + psink_ref[pl.ds(r0, BLOCK), :] * (_dot(head, seg) + _dot(tail, seg))
            dqs = []
            dks = []
            dvs = []
            for kh in range(N_KV_HEADS):
                k_h = band[:, kh * HEAD_DIM:(kh + 1) * HEAD_DIM]
                v_h = band[:, KV_DIM + kh * HEAD_DIM:KV_DIM + (kh + 1) * HEAD_DIM]
                q_g = _stack_heads(qb, kh)
                dao_g = _stack_heads(daob, kh)
                prb = p_ref[b, kh]
                pr = prb.astype(F32)
                dd = jnp.sum(dao_g * _stack_heads(aob, kh), axis=1, keepdims=True)
                dao_gb = dao_g.astype(BF16)
                dp = _dot_nt(dao_gb, v_h)
                dsb = (pr * (dp - dd)).astype(BF16)
                dq_g = _dot(dsb, k_h) * (HEAD_DIM ** -0.5)
                dks.append(_dot_tn(dsb, q_g))
                dvs.append(_dot_tn(prb, dao_gb))
                for g in range(GQA_GROUP):
                    dqs.append(dq_g[g * BLOCK:(g + 1) * BLOCK])
            dq_v[pl.ds(r0, BLOCK), :] = jnp.concatenate(dqs, axis=1)
            dkv_v[pl.ds(r0, 2 * BLOCK), :] += jnp.concatenate(dks + dvs, axis=1)
            return dsk

        dsk = lax.fori_loop(0, nb, blk_body, jnp.zeros((BLOCK, 128), F32))
        dsink_ref[...] -= jnp.sum(dsk, axis=0, keepdims=True)
        dqb = dq_v[...].astype(BF16)
        dxn = _dot_nt(dqb, w_q_v[...])
        xn, xh, r = _rms(h_ref[...], n_ref[...])
        dwq_ref[...] += _dot_tn(xn.astype(BF16), dqb)
        dxx, dn = _rms_bwd(dxn, xh, r, n_ref[...])
        dn_ref[...] += dn
        dhin_ref[...] = dout + dxx
        dkv_ref[...] = dkv_v[BLOCK:, :]
        carry_v[...] = dkv_v[0:BLOCK, :]

    rev = functools.partial(_row_spec, rev_nt=nt)
    row = rev(tm, D_MODEL)
    prev_spec = pl.BlockSpec((BLOCK, 2 * KV_DIM), lambda i: (jnp.maximum((nt - 1 - i) * nb - 1, 0), 0))
    return pl.pallas_call(
        body, name="attn_bwd", grid=(nt,),
        in_specs=[row, row, row, rev(tm, 2 * KV_DIM), prev_spec, row,
                  pl.BlockSpec((nb, N_KV_HEADS, GROUP_ROWS, 2 * BLOCK), lambda i: (nt - 1 - i, 0, 0, 0)),
                  rev(tm, 128), _const_spec((1, D_MODEL)), ANY, ANY],
        out_specs=[row, _const_spec((D_MODEL, D_MODEL)), _const_spec((D_MODEL, D_MODEL)), rev(tm, 2 * KV_DIM),
                   _const_spec((8, 128)), _const_spec((1, D_MODEL))],
        out_shape=[jax.ShapeDtypeStruct((T, D_MODEL), F32), jax.ShapeDtypeStruct((D_MODEL, D_MODEL), F32),
                   jax.ShapeDtypeStruct((D_MODEL, D_MODEL), F32), jax.ShapeDtypeStruct((T, 2 * KV_DIM), F32),
                   jax.ShapeDtypeStruct((8, 128), F32), jax.ShapeDtypeStruct((1, D_MODEL), F32)],
        scratch_shapes=[pltpu.VMEM((D_MODEL, D_MODEL), BF16), pltpu.VMEM((D_MODEL, D_MODEL), BF16),
                        pltpu.VMEM((tm + BLOCK, 2 * KV_DIM), BF16), pltpu.VMEM((tm, D_MODEL), F32),
                        pltpu.VMEM((tm, D_MODEL), F32), pltpu.VMEM((tm + BLOCK, 2 * KV_DIM), F32),
                        pltpu.VMEM((BLOCK, 2 * KV_DIM), F32), pltpu.SemaphoreType.DMA((2,))],
        compiler_params=_params(),
    )(dh, h, q, kv, kv, ao, p, psink, nmix, w_q, w_o)


def _mesh_pos():
    return lax.axis_index("x"), lax.axis_index("y"), lax.axis_index("c")


def _other_chips(x, y):
    return [(1 - x, y), (x, 1 - y), (1 - x, 1 - y)]


HBM_SPEC = pl.BlockSpec(memory_space=pltpu.HBM)
SEM_SPEC = pl.BlockSpec(memory_space=pltpu.SEMAPHORE)


def _split_call(name, bufs, waits=(), starts=(), after=()):
    n, nw, ns, na = len(bufs), len(waits), len(starts), len(after)

    def body(*refs):
        brefs = refs[:n]
        wsems = [(refs[n + 2 * k], refs[n + 2 * k + 1]) for k in range(nw)]
        o = n + 2 * nw + na
        ssems = [(refs[o + 2 * k], refs[o + 2 * k + 1]) for k in range(ns)]
        for (ss, rs), (_, _, fn) in zip(wsems, waits):
            for sending, arriving in fn(brefs, ss, rs):
                sending.wait_send()
                arriving.wait_recv()
        for (ss, rs), (_, fn) in zip(ssems, starts):
            for sending, _ in fn(brefs, ss, rs):
                sending.start()
        if ns:
            token = refs[o + 2 * ns + n]
            token[...] = jnp.zeros(token.shape, token.dtype)

    out_shape, out_specs = [], []
    for cnt, _ in starts:
        out_shape += [pltpu.SemaphoreType.DMA((cnt,)), pltpu.SemaphoreType.DMA((cnt,))]
        out_specs += [SEM_SPEC, SEM_SPEC]
    out_shape += [pltpu.HBM(b.shape, b.dtype) for b in bufs]
    out_specs += [HBM_SPEC] * n
    if ns:
        out_shape.append(jax.ShapeDtypeStruct((8, 128), F32))
        out_specs.append(pl.BlockSpec(memory_space=pltpu.VMEM))
    args = [pltpu.with_memory_space_constraint(b, pltpu.HBM) for b in bufs]
    for ss, rs, _ in waits:
        args += [ss, rs]
    args += list(after)
    res = pl.pallas_call(
        body, name=name, out_shape=tuple(out_shape),
        in_specs=[HBM_SPEC] * n + [SEM_SPEC] * (2 * nw) + [ANY] * na, out_specs=tuple(out_specs),
        input_output_aliases={i: 2 * ns + i for i in range(n)},
        compiler_params=pltpu.CompilerParams(has_side_effects=pltpu.SideEffectType.DATAFLOW_SIDE_EFFECTING),
    )(*args)
    sems = [(res[2 * k], res[2 * k + 1]) for k in range(ns)]
    return list(res[2 * ns:2 * ns + n]), sems, (res[2 * ns + n] if ns else None)


def _cast_place(items, name, deps=()):
    n = len(items)
    mats = [a.shape[-2:] for a, _, _ in items]

    def body(*refs):
        ins, outs, scr, sem = refs[:n], refs[n:2 * n], refs[2 * n:3 * n], refs[3 * n]
        x, y, _ = _mesh_pos()
        cps = []
        for t in range(n):
            scr[t][...] = ins[t][...].astype(scr[t].dtype)
            cp = pltpu.make_async_copy(scr[t], outs[t].at[2 * x + y], sem.at[t])
            cp.start()
            cps.append(cp)
        for cp in cps:
            cp.wait()

    def spec(idx, shape):
        return pl.BlockSpec((None,) * len(idx) + tuple(shape), lambda i: tuple(idx) + (0, 0))

    body, in_specs, args = _add_deps(body, [spec(idx, mat) for (_, idx, _), mat in zip(items, mats)],
                                     [a for a, _, _ in items], deps)
    return pl.pallas_call(
        body, name=name, grid=(1,), in_specs=in_specs, out_specs=[ANY] * n,
        out_shape=[jax.ShapeDtypeStruct((N_SHARD,) + tuple(mat), dt) for (_, _, dt), mat in zip(items, mats)],
        scratch_shapes=[pltpu.VMEM(tuple(mat), dt) for (_, _, dt), mat in zip(items, mats)]
        + [pltpu.SemaphoreType.DMA((n,))],
        compiler_params=_params(),
    )(*args)


def _gather_ici(idx):
    def fn(bufs, ss, rs):
        x, y, c = _mesh_pos()
        pairs = []
        for k, t in enumerate(idx):
            half = bufs[t].shape[1] // 2
            mine = bufs[t].at[2 * x + y, pl.ds(c * half, half), :]
            for j, (cx, cy) in enumerate(_other_chips(x, y)):
                theirs = bufs[t].at[2 * cx + cy, pl.ds(c * half, half), :]
                sem = dict(send_sem=ss.at[3 * k + j], recv_sem=rs.at[3 * k + j],
                           device_id=(cx, cy, c), device_id_type=MESH)
                pairs.append((pltpu.make_async_remote_copy(src_ref=mine, dst_ref=mine, **sem),
                              pltpu.make_async_remote_copy(src_ref=mine, dst_ref=theirs, **sem)))
        return pairs
    return fn


def _gather_d2d(idx):
    def fn(bufs, ss, rs):
        x, y, c = _mesh_pos()
        pairs = []
        for k, t in enumerate(idx):
            half = bufs[t].shape[1] // 2
            for j, (cx, cy) in enumerate(_other_chips(x, y)):
                got = bufs[t].at[2 * cx + cy, pl.ds(c * half, half), :]
                theirs = bufs[t].at[2 * cx + cy, pl.ds((1 - c) * half, half), :]
                sem = dict(send_sem=ss.at[3 * k + j], recv_sem=rs.at[3 * k + j],
                           device_id=(x, y, 1 - c), device_id_type=MESH)
                pairs.append((pltpu.make_async_remote_copy(src_ref=got, dst_ref=got, **sem),
                              pltpu.make_async_remote_copy(src_ref=got, dst_ref=theirs, **sem)))
        return pairs
    return fn


def _alloc(shapes, name):
    def body(*refs):
        pass

    return pl.pallas_call(body, name=name, out_specs=[ANY] * len(shapes),
                          out_shape=[jax.ShapeDtypeStruct(s, d) for s, d in shapes])()


def _send_to_sibling(n):
    def fn(bufs, ss, rs):
        x, y, c = _mesh_pos()
        pairs = []
        for t in range(n):
            src = bufs[t]
            if len(src.shape) == 3:
                half = src.shape[1] // 2
                src = src.at[:, pl.ds((1 - c) * half, half), :]
            cp = pltpu.make_async_remote_copy(src_ref=src, dst_ref=bufs[n + t], send_sem=ss.at[t],
                                              recv_sem=rs.at[t], device_id=(x, y, 1 - c), device_id_type=MESH)
            pairs.append((cp, cp))
        return pairs
    return fn


def _send_to_chips(n):
    def fn(bufs, ss, rs):
        x, y, c = _mesh_pos()
        pairs = []
        for j, (cx, cy) in enumerate(_other_chips(x, y)):
            for t in range(n):
                src = bufs[t].at[j] if len(bufs[t].shape) == 3 else bufs[t]
                cp = pltpu.make_async_remote_copy(src_ref=src, dst_ref=bufs[n + t].at[j], send_sem=ss.at[3 * t + j],
                                                  recv_sem=rs.at[3 * t + j], device_id=(cx, cy, c),
                                                  device_id_type=MESH)
                pairs.append((cp, cp))
        return pairs
    return fn


class _Exchange:
    def __init__(self, name, srcs, land_shapes, fn, n_sems):
        self.name, self.fn = name, fn
        lands = _alloc(land_shapes, name + "_alloc")
        self.n = len(srcs)
        self.bufs, sems, self.token = _split_call(name + "_start", list(srcs) + list(lands),
                                                  starts=[(n_sems, fn)])
        self.sems = sems[0]

    def finish(self, after=()):
        bufs, _, _ = _split_call(self.name + "_wait", self.bufs, waits=[(*self.sems, self.fn)], after=after)
        return bufs[:self.n], bufs[self.n:]


def _row_block(rows, cols, mult=8, limit=3 * 512 * 1024, itemsize=4):
    best = None
    for br in range(mult, rows + 1, mult):
        if rows % br == 0 and br * cols * itemsize <= limit:
            best = br
    assert best is not None, (rows, cols)
    return best


_GROUP_BLOCK_BYTES = 1024 * 1024


def _group_plan(ss):
    plan = []
    for s in ss:
        half, cols = s.shape[-2:]
        br = _row_block(half, cols, mult=16, limit=_GROUP_BLOCK_BYTES)
        plan.append((br, half // br))
    return plan, max(nr for _, nr in plan)


def _chip_partial(gs, ss, ids, name):
    n = len(gs)
    plan, steps = _group_plan(ss)

    def body(ids_ref, *refs):
        for t in range(n):
            refs[2 * n + t][...] = (refs[t][...] + refs[n + t][...]).astype(BF16)

    g_specs, s_specs, o_specs = [], [], []
    for (br, nr), s in zip(plan, ss):
        blk = (None, br, s.shape[2])
        g_specs.append(pl.BlockSpec(
            blk, lambda j, r, ids_ref, nr=nr: (ids_ref[2 + j], ids_ref[0] * nr + jnp.minimum(r, nr - 1), 0)))
        s_specs.append(pl.BlockSpec(blk, lambda j, r, ids_ref, nr=nr: (ids_ref[2 + j], jnp.minimum(r, nr - 1), 0)))
        o_specs.append(pl.BlockSpec(blk, lambda j, r, ids_ref, nr=nr: (j, jnp.minimum(r, nr - 1), 0)))
    return pl.pallas_call(
        body, name=name,
        grid_spec=pltpu.PrefetchScalarGridSpec(num_scalar_prefetch=1, grid=(3, steps),
                                               in_specs=g_specs + s_specs, out_specs=o_specs),
        out_shape=[jax.ShapeDtypeStruct((3,) + s.shape[1:], BF16) for s in ss],
        compiler_params=pltpu.CompilerParams(dimension_semantics=("arbitrary", "arbitrary"),
                                             vmem_limit_bytes=VMEM_LIMIT),
    )(ids, *gs, *ss)


def _chip_sum(gs, ss, qs, ids, name):
    n = len(gs)
    plan, steps = _group_plan(ss)

    def body(ids_ref, *refs):
        for t in range(n):
            q_ref = refs[2 * n + t]
            own = refs[t][...] + refs[n + t][...]
            refs[3 * n + t][...] = (own + q_ref[2].astype(F32)) + (q_ref[0].astype(F32) + q_ref[1].astype(F32))

    g_specs, s_specs, q_specs, o_specs = [], [], [], []
    for (br, nr), s in zip(plan, ss):
        cols = s.shape[2]
        g_specs.append(pl.BlockSpec(
            (None, br, cols), lambda r, ids_ref, nr=nr: (ids_ref[1], ids_ref[0] * nr + jnp.minimum(r, nr - 1), 0)))
        s_specs.append(pl.BlockSpec((None, br, cols), lambda r, ids_ref, nr=nr: (ids_ref[1], jnp.minimum(r, nr - 1), 0)))
        q_specs.append(pl.BlockSpec((3, br, cols), lambda r, ids_ref, nr=nr: (0, jnp.minimum(r, nr - 1), 0)))
        o_specs.append(pl.BlockSpec((br, cols), lambda r, ids_ref, nr=nr: (jnp.minimum(r, nr - 1), 0)))
    return pl.pallas_call(
        body, name=name,
        grid_spec=pltpu.PrefetchScalarGridSpec(num_scalar_prefetch=1, grid=(steps,),
                                               in_specs=g_specs + s_specs + q_specs, out_specs=o_specs),
        out_shape=[jax.ShapeDtypeStruct(s.shape[1:], F32) for s in ss],
        compiler_params=pltpu.CompilerParams(dimension_semantics=("arbitrary",), vmem_limit_bytes=VMEM_LIMIT),
    )(ids, *gs, *ss, *qs)


def _adamw_math(w, g, m, v):
    mn = ADAM_B1 * m + (1.0 - ADAM_B1) * g
    vn = ADAM_B2 * v + (1.0 - ADAM_B2) * (g * g)
    m_hat = mn / (1.0 - ADAM_B1 ** ADAM_STEP)
    v_hat = vn / (1.0 - ADAM_B2 ** ADAM_STEP)
    return -ADAM_LR * (m_hat / (jnp.sqrt(v_hat) + ADAM_EPS) + ADAM_WD * w), mn, vn


def _adamw_halves(w, own, sib, m, v, ids, name, layer=0, n_layers=1, stacked=None):
    C = w.shape[1]
    R = w.shape[0] // n_layers
    half = R // 2
    br = _row_block(half, C)
    nh = half // br
    base = layer * 2 * nh

    def body(ids_ref, w_ref, own_ref, sib_ref, m_ref, v_ref, *rest):
        g_ref, d_ref, mo_ref, vo_ref = rest[-4:]
        is_own = (pl.program_id(0) // nh) == ids_ref[0]
        g = jnp.where(is_own, own_ref[...], sib_ref[...])
        g_ref[...] = g
        d_ref[...], mo_ref[...], vo_ref[...] = _adamw_math(w_ref[...], g, m_ref[...], v_ref[...])

    full = pl.BlockSpec((br, C), lambda r, ids_ref: (base + r, 0))
    own_spec = pl.BlockSpec((br, C), lambda r, ids_ref: (jnp.clip(r - ids_ref[0] * nh, 0, nh - 1), 0))
    sib_spec = pl.BlockSpec((br, C), lambda r, ids_ref: (jnp.clip(r - (1 - ids_ref[0]) * nh, 0, nh - 1), 0))
    in_specs = [full, own_spec, sib_spec, full, full]
    args = [ids, w, own, sib, m, v]
    aliases = {}
    if stacked is not None:
        in_specs += [ANY] * 4
        args += list(stacked)
        aliases = {6 + k: k for k in range(4)}
    return pl.pallas_call(
        body, name=name,
        grid_spec=pltpu.PrefetchScalarGridSpec(
            num_scalar_prefetch=1, grid=(2 * nh,), in_specs=in_specs, out_specs=[full] * 4),
        out_shape=[jax.ShapeDtypeStruct(w.shape, F32)] * 4, input_output_aliases=aliases,
        compiler_params=_params(),
    )(*args)


_PACK_UNIT = 1024


def _pack(arrs):
    flat = []
    for a in arrs:
        f = a.reshape(-1).astype(F32)
        pad = (-f.shape[0]) % _PACK_UNIT
        if pad:
            f = jnp.concatenate([f, jnp.zeros((pad,), F32)])
        flat.append(f)
    return jnp.concatenate(flat).reshape(-1, 128)


def kernel(x, p, norm_mix, norm_ffn, norm_ple, norm_kv, norm_final, a_w_in, a_norm_v, a_w_s, a_b_s, a_w_out, w_kv, b_w_q, b_sinks, b_w_o, f_w_up, f_conv_w, f_conv_b, f_w_down, ple_w_in, ple_w_gate, ple_b_gate, loss_target, m_norm_mix, m_norm_ffn, m_norm_ple, m_norm_kv, m_norm_final, m_a_w_in, m_a_norm_v, m_a_w_s, m_a_b_s, m_a_w_out, m_w_kv, m_b_w_q, m_b_sinks, m_b_w_o, m_f_w_up, m_f_conv_w, m_f_conv_b, m_f_w_down, m_ple_w_in, m_ple_w_gate, m_ple_b_gate, v_norm_mix, v_norm_ffn, v_norm_ple, v_norm_kv, v_norm_final, v_a_w_in, v_a_norm_v, v_a_w_s, v_a_b_s, v_a_w_out, v_w_kv, v_b_w_q, v_b_sinks, v_b_w_o, v_f_w_up, v_f_conv_w, v_f_conv_b, v_f_w_down, v_ple_w_in, v_ple_w_gate, v_ple_b_gate):
    given = dict(locals())

    small_shard = _pack([a_norm_v, f_conv_w])
    pad_rows = (-small_shard.shape[0]) % 16
    if pad_rows:
        small_shard = jnp.concatenate([small_shard, jnp.zeros((pad_rows, 128), F32)])
    groups = [
        [(a_w_in, (0,), BF16), (a_w_out, (0,), BF16), (small_shard, (), F32)],
        [(f_w_up, (0,), BF16), (f_w_down, (0,), BF16)],
        [(ple_w_in, (0,), BF16), (ple_w_gate, (0,), BF16), (w_kv, (), BF16), (b_w_q, (0,), BF16),
         (b_w_o, (0,), BF16), (f_w_up, (1,), BF16), (f_w_down, (1,), BF16), (ple_w_in, (1,), BF16),
         (ple_w_gate, (1,), BF16)],
    ]
    first = list(range(len(groups[0])))
    lands0, sems0, token0 = _split_call("gather_start_g0", _cast_place(groups[0], "cast_place_g0"),
                                        starts=[(3 * len(first), _gather_ici(first))])
    rest, spans, start = [], [], 0
    for gi, items in enumerate(groups[1:], 1):
        rest += _cast_place(items, f"cast_place_g{gi}", deps=(token0,))
        spans.append(list(range(start, start + len(items))))
        start += len(items)
    rest, rest_sems, rest_token = _split_call("gather_start", rest,
                                              starts=[(3 * len(sp), _gather_ici(sp)) for sp in spans])
    group_bufs = [lands0] + [[rest[t] for t in sp] for sp in spans]
    ici_sems = sems0 + rest_sems

    def finish_group(gi, after):
        bufs = group_bufs[gi]
        local = list(range(len(bufs)))
        bufs, d2d_sems, _ = _split_call(f"gather_pass_g{gi}", bufs, waits=[(*ici_sems[gi], _gather_ici(local))],
                                        starts=[(3 * len(local), _gather_d2d(local))], after=after)
        bufs, _, _ = _split_call(f"gather_done_g{gi}", bufs, waits=[(*d2d_sems[0], _gather_d2d(local))])
        return bufs

    def stage0():
        b_in, b_out, b_small = finish_group(0, (rest_token,))
        small_full = b_small.reshape(N_SHARD, -1)
        gv_full = small_full[:, :256].reshape(1, D_MODEL)
        cw_full = small_full[:, _PACK_UNIT:_PACK_UNIT + 2 * 3 * FF_BLK].reshape(N_SHARD, 2, 3, FF_BLK)
        cw_full = jnp.transpose(cw_full, (1, 2, 0, 3)).reshape(2, 3, N_FF)
        return gv_full, cw_full, b_in, b_out.reshape(D_MODEL, D_MODEL)

    def stage1(after):
        b_up, b_dn = finish_group(1, after)
        return b_up, b_dn.reshape(D_FF, D_MODEL)

    def stage2(after):
        pin0, gate0, kv_w, wq, wo, up1, dn1, pin1, gate1 = finish_group(2, after)
        sq = lambda a: a.reshape(D_MODEL, -1)
        return dict(w_pin=[pin0, pin1], w_gate=[sq(gate0), sq(gate1)], w_kv=sq(kv_w), w_q=sq(wq), w_o=sq(wo),
                    w_up1=up1, w_dn1=dn1.reshape(D_FF, D_MODEL))

    dx, (loss, (out_g, out_d, out_m, out_v)) = _local_step(
        x[0], p.reshape(2, -1, PLE_DIM), loss_target[0], norm_mix, norm_ffn, norm_ple, norm_kv, norm_final, a_w_s, a_b_s,
        b_sinks, f_conv_b, ple_b_gate, stage0, stage1, stage2, _Reducer(given))
    weight_names = ['norm_mix', 'norm_ffn', 'norm_ple', 'norm_kv', 'norm_final', 'a_w_in', 'a_norm_v', 'a_w_s',
                    'a_b_s', 'a_w_out', 'w_kv', 'b_w_q', 'b_sinks', 'b_w_o', 'f_w_up', 'f_conv_w', 'f_conv_b',
                    'f_w_down', 'ple_w_in', 'ple_w_gate', 'ple_b_gate']
    return (loss, dx.reshape(x.shape), *[out_g[k] for k in weight_names], *[out_d[k] for k in weight_names],
            *[out_m[k] for k in weight_names], *[out_v[k] for k in weight_names])


def _local_step(xs, p, tgt, norm_mix, norm_ffn, norm_ple, norm_kv, norm_final, a_w_s, a_b_s, b_sinks,
                f_conv_b, ple_b_gate, stage0, stage1, stage2, sched):
    tril = jnp.tril(jnp.ones((CHUNK, CHUNK), F32))
    wsm = (a_w_s[0] * tril[None]).astype(BF16)
    bsb = jnp.broadcast_to(a_b_s[0][:, :, None], (A_GROUPS, CHUNK, CHUNK))
    sinks = b_sinks[0]
    row = lambda a: a.reshape(1, -1)

    gv_full, cw_full, w_in, w_out = stage0()
    h1, zp = _mixer_a_fwd(xs, row(norm_mix[0]), gv_full, wsm, bsb, w_in, w_out)
    w_up0, w_dn0 = stage1((h1,))
    h2, hh0, c0 = _ffn_fwd(h1, row(norm_ffn[0]), cw_full[0], row(f_conv_b[0]), w_up0, w_dn0, 0)
    rest = stage2((h2,))
    w_pin, w_gate, w_kv_f, w_q, w_o = rest['w_pin'], rest['w_gate'], rest['w_kv'], rest['w_q'], rest['w_o']
    w_up = [w_up0, rest['w_up1']]
    w_dn = [w_dn0, rest['w_dn1']]
    h3, kv = _ple_fwd_kv(h2, p, row(norm_ple[0]), row(ple_b_gate[0]), row(norm_kv), w_pin[0], w_gate[0], w_kv_f)
    h4, q, ao, probs, psink = _attn_fwd(h3, row(norm_mix[1]), kv, sinks, w_q, w_o)
    h5, hh1, c1 = _ffn_fwd(h4, row(norm_ffn[1]), cw_full[1], row(f_conv_b[1]), w_up[1], w_dn[1], 1)
    dh6, loss_acc, dn_final = _ple_fwd_final(
        h5, p, tgt, row(norm_ple[1]), row(ple_b_gate[1]), row(norm_final), w_pin[1], w_gate[1])

    def pieces(g):
        return g.reshape(N_SHARD, -1, g.shape[-1])

    dh5, g_pin1, g_gate1, dbg1, dnple1 = _ple_bwd(dh6, h5, p, row(norm_ple[1]), row(ple_b_gate[1]), w_pin[1], w_gate[1], 1)
    early = {('ple_w_in', 1): g_pin1, ('ple_w_gate', 1): pieces(g_gate1)}
    dh4, g_up1, g_dn1, dcw1, dcb1, dnffn1 = _ffn_bwd(
        dh5, h4, hh1, c1, row(norm_ffn[1]), cw_full[1], w_up[1], w_dn[1], 1)
    early['f_w_down', 1] = pieces(g_dn1)
    early['f_w_up', 1] = g_up1
    dh3a, g_wq, g_wo, dkv, dsink, dnmix1 = _attn_bwd(dh4, h3, q, kv, ao, probs, psink, row(norm_mix[1]), w_q, w_o)
    early['b_w_o', 0] = pieces(g_wo)
    early['b_w_q', 0] = pieces(g_wq)
    dh2, g_pin0, g_gate0, dbg0, dnple0, g_wkv, dnkv = _ple_bwd(
        dh3a, h2, p, row(norm_ple[0]), row(ple_b_gate[0]), w_pin[0], w_gate[0], 0,
        kv_args=(h3, dkv, row(norm_kv), w_kv_f))
    early['w_kv', 0] = pieces(g_wkv)
    early['ple_w_in', 0] = g_pin0
    early['ple_w_gate', 0] = pieces(g_gate0)
    deps = sched.early_ready(early)
    dh1, g_up0, g_dn0, dcw0, dcb0, dnffn0 = _ffn_bwd(
        dh2, h1, hh0, c0, row(norm_ffn[0]), cw_full[0], w_up[0], w_dn[0], 0, deps=deps,
        between=lambda part: sched.after_ffn_half((part,)))
    deps = sched.ffn0_ready({('f_w_down', 0): pieces(g_dn0), ('f_w_up', 0): g_up0})
    dx, g_win, g_wout, dws, dbs, dgv, dnmix0 = _mixer_a_bwd(
        dh1, xs, zp, row(norm_mix[0]), gv_full, wsm, bsb, tril, w_in, w_out, deps=deps)
    g_wout = pieces(g_wout)

    small_grads = {
        'norm_mix': jnp.concatenate([dnmix0, dnmix1]), 'norm_ffn': jnp.concatenate([dnffn0, dnffn1]),
        'norm_ple': jnp.concatenate([dnple0, dnple1]), 'norm_kv': dnkv, 'norm_final': dn_final,
        'a_norm_v': dgv, 'a_w_s': dws.reshape(A_GROUPS * CHUNK, CHUNK), 'a_b_s': dbs[:, :, 0],
        'b_sinks': dsink[0:1, :], 'f_conv_w': jnp.concatenate([dcw0, dcw1]),
        'f_conv_b': jnp.concatenate([dcb0, dcb1]), 'ple_b_gate': jnp.concatenate([dbg0, dbg1]),
        'loss': loss_acc,
    }
    outs = sched.finish({('a_w_in', 0): g_win, ('a_w_out', 0): g_wout}, small_grads, (dx,))
    return dx, outs


class _Reducer:
    def __init__(self, given):
        self.given = given
        cx, cy, cc = _mesh_pos()
        self.shard = 2 * cx + cy
        s = self.shard
        self.ids = jnp.stack([cc, s, s ^ 2, s ^ 1, s ^ 3]).astype(jnp.int32)
        self.out = [{}, {}, {}, {}]
        self.stacked = {}

    def _send(self, tag, grads, small=()):
        keys = list(grads)
        srcs = [grads[k] for k in keys] + list(small)
        shapes = [((N_SHARD, g.shape[1] // 2, g.shape[2]), F32) for g in srcs[:len(keys)]]
        shapes += [(s.shape, F32) for s in small]
        return keys, _Exchange(f"send_{tag}", srcs, shapes, _send_to_sibling(len(srcs)), len(srcs))

    def _exchange(self, tag, keys, send, after):
        srcs, lands = send.finish(after)
        n = len(keys)
        parts = _chip_partial(srcs[:n], lands[:n], self.ids, f"chip_partial_{tag}")
        shapes = [(p.shape, BF16) for p in parts]
        if len(srcs) > n:
            small = _small_add(srcs[n:], lands[n:])
            parts += small
            shapes += [((3,) + s.shape, F32) for s in small]
        exch = _Exchange(f"exch_{tag}", parts, shapes, _send_to_chips(len(parts)), 3 * len(parts))
        return (keys, srcs[:n], lands[:n], exch)

    def _swap(self, tag, state, after):
        keys, grads, sib, exch = state
        parts, recv = exch.finish(after)
        n = len(keys)
        own = _chip_sum(grads, sib, recv[:n], self.ids, f"chip_sum_{tag}")
        small_red = _small_sum(parts[n:], recv[n:]) if len(parts) > n else None
        return keys, _Exchange(f"swap_{tag}", own, [(o.shape, F32) for o in own], _send_to_sibling(n), n), small_red

    def _adamw(self, keys, swap, after):
        own, sib = swap.finish(after)
        last = None
        for (name, layer), o, s in zip(keys, own, sib):
            w = self.given[name]
            n_layers = w.shape[0] if w.ndim == 3 else 1
            c2 = w.shape[-1]
            res = _adamw_halves(w.reshape(-1, c2), o, s, self.given['m_' + name].reshape(-1, c2),
                                self.given['v_' + name].reshape(-1, c2), self.ids, f"adamw_{name}{layer}",
                                layer, n_layers, self.stacked.get(name))
            self.stacked[name] = res
            if layer == 0:
                for dst, r in zip(self.out, res):
                    dst[name] = r.reshape(w.shape)
            last = res[0]
        return last

    def early_ready(self, grads):
        self.e_keys, self.e_send = self._send("e", grads)
        return (self.e_send.token,)

    def after_ffn_half(self, after):
        self.e_state = self._exchange("e", self.e_keys, self.e_send, after)
        return (self.e_state[3].token,)

    def ffn0_ready(self, grads):
        _, self.e_swap, _ = self._swap("e", self.e_state, tuple(grads.values()))
        f_keys, f_send = self._send("f", grads)
        self.f_state = self._exchange("f", f_keys, f_send, ())
        return (self.f_state[3].token, self.e_swap.token)

    def finish(self, grads, small_grads, after):
        small_names = list(small_grads)
        a_keys, a_send = self._send("a", grads, [small_grads[k] for k in small_names])
        a_state = self._exchange("a", a_keys, a_send, after)
        e_done = self._adamw(self.e_keys, self.e_swap, (a_state[3].token,))
        f_keys, f_swap, _ = self._swap("f", self.f_state, (e_done,))
        f_done = self._adamw(f_keys, f_swap, ())
        _, a_swap, small_red = self._swap("a", a_state, (f_done,))
        self._adamw(a_keys, a_swap, ())

        given = self.given
        reduced = dict(zip(small_names, small_red))
        loss = reduced.pop('loss')[0, 0]
        names = list(reduced)
        items = []
        for k in names:
            g = reduced[k]
            cols = g.shape[1] // N_SHARD if k in ('a_norm_v', 'f_conv_w') else g.shape[1]
            view = lambda a: _lane_pad(a.reshape(g.shape[0], -1), cols)
            items.append((view(given[k]), g, view(given['m_' + k]), view(given['v_' + k])))
        res = _adamw_small(items, self.ids)
        for k, four in zip(names, res):
            width = given[k].size // four[0].shape[0]
            for dst, r in zip(self.out, four):
                dst[k] = r[:, :width].reshape(given[k].shape)
        return loss, self.out


def _lane_pad(a, cols):
    return a if a.shape[1] == cols else jnp.pad(a, ((0, 0), (0, cols - a.shape[1])))


def _small_add(a_list, b_list):
    n = len(a_list)

    def body(*refs):
        for t in range(n):
            refs[2 * n + t][...] = refs[t][...] + refs[n + t][...]

    return pl.pallas_call(body, name="chip_partial_small",
                          out_shape=[jax.ShapeDtypeStruct(a.shape, F32) for a in a_list])(*a_list, *b_list)


def _small_sum(parts, recvs):
    n = len(parts)

    def body(*refs):
        for t in range(n):
            q = refs[n + t]
            refs[2 * n + t][...] = (refs[t][...] + q[2]) + (q[0] + q[1])

    return pl.pallas_call(body, name="chip_sum_small",
                          out_shape=[jax.ShapeDtypeStruct(p.shape, F32) for p in parts])(*parts, *recvs)


def _adamw_small(items, ids):
    n = len(items)

    def body(ids_ref, *refs):
        for t in range(n):
            w_ref, g_ref, m_ref, v_ref = refs[4 * t:4 * t + 4]
            g_out, d_ref, mo_ref, vo_ref = refs[4 * n + 4 * t:4 * n + 4 * t + 4]
            g = g_ref[...]
            g_out[...] = g
            d_ref[...], mo_ref[...], vo_ref[...] = _adamw_math(w_ref[...], g, m_ref[...], v_ref[...])

    in_specs, out_specs, out_shape, args = [], [], [], []
    for w, g, m, v in items:
        full = pl.BlockSpec(w.shape, lambda i, ids_ref: (0, 0))
        g_spec = full if g.shape == w.shape else pl.BlockSpec(w.shape, lambda i, ids_ref: (0, ids_ref[1]))
        in_specs += [full, g_spec, full, full]
        out_specs += [full] * 4
        out_shape += [jax.ShapeDtypeStruct(w.shape, F32)] * 4
        args += [w, g, m, v]
    res = pl.pallas_call(
        body, name="adamw_small",
        grid_spec=pltpu.PrefetchScalarGridSpec(num_scalar_prefetch=1, grid=(1,), in_specs=in_specs,
                                               out_specs=out_specs),
        out_shape=out_shape, compiler_params=_params(),
    )(ids, *args)
    return [res[4 * t:4 * t + 4] for t in range(n)]
```

```python
import functools
import math

import numpy as np
import jax
import jax.numpy as jnp
from jax import lax
from jax.experimental import pallas as pl
from jax.experimental.pallas import tpu as pltpu

F32 = jnp.float32
BF16 = jnp.bfloat16

D_MODEL = 1024
CHUNK = 128
A_GROUPS = 8
HEAD_DIM = 64
N_Q_HEADS = 16
N_KV_HEADS = 4
GQA_GROUP = N_Q_HEADS // N_KV_HEADS
KV_DIM = N_KV_HEADS * HEAD_DIM
BLOCK = 128
D_FF = 2816
N_FF = 2 * D_FF
FF_BLK = N_FF // 4
PLE_DIM = 256
EPS = 1e-6
NEG = -1e30
N_SHARD = 4

ADAM_LR = 0.001
ADAM_B1 = 0.9
ADAM_B2 = 0.999
ADAM_EPS = 1e-08
ADAM_WD = 0.01
ADAM_STEP = 10

VMEM_LIMIT = 60 * 1024 * 1024
MESH = pl.DeviceIdType.MESH
ANY = pl.BlockSpec(memory_space=pl.ANY)
SMEM = pl.BlockSpec(memory_space=pltpu.SMEM)

_SLOPES = [float(np.float32(2.0 ** (-8.0 * (h + 1) / N_Q_HEADS))) for h in range(N_Q_HEADS)]


def _dot(a, b):
    return jnp.dot(a, b, preferred_element_type=F32)


def _dot_nt(a, b):
    return lax.dot_general(a, b, (((1,), (1,)), ((), ())), preferred_element_type=F32)


def _dot_tn(a, b):
    return lax.dot_general(a, b, (((0,), (0,)), ((), ())), preferred_element_type=F32)


def _rms(x, g):
    r = lax.rsqrt(jnp.mean(x * x, axis=-1, keepdims=True) + EPS)
    xh = x * r
    return xh * g, xh, r


def _rms_bwd(dy, xh, r, g):
    dxh = dy * g
    dg = jnp.sum(dy * xh, axis=0, keepdims=True)
    dx = r * (dxh - xh * jnp.mean(dxh * xh, axis=-1, keepdims=True))
    return dx, dg


_GELU_C = math.sqrt(2.0 / math.pi)


def _gelu(x):
    t = jnp.tanh(_GELU_C * (x + 0.044715 * (x * x * x)))
    return 0.5 * x * (1.0 + t)


def _gelu_grad(x):
    x2 = x * x
    t = jnp.tanh(_GELU_C * (x + 0.044715 * (x2 * x)))
    return 0.5 * (1.0 + t) + 0.5 * x * (1.0 - t * t) * (_GELU_C * (1.0 + 3.0 * 0.044715 * x2))


def _sigmoid(x):
    return 0.5 * jnp.tanh(0.5 * x) + 0.5


def _load_once(pairs, sem):
    @pl.when(pl.program_id(0) == 0)
    def _():
        cps = [pltpu.make_async_copy(s, d, sem.at[i]) for i, (s, d) in enumerate(pairs)]
        for cp in cps:
            cp.start()
        for cp in cps:
            cp.wait()


def _params(n_axes=1, vmem=VMEM_LIMIT):
    return pltpu.CompilerParams(dimension_semantics=("arbitrary",) * n_axes, vmem_limit_bytes=vmem)


def _row_spec(tm, n, rev_nt=None):
    if rev_nt is None:
        return pl.BlockSpec((tm, n), lambda i: (i, 0))
    return pl.BlockSpec((tm, n), lambda i: (rev_nt - 1 - i, 0))


def _const_spec(shape):
    nd = len(shape)
    return pl.BlockSpec(shape, lambda i: (0,) * nd)


def _add_deps(body, in_specs, args, deps):
    nd = len(deps)
    if nd == 0:
        return body, list(in_specs), list(args)

    def wrapped(*refs):
        return body(*refs[nd:])

    return wrapped, [ANY] * nd + list(in_specs), list(deps) + list(args)


def _zero_first(refs):
    @pl.when(pl.program_id(0) == 0)
    def _():
        for r in refs:
            r[...] = jnp.zeros(r.shape, r.dtype)


def _mixer_a_fwd(x, nmix, gv, wsm, bsb, w_in, w_out):
    T = x.shape[0]
    tm = min(512, T)
    nt = T // tm
    nw = 2 * D_MODEL // N_SHARD

    def body(x_ref, nmix_ref, gv_ref, ws_ref, bsb_ref, w_in_hbm, w_out_hbm,
             h1_ref, zp_ref, w_in_v, w_out_v, gated_v, sem):
        _load_once([(w_in_hbm, w_in_v), (w_out_hbm, w_out_v)], sem)
        xv = x_ref[...]
        xn = _rms(xv, nmix_ref[...])[0].astype(BF16)
        for j in range(N_SHARD):
            zp_ref[:, j * nw:(j + 1) * nw] = _dot(xn, w_in_v[j])
        z = _gelu(zp_ref[...])
        u = z[:, :D_MODEL]
        vn = _rms(z[:, D_MODEL:], gv_ref[...])[0].astype(BF16)
        for c in range(tm // CHUNK):
            rows = slice(c * CHUNK, (c + 1) * CHUNK)
            for h in range(A_GROUPS):
                cols = slice(h * CHUNK, (h + 1) * CHUNK)
                s = _dot(ws_ref[h], vn[rows, cols]) + bsb_ref[h]
                gated_v[rows, cols] = (u[rows, cols] * s).astype(BF16)
        h1_ref[...] = xv + _dot(gated_v[...], w_out_v[...])

    return pl.pallas_call(
        body, name="mixer_a_fwd", grid=(nt,),
        in_specs=[_row_spec(tm, D_MODEL), _const_spec((1, D_MODEL)), _const_spec((1, D_MODEL)),
                  _const_spec((A_GROUPS, CHUNK, CHUNK)), _const_spec((A_GROUPS, CHUNK, CHUNK)), ANY, ANY],
        out_specs=[_row_spec(tm, D_MODEL), _row_spec(tm, 2 * D_MODEL)],
        out_shape=[jax.ShapeDtypeStruct((T, D_MODEL), F32), jax.ShapeDtypeStruct((T, 2 * D_MODEL), F32)],
        scratch_shapes=[pltpu.VMEM((N_SHARD, D_MODEL, nw), BF16), pltpu.VMEM((D_MODEL, D_MODEL), BF16),
                        pltpu.VMEM((tm, D_MODEL), BF16), pltpu.SemaphoreType.DMA((2,))],
        compiler_params=_params(),
    )(x, nmix, gv, wsm, bsb, w_in, w_out)


def _mixer_a_bwd(dh, x, zp, nmix, gv, wsm, bsb, tril, w_in, w_out, deps=()):
    T = x.shape[0]
    tm = min(256, T)
    nt = T // tm
    nw = 2 * D_MODEL // N_SHARD

    def body(dh_ref, x_ref, zp_ref, nmix_ref, gv_ref, ws_ref, bsb_ref, tril_ref, w_in_hbm, w_out_hbm,
             dx_ref, dwin_ref, dwout_ref, dws_ref, dbs_ref, dgv_ref, dnmix_ref,
             w_in_v, w_out_v, du_v, dvn_v, dbs_v, gated_ref, sem):
        _load_once([(w_in_hbm, w_in_v), (w_out_hbm, w_out_v)], sem)
        _zero_first([dws_ref, dbs_v, dgv_ref, dnmix_ref, dwin_ref, dwout_ref])
        i = pl.program_id(0)
        dhv = dh_ref[...]
        dhb = dhv.astype(BF16)
        xv = x_ref[...]
        xn, xh, r = _rms(xv, nmix_ref[...])
        xnb = xn.astype(BF16)
        zpv = zp_ref[...]
        z = _gelu(zpv)
        u = z[:, :D_MODEL]
        vn_f, vh, rv = _rms(z[:, D_MODEL:], gv_ref[...])
        vn = vn_f.astype(BF16)
        dgated = _dot_nt(dhb, w_out_v[...])
        for c in range(tm // CHUNK):
            rows = slice(c * CHUNK, (c + 1) * CHUNK)
            for h in range(A_GROUPS):
                cols = slice(h * CHUNK, (h + 1) * CHUNK)
                vn_h = vn[rows, cols]
                s = _dot(ws_ref[h], vn_h) + bsb_ref[h]
                dgt = dgated[rows, cols]
                u_h = u[rows, cols]
                gated_ref[rows, cols] = (u_h * s).astype(BF16)
                du_v[rows, cols] = dgt * s
                ds = dgt * u_h
                dsb = ds.astype(BF16)
                dws_ref[h] += _dot_nt(dsb, vn_h)
                dbs_v[h] += ds
                dvn_v[rows, cols] = _dot_tn(ws_ref[h], dsb)
        dwout_ref[...] += _dot_tn(gated_ref[...], dhb)
        dv, dgv = _rms_bwd(dvn_v[...], vh, rv, gv_ref[...])
        dgv_ref[...] += dgv
        dzu = (du_v[...] * _gelu_grad(zpv[:, :D_MODEL])).astype(BF16)
        dzv = (dv * _gelu_grad(zpv[:, D_MODEL:])).astype(BF16)
        dzs = (dzu[:, :nw], dzu[:, nw:], dzv[:, :nw], dzv[:, nw:])
        dxn = jnp.zeros((tm, D_MODEL), F32)
        for j in range(N_SHARD):
            dxn = dxn + _dot_nt(dzs[j], w_in_v[j])
            dwin_ref[j] += _dot_tn(xnb, dzs[j])
        dxx, dn = _rms_bwd(dxn, xh, r, nmix_ref[...])
        dnmix_ref[...] += dn
        dx_ref[...] = dhv + dxx

        @pl.when(i == nt - 1)
        def _():
            for h in range(A_GROUPS):
                dws_ref[h] = dws_ref[h] * tril_ref[...]
                dbs_ref[h] = jnp.broadcast_to(jnp.sum(dbs_v[h], axis=1, keepdims=True), (CHUNK, CHUNK))

    grp = (A_GROUPS, CHUNK, CHUNK)
    body, in_specs, args = _add_deps(
        body, [_row_spec(tm, D_MODEL), _row_spec(tm, D_MODEL), _row_spec(tm, 2 * D_MODEL),
               _const_spec((1, D_MODEL)), _const_spec((1, D_MODEL)), _const_spec(grp), _const_spec(grp),
               _const_spec((CHUNK, CHUNK)), ANY, ANY],
        [dh, x, zp, nmix, gv, wsm, bsb, tril, w_in, w_out], deps)
    return pl.pallas_call(
        body, name="mixer_a_bwd", grid=(nt,), in_specs=in_specs,
        out_specs=[_row_spec(tm, D_MODEL), _const_spec((N_SHARD, D_MODEL, nw)), _const_spec((D_MODEL, D_MODEL)),
                   _const_spec(grp), _const_spec(grp), _const_spec((1, D_MODEL)), _const_spec((1, D_MODEL))],
        out_shape=[jax.ShapeDtypeStruct((T, D_MODEL), F32), jax.ShapeDtypeStruct((N_SHARD, D_MODEL, nw), F32),
                   jax.ShapeDtypeStruct((D_MODEL, D_MODEL), F32),
                   jax.ShapeDtypeStruct(grp, F32), jax.ShapeDtypeStruct(grp, F32),
                   jax.ShapeDtypeStruct((1, D_MODEL), F32), jax.ShapeDtypeStruct((1, D_MODEL), F32)],
        scratch_shapes=[pltpu.VMEM((N_SHARD, D_MODEL, nw), BF16), pltpu.VMEM((D_MODEL, D_MODEL), BF16),
                        pltpu.VMEM((tm, D_MODEL), F32), pltpu.VMEM((tm, D_MODEL), F32),
                        pltpu.VMEM(grp, F32), pltpu.VMEM((tm, D_MODEL), BF16), pltpu.SemaphoreType.DMA((2,))],
        compiler_params=_params(),
    )(*args)


def _load_ffn_weights(w_up_hbm, w_dn_hbm, layer, w_up_v, w_dn_v, sem):
    _load_once([(w_up_hbm, w_up_v), (w_dn_hbm, w_dn_v)], sem)


def _ffn_fwd(h, nffn, cw, cb, w_up, w_dn, layer):
    T = h.shape[0]
    tm = min(256, T)
    nt = T // tm

    def body(h_ref, n_ref, cw_ref, cb_ref, w_up_hbm, w_dn_hbm, out_ref, hh_ref, c_ref,
             w_up_v, w_dn_v, carry_v, sem):
        _load_ffn_weights(w_up_hbm, w_dn_hbm, layer, w_up_v, w_dn_v, sem)
        _zero_first([carry_v])
        xv = h_ref[...]
        xf = _rms(xv, n_ref[...])[0].astype(BF16)
        acc = xv
        for j in range(2):
            cs = []
            for blk in (j, j + 2):
                cols = slice(blk * FF_BLK, (blk + 1) * FF_BLK)
                hh = _dot(xf, w_up_v[blk])
                hh_ref[:, cols] = hh.astype(BF16)
                ext = jnp.concatenate([carry_v[blk], hh], axis=0)
                carry_v[blk] = hh[tm - 8:, :]
                s1 = pltpu.roll(ext, 1, 0)[8:]
                s2 = pltpu.roll(ext, 2, 0)[8:]
                cv = (cb_ref[:, cols] + cw_ref[0:1, cols] * s2 + cw_ref[1:2, cols] * s1
                      + cw_ref[2:3, cols] * hh)
                c_ref[:, cols] = cv.astype(BF16)
                cs.append(cv)
            act = (cs[0] * _sigmoid(cs[0]) * cs[1]).astype(BF16)
            acc = acc + _dot(act, w_dn_v[j * FF_BLK:(j + 1) * FF_BLK, :])
        out_ref[...] = acc

    return pl.pallas_call(
        body, name=f"ffn_fwd{layer}", grid=(nt,),
        in_specs=[_row_spec(tm, D_MODEL), _const_spec((1, D_MODEL)), _const_spec((3, N_FF)),
                  _const_spec((1, N_FF)), ANY, ANY],
        out_specs=[_row_spec(tm, D_MODEL), _row_spec(tm, N_FF), _row_spec(tm, N_FF)],
        out_shape=[jax.ShapeDtypeStruct((T, D_MODEL), F32), jax.ShapeDtypeStruct((T, N_FF), BF16),
                   jax.ShapeDtypeStruct((T, N_FF), BF16)],
        scratch_shapes=[pltpu.VMEM((N_SHARD, D_MODEL, FF_BLK), BF16), pltpu.VMEM((D_FF, D_MODEL), BF16),
                        pltpu.VMEM((N_SHARD, 8, FF_BLK), F32), pltpu.SemaphoreType.DMA((2 * N_SHARD,))],
        compiler_params=_params(),
    )(h, nffn, cw, cb, w_up, w_dn)


def _wgrad(a, b, bn, col_sharded, name, deps=()):
    T, K = a.shape
    N = b.shape[1]
    tt = min(2048, T)
    nn, ntt = N // bn, T // tt
    kr = K // N_SHARD

    def body(a_ref, b_ref, o_ref):
        @pl.when(pl.program_id(1) == 0)
        def _():
            o_ref[...] = jnp.zeros(o_ref.shape, F32)
        d = _dot_tn(a_ref[...].astype(BF16), b_ref[...].astype(BF16))
        if col_sharded:
            o_ref[...] += d
        else:
            for j in range(N_SHARD):
                o_ref[j] += d[j * kr:(j + 1) * kr]

    if col_sharded:
        assert nn == N_SHARD
        out_spec = pl.BlockSpec((None, K, bn), lambda n, t: (n, 0, 0))
        out_shape = jax.ShapeDtypeStruct((N_SHARD, K, bn), F32)
    else:
        out_spec = pl.BlockSpec((N_SHARD, kr, bn), lambda n, t: (0, 0, n))
        out_shape = jax.ShapeDtypeStruct((N_SHARD, kr, N), F32)
    body, in_specs, args = _add_deps(
        body, [pl.BlockSpec((tt, K), lambda n, t: (t, 0)), pl.BlockSpec((tt, bn), lambda n, t: (t, n))],
        [a, b], deps)
    return pl.pallas_call(
        body, name=name, grid=(nn, ntt), in_specs=in_specs, out_specs=out_spec, out_shape=out_shape,
        compiler_params=pltpu.CompilerParams(dimension_semantics=("arbitrary",) * 2, vmem_limit_bytes=VMEM_LIMIT),
    )(*args)


def _ffn_bwd(dh, h, hh, c, nffn, cw, w_up, w_dn, layer, deps=(), between=None):
    T = h.shape[0]
    tm = min(256, T)
    nt = T // tm

    def body(dh_ref, h_ref, hh_ref, c_ref, n_ref, cw_ref, w_up_hbm, w_dn_hbm,
             dhin_ref, act_ref, dhh_ref, xf_ref, dcw_ref, dcb_ref, dn_ref,
             w_up_v, w_dn_v, carry_v, sem):
        _load_ffn_weights(w_up_hbm, w_dn_hbm, layer, w_up_v, w_dn_v, sem)
        _zero_first([carry_v, dcw_ref, dcb_ref, dn_ref])
        dout = dh_ref[...]
        doutb = dout.astype(BF16)
        xf_f, xh, r = _rms(h_ref[...], n_ref[...])
        xf_ref[...] = xf_f.astype(BF16)
        dxf = jnp.zeros((tm, D_MODEL), F32)
        for j in range(2):
            blks = (j, j + 2)
            cg = c_ref[:, j * FF_BLK:(j + 1) * FF_BLK].astype(F32)
            cu = c_ref[:, (j + 2) * FF_BLK:(j + 3) * FF_BLK].astype(F32)
            sg = _sigmoid(cg)
            sil = cg * sg
            act_ref[:, j * FF_BLK:(j + 1) * FF_BLK] = (sil * cu).astype(BF16)
            dact = _dot_nt(doutb, w_dn_v[j * FF_BLK:(j + 1) * FF_BLK, :])
            dcs = (dact * cu * (sg * (1.0 + cg * (1.0 - sg))), dact * sil)
            for blk, dc in zip(blks, dcs):
                cols = slice(blk * FF_BLK, (blk + 1) * FF_BLK)
                hhv = hh_ref[:, cols].astype(F32)
                ext = jnp.concatenate([dc, carry_v[blk]], axis=0)
                carry_v[blk] = dc[:8, :]
                n = tm + 8
                a1 = pltpu.roll(ext, n - 1, 0)[:tm]
                a2 = pltpu.roll(ext, n - 2, 0)[:tm]
                dcb_ref[:, cols] += jnp.sum(dc, axis=0, keepdims=True)
                dcw_ref[0:1, cols] += jnp.sum(a2 * hhv, axis=0, keepdims=True)
                dcw_ref[1:2, cols] += jnp.sum(a1 * hhv, axis=0, keepdims=True)
                dcw_ref[2:3, cols] += jnp.sum(dc * hhv, axis=0, keepdims=True)
                dhh = (cw_ref[2:3, cols] * dc + cw_ref[1:2, cols] * a1 + cw_ref[0:1, cols] * a2).astype(BF16)
                dhh_ref[:, cols] = dhh
                dxf = dxf + _dot_nt(dhh, w_up_v[blk])
        dxx, dn = _rms_bwd(dxf, xh, r, n_ref[...])
        dn_ref[...] += dn
        dhin_ref[...] = dout + dxx

    rev = functools.partial(_row_spec, rev_nt=nt)
    body, in_specs, args = _add_deps(
        body, [rev(tm, D_MODEL), rev(tm, D_MODEL), rev(tm, N_FF), rev(tm, N_FF),
               _const_spec((1, D_MODEL)), _const_spec((3, N_FF)), ANY, ANY],
        [dh, h, hh, c, nffn, cw, w_up, w_dn], deps)
    dhin, act, dhh, xf, dcw, dcb, dn = pl.pallas_call(
        body, name=f"ffn_bwd{layer}", grid=(nt,), in_specs=in_specs,
        out_specs=[rev(tm, D_MODEL), rev(tm, D_FF), rev(tm, N_FF), rev(tm, D_MODEL),
                   _const_spec((3, N_FF)), _const_spec((1, N_FF)), _const_spec((1, D_MODEL))],
        out_shape=[jax.ShapeDtypeStruct((T, D_MODEL), F32), jax.ShapeDtypeStruct((T, D_FF), BF16),
                   jax.ShapeDtypeStruct((T, N_FF), BF16), jax.ShapeDtypeStruct((T, D_MODEL), BF16),
                   jax.ShapeDtypeStruct((3, N_FF), F32), jax.ShapeDtypeStruct((1, N_FF), F32),
                   jax.ShapeDtypeStruct((1, D_MODEL), F32)],
        scratch_shapes=[pltpu.VMEM((N_SHARD, D_MODEL, FF_BLK), BF16), pltpu.VMEM((D_FF, D_MODEL), BF16),
                        pltpu.VMEM((N_SHARD, 8, FF_BLK), F32), pltpu.SemaphoreType.DMA((2 * N_SHARD,))],
        compiler_params=_params(),
    )(*args)
    deps2 = between(dhin) if between is not None else ()
    dwdn = _wgrad(act, dh, D_MODEL // 2, False, f"wgrad_ffn_down{layer}", deps=deps2)
    dwup = _wgrad(xf, dhh, FF_BLK, True, f"wgrad_ffn_up{layer}", deps=deps2)
    return dhin, dwup, dwdn, dcw, dcb, dn


def _load_ple_weights(w_pin_hbm, w_gate_hbm, layer, w_pin_v, w_gate_v, sem, extra=()):
    _load_once([(w_pin_hbm, w_pin_v), (w_gate_hbm, w_gate_v)] + list(extra), sem)


def _p_spec(tm, layer):
    return pl.BlockSpec((None, tm, PLE_DIM), lambda i: (layer, i, 0))


def _ple_terms(xv, p_ref, n_ref, bg_ref, w_pin_v, w_gate_v, pe_v):
    pw = D_MODEL // N_SHARD
    xg, xh, r = _rms(xv, n_ref[...])
    xgb = xg.astype(BF16)
    gate = _sigmoid(_dot(xgb, w_gate_v[...]) + bg_ref[...])
    pb = p_ref[...].astype(BF16)
    for j in range(N_SHARD):
        pe_v[:, j * pw:(j + 1) * pw] = _dot(pb, w_pin_v[j])
    pe = pe_v[...]
    return pe * gate, pe, gate, xgb, xh, r


def _ple_fwd_kv(h, p, nple, bg, nkv, w_pin, w_gate, w_kv):
    T = h.shape[0]
    tm = min(512, T)
    nt = T // tm
    pw = D_MODEL // N_SHARD

    def body(h_ref, p_ref, n_ref, bg_ref, nkv_ref, w_pin_hbm, w_gate_hbm, w_kv_hbm,
             out_ref, kv_ref, w_pin_v, w_gate_v, w_kv_v, pe_v, sem):
        _load_ple_weights(w_pin_hbm, w_gate_hbm, 0, w_pin_v, w_gate_v, sem, [(w_kv_hbm, w_kv_v)])
        xv = h_ref[...]
        hn = xv + _ple_terms(xv, p_ref, n_ref, bg_ref, w_pin_v, w_gate_v, pe_v)[0]
        out_ref[...] = hn
        kvn = _rms(hn, nkv_ref[...])[0].astype(BF16)
        kv_ref[...] = _dot(kvn, w_kv_v[...]).astype(BF16)

    vec = _const_spec((1, D_MODEL))
    return pl.pallas_call(
        body, name="ple_fwd0", grid=(nt,),
        in_specs=[_row_spec(tm, D_MODEL), _p_spec(tm, 0), vec, vec, vec, ANY, ANY, ANY],
        out_specs=[_row_spec(tm, D_MODEL), _row_spec(tm, 2 * KV_DIM)],
        out_shape=[jax.ShapeDtypeStruct((T, D_MODEL), F32), jax.ShapeDtypeStruct((T, 2 * KV_DIM), BF16)],
        scratch_shapes=[pltpu.VMEM((N_SHARD, PLE_DIM, pw), BF16), pltpu.VMEM((D_MODEL, D_MODEL), BF16),
                        pltpu.VMEM((D_MODEL, 2 * KV_DIM), BF16), pltpu.VMEM((tm, D_MODEL), F32),
                        pltpu.SemaphoreType.DMA((2 * N_SHARD + 1,))],
        compiler_params=_params(),
    )(h, p, nple, bg, nkv, w_pin, w_gate, w_kv)


def _ple_fwd_final(h, p, tgt, nple, bg, nfin, w_pin, w_gate):
    T = h.shape[0]
    tm = min(512, T)
    nt = T // tm
    pw = D_MODEL // N_SHARD

    def body(h_ref, p_ref, t_ref, n_ref, bg_ref, nf_ref, w_pin_hbm, w_gate_hbm,
             dh_ref, loss_ref, dnf_ref, w_pin_v, w_gate_v, pe_v, sem):
        _load_ple_weights(w_pin_hbm, w_gate_hbm, 1, w_pin_v, w_gate_v, sem)
        _zero_first([loss_ref, dnf_ref])
        xv = h_ref[...]
        hn = xv + _ple_terms(xv, p_ref, n_ref, bg_ref, w_pin_v, w_gate_v, pe_v)[0]
        y, yh, r = _rms(hn, nf_ref[...])
        diff = y - t_ref[...]
        loss_ref[...] += 0.5 * jnp.sum(jnp.mean(diff * diff, axis=-1, keepdims=True))
        dy = diff * (1.0 / D_MODEL)
        dhn, dnf = _rms_bwd(dy, yh, r, nf_ref[...])
        dnf_ref[...] += dnf
        dh_ref[...] = dhn

    vec = _const_spec((1, D_MODEL))
    return pl.pallas_call(
        body, name="ple_fwd1", grid=(nt,),
        in_specs=[_row_spec(tm, D_MODEL), _p_spec(tm, 1), _row_spec(tm, D_MODEL), vec, vec, vec, ANY, ANY],
        out_specs=[_row_spec(tm, D_MODEL), _const_spec((8, 128)), vec],
        out_shape=[jax.ShapeDtypeStruct((T, D_MODEL), F32), jax.ShapeDtypeStruct((8, 128), F32),
                   jax.ShapeDtypeStruct((1, D_MODEL), F32)],
        scratch_shapes=[pltpu.VMEM((N_SHARD, PLE_DIM, pw), BF16), pltpu.VMEM((D_MODEL, D_MODEL), BF16),
                        pltpu.VMEM((tm, D_MODEL), F32), pltpu.SemaphoreType.DMA((2 * N_SHARD,))],
        compiler_params=_params(),
    )(h, p, tgt, nple, bg, nfin, w_pin, w_gate)


def _ple_bwd(dh, hb, p, nple, bg, w_pin, w_gate, layer, kv_args=None):
    T = hb.shape[0]
    tm = min(512, T)
    nt = T // tm
    with_kv = kv_args is not None
    pw = D_MODEL // N_SHARD

    def body(*refs):
        if with_kv:
            (dh_ref, hb_ref, p_ref, n_ref, bg_ref, w_pin_hbm, w_gate_hbm, hc_ref, dkv_ref, nkv_ref, w_kv_hbm,
             dhb_ref, dwpin_ref, dwgate_ref, dbg_ref, dn_ref, dwkv_ref, dnkv_ref,
             w_pin_v, w_gate_v, pe_v, w_kv_v, sem) = refs
        else:
            (dh_ref, hb_ref, p_ref, n_ref, bg_ref, w_pin_hbm, w_gate_hbm,
             dhb_ref, dwpin_ref, dwgate_ref, dbg_ref, dn_ref, w_pin_v, w_gate_v, pe_v, sem) = refs
        pairs = [(w_pin_hbm, w_pin_v), (w_gate_hbm, w_gate_v)]
        if with_kv:
            pairs.append((w_kv_hbm, w_kv_v))
        _load_once(pairs, sem)
        _zero_first([dwpin_ref, dwgate_ref, dbg_ref, dn_ref] + ([dwkv_ref, dnkv_ref] if with_kv else []))
        do = dh_ref[...]
        if with_kv:
            dkvb = dkv_ref[...].astype(BF16)
            dkvn = _dot_nt(dkvb, w_kv_v[...])
            kvn, kh, kr = _rms(hc_ref[...], nkv_ref[...])
            dwkv_ref[...] += _dot_tn(kvn.astype(BF16), dkvb)
            dk, dnkv = _rms_bwd(dkvn, kh, kr, nkv_ref[...])
            dnkv_ref[...] += dnkv
            do = do + dk
        _, pe, gate, xgb, xh, r = _ple_terms(hb_ref[...], p_ref, n_ref, bg_ref, w_pin_v, w_gate_v, pe_v)
        dpe = (do * gate).astype(BF16)
        pb = p_ref[...].astype(BF16)
        for j in range(N_SHARD):
            dwpin_ref[j] += _dot_tn(pb, dpe[:, j * pw:(j + 1) * pw])
        da = do * pe * (gate * (1.0 - gate))
        dab = da.astype(BF16)
        dbg_ref[...] += jnp.sum(da, axis=0, keepdims=True)
        dxg = _dot_nt(dab, w_gate_v[...])
        dwgate_ref[...] += _dot_tn(xgb, dab)
        dxx, dn = _rms_bwd(dxg, xh, r, n_ref[...])
        dn_ref[...] += dn
        dhb_ref[...] = do + dxx

    vec = _const_spec((1, D_MODEL))
    row = _row_spec(tm, D_MODEL)
    in_specs = [row, row, _p_spec(tm, layer), vec, vec, ANY, ANY]
    args = [dh, hb, p, nple, bg, w_pin, w_gate]
    out_specs = [row, _const_spec((N_SHARD, PLE_DIM, pw)), _const_spec((D_MODEL, D_MODEL)), vec, vec]
    out_shape = [jax.ShapeDtypeStruct((T, D_MODEL), F32), jax.ShapeDtypeStruct((N_SHARD, PLE_DIM, pw), F32),
                 jax.ShapeDtypeStruct((D_MODEL, D_MODEL), F32),
                 jax.ShapeDtypeStruct((1, D_MODEL), F32), jax.ShapeDtypeStruct((1, D_MODEL), F32)]
    scratch = [pltpu.VMEM((N_SHARD, PLE_DIM, pw), BF16), pltpu.VMEM((D_MODEL, D_MODEL), BF16),
               pltpu.VMEM((tm, D_MODEL), F32)]
    if with_kv:
        hc, dkv, nkv, w_kv = kv_args
        in_specs += [row, _row_spec(tm, 2 * KV_DIM), vec, ANY]
        args += [hc, dkv, nkv, w_kv]
        out_specs += [_const_spec((D_MODEL, 2 * KV_DIM)), vec]
        out_shape += [jax.ShapeDtypeStruct((D_MODEL, 2 * KV_DIM), F32), jax.ShapeDtypeStruct((1, D_MODEL), F32)]
        scratch.append(pltpu.VMEM((D_MODEL, 2 * KV_DIM), BF16))
    scratch.append(pltpu.SemaphoreType.DMA((3,)))
    return pl.pallas_call(
        body, name=f"ple_bwd{layer}", grid=(nt,), in_specs=in_specs, out_specs=out_specs,
        out_shape=out_shape, scratch_shapes=scratch, compiler_params=_params(),
    )(*args)


GROUP_ROWS = GQA_GROUP * BLOCK


def _stack_heads(x, kh):
    return jnp.concatenate([x[:, (kh * GQA_GROUP + g) * HEAD_DIM:(kh * GQA_GROUP + g + 1) * HEAD_DIM]
                            for g in range(GQA_GROUP)], axis=0)


def _attn_fwd(h, nmix, kv, sinks, w_q, w_o):
    T = h.shape[0]
    tm = min(512, T)
    nt = T // tm
    nb = tm // BLOCK

    def body(h_ref, n_ref, kv_ref, kvp_ref, sink_ref, w_q_hbm, w_o_hbm,
             out_ref, q_ref, ao_ref, p_ref, psink_ref, w_q_v, w_o_v, kvs_v, sem):
        _load_once([(w_q_hbm, w_q_v), (w_o_hbm, w_o_v)], sem)
        ti = pl.program_id(0)
        xv = h_ref[...]
        xn = _rms(xv, n_ref[...])[0].astype(BF16)
        q_ref[...] = (_dot(xn, w_q_v[...]) * (HEAD_DIM ** -0.5)).astype(BF16)
        kvs_v[0:BLOCK, :] = kvp_ref[...]
        kvs_v[BLOCK:, :] = kv_ref[...]
        lane = lax.broadcasted_iota(jnp.int32, (BLOCK, 128), 1)
        ii = lax.broadcasted_iota(jnp.int32, (BLOCK, 2 * BLOCK), 0)
        jj = lax.broadcasted_iota(jnp.int32, (BLOCK, 2 * BLOCK), 1)
        dist = ii + BLOCK - jj
        inband = (dist >= 0) & (dist < BLOCK)
        distf = dist.astype(F32)

        def blk_body(b, carry):
            r0 = pl.multiple_of(b * BLOCK, BLOCK)
            valid = inband & ((jj >= BLOCK) | jnp.logical_not(jnp.logical_and(ti == 0, b == 0)))
            qb = q_ref[pl.ds(r0, BLOCK), :]
            band = kvs_v[pl.ds(r0, 2 * BLOCK), :]
            psink_mat = jnp.zeros((BLOCK, 128), F32)
            outs = []
            for hq in range(N_Q_HEADS):
                kh, g = divmod(hq, GQA_GROUP)
                k_h = band[:, kh * HEAD_DIM:(kh + 1) * HEAD_DIM]
                v_h = band[:, KV_DIM + kh * HEAD_DIM:KV_DIM + (kh + 1) * HEAD_DIM]
                s = _dot_nt(qb[:, hq * HEAD_DIM:(hq + 1) * HEAD_DIM], k_h) - _SLOPES[hq] * distf
                s = jnp.where(valid, s, NEG)
                sink = sink_ref[hq]
                m = jnp.maximum(jnp.max(s, axis=1, keepdims=True), sink)
                e = jnp.exp(s - m)
                esink = jnp.exp(sink - m)
                inv = 1.0 / (jnp.sum(e, axis=1, keepdims=True) + esink)
                pb = (e * inv).astype(BF16)
                p_ref[b, kh, g * BLOCK:(g + 1) * BLOCK, :] = pb
                outs.append(_dot(pb, v_h))
                psink_mat = jnp.where(lane == hq, esink * inv, psink_mat)
            ao_ref[pl.ds(r0, BLOCK), :] = jnp.concatenate(outs, axis=1).astype(BF16)
            psink_ref[pl.ds(r0, BLOCK), :] = psink_mat
            return carry

        lax.fori_loop(0, nb, blk_body, 0)
        out_ref[...] = xv + _dot(ao_ref[...], w_o_v[...])

    row = _row_spec(tm, D_MODEL)
    prev_spec = pl.BlockSpec((BLOCK, 2 * KV_DIM), lambda i: (jnp.maximum(i * nb - 1, 0), 0))
    return pl.pallas_call(
        body, name="attn_fwd", grid=(nt,),
        in_specs=[row, _const_spec((1, D_MODEL)), _row_spec(tm, 2 * KV_DIM), prev_spec, SMEM, ANY, ANY],
        out_specs=[row, row, row, pl.BlockSpec((nb, N_KV_HEADS, GROUP_ROWS, 2 * BLOCK), lambda i: (i, 0, 0, 0)),
                   _row_spec(tm, 128)],
        out_shape=[jax.ShapeDtypeStruct((T, D_MODEL), F32), jax.ShapeDtypeStruct((T, D_MODEL), BF16),
                   jax.ShapeDtypeStruct((T, D_MODEL), BF16),
                   jax.ShapeDtypeStruct((T // BLOCK, N_KV_HEADS, GROUP_ROWS, 2 * BLOCK), BF16),
                   jax.ShapeDtypeStruct((T, 128), F32)],
        scratch_shapes=[pltpu.VMEM((D_MODEL, D_MODEL), BF16), pltpu.VMEM((D_MODEL, D_MODEL), BF16),
                        pltpu.VMEM((tm + BLOCK, 2 * KV_DIM), BF16), pltpu.SemaphoreType.DMA((2,))],
        compiler_params=_params(),
    )(h, nmix, kv, kv, sinks, w_q, w_o)


def _attn_bwd(dh, h, q, kv, ao, p, psink, nmix, w_q, w_o):
    T = h.shape[0]
    tm = min(512, T)
    nt = T // tm
    nb = tm // BLOCK

    def body(dh_ref, h_ref, q_ref, kv_ref, kvp_ref, ao_ref, p_ref, psink_ref, n_ref, w_q_hbm, w_o_hbm,
             dhin_ref, dwq_ref, dwo_ref, dkv_ref, dsink_ref, dn_ref,
             w_q_v, w_o_v, kvs_v, dao_v, dq_v, dkv_v, carry_v, sem):
        _load_once([(w_q_hbm, w_q_v), (w_o_hbm, w_o_v)], sem)
        _zero_first([carry_v, dsink_ref, dn_ref, dwq_ref, dwo_ref])
        dout = dh_ref[...]
        doutb = dout.astype(BF16)
        dao_v[...] = _dot_nt(doutb, w_o_v[...])
        dwo_ref[...] += _dot_tn(ao_ref[...], doutb)
        kvs_v[0:BLOCK, :] = kvp_ref[...]
        kvs_v[BLOCK:, :] = kv_ref[...]
        dkv_v[0:tm, :] = jnp.zeros((tm, 2 * KV_DIM), F32)
        dkv_v[tm:, :] = carry_v[...]
        seg = (lax.broadcasted_iota(jnp.int32, (D_MODEL, 128), 0) // HEAD_DIM
               == lax.broadcasted_iota(jnp.int32, (D_MODEL, 128), 1)).astype(BF16)

        def blk_body(b, dsk):
            r0 = pl.multiple_of(b * BLOCK, BLOCK)
            qb = q_ref[pl.ds(r0, BLOCK), :]
            band = kvs_v[pl.ds(r0, 2 * BLOCK), :]
            aob = ao_ref[pl.ds(r0, BLOCK), :].astype(F32)
            daob = dao_v[pl.ds(r0, BLOCK), :]
            prod = daob * aob
            head = prod.astype(BF16)
            tail = (prod - head.astype(F32)).astype(BF16)
            dsk = dsk + psink_ref[pl.ds(r0, BLOCK), :] * (_dot(head, seg) + _dot(tail, seg))
            dqs = []
            dks = []
            dvs = []
            for kh in range(N_KV_HEADS):
                k_h = band[:, kh * HEAD_DIM:(kh + 1) * HEAD_DIM]
                v_h = band[:, KV_DIM + kh * HEAD_DIM:KV_DIM + (kh + 1) * HEAD_DIM]
                q_g = _stack_heads(qb, kh)
                dao_g = _stack_heads(daob, kh)
                prb = p_ref[b, kh]
                pr = prb.astype(F32)
                dd = jnp.sum(dao_g * _stack_heads(aob, kh), axis=1, keepdims=True)
                dao_gb = dao_g.astype(BF16)
                dp = _dot_nt(dao_gb, v_h)
                dsb = (pr * (dp - dd)).astype(BF16)
                dq_g = _dot(dsb, k_h) * (HEAD_DIM ** -0.5)
                dks.append(_dot_tn(dsb, q_g))
                dvs.append(_dot_tn(prb, dao_gb))
                for g in range(GQA_GROUP):
                    dqs.append(dq_g[g * BLOCK:(g + 1) * BLOCK])
            dq_v[pl.ds(r0, BLOCK), :] = jnp.concatenate(dqs, axis=1)
            dkv_v[pl.ds(r0, 2 * BLOCK), :] += jnp.concatenate(dks + dvs, axis=1)
            return dsk

        dsk = lax.fori_loop(0, nb, blk_body, jnp.zeros((BLOCK, 128), F32))
        dsink_ref[...] -= jnp.sum(dsk, axis=0, keepdims=True)
        dqb = dq_v[...].astype(BF16)
        dxn = _dot_nt(dqb, w_q_v[...])
        xn, xh, r = _rms(h_ref[...], n_ref[...])
        dwq_ref[...] += _dot_tn(xn.astype(BF16), dqb)
        dxx, dn = _rms_bwd(dxn, xh, r, n_ref[...])
        dn_ref[...] += dn
        dhin_ref[...] = dout + dxx
        dkv_ref[...] = dkv_v[BLOCK:, :]
        carry_v[...] = dkv_v[0:BLOCK, :]

    rev = functools.partial(_row_spec, rev_nt=nt)
    row = rev(tm, D_MODEL)
    prev_spec = pl.BlockSpec((BLOCK, 2 * KV_DIM), lambda i: (jnp.maximum((nt - 1 - i) * nb - 1, 0), 0))
    return pl.pallas_call(
        body, name="attn_bwd", grid=(nt,),
        in_specs=[row, row, row, rev(tm, 2 * KV_DIM), prev_spec, row,
                  pl.BlockSpec((nb, N_KV_HEADS, GROUP_ROWS, 2 * BLOCK), lambda i: (nt - 1 - i, 0, 0, 0)),
                  rev(tm, 128), _const_spec((1, D_MODEL)), ANY, ANY],
        out_specs=[row, _const_spec((D_MODEL, D_MODEL)), _const_spec((D_MODEL, D_MODEL)), rev(tm, 2 * KV_DIM),
                   _const_spec((8, 128)), _const_spec((1, D_MODEL))],
        out_shape=[jax.ShapeDtypeStruct((T, D_MODEL), F32), jax.ShapeDtypeStruct((D_MODEL, D_MODEL), F32),
                   jax.ShapeDtypeStruct((D_MODEL, D_MODEL), F32), jax.ShapeDtypeStruct((T, 2 * KV_DIM), F32),
                   jax.ShapeDtypeStruct((8, 128), F32), jax.ShapeDtypeStruct((1, D_MODEL), F32)],
        scratch_shapes=[pltpu.VMEM((D_MODEL, D_MODEL), BF16), pltpu.VMEM((D_MODEL, D_MODEL), BF16),
                        pltpu.VMEM((tm + BLOCK, 2 * KV_DIM), BF16), pltpu.VMEM((tm, D_MODEL), F32),
                        pltpu.VMEM((tm, D_MODEL), F32), pltpu.VMEM((tm + BLOCK, 2 * KV_DIM), F32),
                        pltpu.VMEM((BLOCK, 2 * KV_DIM), F32), pltpu.SemaphoreType.DMA((2,))],
        compiler_params=_params(),
    )(dh, h, q, kv, kv, ao, p, psink, nmix, w_q, w_o)


def _mesh_pos():
    return lax.axis_index("x"), lax.axis_index("y"), lax.axis_index("c")


def _other_chips(x, y):
    return [(1 - x, y), (x, 1 - y), (1 - x, 1 - y)]


HBM_SPEC = pl.BlockSpec(memory_space=pltpu.HBM)
SEM_SPEC = pl.BlockSpec(memory_space=pltpu.SEMAPHORE)


def _split_call(name, bufs, waits=(), starts=(), after=()):
    n, nw, ns, na = len(bufs), len(waits), len(starts), len(after)

    def body(*refs):
        brefs = refs[:n]
        wsems = [(refs[n + 2 * k], refs[n + 2 * k + 1]) for k in range(nw)]
        o = n + 2 * nw + na
        ssems = [(refs[o + 2 * k], refs[o + 2 * k + 1]) for k in range(ns)]
        for (ss, rs), (_, _, fn) in zip(wsems, waits):
            for sending, arriving in fn(brefs, ss, rs):
                sending.wait_send()
                arriving.wait_recv()
        for (ss, rs), (_, fn) in zip(ssems, starts):
            for sending, _ in fn(brefs, ss, rs):
                sending.start()
        if ns:
            token = refs[o + 2 * ns + n]
            token[...] = jnp.zeros(token.shape, token.dtype)

    out_shape, out_specs = [], []
    for cnt, _ in starts:
        out_shape += [pltpu.SemaphoreType.DMA((cnt,)), pltpu.SemaphoreType.DMA((cnt,))]
        out_specs += [SEM_SPEC, SEM_SPEC]
    out_shape += [pltpu.HBM(b.shape, b.dtype) for b in bufs]
    out_specs += [HBM_SPEC] * n
    if ns:
        out_shape.append(jax.ShapeDtypeStruct((8, 128), F32))
        out_specs.append(pl.BlockSpec(memory_space=pltpu.VMEM))
    args = [pltpu.with_memory_space_constraint(b, pltpu.HBM) for b in bufs]
    for ss, rs, _ in waits:
        args += [ss, rs]
    args += list(after)
    res = pl.pallas_call(
        body, name=name, out_shape=tuple(out_shape),
        in_specs=[HBM_SPEC] * n + [SEM_SPEC] * (2 * nw) + [ANY] * na, out_specs=tuple(out_specs),
        input_output_aliases={i: 2 * ns + i for i in range(n)},
        compiler_params=pltpu.CompilerParams(has_side_effects=pltpu.SideEffectType.DATAFLOW_SIDE_EFFECTING),
    )(*args)
    sems = [(res[2 * k], res[2 * k + 1]) for k in range(ns)]
    return list(res[2 * ns:2 * ns + n]), sems, (res[2 * ns + n] if ns else None)


def _cast_place(items, name, deps=()):
    n = len(items)
    mats = [a.shape[-2:] for a, _, _ in items]

    def body(*refs):
        ins, outs, scr, sem = refs[:n], refs[n:2 * n], refs[2 * n:3 * n], refs[3 * n]
        x, y, _ = _mesh_pos()
        cps = []
        for t in range(n):
            scr[t][...] = ins[t][...].astype(scr[t].dtype)
            cp = pltpu.make_async_copy(scr[t], outs[t].at[2 * x + y], sem.at[t])
            cp.start()
            cps.append(cp)
        for cp in cps:
            cp.wait()

    def spec(idx, shape):
        return pl.BlockSpec((None,) * len(idx) + tuple(shape), lambda i: tuple(idx) + (0, 0))

    body, in_specs, args = _add_deps(body, [spec(idx, mat) for (_, idx, _), mat in zip(items, mats)],
                                     [a for a, _, _ in items], deps)
    return pl.pallas_call(
        body, name=name, grid=(1,), in_specs=in_specs, out_specs=[ANY] * n,
        out_shape=[jax.ShapeDtypeStruct((N_SHARD,) + tuple(mat), dt) for (_, _, dt), mat in zip(items, mats)],
        scratch_shapes=[pltpu.VMEM(tuple(mat), dt) for (_, _, dt), mat in zip(items, mats)]
        + [pltpu.SemaphoreType.DMA((n,))],
        compiler_params=_params(),
    )(*args)


def _gather_ici(idx):
    def fn(bufs, ss, rs):
        x, y, c = _mesh_pos()
        pairs = []
        for k, t in enumerate(idx):
            half = bufs[t].shape[1] // 2
            mine = bufs[t].at[2 * x + y, pl.ds(c * half, half), :]
            for j, (cx, cy) in enumerate(_other_chips(x, y)):
                theirs = bufs[t].at[2 * cx + cy, pl.ds(c * half, half), :]
                sem = dict(send_sem=ss.at[3 * k + j], recv_sem=rs.at[3 * k + j],
                           device_id=(cx, cy, c), device_id_type=MESH)
                pairs.append((pltpu.make_async_remote_copy(src_ref=mine, dst_ref=mine, **sem),
                              pltpu.make_async_remote_copy(src_ref=mine, dst_ref=theirs, **sem)))
        return pairs
    return fn


def _gather_d2d(idx):
    def fn(bufs, ss, rs):
        x, y, c = _mesh_pos()
        pairs = []
        for k, t in enumerate(idx):
            half = bufs[t].shape[1] // 2
            for j, (cx, cy) in enumerate(_other_chips(x, y)):
                got = bufs[t].at[2 * cx + cy, pl.ds(c * half, half), :]
                theirs = bufs[t].at[2 * cx + cy, pl.ds((1 - c) * half, half), :]
                sem = dict(send_sem=ss.at[3 * k + j], recv_sem=rs.at[3 * k + j],
                           device_id=(x, y, 1 - c), device_id_type=MESH)
                pairs.append((pltpu.make_async_remote_copy(src_ref=got, dst_ref=got, **sem),
                              pltpu.make_async_remote_copy(src_ref=got, dst_ref=theirs, **sem)))
        return pairs
    return fn


def _alloc(shapes, name):
    def body(*refs):
        pass

    return pl.pallas_call(body, name=name, out_specs=[ANY] * len(shapes),
                          out_shape=[jax.ShapeDtypeStruct(s, d) for s, d in shapes])()


def _send_to_sibling(n):
    def fn(bufs, ss, rs):
        x, y, c = _mesh_pos()
        pairs = []
        for t in range(n):
            src = bufs[t]
            if len(src.shape) == 3:
                half = src.shape[1] // 2
                src = src.at[:, pl.ds((1 - c) * half, half), :]
            cp = pltpu.make_async_remote_copy(src_ref=src, dst_ref=bufs[n + t], send_sem=ss.at[t],
                                              recv_sem=rs.at[t], device_id=(x, y, 1 - c), device_id_type=MESH)
            pairs.append((cp, cp))
        return pairs
    return fn


def _send_to_chips(n):
    def fn(bufs, ss, rs):
        x, y, c = _mesh_pos()
        pairs = []
        for j, (cx, cy) in enumerate(_other_chips(x, y)):
            for t in range(n):
                src = bufs[t].at[j] if len(bufs[t].shape) == 3 else bufs[t]
                cp = pltpu.make_async_remote_copy(src_ref=src, dst_ref=bufs[n + t].at[j], send_sem=ss.at[3 * t + j],
                                                  recv_sem=rs.at[3 * t + j], device_id=(cx, cy, c),
                                                  device_id_type=MESH)
                pairs.append((cp, cp))
        return pairs
    return fn


class _Exchange:
    def __init__(self, name, srcs, land_shapes, fn, n_sems):
        self.name, self.fn = name, fn
        lands = _alloc(land_shapes, name + "_alloc")
        self.n = len(srcs)
        self.bufs, sems, self.token = _split_call(name + "_start", list(srcs) + list(lands),
                                                  starts=[(n_sems, fn)])
        self.sems = sems[0]

    def finish(self, after=()):
        bufs, _, _ = _split_call(self.name + "_wait", self.bufs, waits=[(*self.sems, self.fn)], after=after)
        return bufs[:self.n], bufs[self.n:]


def _row_block(rows, cols, mult=8, limit=3 * 512 * 1024, itemsize=4):
    best = None
    for br in range(mult, rows + 1, mult):
        if rows % br == 0 and br * cols * itemsize <= limit:
            best = br
    assert best is not None, (rows, cols)
    return best


_GROUP_BLOCK_BYTES = 1024 * 1024


def _group_plan(ss):
    plan = []
    for s in ss:
        half, cols = s.shape[-2:]
        br = _row_block(half, cols, mult=16, limit=_GROUP_BLOCK_BYTES)
        plan.append((br, half // br))
    return plan, max(nr for _, nr in plan)


def _chip_partial(gs, ss, ids, name):
    n = len(gs)
    plan, steps = _group_plan(ss)

    def body(ids_ref, *refs):
        for t in range(n):
            refs[2 * n + t][...] = (refs[t][...] + refs[n + t][...]).astype(BF16)

    g_specs, s_specs, o_specs = [], [], []
    for (br, nr), s in zip(plan, ss):
        blk = (None, br, s.shape[2])
        g_specs.append(pl.BlockSpec(
            blk, lambda j, r, ids_ref, nr=nr: (ids_ref[2 + j], ids_ref[0] * nr + jnp.minimum(r, nr - 1), 0)))
        s_specs.append(pl.BlockSpec(blk, lambda j, r, ids_ref, nr=nr: (ids_ref[2 + j], jnp.minimum(r, nr - 1), 0)))
        o_specs.append(pl.BlockSpec(blk, lambda j, r, ids_ref, nr=nr: (j, jnp.minimum(r, nr - 1), 0)))
    return pl.pallas_call(
        body, name=name,
        grid_spec=pltpu.PrefetchScalarGridSpec(num_scalar_prefetch=1, grid=(3, steps),
                                               in_specs=g_specs + s_specs, out_specs=o_specs),
        out_shape=[jax.ShapeDtypeStruct((3,) + s.shape[1:], BF16) for s in ss],
        compiler_params=pltpu.CompilerParams(dimension_semantics=("arbitrary", "arbitrary"),
                                             vmem_limit_bytes=VMEM_LIMIT),
    )(ids, *gs, *ss)


def _chip_sum(gs, ss, qs, ids, name):
    n = len(gs)
    plan, steps = _group_plan(ss)

    def body(ids_ref, *refs):
        for t in range(n):
            q_ref = refs[2 * n + t]
            own = refs[t][...] + refs[n + t][...]
            refs[3 * n + t][...] = (own + q_ref[2].astype(F32)) + (q_ref[0].astype(F32) + q_ref[1].astype(F32))

    g_specs, s_specs, q_specs, o_specs = [], [], [], []
    for (br, nr), s in zip(plan, ss):
        cols = s.shape[2]
        g_specs.append(pl.BlockSpec(
            (None, br, cols), lambda r, ids_ref, nr=nr: (ids_ref[1], ids_ref[0] * nr + jnp.minimum(r, nr - 1), 0)))
        s_specs.append(pl.BlockSpec((None, br, cols), lambda r, ids_ref, nr=nr: (ids_ref[1], jnp.minimum(r, nr - 1), 0)))
        q_specs.append(pl.BlockSpec((3, br, cols), lambda r, ids_ref, nr=nr: (0, jnp.minimum(r, nr - 1), 0)))
        o_specs.append(pl.BlockSpec((br, cols), lambda r, ids_ref, nr=nr: (jnp.minimum(r, nr - 1), 0)))
    return pl.pallas_call(
        body, name=name,
        grid_spec=pltpu.PrefetchScalarGridSpec(num_scalar_prefetch=1, grid=(steps,),
                                               in_specs=g_specs + s_specs + q_specs, out_specs=o_specs),
        out_shape=[jax.ShapeDtypeStruct(s.shape[1:], F32) for s in ss],
        compiler_params=pltpu.CompilerParams(dimension_semantics=("arbitrary",), vmem_limit_bytes=VMEM_LIMIT),
    )(ids, *gs, *ss, *qs)


def _adamw_math(w, g, m, v):
    mn = ADAM_B1 * m + (1.0 - ADAM_B1) * g
    vn = ADAM_B2 * v + (1.0 - ADAM_B2) * (g * g)
    m_hat = mn / (1.0 - ADAM_B1 ** ADAM_STEP)
    v_hat = vn / (1.0 - ADAM_B2 ** ADAM_STEP)
    return -ADAM_LR * (m_hat / (jnp.sqrt(v_hat) + ADAM_EPS) + ADAM_WD * w), mn, vn


_ADAMW_BLOCK_BYTES = 256 * 1024


def _adamw_halves(items, ids, name):
    n = len(items)
    plan = []
    for w, own, *_ in items:
        br = _row_block(own.shape[0], w.shape[1], limit=_ADAMW_BLOCK_BYTES)
        plan.append((br, own.shape[0] // br))
    steps = max(2 * nh for _, nh in plan)
    with_stack = [t for t in range(n) if items[t][7] is not None]

    def body(ids_ref, *refs):
        outs = refs[len(refs) - 4 * n:]
        for t in range(n):
            w_ref, own_ref, sib_ref, m_ref, v_ref = refs[5 * t:5 * t + 5]
            g_ref, d_ref, mo_ref, vo_ref = outs[4 * t:4 * t + 4]
            nh = plan[t][1]
            is_own = (jnp.minimum(pl.program_id(0), 2 * nh - 1) // nh) == ids_ref[0]
            g = jnp.where(is_own, own_ref[...], sib_ref[...])
            g_ref[...] = g
            d_ref[...], mo_ref[...], vo_ref[...] = _adamw_math(w_ref[...], g, m_ref[...], v_ref[...])

    in_specs, out_specs, out_shape, args = [], [], [], [ids]
    for (w, own, sib, m, v, layer, n_layers, _), (br, nh) in zip(items, plan):
        blk = (br, w.shape[1])
        base = layer * 2 * nh

        def full_map(r, ids_ref, nh=nh, base=base):
            return (base + jnp.minimum(r, 2 * nh - 1), 0)

        def own_map(r, ids_ref, nh=nh):
            return (jnp.clip(jnp.minimum(r, 2 * nh - 1) - ids_ref[0] * nh, 0, nh - 1), 0)

        def sib_map(r, ids_ref, nh=nh):
            return (jnp.clip(jnp.minimum(r, 2 * nh - 1) - (1 - ids_ref[0]) * nh, 0, nh - 1), 0)

        full = pl.BlockSpec(blk, full_map)
        in_specs += [full, pl.BlockSpec(blk, own_map), pl.BlockSpec(blk, sib_map), full, full]
        out_specs += [full] * 4
        out_shape += [jax.ShapeDtypeStruct(w.shape, F32)] * 4
        args += [w, own, sib, m, v]
    aliases = {}
    for k, t in enumerate(with_stack):
        in_specs += [ANY] * 4
        args += list(items[t][7])
        aliases.update({1 + 5 * n + 4 * k + j: 4 * t + j for j in range(4)})
    res = pl.pallas_call(
        body, name=name,
        grid_spec=pltpu.PrefetchScalarGridSpec(
            num_scalar_prefetch=1, grid=(steps,), in_specs=in_specs, out_specs=out_specs),
        out_shape=out_shape, input_output_aliases=aliases, compiler_params=_params(),
    )(*args)
    return [res[4 * t:4 * t + 4] for t in range(n)]


_PACK_UNIT = 1024


def _pack(arrs):
    flat = []
    for a in arrs:
        f = a.reshape(-1).astype(F32)
        pad = (-f.shape[0]) % _PACK_UNIT
        if pad:
            f = jnp.concatenate([f, jnp.zeros((pad,), F32)])
        flat.append(f)
    return jnp.concatenate(flat).reshape(-1, 128)


def kernel(x, p, norm_mix, norm_ffn, norm_ple, norm_kv, norm_final, a_w_in, a_norm_v, a_w_s, a_b_s, a_w_out, w_kv, b_w_q, b_sinks, b_w_o, f_w_up, f_conv_w, f_conv_b, f_w_down, ple_w_in, ple_w_gate, ple_b_gate, loss_target, m_norm_mix, m_norm_ffn, m_norm_ple, m_norm_kv, m_norm_final, m_a_w_in, m_a_norm_v, m_a_w_s, m_a_b_s, m_a_w_out, m_w_kv, m_b_w_q, m_b_sinks, m_b_w_o, m_f_w_up, m_f_conv_w, m_f_conv_b, m_f_w_down, m_ple_w_in, m_ple_w_gate, m_ple_b_gate, v_norm_mix, v_norm_ffn, v_norm_ple, v_norm_kv, v_norm_final, v_a_w_in, v_a_norm_v, v_a_w_s, v_a_b_s, v_a_w_out, v_w_kv, v_b_w_q, v_b_sinks, v_b_w_o, v_f_w_up, v_f_conv_w, v_f_conv_b, v_f_w_down, v_ple_w_in, v_ple_w_gate, v_ple_b_gate):
    given = dict(locals())

    small_shard = _pack([a_norm_v, f_conv_w])
    pad_rows = (-small_shard.shape[0]) % 16
    if pad_rows:
        small_shard = jnp.concatenate([small_shard, jnp.zeros((pad_rows, 128), F32)])
    groups = [
        [(a_w_in, (0,), BF16), (a_w_out, (0,), BF16), (small_shard, (), F32)],
        [(f_w_up, (0,), BF16), (f_w_down, (0,), BF16)],
        [(ple_w_in, (0,), BF16), (ple_w_gate, (0,), BF16), (w_kv, (), BF16), (b_w_q, (0,), BF16),
         (b_w_o, (0,), BF16), (f_w_up, (1,), BF16), (f_w_down, (1,), BF16), (ple_w_in, (1,), BF16),
         (ple_w_gate, (1,), BF16)],
    ]
    first = list(range(len(groups[0])))
    lands0, sems0, token0 = _split_call("gather_start_g0", _cast_place(groups[0], "cast_place_g0"),
                                        starts=[(3 * len(first), _gather_ici(first))])
    rest, spans, start = [], [], 0
    for gi, items in enumerate(groups[1:], 1):
        rest += _cast_place(items, f"cast_place_g{gi}", deps=(token0,))
        spans.append(list(range(start, start + len(items))))
        start += len(items)
    rest, rest_sems, rest_token = _split_call("gather_start", rest,
                                              starts=[(3 * len(sp), _gather_ici(sp)) for sp in spans])
    group_bufs = [lands0] + [[rest[t] for t in sp] for sp in spans]
    ici_sems = sems0 + rest_sems

    def finish_group(gi, after):
        bufs = group_bufs[gi]
        local = list(range(len(bufs)))
        bufs, d2d_sems, _ = _split_call(f"gather_pass_g{gi}", bufs, waits=[(*ici_sems[gi], _gather_ici(local))],
                                        starts=[(3 * len(local), _gather_d2d(local))], after=after)
        bufs, _, _ = _split_call(f"gather_done_g{gi}", bufs, waits=[(*d2d_sems[0], _gather_d2d(local))])
        return bufs

    def stage0():
        b_in, b_out, b_small = finish_group(0, (rest_token,))
        small_full = b_small.reshape(N_SHARD, -1)
        gv_full = small_full[:, :256].reshape(1, D_MODEL)
        cw_full = small_full[:, _PACK_UNIT:_PACK_UNIT + 2 * 3 * FF_BLK].reshape(N_SHARD, 2, 3, FF_BLK)
        cw_full = jnp.transpose(cw_full, (1, 2, 0, 3)).reshape(2, 3, N_FF)
        return gv_full, cw_full, b_in, b_out.reshape(D_MODEL, D_MODEL)

    def stage1(after):
        b_up, b_dn = finish_group(1, after)
        return b_up, b_dn.reshape(D_FF, D_MODEL)

    def stage2(after):
        pin0, gate0, kv_w, wq, wo, up1, dn1, pin1, gate1 = finish_group(2, after)
        sq = lambda a: a.reshape(D_MODEL, -1)
        return dict(w_pin=[pin0, pin1], w_gate=[sq(gate0), sq(gate1)], w_kv=sq(kv_w), w_q=sq(wq), w_o=sq(wo),
                    w_up1=up1, w_dn1=dn1.reshape(D_FF, D_MODEL))

    dx, (loss, (out_g, out_d, out_m, out_v)) = _local_step(
        x[0], p.reshape(2, -1, PLE_DIM), loss_target[0], norm_mix, norm_ffn, norm_ple, norm_kv, norm_final, a_w_s, a_b_s,
        b_sinks, f_conv_b, ple_b_gate, stage0, stage1, stage2, _Reducer(given))
    weight_names = ['norm_mix', 'norm_ffn', 'norm_ple', 'norm_kv', 'norm_final', 'a_w_in', 'a_norm_v', 'a_w_s',
                    'a_b_s', 'a_w_out', 'w_kv', 'b_w_q', 'b_sinks', 'b_w_o', 'f_w_up', 'f_conv_w', 'f_conv_b',
                    'f_w_down', 'ple_w_in', 'ple_w_gate', 'ple_b_gate']
    return (loss, dx.reshape(x.shape), *[out_g[k] for k in weight_names], *[out_d[k] for k in weight_names],
            *[out_m[k] for k in weight_names], *[out_v[k] for k in weight_names])


def _local_step(xs, p, tgt, norm_mix, norm_ffn, norm_ple, norm_kv, norm_final, a_w_s, a_b_s, b_sinks,
                f_conv_b, ple_b_gate, stage0, stage1, stage2, sched):
    tril = jnp.tril(jnp.ones((CHUNK, CHUNK), F32))
    wsm = (a_w_s[0] * tril[None]).astype(BF16)
    bsb = jnp.broadcast_to(a_b_s[0][:, :, None], (A_GROUPS, CHUNK, CHUNK))
    sinks = b_sinks[0]
    row = lambda a: a.reshape(1, -1)

    gv_full, cw_full, w_in, w_out = stage0()
    h1, zp = _mixer_a_fwd(xs, row(norm_mix[0]), gv_full, wsm, bsb, w_in, w_out)
    w_up0, w_dn0 = stage1((h1,))
    h2, hh0, c0 = _ffn_fwd(h1, row(norm_ffn[0]), cw_full[0], row(f_conv_b[0]), w_up0, w_dn0, 0)
    rest = stage2((h2,))
    w_pin, w_gate, w_kv_f, w_q, w_o = rest['w_pin'], rest['w_gate'], rest['w_kv'], rest['w_q'], rest['w_o']
    w_up = [w_up0, rest['w_up1']]
    w_dn = [w_dn0, rest['w_dn1']]
    h3, kv = _ple_fwd_kv(h2, p, row(norm_ple[0]), row(ple_b_gate[0]), row(norm_kv), w_pin[0], w_gate[0], w_kv_f)
    h4, q, ao, probs, psink = _attn_fwd(h3, row(norm_mix[1]), kv, sinks, w_q, w_o)
    h5, hh1, c1 = _ffn_fwd(h4, row(norm_ffn[1]), cw_full[1], row(f_conv_b[1]), w_up[1], w_dn[1], 1)
    dh6, loss_acc, dn_final = _ple_fwd_final(
        h5, p, tgt, row(norm_ple[1]), row(ple_b_gate[1]), row(norm_final), w_pin[1], w_gate[1])

    def pieces(g):
        return g.reshape(N_SHARD, -1, g.shape[-1])

    dh5, g_pin1, g_gate1, dbg1, dnple1 = _ple_bwd(dh6, h5, p, row(norm_ple[1]), row(ple_b_gate[1]), w_pin[1], w_gate[1], 1)
    early = {('ple_w_in', 1): g_pin1, ('ple_w_gate', 1): pieces(g_gate1)}
    dh4, g_up1, g_dn1, dcw1, dcb1, dnffn1 = _ffn_bwd(
        dh5, h4, hh1, c1, row(norm_ffn[1]), cw_full[1], w_up[1], w_dn[1], 1)
    early['f_w_down', 1] = pieces(g_dn1)
    early['f_w_up', 1] = g_up1
    dh3a, g_wq, g_wo, dkv, dsink, dnmix1 = _attn_bwd(dh4, h3, q, kv, ao, probs, psink, row(norm_mix[1]), w_q, w_o)
    early['b_w_o', 0] = pieces(g_wo)
    early['b_w_q', 0] = pieces(g_wq)
    dh2, g_pin0, g_gate0, dbg0, dnple0, g_wkv, dnkv = _ple_bwd(
        dh3a, h2, p, row(norm_ple[0]), row(ple_b_gate[0]), w_pin[0], w_gate[0], 0,
        kv_args=(h3, dkv, row(norm_kv), w_kv_f))
    early['w_kv', 0] = pieces(g_wkv)
    early['ple_w_in', 0] = g_pin0
    early['ple_w_gate', 0] = pieces(g_gate0)
    deps = sched.early_ready(early)
    dh1, g_up0, g_dn0, dcw0, dcb0, dnffn0 = _ffn_bwd(
        dh2, h1, hh0, c0, row(norm_ffn[0]), cw_full[0], w_up[0], w_dn[0], 0, deps=deps,
        between=lambda part: sched.after_ffn_half((part,)))
    deps = sched.ffn0_ready({('f_w_down', 0): pieces(g_dn0), ('f_w_up', 0): g_up0})
    dx, g_win, g_wout, dws, dbs, dgv, dnmix0 = _mixer_a_bwd(
        dh1, xs, zp, row(norm_mix[0]), gv_full, wsm, bsb, tril, w_in, w_out, deps=deps)
    g_wout = pieces(g_wout)

    small_grads = {
        'norm_mix': jnp.concatenate([dnmix0, dnmix1]), 'norm_ffn': jnp.concatenate([dnffn0, dnffn1]),
        'norm_ple': jnp.concatenate([dnple0, dnple1]), 'norm_kv': dnkv, 'norm_final': dn_final,
        'a_norm_v': dgv, 'a_w_s': dws.reshape(A_GROUPS * CHUNK, CHUNK), 'a_b_s': dbs[:, :, 0],
        'b_sinks': dsink[0:1, :], 'f_conv_w': jnp.concatenate([dcw0, dcw1]),
        'f_conv_b': jnp.concatenate([dcb0, dcb1]), 'ple_b_gate': jnp.concatenate([dbg0, dbg1]),
        'loss': loss_acc,
    }
    outs = sched.finish({('a_w_in', 0): g_win, ('a_w_out', 0): g_wout}, small_grads, (dx,))
    return dx, outs


class _Reducer:
    def __init__(self, given):
        self.given = given
        cx, cy, cc = _mesh_pos()
        self.shard = 2 * cx + cy
        s = self.shard
        self.ids = jnp.stack([cc, s, s ^ 2, s ^ 1, s ^ 3]).astype(jnp.int32)
        self.out = [{}, {}, {}, {}]
        self.stacked = {}

    def _send(self, tag, grads, small=()):
        keys = list(grads)
        srcs = [grads[k] for k in keys] + list(small)
        shapes = [((N_SHARD, g.shape[1] // 2, g.shape[2]), F32) for g in srcs[:len(keys)]]
        shapes += [(s.shape, F32) for s in small]
        return keys, _Exchange(f"send_{tag}", srcs, shapes, _send_to_sibling(len(srcs)), len(srcs))

    def _exchange(self, tag, keys, send, after):
        srcs, lands = send.finish(after)
        n = len(keys)
        parts = _chip_partial(srcs[:n], lands[:n], self.ids, f"chip_partial_{tag}")
        shapes = [(p.shape, BF16) for p in parts]
        if len(srcs) > n:
            small = _small_add(srcs[n:], lands[n:])
            parts += small
            shapes += [((3,) + s.shape, F32) for s in small]
        exch = _Exchange(f"exch_{tag}", parts, shapes, _send_to_chips(len(parts)), 3 * len(parts))
        return (keys, srcs[:n], lands[:n], exch)

    def _swap(self, tag, state, after):
        keys, grads, sib, exch = state
        parts, recv = exch.finish(after)
        n = len(keys)
        own = _chip_sum(grads, sib, recv[:n], self.ids, f"chip_sum_{tag}")
        small_red = _small_sum(parts[n:], recv[n:]) if len(parts) > n else None
        return keys, _Exchange(f"swap_{tag}", own, [(o.shape, F32) for o in own], _send_to_sibling(n), n), small_red

    def _adamw(self, keys, swap, after):
        own, sib = swap.finish(after)
        todo = list(zip(keys, own, sib))
        last, batch_no = None, 0
        while todo:
            batch, later, seen = [], [], set()
            for entry in todo:
                (later if entry[0][0] in seen else batch).append(entry)
                seen.add(entry[0][0])
            items = []
            for (name, layer), o, s in batch:
                w = self.given[name]
                c2 = w.shape[-1]
                items.append((w.reshape(-1, c2), o, s, self.given['m_' + name].reshape(-1, c2),
                              self.given['v_' + name].reshape(-1, c2), layer,
                              w.shape[0] if w.ndim == 3 else 1, self.stacked.get(name)))
            results = _adamw_halves(items, self.ids, f"adamw_{swap.name}_{batch_no}")
            for ((name, layer), _, _), res in zip(batch, results):
                self.stacked[name] = res
                if layer == 0:
                    for dst, r in zip(self.out, res):
                        dst[name] = r.reshape(self.given[name].shape)
                last = res[0]
            todo, batch_no = later, batch_no + 1
        return last

    def early_ready(self, grads):
        self.e_keys, self.e_send = self._send("e", grads)
        return (self.e_send.token,)

    def after_ffn_half(self, after):
        self.e_state = self._exchange("e", self.e_keys, self.e_send, after)
        return (self.e_state[3].token,)

    def ffn0_ready(self, grads):
        _, self.e_swap, _ = self._swap("e", self.e_state, tuple(grads.values()))
        f_keys, f_send = self._send("f", grads)
        self.f_state = self._exchange("f", f_keys, f_send, ())
        return (self.f_state[3].token, self.e_swap.token)

    def finish(self, grads, small_grads, after):
        small_names = list(small_grads)
        a_keys, a_send = self._send("a", grads, [small_grads[k] for k in small_names])
        a_state = self._exchange("a", a_keys, a_send, after)
        e_done = self._adamw(self.e_keys, self.e_swap, (a_state[3].token,))
        f_keys, f_swap, _ = self._swap("f", self.f_state, (e_done,))
        f_done = self._adamw(f_keys, f_swap, ())
        _, a_swap, small_red = self._swap("a", a_state, (f_done,))
        self._adamw(a_keys, a_swap, ())

        given = self.given
        reduced = dict(zip(small_names, small_red))
        loss = reduced.pop('loss')[0, 0]
        names = list(reduced)
        items = []
        for k in names:
            g = reduced[k]
            cols = g.shape[1] // N_SHARD if k in ('a_norm_v', 'f_conv_w') else g.shape[1]
            view = lambda a: _lane_pad(a.reshape(g.shape[0], -1), cols)
            items.append((view(given[k]), g, view(given['m_' + k]), view(given['v_' + k])))
        res = _adamw_small(items, self.ids)
        for k, four in zip(names, res):
            width = given[k].size // four[0].shape[0]
            for dst, r in zip(self.out, four):
                dst[k] = r[:, :width].reshape(given[k].shape)
        return loss, self.out


def _lane_pad(a, cols):
    return a if a.shape[1] == cols else jnp.pad(a, ((0, 0), (0, cols - a.shape[1])))


def _small_add(a_list, b_list):
    n = len(a_list)

    def body(*refs):
        for t in range(n):
            refs[2 * n + t][...] = refs[t][...] + refs[n + t][...]

    return pl.pallas_call(body, name="chip_partial_small",
                          out_shape=[jax.ShapeDtypeStruct(a.shape, F32) for a in a_list])(*a_list, *b_list)


def _small_sum(parts, recvs):
    n = len(parts)

    def body(*refs):
        for t in range(n):
            q = refs[n + t]
            refs[2 * n + t][...] = (refs[t][...] + q[2]) + (q[0] + q[1])

    return pl.pallas_call(body, name="chip_sum_small",
                          out_shape=[jax.ShapeDtypeStruct(p.shape, F32) for p in parts])(*parts, *recvs)


def _adamw_small(items, ids):
    n = len(items)

    def body(ids_ref, *refs):
        for t in range(n):
            w_ref, g_ref, m_ref, v_ref = refs[4 * t:4 * t + 4]
            g_out, d_ref, mo_ref, vo_ref = refs[4 * n + 4 * t:4 * n + 4 * t + 4]
            g = g_ref[...]
            g_out[...] = g
            d_ref[...], mo_ref[...], vo_ref[...] = _adamw_math(w_ref[...], g, m_ref[...], v_ref[...])

    in_specs, out_specs, out_shape, args = [], [], [], []
    for w, g, m, v in items:
        full = pl.BlockSpec(w.shape, lambda i, ids_ref: (0, 0))
        g_spec = full if g.shape == w.shape else pl.BlockSpec(w.shape, lambda i, ids_ref: (0, ids_ref[1]))
        in_specs += [full, g_spec, full, full]
        out_specs += [full] * 4
        out_shape += [jax.ShapeDtypeStruct(w.shape, F32)] * 4
        args += [w, g, m, v]
    res = pl.pallas_call(
        body, name="adamw_small",
        grid_spec=pltpu.PrefetchScalarGridSpec(num_scalar_prefetch=1, grid=(1,), in_specs=in_specs,
                                               out_specs=out_specs),
        out_shape=out_shape, compiler_params=_params(),
    )(ids, *args)
    return [res[4 * t:4 * t + 4] for t in range(n)]
```

```python
import functools
import math

import numpy as np
import jax
import jax.numpy as jnp
from jax import lax
from jax.experimental import pallas as pl
from jax.experimental.pallas import tpu as pltpu

F32 = jnp.float32
BF16 = jnp.bfloat16

D_MODEL = 1024
CHUNK = 128
A_GROUPS = 8
HEAD_DIM = 64
N_Q_HEADS = 16
N_KV_HEADS = 4
GQA_GROUP = N_Q_HEADS // N_KV_HEADS
KV_DIM = N_KV_HEADS * HEAD_DIM
BLOCK = 128
D_FF = 2816
N_FF = 2 * D_FF
FF_BLK = N_FF // 4
PLE_DIM = 256
EPS = 1e-6
NEG = -1e30
N_SHARD = 4

ADAM_LR = 0.001
ADAM_B1 = 0.9
ADAM_B2 = 0.999
ADAM_EPS = 1e-08
ADAM_WD = 0.01
ADAM_STEP = 10

VMEM_LIMIT = 60 * 1024 * 1024
MESH = pl.DeviceIdType.MESH
ANY = pl.BlockSpec(memory_space=pl.ANY)
SMEM = pl.BlockSpec(memory_space=pltpu.SMEM)

_SLOPES = [float(np.float32(2.0 ** (-8.0 * (h + 1) / N_Q_HEADS))) for h in range(N_Q_HEADS)]


def _dot(a, b):
    return jnp.dot(a, b, preferred_element_type=F32)


def _dot_nt(a, b):
    return lax.dot_general(a, b, (((1,), (1,)), ((), ())), preferred_element_type=F32)


def _dot_tn(a, b):
    return lax.dot_general(a, b, (((0,), (0,)), ((), ())), preferred_element_type=F32)


def _rms(x, g):
    r = lax.rsqrt(jnp.mean(x * x, axis=-1, keepdims=True) + EPS)
    xh = x * r
    return xh * g, xh, r


def _rms_bwd(dy, xh, r, g):
    dxh = dy * g
    dg = jnp.sum(dy * xh, axis=0, keepdims=True)
    dx = r * (dxh - xh * jnp.mean(dxh * xh, axis=-1, keepdims=True))
    return dx, dg


_GELU_C = math.sqrt(2.0 / math.pi)


def _gelu(x):
    t = jnp.tanh(_GELU_C * (x + 0.044715 * (x * x * x)))
    return 0.5 * x * (1.0 + t)


def _gelu_grad(x):
    x2 = x * x
    t = jnp.tanh(_GELU_C * (x + 0.044715 * (x2 * x)))
    return 0.5 * (1.0 + t) + 0.5 * x * (1.0 - t * t) * (_GELU_C * (1.0 + 3.0 * 0.044715 * x2))


def _sigmoid(x):
    return 0.5 * jnp.tanh(0.5 * x) + 0.5


def _load_once(pairs, sem):
    @pl.when(pl.program_id(0) == 0)
    def _():
        cps = [pltpu.make_async_copy(s, d, sem.at[i]) for i, (s, d) in enumerate(pairs)]
        for cp in cps:
            cp.start()
        for cp in cps:
            cp.wait()


def _params(n_axes=1, vmem=VMEM_LIMIT):
    return pltpu.CompilerParams(dimension_semantics=("arbitrary",) * n_axes, vmem_limit_bytes=vmem)


def _row_spec(tm, n, rev_nt=None):
    if rev_nt is None:
        return pl.BlockSpec((tm, n), lambda i: (i, 0))
    return pl.BlockSpec((tm, n), lambda i: (rev_nt - 1 - i, 0))


def _const_spec(shape):
    nd = len(shape)
    return pl.BlockSpec(shape, lambda i: (0,) * nd)


def _add_deps(body, in_specs, args, deps):
    nd = len(deps)
    if nd == 0:
        return body, list(in_specs), list(args)

    def wrapped(*refs):
        return body(*refs[nd:])

    return wrapped, [ANY] * nd + list(in_specs), list(deps) + list(args)


def _zero_first(refs):
    @pl.when(pl.program_id(0) == 0)
    def _():
        for r in refs:
            r[...] = jnp.zeros(r.shape, r.dtype)


def _mixer_a_fwd(x, nmix, gv, wsm, bsb, w_in, w_out):
    T = x.shape[0]
    tm = min(512, T)
    nt = T // tm
    nw = 2 * D_MODEL // N_SHARD

    def body(x_ref, nmix_ref, gv_ref, ws_ref, bsb_ref, w_in_hbm, w_out_hbm,
             h1_ref, zp_ref, w_in_v, w_out_v, gated_v, sem):
        _load_once([(w_in_hbm, w_in_v), (w_out_hbm, w_out_v)], sem)
        xv = x_ref[...]
        xn = _rms(xv, nmix_ref[...])[0].astype(BF16)
        for j in range(N_SHARD):
            zp_ref[:, j * nw:(j + 1) * nw] = _dot(xn, w_in_v[j])
        z = _gelu(zp_ref[...])
        u = z[:, :D_MODEL]
        vn = _rms(z[:, D_MODEL:], gv_ref[...])[0].astype(BF16)
        for c in range(tm // CHUNK):
            rows = slice(c * CHUNK, (c + 1) * CHUNK)
            for h in range(A_GROUPS):
                cols = slice(h * CHUNK, (h + 1) * CHUNK)
                s = _dot(ws_ref[h], vn[rows, cols]) + bsb_ref[h]
                gated_v[rows, cols] = (u[rows, cols] * s).astype(BF16)
        h1_ref[...] = xv + _dot(gated_v[...], w_out_v[...])

    return pl.pallas_call(
        body, name="mixer_a_fwd", grid=(nt,),
        in_specs=[_row_spec(tm, D_MODEL), _const_spec((1, D_MODEL)), _const_spec((1, D_MODEL)),
                  _const_spec((A_GROUPS, CHUNK, CHUNK)), _const_spec((A_GROUPS, CHUNK, CHUNK)), ANY, ANY],
        out_specs=[_row_spec(tm, D_MODEL), _row_spec(tm, 2 * D_MODEL)],
        out_shape=[jax.ShapeDtypeStruct((T, D_MODEL), F32), jax.ShapeDtypeStruct((T, 2 * D_MODEL), F32)],
        scratch_shapes=[pltpu.VMEM((N_SHARD, D_MODEL, nw), BF16), pltpu.VMEM((D_MODEL, D_MODEL), BF16),
                        pltpu.VMEM((tm, D_MODEL), BF16), pltpu.SemaphoreType.DMA((2,))],
        compiler_params=_params(),
    )(x, nmix, gv, wsm, bsb, w_in, w_out)


def _mixer_a_bwd(dh, x, zp, nmix, gv, wsm, bsb, tril, w_in, w_out, deps=()):
    T = x.shape[0]
    tm = min(256, T)
    nt = T // tm
    nw = 2 * D_MODEL // N_SHARD

    def body(dh_ref, x_ref, zp_ref, nmix_ref, gv_ref, ws_ref, bsb_ref, tril_ref, w_in_hbm, w_out_hbm,
             dx_ref, dwin_ref, dwout_ref, dws_ref, dbs_ref, dgv_ref, dnmix_ref,
             w_in_v, w_out_v, du_v, dvn_v, dbs_v, gated_ref, sem):
        _load_once([(w_in_hbm, w_in_v), (w_out_hbm, w_out_v)], sem)
        _zero_first([dws_ref, dbs_v, dgv_ref, dnmix_ref, dwin_ref, dwout_ref])
        i = pl.program_id(0)
        dhv = dh_ref[...]
        dhb = dhv.astype(BF16)
        xv = x_ref[...]
        xn, xh, r = _rms(xv, nmix_ref[...])
        xnb = xn.astype(BF16)
        zpv = zp_ref[...]
        z = _gelu(zpv)
        u = z[:, :D_MODEL]
        vn_f, vh, rv = _rms(z[:, D_MODEL:], gv_ref[...])
        vn = vn_f.astype(BF16)
        dgated = _dot_nt(dhb, w_out_v[...])
        for c in range(tm // CHUNK):
            rows = slice(c * CHUNK, (c + 1) * CHUNK)
            for h in range(A_GROUPS):
                cols = slice(h * CHUNK, (h + 1) * CHUNK)
                vn_h = vn[rows, cols]
                s = _dot(ws_ref[h], vn_h) + bsb_ref[h]
                dgt = dgated[rows, cols]
                u_h = u[rows, cols]
                gated_ref[rows, cols] = (u_h * s).astype(BF16)
                du_v[rows, cols] = dgt * s
                ds = dgt * u_h
                dsb = ds.astype(BF16)
                dws_ref[h] += _dot_nt(dsb, vn_h)
                dbs_v[h] += ds
                dvn_v[rows, cols] = _dot_tn(ws_ref[h], dsb)
        dwout_ref[...] += _dot_tn(gated_ref[...], dhb)
        dv, dgv = _rms_bwd(dvn_v[...], vh, rv, gv_ref[...])
        dgv_ref[...] += dgv
        dzu = (du_v[...] * _gelu_grad(zpv[:, :D_MODEL])).astype(BF16)
        dzv = (dv * _gelu_grad(zpv[:, D_MODEL:])).astype(BF16)
        dzs = (dzu[:, :nw], dzu[:, nw:], dzv[:, :nw], dzv[:, nw:])
        dxn = jnp.zeros((tm, D_MODEL), F32)
        for j in range(N_SHARD):
            dxn = dxn + _dot_nt(dzs[j], w_in_v[j])
            dwin_ref[j] += _dot_tn(xnb, dzs[j])
        dxx, dn = _rms_bwd(dxn, xh, r, nmix_ref[...])
        dnmix_ref[...] += dn
        dx_ref[...] = dhv + dxx

        @pl.when(i == nt - 1)
        def _():
            for h in range(A_GROUPS):
                dws_ref[h] = dws_ref[h] * tril_ref[...]
                dbs_ref[h] = jnp.broadcast_to(jnp.sum(dbs_v[h], axis=1, keepdims=True), (CHUNK, CHUNK))

    grp = (A_GROUPS, CHUNK, CHUNK)
    body, in_specs, args = _add_deps(
        body, [_row_spec(tm, D_MODEL), _row_spec(tm, D_MODEL), _row_spec(tm, 2 * D_MODEL),
               _const_spec((1, D_MODEL)), _const_spec((1, D_MODEL)), _const_spec(grp), _const_spec(grp),
               _const_spec((CHUNK, CHUNK)), ANY, ANY],
        [dh, x, zp, nmix, gv, wsm, bsb, tril, w_in, w_out], deps)
    return pl.pallas_call(
        body, name="mixer_a_bwd", grid=(nt,), in_specs=in_specs,
        out_specs=[_row_spec(tm, D_MODEL), _const_spec((N_SHARD, D_MODEL, nw)), _const_spec((D_MODEL, D_MODEL)),
                   _const_spec(grp), _const_spec(grp), _const_spec((1, D_MODEL)), _const_spec((1, D_MODEL))],
        out_shape=[jax.ShapeDtypeStruct((T, D_MODEL), F32), jax.ShapeDtypeStruct((N_SHARD, D_MODEL, nw), F32),
                   jax.ShapeDtypeStruct((D_MODEL, D_MODEL), F32),
                   jax.ShapeDtypeStruct(grp, F32), jax.ShapeDtypeStruct(grp, F32),
                   jax.ShapeDtypeStruct((1, D_MODEL), F32), jax.ShapeDtypeStruct((1, D_MODEL), F32)],
        scratch_shapes=[pltpu.VMEM((N_SHARD, D_MODEL, nw), BF16), pltpu.VMEM((D_MODEL, D_MODEL), BF16),
                        pltpu.VMEM((tm, D_MODEL), F32), pltpu.VMEM((tm, D_MODEL), F32),
                        pltpu.VMEM(grp, F32), pltpu.VMEM((tm, D_MODEL), BF16), pltpu.SemaphoreType.DMA((2,))],
        compiler_params=_params(),
    )(*args)


def _load_ffn_weights(w_up_hbm, w_dn_hbm, layer, w_up_v, w_dn_v, sem):
    _load_once([(w_up_hbm, w_up_v), (w_dn_hbm, w_dn_v)], sem)


def _ffn_fwd(h, nffn, cw, cb, w_up, w_dn, layer):
    T = h.shape[0]
    tm = min(256, T)
    nt = T // tm

    def body(h_ref, n_ref, cw_ref, cb_ref, w_up_hbm, w_dn_hbm, out_ref, hh_ref, c_ref,
             w_up_v, w_dn_v, carry_v, sem):
        _load_ffn_weights(w_up_hbm, w_dn_hbm, layer, w_up_v, w_dn_v, sem)
        _zero_first([carry_v])
        xv = h_ref[...]
        xf = _rms(xv, n_ref[...])[0].astype(BF16)
        acc = xv
        for j in range(2):
            cs = []
            for blk in (j, j + 2):
                cols = slice(blk * FF_BLK, (blk + 1) * FF_BLK)
                hh = _dot(xf, w_up_v[blk])
                hh_ref[:, cols] = hh.astype(BF16)
                ext = jnp.concatenate([carry_v[blk], hh], axis=0)
                carry_v[blk] = hh[tm - 8:, :]
                s1 = pltpu.roll(ext, 1, 0)[8:]
                s2 = pltpu.roll(ext, 2, 0)[8:]
                cv = (cb_ref[:, cols] + cw_ref[0:1, cols] * s2 + cw_ref[1:2, cols] * s1
                      + cw_ref[2:3, cols] * hh)
                c_ref[:, cols] = cv.astype(BF16)
                cs.append(cv)
            act = (cs[0] * _sigmoid(cs[0]) * cs[1]).astype(BF16)
            acc = acc + _dot(act, w_dn_v[j * FF_BLK:(j + 1) * FF_BLK, :])
        out_ref[...] = acc

    return pl.pallas_call(
        body, name=f"ffn_fwd{layer}", grid=(nt,),
        in_specs=[_row_spec(tm, D_MODEL), _const_spec((1, D_MODEL)), _const_spec((3, N_FF)),
                  _const_spec((1, N_FF)), ANY, ANY],
        out_specs=[_row_spec(tm, D_MODEL), _row_spec(tm, N_FF), _row_spec(tm, N_FF)],
        out_shape=[jax.ShapeDtypeStruct((T, D_MODEL), F32), jax.ShapeDtypeStruct((T, N_FF), BF16),
                   jax.ShapeDtypeStruct((T, N_FF), BF16)],
        scratch_shapes=[pltpu.VMEM((N_SHARD, D_MODEL, FF_BLK), BF16), pltpu.VMEM((D_FF, D_MODEL), BF16),
                        pltpu.VMEM((N_SHARD, 8, FF_BLK), F32), pltpu.SemaphoreType.DMA((2 * N_SHARD,))],
        compiler_params=_params(),
    )(h, nffn, cw, cb, w_up, w_dn)


def _wgrad(a, b, bn, col_sharded, name, deps=()):
    T, K = a.shape
    N = b.shape[1]
    tt = min(2048, T)
    nn, ntt = N // bn, T // tt
    kr = K // N_SHARD

    def body(a_ref, b_ref, o_ref):
        @pl.when(pl.program_id(1) == 0)
        def _():
            o_ref[...] = jnp.zeros(o_ref.shape, F32)
        d = _dot_tn(a_ref[...].astype(BF16), b_ref[...].astype(BF16))
        if col_sharded:
            o_ref[...] += d
        else:
            for j in range(N_SHARD):
                o_ref[j] += d[j * kr:(j + 1) * kr]

    if col_sharded:
        assert nn == N_SHARD
        out_spec = pl.BlockSpec((None, K, bn), lambda n, t: (n, 0, 0))
        out_shape = jax.ShapeDtypeStruct((N_SHARD, K, bn), F32)
    else:
        out_spec = pl.BlockSpec((N_SHARD, kr, bn), lambda n, t: (0, 0, n))
        out_shape = jax.ShapeDtypeStruct((N_SHARD, kr, N), F32)
    body, in_specs, args = _add_deps(
        body, [pl.BlockSpec((tt, K), lambda n, t: (t, 0)), pl.BlockSpec((tt, bn), lambda n, t: (t, n))],
        [a, b], deps)
    return pl.pallas_call(
        body, name=name, grid=(nn, ntt), in_specs=in_specs, out_specs=out_spec, out_shape=out_shape,
        compiler_params=pltpu.CompilerParams(dimension_semantics=("arbitrary",) * 2, vmem_limit_bytes=VMEM_LIMIT),
    )(*args)


def _ffn_bwd(dh, h, hh, c, nffn, cw, w_up, w_dn, layer, deps=(), between=None):
    T = h.shape[0]
    tm = min(256, T)
    nt = T // tm

    def body(dh_ref, h_ref, hh_ref, c_ref, n_ref, cw_ref, w_up_hbm, w_dn_hbm,
             dhin_ref, act_ref, dhh_ref, xf_ref, dcw_ref, dcb_ref, dn_ref,
             w_up_v, w_dn_v, carry_v, sem):
        _load_ffn_weights(w_up_hbm, w_dn_hbm, layer, w_up_v, w_dn_v, sem)
        _zero_first([carry_v, dcw_ref, dcb_ref, dn_ref])
        dout = dh_ref[...]
        doutb = dout.astype(BF16)
        xf_f, xh, r = _rms(h_ref[...], n_ref[...])
        xf_ref[...] = xf_f.astype(BF16)
        dxf = jnp.zeros((tm, D_MODEL), F32)
        for j in range(2):
            blks = (j, j + 2)
            cg = c_ref[:, j * FF_BLK:(j + 1) * FF_BLK].astype(F32)
            cu = c_ref[:, (j + 2) * FF_BLK:(j + 3) * FF_BLK].astype(F32)
            sg = _sigmoid(cg)
            sil = cg * sg
            act_ref[:, j * FF_BLK:(j + 1) * FF_BLK] = (sil * cu).astype(BF16)
            dact = _dot_nt(doutb, w_dn_v[j * FF_BLK:(j + 1) * FF_BLK, :])
            dcs = (dact * cu * (sg * (1.0 + cg * (1.0 - sg))), dact * sil)
            for blk, dc in zip(blks, dcs):
                cols = slice(blk * FF_BLK, (blk + 1) * FF_BLK)
                hhv = hh_ref[:, cols].astype(F32)
                ext = jnp.concatenate([dc, carry_v[blk]], axis=0)
                carry_v[blk] = dc[:8, :]
                n = tm + 8
                a1 = pltpu.roll(ext, n - 1, 0)[:tm]
                a2 = pltpu.roll(ext, n - 2, 0)[:tm]
                dcb_ref[:, cols] += jnp.sum(dc, axis=0, keepdims=True)
                dcw_ref[0:1, cols] += jnp.sum(a2 * hhv, axis=0, keepdims=True)
                dcw_ref[1:2, cols] += jnp.sum(a1 * hhv, axis=0, keepdims=True)
                dcw_ref[2:3, cols] += jnp.sum(dc * hhv, axis=0, keepdims=True)
                dhh = (cw_ref[2:3, cols] * dc + cw_ref[1:2, cols] * a1 + cw_ref[0:1, cols] * a2).astype(BF16)
                dhh_ref[:, cols] = dhh
                dxf = dxf + _dot_nt(dhh, w_up_v[blk])
        dxx, dn = _rms_bwd(dxf, xh, r, n_ref[...])
        dn_ref[...] += dn
        dhin_ref[...] = dout + dxx

    rev = functools.partial(_row_spec, rev_nt=nt)
    body, in_specs, args = _add_deps(
        body, [rev(tm, D_MODEL), rev(tm, D_MODEL), rev(tm, N_FF), rev(tm, N_FF),
               _const_spec((1, D_MODEL)), _const_spec((3, N_FF)), ANY, ANY],
        [dh, h, hh, c, nffn, cw, w_up, w_dn], deps)
    dhin, act, dhh, xf, dcw, dcb, dn = pl.pallas_call(
        body, name=f"ffn_bwd{layer}", grid=(nt,), in_specs=in_specs,
        out_specs=[rev(tm, D_MODEL), rev(tm, D_FF), rev(tm, N_FF), rev(tm, D_MODEL),
                   _const_spec((3, N_FF)), _const_spec((1, N_FF)), _const_spec((1, D_MODEL))],
        out_shape=[jax.ShapeDtypeStruct((T, D_MODEL), F32), jax.ShapeDtypeStruct((T, D_FF), BF16),
                   jax.ShapeDtypeStruct((T, N_FF), BF16), jax.ShapeDtypeStruct((T, D_MODEL), BF16),
                   jax.ShapeDtypeStruct((3, N_FF), F32), jax.ShapeDtypeStruct((1, N_FF), F32),
                   jax.ShapeDtypeStruct((1, D_MODEL), F32)],
        scratch_shapes=[pltpu.VMEM((N_SHARD, D_MODEL, FF_BLK), BF16), pltpu.VMEM((D_FF, D_MODEL), BF16),
                        pltpu.VMEM((N_SHARD, 8, FF_BLK), F32), pltpu.SemaphoreType.DMA((2 * N_SHARD,))],
        compiler_params=_params(),
    )(*args)
    deps2 = between(dhin) if between is not None else ()
    dwdn = _wgrad(act, dh, D_MODEL // 2, False, f"wgrad_ffn_down{layer}", deps=deps2)
    dwup = _wgrad(xf, dhh, FF_BLK, True, f"wgrad_ffn_up{layer}", deps=deps2)
    return dhin, dwup, dwdn, dcw, dcb, dn


def _load_ple_weights(w_pin_hbm, w_gate_hbm, layer, w_pin_v, w_gate_v, sem, extra=()):
    _load_once([(w_pin_hbm, w_pin_v), (w_gate_hbm, w_gate_v)] + list(extra), sem)


def _p_spec(tm, layer):
    return pl.BlockSpec((None, tm, PLE_DIM), lambda i: (layer, i, 0))


def _ple_terms(xv, p_ref, n_ref, bg_ref, w_pin_v, w_gate_v, pe_v):
    pw = D_MODEL // N_SHARD
    xg, xh, r = _rms(xv, n_ref[...])
    xgb = xg.astype(BF16)
    gate = _sigmoid(_dot(xgb, w_gate_v[...]) + bg_ref[...])
    pb = p_ref[...].astype(BF16)
    for j in range(N_SHARD):
        pe_v[:, j * pw:(j + 1) * pw] = _dot(pb, w_pin_v[j])
    pe = pe_v[...]
    return pe * gate, pe, gate, xgb, xh, r


def _ple_fwd_kv(h, p, nple, bg, nkv, w_pin, w_gate, w_kv):
    T = h.shape[0]
    tm = min(512, T)
    nt = T // tm
    pw = D_MODEL // N_SHARD

    def body(h_ref, p_ref, n_ref, bg_ref, nkv_ref, w_pin_hbm, w_gate_hbm, w_kv_hbm,
             out_ref, kv_ref, w_pin_v, w_gate_v, w_kv_v, pe_v, sem):
        _load_ple_weights(w_pin_hbm, w_gate_hbm, 0, w_pin_v, w_gate_v, sem, [(w_kv_hbm, w_kv_v)])
        xv = h_ref[...]
        hn = xv + _ple_terms(xv, p_ref, n_ref, bg_ref, w_pin_v, w_gate_v, pe_v)[0]
        out_ref[...] = hn
        kvn = _rms(hn, nkv_ref[...])[0].astype(BF16)
        kv_ref[...] = _dot(kvn, w_kv_v[...]).astype(BF16)

    vec = _const_spec((1, D_MODEL))
    return pl.pallas_call(
        body, name="ple_fwd0", grid=(nt,),
        in_specs=[_row_spec(tm, D_MODEL), _p_spec(tm, 0), vec, vec, vec, ANY, ANY, ANY],
        out_specs=[_row_spec(tm, D_MODEL), _row_spec(tm, 2 * KV_DIM)],
        out_shape=[jax.ShapeDtypeStruct((T, D_MODEL), F32), jax.ShapeDtypeStruct((T, 2 * KV_DIM), BF16)],
        scratch_shapes=[pltpu.VMEM((N_SHARD, PLE_DIM, pw), BF16), pltpu.VMEM((D_MODEL, D_MODEL), BF16),
                        pltpu.VMEM((D_MODEL, 2 * KV_DIM), BF16), pltpu.VMEM((tm, D_MODEL), F32),
                        pltpu.SemaphoreType.DMA((2 * N_SHARD + 1,))],
        compiler_params=_params(),
    )(h, p, nple, bg, nkv, w_pin, w_gate, w_kv)


def _ple_fwd_final(h, p, tgt, nple, bg, nfin, w_pin, w_gate):
    T = h.shape[0]
    tm = min(512, T)
    nt = T // tm
    pw = D_MODEL // N_SHARD

    def body(h_ref, p_ref, t_ref, n_ref, bg_ref, nf_ref, w_pin_hbm, w_gate_hbm,
             dh_ref, loss_ref, dnf_ref, w_pin_v, w_gate_v, pe_v, sem):
        _load_ple_weights(w_pin_hbm, w_gate_hbm, 1, w_pin_v, w_gate_v, sem)
        _zero_first([loss_ref, dnf_ref])
        xv = h_ref[...]
        hn = xv + _ple_terms(xv, p_ref, n_ref, bg_ref, w_pin_v, w_gate_v, pe_v)[0]
        y, yh, r = _rms(hn, nf_ref[...])
        diff = y - t_ref[...]
        loss_ref[...] += 0.5 * jnp.sum(jnp.mean(diff * diff, axis=-1, keepdims=True))
        dy = diff * (1.0 / D_MODEL)
        dhn, dnf = _rms_bwd(dy, yh, r, nf_ref[...])
        dnf_ref[...] += dnf
        dh_ref[...] = dhn

    vec = _const_spec((1, D_MODEL))
    return pl.pallas_call(
        body, name="ple_fwd1", grid=(nt,),
        in_specs=[_row_spec(tm, D_MODEL), _p_spec(tm, 1), _row_spec(tm, D_MODEL), vec, vec, vec, ANY, ANY],
        out_specs=[_row_spec(tm, D_MODEL), _const_spec((8, 128)), vec],
        out_shape=[jax.ShapeDtypeStruct((T, D_MODEL), F32), jax.ShapeDtypeStruct((8, 128), F32),
                   jax.ShapeDtypeStruct((1, D_MODEL), F32)],
        scratch_shapes=[pltpu.VMEM((N_SHARD, PLE_DIM, pw), BF16), pltpu.VMEM((D_MODEL, D_MODEL), BF16),
                        pltpu.VMEM((tm, D_MODEL), F32), pltpu.SemaphoreType.DMA((2 * N_SHARD,))],
        compiler_params=_params(),
    )(h, p, tgt, nple, bg, nfin, w_pin, w_gate)


def _ple_bwd(dh, hb, p, nple, bg, w_pin, w_gate, layer, kv_args=None):
    T = hb.shape[0]
    tm = min(512, T)
    nt = T // tm
    with_kv = kv_args is not None
    pw = D_MODEL // N_SHARD

    def body(*refs):
        if with_kv:
            (dh_ref, hb_ref, p_ref, n_ref, bg_ref, w_pin_hbm, w_gate_hbm, hc_ref, dkv_ref, nkv_ref, w_kv_hbm,
             dhb_ref, dwpin_ref, dwgate_ref, dbg_ref, dn_ref, dwkv_ref, dnkv_ref,
             w_pin_v, w_gate_v, pe_v, w_kv_v, sem) = refs
        else:
            (dh_ref, hb_ref, p_ref, n_ref, bg_ref, w_pin_hbm, w_gate_hbm,
             dhb_ref, dwpin_ref, dwgate_ref, dbg_ref, dn_ref, w_pin_v, w_gate_v, pe_v, sem) = refs
        pairs = [(w_pin_hbm, w_pin_v), (w_gate_hbm, w_gate_v)]
        if with_kv:
            pairs.append((w_kv_hbm, w_kv_v))
        _load_once(pairs, sem)
        _zero_first([dwpin_ref, dwgate_ref, dbg_ref, dn_ref] + ([dwkv_ref, dnkv_ref] if with_kv else []))
        do = dh_ref[...]
        if with_kv:
            dkvb = dkv_ref[...].astype(BF16)
            dkvn = _dot_nt(dkvb, w_kv_v[...])
            kvn, kh, kr = _rms(hc_ref[...], nkv_ref[...])
            dwkv_ref[...] += _dot_tn(kvn.astype(BF16), dkvb)
            dk, dnkv = _rms_bwd(dkvn, kh, kr, nkv_ref[...])
            dnkv_ref[...] += dnkv
            do = do + dk
        _, pe, gate, xgb, xh, r = _ple_terms(hb_ref[...], p_ref, n_ref, bg_ref, w_pin_v, w_gate_v, pe_v)
        dpe = (do * gate).astype(BF16)
        pb = p_ref[...].astype(BF16)
        for j in range(N_SHARD):
            dwpin_ref[j] += _dot_tn(pb, dpe[:, j * pw:(j + 1) * pw])
        da = do * pe * (gate * (1.0 - gate))
        dab = da.astype(BF16)
        dbg_ref[...] += jnp.sum(da, axis=0, keepdims=True)
        dxg = _dot_nt(dab, w_gate_v[...])
        dwgate_ref[...] += _dot_tn(xgb, dab)
        dxx, dn = _rms_bwd(dxg, xh, r, n_ref[...])
        dn_ref[...] += dn
        dhb_ref[...] = do + dxx

    vec = _const_spec((1, D_MODEL))
    row = _row_spec(tm, D_MODEL)
    in_specs = [row, row, _p_spec(tm, layer), vec, vec, ANY, ANY]
    args = [dh, hb, p, nple, bg, w_pin, w_gate]
    out_specs = [row, _const_spec((N_SHARD, PLE_DIM, pw)), _const_spec((D_MODEL, D_MODEL)), vec, vec]
    out_shape = [jax.ShapeDtypeStruct((T, D_MODEL), F32), jax.ShapeDtypeStruct((N_SHARD, PLE_DIM, pw), F32),
                 jax.ShapeDtypeStruct((D_MODEL, D_MODEL), F32),
                 jax.ShapeDtypeStruct((1, D_MODEL), F32), jax.ShapeDtypeStruct((1, D_MODEL), F32)]
    scratch = [pltpu.VMEM((N_SHARD, PLE_DIM, pw), BF16), pltpu.VMEM((D_MODEL, D_MODEL), BF16),
               pltpu.VMEM((tm, D_MODEL), F32)]
    if with_kv:
        hc, dkv, nkv, w_kv = kv_args
        in_specs += [row, _row_spec(tm, 2 * KV_DIM), vec, ANY]
        args += [hc, dkv, nkv, w_kv]
        out_specs += [_const_spec((D_MODEL, 2 * KV_DIM)), vec]
        out_shape += [jax.ShapeDtypeStruct((D_MODEL, 2 * KV_DIM), F32), jax.ShapeDtypeStruct((1, D_MODEL), F32)]
        scratch.append(pltpu.VMEM((D_MODEL, 2 * KV_DIM), BF16))
    scratch.append(pltpu.SemaphoreType.DMA((3,)))
    return pl.pallas_call(
        body, name=f"ple_bwd{layer}", grid=(nt,), in_specs=in_specs, out_specs=out_specs,
        out_shape=out_shape, scratch_shapes=scratch, compiler_params=_params(),
    )(*args)


GROUP_ROWS = GQA_GROUP * BLOCK


def _stack_heads(x, kh):
    return jnp.concatenate([x[:, (kh * GQA_GROUP + g) * HEAD_DIM:(kh * GQA_GROUP + g + 1) * HEAD_DIM]
                            for g in range(GQA_GROUP)], axis=0)


def _attn_fwd(h, nmix, kv, sinks, w_q, w_o):
    T = h.shape[0]
    tm = min(512, T)
    nt = T // tm
    nb = tm // BLOCK

    def body(h_ref, n_ref, kv_ref, kvp_ref, sink_ref, w_q_hbm, w_o_hbm,
             out_ref, q_ref, ao_ref, p_ref, psink_ref, w_q_v, w_o_v, kvs_v, sem):
        _load_once([(w_q_hbm, w_q_v), (w_o_hbm, w_o_v)], sem)
        ti = pl.program_id(0)
        xv = h_ref[...]
        xn = _rms(xv, n_ref[...])[0].astype(BF16)
        q_ref[...] = (_dot(xn, w_q_v[...]) * (HEAD_DIM ** -0.5)).astype(BF16)
        kvs_v[0:BLOCK, :] = kvp_ref[...]
        kvs_v[BLOCK:, :] = kv_ref[...]
        lane = lax.broadcasted_iota(jnp.int32, (BLOCK, 128), 1)
        ii = lax.broadcasted_iota(jnp.int32, (BLOCK, 2 * BLOCK), 0)
        jj = lax.broadcasted_iota(jnp.int32, (BLOCK, 2 * BLOCK), 1)
        dist = ii + BLOCK - jj
        inband = (dist >= 0) & (dist < BLOCK)
        distf = dist.astype(F32)

        def blk_body(b, carry):
            r0 = pl.multiple_of(b * BLOCK, BLOCK)
            valid = inband & ((jj >= BLOCK) | jnp.logical_not(jnp.logical_and(ti == 0, b == 0)))
            qb = q_ref[pl.ds(r0, BLOCK), :]
            band = kvs_v[pl.ds(r0, 2 * BLOCK), :]
            psink_mat = jnp.zeros((BLOCK, 128), F32)
            outs = []
            for hq in range(N_Q_HEADS):
                kh, g = divmod(hq, GQA_GROUP)
                k_h = band[:, kh * HEAD_DIM:(kh + 1) * HEAD_DIM]
                v_h = band[:, KV_DIM + kh * HEAD_DIM:KV_DIM + (kh + 1) * HEAD_DIM]
                s = _dot_nt(qb[:, hq * HEAD_DIM:(hq + 1) * HEAD_DIM], k_h) - _SLOPES[hq] * distf
                s = jnp.where(valid, s, NEG)
                sink = sink_ref[hq]
                m = jnp.maximum(jnp.max(s, axis=1, keepdims=True), sink)
                e = jnp.exp(s - m)
                esink = jnp.exp(sink - m)
                inv = 1.0 / (jnp.sum(e, axis=1, keepdims=True) + esink)
                pb = (e * inv).astype(BF16)
                p_ref[b, kh, g * BLOCK:(g + 1) * BLOCK, :] = pb
                outs.append(_dot(pb, v_h))
                psink_mat = jnp.where(lane == hq, esink * inv, psink_mat)
            ao_ref[pl.ds(r0, BLOCK), :] = jnp.concatenate(outs, axis=1).astype(BF16)
            psink_ref[pl.ds(r0, BLOCK), :] = psink_mat
            return carry

        lax.fori_loop(0, nb, blk_body, 0)
        out_ref[...] = xv + _dot(ao_ref[...], w_o_v[...])

    row = _row_spec(tm, D_MODEL)
    prev_spec = pl.BlockSpec((BLOCK, 2 * KV_DIM), lambda i: (jnp.maximum(i * nb - 1, 0), 0))
    return pl.pallas_call(
        body, name="attn_fwd", grid=(nt,),
        in_specs=[row, _const_spec((1, D_MODEL)), _row_spec(tm, 2 * KV_DIM), prev_spec, SMEM, ANY, ANY],
        out_specs=[row, row, row, pl.BlockSpec((nb, N_KV_HEADS, GROUP_ROWS, 2 * BLOCK), lambda i: (i, 0, 0, 0)),
                   _row_spec(tm, 128)],
        out_shape=[jax.ShapeDtypeStruct((T, D_MODEL), F32), jax.ShapeDtypeStruct((T, D_MODEL), BF16),
                   jax.ShapeDtypeStruct((T, D_MODEL), BF16),
                   jax.ShapeDtypeStruct((T // BLOCK, N_KV_HEADS, GROUP_ROWS, 2 * BLOCK), BF16),
                   jax.ShapeDtypeStruct((T, 128), F32)],
        scratch_shapes=[pltpu.VMEM((D_MODEL, D_MODEL), BF16), pltpu.VMEM((D_MODEL, D_MODEL), BF16),
                        pltpu.VMEM((tm + BLOCK, 2 * KV_DIM), BF16), pltpu.SemaphoreType.DMA((2,))],
        compiler_params=_params(),
    )(h, nmix, kv, kv, sinks, w_q, w_o)


def _attn_bwd(dh, h, q, kv, ao, p, psink, nmix, w_q, w_o):
    T = h.shape[0]
    tm = min(512, T)
    nt = T // tm
    nb = tm // BLOCK

    def body(dh_ref, h_ref, q_ref, kv_ref, kvp_ref, ao_ref, p_ref, psink_ref, n_ref, w_q_hbm, w_o_hbm,
             dhin_ref, dwq_ref, dwo_ref, dkv_ref, dsink_ref, dn_ref,
             w_q_v, w_o_v, kvs_v, dao_v, dq_v, dkv_v, carry_v, sem):
        _load_once([(w_q_hbm, w_q_v), (w_o_hbm, w_o_v)], sem)
        _zero_first([carry_v, dsink_ref, dn_ref, dwq_ref, dwo_ref])
        dout = dh_ref[...]
        doutb = dout.astype(BF16)
        dao_v[...] = _dot_nt(doutb, w_o_v[...])
        dwo_ref[...] += _dot_tn(ao_ref[...], doutb)
        kvs_v[0:BLOCK, :] = kvp_ref[...]
        kvs_v[BLOCK:, :] = kv_ref[...]
        dkv_v[0:tm, :] = jnp.zeros((tm, 2 * KV_DIM), F32)
        dkv_v[tm:, :] = carry_v[...]
        seg = (lax.broadcasted_iota(jnp.int32, (D_MODEL, 128), 0) // HEAD_DIM
               == lax.broadcasted_iota(jnp.int32, (D_MODEL, 128), 1)).astype(BF16)

        def blk_body(b, dsk):
            r0 = pl.multiple_of(b * BLOCK, BLOCK)
            qb = q_ref[pl.ds(r0, BLOCK), :]
            band = kvs_v[pl.ds(r0, 2 * BLOCK), :]
            aob = ao_ref[pl.ds(r0, BLOCK), :].astype(F32)
            daob = dao_v[pl.ds(r0, BLOCK), :]
            prod = daob * aob
            head = prod.astype(BF16)
            tail = (prod - head.astype(F32)).astype(BF16)
            dsk = dsk + psink_ref[pl.ds(r0, BLOCK), :] * (_dot(head, seg) + _dot(tail, seg))
            dqs = []
            dks = []
            dvs = []
            for kh in range(N_KV_HEADS):
                k_h = band[:, kh * HEAD_DIM:(kh + 1) * HEAD_DIM]
                v_h = band[:, KV_DIM + kh * HEAD_DIM:KV_DIM + (kh + 1) * HEAD_DIM]
                q_g = _stack_heads(qb, kh)
                dao_g = _stack_heads(daob, kh)
                prb = p_ref[b, kh]
                pr = prb.astype(F32)
                dd = jnp.sum(dao_g * _stack_heads(aob, kh), axis=1, keepdims=True)
                dao_gb = dao_g.astype(BF16)
                dp = _dot_nt(dao_gb, v_h)
                dsb = (pr * (dp - dd)).astype(BF16)
                dq_g = _dot(dsb, k_h) * (HEAD_DIM ** -0.5)
                dks.append(_dot_tn(dsb, q_g))
                dvs.append(_dot_tn(prb, dao_gb))
                for g in range(GQA_GROUP):
                    dqs.append(dq_g[g * BLOCK:(g + 1) * BLOCK])
            dq_v[pl.ds(r0, BLOCK), :] = jnp.concatenate(dqs, axis=1)
            dkv_v[pl.ds(r0, 2 * BLOCK), :] += jnp.concatenate(dks + dvs, axis=1)
            return dsk

        dsk = lax.fori_loop(0, nb, blk_body, jnp.zeros((BLOCK, 128), F32))
        dsink_ref[...] -= jnp.sum(dsk, axis=0, keepdims=True)
        dqb = dq_v[...].astype(BF16)
        dxn = _dot_nt(dqb, w_q_v[...])
        xn, xh, r = _rms(h_ref[...], n_ref[...])
        dwq_ref[...] += _dot_tn(xn.astype(BF16), dqb)
        dxx, dn = _rms_bwd(dxn, xh, r, n_ref[...])
        dn_ref[...] += dn
        dhin_ref[...] = dout + dxx
        dkv_ref[...] = dkv_v[BLOCK:, :]
        carry_v[...] = dkv_v[0:BLOCK, :]

    rev = functools.partial(_row_spec, rev_nt=nt)
    row = rev(tm, D_MODEL)
    prev_spec = pl.BlockSpec((BLOCK, 2 * KV_DIM), lambda i: (jnp.maximum((nt - 1 - i) * nb - 1, 0), 0))
    return pl.pallas_call(
        body, name="attn_bwd", grid=(nt,),
        in_specs=[row, row, row, rev(tm, 2 * KV_DIM), prev_spec, row,
                  pl.BlockSpec((nb, N_KV_HEADS, GROUP_ROWS, 2 * BLOCK), lambda i: (nt - 1 - i, 0, 0, 0)),
                  rev(tm, 128), _const_spec((1, D_MODEL)), ANY, ANY],
        out_specs=[row, _const_spec((D_MODEL, D_MODEL)), _const_spec((D_MODEL, D_MODEL)), rev(tm, 2 * KV_DIM),
                   _const_spec((8, 128)), _const_spec((1, D_MODEL))],
        out_shape=[jax.ShapeDtypeStruct((T, D_MODEL), F32), jax.ShapeDtypeStruct((D_MODEL, D_MODEL), F32),
                   jax.ShapeDtypeStruct((D_MODEL, D_MODEL), F32), jax.ShapeDtypeStruct((T, 2 * KV_DIM), F32),
                   jax.ShapeDtypeStruct((8, 128), F32), jax.ShapeDtypeStruct((1, D_MODEL), F32)],
        scratch_shapes=[pltpu.VMEM((D_MODEL, D_MODEL), BF16), pltpu.VMEM((D_MODEL, D_MODEL), BF16),
                        pltpu.VMEM((tm + BLOCK, 2 * KV_DIM), BF16), pltpu.VMEM((tm, D_MODEL), F32),
                        pltpu.VMEM((tm, D_MODEL), F32), pltpu.VMEM((tm + BLOCK, 2 * KV_DIM), F32),
                        pltpu.VMEM((BLOCK, 2 * KV_DIM), F32), pltpu.SemaphoreType.DMA((2,))],
        compiler_params=_params(),
    )(dh, h, q, kv, kv, ao, p, psink, nmix, w_q, w_o)


def _mesh_pos():
    return lax.axis_index("x"), lax.axis_index("y"), lax.axis_index("c")


def _other_chips(x, y):
    return [(1 - x, y), (x, 1 - y), (1 - x, 1 - y)]


HBM_SPEC = pl.BlockSpec(memory_space=pltpu.HBM)
SEM_SPEC = pl.BlockSpec(memory_space=pltpu.SEMAPHORE)


def _split_call(name, bufs, waits=(), starts=(), after=()):
    n, nw, ns, na = len(bufs), len(waits), len(starts), len(after)

    def body(*refs):
        brefs = refs[:n]
        wsems = [(refs[n + 2 * k], refs[n + 2 * k + 1]) for k in range(nw)]
        o = n + 2 * nw + na
        ssems = [(refs[o + 2 * k], refs[o + 2 * k + 1]) for k in range(ns)]
        for (ss, rs), (_, _, fn) in zip(wsems, waits):
            for sending, arriving in fn(brefs, ss, rs):
                sending.wait_send()
                arriving.wait_recv()
        for (ss, rs), (_, fn) in zip(ssems, starts):
            for sending, _ in fn(brefs, ss, rs):
                sending.start()
        if ns:
            token = refs[o + 2 * ns + n]
            token[...] = jnp.zeros(token.shape, token.dtype)

    out_shape, out_specs = [], []
    for cnt, _ in starts:
        out_shape += [pltpu.SemaphoreType.DMA((cnt,)), pltpu.SemaphoreType.DMA((cnt,))]
        out_specs += [SEM_SPEC, SEM_SPEC]
    out_shape += [pltpu.HBM(b.shape, b.dtype) for b in bufs]
    out_specs += [HBM_SPEC] * n
    if ns:
        out_shape.append(jax.ShapeDtypeStruct((8, 128), F32))
        out_specs.append(pl.BlockSpec(memory_space=pltpu.VMEM))
    args = [pltpu.with_memory_space_constraint(b, pltpu.HBM) for b in bufs]
    for ss, rs, _ in waits:
        args += [ss, rs]
    args += list(after)
    res = pl.pallas_call(
        body, name=name, out_shape=tuple(out_shape),
        in_specs=[HBM_SPEC] * n + [SEM_SPEC] * (2 * nw) + [ANY] * na, out_specs=tuple(out_specs),
        input_output_aliases={i: 2 * ns + i for i in range(n)},
        compiler_params=pltpu.CompilerParams(has_side_effects=pltpu.SideEffectType.DATAFLOW_SIDE_EFFECTING),
    )(*args)
    sems = [(res[2 * k], res[2 * k + 1]) for k in range(ns)]
    return list(res[2 * ns:2 * ns + n]), sems, (res[2 * ns + n] if ns else None)


def _cast_place(items, name, deps=()):
    n = len(items)
    mats = [a.shape[-2:] for a, _, _ in items]

    def body(*refs):
        ins, outs, scr, sem = refs[:n], refs[n:2 * n], refs[2 * n:3 * n], refs[3 * n]
        x, y, _ = _mesh_pos()
        cps = []
        for t in range(n):
            scr[t][...] = ins[t][...].astype(scr[t].dtype)
            cp = pltpu.make_async_copy(scr[t], outs[t].at[2 * x + y], sem.at[t])
            cp.start()
            cps.append(cp)
        for cp in cps:
            cp.wait()

    def spec(idx, shape):
        return pl.BlockSpec((None,) * len(idx) + tuple(shape), lambda i: tuple(idx) + (0, 0))

    body, in_specs, args = _add_deps(body, [spec(idx, mat) for (_, idx, _), mat in zip(items, mats)],
                                     [a for a, _, _ in items], deps)
    return pl.pallas_call(
        body, name=name, grid=(1,), in_specs=in_specs, out_specs=[ANY] * n,
        out_shape=[jax.ShapeDtypeStruct((N_SHARD,) + tuple(mat), dt) for (_, _, dt), mat in zip(items, mats)],
        scratch_shapes=[pltpu.VMEM(tuple(mat), dt) for (_, _, dt), mat in zip(items, mats)]
        + [pltpu.SemaphoreType.DMA((n,))],
        compiler_params=_params(),
    )(*args)


def _gather_ici(idx):
    def fn(bufs, ss, rs):
        x, y, c = _mesh_pos()
        pairs = []
        for k, t in enumerate(idx):
            half = bufs[t].shape[1] // 2
            mine = bufs[t].at[2 * x + y, pl.ds(c * half, half), :]
            for j, (cx, cy) in enumerate(_other_chips(x, y)):
                theirs = bufs[t].at[2 * cx + cy, pl.ds(c * half, half), :]
                sem = dict(send_sem=ss.at[3 * k + j], recv_sem=rs.at[3 * k + j],
                           device_id=(cx, cy, c), device_id_type=MESH)
                pairs.append((pltpu.make_async_remote_copy(src_ref=mine, dst_ref=mine, **sem),
                              pltpu.make_async_remote_copy(src_ref=mine, dst_ref=theirs, **sem)))
        return pairs
    return fn


def _gather_d2d(idx):
    def fn(bufs, ss, rs):
        x, y, c = _mesh_pos()
        pairs = []
        for k, t in enumerate(idx):
            half = bufs[t].shape[1] // 2
            for j, (cx, cy) in enumerate(_other_chips(x, y)):
                got = bufs[t].at[2 * cx + cy, pl.ds(c * half, half), :]
                theirs = bufs[t].at[2 * cx + cy, pl.ds((1 - c) * half, half), :]
                sem = dict(send_sem=ss.at[3 * k + j], recv_sem=rs.at[3 * k + j],
                           device_id=(x, y, 1 - c), device_id_type=MESH)
                pairs.append((pltpu.make_async_remote_copy(src_ref=got, dst_ref=got, **sem),
                              pltpu.make_async_remote_copy(src_ref=got, dst_ref=theirs, **sem)))
        return pairs
    return fn


def _alloc(shapes, name):
    def body(*refs):
        pass

    return pl.pallas_call(body, name=name, out_specs=[ANY] * len(shapes),
                          out_shape=[jax.ShapeDtypeStruct(s, d) for s, d in shapes])()


def _send_to_sibling(n):
    def fn(bufs, ss, rs):
        x, y, c = _mesh_pos()
        pairs = []
        for t in range(n):
            src = bufs[t]
            if len(src.shape) == 3:
                half = src.shape[1] // 2
                src = src.at[:, pl.ds((1 - c) * half, half), :]
            cp = pltpu.make_async_remote_copy(src_ref=src, dst_ref=bufs[n + t], send_sem=ss.at[t],
                                              recv_sem=rs.at[t], device_id=(x, y, 1 - c), device_id_type=MESH)
            pairs.append((cp, cp))
        return pairs
    return fn


def _send_to_chips(n):
    def fn(bufs, ss, rs):
        x, y, c = _mesh_pos()
        pairs = []
        for j, (cx, cy) in enumerate(_other_chips(x, y)):
            for t in range(n):
                src = bufs[t].at[j] if len(bufs[t].shape) == 3 else bufs[t]
                cp = pltpu.make_async_remote_copy(src_ref=src, dst_ref=bufs[n + t].at[j], send_sem=ss.at[3 * t + j],
                                                  recv_sem=rs.at[3 * t + j], device_id=(cx, cy, c),
                                                  device_id_type=MESH)
                pairs.append((cp, cp))
        return pairs
    return fn


class _Exchange:
    def __init__(self, name, srcs, land_shapes, fn, n_sems):
        self.name, self.fn = name, fn
        lands = _alloc(land_shapes, name + "_alloc")
        self.n = len(srcs)
        self.bufs, sems, self.token = _split_call(name + "_start", list(srcs) + list(lands),
                                                  starts=[(n_sems, fn)])
        self.sems = sems[0]

    def finish(self, after=()):
        bufs, _, _ = _split_call(self.name + "_wait", self.bufs, waits=[(*self.sems, self.fn)], after=after)
        return bufs[:self.n], bufs[self.n:]


def _row_block(rows, cols, mult=8, limit=3 * 512 * 1024, itemsize=4):
    best = None
    for br in range(mult, rows + 1, mult):
        if rows % br == 0 and br * cols * itemsize <= limit:
            best = br
    assert best is not None, (rows, cols)
    return best


_GROUP_BLOCK_BYTES = 1024 * 1024


def _group_plan(ss):
    plan = []
    for s in ss:
        half, cols = s.shape[-2:]
        br = _row_block(half, cols, mult=16, limit=_GROUP_BLOCK_BYTES)
        plan.append((br, half // br))
    return plan, max(nr for _, nr in plan)


def _chip_partial(gs, ss, ids, name):
    n = len(gs)
    plan, steps = _group_plan(ss)

    def body(ids_ref, *refs):
        for t in range(n):
            refs[2 * n + t][...] = (refs[t][...] + refs[n + t][...]).astype(BF16)

    g_specs, s_specs, o_specs = [], [], []
    for (br, nr), s in zip(plan, ss):
        blk = (None, br, s.shape[2])
        g_specs.append(pl.BlockSpec(
            blk, lambda j, r, ids_ref, nr=nr: (ids_ref[2 + j], ids_ref[0] * nr + jnp.minimum(r, nr - 1), 0)))
        s_specs.append(pl.BlockSpec(blk, lambda j, r, ids_ref, nr=nr: (ids_ref[2 + j], jnp.minimum(r, nr - 1), 0)))
        o_specs.append(pl.BlockSpec(blk, lambda j, r, ids_ref, nr=nr: (j, jnp.minimum(r, nr - 1), 0)))
    return pl.pallas_call(
        body, name=name,
        grid_spec=pltpu.PrefetchScalarGridSpec(num_scalar_prefetch=1, grid=(3, steps),
                                               in_specs=g_specs + s_specs, out_specs=o_specs),
        out_shape=[jax.ShapeDtypeStruct((3,) + s.shape[1:], BF16) for s in ss],
        compiler_params=pltpu.CompilerParams(dimension_semantics=("arbitrary", "arbitrary"),
                                             vmem_limit_bytes=VMEM_LIMIT),
    )(ids, *gs, *ss)


def _chip_sum(gs, ss, qs, ids, name):
    n = len(gs)
    plan, steps = _group_plan(ss)

    def body(ids_ref, *refs):
        for t in range(n):
            q_ref = refs[2 * n + t]
            own = refs[t][...] + refs[n + t][...]
            refs[3 * n + t][...] = (own + q_ref[2].astype(F32)) + (q_ref[0].astype(F32) + q_ref[1].astype(F32))

    g_specs, s_specs, q_specs, o_specs = [], [], [], []
    for (br, nr), s in zip(plan, ss):
        cols = s.shape[2]
        g_specs.append(pl.BlockSpec(
            (None, br, cols), lambda r, ids_ref, nr=nr: (ids_ref[1], ids_ref[0] * nr + jnp.minimum(r, nr - 1), 0)))
        s_specs.append(pl.BlockSpec((None, br, cols), lambda r, ids_ref, nr=nr: (ids_ref[1], jnp.minimum(r, nr - 1), 0)))
        q_specs.append(pl.BlockSpec((3, br, cols), lambda r, ids_ref, nr=nr: (0, jnp.minimum(r, nr - 1), 0)))
        o_specs.append(pl.BlockSpec((br, cols), lambda r, ids_ref, nr=nr: (jnp.minimum(r, nr - 1), 0)))
    return pl.pallas_call(
        body, name=name,
        grid_spec=pltpu.PrefetchScalarGridSpec(num_scalar_prefetch=1, grid=(steps,),
                                               in_specs=g_specs + s_specs + q_specs, out_specs=o_specs),
        out_shape=[jax.ShapeDtypeStruct(s.shape[1:], F32) for s in ss],
        compiler_params=pltpu.CompilerParams(dimension_semantics=("arbitrary",), vmem_limit_bytes=VMEM_LIMIT),
    )(ids, *gs, *ss, *qs)


def _adamw_math(w, g, m, v):
    mn = ADAM_B1 * m + (1.0 - ADAM_B1) * g
    vn = ADAM_B2 * v + (1.0 - ADAM_B2) * (g * g)
    m_hat = mn / (1.0 - ADAM_B1 ** ADAM_STEP)
    v_hat = vn / (1.0 - ADAM_B2 ** ADAM_STEP)
    return -ADAM_LR * (m_hat / (jnp.sqrt(v_hat) + ADAM_EPS) + ADAM_WD * w), mn, vn


def _adamw_halves(w, own, sib, m, v, ids, name, layer=0, n_layers=1, stacked=None):
    C = w.shape[1]
    R = w.shape[0] // n_layers
    half = R // 2
    br = _row_block(half, C)
    nh = half // br
    base = layer * 2 * nh

    def body(ids_ref, w_ref, own_ref, sib_ref, m_ref, v_ref, *rest):
        g_ref, d_ref, mo_ref, vo_ref = rest[-4:]
        is_own = (pl.program_id(0) // nh) == ids_ref[0]
        g = jnp.where(is_own, own_ref[...], sib_ref[...])
        g_ref[...] = g
        d_ref[...], mo_ref[...], vo_ref[...] = _adamw_math(w_ref[...], g, m_ref[...], v_ref[...])

    full = pl.BlockSpec((br, C), lambda r, ids_ref: (base + r, 0))
    own_spec = pl.BlockSpec((br, C), lambda r, ids_ref: (jnp.clip(r - ids_ref[0] * nh, 0, nh - 1), 0))
    sib_spec = pl.BlockSpec((br, C), lambda r, ids_ref: (jnp.clip(r - (1 - ids_ref[0]) * nh, 0, nh - 1), 0))
    in_specs = [full, own_spec, sib_spec, full, full]
    args = [ids, w, own, sib, m, v]
    aliases = {}
    if stacked is not None:
        in_specs += [ANY] * 4
        args += list(stacked)
        aliases = {6 + k: k for k in range(4)}
    return pl.pallas_call(
        body, name=name,
        grid_spec=pltpu.PrefetchScalarGridSpec(
            num_scalar_prefetch=1, grid=(2 * nh,), in_specs=in_specs, out_specs=[full] * 4),
        out_shape=[jax.ShapeDtypeStruct(w.shape, F32)] * 4, input_output_aliases=aliases,
        compiler_params=_params(),
    )(*args)


_PACK_UNIT = 1024


def _pack(arrs):
    flat = []
    for a in arrs:
        f = a.reshape(-1).astype(F32)
        pad = (-f.shape[0]) % _PACK_UNIT
        if pad:
            f = jnp.concatenate([f, jnp.zeros((pad,), F32)])
        flat.append(f)
    return jnp.concatenate(flat).reshape(-1, 128)


def kernel(x, p, norm_mix, norm_ffn, norm_ple, norm_kv, norm_final, a_w_in, a_norm_v, a_w_s, a_b_s, a_w_out, w_kv, b_w_q, b_sinks, b_w_o, f_w_up, f_conv_w, f_conv_b, f_w_down, ple_w_in, ple_w_gate, ple_b_gate, loss_target, m_norm_mix, m_norm_ffn, m_norm_ple, m_norm_kv, m_norm_final, m_a_w_in, m_a_norm_v, m_a_w_s, m_a_b_s, m_a_w_out, m_w_kv, m_b_w_q, m_b_sinks, m_b_w_o, m_f_w_up, m_f_conv_w, m_f_conv_b, m_f_w_down, m_ple_w_in, m_ple_w_gate, m_ple_b_gate, v_norm_mix, v_norm_ffn, v_norm_ple, v_norm_kv, v_norm_final, v_a_w_in, v_a_norm_v, v_a_w_s, v_a_b_s, v_a_w_out, v_w_kv, v_b_w_q, v_b_sinks, v_b_w_o, v_f_w_up, v_f_conv_w, v_f_conv_b, v_f_w_down, v_ple_w_in, v_ple_w_gate, v_ple_b_gate):
    given = dict(locals())

    small_shard = _pack([a_norm_v, f_conv_w])
    pad_rows = (-small_shard.shape[0]) % 16
    if pad_rows:
        small_shard = jnp.concatenate([small_shard, jnp.zeros((pad_rows, 128), F32)])
    groups = [
        [(a_w_in, (0,), BF16), (a_w_out, (0,), BF16), (small_shard, (), F32)],
        [(f_w_up, (0,), BF16), (f_w_down, (0,), BF16)],
        [(ple_w_in, (0,), BF16), (ple_w_gate, (0,), BF16), (w_kv, (), BF16), (b_w_q, (0,), BF16),
         (b_w_o, (0,), BF16), (f_w_up, (1,), BF16), (f_w_down, (1,), BF16), (ple_w_in, (1,), BF16),
         (ple_w_gate, (1,), BF16)],
    ]
    first = list(range(len(groups[0])))
    lands0, sems0, token0 = _split_call("gather_start_g0", _cast_place(groups[0], "cast_place_g0"),
                                        starts=[(3 * len(first), _gather_ici(first))])
    rest, spans, start = [], [], 0
    for gi, items in enumerate(groups[1:], 1):
        rest += _cast_place(items, f"cast_place_g{gi}", deps=(token0,))
        spans.append(list(range(start, start + len(items))))
        start += len(items)
    rest, rest_sems, rest_token = _split_call("gather_start", rest,
                                              starts=[(3 * len(sp), _gather_ici(sp)) for sp in spans])
    group_bufs = [lands0] + [[rest[t] for t in sp] for sp in spans]
    ici_sems = sems0 + rest_sems

    def finish_group(gi, after):
        bufs = group_bufs[gi]
        local = list(range(len(bufs)))
        bufs, d2d_sems, _ = _split_call(f"gather_pass_g{gi}", bufs, waits=[(*ici_sems[gi], _gather_ici(local))],
                                        starts=[(3 * len(local), _gather_d2d(local))], after=after)
        bufs, _, _ = _split_call(f"gather_done_g{gi}", bufs, waits=[(*d2d_sems[0], _gather_d2d(local))])
        return bufs

    def stage0():
        b_in, b_out, b_small = finish_group(0, (rest_token,))
        small_full = b_small.reshape(N_SHARD, -1)
        gv_full = small_full[:, :256].reshape(1, D_MODEL)
        cw_full = small_full[:, _PACK_UNIT:_PACK_UNIT + 2 * 3 * FF_BLK].reshape(N_SHARD, 2, 3, FF_BLK)
        cw_full = jnp.transpose(cw_full, (1, 2, 0, 3)).reshape(2, 3, N_FF)
        return gv_full, cw_full, b_in, b_out.reshape(D_MODEL, D_MODEL)

    def stage1(after):
        b_up, b_dn = finish_group(1, after)
        return b_up, b_dn.reshape(D_FF, D_MODEL)

    def stage2(after):
        pin0, gate0, kv_w, wq, wo, up1, dn1, pin1, gate1 = finish_group(2, after)
        sq = lambda a: a.reshape(D_MODEL, -1)
        return dict(w_pin=[pin0, pin1], w_gate=[sq(gate0), sq(gate1)], w_kv=sq(kv_w), w_q=sq(wq), w_o=sq(wo),
                    w_up1=up1, w_dn1=dn1.reshape(D_FF, D_MODEL))

    dx, (loss, (out_g, out_d, out_m, out_v)) = _local_step(
        x[0], p.reshape(2, -1, PLE_DIM), loss_target[0], norm_mix, norm_ffn, norm_ple, norm_kv, norm_final, a_w_s, a_b_s,
        b_sinks, f_conv_b, ple_b_gate, stage0, stage1, stage2, _Reducer(given))
    weight_names = ['norm_mix', 'norm_ffn', 'norm_ple', 'norm_kv', 'norm_final', 'a_w_in', 'a_norm_v', 'a_w_s',
                    'a_b_s', 'a_w_out', 'w_kv', 'b_w_q', 'b_sinks', 'b_w_o', 'f_w_up', 'f_conv_w', 'f_conv_b',
                    'f_w_down', 'ple_w_in', 'ple_w_gate', 'ple_b_gate']
    return (loss, dx.reshape(x.shape), *[out_g[k] for k in weight_names], *[out_d[k] for k in weight_names],
            *[out_m[k] for k in weight_names], *[out_v[k] for k in weight_names])


def _local_step(xs, p, tgt, norm_mix, norm_ffn, norm_ple, norm_kv, norm_final, a_w_s, a_b_s, b_sinks,
                f_conv_b, ple_b_gate, stage0, stage1, stage2, sched):
    tril = jnp.tril(jnp.ones((CHUNK, CHUNK), F32))
    wsm = (a_w_s[0] * tril[None]).astype(BF16)
    bsb = jnp.broadcast_to(a_b_s[0][:, :, None], (A_GROUPS, CHUNK, CHUNK))
    sinks = b_sinks[0]
    row = lambda a: a.reshape(1, -1)

    gv_full, cw_full, w_in, w_out = stage0()
    h1, zp = _mixer_a_fwd(xs, row(norm_mix[0]), gv_full, wsm, bsb, w_in, w_out)
    w_up0, w_dn0 = stage1((h1,))
    h2, hh0, c0 = _ffn_fwd(h1, row(norm_ffn[0]), cw_full[0], row(f_conv_b[0]), w_up0, w_dn0, 0)
    rest = stage2((h2,))
    w_pin, w_gate, w_kv_f, w_q, w_o = rest['w_pin'], rest['w_gate'], rest['w_kv'], rest['w_q'], rest['w_o']
    w_up = [w_up0, rest['w_up1']]
    w_dn = [w_dn0, rest['w_dn1']]
    h3, kv = _ple_fwd_kv(h2, p, row(norm_ple[0]), row(ple_b_gate[0]), row(norm_kv), w_pin[0], w_gate[0], w_kv_f)
    h4, q, ao, probs, psink = _attn_fwd(h3, row(norm_mix[1]), kv, sinks, w_q, w_o)
    h5, hh1, c1 = _ffn_fwd(h4, row(norm_ffn[1]), cw_full[1], row(f_conv_b[1]), w_up[1], w_dn[1], 1)
    dh6, loss_acc, dn_final = _ple_fwd_final(
        h5, p, tgt, row(norm_ple[1]), row(ple_b_gate[1]), row(norm_final), w_pin[1], w_gate[1])

    def pieces(g):
        return g.reshape(N_SHARD, -1, g.shape[-1])

    dh5, g_pin1, g_gate1, dbg1, dnple1 = _ple_bwd(dh6, h5, p, row(norm_ple[1]), row(ple_b_gate[1]), w_pin[1], w_gate[1], 1)
    early = {('ple_w_in', 1): g_pin1, ('ple_w_gate', 1): pieces(g_gate1)}
    dh4, g_up1, g_dn1, dcw1, dcb1, dnffn1 = _ffn_bwd(
        dh5, h4, hh1, c1, row(norm_ffn[1]), cw_full[1], w_up[1], w_dn[1], 1)
    early['f_w_down', 1] = pieces(g_dn1)
    early['f_w_up', 1] = g_up1
    dh3a, g_wq, g_wo, dkv, dsink, dnmix1 = _attn_bwd(dh4, h3, q, kv, ao, probs, psink, row(norm_mix[1]), w_q, w_o)
    early['b_w_o', 0] = pieces(g_wo)
    early['b_w_q', 0] = pieces(g_wq)
    dh2, g_pin0, g_gate0, dbg0, dnple0, g_wkv, dnkv = _ple_bwd(
        dh3a, h2, p, row(norm_ple[0]), row(ple_b_gate[0]), w_pin[0], w_gate[0], 0,
        kv_args=(h3, dkv, row(norm_kv), w_kv_f))
    early['w_kv', 0] = pieces(g_wkv)
    early['ple_w_in', 0] = g_pin0
    early['ple_w_gate', 0] = pieces(g_gate0)
    deps = sched.early_ready(early)
    dh1, g_up0, g_dn0, dcw0, dcb0, dnffn0 = _ffn_bwd(
        dh2, h1, hh0, c0, row(norm_ffn[0]), cw_full[0], w_up[0], w_dn[0], 0, deps=deps,
        between=lambda part: sched.after_ffn_half((part,)))
    deps = sched.ffn0_ready({('f_w_down', 0): pieces(g_dn0), ('f_w_up', 0): g_up0})
    dx, g_win, g_wout, dws, dbs, dgv, dnmix0 = _mixer_a_bwd(
        dh1, xs, zp, row(norm_mix[0]), gv_full, wsm, bsb, tril, w_in, w_out, deps=deps)
    g_wout = pieces(g_wout)

    small_grads = {
        'norm_mix': jnp.concatenate([dnmix0, dnmix1]), 'norm_ffn': jnp.concatenate([dnffn0, dnffn1]),
        'norm_ple': jnp.concatenate([dnple0, dnple1]), 'norm_kv': dnkv, 'norm_final': dn_final,
        'a_norm_v': dgv, 'a_w_s': dws.reshape(A_GROUPS * CHUNK, CHUNK), 'a_b_s': dbs[:, :, 0],
        'b_sinks': dsink[0:1, :], 'f_conv_w': jnp.concatenate([dcw0, dcw1]),
        'f_conv_b': jnp.concatenate([dcb0, dcb1]), 'ple_b_gate': jnp.concatenate([dbg0, dbg1]),
        'loss': loss_acc,
    }
    outs = sched.finish({('a_w_in', 0): g_win, ('a_w_out', 0): g_wout}, small_grads, (dx,))
    return dx, outs


class _Reducer:
    def __init__(self, given):
        self.given = given
        cx, cy, cc = _mesh_pos()
        self.shard = 2 * cx + cy
        s = self.shard
        self.ids = jnp.stack([cc, s, s ^ 2, s ^ 1, s ^ 3]).astype(jnp.int32)
        self.out = [{}, {}, {}, {}]
        self.stacked = {}

    def _send(self, tag, grads, small=()):
        keys = list(grads)
        srcs = [grads[k] for k in keys] + list(small)
        shapes = [((N_SHARD, g.shape[1] // 2, g.shape[2]), F32) for g in srcs[:len(keys)]]
        shapes += [(s.shape, F32) for s in small]
        return keys, _Exchange(f"send_{tag}", srcs, shapes, _send_to_sibling(len(srcs)), len(srcs))

    def _exchange(self, tag, keys, send, after):
        srcs, lands = send.finish(after)
        n = len(keys)
        parts = _chip_partial(srcs[:n], lands[:n], self.ids, f"chip_partial_{tag}")
        shapes = [(p.shape, BF16) for p in parts]
        if len(srcs) > n:
            small = _small_add(srcs[n:], lands[n:])
            parts += small
            shapes += [((3,) + s.shape, F32) for s in small]
        exch = _Exchange(f"exch_{tag}", parts, shapes, _send_to_chips(len(parts)), 3 * len(parts))
        return (keys, srcs[:n], lands[:n], exch)

    def _swap(self, tag, state, after):
        keys, grads, sib, exch = state
        parts, recv = exch.finish(after)
        n = len(keys)
        own = _chip_sum(grads, sib, recv[:n], self.ids, f"chip_sum_{tag}")
        small_red = _small_sum(parts[n:], recv[n:]) if len(parts) > n else None
        return keys, _Exchange(f"swap_{tag}", own, [(o.shape, F32) for o in own], _send_to_sibling(n), n), small_red

    def _adamw(self, keys, swap, after):
        own, sib = swap.finish(after)
        last = None
        for (name, layer), o, s in zip(keys, own, sib):
            w = self.given[name]
            n_layers = w.shape[0] if w.ndim == 3 else 1
            c2 = w.shape[-1]
            res = _adamw_halves(w.reshape(-1, c2), o, s, self.given['m_' + name].reshape(-1, c2),
                                self.given['v_' + name].reshape(-1, c2), self.ids, f"adamw_{name}{layer}",
                                layer, n_layers, self.stacked.get(name))
            self.stacked[name] = res
            if layer == 0:
                for dst, r in zip(self.out, res):
                    dst[name] = r.reshape(w.shape)
            last = res[0]
        return last

    def early_ready(self, grads):
        self.e_keys, self.e_send = self._send("e", grads)
        return (self.e_send.token,)

    def after_ffn_half(self, after):
        self.e_state = self._exchange("e", self.e_keys, self.e_send, after)
        return (self.e_state[3].token,)

    def ffn0_ready(self, grads):
        _, self.e_swap, _ = self._swap("e", self.e_state, tuple(grads.values()))
        self.f_keys, self.f_send = self._send("f", grads)
        return (self.f_send.token, self.e_swap.token)

    def finish(self, grads, small_grads, after):
        small_names = list(small_grads)
        f_state = self._exchange("f", self.f_keys, self.f_send, after)
        a_keys, a_send = self._send("a", grads, [small_grads[k] for k in small_names])
        a_state = self._exchange("a", a_keys, a_send, (f_state[3].token,))
        e_done = self._adamw(self.e_keys, self.e_swap, (a_state[3].token,))
        f_keys, f_swap, _ = self._swap("f", f_state, (e_done,))
        f_done = self._adamw(f_keys, f_swap, ())
        _, a_swap, small_red = self._swap("a", a_state, (f_done,))
        self._adamw(a_keys, a_swap, ())

        given = self.given
        reduced = dict(zip(small_names, small_red))
        loss = reduced.pop('loss')[0, 0]
        names = list(reduced)
        items = []
        for k in names:
            g = reduced[k]
            cols = g.shape[1] // N_SHARD if k in ('a_norm_v', 'f_conv_w') else g.shape[1]
            view = lambda a: _lane_pad(a.reshape(g.shape[0], -1), cols)
            items.append((view(given[k]), g, view(given['m_' + k]), view(given['v_' + k])))
        res = _adamw_small(items, self.ids)
        for k, four in zip(names, res):
            width = given[k].size // four[0].shape[0]
            for dst, r in zip(self.out, four):
                dst[k] = r[:, :width].reshape(given[k].shape)
        return loss, self.out


def _lane_pad(a, cols):
    return a if a.shape[1] == cols else jnp.pad(a, ((0, 0), (0, cols - a.shape[1])))


def _small_add(a_list, b_list):
    n = len(a_list)

    def body(*refs):
        for t in range(n):
            refs[2 * n + t][...] = refs[t][...] + refs[n + t][...]

    return pl.pallas_call(body, name="chip_partial_small",
                          out_shape=[jax.ShapeDtypeStruct(a.shape, F32) for a in a_list])(*a_list, *b_list)


def _small_sum(parts, recvs):
    n = len(parts)

    def body(*refs):
        for t in range(n):
            q = refs[n + t]
            refs[2 * n + t][...] = (refs[t][...] + q[2]) + (q[0] + q[1])

    return pl.pallas_call(body, name="chip_sum_small",
                          out_shape=[jax.ShapeDtypeStruct(p.shape, F32) for p in parts])(*parts, *recvs)


def _adamw_small(items, ids):
    n = len(items)

    def body(ids_ref, *refs):
        for t in range(n):
            w_ref, g_ref, m_ref, v_ref = refs[4 * t:4 * t + 4]
            g_out, d_ref, mo_ref, vo_ref = refs[4 * n + 4 * t:4 * n + 4 * t + 4]
            g = g_ref[...]
            g_out[...] = g
            d_ref[...], mo_ref[...], vo_ref[...] = _adamw_math(w_ref[...], g, m_ref[...], v_ref[...])

    in_specs, out_specs, out_shape, args = [], [], [], []
    for w, g, m, v in items:
        full = pl.BlockSpec(w.shape, lambda i, ids_ref: (0, 0))
        g_spec = full if g.shape == w.shape else pl.BlockSpec(w.shape, lambda i, ids_ref: (0, ids_ref[1]))
        in_specs += [full, g_spec, full, full]
        out_specs += [full] * 4
        out_shape += [jax.ShapeDtypeStruct(w.shape, F32)] * 4
        args += [w, g, m, v]
    res = pl.pallas_call(
        body, name="adamw_small",
        grid_spec=pltpu.PrefetchScalarGridSpec(num_scalar_prefetch=1, grid=(1,), in_specs=in_specs,
                                               out_specs=out_specs),
        out_shape=out_shape, compiler_params=_params(),
    )(ids, *args)
    return [res[4 * t:4 * t + 4] for t in range(n)]
```

```python
import functools
import math

import numpy as np
import jax
import jax.numpy as jnp
from jax import lax
from jax.experimental import pallas as pl
from jax.experimental.pallas import tpu as pltpu

F32 = jnp.float32
BF16 = jnp.bfloat16

D_MODEL = 1024
CHUNK = 128
A_GROUPS = 8
HEAD_DIM = 64
N_Q_HEADS = 16
N_KV_HEADS = 4
GQA_GROUP = N_Q_HEADS // N_KV_HEADS
KV_DIM = N_KV_HEADS * HEAD_DIM
BLOCK = 128
D_FF = 2816
N_FF = 2 * D_FF
FF_BLK = N_FF // 4
PLE_DIM = 256
EPS = 1e-6
NEG = -1e30
N_SHARD = 4

ADAM_LR = 0.001
ADAM_B1 = 0.9
ADAM_B2 = 0.999
ADAM_EPS = 1e-08
ADAM_WD = 0.01
ADAM_STEP = 10

VMEM_LIMIT = 60 * 1024 * 1024
MESH = pl.DeviceIdType.MESH
ANY = pl.BlockSpec(memory_space=pl.ANY)
SMEM = pl.BlockSpec(memory_space=pltpu.SMEM)

_SLOPES = [float(np.float32(2.0 ** (-8.0 * (h + 1) / N_Q_HEADS))) for h in range(N_Q_HEADS)]


def _dot(a, b):
    return jnp.dot(a, b, preferred_element_type=F32)


def _dot_nt(a, b):
    return lax.dot_general(a, b, (((1,), (1,)), ((), ())), preferred_element_type=F32)


def _dot_tn(a, b):
    return lax.dot_general(a, b, (((0,), (0,)), ((), ())), preferred_element_type=F32)


def _rms(x, g):
    r = lax.rsqrt(jnp.mean(x * x, axis=-1, keepdims=True) + EPS)
    xh = x * r
    return xh * g, xh, r


def _rms_bwd(dy, xh, r, g):
    dxh = dy * g
    dg = jnp.sum(dy * xh, axis=0, keepdims=True)
    dx = r * (dxh - xh * jnp.mean(dxh * xh, axis=-1, keepdims=True))
    return dx, dg


_GELU_C = math.sqrt(2.0 / math.pi)


def _gelu(x):
    t = jnp.tanh(_GELU_C * (x + 0.044715 * (x * x * x)))
    return 0.5 * x * (1.0 + t)


def _gelu_grad(x):
    x2 = x * x
    t = jnp.tanh(_GELU_C * (x + 0.044715 * (x2 * x)))
    return 0.5 * (1.0 + t) + 0.5 * x * (1.0 - t * t) * (_GELU_C * (1.0 + 3.0 * 0.044715 * x2))


def _sigmoid(x):
    return 0.5 * jnp.tanh(0.5 * x) + 0.5


def _load_once(pairs, sem):
    @pl.when(pl.program_id(0) == 0)
    def _():
        cps = [pltpu.make_async_copy(s, d, sem.at[i]) for i, (s, d) in enumerate(pairs)]
        for cp in cps:
            cp.start()
        for cp in cps:
            cp.wait()


def _params(n_axes=1, vmem=VMEM_LIMIT):
    return pltpu.CompilerParams(dimension_semantics=("arbitrary",) * n_axes, vmem_limit_bytes=vmem)


def _row_spec(tm, n, rev_nt=None):
    if rev_nt is None:
        return pl.BlockSpec((tm, n), lambda i: (i, 0))
    return pl.BlockSpec((tm, n), lambda i: (rev_nt - 1 - i, 0))


def _const_spec(shape):
    nd = len(shape)
    return pl.BlockSpec(shape, lambda i: (0,) * nd)


def _add_deps(body, in_specs, args, deps):
    nd = len(deps)
    if nd == 0:
        return body, list(in_specs), list(args)

    def wrapped(*refs):
        return body(*refs[nd:])

    return wrapped, [ANY] * nd + list(in_specs), list(deps) + list(args)


def _zero_first(refs):
    @pl.when(pl.program_id(0) == 0)
    def _():
        for r in refs:
            r[...] = jnp.zeros(r.shape, r.dtype)


def _mixer_a_fwd(x, nmix, gv, wsm, bsb, w_in, w_out):
    T = x.shape[0]
    tm = min(512, T)
    nt = T // tm
    nw = 2 * D_MODEL // N_SHARD

    def body(x_ref, nmix_ref, gv_ref, ws_ref, bsb_ref, w_in_hbm, w_out_hbm,
             h1_ref, zp_ref, w_in_v, w_out_v, gated_v, sem):
        _load_once([(w_in_hbm, w_in_v), (w_out_hbm, w_out_v)], sem)
        xv = x_ref[...]
        xn = _rms(xv, nmix_ref[...])[0].astype(BF16)
        for j in range(N_SHARD):
            zp_ref[:, j * nw:(j + 1) * nw] = _dot(xn, w_in_v[j])
        z = _gelu(zp_ref[...])
        u = z[:, :D_MODEL]
        vn = _rms(z[:, D_MODEL:], gv_ref[...])[0].astype(BF16)
        for c in range(tm // CHUNK):
            rows = slice(c * CHUNK, (c + 1) * CHUNK)
            for h in range(A_GROUPS):
                cols = slice(h * CHUNK, (h + 1) * CHUNK)
                s = _dot(ws_ref[h], vn[rows, cols]) + bsb_ref[h]
                gated_v[rows, cols] = (u[rows, cols] * s).astype(BF16)
        h1_ref[...] = xv + _dot(gated_v[...], w_out_v[...])

    return pl.pallas_call(
        body, name="mixer_a_fwd", grid=(nt,),
        in_specs=[_row_spec(tm, D_MODEL), _const_spec((1, D_MODEL)), _const_spec((1, D_MODEL)),
                  _const_spec((A_GROUPS, CHUNK, CHUNK)), _const_spec((A_GROUPS, CHUNK, CHUNK)), ANY, ANY],
        out_specs=[_row_spec(tm, D_MODEL), _row_spec(tm, 2 * D_MODEL)],
        out_shape=[jax.ShapeDtypeStruct((T, D_MODEL), F32), jax.ShapeDtypeStruct((T, 2 * D_MODEL), F32)],
        scratch_shapes=[pltpu.VMEM((N_SHARD, D_MODEL, nw), BF16), pltpu.VMEM((D_MODEL, D_MODEL), BF16),
                        pltpu.VMEM((tm, D_MODEL), BF16), pltpu.SemaphoreType.DMA((2,))],
        compiler_params=_params(),
    )(x, nmix, gv, wsm, bsb, w_in, w_out)


def _mixer_a_bwd(dh, x, zp, nmix, gv, wsm, bsb, tril, w_in, w_out, deps=()):
    T = x.shape[0]
    tm = min(256, T)
    nt = T // tm
    nw = 2 * D_MODEL // N_SHARD

    def body(dh_ref, x_ref, zp_ref, nmix_ref, gv_ref, ws_ref, bsb_ref, tril_ref, w_in_hbm, w_out_hbm,
             dx_ref, dwin_ref, dwout_ref, dws_ref, dbs_ref, dgv_ref, dnmix_ref,
             w_in_v, w_out_v, du_v, dvn_v, dbs_v, gated_ref, sem):
        _load_once([(w_in_hbm, w_in_v), (w_out_hbm, w_out_v)], sem)
        _zero_first([dws_ref, dbs_v, dgv_ref, dnmix_ref, dwin_ref, dwout_ref])
        i = pl.program_id(0)
        dhv = dh_ref[...]
        dhb = dhv.astype(BF16)
        xv = x_ref[...]
        xn, xh, r = _rms(xv, nmix_ref[...])
        xnb = xn.astype(BF16)
        zpv = zp_ref[...]
        z = _gelu(zpv)
        u = z[:, :D_MODEL]
        vn_f, vh, rv = _rms(z[:, D_MODEL:], gv_ref[...])
        vn = vn_f.astype(BF16)
        dgated = _dot_nt(dhb, w_out_v[...])
        for c in range(tm // CHUNK):
            rows = slice(c * CHUNK, (c + 1) * CHUNK)
            for h in range(A_GROUPS):
                cols = slice(h * CHUNK, (h + 1) * CHUNK)
                vn_h = vn[rows, cols]
                s = _dot(ws_ref[h], vn_h) + bsb_ref[h]
                dgt = dgated[rows, cols]
                u_h = u[rows, cols]
                gated_ref[rows, cols] = (u_h * s).astype(BF16)
                du_v[rows, cols] = dgt * s
                ds = dgt * u_h
                dsb = ds.astype(BF16)
                dws_ref[h] += _dot_nt(dsb, vn_h)
                dbs_v[h] += ds
                dvn_v[rows, cols] = _dot_tn(ws_ref[h], dsb)
        dwout_ref[...] += _dot_tn(gated_ref[...], dhb)
        dv, dgv = _rms_bwd(dvn_v[...], vh, rv, gv_ref[...])
        dgv_ref[...] += dgv
        dzu = (du_v[...] * _gelu_grad(zpv[:, :D_MODEL])).astype(BF16)
        dzv = (dv * _gelu_grad(zpv[:, D_MODEL:])).astype(BF16)
        dzs = (dzu[:, :nw], dzu[:, nw:], dzv[:, :nw], dzv[:, nw:])
        dxn = jnp.zeros((tm, D_MODEL), F32)
        for j in range(N_SHARD):
            dxn = dxn + _dot_nt(dzs[j], w_in_v[j])
            dwin_ref[j] += _dot_tn(xnb, dzs[j])
        dxx, dn = _rms_bwd(dxn, xh, r, nmix_ref[...])
        dnmix_ref[...] += dn
        dx_ref[...] = dhv + dxx

        @pl.when(i == nt - 1)
        def _():
            for h in range(A_GROUPS):
                dws_ref[h] = dws_ref[h] * tril_ref[...]
                dbs_ref[h] = jnp.broadcast_to(jnp.sum(dbs_v[h], axis=1, keepdims=True), (CHUNK, CHUNK))

    grp = (A_GROUPS, CHUNK, CHUNK)
    body, in_specs, args = _add_deps(
        body, [_row_spec(tm, D_MODEL), _row_spec(tm, D_MODEL), _row_spec(tm, 2 * D_MODEL),
               _const_spec((1, D_MODEL)), _const_spec((1, D_MODEL)), _const_spec(grp), _const_spec(grp),
               _const_spec((CHUNK, CHUNK)), ANY, ANY],
        [dh, x, zp, nmix, gv, wsm, bsb, tril, w_in, w_out], deps)
    return pl.pallas_call(
        body, name="mixer_a_bwd", grid=(nt,), in_specs=in_specs,
        out_specs=[_row_spec(tm, D_MODEL), _const_spec((N_SHARD, D_MODEL, nw)), _const_spec((D_MODEL, D_MODEL)),
                   _const_spec(grp), _const_spec(grp), _const_spec((1, D_MODEL)), _const_spec((1, D_MODEL))],
        out_shape=[jax.ShapeDtypeStruct((T, D_MODEL), F32), jax.ShapeDtypeStruct((N_SHARD, D_MODEL, nw), F32),
                   jax.ShapeDtypeStruct((D_MODEL, D_MODEL), F32),
                   jax.ShapeDtypeStruct(grp, F32), jax.ShapeDtypeStruct(grp, F32),
                   jax.ShapeDtypeStruct((1, D_MODEL), F32), jax.ShapeDtypeStruct((1, D_MODEL), F32)],
        scratch_shapes=[pltpu.VMEM((N_SHARD, D_MODEL, nw), BF16), pltpu.VMEM((D_MODEL, D_MODEL), BF16),
                        pltpu.VMEM((tm, D_MODEL), F32), pltpu.VMEM((tm, D_MODEL), F32),
                        pltpu.VMEM(grp, F32), pltpu.VMEM((tm, D_MODEL), BF16), pltpu.SemaphoreType.DMA((2,))],
        compiler_params=_params(),
    )(*args)


def _load_ffn_weights(w_up_hbm, w_dn_hbm, layer, w_up_v, w_dn_v, sem):
    _load_once([(w_up_hbm, w_up_v), (w_dn_hbm, w_dn_v)], sem)


def _ffn_fwd(h, nffn, cw, cb, w_up, w_dn, layer):
    T = h.shape[0]
    tm = min(256, T)
    nt = T // tm

    def body(h_ref, n_ref, cw_ref, cb_ref, w_up_hbm, w_dn_hbm, out_ref, hh_ref, gate_ref,
             w_up_v, w_dn_v, carry_v, sem):
        _load_ffn_weights(w_up_hbm, w_dn_hbm, layer, w_up_v, w_dn_v, sem)
        _zero_first([carry_v])
        xv = h_ref[...]
        xf = _rms(xv, n_ref[...])[0].astype(BF16)
        acc = xv
        for j in range(2):
            cs = []
            for blk in (j, j + 2):
                cols = slice(blk * FF_BLK, (blk + 1) * FF_BLK)
                hh = _dot(xf, w_up_v[blk])
                hh_ref[:, cols] = hh.astype(BF16)
                ext = jnp.concatenate([carry_v[blk], hh], axis=0)
                carry_v[blk] = hh[tm - 8:, :]
                s1 = pltpu.roll(ext, 1, 0)[8:]
                s2 = pltpu.roll(ext, 2, 0)[8:]
                cs.append(cb_ref[:, cols] + cw_ref[0:1, cols] * s2 + cw_ref[1:2, cols] * s1
                          + cw_ref[2:3, cols] * hh)
            cg, cu = cs
            sg = _sigmoid(cg)
            sil = cg * sg
            act = (sil * cu).astype(BF16)
            for kind, val in enumerate((cu, sil, sg * (1.0 + cg * (1.0 - sg)))):
                gate_ref[:, kind * D_FF + j * FF_BLK:kind * D_FF + (j + 1) * FF_BLK] = val.astype(BF16)
            gate_ref[:, 3 * D_FF + j * FF_BLK:3 * D_FF + (j + 1) * FF_BLK] = act
            acc = acc + _dot(act, w_dn_v[j * FF_BLK:(j + 1) * FF_BLK, :])
        out_ref[...] = acc

    return pl.pallas_call(
        body, name=f"ffn_fwd{layer}", grid=(nt,),
        in_specs=[_row_spec(tm, D_MODEL), _const_spec((1, D_MODEL)), _const_spec((3, N_FF)),
                  _const_spec((1, N_FF)), ANY, ANY],
        out_specs=[_row_spec(tm, D_MODEL), _row_spec(tm, N_FF), _row_spec(tm, 4 * D_FF)],
        out_shape=[jax.ShapeDtypeStruct((T, D_MODEL), F32), jax.ShapeDtypeStruct((T, N_FF), BF16),
                   jax.ShapeDtypeStruct((T, 4 * D_FF), BF16)],
        scratch_shapes=[pltpu.VMEM((N_SHARD, D_MODEL, FF_BLK), BF16), pltpu.VMEM((D_FF, D_MODEL), BF16),
                        pltpu.VMEM((N_SHARD, 8, FF_BLK), F32), pltpu.SemaphoreType.DMA((2 * N_SHARD,))],
        compiler_params=_params(),
    )(h, nffn, cw, cb, w_up, w_dn)


def _wgrad(a, b, bn, col_sharded, name, deps=(), a_cols=None):
    T = a.shape[0]
    K, a_blk = (a.shape[1], 0) if a_cols is None else a_cols
    N = b.shape[1]
    tt = min(2048, T)
    nn, ntt = N // bn, T // tt
    kr = K // N_SHARD

    def body(a_ref, b_ref, o_ref):
        @pl.when(pl.program_id(1) == 0)
        def _():
            o_ref[...] = jnp.zeros(o_ref.shape, F32)
        d = _dot_tn(a_ref[...].astype(BF16), b_ref[...].astype(BF16))
        if col_sharded:
            o_ref[...] += d
        else:
            for j in range(N_SHARD):
                o_ref[j] += d[j * kr:(j + 1) * kr]

    if col_sharded:
        assert nn == N_SHARD
        out_spec = pl.BlockSpec((None, K, bn), lambda n, t: (n, 0, 0))
        out_shape = jax.ShapeDtypeStruct((N_SHARD, K, bn), F32)
    else:
        out_spec = pl.BlockSpec((N_SHARD, kr, bn), lambda n, t: (0, 0, n))
        out_shape = jax.ShapeDtypeStruct((N_SHARD, kr, N), F32)
    body, in_specs, args = _add_deps(
        body, [pl.BlockSpec((tt, K), lambda n, t: (t, a_blk)), pl.BlockSpec((tt, bn), lambda n, t: (t, n))],
        [a, b], deps)
    return pl.pallas_call(
        body, name=name, grid=(nn, ntt), in_specs=in_specs, out_specs=out_spec, out_shape=out_shape,
        compiler_params=pltpu.CompilerParams(dimension_semantics=("arbitrary",) * 2, vmem_limit_bytes=VMEM_LIMIT),
    )(*args)


def _ffn_bwd(dh, h, hh, gate, nffn, cw, w_up, w_dn, layer, deps=(), between=None):
    T = h.shape[0]
    tm = min(256, T)
    nt = T // tm

    def body(dh_ref, h_ref, hh_ref, u_ref, sil_ref, dsil_ref, n_ref, cw_ref, w_up_hbm, w_dn_hbm,
             dhin_ref, dhh_ref, xf_ref, dcw_ref, dcb_ref, dn_ref,
             w_up_v, w_dn_v, carry_v, sem):
        _load_ffn_weights(w_up_hbm, w_dn_hbm, layer, w_up_v, w_dn_v, sem)
        _zero_first([carry_v, dcw_ref, dcb_ref, dn_ref])
        dout = dh_ref[...]
        doutb = dout.astype(BF16)
        xf_f, xh, r = _rms(h_ref[...], n_ref[...])
        xf_ref[...] = xf_f.astype(BF16)
        dxf = jnp.zeros((tm, D_MODEL), F32)
        for j in range(2):
            blks = (j, j + 2)
            pair = slice(j * FF_BLK, (j + 1) * FF_BLK)
            dact = _dot_nt(doutb, w_dn_v[pair, :])
            dcs = (dact * u_ref[:, pair].astype(F32) * dsil_ref[:, pair].astype(F32),
                   dact * sil_ref[:, pair].astype(F32))
            for blk, dc in zip(blks, dcs):
                cols = slice(blk * FF_BLK, (blk + 1) * FF_BLK)
                hhv = hh_ref[:, cols].astype(F32)
                ext = jnp.concatenate([dc, carry_v[blk]], axis=0)
                carry_v[blk] = dc[:8, :]
                n = tm + 8
                a1 = pltpu.roll(ext, n - 1, 0)[:tm]
                a2 = pltpu.roll(ext, n - 2, 0)[:tm]
                dcb_ref[:, cols] += jnp.sum(dc, axis=0, keepdims=True)
                dcw_ref[0:1, cols] += jnp.sum(a2 * hhv, axis=0, keepdims=True)
                dcw_ref[1:2, cols] += jnp.sum(a1 * hhv, axis=0, keepdims=True)
                dcw_ref[2:3, cols] += jnp.sum(dc * hhv, axis=0, keepdims=True)
                dhh = (cw_ref[2:3, cols] * dc + cw_ref[1:2, cols] * a1 + cw_ref[0:1, cols] * a2).astype(BF16)
                dhh_ref[:, cols] = dhh
                dxf = dxf + _dot_nt(dhh, w_up_v[blk])
        dxx, dn = _rms_bwd(dxf, xh, r, n_ref[...])
        dn_ref[...] += dn
        dhin_ref[...] = dout + dxx

    rev = functools.partial(_row_spec, rev_nt=nt)

    def kind(k):
        return pl.BlockSpec((tm, D_FF), lambda i: (nt - 1 - i, k))

    body, in_specs, args = _add_deps(
        body, [rev(tm, D_MODEL), rev(tm, D_MODEL), rev(tm, N_FF), kind(0), kind(1), kind(2),
               _const_spec((1, D_MODEL)), _const_spec((3, N_FF)), ANY, ANY],
        [dh, h, hh, gate, gate, gate, nffn, cw, w_up, w_dn], deps)
    dhin, dhh, xf, dcw, dcb, dn = pl.pallas_call(
        body, name=f"ffn_bwd{layer}", grid=(nt,), in_specs=in_specs,
        out_specs=[rev(tm, D_MODEL), rev(tm, N_FF), rev(tm, D_MODEL),
                   _const_spec((3, N_FF)), _const_spec((1, N_FF)), _const_spec((1, D_MODEL))],
        out_shape=[jax.ShapeDtypeStruct((T, D_MODEL), F32),
                   jax.ShapeDtypeStruct((T, N_FF), BF16), jax.ShapeDtypeStruct((T, D_MODEL), BF16),
                   jax.ShapeDtypeStruct((3, N_FF), F32), jax.ShapeDtypeStruct((1, N_FF), F32),
                   jax.ShapeDtypeStruct((1, D_MODEL), F32)],
        scratch_shapes=[pltpu.VMEM((N_SHARD, D_MODEL, FF_BLK), BF16), pltpu.VMEM((D_FF, D_MODEL), BF16),
                        pltpu.VMEM((N_SHARD, 8, FF_BLK), F32), pltpu.SemaphoreType.DMA((2 * N_SHARD,))],
        compiler_params=_params(),
    )(*args)
    deps2 = between(dhin) if between is not None else ()
    dwdn = _wgrad(gate, dh, D_MODEL // 2, False, f"wgrad_ffn_down{layer}", deps=deps2, a_cols=(D_FF, 3))
    dwup = _wgrad(xf, dhh, FF_BLK, True, f"wgrad_ffn_up{layer}", deps=deps2)
    return dhin, dwup, dwdn, dcw, dcb, dn


def _load_ple_weights(w_pin_hbm, w_gate_hbm, layer, w_pin_v, w_gate_v, sem, extra=()):
    _load_once([(w_pin_hbm, w_pin_v), (w_gate_hbm, w_gate_v)] + list(extra), sem)


def _p_spec(tm, layer):
    return pl.BlockSpec((None, tm, PLE_DIM), lambda i: (layer, i, 0))


def _ple_terms(xv, p_ref, n_ref, bg_ref, w_pin_v, w_gate_v, pe_v):
    pw = D_MODEL // N_SHARD
    xg, xh, r = _rms(xv, n_ref[...])
    xgb = xg.astype(BF16)
    gate = _sigmoid(_dot(xgb, w_gate_v[...]) + bg_ref[...])
    pb = p_ref[...].astype(BF16)
    for j in range(N_SHARD):
        pe_v[:, j * pw:(j + 1) * pw] = _dot(pb, w_pin_v[j])
    pe = pe_v[...]
    return pe * gate, pe, gate, xgb, xh, r


def _ple_fwd_kv(h, p, nple, bg, nkv, w_pin, w_gate, w_kv):
    T = h.shape[0]
    tm = min(512, T)
    nt = T // tm
    pw = D_MODEL // N_SHARD

    def body(h_ref, p_ref, n_ref, bg_ref, nkv_ref, w_pin_hbm, w_gate_hbm, w_kv_hbm,
             out_ref, kv_ref, w_pin_v, w_gate_v, w_kv_v, pe_v, sem):
        _load_ple_weights(w_pin_hbm, w_gate_hbm, 0, w_pin_v, w_gate_v, sem, [(w_kv_hbm, w_kv_v)])
        xv = h_ref[...]
        hn = xv + _ple_terms(xv, p_ref, n_ref, bg_ref, w_pin_v, w_gate_v, pe_v)[0]
        out_ref[...] = hn
        kvn = _rms(hn, nkv_ref[...])[0].astype(BF16)
        kv_ref[...] = _dot(kvn, w_kv_v[...]).astype(BF16)

    vec = _const_spec((1, D_MODEL))
    return pl.pallas_call(
        body, name="ple_fwd0", grid=(nt,),
        in_specs=[_row_spec(tm, D_MODEL), _p_spec(tm, 0), vec, vec, vec, ANY, ANY, ANY],
        out_specs=[_row_spec(tm, D_MODEL), _row_spec(tm, 2 * KV_DIM)],
        out_shape=[jax.ShapeDtypeStruct((T, D_MODEL), F32), jax.ShapeDtypeStruct((T, 2 * KV_DIM), BF16)],
        scratch_shapes=[pltpu.VMEM((N_SHARD, PLE_DIM, pw), BF16), pltpu.VMEM((D_MODEL, D_MODEL), BF16),
                        pltpu.VMEM((D_MODEL, 2 * KV_DIM), BF16), pltpu.VMEM((tm, D_MODEL), F32),
                        pltpu.SemaphoreType.DMA((2 * N_SHARD + 1,))],
        compiler_params=_params(),
    )(h, p, nple, bg, nkv, w_pin, w_gate, w_kv)


def _ple_fwd_final(h, p, tgt, nple, bg, nfin, w_pin, w_gate):
    T = h.shape[0]
    tm = min(512, T)
    nt = T // tm
    pw = D_MODEL // N_SHARD

    def body(h_ref, p_ref, t_ref, n_ref, bg_ref, nf_ref, w_pin_hbm, w_gate_hbm,
             dh_ref, loss_ref, dnf_ref, w_pin_v, w_gate_v, pe_v, sem):
        _load_ple_weights(w_pin_hbm, w_gate_hbm, 1, w_pin_v, w_gate_v, sem)
        _zero_first([loss_ref, dnf_ref])
        xv = h_ref[...]
        hn = xv + _ple_terms(xv, p_ref, n_ref, bg_ref, w_pin_v, w_gate_v, pe_v)[0]
        y, yh, r = _rms(hn, nf_ref[...])
        diff = y - t_ref[...]
        loss_ref[...] += 0.5 * jnp.sum(jnp.mean(diff * diff, axis=-1, keepdims=True))
        dy = diff * (1.0 / D_MODEL)
        dhn, dnf = _rms_bwd(dy, yh, r, nf_ref[...])
        dnf_ref[...] += dnf
        dh_ref[...] = dhn

    vec = _const_spec((1, D_MODEL))
    return pl.pallas_call(
        body, name="ple_fwd1", grid=(nt,),
        in_specs=[_row_spec(tm, D_MODEL), _p_spec(tm, 1), _row_spec(tm, D_MODEL), vec, vec, vec, ANY, ANY],
        out_specs=[_row_spec(tm, D_MODEL), _const_spec((8, 128)), vec],
        out_shape=[jax.ShapeDtypeStruct((T, D_MODEL), F32), jax.ShapeDtypeStruct((8, 128), F32),
                   jax.ShapeDtypeStruct((1, D_MODEL), F32)],
        scratch_shapes=[pltpu.VMEM((N_SHARD, PLE_DIM, pw), BF16), pltpu.VMEM((D_MODEL, D_MODEL), BF16),
                        pltpu.VMEM((tm, D_MODEL), F32), pltpu.SemaphoreType.DMA((2 * N_SHARD,))],
        compiler_params=_params(),
    )(h, p, tgt, nple, bg, nfin, w_pin, w_gate)


def _ple_bwd(dh, hb, p, nple, bg, w_pin, w_gate, layer, kv_args=None):
    T = hb.shape[0]
    tm = min(512, T)
    nt = T // tm
    with_kv = kv_args is not None
    pw = D_MODEL // N_SHARD

    def body(*refs):
        if with_kv:
            (dh_ref, hb_ref, p_ref, n_ref, bg_ref, w_pin_hbm, w_gate_hbm, hc_ref, dkv_ref, nkv_ref, w_kv_hbm,
             dhb_ref, dwpin_ref, dwgate_ref, dbg_ref, dn_ref, dwkv_ref, dnkv_ref,
             w_pin_v, w_gate_v, pe_v, w_kv_v, sem) = refs
        else:
            (dh_ref, hb_ref, p_ref, n_ref, bg_ref, w_pin_hbm, w_gate_hbm,
             dhb_ref, dwpin_ref, dwgate_ref, dbg_ref, dn_ref, w_pin_v, w_gate_v, pe_v, sem) = refs
        pairs = [(w_pin_hbm, w_pin_v), (w_gate_hbm, w_gate_v)]
        if with_kv:
            pairs.append((w_kv_hbm, w_kv_v))
        _load_once(pairs, sem)
        _zero_first([dwpin_ref, dwgate_ref, dbg_ref, dn_ref] + ([dwkv_ref, dnkv_ref] if with_kv else []))
        do = dh_ref[...]
        if with_kv:
            dkvb = dkv_ref[...].astype(BF16)
            dkvn = _dot_nt(dkvb, w_kv_v[...])
            kvn, kh, kr = _rms(hc_ref[...], nkv_ref[...])
            dwkv_ref[...] += _dot_tn(kvn.astype(BF16), dkvb)
            dk, dnkv = _rms_bwd(dkvn, kh, kr, nkv_ref[...])
            dnkv_ref[...] += dnkv
            do = do + dk
        _, pe, gate, xgb, xh, r = _ple_terms(hb_ref[...], p_ref, n_ref, bg_ref, w_pin_v, w_gate_v, pe_v)
        dpe = (do * gate).astype(BF16)
        pb = p_ref[...].astype(BF16)
        for j in range(N_SHARD):
            dwpin_ref[j] += _dot_tn(pb, dpe[:, j * pw:(j + 1) * pw])
        da = do * pe * (gate * (1.0 - gate))
        dab = da.astype(BF16)
        dbg_ref[...] += jnp.sum(da, axis=0, keepdims=True)
        dxg = _dot_nt(dab, w_gate_v[...])
        dwgate_ref[...] += _dot_tn(xgb, dab)
        dxx, dn = _rms_bwd(dxg, xh, r, n_ref[...])
        dn_ref[...] += dn
        dhb_ref[...] = do + dxx

    vec = _const_spec((1, D_MODEL))
    row = _row_spec(tm, D_MODEL)
    in_specs = [row, row, _p_spec(tm, layer), vec, vec, ANY, ANY]
    args = [dh, hb, p, nple, bg, w_pin, w_gate]
    out_specs = [row, _const_spec((N_SHARD, PLE_DIM, pw)), _const_spec((D_MODEL, D_MODEL)), vec, vec]
    out_shape = [jax.ShapeDtypeStruct((T, D_MODEL), F32), jax.ShapeDtypeStruct((N_SHARD, PLE_DIM, pw), F32),
                 jax.ShapeDtypeStruct((D_MODEL, D_MODEL), F32),
                 jax.ShapeDtypeStruct((1, D_MODEL), F32), jax.ShapeDtypeStruct((1, D_MODEL), F32)]
    scratch = [pltpu.VMEM((N_SHARD, PLE_DIM, pw), BF16), pltpu.VMEM((D_MODEL, D_MODEL), BF16),
               pltpu.VMEM((tm, D_MODEL), F32)]
    if with_kv:
        hc, dkv, nkv, w_kv = kv_args
        in_specs += [row, _row_spec(tm, 2 * KV_DIM), vec, ANY]
        args += [hc, dkv, nkv, w_kv]
        out_specs += [_const_spec((D_MODEL, 2 * KV_DIM)), vec]
        out_shape += [jax.ShapeDtypeStruct((D_MODEL, 2 * KV_DIM), F32), jax.ShapeDtypeStruct((1, D_MODEL), F32)]
        scratch.append(pltpu.VMEM((D_MODEL, 2 * KV_DIM), BF16))
    scratch.append(pltpu.SemaphoreType.DMA((3,)))
    return pl.pallas_call(
        body, name=f"ple_bwd{layer}", grid=(nt,), in_specs=in_specs, out_specs=out_specs,
        out_shape=out_shape, scratch_shapes=scratch, compiler_params=_params(),
    )(*args)


GROUP_ROWS = GQA_GROUP * BLOCK


def _stack_heads(x, kh):
    return jnp.concatenate([x[:, (kh * GQA_GROUP + g) * HEAD_DIM:(kh * GQA_GROUP + g + 1) * HEAD_DIM]
                            for g in range(GQA_GROUP)], axis=0)


def _attn_fwd(h, nmix, kv, sinks, w_q, w_o):
    T = h.shape[0]
    tm = min(512, T)
    nt = T // tm
    nb = tm // BLOCK

    def body(h_ref, n_ref, kv_ref, kvp_ref, sink_ref, w_q_hbm, w_o_hbm,
             out_ref, q_ref, ao_ref, p_ref, psink_ref, w_q_v, w_o_v, kvs_v, sem):
        _load_once([(w_q_hbm, w_q_v), (w_o_hbm, w_o_v)], sem)
        ti = pl.program_id(0)
        xv = h_ref[...]
        xn = _rms(xv, n_ref[...])[0].astype(BF16)
        q_ref[...] = (_dot(xn, w_q_v[...]) * (HEAD_DIM ** -0.5)).astype(BF16)
        kvs_v[0:BLOCK, :] = kvp_ref[...]
        kvs_v[BLOCK:, :] = kv_ref[...]
        lane = lax.broadcasted_iota(jnp.int32, (BLOCK, 128), 1)
        ii = lax.broadcasted_iota(jnp.int32, (BLOCK, 2 * BLOCK), 0)
        jj = lax.broadcasted_iota(jnp.int32, (BLOCK, 2 * BLOCK), 1)
        dist = ii + BLOCK - jj
        inband = (dist >= 0) & (dist < BLOCK)
        distf = dist.astype(F32)

        def blk_body(b, carry):
            r0 = pl.multiple_of(b * BLOCK, BLOCK)
            valid = inband & ((jj >= BLOCK) | jnp.logical_not(jnp.logical_and(ti == 0, b == 0)))
            qb = q_ref[pl.ds(r0, BLOCK), :]
            band = kvs_v[pl.ds(r0, 2 * BLOCK), :]
            psink_mat = jnp.zeros((BLOCK, 128), F32)
            outs = []
            for hq in range(N_Q_HEADS):
                kh, g = divmod(hq, GQA_GROUP)
                k_h = band[:, kh * HEAD_DIM:(kh + 1) * HEAD_DIM]
                v_h = band[:, KV_DIM + kh * HEAD_DIM:KV_DIM + (kh + 1) * HEAD_DIM]
                s = _dot_nt(qb[:, hq * HEAD_DIM:(hq + 1) * HEAD_DIM], k_h) - _SLOPES[hq] * distf
                s = jnp.where(valid, s, NEG)
                sink = sink_ref[hq]
                m = jnp.maximum(jnp.max(s, axis=1, keepdims=True), sink)
                e = jnp.exp(s - m)
                esink = jnp.exp(sink - m)
                inv = 1.0 / (jnp.sum(e, axis=1, keepdims=True) + esink)
                pb = (e * inv).astype(BF16)
                p_ref[b, kh, g * BLOCK:(g + 1) * BLOCK, :] = pb
                outs.append(_dot(pb, v_h))
                psink_mat = jnp.where(lane == hq, esink * inv, psink_mat)
            ao_ref[pl.ds(r0, BLOCK), :] = jnp.concatenate(outs, axis=1).astype(BF16)
            psink_ref[pl.ds(r0, BLOCK), :] = psink_mat
            return carry

        lax.fori_loop(0, nb, blk_body, 0)
        out_ref[...] = xv + _dot(ao_ref[...], w_o_v[...])

    row = _row_spec(tm, D_MODEL)
    prev_spec = pl.BlockSpec((BLOCK, 2 * KV_DIM), lambda i: (jnp.maximum(i * nb - 1, 0), 0))
    return pl.pallas_call(
        body, name="attn_fwd", grid=(nt,),
        in_specs=[row, _const_spec((1, D_MODEL)), _row_spec(tm, 2 * KV_DIM), prev_spec, SMEM, ANY, ANY],
        out_specs=[row, row, row, pl.BlockSpec((nb, N_KV_HEADS, GROUP_ROWS, 2 * BLOCK), lambda i: (i, 0, 0, 0)),
                   _row_spec(tm, 128)],
        out_shape=[jax.ShapeDtypeStruct((T, D_MODEL), F32), jax.ShapeDtypeStruct((T, D_MODEL), BF16),
                   jax.ShapeDtypeStruct((T, D_MODEL), BF16),
                   jax.ShapeDtypeStruct((T // BLOCK, N_KV_HEADS, GROUP_ROWS, 2 * BLOCK), BF16),
                   jax.ShapeDtypeStruct((T, 128), F32)],
        scratch_shapes=[pltpu.VMEM((D_MODEL, D_MODEL), BF16), pltpu.VMEM((D_MODEL, D_MODEL), BF16),
                        pltpu.VMEM((tm + BLOCK, 2 * KV_DIM), BF16), pltpu.SemaphoreType.DMA((2,))],
        compiler_params=_params(),
    )(h, nmix, kv, kv, sinks, w_q, w_o)


def _attn_bwd(dh, h, q, kv, ao, p, psink, nmix, w_q, w_o):
    T = h.shape[0]
    tm = min(512, T)
    nt = T // tm
    nb = tm // BLOCK

    def body(dh_ref, h_ref, q_ref, kv_ref, kvp_ref, ao_ref, p_ref, psink_ref, n_ref, w_q_hbm, w_o_hbm,
             dhin_ref, dwq_ref, dwo_ref, dkv_ref, dsink_ref, dn_ref,
             w_q_v, w_o_v, kvs_v, dao_v, dq_v, dkv_v, carry_v, sem):
        _load_once([(w_q_hbm, w_q_v), (w_o_hbm, w_o_v)], sem)
        _zero_first([carry_v, dsink_ref, dn_ref, dwq_ref, dwo_ref])
        dout = dh_ref[...]
        doutb = dout.astype(BF16)
        dao_v[...] = _dot_nt(doutb, w_o_v[...])
        dwo_ref[...] += _dot_tn(ao_ref[...], doutb)
        kvs_v[0:BLOCK, :] = kvp_ref[...]
        kvs_v[BLOCK:, :] = kv_ref[...]
        dkv_v[0:tm, :] = jnp.zeros((tm, 2 * KV_DIM), F32)
        dkv_v[tm:, :] = carry_v[...]
        seg = (lax.broadcasted_iota(jnp.int32, (D_MODEL, 128), 0) // HEAD_DIM
               == lax.broadcasted_iota(jnp.int32, (D_MODEL, 128), 1)).astype(BF16)

        def blk_body(b, dsk):
            r0 = pl.multiple_of(b * BLOCK, BLOCK)
            qb = q_ref[pl.ds(r0, BLOCK), :]
            band = kvs_v[pl.ds(r0, 2 * BLOCK), :]
            aob = ao_ref[pl.ds(r0, BLOCK), :].astype(F32)
            daob = dao_v[pl.ds(r0, BLOCK), :]
            prod = daob * aob
            head = prod.astype(BF16)
            tail = (prod - head.astype(F32)).astype(BF16)
            dsk = dsk + psink_ref[pl.ds(r0, BLOCK), :] * (_dot(head, seg) + _dot(tail, seg))
            dqs = []
            dks = []
            dvs = []
            for kh in range(N_KV_HEADS):
                k_h = band[:, kh * HEAD_DIM:(kh + 1) * HEAD_DIM]
                v_h = band[:, KV_DIM + kh * HEAD_DIM:KV_DIM + (kh + 1) * HEAD_DIM]
                q_g = _stack_heads(qb, kh)
                dao_g = _stack_heads(daob, kh)
                prb = p_ref[b, kh]
                pr = prb.astype(F32)
                dd = jnp.sum(dao_g * _stack_heads(aob, kh), axis=1, keepdims=True)
                dao_gb = dao_g.astype(BF16)
                dp = _dot_nt(dao_gb, v_h)
                dsb = (pr * (dp - dd)).astype(BF16)
                dq_g = _dot(dsb, k_h) * (HEAD_DIM ** -0.5)
                dks.append(_dot_tn(dsb, q_g))
                dvs.append(_dot_tn(prb, dao_gb))
                for g in range(GQA_GROUP):
                    dqs.append(dq_g[g * BLOCK:(g + 1) * BLOCK])
            dq_v[pl.ds(r0, BLOCK), :] = jnp.concatenate(dqs, axis=1)
            dkv_v[pl.ds(r0, 2 * BLOCK), :] += jnp.concatenate(dks + dvs, axis=1)
            return dsk

        dsk = lax.fori_loop(0, nb, blk_body, jnp.zeros((BLOCK, 128), F32))
        dsink_ref[...] -= jnp.sum(dsk, axis=0, keepdims=True)
        dqb = dq_v[...].astype(BF16)
        dxn = _dot_nt(dqb, w_q_v[...])
        xn, xh, r = _rms(h_ref[...], n_ref[...])
        dwq_ref[...] += _dot_tn(xn.astype(BF16), dqb)
        dxx, dn = _rms_bwd(dxn, xh, r, n_ref[...])
        dn_ref[...] += dn
        dhin_ref[...] = dout + dxx
        dkv_ref[...] = dkv_v[BLOCK:, :]
        carry_v[...] = dkv_v[0:BLOCK, :]

    rev = functools.partial(_row_spec, rev_nt=nt)
    row = rev(tm, D_MODEL)
    prev_spec = pl.BlockSpec((BLOCK, 2 * KV_DIM), lambda i: (jnp.maximum((nt - 1 - i) * nb - 1, 0), 0))
    return pl.pallas_call(
        body, name="attn_bwd", grid=(nt,),
        in_specs=[row, row, row, rev(tm, 2 * KV_DIM), prev_spec, row,
                  pl.BlockSpec((nb, N_KV_HEADS, GROUP_ROWS, 2 * BLOCK), lambda i: (nt - 1 - i, 0, 0, 0)),
                  rev(tm, 128), _const_spec((1, D_MODEL)), ANY, ANY],
        out_specs=[row, _const_spec((D_MODEL, D_MODEL)), _const_spec((D_MODEL, D_MODEL)), rev(tm, 2 * KV_DIM),
                   _const_spec((8, 128)), _const_spec((1, D_MODEL))],
        out_shape=[jax.ShapeDtypeStruct((T, D_MODEL), F32), jax.ShapeDtypeStruct((D_MODEL, D_MODEL), F32),
                   jax.ShapeDtypeStruct((D_MODEL, D_MODEL), F32), jax.ShapeDtypeStruct((T, 2 * KV_DIM), F32),
                   jax.ShapeDtypeStruct((8, 128), F32), jax.ShapeDtypeStruct((1, D_MODEL), F32)],
        scratch_shapes=[pltpu.VMEM((D_MODEL, D_MODEL), BF16), pltpu.VMEM((D_MODEL, D_MODEL), BF16),
                        pltpu.VMEM((tm + BLOCK, 2 * KV_DIM), BF16), pltpu.VMEM((tm, D_MODEL), F32),
                        pltpu.VMEM((tm, D_MODEL), F32), pltpu.VMEM((tm + BLOCK, 2 * KV_DIM), F32),
                        pltpu.VMEM((BLOCK, 2 * KV_DIM), F32), pltpu.SemaphoreType.DMA((2,))],
        compiler_params=_params(),
    )(dh, h, q, kv, kv, ao, p, psink, nmix, w_q, w_o)


def _mesh_pos():
    return lax.axis_index("x"), lax.axis_index("y"), lax.axis_index("c")


def _other_chips(x, y):
    return [(1 - x, y), (x, 1 - y), (1 - x, 1 - y)]


HBM_SPEC = pl.BlockSpec(memory_space=pltpu.HBM)
SEM_SPEC = pl.BlockSpec(memory_space=pltpu.SEMAPHORE)


def _split_call(name, bufs, waits=(), starts=(), after=()):
    n, nw, ns, na = len(bufs), len(waits), len(starts), len(after)

    def body(*refs):
        brefs = refs[:n]
        wsems = [(refs[n + 2 * k], refs[n + 2 * k + 1]) for k in range(nw)]
        o = n + 2 * nw + na
        ssems = [(refs[o + 2 * k], refs[o + 2 * k + 1]) for k in range(ns)]
        for (ss, rs), (_, _, fn) in zip(wsems, waits):
            for sending, arriving in fn(brefs, ss, rs):
                sending.wait_send()
                arriving.wait_recv()
        for (ss, rs), (_, fn) in zip(ssems, starts):
            for sending, _ in fn(brefs, ss, rs):
                sending.start()
        if ns:
            token = refs[o + 2 * ns + n]
            token[...] = jnp.zeros(token.shape, token.dtype)

    out_shape, out_specs = [], []
    for cnt, _ in starts:
        out_shape += [pltpu.SemaphoreType.DMA((cnt,)), pltpu.SemaphoreType.DMA((cnt,))]
        out_specs += [SEM_SPEC, SEM_SPEC]
    out_shape += [pltpu.HBM(b.shape, b.dtype) for b in bufs]
    out_specs += [HBM_SPEC] * n
    if ns:
        out_shape.append(jax.ShapeDtypeStruct((8, 128), F32))
        out_specs.append(pl.BlockSpec(memory_space=pltpu.VMEM))
    args = [pltpu.with_memory_space_constraint(b, pltpu.HBM) for b in bufs]
    for ss, rs, _ in waits:
        args += [ss, rs]
    args += list(after)
    res = pl.pallas_call(
        body, name=name, out_shape=tuple(out_shape),
        in_specs=[HBM_SPEC] * n + [SEM_SPEC] * (2 * nw) + [ANY] * na, out_specs=tuple(out_specs),
        input_output_aliases={i: 2 * ns + i for i in range(n)},
        compiler_params=pltpu.CompilerParams(has_side_effects=pltpu.SideEffectType.DATAFLOW_SIDE_EFFECTING),
    )(*args)
    sems = [(res[2 * k], res[2 * k + 1]) for k in range(ns)]
    return list(res[2 * ns:2 * ns + n]), sems, (res[2 * ns + n] if ns else None)


def _cast_place(items, name, deps=()):
    n = len(items)
    mats = [a.shape[-2:] for a, _, _ in items]

    def body(*refs):
        ins, outs, scr, sem = refs[:n], refs[n:2 * n], refs[2 * n:3 * n], refs[3 * n]
        x, y, _ = _mesh_pos()
        cps = []
        for t in range(n):
            scr[t][...] = ins[t][...].astype(scr[t].dtype)
            cp = pltpu.make_async_copy(scr[t], outs[t].at[2 * x + y], sem.at[t])
            cp.start()
            cps.append(cp)
        for cp in cps:
            cp.wait()

    def spec(idx, shape):
        return pl.BlockSpec((None,) * len(idx) + tuple(shape), lambda i: tuple(idx) + (0, 0))

    body, in_specs, args = _add_deps(body, [spec(idx, mat) for (_, idx, _), mat in zip(items, mats)],
                                     [a for a, _, _ in items], deps)
    return pl.pallas_call(
        body, name=name, grid=(1,), in_specs=in_specs, out_specs=[ANY] * n,
        out_shape=[jax.ShapeDtypeStruct((N_SHARD,) + tuple(mat), dt) for (_, _, dt), mat in zip(items, mats)],
        scratch_shapes=[pltpu.VMEM(tuple(mat), dt) for (_, _, dt), mat in zip(items, mats)]
        + [pltpu.SemaphoreType.DMA((n,))],
        compiler_params=_params(),
    )(*args)


def _gather_ici(idx):
    def fn(bufs, ss, rs):
        x, y, c = _mesh_pos()
        pairs = []
        for k, t in enumerate(idx):
            half = bufs[t].shape[1] // 2
            mine = bufs[t].at[2 * x + y, pl.ds(c * half, half), :]
            for j, (cx, cy) in enumerate(_other_chips(x, y)):
                theirs = bufs[t].at[2 * cx + cy, pl.ds(c * half, half), :]
                sem = dict(send_sem=ss.at[3 * k + j], recv_sem=rs.at[3 * k + j],
                           device_id=(cx, cy, c), device_id_type=MESH)
                pairs.append((pltpu.make_async_remote_copy(src_ref=mine, dst_ref=mine, **sem),
                              pltpu.make_async_remote_copy(src_ref=mine, dst_ref=theirs, **sem)))
        return pairs
    return fn


def _gather_d2d(idx):
    def fn(bufs, ss, rs):
        x, y, c = _mesh_pos()
        pairs = []
        for k, t in enumerate(idx):
            half = bufs[t].shape[1] // 2
            for j, (cx, cy) in enumerate(_other_chips(x, y)):
                got = bufs[t].at[2 * cx + cy, pl.ds(c * half, half), :]
                theirs = bufs[t].at[2 * cx + cy, pl.ds((1 - c) * half, half), :]
                sem = dict(send_sem=ss.at[3 * k + j], recv_sem=rs.at[3 * k + j],
                           device_id=(x, y, 1 - c), device_id_type=MESH)
                pairs.append((pltpu.make_async_remote_copy(src_ref=got, dst_ref=got, **sem),
                              pltpu.make_async_remote_copy(src_ref=got, dst_ref=theirs, **sem)))
        return pairs
    return fn


def _alloc(shapes, name):
    def body(*refs):
        pass

    return pl.pallas_call(body, name=name, out_specs=[ANY] * len(shapes),
                          out_shape=[jax.ShapeDtypeStruct(s, d) for s, d in shapes])()


def _send_to_sibling(n):
    def fn(bufs, ss, rs):
        x, y, c = _mesh_pos()
        pairs = []
        for t in range(n):
            src = bufs[t]
            if len(src.shape) == 3:
                half = src.shape[1] // 2
                src = src.at[:, pl.ds((1 - c) * half, half), :]
            cp = pltpu.make_async_remote_copy(src_ref=src, dst_ref=bufs[n + t], send_sem=ss.at[t],
                                              recv_sem=rs.at[t], device_id=(x, y, 1 - c), device_id_type=MESH)
            pairs.append((cp, cp))
        return pairs
    return fn


def _send_to_chips(n):
    def fn(bufs, ss, rs):
        x, y, c = _mesh_pos()
        pairs = []
        for j, (cx, cy) in enumerate(_other_chips(x, y)):
            for t in range(n):
                src = bufs[t].at[j] if len(bufs[t].shape) == 3 else bufs[t]
                cp = pltpu.make_async_remote_copy(src_ref=src, dst_ref=bufs[n + t].at[j], send_sem=ss.at[3 * t + j],
                                                  recv_sem=rs.at[3 * t + j], device_id=(cx, cy, c),
                                                  device_id_type=MESH)
                pairs.append((cp, cp))
        return pairs
    return fn


class _Exchange:
    def __init__(self, name, srcs, land_shapes, fn, n_sems):
        self.name, self.fn = name, fn
        lands = _alloc(land_shapes, name + "_alloc")
        self.n = len(srcs)
        self.bufs, sems, self.token = _split_call(name + "_start", list(srcs) + list(lands),
                                                  starts=[(n_sems, fn)])
        self.sems = sems[0]

    def finish(self, after=()):
        bufs, _, _ = _split_call(self.name + "_wait", self.bufs, waits=[(*self.sems, self.fn)], after=after)
        return bufs[:self.n], bufs[self.n:]


def _row_block(rows, cols, mult=8, limit=3 * 512 * 1024, itemsize=4):
    best = None
    for br in range(mult, rows + 1, mult):
        if rows % br == 0 and br * cols * itemsize <= limit:
            best = br
    assert best is not None, (rows, cols)
    return best


_GROUP_BLOCK_BYTES = 1024 * 1024


def _group_plan(ss):
    plan = []
    for s in ss:
        half, cols = s.shape[-2:]
        br = _row_block(half, cols, mult=16, limit=_GROUP_BLOCK_BYTES)
        plan.append((br, half // br))
    return plan, max(nr for _, nr in plan)


def _chip_partial(gs, ss, ids, name):
    n = len(gs)
    plan, steps = _group_plan(ss)

    def body(ids_ref, *refs):
        for t in range(n):
            refs[2 * n + t][...] = (refs[t][...] + refs[n + t][...]).astype(BF16)

    g_specs, s_specs, o_specs = [], [], []
    for (br, nr), s in zip(plan, ss):
        blk = (None, br, s.shape[2])
        g_specs.append(pl.BlockSpec(
            blk, lambda j, r, ids_ref, nr=nr: (ids_ref[2 + j], ids_ref[0] * nr + jnp.minimum(r, nr - 1), 0)))
        s_specs.append(pl.BlockSpec(blk, lambda j, r, ids_ref, nr=nr: (ids_ref[2 + j], jnp.minimum(r, nr - 1), 0)))
        o_specs.append(pl.BlockSpec(blk, lambda j, r, ids_ref, nr=nr: (j, jnp.minimum(r, nr - 1), 0)))
    return pl.pallas_call(
        body, name=name,
        grid_spec=pltpu.PrefetchScalarGridSpec(num_scalar_prefetch=1, grid=(3, steps),
                                               in_specs=g_specs + s_specs, out_specs=o_specs),
        out_shape=[jax.ShapeDtypeStruct((3,) + s.shape[1:], BF16) for s in ss],
        compiler_params=pltpu.CompilerParams(dimension_semantics=("arbitrary", "arbitrary"),
                                             vmem_limit_bytes=VMEM_LIMIT),
    )(ids, *gs, *ss)


def _chip_sum(gs, ss, qs, ids, name):
    n = len(gs)
    plan, steps = _group_plan(ss)

    def body(ids_ref, *refs):
        for t in range(n):
            q_ref = refs[2 * n + t]
            own = refs[t][...] + refs[n + t][...]
            refs[3 * n + t][...] = (own + q_ref[2].astype(F32)) + (q_ref[0].astype(F32) + q_ref[1].astype(F32))

    g_specs, s_specs, q_specs, o_specs = [], [], [], []
    for (br, nr), s in zip(plan, ss):
        cols = s.shape[2]
        g_specs.append(pl.BlockSpec(
            (None, br, cols), lambda r, ids_ref, nr=nr: (ids_ref[1], ids_ref[0] * nr + jnp.minimum(r, nr - 1), 0)))
        s_specs.append(pl.BlockSpec((None, br, cols), lambda r, ids_ref, nr=nr: (ids_ref[1], jnp.minimum(r, nr - 1), 0)))
        q_specs.append(pl.BlockSpec((3, br, cols), lambda r, ids_ref, nr=nr: (0, jnp.minimum(r, nr - 1), 0)))
        o_specs.append(pl.BlockSpec((br, cols), lambda r, ids_ref, nr=nr: (jnp.minimum(r, nr - 1), 0)))
    return pl.pallas_call(
        body, name=name,
        grid_spec=pltpu.PrefetchScalarGridSpec(num_scalar_prefetch=1, grid=(steps,),
                                               in_specs=g_specs + s_specs + q_specs, out_specs=o_specs),
        out_shape=[jax.ShapeDtypeStruct(s.shape[1:], F32) for s in ss],
        compiler_params=pltpu.CompilerParams(dimension_semantics=("arbitrary",), vmem_limit_bytes=VMEM_LIMIT),
    )(ids, *gs, *ss, *qs)


def _adamw_math(w, g, m, v):
    mn = ADAM_B1 * m + (1.0 - ADAM_B1) * g
    vn = ADAM_B2 * v + (1.0 - ADAM_B2) * (g * g)
    m_hat = mn / (1.0 - ADAM_B1 ** ADAM_STEP)
    v_hat = vn / (1.0 - ADAM_B2 ** ADAM_STEP)
    return -ADAM_LR * (m_hat / (jnp.sqrt(v_hat) + ADAM_EPS) + ADAM_WD * w), mn, vn


def _adamw_halves(w, own, sib, m, v, ids, name, layer=0, n_layers=1, stacked=None):
    C = w.shape[1]
    R = w.shape[0] // n_layers
    half = R // 2
    br = _row_block(half, C)
    nh = half // br
    base = layer * 2 * nh

    def body(ids_ref, w_ref, own_ref, sib_ref, m_ref, v_ref, *rest):
        g_ref, d_ref, mo_ref, vo_ref = rest[-4:]
        is_own = (pl.program_id(0) // nh) == ids_ref[0]
        g = jnp.where(is_own, own_ref[...], sib_ref[...])
        g_ref[...] = g
        d_ref[...], mo_ref[...], vo_ref[...] = _adamw_math(w_ref[...], g, m_ref[...], v_ref[...])

    full = pl.BlockSpec((br, C), lambda r, ids_ref: (base + r, 0))
    own_spec = pl.BlockSpec((br, C), lambda r, ids_ref: (jnp.clip(r - ids_ref[0] * nh, 0, nh - 1), 0))
    sib_spec = pl.BlockSpec((br, C), lambda r, ids_ref: (jnp.clip(r - (1 - ids_ref[0]) * nh, 0, nh - 1), 0))
    in_specs = [full, own_spec, sib_spec, full, full]
    args = [ids, w, own, sib, m, v]
    aliases = {}
    if stacked is not None:
        in_specs += [ANY] * 4
        args += list(stacked)
        aliases = {6 + k: k for k in range(4)}
    return pl.pallas_call(
        body, name=name,
        grid_spec=pltpu.PrefetchScalarGridSpec(
            num_scalar_prefetch=1, grid=(2 * nh,), in_specs=in_specs, out_specs=[full] * 4),
        out_shape=[jax.ShapeDtypeStruct(w.shape, F32)] * 4, input_output_aliases=aliases,
        compiler_params=_params(),
    )(*args)


_PACK_UNIT = 1024


def _pack(arrs):
    flat = []
    for a in arrs:
        f = a.reshape(-1).astype(F32)
        pad = (-f.shape[0]) % _PACK_UNIT
        if pad:
            f = jnp.concatenate([f, jnp.zeros((pad,), F32)])
        flat.append(f)
    return jnp.concatenate(flat).reshape(-1, 128)


def kernel(x, p, norm_mix, norm_ffn, norm_ple, norm_kv, norm_final, a_w_in, a_norm_v, a_w_s, a_b_s, a_w_out, w_kv, b_w_q, b_sinks, b_w_o, f_w_up, f_conv_w, f_conv_b, f_w_down, ple_w_in, ple_w_gate, ple_b_gate, loss_target, m_norm_mix, m_norm_ffn, m_norm_ple, m_norm_kv, m_norm_final, m_a_w_in, m_a_norm_v, m_a_w_s, m_a_b_s, m_a_w_out, m_w_kv, m_b_w_q, m_b_sinks, m_b_w_o, m_f_w_up, m_f_conv_w, m_f_conv_b, m_f_w_down, m_ple_w_in, m_ple_w_gate, m_ple_b_gate, v_norm_mix, v_norm_ffn, v_norm_ple, v_norm_kv, v_norm_final, v_a_w_in, v_a_norm_v, v_a_w_s, v_a_b_s, v_a_w_out, v_w_kv, v_b_w_q, v_b_sinks, v_b_w_o, v_f_w_up, v_f_conv_w, v_f_conv_b, v_f_w_down, v_ple_w_in, v_ple_w_gate, v_ple_b_gate):
    given = dict(locals())

    small_shard = _pack([a_norm_v, f_conv_w])
    pad_rows = (-small_shard.shape[0]) % 16
    if pad_rows:
        small_shard = jnp.concatenate([small_shard, jnp.zeros((pad_rows, 128), F32)])
    groups = [
        [(a_w_in, (0,), BF16), (a_w_out, (0,), BF16), (small_shard, (), F32)],
        [(f_w_up, (0,), BF16), (f_w_down, (0,), BF16)],
        [(ple_w_in, (0,), BF16), (ple_w_gate, (0,), BF16), (w_kv, (), BF16), (b_w_q, (0,), BF16),
         (b_w_o, (0,), BF16), (f_w_up, (1,), BF16), (f_w_down, (1,), BF16), (ple_w_in, (1,), BF16),
         (ple_w_gate, (1,), BF16)],
    ]
    first = list(range(len(groups[0])))
    lands0, sems0, token0 = _split_call("gather_start_g0", _cast_place(groups[0], "cast_place_g0"),
                                        starts=[(3 * len(first), _gather_ici(first))])
    rest, spans, start = [], [], 0
    for gi, items in enumerate(groups[1:], 1):
        rest += _cast_place(items, f"cast_place_g{gi}", deps=(token0,))
        spans.append(list(range(start, start + len(items))))
        start += len(items)
    rest, rest_sems, rest_token = _split_call("gather_start", rest,
                                              starts=[(3 * len(sp), _gather_ici(sp)) for sp in spans])
    group_bufs = [lands0] + [[rest[t] for t in sp] for sp in spans]
    ici_sems = sems0 + rest_sems

    def finish_group(gi, after):
        bufs = group_bufs[gi]
        local = list(range(len(bufs)))
        bufs, d2d_sems, _ = _split_call(f"gather_pass_g{gi}", bufs, waits=[(*ici_sems[gi], _gather_ici(local))],
                                        starts=[(3 * len(local), _gather_d2d(local))], after=after)
        bufs, _, _ = _split_call(f"gather_done_g{gi}", bufs, waits=[(*d2d_sems[0], _gather_d2d(local))])
        return bufs

    def stage0():
        b_in, b_out, b_small = finish_group(0, (rest_token,))
        small_full = b_small.reshape(N_SHARD, -1)
        gv_full = small_full[:, :256].reshape(1, D_MODEL)
        cw_full = small_full[:, _PACK_UNIT:_PACK_UNIT + 2 * 3 * FF_BLK].reshape(N_SHARD, 2, 3, FF_BLK)
        cw_full = jnp.transpose(cw_full, (1, 2, 0, 3)).reshape(2, 3, N_FF)
        return gv_full, cw_full, b_in, b_out.reshape(D_MODEL, D_MODEL)

    def stage1(after):
        b_up, b_dn = finish_group(1, after)
        return b_up, b_dn.reshape(D_FF, D_MODEL)

    def stage2(after):
        pin0, gate0, kv_w, wq, wo, up1, dn1, pin1, gate1 = finish_group(2, after)
        sq = lambda a: a.reshape(D_MODEL, -1)
        return dict(w_pin=[pin0, pin1], w_gate=[sq(gate0), sq(gate1)], w_kv=sq(kv_w), w_q=sq(wq), w_o=sq(wo),
                    w_up1=up1, w_dn1=dn1.reshape(D_FF, D_MODEL))

    dx, (loss, (out_g, out_d, out_m, out_v)) = _local_step(
        x[0], p.reshape(2, -1, PLE_DIM), loss_target[0], norm_mix, norm_ffn, norm_ple, norm_kv, norm_final, a_w_s, a_b_s,
        b_sinks, f_conv_b, ple_b_gate, stage0, stage1, stage2, _Reducer(given))
    weight_names = ['norm_mix', 'norm_ffn', 'norm_ple', 'norm_kv', 'norm_final', 'a_w_in', 'a_norm_v', 'a_w_s',
                    'a_b_s', 'a_w_out', 'w_kv', 'b_w_q', 'b_sinks', 'b_w_o', 'f_w_up', 'f_conv_w', 'f_conv_b',
                    'f_w_down', 'ple_w_in', 'ple_w_gate', 'ple_b_gate']
    return (loss, dx.reshape(x.shape), *[out_g[k] for k in weight_names], *[out_d[k] for k in weight_names],
            *[out_m[k] for k in weight_names], *[out_v[k] for k in weight_names])


def _local_step(xs, p, tgt, norm_mix, norm_ffn, norm_ple, norm_kv, norm_final, a_w_s, a_b_s, b_sinks,
                f_conv_b, ple_b_gate, stage0, stage1, stage2, sched):
    tril = jnp.tril(jnp.ones((CHUNK, CHUNK), F32))
    wsm = (a_w_s[0] * tril[None]).astype(BF16)
    bsb = jnp.broadcast_to(a_b_s[0][:, :, None], (A_GROUPS, CHUNK, CHUNK))
    sinks = b_sinks[0]
    row = lambda a: a.reshape(1, -1)

    gv_full, cw_full, w_in, w_out = stage0()
    h1, zp = _mixer_a_fwd(xs, row(norm_mix[0]), gv_full, wsm, bsb, w_in, w_out)
    w_up0, w_dn0 = stage1((h1,))
    h2, hh0, c0 = _ffn_fwd(h1, row(norm_ffn[0]), cw_full[0], row(f_conv_b[0]), w_up0, w_dn0, 0)
    rest = stage2((h2,))
    w_pin, w_gate, w_kv_f, w_q, w_o = rest['w_pin'], rest['w_gate'], rest['w_kv'], rest['w_q'], rest['w_o']
    w_up = [w_up0, rest['w_up1']]
    w_dn = [w_dn0, rest['w_dn1']]
    h3, kv = _ple_fwd_kv(h2, p, row(norm_ple[0]), row(ple_b_gate[0]), row(norm_kv), w_pin[0], w_gate[0], w_kv_f)
    h4, q, ao, probs, psink = _attn_fwd(h3, row(norm_mix[1]), kv, sinks, w_q, w_o)
    h5, hh1, c1 = _ffn_fwd(h4, row(norm_ffn[1]), cw_full[1], row(f_conv_b[1]), w_up[1], w_dn[1], 1)
    dh6, loss_acc, dn_final = _ple_fwd_final(
        h5, p, tgt, row(norm_ple[1]), row(ple_b_gate[1]), row(norm_final), w_pin[1], w_gate[1])

    def pieces(g):
        return g.reshape(N_SHARD, -1, g.shape[-1])

    dh5, g_pin1, g_gate1, dbg1, dnple1 = _ple_bwd(dh6, h5, p, row(norm_ple[1]), row(ple_b_gate[1]), w_pin[1], w_gate[1], 1)
    early = {('ple_w_in', 1): g_pin1, ('ple_w_gate', 1): pieces(g_gate1)}
    dh4, g_up1, g_dn1, dcw1, dcb1, dnffn1 = _ffn_bwd(
        dh5, h4, hh1, c1, row(norm_ffn[1]), cw_full[1], w_up[1], w_dn[1], 1)
    early['f_w_down', 1] = pieces(g_dn1)
    early['f_w_up', 1] = g_up1
    dh3a, g_wq, g_wo, dkv, dsink, dnmix1 = _attn_bwd(dh4, h3, q, kv, ao, probs, psink, row(norm_mix[1]), w_q, w_o)
    early['b_w_o', 0] = pieces(g_wo)
    early['b_w_q', 0] = pieces(g_wq)
    dh2, g_pin0, g_gate0, dbg0, dnple0, g_wkv, dnkv = _ple_bwd(
        dh3a, h2, p, row(norm_ple[0]), row(ple_b_gate[0]), w_pin[0], w_gate[0], 0,
        kv_args=(h3, dkv, row(norm_kv), w_kv_f))
    early['w_kv', 0] = pieces(g_wkv)
    early['ple_w_in', 0] = g_pin0
    early['ple_w_gate', 0] = pieces(g_gate0)
    deps = sched.early_ready(early)
    dh1, g_up0, g_dn0, dcw0, dcb0, dnffn0 = _ffn_bwd(
        dh2, h1, hh0, c0, row(norm_ffn[0]), cw_full[0], w_up[0], w_dn[0], 0, deps=deps,
        between=lambda part: sched.after_ffn_half((part,)))
    deps = sched.ffn0_ready({('f_w_down', 0): pieces(g_dn0), ('f_w_up', 0): g_up0})
    dx, g_win, g_wout, dws, dbs, dgv, dnmix0 = _mixer_a_bwd(
        dh1, xs, zp, row(norm_mix[0]), gv_full, wsm, bsb, tril, w_in, w_out, deps=deps)
    g_wout = pieces(g_wout)

    small_grads = {
        'norm_mix': jnp.concatenate([dnmix0, dnmix1]), 'norm_ffn': jnp.concatenate([dnffn0, dnffn1]),
        'norm_ple': jnp.concatenate([dnple0, dnple1]), 'norm_kv': dnkv, 'norm_final': dn_final,
        'a_norm_v': dgv, 'a_w_s': dws.reshape(A_GROUPS * CHUNK, CHUNK), 'a_b_s': dbs[:, :, 0],
        'b_sinks': dsink[0:1, :], 'f_conv_w': jnp.concatenate([dcw0, dcw1]),
        'f_conv_b': jnp.concatenate([dcb0, dcb1]), 'ple_b_gate': jnp.concatenate([dbg0, dbg1]),
        'loss': loss_acc,
    }
    outs = sched.finish({('a_w_in', 0): g_win, ('a_w_out', 0): g_wout}, small_grads, (dx,))
    return dx, outs


class _Reducer:
    def __init__(self, given):
        self.given = given
        cx, cy, cc = _mesh_pos()
        self.shard = 2 * cx + cy
        s = self.shard
        self.ids = jnp.stack([cc, s, s ^ 2, s ^ 1, s ^ 3]).astype(jnp.int32)
        self.out = [{}, {}, {}, {}]
        self.stacked = {}

    def _send(self, tag, grads, small=()):
        keys = list(grads)
        srcs = [grads[k] for k in keys] + list(small)
        shapes = [((N_SHARD, g.shape[1] // 2, g.shape[2]), F32) for g in srcs[:len(keys)]]
        shapes += [(s.shape, F32) for s in small]
        return keys, _Exchange(f"send_{tag}", srcs, shapes, _send_to_sibling(len(srcs)), len(srcs))

    def _exchange(self, tag, keys, send, after):
        srcs, lands = send.finish(after)
        n = len(keys)
        parts = _chip_partial(srcs[:n], lands[:n], self.ids, f"chip_partial_{tag}")
        shapes = [(p.shape, BF16) for p in parts]
        if len(srcs) > n:
            small = _small_add(srcs[n:], lands[n:])
            parts += small
            shapes += [((3,) + s.shape, F32) for s in small]
        exch = _Exchange(f"exch_{tag}", parts, shapes, _send_to_chips(len(parts)), 3 * len(parts))
        return (keys, srcs[:n], lands[:n], exch)

    def _swap(self, tag, state, after):
        keys, grads, sib, exch = state
        parts, recv = exch.finish(after)
        n = len(keys)
        own = _chip_sum(grads, sib, recv[:n], self.ids, f"chip_sum_{tag}")
        small_red = _small_sum(parts[n:], recv[n:]) if len(parts) > n else None
        return keys, _Exchange(f"swap_{tag}", own, [(o.shape, F32) for o in own], _send_to_sibling(n), n), small_red

    def _adamw(self, keys, swap, after):
        own, sib = swap.finish(after)
        last = None
        for (name, layer), o, s in zip(keys, own, sib):
            w = self.given[name]
            n_layers = w.shape[0] if w.ndim == 3 else 1
            c2 = w.shape[-1]
            res = _adamw_halves(w.reshape(-1, c2), o, s, self.given['m_' + name].reshape(-1, c2),
                                self.given['v_' + name].reshape(-1, c2), self.ids, f"adamw_{name}{layer}",
                                layer, n_layers, self.stacked.get(name))
            self.stacked[name] = res
            if layer == 0:
                for dst, r in zip(self.out, res):
                    dst[name] = r.reshape(w.shape)
            last = res[0]
        return last

    def early_ready(self, grads):
        self.e_keys, self.e_send = self._send("e", grads)
        return (self.e_send.token,)

    def after_ffn_half(self, after):
        self.e_state = self._exchange("e", self.e_keys, self.e_send, after)
        return (self.e_state[3].token,)

    def ffn0_ready(self, grads):
        _, self.e_swap, _ = self._swap("e", self.e_state, tuple(grads.values()))
        self.f_keys, self.f_send = self._send("f", grads)
        return (self.f_send.token, self.e_swap.token)

    def finish(self, grads, small_grads, after):
        small_names = list(small_grads)
        f_state = self._exchange("f", self.f_keys, self.f_send, after)
        a_keys, a_send = self._send("a", grads, [small_grads[k] for k in small_names])
        a_state = self._exchange("a", a_keys, a_send, (f_state[3].token,))
        e_done = self._adamw(self.e_keys, self.e_swap, (a_state[3].token,))
        f_keys, f_swap, _ = self._swap("f", f_state, (e_done,))
        f_done = self._adamw(f_keys, f_swap, ())
        _, a_swap, small_red = self._swap("a", a_state, (f_done,))
        self._adamw(a_keys, a_swap, ())

        given = self.given
        reduced = dict(zip(small_names, small_red))
        loss = reduced.pop('loss')[0, 0]
        names = list(reduced)
        items = []
        for k in names:
            g = reduced[k]
            cols = g.shape[1] // N_SHARD if k in ('a_norm_v', 'f_conv_w') else g.shape[1]
            view = lambda a: _lane_pad(a.reshape(g.shape[0], -1), cols)
            items.append((view(given[k]), g, view(given['m_' + k]), view(given['v_' + k])))
        res = _adamw_small(items, self.ids)
        for k, four in zip(names, res):
            width = given[k].size // four[0].shape[0]
            for dst, r in zip(self.out, four):
                dst[k] = r[:, :width].reshape(given[k].shape)
        return loss, self.out


def _lane_pad(a, cols):
    return a if a.shape[1] == cols else jnp.pad(a, ((0, 0), (0, cols - a.shape[1])))


def _small_add(a_list, b_list):
    n = len(a_list)

    def body(*refs):
        for t in range(n):
            refs[2 * n + t][...] = refs[t][...] + refs[n + t][...]

    return pl.pallas_call(body, name="chip_partial_small",
                          out_shape=[jax.ShapeDtypeStruct(a.shape, F32) for a in a_list])(*a_list, *b_list)


def _small_sum(parts, recvs):
    n = len(parts)

    def body(*refs):
        for t in range(n):
            q = refs[n + t]
            refs[2 * n + t][...] = (refs[t][...] + q[2]) + (q[0] + q[1])

    return pl.pallas_call(body, name="chip_sum_small",
                          out_shape=[jax.ShapeDtypeStruct(p.shape, F32) for p in parts])(*parts, *recvs)


def _adamw_small(items, ids):
    n = len(items)

    def body(ids_ref, *refs):
        for t in range(n):
            w_ref, g_ref, m_ref, v_ref = refs[4 * t:4 * t + 4]
            g_out, d_ref, mo_ref, vo_ref = refs[4 * n + 4 * t:4 * n + 4 * t + 4]
            g = g_ref[...]
            g_out[...] = g
            d_ref[...], mo_ref[...], vo_ref[...] = _adamw_math(w_ref[...], g, m_ref[...], v_ref[...])

    in_specs, out_specs, out_shape, args = [], [], [], []
    for w, g, m, v in items:
        full = pl.BlockSpec(w.shape, lambda i, ids_ref: (0, 0))
        g_spec = full if g.shape == w.shape else pl.BlockSpec(w.shape, lambda i, ids_ref: (0, ids_ref[1]))
        in_specs += [full, g_spec, full, full]
        out_specs += [full] * 4
        out_shape += [jax.ShapeDtypeStruct(w.shape, F32)] * 4
        args += [w, g, m, v]
    res = pl.pallas_call(
        body, name="adamw_small",
        grid_spec=pltpu.PrefetchScalarGridSpec(num_scalar_prefetch=1, grid=(1,), in_specs=in_specs,
                                               out_specs=out_specs),
        out_shape=out_shape, compiler_params=_params(),
    )(ids, *args)
    return [res[4 * t:4 * t + 4] for t in range(n)]
```

```python
import functools
import math

import numpy as np
import jax
import jax.numpy as jnp
from jax import lax
from jax.experimental import pallas as pl
from jax.experimental.pallas import tpu as pltpu

F32 = jnp.float32
BF16 = jnp.bfloat16

D_MODEL = 1024
CHUNK = 128
A_GROUPS = 8
HEAD_DIM = 64
N_Q_HEADS = 16
N_KV_HEADS = 4
GQA_GROUP = N_Q_HEADS // N_KV_HEADS
KV_DIM = N_KV_HEADS * HEAD_DIM
BLOCK = 128
D_FF = 2816
N_FF = 2 * D_FF
FF_BLK = N_FF // 4
PLE_DIM = 256
EPS = 1e-6
NEG = -1e30
N_SHARD = 4

ADAM_LR = 0.001
ADAM_B1 = 0.9
ADAM_B2 = 0.999
ADAM_EPS = 1e-08
ADAM_WD = 0.01
ADAM_STEP = 10

VMEM_LIMIT = 60 * 1024 * 1024
MESH = pl.DeviceIdType.MESH
ANY = pl.BlockSpec(memory_space=pl.ANY)
SMEM = pl.BlockSpec(memory_space=pltpu.SMEM)

_SLOPES = [float(np.float32(2.0 ** (-8.0 * (h + 1) / N_Q_HEADS))) for h in range(N_Q_HEADS)]


def _dot(a, b):
    return jnp.dot(a, b, preferred_element_type=F32)


def _dot_nt(a, b):
    return lax.dot_general(a, b, (((1,), (1,)), ((), ())), preferred_element_type=F32)


def _dot_tn(a, b):
    return lax.dot_general(a, b, (((0,), (0,)), ((), ())), preferred_element_type=F32)


def _rms(x, g):
    r = lax.rsqrt(jnp.mean(x * x, axis=-1, keepdims=True) + EPS)
    xh = x * r
    return xh * g, xh, r


def _rms_bwd(dy, xh, r, g):
    dxh = dy * g
    dg = jnp.sum(dy * xh, axis=0, keepdims=True)
    dx = r * (dxh - xh * jnp.mean(dxh * xh, axis=-1, keepdims=True))
    return dx, dg


_GELU_C = math.sqrt(2.0 / math.pi)


def _gelu(x):
    t = jnp.tanh(_GELU_C * (x + 0.044715 * (x * x * x)))
    return 0.5 * x * (1.0 + t)


def _gelu_grad(x):
    x2 = x * x
    t = jnp.tanh(_GELU_C * (x + 0.044715 * (x2 * x)))
    return 0.5 * (1.0 + t) + 0.5 * x * (1.0 - t * t) * (_GELU_C * (1.0 + 3.0 * 0.044715 * x2))


def _sigmoid(x):
    return 0.5 * jnp.tanh(0.5 * x) + 0.5


def _load_once(pairs, sem):
    @pl.when(pl.program_id(0) == 0)
    def _():
        cps = [pltpu.make_async_copy(s, d, sem.at[i]) for i, (s, d) in enumerate(pairs)]
        for cp in cps:
            cp.start()
        for cp in cps:
            cp.wait()


def _params(n_axes=1, vmem=VMEM_LIMIT):
    return pltpu.CompilerParams(dimension_semantics=("arbitrary",) * n_axes, vmem_limit_bytes=vmem)


def _row_spec(tm, n, rev_nt=None):
    if rev_nt is None:
        return pl.BlockSpec((tm, n), lambda i: (i, 0))
    return pl.BlockSpec((tm, n), lambda i: (rev_nt - 1 - i, 0))


def _const_spec(shape):
    nd = len(shape)
    return pl.BlockSpec(shape, lambda i: (0,) * nd)


def _add_deps(body, in_specs, args, deps):
    nd = len(deps)
    if nd == 0:
        return body, list(in_specs), list(args)

    def wrapped(*refs):
        return body(*refs[nd:])

    return wrapped, [ANY] * nd + list(in_specs), list(deps) + list(args)


def _zero_first(refs):
    @pl.when(pl.program_id(0) == 0)
    def _():
        for r in refs:
            r[...] = jnp.zeros(r.shape, r.dtype)


def _mixer_a_fwd(x, nmix, gv, wsm, bsb, w_in, w_out):
    T = x.shape[0]
    tm = min(512, T)
    nt = T // tm
    nw = 2 * D_MODEL // N_SHARD

    def body(x_ref, nmix_ref, gv_ref, ws_ref, bsb_ref, w_in_hbm, w_out_hbm,
             h1_ref, zp_ref, w_in_v, w_out_v, gated_v, sem):
        _load_once([(w_in_hbm, w_in_v), (w_out_hbm, w_out_v)], sem)
        xv = x_ref[...]
        xn = _rms(xv, nmix_ref[...])[0].astype(BF16)
        for j in range(N_SHARD):
            zp_ref[:, j * nw:(j + 1) * nw] = _dot(xn, w_in_v[j])
        z = _gelu(zp_ref[...])
        u = z[:, :D_MODEL]
        vn = _rms(z[:, D_MODEL:], gv_ref[...])[0].astype(BF16)
        for c in range(tm // CHUNK):
            rows = slice(c * CHUNK, (c + 1) * CHUNK)
            for h in range(A_GROUPS):
                cols = slice(h * CHUNK, (h + 1) * CHUNK)
                s = _dot(ws_ref[h], vn[rows, cols]) + bsb_ref[h]
                gated_v[rows, cols] = (u[rows, cols] * s).astype(BF16)
        h1_ref[...] = xv + _dot(gated_v[...], w_out_v[...])

    return pl.pallas_call(
        body, name="mixer_a_fwd", grid=(nt,),
        in_specs=[_row_spec(tm, D_MODEL), _const_spec((1, D_MODEL)), _const_spec((1, D_MODEL)),
                  _const_spec((A_GROUPS, CHUNK, CHUNK)), _const_spec((A_GROUPS, CHUNK, CHUNK)), ANY, ANY],
        out_specs=[_row_spec(tm, D_MODEL), _row_spec(tm, 2 * D_MODEL)],
        out_shape=[jax.ShapeDtypeStruct((T, D_MODEL), F32), jax.ShapeDtypeStruct((T, 2 * D_MODEL), F32)],
        scratch_shapes=[pltpu.VMEM((N_SHARD, D_MODEL, nw), BF16), pltpu.VMEM((D_MODEL, D_MODEL), BF16),
                        pltpu.VMEM((tm, D_MODEL), BF16), pltpu.SemaphoreType.DMA((2,))],
        compiler_params=_params(),
    )(x, nmix, gv, wsm, bsb, w_in, w_out)


def _mixer_a_bwd(dh, x, zp, nmix, gv, wsm, bsb, tril, w_in, w_out, deps=()):
    T = x.shape[0]
    tm = min(256, T)
    nt = T // tm
    nw = 2 * D_MODEL // N_SHARD

    def body(dh_ref, x_ref, zp_ref, nmix_ref, gv_ref, ws_ref, bsb_ref, tril_ref, w_in_hbm, w_out_hbm,
             dx_ref, dwin_ref, dwout_ref, dws_ref, dbs_ref, dgv_ref, dnmix_ref,
             w_in_v, w_out_v, du_v, dvn_v, dbs_v, gated_ref, sem):
        _load_once([(w_in_hbm, w_in_v), (w_out_hbm, w_out_v)], sem)
        _zero_first([dws_ref, dbs_v, dgv_ref, dnmix_ref, dwin_ref, dwout_ref])
        i = pl.program_id(0)
        dhv = dh_ref[...]
        dhb = dhv.astype(BF16)
        xv = x_ref[...]
        xn, xh, r = _rms(xv, nmix_ref[...])
        xnb = xn.astype(BF16)
        zpv = zp_ref[...]
        z = _gelu(zpv)
        u = z[:, :D_MODEL]
        vn_f, vh, rv = _rms(z[:, D_MODEL:], gv_ref[...])
        vn = vn_f.astype(BF16)
        dgated = _dot_nt(dhb, w_out_v[...])
        for c in range(tm // CHUNK):
            rows = slice(c * CHUNK, (c + 1) * CHUNK)
            for h in range(A_GROUPS):
                cols = slice(h * CHUNK, (h + 1) * CHUNK)
                vn_h = vn[rows, cols]
                s = _dot(ws_ref[h], vn_h) + bsb_ref[h]
                dgt = dgated[rows, cols]
                u_h = u[rows, cols]
                gated_ref[rows, cols] = (u_h * s).astype(BF16)
                du_v[rows, cols] = dgt * s
                ds = dgt * u_h
                dsb = ds.astype(BF16)
                dws_ref[h] += _dot_nt(dsb, vn_h)
                dbs_v[h] += ds
                dvn_v[rows, cols] = _dot_tn(ws_ref[h], dsb)
        dwout_ref[...] += _dot_tn(gated_ref[...], dhb)
        dv, dgv = _rms_bwd(dvn_v[...], vh, rv, gv_ref[...])
        dgv_ref[...] += dgv
        dzu = (du_v[...] * _gelu_grad(zpv[:, :D_MODEL])).astype(BF16)
        dzv = (dv * _gelu_grad(zpv[:, D_MODEL:])).astype(BF16)
        dzs = (dzu[:, :nw], dzu[:, nw:], dzv[:, :nw], dzv[:, nw:])
        dxn = jnp.zeros((tm, D_MODEL), F32)
        for j in range(N_SHARD):
            dxn = dxn + _dot_nt(dzs[j], w_in_v[j])
            dwin_ref[j] += _dot_tn(xnb, dzs[j])
        dxx, dn = _rms_bwd(dxn, xh, r, nmix_ref[...])
        dnmix_ref[...] += dn
        dx_ref[...] = dhv + dxx

        @pl.when(i == nt - 1)
        def _():
            for h in range(A_GROUPS):
                dws_ref[h] = dws_ref[h] * tril_ref[...]
                dbs_ref[h] = jnp.broadcast_to(jnp.sum(dbs_v[h], axis=1, keepdims=True), (CHUNK, CHUNK))

    grp = (A_GROUPS, CHUNK, CHUNK)
    body, in_specs, args = _add_deps(
        body, [_row_spec(tm, D_MODEL), _row_spec(tm, D_MODEL), _row_spec(tm, 2 * D_MODEL),
               _const_spec((1, D_MODEL)), _const_spec((1, D_MODEL)), _const_spec(grp), _const_spec(grp),
               _const_spec((CHUNK, CHUNK)), ANY, ANY],
        [dh, x, zp, nmix, gv, wsm, bsb, tril, w_in, w_out], deps)
    return pl.pallas_call(
        body, name="mixer_a_bwd", grid=(nt,), in_specs=in_specs,
        out_specs=[_row_spec(tm, D_MODEL), _const_spec((N_SHARD, D_MODEL, nw)), _const_spec((D_MODEL, D_MODEL)),
                   _const_spec(grp), _const_spec(grp), _const_spec((1, D_MODEL)), _const_spec((1, D_MODEL))],
        out_shape=[jax.ShapeDtypeStruct((T, D_MODEL), F32), jax.ShapeDtypeStruct((N_SHARD, D_MODEL, nw), F32),
                   jax.ShapeDtypeStruct((D_MODEL, D_MODEL), F32),
                   jax.ShapeDtypeStruct(grp, F32), jax.ShapeDtypeStruct(grp, F32),
                   jax.ShapeDtypeStruct((1, D_MODEL), F32), jax.ShapeDtypeStruct((1, D_MODEL), F32)],
        scratch_shapes=[pltpu.VMEM((N_SHARD, D_MODEL, nw), BF16), pltpu.VMEM((D_MODEL, D_MODEL), BF16),
                        pltpu.VMEM((tm, D_MODEL), F32), pltpu.VMEM((tm, D_MODEL), F32),
                        pltpu.VMEM(grp, F32), pltpu.VMEM((tm, D_MODEL), BF16), pltpu.SemaphoreType.DMA((2,))],
        compiler_params=_params(),
    )(*args)


def _load_ffn_weights(w_up_hbm, w_dn_hbm, layer, w_up_v, w_dn_v, sem):
    _load_once([(w_up_hbm, w_up_v), (w_dn_hbm, w_dn_v)], sem)


def _ffn_fwd(h, nffn, cw, cb, w_up, w_dn, layer):
    T = h.shape[0]
    tm = min(256, T)
    nt = T // tm

    def body(h_ref, n_ref, cw_ref, cb_ref, w_up_hbm, w_dn_hbm, out_ref, hh_ref, gate_ref,
             w_up_v, w_dn_v, carry_v, sem):
        _load_ffn_weights(w_up_hbm, w_dn_hbm, layer, w_up_v, w_dn_v, sem)
        _zero_first([carry_v])
        xv = h_ref[...]
        xf = _rms(xv, n_ref[...])[0].astype(BF16)
        acc = xv
        for j in range(2):
            cs = []
            for blk in (j, j + 2):
                cols = slice(blk * FF_BLK, (blk + 1) * FF_BLK)
                hh = _dot(xf, w_up_v[blk])
                hh_ref[:, cols] = hh.astype(BF16)
                ext = jnp.concatenate([carry_v[blk], hh], axis=0)
                carry_v[blk] = hh[tm - 8:, :]
                s1 = pltpu.roll(ext, 1, 0)[8:]
                s2 = pltpu.roll(ext, 2, 0)[8:]
                cs.append(cb_ref[:, cols] + cw_ref[0:1, cols] * s2 + cw_ref[1:2, cols] * s1
                          + cw_ref[2:3, cols] * hh)
            cg, cu = cs
            sg = _sigmoid(cg)
            sil = cg * sg
            act = (sil * cu).astype(BF16)
            for kind, val in enumerate((cu, sil, sg * (1.0 + cg * (1.0 - sg)))):
                gate_ref[:, kind * D_FF + j * FF_BLK:kind * D_FF + (j + 1) * FF_BLK] = val.astype(BF16)
            gate_ref[:, 3 * D_FF + j * FF_BLK:3 * D_FF + (j + 1) * FF_BLK] = act
            acc = acc + _dot(act, w_dn_v[j * FF_BLK:(j + 1) * FF_BLK, :])
        out_ref[...] = acc

    return pl.pallas_call(
        body, name=f"ffn_fwd{layer}", grid=(nt,),
        in_specs=[_row_spec(tm, D_MODEL), _const_spec((1, D_MODEL)), _const_spec((3, N_FF)),
                  _const_spec((1, N_FF)), ANY, ANY],
        out_specs=[_row_spec(tm, D_MODEL), _row_spec(tm, N_FF), _row_spec(tm, 4 * D_FF)],
        out_shape=[jax.ShapeDtypeStruct((T, D_MODEL), F32), jax.ShapeDtypeStruct((T, N_FF), BF16),
                   jax.ShapeDtypeStruct((T, 4 * D_FF), BF16)],
        scratch_shapes=[pltpu.VMEM((N_SHARD, D_MODEL, FF_BLK), BF16), pltpu.VMEM((D_FF, D_MODEL), BF16),
                        pltpu.VMEM((N_SHARD, 8, FF_BLK), F32), pltpu.SemaphoreType.DMA((2 * N_SHARD,))],
        compiler_params=_params(),
    )(h, nffn, cw, cb, w_up, w_dn)


def _wgrad(a, b, bn, col_sharded, name, deps=(), a_cols=None):
    T = a.shape[0]
    K, a_blk = (a.shape[1], 0) if a_cols is None else a_cols
    N = b.shape[1]
    tt = min(2048, T)
    nn, ntt = N // bn, T // tt
    kr = K // N_SHARD

    def body(a_ref, b_ref, o_ref):
        @pl.when(pl.program_id(1) == 0)
        def _():
            o_ref[...] = jnp.zeros(o_ref.shape, F32)
        d = _dot_tn(a_ref[...].astype(BF16), b_ref[...].astype(BF16))
        if col_sharded:
            o_ref[...] += d
        else:
            for j in range(N_SHARD):
                o_ref[j] += d[j * kr:(j + 1) * kr]

    if col_sharded:
        assert nn == N_SHARD
        out_spec = pl.BlockSpec((None, K, bn), lambda n, t: (n, 0, 0))
        out_shape = jax.ShapeDtypeStruct((N_SHARD, K, bn), F32)
    else:
        out_spec = pl.BlockSpec((N_SHARD, kr, bn), lambda n, t: (0, 0, n))
        out_shape = jax.ShapeDtypeStruct((N_SHARD, kr, N), F32)
    body, in_specs, args = _add_deps(
        body, [pl.BlockSpec((tt, K), lambda n, t: (t, a_blk)), pl.BlockSpec((tt, bn), lambda n, t: (t, n))],
        [a, b], deps)
    return pl.pallas_call(
        body, name=name, grid=(nn, ntt), in_specs=in_specs, out_specs=out_spec, out_shape=out_shape,
        compiler_params=pltpu.CompilerParams(dimension_semantics=("arbitrary",) * 2, vmem_limit_bytes=VMEM_LIMIT),
    )(*args)


def _ffn_bwd(dh, h, hh, gate, nffn, cw, w_up, w_dn, layer, deps=(), between=None):
    T = h.shape[0]
    tm = min(256, T)
    nt = T // tm

    def body(dh_ref, h_ref, hh_ref, u_ref, sil_ref, dsil_ref, n_ref, cw_ref, w_up_hbm, w_dn_hbm,
             dhin_ref, dhh_ref, xf_ref, dcw_ref, dcb_ref, dn_ref,
             w_up_v, w_dn_v, carry_v, sem):
        _load_ffn_weights(w_up_hbm, w_dn_hbm, layer, w_up_v, w_dn_v, sem)
        _zero_first([carry_v, dcw_ref, dcb_ref, dn_ref])
        dout = dh_ref[...]
        doutb = dout.astype(BF16)
        xf_f, xh, r = _rms(h_ref[...], n_ref[...])
        xf_ref[...] = xf_f.astype(BF16)
        dxf = jnp.zeros((tm, D_MODEL), F32)
        for j in range(2):
            blks = (j, j + 2)
            pair = slice(j * FF_BLK, (j + 1) * FF_BLK)
            dact = _dot_nt(doutb, w_dn_v[pair, :])
            dcs = (dact * u_ref[:, pair].astype(F32) * dsil_ref[:, pair].astype(F32),
                   dact * sil_ref[:, pair].astype(F32))
            for blk, dc in zip(blks, dcs):
                cols = slice(blk * FF_BLK, (blk + 1) * FF_BLK)
                hhv = hh_ref[:, cols].astype(F32)
                ext = jnp.concatenate([dc, carry_v[blk]], axis=0)
                carry_v[blk] = dc[:8, :]
                n = tm + 8
                a1 = pltpu.roll(ext, n - 1, 0)[:tm]
                a2 = pltpu.roll(ext, n - 2, 0)[:tm]
                dcb_ref[:, cols] += jnp.sum(dc, axis=0, keepdims=True)
                dcw_ref[0:1, cols] += jnp.sum(a2 * hhv, axis=0, keepdims=True)
                dcw_ref[1:2, cols] += jnp.sum(a1 * hhv, axis=0, keepdims=True)
                dcw_ref[2:3, cols] += jnp.sum(dc * hhv, axis=0, keepdims=True)
                dhh = (cw_ref[2:3, cols] * dc + cw_ref[1:2, cols] * a1 + cw_ref[0:1, cols] * a2).astype(BF16)
                dhh_ref[:, cols] = dhh
                dxf = dxf + _dot_nt(dhh, w_up_v[blk])
        dxx, dn = _rms_bwd(dxf, xh, r, n_ref[...])
        dn_ref[...] += dn
        dhin_ref[...] = dout + dxx

    rev = functools.partial(_row_spec, rev_nt=nt)

    def kind(k):
        return pl.BlockSpec((tm, D_FF), lambda i: (nt - 1 - i, k))

    body, in_specs, args = _add_deps(
        body, [rev(tm, D_MODEL), rev(tm, D_MODEL), rev(tm, N_FF), kind(0), kind(1), kind(2),
               _const_spec((1, D_MODEL)), _const_spec((3, N_FF)), ANY, ANY],
        [dh, h, hh, gate, gate, gate, nffn, cw, w_up, w_dn], deps)
    dhin, dhh, xf, dcw, dcb, dn = pl.pallas_call(
        body, name=f"ffn_bwd{layer}", grid=(nt,), in_specs=in_specs,
        out_specs=[rev(tm, D_MODEL), rev(tm, N_FF), rev(tm, D_MODEL),
                   _const_spec((3, N_FF)), _const_spec((1, N_FF)), _const_spec((1, D_MODEL))],
        out_shape=[jax.ShapeDtypeStruct((T, D_MODEL), F32),
                   jax.ShapeDtypeStruct((T, N_FF), BF16), jax.ShapeDtypeStruct((T, D_MODEL), BF16),
                   jax.ShapeDtypeStruct((3, N_FF), F32), jax.ShapeDtypeStruct((1, N_FF), F32),
                   jax.ShapeDtypeStruct((1, D_MODEL), F32)],
        scratch_shapes=[pltpu.VMEM((N_SHARD, D_MODEL, FF_BLK), BF16), pltpu.VMEM((D_FF, D_MODEL), BF16),
                        pltpu.VMEM((N_SHARD, 8, FF_BLK), F32), pltpu.SemaphoreType.DMA((2 * N_SHARD,))],
        compiler_params=_params(),
    )(*args)
    deps2 = between(dhin) if between is not None else ()
    dwdn = _wgrad(gate, dh, D_MODEL // 2, False, f"wgrad_ffn_down{layer}", deps=deps2, a_cols=(D_FF, 3))
    dwup = _wgrad(xf, dhh, FF_BLK, True, f"wgrad_ffn_up{layer}", deps=deps2)
    return dhin, dwup, dwdn, dcw, dcb, dn


def _load_ple_weights(w_pin_hbm, w_gate_hbm, layer, w_pin_v, w_gate_v, sem, extra=()):
    _load_once([(w_pin_hbm, w_pin_v), (w_gate_hbm, w_gate_v)] + list(extra), sem)


def _p_spec(tm, layer):
    return pl.BlockSpec((None, tm, PLE_DIM), lambda i: (layer, i, 0))


def _ple_terms(xv, p_ref, n_ref, bg_ref, w_pin_v, w_gate_v, pe_v, a_ref, saved):
    pw = D_MODEL // N_SHARD
    xg, xh, r = _rms(xv, n_ref[...])
    xgb = xg.astype(BF16)
    if saved:
        gate = _sigmoid(a_ref[...].astype(F32))
    else:
        a = _dot(xgb, w_gate_v[...]) + bg_ref[...]
        a_ref[...] = a.astype(BF16)
        gate = _sigmoid(a)
    pb = p_ref[...].astype(BF16)
    for j in range(N_SHARD):
        pe_v[:, j * pw:(j + 1) * pw] = _dot(pb, w_pin_v[j])
    pe = pe_v[...]
    return pe * gate, pe, gate, xgb, xh, r


def _ple_fwd_kv(h, p, nple, bg, nkv, w_pin, w_gate, w_kv):
    T = h.shape[0]
    tm = min(512, T)
    nt = T // tm
    pw = D_MODEL // N_SHARD

    def body(h_ref, p_ref, n_ref, bg_ref, nkv_ref, w_pin_hbm, w_gate_hbm, w_kv_hbm,
             out_ref, kv_ref, a_ref, w_pin_v, w_gate_v, w_kv_v, pe_v, sem):
        _load_ple_weights(w_pin_hbm, w_gate_hbm, 0, w_pin_v, w_gate_v, sem, [(w_kv_hbm, w_kv_v)])
        xv = h_ref[...]
        hn = xv + _ple_terms(xv, p_ref, n_ref, bg_ref, w_pin_v, w_gate_v, pe_v, a_ref, False)[0]
        out_ref[...] = hn
        kvn = _rms(hn, nkv_ref[...])[0].astype(BF16)
        kv_ref[...] = _dot(kvn, w_kv_v[...]).astype(BF16)

    vec = _const_spec((1, D_MODEL))
    return pl.pallas_call(
        body, name="ple_fwd0", grid=(nt,),
        in_specs=[_row_spec(tm, D_MODEL), _p_spec(tm, 0), vec, vec, vec, ANY, ANY, ANY],
        out_specs=[_row_spec(tm, D_MODEL), _row_spec(tm, 2 * KV_DIM), _row_spec(tm, D_MODEL)],
        out_shape=[jax.ShapeDtypeStruct((T, D_MODEL), F32), jax.ShapeDtypeStruct((T, 2 * KV_DIM), BF16),
                   jax.ShapeDtypeStruct((T, D_MODEL), BF16)],
        scratch_shapes=[pltpu.VMEM((N_SHARD, PLE_DIM, pw), BF16), pltpu.VMEM((D_MODEL, D_MODEL), BF16),
                        pltpu.VMEM((D_MODEL, 2 * KV_DIM), BF16), pltpu.VMEM((tm, D_MODEL), F32),
                        pltpu.SemaphoreType.DMA((2 * N_SHARD + 1,))],
        compiler_params=_params(),
    )(h, p, nple, bg, nkv, w_pin, w_gate, w_kv)


def _ple_fwd_final(h, p, tgt, nple, bg, nfin, w_pin, w_gate):
    T = h.shape[0]
    tm = min(512, T)
    nt = T // tm
    pw = D_MODEL // N_SHARD

    def body(h_ref, p_ref, t_ref, n_ref, bg_ref, nf_ref, w_pin_hbm, w_gate_hbm,
             dh_ref, loss_ref, dnf_ref, a_ref, w_pin_v, w_gate_v, pe_v, sem):
        _load_ple_weights(w_pin_hbm, w_gate_hbm, 1, w_pin_v, w_gate_v, sem)
        _zero_first([loss_ref, dnf_ref])
        xv = h_ref[...]
        hn = xv + _ple_terms(xv, p_ref, n_ref, bg_ref, w_pin_v, w_gate_v, pe_v, a_ref, False)[0]
        y, yh, r = _rms(hn, nf_ref[...])
        diff = y - t_ref[...]
        loss_ref[...] += 0.5 * jnp.sum(jnp.mean(diff * diff, axis=-1, keepdims=True))
        dy = diff * (1.0 / D_MODEL)
        dhn, dnf = _rms_bwd(dy, yh, r, nf_ref[...])
        dnf_ref[...] += dnf
        dh_ref[...] = dhn

    vec = _const_spec((1, D_MODEL))
    return pl.pallas_call(
        body, name="ple_fwd1", grid=(nt,),
        in_specs=[_row_spec(tm, D_MODEL), _p_spec(tm, 1), _row_spec(tm, D_MODEL), vec, vec, vec, ANY, ANY],
        out_specs=[_row_spec(tm, D_MODEL), _const_spec((8, 128)), vec, _row_spec(tm, D_MODEL)],
        out_shape=[jax.ShapeDtypeStruct((T, D_MODEL), F32), jax.ShapeDtypeStruct((8, 128), F32),
                   jax.ShapeDtypeStruct((1, D_MODEL), F32), jax.ShapeDtypeStruct((T, D_MODEL), BF16)],
        scratch_shapes=[pltpu.VMEM((N_SHARD, PLE_DIM, pw), BF16), pltpu.VMEM((D_MODEL, D_MODEL), BF16),
                        pltpu.VMEM((tm, D_MODEL), F32), pltpu.SemaphoreType.DMA((2 * N_SHARD,))],
        compiler_params=_params(),
    )(h, p, tgt, nple, bg, nfin, w_pin, w_gate)


def _ple_bwd(dh, hb, p, a, nple, w_pin, w_gate, layer, kv_args=None):
    T = hb.shape[0]
    tm = min(512, T)
    nt = T // tm
    with_kv = kv_args is not None
    pw = D_MODEL // N_SHARD

    def body(*refs):
        if with_kv:
            (dh_ref, hb_ref, p_ref, a_ref, n_ref, w_pin_hbm, w_gate_hbm, hc_ref, dkv_ref, nkv_ref, w_kv_hbm,
             dhb_ref, dwpin_ref, dwgate_ref, dbg_ref, dn_ref, dwkv_ref, dnkv_ref,
             w_pin_v, w_gate_v, pe_v, w_kv_v, sem) = refs
        else:
            (dh_ref, hb_ref, p_ref, a_ref, n_ref, w_pin_hbm, w_gate_hbm,
             dhb_ref, dwpin_ref, dwgate_ref, dbg_ref, dn_ref, w_pin_v, w_gate_v, pe_v, sem) = refs
        pairs = [(w_pin_hbm, w_pin_v), (w_gate_hbm, w_gate_v)]
        if with_kv:
            pairs.append((w_kv_hbm, w_kv_v))
        _load_once(pairs, sem)
        _zero_first([dwpin_ref, dwgate_ref, dbg_ref, dn_ref] + ([dwkv_ref, dnkv_ref] if with_kv else []))
        do = dh_ref[...]
        if with_kv:
            dkvb = dkv_ref[...].astype(BF16)
            dkvn = _dot_nt(dkvb, w_kv_v[...])
            kvn, kh, kr = _rms(hc_ref[...], nkv_ref[...])
            dwkv_ref[...] += _dot_tn(kvn.astype(BF16), dkvb)
            dk, dnkv = _rms_bwd(dkvn, kh, kr, nkv_ref[...])
            dnkv_ref[...] += dnkv
            do = do + dk
        _, pe, gate, xgb, xh, r = _ple_terms(hb_ref[...], p_ref, n_ref, None, w_pin_v, w_gate_v, pe_v, a_ref, True)
        dpe = (do * gate).astype(BF16)
        pb = p_ref[...].astype(BF16)
        for j in range(N_SHARD):
            dwpin_ref[j] += _dot_tn(pb, dpe[:, j * pw:(j + 1) * pw])
        da = do * pe * (gate * (1.0 - gate))
        dab = da.astype(BF16)
        dbg_ref[...] += jnp.sum(da, axis=0, keepdims=True)
        dxg = _dot_nt(dab, w_gate_v[...])
        dwgate_ref[...] += _dot_tn(xgb, dab)
        dxx, dn = _rms_bwd(dxg, xh, r, n_ref[...])
        dn_ref[...] += dn
        dhb_ref[...] = do + dxx

    vec = _const_spec((1, D_MODEL))
    row = _row_spec(tm, D_MODEL)
    in_specs = [row, row, _p_spec(tm, layer), row, vec, ANY, ANY]
    args = [dh, hb, p, a, nple, w_pin, w_gate]
    out_specs = [row, _const_spec((N_SHARD, PLE_DIM, pw)), _const_spec((D_MODEL, D_MODEL)), vec, vec]
    out_shape = [jax.ShapeDtypeStruct((T, D_MODEL), F32), jax.ShapeDtypeStruct((N_SHARD, PLE_DIM, pw), F32),
                 jax.ShapeDtypeStruct((D_MODEL, D_MODEL), F32),
                 jax.ShapeDtypeStruct((1, D_MODEL), F32), jax.ShapeDtypeStruct((1, D_MODEL), F32)]
    scratch = [pltpu.VMEM((N_SHARD, PLE_DIM, pw), BF16), pltpu.VMEM((D_MODEL, D_MODEL), BF16),
               pltpu.VMEM((tm, D_MODEL), F32)]
    if with_kv:
        hc, dkv, nkv, w_kv = kv_args
        in_specs += [row, _row_spec(tm, 2 * KV_DIM), vec, ANY]
        args += [hc, dkv, nkv, w_kv]
        out_specs += [_const_spec((D_MODEL, 2 * KV_DIM)), vec]
        out_shape += [jax.ShapeDtypeStruct((D_MODEL, 2 * KV_DIM), F32), jax.ShapeDtypeStruct((1, D_MODEL), F32)]
        scratch.append(pltpu.VMEM((D_MODEL, 2 * KV_DIM), BF16))
    scratch.append(pltpu.SemaphoreType.DMA((3,)))
    return pl.pallas_call(
        body, name=f"ple_bwd{layer}", grid=(nt,), in_specs=in_specs, out_specs=out_specs,
        out_shape=out_shape, scratch_shapes=scratch, compiler_params=_params(),
    )(*args)


GROUP_ROWS = GQA_GROUP * BLOCK


def _stack_heads(x, kh):
    return jnp.concatenate([x[:, (kh * GQA_GROUP + g) * HEAD_DIM:(kh * GQA_GROUP + g + 1) * HEAD_DIM]
                            for g in range(GQA_GROUP)], axis=0)


def _attn_fwd(h, nmix, kv, sinks, w_q, w_o):
    T = h.shape[0]
    tm = min(512, T)
    nt = T // tm
    nb = tm // BLOCK

    def body(h_ref, n_ref, kv_ref, kvp_ref, sink_ref, w_q_hbm, w_o_hbm,
             out_ref, q_ref, ao_ref, p_ref, psink_ref, w_q_v, w_o_v, kvs_v, sem):
        _load_once([(w_q_hbm, w_q_v), (w_o_hbm, w_o_v)], sem)
        ti = pl.program_id(0)
        xv = h_ref[...]
        xn = _rms(xv, n_ref[...])[0].astype(BF16)
        q_ref[...] = (_dot(xn, w_q_v[...]) * (HEAD_DIM ** -0.5)).astype(BF16)
        kvs_v[0:BLOCK, :] = kvp_ref[...]
        kvs_v[BLOCK:, :] = kv_ref[...]
        lane = lax.broadcasted_iota(jnp.int32, (BLOCK, 128), 1)
        ii = lax.broadcasted_iota(jnp.int32, (BLOCK, 2 * BLOCK), 0)
        jj = lax.broadcasted_iota(jnp.int32, (BLOCK, 2 * BLOCK), 1)
        dist = ii + BLOCK - jj
        inband = (dist >= 0) & (dist < BLOCK)
        distf = dist.astype(F32)

        def blk_body(b, carry):
            r0 = pl.multiple_of(b * BLOCK, BLOCK)
            valid = inband & ((jj >= BLOCK) | jnp.logical_not(jnp.logical_and(ti == 0, b == 0)))
            qb = q_ref[pl.ds(r0, BLOCK), :]
            band = kvs_v[pl.ds(r0, 2 * BLOCK), :]
            psink_mat = jnp.zeros((BLOCK, 128), F32)
            outs = []
            for hq in range(N_Q_HEADS):
                kh, g = divmod(hq, GQA_GROUP)
                k_h = band[:, kh * HEAD_DIM:(kh + 1) * HEAD_DIM]
                v_h = band[:, KV_DIM + kh * HEAD_DIM:KV_DIM + (kh + 1) * HEAD_DIM]
                s = _dot_nt(qb[:, hq * HEAD_DIM:(hq + 1) * HEAD_DIM], k_h) - _SLOPES[hq] * distf
                s = jnp.where(valid, s, NEG)
                sink = sink_ref[hq]
                m = jnp.maximum(jnp.max(s, axis=1, keepdims=True), sink)
                e = jnp.exp(s - m)
                esink = jnp.exp(sink - m)
                inv = 1.0 / (jnp.sum(e, axis=1, keepdims=True) + esink)
                pb = (e * inv).astype(BF16)
                p_ref[b, kh, g * BLOCK:(g + 1) * BLOCK, :] = pb
                outs.append(_dot(pb, v_h))
                psink_mat = jnp.where(lane == hq, esink * inv, psink_mat)
            ao_ref[pl.ds(r0, BLOCK), :] = jnp.concatenate(outs, axis=1).astype(BF16)
            psink_ref[pl.ds(r0, BLOCK), :] = psink_mat
            return carry

        lax.fori_loop(0, nb, blk_body, 0)
        out_ref[...] = xv + _dot(ao_ref[...], w_o_v[...])

    row = _row_spec(tm, D_MODEL)
    prev_spec = pl.BlockSpec((BLOCK, 2 * KV_DIM), lambda i: (jnp.maximum(i * nb - 1, 0), 0))
    return pl.pallas_call(
        body, name="attn_fwd", grid=(nt,),
        in_specs=[row, _const_spec((1, D_MODEL)), _row_spec(tm, 2 * KV_DIM), prev_spec, SMEM, ANY, ANY],
        out_specs=[row, row, row, pl.BlockSpec((nb, N_KV_HEADS, GROUP_ROWS, 2 * BLOCK), lambda i: (i, 0, 0, 0)),
                   _row_spec(tm, 128)],
        out_shape=[jax.ShapeDtypeStruct((T, D_MODEL), F32), jax.ShapeDtypeStruct((T, D_MODEL), BF16),
                   jax.ShapeDtypeStruct((T, D_MODEL), BF16),
                   jax.ShapeDtypeStruct((T // BLOCK, N_KV_HEADS, GROUP_ROWS, 2 * BLOCK), BF16),
                   jax.ShapeDtypeStruct((T, 128), F32)],
        scratch_shapes=[pltpu.VMEM((D_MODEL, D_MODEL), BF16), pltpu.VMEM((D_MODEL, D_MODEL), BF16),
                        pltpu.VMEM((tm + BLOCK, 2 * KV_DIM), BF16), pltpu.SemaphoreType.DMA((2,))],
        compiler_params=_params(),
    )(h, nmix, kv, kv, sinks, w_q, w_o)


def _attn_bwd(dh, h, q, kv, ao, p, psink, nmix, w_q, w_o):
    T = h.shape[0]
    tm = min(512, T)
    nt = T // tm
    nb = tm // BLOCK

    def body(dh_ref, h_ref, q_ref, kv_ref, kvp_ref, ao_ref, p_ref, psink_ref, n_ref, w_q_hbm, w_o_hbm,
             dhin_ref, dwq_ref, dwo_ref, dkv_ref, dsink_ref, dn_ref,
             w_q_v, w_o_v, kvs_v, dao_v, dq_v, dkv_v, carry_v, sem):
        _load_once([(w_q_hbm, w_q_v), (w_o_hbm, w_o_v)], sem)
        _zero_first([carry_v, dsink_ref, dn_ref, dwq_ref, dwo_ref])
        dout = dh_ref[...]
        doutb = dout.astype(BF16)
        dao_v[...] = _dot_nt(doutb, w_o_v[...])
        dwo_ref[...] += _dot_tn(ao_ref[...], doutb)
        kvs_v[0:BLOCK, :] = kvp_ref[...]
        kvs_v[BLOCK:, :] = kv_ref[...]
        dkv_v[0:tm, :] = jnp.zeros((tm, 2 * KV_DIM), F32)
        dkv_v[tm:, :] = carry_v[...]
        seg = (lax.broadcasted_iota(jnp.int32, (D_MODEL, 128), 0) // HEAD_DIM
               == lax.broadcasted_iota(jnp.int32, (D_MODEL, 128), 1)).astype(BF16)

        def blk_body(b, dsk):
            r0 = pl.multiple_of(b * BLOCK, BLOCK)
            qb = q_ref[pl.ds(r0, BLOCK), :]
            band = kvs_v[pl.ds(r0, 2 * BLOCK), :]
            aob = ao_ref[pl.ds(r0, BLOCK), :].astype(F32)
            daob = dao_v[pl.ds(r0, BLOCK), :]
            prod = daob * aob
            head = prod.astype(BF16)
            tail = (prod - head.astype(F32)).astype(BF16)
            dsk = dsk + psink_ref[pl.ds(r0, BLOCK), :] * (_dot(head, seg) + _dot(tail, seg))
            dqs = []
            dks = []
            dvs = []
            for kh in range(N_KV_HEADS):
                k_h = band[:, kh * HEAD_DIM:(kh + 1) * HEAD_DIM]
                v_h = band[:, KV_DIM + kh * HEAD_DIM:KV_DIM + (kh + 1) * HEAD_DIM]
                q_g = _stack_heads(qb, kh)
                dao_g = _stack_heads(daob, kh)
                prb = p_ref[b, kh]
                pr = prb.astype(F32)
                dd = jnp.sum(dao_g * _stack_heads(aob, kh), axis=1, keepdims=True)
                dao_gb = dao_g.astype(BF16)
                dp = _dot_nt(dao_gb, v_h)
                dsb = (pr * (dp - dd)).astype(BF16)
                dq_g = _dot(dsb, k_h) * (HEAD_DIM ** -0.5)
                dks.append(_dot_tn(dsb, q_g))
                dvs.append(_dot_tn(prb, dao_gb))
                for g in range(GQA_GROUP):
                    dqs.append(dq_g[g * BLOCK:(g + 1) * BLOCK])
            dq_v[pl.ds(r0, BLOCK), :] = jnp.concatenate(dqs, axis=1)
            dkv_v[pl.ds(r0, 2 * BLOCK), :] += jnp.concatenate(dks + dvs, axis=1)
            return dsk

        dsk = lax.fori_loop(0, nb, blk_body, jnp.zeros((BLOCK, 128), F32))
        dsink_ref[...] -= jnp.sum(dsk, axis=0, keepdims=True)
        dqb = dq_v[...].astype(BF16)
        dxn = _dot_nt(dqb, w_q_v[...])
        xn, xh, r = _rms(h_ref[...], n_ref[...])
        dwq_ref[...] += _dot_tn(xn.astype(BF16), dqb)
        dxx, dn = _rms_bwd(dxn, xh, r, n_ref[...])
        dn_ref[...] += dn
        dhin_ref[...] = dout + dxx
        dkv_ref[...] = dkv_v[BLOCK:, :]
        carry_v[...] = dkv_v[0:BLOCK, :]

    rev = functools.partial(_row_spec, rev_nt=nt)
    row = rev(tm, D_MODEL)
    prev_spec = pl.BlockSpec((BLOCK, 2 * KV_DIM), lambda i: (jnp.maximum((nt - 1 - i) * nb - 1, 0), 0))
    return pl.pallas_call(
        body, name="attn_bwd", grid=(nt,),
        in_specs=[row, row, row, rev(tm, 2 * KV_DIM), prev_spec, row,
                  pl.BlockSpec((nb, N_KV_HEADS, GROUP_ROWS, 2 * BLOCK), lambda i: (nt - 1 - i, 0, 0, 0)),
                  rev(tm, 128), _const_spec((1, D_MODEL)), ANY, ANY],
        out_specs=[row, _const_spec((D_MODEL, D_MODEL)), _const_spec((D_MODEL, D_MODEL)), rev(tm, 2 * KV_DIM),
                   _const_spec((8, 128)), _const_spec((1, D_MODEL))],
        out_shape=[jax.ShapeDtypeStruct((T, D_MODEL), F32), jax.ShapeDtypeStruct((D_MODEL, D_MODEL), F32),
                   jax.ShapeDtypeStruct((D_MODEL, D_MODEL), F32), jax.ShapeDtypeStruct((T, 2 * KV_DIM), F32),
                   jax.ShapeDtypeStruct((8, 128), F32), jax.ShapeDtypeStruct((1, D_MODEL), F32)],
        scratch_shapes=[pltpu.VMEM((D_MODEL, D_MODEL), BF16), pltpu.VMEM((D_MODEL, D_MODEL), BF16),
                        pltpu.VMEM((tm + BLOCK, 2 * KV_DIM), BF16), pltpu.VMEM((tm, D_MODEL), F32),
                        pltpu.VMEM((tm, D_MODEL), F32), pltpu.VMEM((tm + BLOCK, 2 * KV_DIM), F32),
                        pltpu.VMEM((BLOCK, 2 * KV_DIM), F32), pltpu.SemaphoreType.DMA((2,))],
        compiler_params=_params(),
    )(dh, h, q, kv, kv, ao, p, psink, nmix, w_q, w_o)


def _mesh_pos():
    return lax.axis_index("x"), lax.axis_index("y"), lax.axis_index("c")


def _other_chips(x, y):
    return [(1 - x, y), (x, 1 - y), (1 - x, 1 - y)]


HBM_SPEC = pl.BlockSpec(memory_space=pltpu.HBM)
SEM_SPEC = pl.BlockSpec(memory_space=pltpu.SEMAPHORE)


def _split_call(name, bufs, waits=(), starts=(), after=()):
    n, nw, ns, na = len(bufs), len(waits), len(starts), len(after)

    def body(*refs):
        brefs = refs[:n]
        wsems = [(refs[n + 2 * k], refs[n + 2 * k + 1]) for k in range(nw)]
        o = n + 2 * nw + na
        ssems = [(refs[o + 2 * k], refs[o + 2 * k + 1]) for k in range(ns)]
        for (ss, rs), (_, _, fn) in zip(wsems, waits):
            for sending, arriving in fn(brefs, ss, rs):
                sending.wait_send()
                arriving.wait_recv()
        for (ss, rs), (_, fn) in zip(ssems, starts):
            for sending, _ in fn(brefs, ss, rs):
                sending.start()
        if ns:
            token = refs[o + 2 * ns + n]
            token[...] = jnp.zeros(token.shape, token.dtype)

    out_shape, out_specs = [], []
    for cnt, _ in starts:
        out_shape += [pltpu.SemaphoreType.DMA((cnt,)), pltpu.SemaphoreType.DMA((cnt,))]
        out_specs += [SEM_SPEC, SEM_SPEC]
    out_shape += [pltpu.HBM(b.shape, b.dtype) for b in bufs]
    out_specs += [HBM_SPEC] * n
    if ns:
        out_shape.append(jax.ShapeDtypeStruct((8, 128), F32))
        out_specs.append(pl.BlockSpec(memory_space=pltpu.VMEM))
    args = [pltpu.with_memory_space_constraint(b, pltpu.HBM) for b in bufs]
    for ss, rs, _ in waits:
        args += [ss, rs]
    args += list(after)
    res = pl.pallas_call(
        body, name=name, out_shape=tuple(out_shape),
        in_specs=[HBM_SPEC] * n + [SEM_SPEC] * (2 * nw) + [ANY] * na, out_specs=tuple(out_specs),
        input_output_aliases={i: 2 * ns + i for i in range(n)},
        compiler_params=pltpu.CompilerParams(has_side_effects=pltpu.SideEffectType.DATAFLOW_SIDE_EFFECTING),
    )(*args)
    sems = [(res[2 * k], res[2 * k + 1]) for k in range(ns)]
    return list(res[2 * ns:2 * ns + n]), sems, (res[2 * ns + n] if ns else None)


def _cast_place(items, name, deps=()):
    n = len(items)
    mats = [a.shape[-2:] for a, _, _ in items]

    def body(*refs):
        ins, outs, scr, sem = refs[:n], refs[n:2 * n], refs[2 * n:3 * n], refs[3 * n]
        x, y, _ = _mesh_pos()
        cps = []
        for t in range(n):
            scr[t][...] = ins[t][...].astype(scr[t].dtype)
            cp = pltpu.make_async_copy(scr[t], outs[t].at[2 * x + y], sem.at[t])
            cp.start()
            cps.append(cp)
        for cp in cps:
            cp.wait()

    def spec(idx, shape):
        return pl.BlockSpec((None,) * len(idx) + tuple(shape), lambda i: tuple(idx) + (0, 0))

    body, in_specs, args = _add_deps(body, [spec(idx, mat) for (_, idx, _), mat in zip(items, mats)],
                                     [a for a, _, _ in items], deps)
    return pl.pallas_call(
        body, name=name, grid=(1,), in_specs=in_specs, out_specs=[ANY] * n,
        out_shape=[jax.ShapeDtypeStruct((N_SHARD,) + tuple(mat), dt) for (_, _, dt), mat in zip(items, mats)],
        scratch_shapes=[pltpu.VMEM(tuple(mat), dt) for (_, _, dt), mat in zip(items, mats)]
        + [pltpu.SemaphoreType.DMA((n,))],
        compiler_params=_params(),
    )(*args)


def _gather_ici(idx):
    def fn(bufs, ss, rs):
        x, y, c = _mesh_pos()
        pairs = []
        for k, t in enumerate(idx):
            half = bufs[t].shape[1] // 2
            mine = bufs[t].at[2 * x + y, pl.ds(c * half, half), :]
            for j, (cx, cy) in enumerate(_other_chips(x, y)):
                theirs = bufs[t].at[2 * cx + cy, pl.ds(c * half, half), :]
                sem = dict(send_sem=ss.at[3 * k + j], recv_sem=rs.at[3 * k + j],
                           device_id=(cx, cy, c), device_id_type=MESH)
                pairs.append((pltpu.make_async_remote_copy(src_ref=mine, dst_ref=mine, **sem),
                              pltpu.make_async_remote_copy(src_ref=mine, dst_ref=theirs, **sem)))
        return pairs
    return fn


def _gather_d2d(idx):
    def fn(bufs, ss, rs):
        x, y, c = _mesh_pos()
        pairs = []
        for k, t in enumerate(idx):
            half = bufs[t].shape[1] // 2
            for j, (cx, cy) in enumerate(_other_chips(x, y)):
                got = bufs[t].at[2 * cx + cy, pl.ds(c * half, half), :]
                theirs = bufs[t].at[2 * cx + cy, pl.ds((1 - c) * half, half), :]
                sem = dict(send_sem=ss.at[3 * k + j], recv_sem=rs.at[3 * k + j],
                           device_id=(x, y, 1 - c), device_id_type=MESH)
                pairs.append((pltpu.make_async_remote_copy(src_ref=got, dst_ref=got, **sem),
                              pltpu.make_async_remote_copy(src_ref=got, dst_ref=theirs, **sem)))
        return pairs
    return fn


def _alloc(shapes, name):
    def body(*refs):
        pass

    return pl.pallas_call(body, name=name, out_specs=[ANY] * len(shapes),
                          out_shape=[jax.ShapeDtypeStruct(s, d) for s, d in shapes])()


def _send_to_sibling(n):
    def fn(bufs, ss, rs):
        x, y, c = _mesh_pos()
        pairs = []
        for t in range(n):
            src = bufs[t]
            if len(src.shape) == 3:
                half = src.shape[1] // 2
                src = src.at[:, pl.ds((1 - c) * half, half), :]
            cp = pltpu.make_async_remote_copy(src_ref=src, dst_ref=bufs[n + t], send_sem=ss.at[t],
                                              recv_sem=rs.at[t], device_id=(x, y, 1 - c), device_id_type=MESH)
            pairs.append((cp, cp))
        return pairs
    return fn


def _send_to_chips(n):
    def fn(bufs, ss, rs):
        x, y, c = _mesh_pos()
        pairs = []
        for j, (cx, cy) in enumerate(_other_chips(x, y)):
            for t in range(n):
                src = bufs[t].at[j] if len(bufs[t].shape) == 3 else bufs[t]
                cp = pltpu.make_async_remote_copy(src_ref=src, dst_ref=bufs[n + t].at[j], send_sem=ss.at[3 * t + j],
                                                  recv_sem=rs.at[3 * t + j], device_id=(cx, cy, c),
                                                  device_id_type=MESH)
                pairs.append((cp, cp))
        return pairs
    return fn


class _Exchange:
    def __init__(self, name, srcs, land_shapes, fn, n_sems):
        self.name, self.fn = name, fn
        lands = _alloc(land_shapes, name + "_alloc")
        self.n = len(srcs)
        self.bufs, sems, self.token = _split_call(name + "_start", list(srcs) + list(lands),
                                                  starts=[(n_sems, fn)])
        self.sems = sems[0]

    def finish(self, after=()):
        bufs, _, _ = _split_call(self.name + "_wait", self.bufs, waits=[(*self.sems, self.fn)], after=after)
        return bufs[:self.n], bufs[self.n:]


def _row_block(rows, cols, mult=8, limit=3 * 512 * 1024, itemsize=4):
    best = None
    for br in range(mult, rows + 1, mult):
        if rows % br == 0 and br * cols * itemsize <= limit:
            best = br
    assert best is not None, (rows, cols)
    return best


_GROUP_BLOCK_BYTES = 1024 * 1024


def _group_plan(ss):
    plan = []
    for s in ss:
        half, cols = s.shape[-2:]
        br = _row_block(half, cols, mult=16, limit=_GROUP_BLOCK_BYTES)
        plan.append((br, half // br))
    return plan, max(nr for _, nr in plan)


def _chip_partial(gs, ss, ids, name):
    n = len(gs)
    plan, steps = _group_plan(ss)

    def body(ids_ref, *refs):
        for t in range(n):
            refs[2 * n + t][...] = (refs[t][...] + refs[n + t][...]).astype(BF16)

    g_specs, s_specs, o_specs = [], [], []
    for (br, nr), s in zip(plan, ss):
        blk = (None, br, s.shape[2])
        g_specs.append(pl.BlockSpec(
            blk, lambda j, r, ids_ref, nr=nr: (ids_ref[2 + j], ids_ref[0] * nr + jnp.minimum(r, nr - 1), 0)))
        s_specs.append(pl.BlockSpec(blk, lambda j, r, ids_ref, nr=nr: (ids_ref[2 + j], jnp.minimum(r, nr - 1), 0)))
        o_specs.append(pl.BlockSpec(blk, lambda j, r, ids_ref, nr=nr: (j, jnp.minimum(r, nr - 1), 0)))
    return pl.pallas_call(
        body, name=name,
        grid_spec=pltpu.PrefetchScalarGridSpec(num_scalar_prefetch=1, grid=(3, steps),
                                               in_specs=g_specs + s_specs, out_specs=o_specs),
        out_shape=[jax.ShapeDtypeStruct((3,) + s.shape[1:], BF16) for s in ss],
        compiler_params=pltpu.CompilerParams(dimension_semantics=("arbitrary", "arbitrary"),
                                             vmem_limit_bytes=VMEM_LIMIT),
    )(ids, *gs, *ss)


def _chip_sum(gs, ss, qs, ids, name):
    n = len(gs)
    plan, steps = _group_plan(ss)

    def body(ids_ref, *refs):
        for t in range(n):
            q_ref = refs[2 * n + t]
            own = refs[t][...] + refs[n + t][...]
            refs[3 * n + t][...] = (own + q_ref[2].astype(F32)) + (q_ref[0].astype(F32) + q_ref[1].astype(F32))

    g_specs, s_specs, q_specs, o_specs = [], [], [], []
    for (br, nr), s in zip(plan, ss):
        cols = s.shape[2]
        g_specs.append(pl.BlockSpec(
            (None, br, cols), lambda r, ids_ref, nr=nr: (ids_ref[1], ids_ref[0] * nr + jnp.minimum(r, nr - 1), 0)))
        s_specs.append(pl.BlockSpec((None, br, cols), lambda r, ids_ref, nr=nr: (ids_ref[1], jnp.minimum(r, nr - 1), 0)))
        q_specs.append(pl.BlockSpec((3, br, cols), lambda r, ids_ref, nr=nr: (0, jnp.minimum(r, nr - 1), 0)))
        o_specs.append(pl.BlockSpec((br, cols), lambda r, ids_ref, nr=nr: (jnp.minimum(r, nr - 1), 0)))
    return pl.pallas_call(
        body, name=name,
        grid_spec=pltpu.PrefetchScalarGridSpec(num_scalar_prefetch=1, grid=(steps,),
                                               in_specs=g_specs + s_specs + q_specs, out_specs=o_specs),
        out_shape=[jax.ShapeDtypeStruct(s.shape[1:], F32) for s in ss],
        compiler_params=pltpu.CompilerParams(dimension_semantics=("arbitrary",), vmem_limit_bytes=VMEM_LIMIT),
    )(ids, *gs, *ss, *qs)


def _adamw_math(w, g, m, v):
    mn = ADAM_B1 * m + (1.0 - ADAM_B1) * g
    vn = ADAM_B2 * v + (1.0 - ADAM_B2) * (g * g)
    m_hat = mn / (1.0 - ADAM_B1 ** ADAM_STEP)
    v_hat = vn / (1.0 - ADAM_B2 ** ADAM_STEP)
    return -ADAM_LR * (m_hat / (jnp.sqrt(v_hat) + ADAM_EPS) + ADAM_WD * w), mn, vn


def _adamw_halves(w, own, sib, m, v, ids, name, layer=0, n_layers=1, stacked=None):
    C = w.shape[1]
    R = w.shape[0] // n_layers
    half = R // 2
    br = _row_block(half, C)
    nh = half // br
    base = layer * 2 * nh

    def body(ids_ref, w_ref, own_ref, sib_ref, m_ref, v_ref, *rest):
        g_ref, d_ref, mo_ref, vo_ref = rest[-4:]
        is_own = (pl.program_id(0) // nh) == ids_ref[0]
        g = jnp.where(is_own, own_ref[...], sib_ref[...])
        g_ref[...] = g
        d_ref[...], mo_ref[...], vo_ref[...] = _adamw_math(w_ref[...], g, m_ref[...], v_ref[...])

    full = pl.BlockSpec((br, C), lambda r, ids_ref: (base + r, 0))
    own_spec = pl.BlockSpec((br, C), lambda r, ids_ref: (jnp.clip(r - ids_ref[0] * nh, 0, nh - 1), 0))
    sib_spec = pl.BlockSpec((br, C), lambda r, ids_ref: (jnp.clip(r - (1 - ids_ref[0]) * nh, 0, nh - 1), 0))
    in_specs = [full, own_spec, sib_spec, full, full]
    args = [ids, w, own, sib, m, v]
    aliases = {}
    if stacked is not None:
        in_specs += [ANY] * 4
        args += list(stacked)
        aliases = {6 + k: k for k in range(4)}
    return pl.pallas_call(
        body, name=name,
        grid_spec=pltpu.PrefetchScalarGridSpec(
            num_scalar_prefetch=1, grid=(2 * nh,), in_specs=in_specs, out_specs=[full] * 4),
        out_shape=[jax.ShapeDtypeStruct(w.shape, F32)] * 4, input_output_aliases=aliases,
        compiler_params=_params(),
    )(*args)


_PACK_UNIT = 1024


def _pack(arrs):
    flat = []
    for a in arrs:
        f = a.reshape(-1).astype(F32)
        pad = (-f.shape[0]) % _PACK_UNIT
        if pad:
            f = jnp.concatenate([f, jnp.zeros((pad,), F32)])
        flat.append(f)
    return jnp.concatenate(flat).reshape(-1, 128)


def kernel(x, p, norm_mix, norm_ffn, norm_ple, norm_kv, norm_final, a_w_in, a_norm_v, a_w_s, a_b_s, a_w_out, w_kv, b_w_q, b_sinks, b_w_o, f_w_up, f_conv_w, f_conv_b, f_w_down, ple_w_in, ple_w_gate, ple_b_gate, loss_target, m_norm_mix, m_norm_ffn, m_norm_ple, m_norm_kv, m_norm_final, m_a_w_in, m_a_norm_v, m_a_w_s, m_a_b_s, m_a_w_out, m_w_kv, m_b_w_q, m_b_sinks, m_b_w_o, m_f_w_up, m_f_conv_w, m_f_conv_b, m_f_w_down, m_ple_w_in, m_ple_w_gate, m_ple_b_gate, v_norm_mix, v_norm_ffn, v_norm_ple, v_norm_kv, v_norm_final, v_a_w_in, v_a_norm_v, v_a_w_s, v_a_b_s, v_a_w_out, v_w_kv, v_b_w_q, v_b_sinks, v_b_w_o, v_f_w_up, v_f_conv_w, v_f_conv_b, v_f_w_down, v_ple_w_in, v_ple_w_gate, v_ple_b_gate):
    given = dict(locals())

    small_shard = _pack([a_norm_v, f_conv_w])
    pad_rows = (-small_shard.shape[0]) % 16
    if pad_rows:
        small_shard = jnp.concatenate([small_shard, jnp.zeros((pad_rows, 128), F32)])
    groups = [
        [(a_w_in, (0,), BF16), (a_w_out, (0,), BF16), (small_shard, (), F32)],
        [(f_w_up, (0,), BF16), (f_w_down, (0,), BF16)],
        [(ple_w_in, (0,), BF16), (ple_w_gate, (0,), BF16), (w_kv, (), BF16), (b_w_q, (0,), BF16),
         (b_w_o, (0,), BF16), (f_w_up, (1,), BF16), (f_w_down, (1,), BF16), (ple_w_in, (1,), BF16),
         (ple_w_gate, (1,), BF16)],
    ]
    first = list(range(len(groups[0])))
    lands0, sems0, token0 = _split_call("gather_start_g0", _cast_place(groups[0], "cast_place_g0"),
                                        starts=[(3 * len(first), _gather_ici(first))])
    rest, spans, start = [], [], 0
    for gi, items in enumerate(groups[1:], 1):
        rest += _cast_place(items, f"cast_place_g{gi}", deps=(token0,))
        spans.append(list(range(start, start + len(items))))
        start += len(items)
    rest, rest_sems, rest_token = _split_call("gather_start", rest,
                                              starts=[(3 * len(sp), _gather_ici(sp)) for sp in spans])
    group_bufs = [lands0] + [[rest[t] for t in sp] for sp in spans]
    ici_sems = sems0 + rest_sems

    def finish_group(gi, after):
        bufs = group_bufs[gi]
        local = list(range(len(bufs)))
        bufs, d2d_sems, _ = _split_call(f"gather_pass_g{gi}", bufs, waits=[(*ici_sems[gi], _gather_ici(local))],
                                        starts=[(3 * len(local), _gather_d2d(local))], after=after)
        bufs, _, _ = _split_call(f"gather_done_g{gi}", bufs, waits=[(*d2d_sems[0], _gather_d2d(local))])
        return bufs

    def stage0():
        b_in, b_out, b_small = finish_group(0, (rest_token,))
        small_full = b_small.reshape(N_SHARD, -1)
        gv_full = small_full[:, :256].reshape(1, D_MODEL)
        cw_full = small_full[:, _PACK_UNIT:_PACK_UNIT + 2 * 3 * FF_BLK].reshape(N_SHARD, 2, 3, FF_BLK)
        cw_full = jnp.transpose(cw_full, (1, 2, 0, 3)).reshape(2, 3, N_FF)
        return gv_full, cw_full, b_in, b_out.reshape(D_MODEL, D_MODEL)

    def stage1(after):
        b_up, b_dn = finish_group(1, after)
        return b_up, b_dn.reshape(D_FF, D_MODEL)

    def stage2(after):
        pin0, gate0, kv_w, wq, wo, up1, dn1, pin1, gate1 = finish_group(2, after)
        sq = lambda a: a.reshape(D_MODEL, -1)
        return dict(w_pin=[pin0, pin1], w_gate=[sq(gate0), sq(gate1)], w_kv=sq(kv_w), w_q=sq(wq), w_o=sq(wo),
                    w_up1=up1, w_dn1=dn1.reshape(D_FF, D_MODEL))

    dx, (loss, (out_g, out_d, out_m, out_v)) = _local_step(
        x[0], p.reshape(2, -1, PLE_DIM), loss_target[0], norm_mix, norm_ffn, norm_ple, norm_kv, norm_final, a_w_s, a_b_s,
        b_sinks, f_conv_b, ple_b_gate, stage0, stage1, stage2, _Reducer(given))
    weight_names = ['norm_mix', 'norm_ffn', 'norm_ple', 'norm_kv', 'norm_final', 'a_w_in', 'a_norm_v', 'a_w_s',
                    'a_b_s', 'a_w_out', 'w_kv', 'b_w_q', 'b_sinks', 'b_w_o', 'f_w_up', 'f_conv_w', 'f_conv_b',
                    'f_w_down', 'ple_w_in', 'ple_w_gate', 'ple_b_gate']
    return (loss, dx.reshape(x.shape), *[out_g[k] for k in weight_names], *[out_d[k] for k in weight_names],
            *[out_m[k] for k in weight_names], *[out_v[k] for k in weight_names])


def _local_step(xs, p, tgt, norm_mix, norm_ffn, norm_ple, norm_kv, norm_final, a_w_s, a_b_s, b_sinks,
                f_conv_b, ple_b_gate, stage0, stage1, stage2, sched):
    tril = jnp.tril(jnp.ones((CHUNK, CHUNK), F32))
    wsm = (a_w_s[0] * tril[None]).astype(BF16)
    bsb = jnp.broadcast_to(a_b_s[0][:, :, None], (A_GROUPS, CHUNK, CHUNK))
    sinks = b_sinks[0]
    row = lambda a: a.reshape(1, -1)

    gv_full, cw_full, w_in, w_out = stage0()
    h1, zp = _mixer_a_fwd(xs, row(norm_mix[0]), gv_full, wsm, bsb, w_in, w_out)
    w_up0, w_dn0 = stage1((h1,))
    h2, hh0, c0 = _ffn_fwd(h1, row(norm_ffn[0]), cw_full[0], row(f_conv_b[0]), w_up0, w_dn0, 0)
    rest = stage2((h2,))
    w_pin, w_gate, w_kv_f, w_q, w_o = rest['w_pin'], rest['w_gate'], rest['w_kv'], rest['w_q'], rest['w_o']
    w_up = [w_up0, rest['w_up1']]
    w_dn = [w_dn0, rest['w_dn1']]
    h3, kv, a0 = _ple_fwd_kv(h2, p, row(norm_ple[0]), row(ple_b_gate[0]), row(norm_kv), w_pin[0], w_gate[0], w_kv_f)
    h4, q, ao, probs, psink = _attn_fwd(h3, row(norm_mix[1]), kv, sinks, w_q, w_o)
    h5, hh1, c1 = _ffn_fwd(h4, row(norm_ffn[1]), cw_full[1], row(f_conv_b[1]), w_up[1], w_dn[1], 1)
    dh6, loss_acc, dn_final, a1 = _ple_fwd_final(
        h5, p, tgt, row(norm_ple[1]), row(ple_b_gate[1]), row(norm_final), w_pin[1], w_gate[1])

    def pieces(g):
        return g.reshape(N_SHARD, -1, g.shape[-1])

    dh5, g_pin1, g_gate1, dbg1, dnple1 = _ple_bwd(dh6, h5, p, a1, row(norm_ple[1]), w_pin[1], w_gate[1], 1)
    early = {('ple_w_in', 1): g_pin1, ('ple_w_gate', 1): pieces(g_gate1)}
    dh4, g_up1, g_dn1, dcw1, dcb1, dnffn1 = _ffn_bwd(
        dh5, h4, hh1, c1, row(norm_ffn[1]), cw_full[1], w_up[1], w_dn[1], 1)
    early['f_w_down', 1] = pieces(g_dn1)
    early['f_w_up', 1] = g_up1
    dh3a, g_wq, g_wo, dkv, dsink, dnmix1 = _attn_bwd(dh4, h3, q, kv, ao, probs, psink, row(norm_mix[1]), w_q, w_o)
    early['b_w_o', 0] = pieces(g_wo)
    early['b_w_q', 0] = pieces(g_wq)
    dh2, g_pin0, g_gate0, dbg0, dnple0, g_wkv, dnkv = _ple_bwd(
        dh3a, h2, p, a0, row(norm_ple[0]), w_pin[0], w_gate[0], 0,
        kv_args=(h3, dkv, row(norm_kv), w_kv_f))
    early['w_kv', 0] = pieces(g_wkv)
    early['ple_w_in', 0] = g_pin0
    early['ple_w_gate', 0] = pieces(g_gate0)
    deps = sched.early_ready(early)
    dh1, g_up0, g_dn0, dcw0, dcb0, dnffn0 = _ffn_bwd(
        dh2, h1, hh0, c0, row(norm_ffn[0]), cw_full[0], w_up[0], w_dn[0], 0, deps=deps,
        between=lambda part: sched.after_ffn_half((part,)))
    deps = sched.ffn0_ready({('f_w_down', 0): pieces(g_dn0), ('f_w_up', 0): g_up0})
    dx, g_win, g_wout, dws, dbs, dgv, dnmix0 = _mixer_a_bwd(
        dh1, xs, zp, row(norm_mix[0]), gv_full, wsm, bsb, tril, w_in, w_out, deps=deps)
    g_wout = pieces(g_wout)

    small_grads = {
        'norm_mix': jnp.concatenate([dnmix0, dnmix1]), 'norm_ffn': jnp.concatenate([dnffn0, dnffn1]),
        'norm_ple': jnp.concatenate([dnple0, dnple1]), 'norm_kv': dnkv, 'norm_final': dn_final,
        'a_norm_v': dgv, 'a_w_s': dws.reshape(A_GROUPS * CHUNK, CHUNK), 'a_b_s': dbs[:, :, 0],
        'b_sinks': dsink[0:1, :], 'f_conv_w': jnp.concatenate([dcw0, dcw1]),
        'f_conv_b': jnp.concatenate([dcb0, dcb1]), 'ple_b_gate': jnp.concatenate([dbg0, dbg1]),
        'loss': loss_acc,
    }
    outs = sched.finish({('a_w_in', 0): g_win, ('a_w_out', 0): g_wout}, small_grads, (dx,))
    return dx, outs


class _Reducer:
    def __init__(self, given):
        self.given = given
        cx, cy, cc = _mesh_pos()
        self.shard = 2 * cx + cy
        s = self.shard
        self.ids = jnp.stack([cc, s, s ^ 2, s ^ 1, s ^ 3]).astype(jnp.int32)
        self.out = [{}, {}, {}, {}]
        self.stacked = {}

    def _send(self, tag, grads, small=()):
        keys = list(grads)
        srcs = [grads[k] for k in keys] + list(small)
        shapes = [((N_SHARD, g.shape[1] // 2, g.shape[2]), F32) for g in srcs[:len(keys)]]
        shapes += [(s.shape, F32) for s in small]
        return keys, _Exchange(f"send_{tag}", srcs, shapes, _send_to_sibling(len(srcs)), len(srcs))

    def _exchange(self, tag, keys, send, after):
        srcs, lands = send.finish(after)
        n = len(keys)
        parts = _chip_partial(srcs[:n], lands[:n], self.ids, f"chip_partial_{tag}")
        shapes = [(p.shape, BF16) for p in parts]
        if len(srcs) > n:
            small = _small_add(srcs[n:], lands[n:])
            parts += small
            shapes += [((3,) + s.shape, F32) for s in small]
        exch = _Exchange(f"exch_{tag}", parts, shapes, _send_to_chips(len(parts)), 3 * len(parts))
        return (keys, srcs[:n], lands[:n], exch)

    def _swap(self, tag, state, after):
        keys, grads, sib, exch = state
        parts, recv = exch.finish(after)
        n = len(keys)
        own = _chip_sum(grads, sib, recv[:n], self.ids, f"chip_sum_{tag}")
        small_red = _small_sum(parts[n:], recv[n:]) if len(parts) > n else None
        return keys, _Exchange(f"swap_{tag}", own, [(o.shape, F32) for o in own], _send_to_sibling(n), n), small_red

    def _adamw(self, keys, swap, after):
        own, sib = swap.finish(after)
        last = None
        for (name, layer), o, s in zip(keys, own, sib):
            w = self.given[name]
            n_layers = w.shape[0] if w.ndim == 3 else 1
            c2 = w.shape[-1]
            res = _adamw_halves(w.reshape(-1, c2), o, s, self.given['m_' + name].reshape(-1, c2),
                                self.given['v_' + name].reshape(-1, c2), self.ids, f"adamw_{name}{layer}",
                                layer, n_layers, self.stacked.get(name))
            self.stacked[name] = res
            if layer == 0:
                for dst, r in zip(self.out, res):
                    dst[name] = r.reshape(w.shape)
            last = res[0]
        return last

    def early_ready(self, grads):
        self.e_keys, self.e_send = self._send("e", grads)
        return (self.e_send.token,)

    def after_ffn_half(self, after):
        self.e_state = self._exchange("e", self.e_keys, self.e_send, after)
        return (self.e_state[3].token,)

    def ffn0_ready(self, grads):
        _, self.e_swap, _ = self._swap("e", self.e_state, tuple(grads.values()))
        self.f_keys, self.f_send = self._send("f", grads)
        return (self.f_send.token, self.e_swap.token)

    def finish(self, grads, small_grads, after):
        small_names = list(small_grads)
        f_state = self._exchange("f", self.f_keys, self.f_send, after)
        a_keys, a_send = self._send("a", grads, [small_grads[k] for k in small_names])
        a_state = self._exchange("a", a_keys, a_send, (f_state[3].token,))
        e_done = self._adamw(self.e_keys, self.e_swap, (a_state[3].token,))
        f_keys, f_swap, _ = self._swap("f", f_state, (e_done,))
        f_done = self._adamw(f_keys, f_swap, ())
        _, a_swap, small_red = self._swap("a", a_state, (f_done,))
        self._adamw(a_keys, a_swap, ())

        given = self.given
        reduced = dict(zip(small_names, small_red))
        loss = reduced.pop('loss')[0, 0]
        names = list(reduced)
        items = []
        for k in names:
            g = reduced[k]
            cols = g.shape[1] // N_SHARD if k in ('a_norm_v', 'f_conv_w') else g.shape[1]
            view = lambda a: _lane_pad(a.reshape(g.shape[0], -1), cols)
            items.append((view(given[k]), g, view(given['m_' + k]), view(given['v_' + k])))
        res = _adamw_small(items, self.ids)
        for k, four in zip(names, res):
            width = given[k].size // four[0].shape[0]
            for dst, r in zip(self.out, four):
                dst[k] = r[:, :width].reshape(given[k].shape)
        return loss, self.out


def _lane_pad(a, cols):
    return a if a.shape[1] == cols else jnp.pad(a, ((0, 0), (0, cols - a.shape[1])))


def _small_add(a_list, b_list):
    n = len(a_list)

    def body(*refs):
        for t in range(n):
            refs[2 * n + t][...] = refs[t][...] + refs[n + t][...]

    return pl.pallas_call(body, name="chip_partial_small",
                          out_shape=[jax.ShapeDtypeStruct(a.shape, F32) for a in a_list])(*a_list, *b_list)


def _small_sum(parts, recvs):
    n = len(parts)

    def body(*refs):
        for t in range(n):
            q = refs[n + t]
            refs[2 * n + t][...] = (refs[t][...] + q[2]) + (q[0] + q[1])

    return pl.pallas_call(body, name="chip_sum_small",
                          out_shape=[jax.ShapeDtypeStruct(p.shape, F32) for p in parts])(*parts, *recvs)


def _adamw_small(items, ids):
    n = len(items)

    def body(ids_ref, *refs):
        for t in range(n):
            w_ref, g_ref, m_ref, v_ref = refs[4 * t:4 * t + 4]
            g_out, d_ref, mo_ref, vo_ref = refs[4 * n + 4 * t:4 * n + 4 * t + 4]
            g = g_ref[...]
            g_out[...] = g
            d_ref[...], mo_ref[...], vo_ref[...] = _adamw_math(w_ref[...], g, m_ref[...], v_ref[...])

    in_specs, out_specs, out_shape, args = [], [], [], []
    for w, g, m, v in items:
        full = pl.BlockSpec(w.shape, lambda i, ids_ref: (0, 0))
        g_spec = full if g.shape == w.shape else pl.BlockSpec(w.shape, lambda i, ids_ref: (0, ids_ref[1]))
        in_specs += [full, g_spec, full, full]
        out_specs += [full] * 4
        out_shape += [jax.ShapeDtypeStruct(w.shape, F32)] * 4
        args += [w, g, m, v]
    res = pl.pallas_call(
        body, name="adamw_small",
        grid_spec=pltpu.PrefetchScalarGridSpec(num_scalar_prefetch=1, grid=(1,), in_specs=in_specs,
                                               out_specs=out_specs),
        out_shape=out_shape, compiler_params=_params(),
    )(ids, *args)
    return [res[4 * t:4 * t + 4] for t in range(n)]
```

```python
import functools
import math

import numpy as np
import jax
import jax.numpy as jnp
from jax import lax
from jax.experimental import pallas as pl
from jax.experimental.pallas import tpu as pltpu

F32 = jnp.float32
BF16 = jnp.bfloat16

D_MODEL = 1024
CHUNK = 128
A_GROUPS = 8
HEAD_DIM = 64
N_Q_HEADS = 16
N_KV_HEADS = 4
GQA_GROUP = N_Q_HEADS // N_KV_HEADS
KV_DIM = N_KV_HEADS * HEAD_DIM
BLOCK = 128
D_FF = 2816
N_FF = 2 * D_FF
FF_BLK = N_FF // 4
PLE_DIM = 256
EPS = 1e-6
NEG = -1e30
N_SHARD = 4

ADAM_LR = 0.001
ADAM_B1 = 0.9
ADAM_B2 = 0.999
ADAM_EPS = 1e-08
ADAM_WD = 0.01
ADAM_STEP = 10

VMEM_LIMIT = 60 * 1024 * 1024
MESH = pl.DeviceIdType.MESH
ANY = pl.BlockSpec(memory_space=pl.ANY)
SMEM = pl.BlockSpec(memory_space=pltpu.SMEM)

_SLOPES = [float(np.float32(2.0 ** (-8.0 * (h + 1) / N_Q_HEADS))) for h in range(N_Q_HEADS)]


def _dot(a, b):
    return jnp.dot(a, b, preferred_element_type=F32)


def _dot_nt(a, b):
    return lax.dot_general(a, b, (((1,), (1,)), ((), ())), preferred_element_type=F32)


def _dot_tn(a, b):
    return lax.dot_general(a, b, (((0,), (0,)), ((), ())), preferred_element_type=F32)


def _rms(x, g):
    r = lax.rsqrt(jnp.mean(x * x, axis=-1, keepdims=True) + EPS)
    xh = x * r
    return xh * g, xh, r


def _rms_bwd(dy, xh, r, g):
    dxh = dy * g
    dg = jnp.sum(dy * xh, axis=0, keepdims=True)
    dx = r * (dxh - xh * jnp.mean(dxh * xh, axis=-1, keepdims=True))
    return dx, dg


_GELU_C = math.sqrt(2.0 / math.pi)


def _gelu(x):
    t = jnp.tanh(_GELU_C * (x + 0.044715 * (x * x * x)))
    return 0.5 * x * (1.0 + t)


def _gelu_grad(x):
    x2 = x * x
    t = jnp.tanh(_GELU_C * (x + 0.044715 * (x2 * x)))
    return 0.5 * (1.0 + t) + 0.5 * x * (1.0 - t * t) * (_GELU_C * (1.0 + 3.0 * 0.044715 * x2))


def _sigmoid(x):
    return 0.5 * jnp.tanh(0.5 * x) + 0.5


def _load_once(pairs, sem):
    @pl.when(pl.program_id(0) == 0)
    def _():
        cps = [pltpu.make_async_copy(s, d, sem.at[i]) for i, (s, d) in enumerate(pairs)]
        for cp in cps:
            cp.start()
        for cp in cps:
            cp.wait()


def _params(n_axes=1, vmem=VMEM_LIMIT):
    return pltpu.CompilerParams(dimension_semantics=("arbitrary",) * n_axes, vmem_limit_bytes=vmem)


def _row_spec(tm, n, rev_nt=None):
    if rev_nt is None:
        return pl.BlockSpec((tm, n), lambda i: (i, 0))
    return pl.BlockSpec((tm, n), lambda i: (rev_nt - 1 - i, 0))


def _const_spec(shape):
    nd = len(shape)
    return pl.BlockSpec(shape, lambda i: (0,) * nd)


def _add_deps(body, in_specs, args, deps):
    nd = len(deps)
    if nd == 0:
        return body, list(in_specs), list(args)

    def wrapped(*refs):
        return body(*refs[nd:])

    return wrapped, [ANY] * nd + list(in_specs), list(deps) + list(args)


def _zero_first(refs):
    @pl.when(pl.program_id(0) == 0)
    def _():
        for r in refs:
            r[...] = jnp.zeros(r.shape, r.dtype)


def _mixer_a_fwd(x, nmix, gv, wsm, bsb, w_in, w_out):
    T = x.shape[0]
    tm = min(512, T)
    nt = T // tm
    nw = 2 * D_MODEL // N_SHARD

    def body(x_ref, nmix_ref, gv_ref, ws_ref, bsb_ref, w_in_hbm, w_out_hbm,
             h1_ref, zp_ref, w_in_v, w_out_v, gated_v, sem):
        _load_once([(w_in_hbm, w_in_v), (w_out_hbm, w_out_v)], sem)
        xv = x_ref[...]
        xn = _rms(xv, nmix_ref[...])[0].astype(BF16)
        for j in range(N_SHARD):
            zp_ref[:, j * nw:(j + 1) * nw] = _dot(xn, w_in_v[j])
        z = _gelu(zp_ref[...])
        u = z[:, :D_MODEL]
        vn = _rms(z[:, D_MODEL:], gv_ref[...])[0].astype(BF16)
        for c in range(tm // CHUNK):
            rows = slice(c * CHUNK, (c + 1) * CHUNK)
            for h in range(A_GROUPS):
                cols = slice(h * CHUNK, (h + 1) * CHUNK)
                s = _dot(ws_ref[h], vn[rows, cols]) + bsb_ref[h]
                gated_v[rows, cols] = (u[rows, cols] * s).astype(BF16)
        h1_ref[...] = xv + _dot(gated_v[...], w_out_v[...])

    return pl.pallas_call(
        body, name="mixer_a_fwd", grid=(nt,),
        in_specs=[_row_spec(tm, D_MODEL), _const_spec((1, D_MODEL)), _const_spec((1, D_MODEL)),
                  _const_spec((A_GROUPS, CHUNK, CHUNK)), _const_spec((A_GROUPS, CHUNK, CHUNK)), ANY, ANY],
        out_specs=[_row_spec(tm, D_MODEL), _row_spec(tm, 2 * D_MODEL)],
        out_shape=[jax.ShapeDtypeStruct((T, D_MODEL), F32), jax.ShapeDtypeStruct((T, 2 * D_MODEL), F32)],
        scratch_shapes=[pltpu.VMEM((N_SHARD, D_MODEL, nw), BF16), pltpu.VMEM((D_MODEL, D_MODEL), BF16),
                        pltpu.VMEM((tm, D_MODEL), BF16), pltpu.SemaphoreType.DMA((2,))],
        compiler_params=_params(),
    )(x, nmix, gv, wsm, bsb, w_in, w_out)


def _mixer_a_bwd(dh, x, zp, nmix, gv, wsm, bsb, tril, w_in, w_out, deps=()):
    T = x.shape[0]
    tm = min(256, T)
    nt = T // tm
    nw = 2 * D_MODEL // N_SHARD

    def body(dh_ref, x_ref, zp_ref, nmix_ref, gv_ref, ws_ref, bsb_ref, tril_ref, w_in_hbm, w_out_hbm,
             dx_ref, dwin_ref, dwout_ref, dws_ref, dbs_ref, dgv_ref, dnmix_ref,
             w_in_v, w_out_v, du_v, dvn_v, dbs_v, gated_ref, sem):
        _load_once([(w_in_hbm, w_in_v), (w_out_hbm, w_out_v)], sem)
        _zero_first([dws_ref, dbs_v, dgv_ref, dnmix_ref, dwin_ref, dwout_ref])
        i = pl.program_id(0)
        dhv = dh_ref[...]
        dhb = dhv.astype(BF16)
        xv = x_ref[...]
        xn, xh, r = _rms(xv, nmix_ref[...])
        xnb = xn.astype(BF16)
        zpv = zp_ref[...]
        z = _gelu(zpv)
        u = z[:, :D_MODEL]
        vn_f, vh, rv = _rms(z[:, D_MODEL:], gv_ref[...])
        vn = vn_f.astype(BF16)
        dgated = _dot_nt(dhb, w_out_v[...])
        for c in range(tm // CHUNK):
            rows = slice(c * CHUNK, (c + 1) * CHUNK)
            for h in range(A_GROUPS):
                cols = slice(h * CHUNK, (h + 1) * CHUNK)
                vn_h = vn[rows, cols]
                s = _dot(ws_ref[h], vn_h) + bsb_ref[h]
                dgt = dgated[rows, cols]
                u_h = u[rows, cols]
                gated_ref[rows, cols] = (u_h * s).astype(BF16)
                du_v[rows, cols] = dgt * s
                ds = dgt * u_h
                dsb = ds.astype(BF16)
                dws_ref[h] += _dot_nt(dsb, vn_h)
                dbs_v[h] += ds
                dvn_v[rows, cols] = _dot_tn(ws_ref[h], dsb)
        dwout_ref[...] += _dot_tn(gated_ref[...], dhb)
        dv, dgv = _rms_bwd(dvn_v[...], vh, rv, gv_ref[...])
        dgv_ref[...] += dgv
        dzu = (du_v[...] * _gelu_grad(zpv[:, :D_MODEL])).astype(BF16)
        dzv = (dv * _gelu_grad(zpv[:, D_MODEL:])).astype(BF16)
        dzs = (dzu[:, :nw], dzu[:, nw:], dzv[:, :nw], dzv[:, nw:])
        dxn = jnp.zeros((tm, D_MODEL), F32)
        for j in range(N_SHARD):
            dxn = dxn + _dot_nt(dzs[j], w_in_v[j])
            dwin_ref[j] += _dot_tn(xnb, dzs[j])
        dxx, dn = _rms_bwd(dxn, xh, r, nmix_ref[...])
        dnmix_ref[...] += dn
        dx_ref[...] = dhv + dxx

        @pl.when(i == nt - 1)
        def _():
            for h in range(A_GROUPS):
                dws_ref[h] = dws_ref[h] * tril_ref[...]
                dbs_ref[h] = jnp.broadcast_to(jnp.sum(dbs_v[h], axis=1, keepdims=True), (CHUNK, CHUNK))

    grp = (A_GROUPS, CHUNK, CHUNK)
    body, in_specs, args = _add_deps(
        body, [_row_spec(tm, D_MODEL), _row_spec(tm, D_MODEL), _row_spec(tm, 2 * D_MODEL),
               _const_spec((1, D_MODEL)), _const_spec((1, D_MODEL)), _const_spec(grp), _const_spec(grp),
               _const_spec((CHUNK, CHUNK)), ANY, ANY],
        [dh, x, zp, nmix, gv, wsm, bsb, tril, w_in, w_out], deps)
    return pl.pallas_call(
        body, name="mixer_a_bwd", grid=(nt,), in_specs=in_specs,
        out_specs=[_row_spec(tm, D_MODEL), _const_spec((N_SHARD, D_MODEL, nw)), _const_spec((D_MODEL, D_MODEL)),
                   _const_spec(grp), _const_spec(grp), _const_spec((1, D_MODEL)), _const_spec((1, D_MODEL))],
        out_shape=[jax.ShapeDtypeStruct((T, D_MODEL), F32), jax.ShapeDtypeStruct((N_SHARD, D_MODEL, nw), F32),
                   jax.ShapeDtypeStruct((D_MODEL, D_MODEL), F32),
                   jax.ShapeDtypeStruct(grp, F32), jax.ShapeDtypeStruct(grp, F32),
                   jax.ShapeDtypeStruct((1, D_MODEL), F32), jax.ShapeDtypeStruct((1, D_MODEL), F32)],
        scratch_shapes=[pltpu.VMEM((N_SHARD, D_MODEL, nw), BF16), pltpu.VMEM((D_MODEL, D_MODEL), BF16),
                        pltpu.VMEM((tm, D_MODEL), F32), pltpu.VMEM((tm, D_MODEL), F32),
                        pltpu.VMEM(grp, F32), pltpu.VMEM((tm, D_MODEL), BF16), pltpu.SemaphoreType.DMA((2,))],
        compiler_params=_params(),
    )(*args)


def _load_ffn_weights(w_up_hbm, w_dn_hbm, layer, w_up_v, w_dn_v, sem):
    _load_once([(w_up_hbm, w_up_v), (w_dn_hbm, w_dn_v)], sem)


def _ffn_fwd(h, nffn, cw, cb, w_up, w_dn, layer):
    T = h.shape[0]
    tm = min(256, T)
    nt = T // tm

    def body(h_ref, n_ref, cw_ref, cb_ref, w_up_hbm, w_dn_hbm, out_ref, hh_ref, gate_ref,
             w_up_v, w_dn_v, carry_v, sem):
        _load_ffn_weights(w_up_hbm, w_dn_hbm, layer, w_up_v, w_dn_v, sem)
        _zero_first([carry_v])
        xv = h_ref[...]
        xf = _rms(xv, n_ref[...])[0].astype(BF16)
        acc = xv
        for j in range(2):
            cs = []
            for blk in (j, j + 2):
                cols = slice(blk * FF_BLK, (blk + 1) * FF_BLK)
                hh = _dot(xf, w_up_v[blk])
                hh_ref[:, cols] = hh.astype(BF16)
                ext = jnp.concatenate([carry_v[blk], hh], axis=0)
                carry_v[blk] = hh[tm - 8:, :]
                s1 = pltpu.roll(ext, 1, 0)[8:]
                s2 = pltpu.roll(ext, 2, 0)[8:]
                cs.append(cb_ref[:, cols] + cw_ref[0:1, cols] * s2 + cw_ref[1:2, cols] * s1
                          + cw_ref[2:3, cols] * hh)
            cg, cu = cs
            sg = _sigmoid(cg)
            sil = cg * sg
            act = (sil * cu).astype(BF16)
            for kind, val in enumerate((cu * (sg * (1.0 + cg * (1.0 - sg))), sil)):
                gate_ref[:, kind * D_FF + j * FF_BLK:kind * D_FF + (j + 1) * FF_BLK] = val.astype(BF16)
            gate_ref[:, 2 * D_FF + j * FF_BLK:2 * D_FF + (j + 1) * FF_BLK] = act
            acc = acc + _dot(act, w_dn_v[j * FF_BLK:(j + 1) * FF_BLK, :])
        out_ref[...] = acc

    return pl.pallas_call(
        body, name=f"ffn_fwd{layer}", grid=(nt,),
        in_specs=[_row_spec(tm, D_MODEL), _const_spec((1, D_MODEL)), _const_spec((3, N_FF)),
                  _const_spec((1, N_FF)), ANY, ANY],
        out_specs=[_row_spec(tm, D_MODEL), _row_spec(tm, N_FF), _row_spec(tm, 3 * D_FF)],
        out_shape=[jax.ShapeDtypeStruct((T, D_MODEL), F32), jax.ShapeDtypeStruct((T, N_FF), BF16),
                   jax.ShapeDtypeStruct((T, 3 * D_FF), BF16)],
        scratch_shapes=[pltpu.VMEM((N_SHARD, D_MODEL, FF_BLK), BF16), pltpu.VMEM((D_FF, D_MODEL), BF16),
                        pltpu.VMEM((N_SHARD, 8, FF_BLK), F32), pltpu.SemaphoreType.DMA((2 * N_SHARD,))],
        compiler_params=_params(),
    )(h, nffn, cw, cb, w_up, w_dn)


def _wgrad(a, b, bn, col_sharded, name, deps=(), a_cols=None):
    T = a.shape[0]
    K, a_blk = (a.shape[1], 0) if a_cols is None else a_cols
    N = b.shape[1]
    tt = min(2048, T)
    nn, ntt = N // bn, T // tt
    kr = K // N_SHARD

    def body(a_ref, b_ref, o_ref):
        @pl.when(pl.program_id(1) == 0)
        def _():
            o_ref[...] = jnp.zeros(o_ref.shape, F32)
        d = _dot_tn(a_ref[...].astype(BF16), b_ref[...].astype(BF16))
        if col_sharded:
            o_ref[...] += d
        else:
            for j in range(N_SHARD):
                o_ref[j] += d[j * kr:(j + 1) * kr]

    if col_sharded:
        assert nn == N_SHARD
        out_spec = pl.BlockSpec((None, K, bn), lambda n, t: (n, 0, 0))
        out_shape = jax.ShapeDtypeStruct((N_SHARD, K, bn), F32)
    else:
        out_spec = pl.BlockSpec((N_SHARD, kr, bn), lambda n, t: (0, 0, n))
        out_shape = jax.ShapeDtypeStruct((N_SHARD, kr, N), F32)
    body, in_specs, args = _add_deps(
        body, [pl.BlockSpec((tt, K), lambda n, t: (t, a_blk)), pl.BlockSpec((tt, bn), lambda n, t: (t, n))],
        [a, b], deps)
    return pl.pallas_call(
        body, name=name, grid=(nn, ntt), in_specs=in_specs, out_specs=out_spec, out_shape=out_shape,
        compiler_params=pltpu.CompilerParams(dimension_semantics=("arbitrary",) * 2, vmem_limit_bytes=VMEM_LIMIT),
    )(*args)


def _ffn_bwd(dh, h, hh, gate, nffn, cw, w_up, w_dn, layer, deps=(), between=None):
    T = h.shape[0]
    tm = min(256, T)
    nt = T // tm

    def body(dh_ref, h_ref, hh_ref, gd_ref, sil_ref, n_ref, cw_ref, w_up_hbm, w_dn_hbm,
             dhin_ref, dhh_ref, xf_ref, dcw_ref, dcb_ref, dn_ref,
             w_up_v, w_dn_v, carry_v, sem):
        _load_ffn_weights(w_up_hbm, w_dn_hbm, layer, w_up_v, w_dn_v, sem)
        _zero_first([carry_v, dcw_ref, dcb_ref, dn_ref])
        dout = dh_ref[...]
        doutb = dout.astype(BF16)
        xf_f, xh, r = _rms(h_ref[...], n_ref[...])
        xf_ref[...] = xf_f.astype(BF16)
        dxf = jnp.zeros((tm, D_MODEL), F32)
        for j in range(2):
            blks = (j, j + 2)
            pair = slice(j * FF_BLK, (j + 1) * FF_BLK)
            dact = _dot_nt(doutb, w_dn_v[pair, :])
            dcs = (dact * gd_ref[:, pair].astype(F32), dact * sil_ref[:, pair].astype(F32))
            for blk, dc in zip(blks, dcs):
                cols = slice(blk * FF_BLK, (blk + 1) * FF_BLK)
                hhv = hh_ref[:, cols].astype(F32)
                ext = jnp.concatenate([dc, carry_v[blk]], axis=0)
                carry_v[blk] = dc[:8, :]
                n = tm + 8
                a1 = pltpu.roll(ext, n - 1, 0)[:tm]
                a2 = pltpu.roll(ext, n - 2, 0)[:tm]
                dcb_ref[:, cols] += jnp.sum(dc, axis=0, keepdims=True)
                dcw_ref[0:1, cols] += jnp.sum(a2 * hhv, axis=0, keepdims=True)
                dcw_ref[1:2, cols] += jnp.sum(a1 * hhv, axis=0, keepdims=True)
                dcw_ref[2:3, cols] += jnp.sum(dc * hhv, axis=0, keepdims=True)
                dhh = (cw_ref[2:3, cols] * dc + cw_ref[1:2, cols] * a1 + cw_ref[0:1, cols] * a2).astype(BF16)
                dhh_ref[:, cols] = dhh
                dxf = dxf + _dot_nt(dhh, w_up_v[blk])
        dxx, dn = _rms_bwd(dxf, xh, r, n_ref[...])
        dn_ref[...] += dn
        dhin_ref[...] = dout + dxx

    rev = functools.partial(_row_spec, rev_nt=nt)

    def kind(k):
        return pl.BlockSpec((tm, D_FF), lambda i: (nt - 1 - i, k))

    body, in_specs, args = _add_deps(
        body, [rev(tm, D_MODEL), rev(tm, D_MODEL), rev(tm, N_FF), kind(0), kind(1),
               _const_spec((1, D_MODEL)), _const_spec((3, N_FF)), ANY, ANY],
        [dh, h, hh, gate, gate, nffn, cw, w_up, w_dn], deps)
    dhin, dhh, xf, dcw, dcb, dn = pl.pallas_call(
        body, name=f"ffn_bwd{layer}", grid=(nt,), in_specs=in_specs,
        out_specs=[rev(tm, D_MODEL), rev(tm, N_FF), rev(tm, D_MODEL),
                   _const_spec((3, N_FF)), _const_spec((1, N_FF)), _const_spec((1, D_MODEL))],
        out_shape=[jax.ShapeDtypeStruct((T, D_MODEL), F32),
                   jax.ShapeDtypeStruct((T, N_FF), BF16), jax.ShapeDtypeStruct((T, D_MODEL), BF16),
                   jax.ShapeDtypeStruct((3, N_FF), F32), jax.ShapeDtypeStruct((1, N_FF), F32),
                   jax.ShapeDtypeStruct((1, D_MODEL), F32)],
        scratch_shapes=[pltpu.VMEM((N_SHARD, D_MODEL, FF_BLK), BF16), pltpu.VMEM((D_FF, D_MODEL), BF16),
                        pltpu.VMEM((N_SHARD, 8, FF_BLK), F32), pltpu.SemaphoreType.DMA((2 * N_SHARD,))],
        compiler_params=_params(),
    )(*args)
    deps2 = between(dhin) if between is not None else ()
    dwdn = _wgrad(gate, dh, D_MODEL // 2, False, f"wgrad_ffn_down{layer}", deps=deps2, a_cols=(D_FF, 2))
    dwup = _wgrad(xf, dhh, FF_BLK, True, f"wgrad_ffn_up{layer}", deps=deps2)
    return dhin, dwup, dwdn, dcw, dcb, dn


def _load_ple_weights(w_pin_hbm, w_gate_hbm, layer, w_pin_v, w_gate_v, sem, extra=()):
    _load_once([(w_pin_hbm, w_pin_v), (w_gate_hbm, w_gate_v)] + list(extra), sem)


def _p_spec(tm, layer):
    return pl.BlockSpec((None, tm, PLE_DIM), lambda i: (layer, i, 0))


def _ple_terms(xv, p_ref, n_ref, bg_ref, w_pin_v, w_gate_v, pe_v, a_ref, saved):
    pw = D_MODEL // N_SHARD
    xg, xh, r = _rms(xv, n_ref[...])
    xgb = xg.astype(BF16)
    if saved:
        gate = _sigmoid(a_ref[...].astype(F32))
    else:
        a = _dot(xgb, w_gate_v[...]) + bg_ref[...]
        a_ref[...] = a.astype(BF16)
        gate = _sigmoid(a)
    pb = p_ref[...].astype(BF16)
    for j in range(N_SHARD):
        pe_v[:, j * pw:(j + 1) * pw] = _dot(pb, w_pin_v[j])
    pe = pe_v[...]
    return pe * gate, pe, gate, xgb, xh, r


def _ple_fwd_kv(h, p, nple, bg, nkv, w_pin, w_gate, w_kv):
    T = h.shape[0]
    tm = min(512, T)
    nt = T // tm
    pw = D_MODEL // N_SHARD

    def body(h_ref, p_ref, n_ref, bg_ref, nkv_ref, w_pin_hbm, w_gate_hbm, w_kv_hbm,
             out_ref, kv_ref, a_ref, w_pin_v, w_gate_v, w_kv_v, pe_v, sem):
        _load_ple_weights(w_pin_hbm, w_gate_hbm, 0, w_pin_v, w_gate_v, sem, [(w_kv_hbm, w_kv_v)])
        xv = h_ref[...]
        hn = xv + _ple_terms(xv, p_ref, n_ref, bg_ref, w_pin_v, w_gate_v, pe_v, a_ref, False)[0]
        out_ref[...] = hn
        kvn = _rms(hn, nkv_ref[...])[0].astype(BF16)
        kv_ref[...] = _dot(kvn, w_kv_v[...]).astype(BF16)

    vec = _const_spec((1, D_MODEL))
    return pl.pallas_call(
        body, name="ple_fwd0", grid=(nt,),
        in_specs=[_row_spec(tm, D_MODEL), _p_spec(tm, 0), vec, vec, vec, ANY, ANY, ANY],
        out_specs=[_row_spec(tm, D_MODEL), _row_spec(tm, 2 * KV_DIM), _row_spec(tm, D_MODEL)],
        out_shape=[jax.ShapeDtypeStruct((T, D_MODEL), F32), jax.ShapeDtypeStruct((T, 2 * KV_DIM), BF16),
                   jax.ShapeDtypeStruct((T, D_MODEL), BF16)],
        scratch_shapes=[pltpu.VMEM((N_SHARD, PLE_DIM, pw), BF16), pltpu.VMEM((D_MODEL, D_MODEL), BF16),
                        pltpu.VMEM((D_MODEL, 2 * KV_DIM), BF16), pltpu.VMEM((tm, D_MODEL), F32),
                        pltpu.SemaphoreType.DMA((2 * N_SHARD + 1,))],
        compiler_params=_params(),
    )(h, p, nple, bg, nkv, w_pin, w_gate, w_kv)


def _ple_fwd_final(h, p, tgt, nple, bg, nfin, w_pin, w_gate):
    T = h.shape[0]
    tm = min(512, T)
    nt = T // tm
    pw = D_MODEL // N_SHARD

    def body(h_ref, p_ref, t_ref, n_ref, bg_ref, nf_ref, w_pin_hbm, w_gate_hbm,
             dh_ref, loss_ref, dnf_ref, a_ref, w_pin_v, w_gate_v, pe_v, sem):
        _load_ple_weights(w_pin_hbm, w_gate_hbm, 1, w_pin_v, w_gate_v, sem)
        _zero_first([loss_ref, dnf_ref])
        xv = h_ref[...]
        hn = xv + _ple_terms(xv, p_ref, n_ref, bg_ref, w_pin_v, w_gate_v, pe_v, a_ref, False)[0]
        y, yh, r = _rms(hn, nf_ref[...])
        diff = y - t_ref[...]
        loss_ref[...] += 0.5 * jnp.sum(jnp.mean(diff * diff, axis=-1, keepdims=True))
        dy = diff * (1.0 / D_MODEL)
        dhn, dnf = _rms_bwd(dy, yh, r, nf_ref[...])
        dnf_ref[...] += dnf
        dh_ref[...] = dhn

    vec = _const_spec((1, D_MODEL))
    return pl.pallas_call(
        body, name="ple_fwd1", grid=(nt,),
        in_specs=[_row_spec(tm, D_MODEL), _p_spec(tm, 1), _row_spec(tm, D_MODEL), vec, vec, vec, ANY, ANY],
        out_specs=[_row_spec(tm, D_MODEL), _const_spec((8, 128)), vec, _row_spec(tm, D_MODEL)],
        out_shape=[jax.ShapeDtypeStruct((T, D_MODEL), F32), jax.ShapeDtypeStruct((8, 128), F32),
                   jax.ShapeDtypeStruct((1, D_MODEL), F32), jax.ShapeDtypeStruct((T, D_MODEL), BF16)],
        scratch_shapes=[pltpu.VMEM((N_SHARD, PLE_DIM, pw), BF16), pltpu.VMEM((D_MODEL, D_MODEL), BF16),
                        pltpu.VMEM((tm, D_MODEL), F32), pltpu.SemaphoreType.DMA((2 * N_SHARD,))],
        compiler_params=_params(),
    )(h, p, tgt, nple, bg, nfin, w_pin, w_gate)


def _ple_bwd(dh, hb, p, a, nple, w_pin, w_gate, layer, kv_args=None):
    T = hb.shape[0]
    tm = min(512, T)
    nt = T // tm
    with_kv = kv_args is not None
    pw = D_MODEL // N_SHARD

    def body(*refs):
        if with_kv:
            (dh_ref, hb_ref, p_ref, a_ref, n_ref, w_pin_hbm, w_gate_hbm, hc_ref, dkv_ref, nkv_ref, w_kv_hbm,
             dhb_ref, dwpin_ref, dwgate_ref, dbg_ref, dn_ref, dwkv_ref, dnkv_ref,
             w_pin_v, w_gate_v, pe_v, w_kv_v, sem) = refs
        else:
            (dh_ref, hb_ref, p_ref, a_ref, n_ref, w_pin_hbm, w_gate_hbm,
             dhb_ref, dwpin_ref, dwgate_ref, dbg_ref, dn_ref, w_pin_v, w_gate_v, pe_v, sem) = refs
        pairs = [(w_pin_hbm, w_pin_v), (w_gate_hbm, w_gate_v)]
        if with_kv:
            pairs.append((w_kv_hbm, w_kv_v))
        _load_once(pairs, sem)
        _zero_first([dwpin_ref, dwgate_ref, dbg_ref, dn_ref] + ([dwkv_ref, dnkv_ref] if with_kv else []))
        do = dh_ref[...]
        if with_kv:
            dkvb = dkv_ref[...].astype(BF16)
            dkvn = _dot_nt(dkvb, w_kv_v[...])
            kvn, kh, kr = _rms(hc_ref[...], nkv_ref[...])
            dwkv_ref[...] += _dot_tn(kvn.astype(BF16), dkvb)
            dk, dnkv = _rms_bwd(dkvn, kh, kr, nkv_ref[...])
            dnkv_ref[...] += dnkv
            do = do + dk
        _, pe, gate, xgb, xh, r = _ple_terms(hb_ref[...], p_ref, n_ref, None, w_pin_v, w_gate_v, pe_v, a_ref, True)
        dpe = (do * gate).astype(BF16)
        pb = p_ref[...].astype(BF16)
        for j in range(N_SHARD):
            dwpin_ref[j] += _dot_tn(pb, dpe[:, j * pw:(j + 1) * pw])
        da = do * pe * (gate * (1.0 - gate))
        dab = da.astype(BF16)
        dbg_ref[...] += jnp.sum(da, axis=0, keepdims=True)
        dxg = _dot_nt(dab, w_gate_v[...])
        dwgate_ref[...] += _dot_tn(xgb, dab)
        dxx, dn = _rms_bwd(dxg, xh, r, n_ref[...])
        dn_ref[...] += dn
        dhb_ref[...] = do + dxx

    vec = _const_spec((1, D_MODEL))
    row = _row_spec(tm, D_MODEL)
    in_specs = [row, row, _p_spec(tm, layer), row, vec, ANY, ANY]
    args = [dh, hb, p, a, nple, w_pin, w_gate]
    out_specs = [row, _const_spec((N_SHARD, PLE_DIM, pw)), _const_spec((D_MODEL, D_MODEL)), vec, vec]
    out_shape = [jax.ShapeDtypeStruct((T, D_MODEL), F32), jax.ShapeDtypeStruct((N_SHARD, PLE_DIM, pw), F32),
                 jax.ShapeDtypeStruct((D_MODEL, D_MODEL), F32),
                 jax.ShapeDtypeStruct((1, D_MODEL), F32), jax.ShapeDtypeStruct((1, D_MODEL), F32)]
    scratch = [pltpu.VMEM((N_SHARD, PLE_DIM, pw), BF16), pltpu.VMEM((D_MODEL, D_MODEL), BF16),
               pltpu.VMEM((tm, D_MODEL), F32)]
    if with_kv:
        hc, dkv, nkv, w_kv = kv_args
        in_specs += [row, _row_spec(tm, 2 * KV_DIM), vec, ANY]
        args += [hc, dkv, nkv, w_kv]
        out_specs += [_const_spec((D_MODEL, 2 * KV_DIM)), vec]
        out_shape += [jax.ShapeDtypeStruct((D_MODEL, 2 * KV_DIM), F32), jax.ShapeDtypeStruct((1, D_MODEL), F32)]
        scratch.append(pltpu.VMEM((D_MODEL, 2 * KV_DIM), BF16))
    scratch.append(pltpu.SemaphoreType.DMA((3,)))
    return pl.pallas_call(
        body, name=f"ple_bwd{layer}", grid=(nt,), in_specs=in_specs, out_specs=out_specs,
        out_shape=out_shape, scratch_shapes=scratch, compiler_params=_params(),
    )(*args)


GROUP_ROWS = GQA_GROUP * BLOCK


def _stack_heads(x, kh):
    return jnp.concatenate([x[:, (kh * GQA_GROUP + g) * HEAD_DIM:(kh * GQA_GROUP + g + 1) * HEAD_DIM]
                            for g in range(GQA_GROUP)], axis=0)


def _attn_fwd(h, nmix, kv, sinks, w_q, w_o):
    T = h.shape[0]
    tm = min(512, T)
    nt = T // tm
    nb = tm // BLOCK

    def body(h_ref, n_ref, kv_ref, kvp_ref, sink_ref, w_q_hbm, w_o_hbm,
             out_ref, q_ref, ao_ref, p_ref, psink_ref, w_q_v, w_o_v, kvs_v, sem):
        _load_once([(w_q_hbm, w_q_v), (w_o_hbm, w_o_v)], sem)
        ti = pl.program_id(0)
        xv = h_ref[...]
        xn = _rms(xv, n_ref[...])[0].astype(BF16)
        q_ref[...] = (_dot(xn, w_q_v[...]) * (HEAD_DIM ** -0.5)).astype(BF16)
        kvs_v[0:BLOCK, :] = kvp_ref[...]
        kvs_v[BLOCK:, :] = kv_ref[...]
        lane = lax.broadcasted_iota(jnp.int32, (BLOCK, 128), 1)
        ii = lax.broadcasted_iota(jnp.int32, (BLOCK, 2 * BLOCK), 0)
        jj = lax.broadcasted_iota(jnp.int32, (BLOCK, 2 * BLOCK), 1)
        dist = ii + BLOCK - jj
        inband = (dist >= 0) & (dist < BLOCK)
        distf = dist.astype(F32)

        def blk_body(b, carry):
            r0 = pl.multiple_of(b * BLOCK, BLOCK)
            valid = inband & ((jj >= BLOCK) | jnp.logical_not(jnp.logical_and(ti == 0, b == 0)))
            qb = q_ref[pl.ds(r0, BLOCK), :]
            band = kvs_v[pl.ds(r0, 2 * BLOCK), :]
            psink_mat = jnp.zeros((BLOCK, 128), F32)
            outs = []
            for hq in range(N_Q_HEADS):
                kh, g = divmod(hq, GQA_GROUP)
                k_h = band[:, kh * HEAD_DIM:(kh + 1) * HEAD_DIM]
                v_h = band[:, KV_DIM + kh * HEAD_DIM:KV_DIM + (kh + 1) * HEAD_DIM]
                s = _dot_nt(qb[:, hq * HEAD_DIM:(hq + 1) * HEAD_DIM], k_h) - _SLOPES[hq] * distf
                s = jnp.where(valid, s, NEG)
                sink = sink_ref[hq]
                m = jnp.maximum(jnp.max(s, axis=1, keepdims=True), sink)
                e = jnp.exp(s - m)
                esink = jnp.exp(sink - m)
                inv = 1.0 / (jnp.sum(e, axis=1, keepdims=True) + esink)
                pb = (e * inv).astype(BF16)
                p_ref[b, kh, g * BLOCK:(g + 1) * BLOCK, :] = pb
                outs.append(_dot(pb, v_h))
                psink_mat = jnp.where(lane == hq, esink * inv, psink_mat)
            ao_ref[pl.ds(r0, BLOCK), :] = jnp.concatenate(outs, axis=1).astype(BF16)
            psink_ref[pl.ds(r0, BLOCK), :] = psink_mat
            return carry

        lax.fori_loop(0, nb, blk_body, 0)
        out_ref[...] = xv + _dot(ao_ref[...], w_o_v[...])

    row = _row_spec(tm, D_MODEL)
    prev_spec = pl.BlockSpec((BLOCK, 2 * KV_DIM), lambda i: (jnp.maximum(i * nb - 1, 0), 0))
    return pl.pallas_call(
        body, name="attn_fwd", grid=(nt,),
        in_specs=[row, _const_spec((1, D_MODEL)), _row_spec(tm, 2 * KV_DIM), prev_spec, SMEM, ANY, ANY],
        out_specs=[row, row, row, pl.BlockSpec((nb, N_KV_HEADS, GROUP_ROWS, 2 * BLOCK), lambda i: (i, 0, 0, 0)),
                   _row_spec(tm, 128)],
        out_shape=[jax.ShapeDtypeStruct((T, D_MODEL), F32), jax.ShapeDtypeStruct((T, D_MODEL), BF16),
                   jax.ShapeDtypeStruct((T, D_MODEL), BF16),
                   jax.ShapeDtypeStruct((T // BLOCK, N_KV_HEADS, GROUP_ROWS, 2 * BLOCK), BF16),
                   jax.ShapeDtypeStruct((T, 128), F32)],
        scratch_shapes=[pltpu.VMEM((D_MODEL, D_MODEL), BF16), pltpu.VMEM((D_MODEL, D_MODEL), BF16),
                        pltpu.VMEM((tm + BLOCK, 2 * KV_DIM), BF16), pltpu.SemaphoreType.DMA((2,))],
        compiler_params=_params(),
    )(h, nmix, kv, kv, sinks, w_q, w_o)


def _attn_bwd(dh, h, q, kv, ao, p, psink, nmix, w_q, w_o):
    T = h.shape[0]
    tm = min(512, T)
    nt = T // tm
    nb = tm // BLOCK

    def body(dh_ref, h_ref, q_ref, kv_ref, kvp_ref, ao_ref, p_ref, psink_ref, n_ref, w_q_hbm, w_o_hbm,
             dhin_ref, dwq_ref, dwo_ref, dkv_ref, dsink_ref, dn_ref,
             w_q_v, w_o_v, kvs_v, dao_v, dq_v, dkv_v, carry_v, sem):
        _load_once([(w_q_hbm, w_q_v), (w_o_hbm, w_o_v)], sem)
        _zero_first([carry_v, dsink_ref, dn_ref, dwq_ref, dwo_ref])
        dout = dh_ref[...]
        doutb = dout.astype(BF16)
        dao_v[...] = _dot_nt(doutb, w_o_v[...])
        dwo_ref[...] += _dot_tn(ao_ref[...], doutb)
        kvs_v[0:BLOCK, :] = kvp_ref[...]
        kvs_v[BLOCK:, :] = kv_ref[...]
        dkv_v[0:tm, :] = jnp.zeros((tm, 2 * KV_DIM), F32)
        dkv_v[tm:, :] = carry_v[...]
        seg = (lax.broadcasted_iota(jnp.int32, (D_MODEL, 128), 0) // HEAD_DIM
               == lax.broadcasted_iota(jnp.int32, (D_MODEL, 128), 1)).astype(BF16)

        def blk_body(b, dsk):
            r0 = pl.multiple_of(b * BLOCK, BLOCK)
            qb = q_ref[pl.ds(r0, BLOCK), :]
            band = kvs_v[pl.ds(r0, 2 * BLOCK), :]
            aob = ao_ref[pl.ds(r0, BLOCK), :].astype(F32)
            daob = dao_v[pl.ds(r0, BLOCK), :]
            prod = daob * aob
            head = prod.astype(BF16)
            tail = (prod - head.astype(F32)).astype(BF16)
            dsk = dsk + psink_ref[pl.ds(r0, BLOCK), :] * (_dot(head, seg) + _dot(tail, seg))
            dqs = []
            dks = []
            dvs = []
            for kh in range(N_KV_HEADS):
                k_h = band[:, kh * HEAD_DIM:(kh + 1) * HEAD_DIM]
                v_h = band[:, KV_DIM + kh * HEAD_DIM:KV_DIM + (kh + 1) * HEAD_DIM]
                q_g = _stack_heads(qb, kh)
                dao_g = _stack_heads(daob, kh)
                prb = p_ref[b, kh]
                pr = prb.astype(F32)
                dd = jnp.sum(dao_g * _stack_heads(aob, kh), axis=1, keepdims=True)
                dao_gb = dao_g.astype(BF16)
                dp = _dot_nt(dao_gb, v_h)
                dsb = (pr * (dp - dd)).astype(BF16)
                dq_g = _dot(dsb, k_h) * (HEAD_DIM ** -0.5)
                dks.append(_dot_tn(dsb, q_g))
                dvs.append(_dot_tn(prb, dao_gb))
                for g in range(GQA_GROUP):
                    dqs.append(dq_g[g * BLOCK:(g + 1) * BLOCK])
            dq_v[pl.ds(r0, BLOCK), :] = jnp.concatenate(dqs, axis=1)
            dkv_v[pl.ds(r0, 2 * BLOCK), :] += jnp.concatenate(dks + dvs, axis=1)
            return dsk

        dsk = lax.fori_loop(0, nb, blk_body, jnp.zeros((BLOCK, 128), F32))
        dsink_ref[...] -= jnp.sum(dsk, axis=0, keepdims=True)
        dqb = dq_v[...].astype(BF16)
        dxn = _dot_nt(dqb, w_q_v[...])
        xn, xh, r = _rms(h_ref[...], n_ref[...])
        dwq_ref[...] += _dot_tn(xn.astype(BF16), dqb)
        dxx, dn = _rms_bwd(dxn, xh, r, n_ref[...])
        dn_ref[...] += dn
        dhin_ref[...] = dout + dxx
        dkv_ref[...] = dkv_v[BLOCK:, :]
        carry_v[...] = dkv_v[0:BLOCK, :]

    rev = functools.partial(_row_spec, rev_nt=nt)
    row = rev(tm, D_MODEL)
    prev_spec = pl.BlockSpec((BLOCK, 2 * KV_DIM), lambda i: (jnp.maximum((nt - 1 - i) * nb - 1, 0), 0))
    return pl.pallas_call(
        body, name="attn_bwd", grid=(nt,),
        in_specs=[row, row, row, rev(tm, 2 * KV_DIM), prev_spec, row,
                  pl.BlockSpec((nb, N_KV_HEADS, GROUP_ROWS, 2 * BLOCK), lambda i: (nt - 1 - i, 0, 0, 0)),
                  rev(tm, 128), _const_spec((1, D_MODEL)), ANY, ANY],
        out_specs=[row, _const_spec((D_MODEL, D_MODEL)), _const_spec((D_MODEL, D_MODEL)), rev(tm, 2 * KV_DIM),
                   _const_spec((8, 128)), _const_spec((1, D_MODEL))],
        out_shape=[jax.ShapeDtypeStruct((T, D_MODEL), F32), jax.ShapeDtypeStruct((D_MODEL, D_MODEL), F32),
                   jax.ShapeDtypeStruct((D_MODEL, D_MODEL), F32), jax.ShapeDtypeStruct((T, 2 * KV_DIM), F32),
                   jax.ShapeDtypeStruct((8, 128), F32), jax.ShapeDtypeStruct((1, D_MODEL), F32)],
        scratch_shapes=[pltpu.VMEM((D_MODEL, D_MODEL), BF16), pltpu.VMEM((D_MODEL, D_MODEL), BF16),
                        pltpu.VMEM((tm + BLOCK, 2 * KV_DIM), BF16), pltpu.VMEM((tm, D_MODEL), F32),
                        pltpu.VMEM((tm, D_MODEL), F32), pltpu.VMEM((tm + BLOCK, 2 * KV_DIM), F32),
                        pltpu.VMEM((BLOCK, 2 * KV_DIM), F32), pltpu.SemaphoreType.DMA((2,))],
        compiler_params=_params(),
    )(dh, h, q, kv, kv, ao, p, psink, nmix, w_q, w_o)


def _mesh_pos():
    return lax.axis_index("x"), lax.axis_index("y"), lax.axis_index("c")


def _other_chips(x, y):
    return [(1 - x, y), (x, 1 - y), (1 - x, 1 - y)]


HBM_SPEC = pl.BlockSpec(memory_space=pltpu.HBM)
SEM_SPEC = pl.BlockSpec(memory_space=pltpu.SEMAPHORE)


def _split_call(name, bufs, waits=(), starts=(), after=()):
    n, nw, ns, na = len(bufs), len(waits), len(starts), len(after)

    def body(*refs):
        brefs = refs[:n]
        wsems = [(refs[n + 2 * k], refs[n + 2 * k + 1]) for k in range(nw)]
        o = n + 2 * nw + na
        ssems = [(refs[o + 2 * k], refs[o + 2 * k + 1]) for k in range(ns)]
        for (ss, rs), (_, _, fn) in zip(wsems, waits):
            for sending, arriving in fn(brefs, ss, rs):
                sending.wait_send()
                arriving.wait_recv()
        for (ss, rs), (_, fn) in zip(ssems, starts):
            for sending, _ in fn(brefs, ss, rs):
                sending.start()
        if ns:
            token = refs[o + 2 * ns + n]
            token[...] = jnp.zeros(token.shape, token.dtype)

    out_shape, out_specs = [], []
    for cnt, _ in starts:
        out_shape += [pltpu.SemaphoreType.DMA((cnt,)), pltpu.SemaphoreType.DMA((cnt,))]
        out_specs += [SEM_SPEC, SEM_SPEC]
    out_shape += [pltpu.HBM(b.shape, b.dtype) for b in bufs]
    out_specs += [HBM_SPEC] * n
    if ns:
        out_shape.append(jax.ShapeDtypeStruct((8, 128), F32))
        out_specs.append(pl.BlockSpec(memory_space=pltpu.VMEM))
    args = [pltpu.with_memory_space_constraint(b, pltpu.HBM) for b in bufs]
    for ss, rs, _ in waits:
        args += [ss, rs]
    args += list(after)
    res = pl.pallas_call(
        body, name=name, out_shape=tuple(out_shape),
        in_specs=[HBM_SPEC] * n + [SEM_SPEC] * (2 * nw) + [ANY] * na, out_specs=tuple(out_specs),
        input_output_aliases={i: 2 * ns + i for i in range(n)},
        compiler_params=pltpu.CompilerParams(has_side_effects=pltpu.SideEffectType.DATAFLOW_SIDE_EFFECTING),
    )(*args)
    sems = [(res[2 * k], res[2 * k + 1]) for k in range(ns)]
    return list(res[2 * ns:2 * ns + n]), sems, (res[2 * ns + n] if ns else None)


def _cast_place(items, name, deps=()):
    n = len(items)
    mats = [a.shape[-2:] for a, _, _ in items]

    def body(*refs):
        ins, outs, scr, sem = refs[:n], refs[n:2 * n], refs[2 * n:3 * n], refs[3 * n]
        x, y, _ = _mesh_pos()
        cps = []
        for t in range(n):
            scr[t][...] = ins[t][...].astype(scr[t].dtype)
            cp = pltpu.make_async_copy(scr[t], outs[t].at[2 * x + y], sem.at[t])
            cp.start()
            cps.append(cp)
        for cp in cps:
            cp.wait()

    def spec(idx, shape):
        return pl.BlockSpec((None,) * len(idx) + tuple(shape), lambda i: tuple(idx) + (0, 0))

    body, in_specs, args = _add_deps(body, [spec(idx, mat) for (_, idx, _), mat in zip(items, mats)],
                                     [a for a, _, _ in items], deps)
    return pl.pallas_call(
        body, name=name, grid=(1,), in_specs=in_specs, out_specs=[ANY] * n,
        out_shape=[jax.ShapeDtypeStruct((N_SHARD,) + tuple(mat), dt) for (_, _, dt), mat in zip(items, mats)],
        scratch_shapes=[pltpu.VMEM(tuple(mat), dt) for (_, _, dt), mat in zip(items, mats)]
        + [pltpu.SemaphoreType.DMA((n,))],
        compiler_params=_params(),
    )(*args)


def _gather_ici(idx):
    def fn(bufs, ss, rs):
        x, y, c = _mesh_pos()
        pairs = []
        for k, t in enumerate(idx):
            half = bufs[t].shape[1] // 2
            mine = bufs[t].at[2 * x + y, pl.ds(c * half, half), :]
            for j, (cx, cy) in enumerate(_other_chips(x, y)):
                theirs = bufs[t].at[2 * cx + cy, pl.ds(c * half, half), :]
                sem = dict(send_sem=ss.at[3 * k + j], recv_sem=rs.at[3 * k + j],
                           device_id=(cx, cy, c), device_id_type=MESH)
                pairs.append((pltpu.make_async_remote_copy(src_ref=mine, dst_ref=mine, **sem),
                              pltpu.make_async_remote_copy(src_ref=mine, dst_ref=theirs, **sem)))
        return pairs
    return fn


def _gather_d2d(idx):
    def fn(bufs, ss, rs):
        x, y, c = _mesh_pos()
        pairs = []
        for k, t in enumerate(idx):
            half = bufs[t].shape[1] // 2
            for j, (cx, cy) in enumerate(_other_chips(x, y)):
                got = bufs[t].at[2 * cx + cy, pl.ds(c * half, half), :]
                theirs = bufs[t].at[2 * cx + cy, pl.ds((1 - c) * half, half), :]
                sem = dict(send_sem=ss.at[3 * k + j], recv_sem=rs.at[3 * k + j],
                           device_id=(x, y, 1 - c), device_id_type=MESH)
                pairs.append((pltpu.make_async_remote_copy(src_ref=got, dst_ref=got, **sem),
                              pltpu.make_async_remote_copy(src_ref=got, dst_ref=theirs, **sem)))
        return pairs
    return fn


def _alloc(shapes, name):
    def body(*refs):
        pass

    return pl.pallas_call(body, name=name, out_specs=[ANY] * len(shapes),
                          out_shape=[jax.ShapeDtypeStruct(s, d) for s, d in shapes])()


def _send_to_sibling(n):
    def fn(bufs, ss, rs):
        x, y, c = _mesh_pos()
        pairs = []
        for t in range(n):
            src = bufs[t]
            if len(src.shape) == 3:
                half = src.shape[1] // 2
                src = src.at[:, pl.ds((1 - c) * half, half), :]
            cp = pltpu.make_async_remote_copy(src_ref=src, dst_ref=bufs[n + t], send_sem=ss.at[t],
                                              recv_sem=rs.at[t], device_id=(x, y, 1 - c), device_id_type=MESH)
            pairs.append((cp, cp))
        return pairs
    return fn


def _send_to_chips(n):
    def fn(bufs, ss, rs):
        x, y, c = _mesh_pos()
        pairs = []
        for j, (cx, cy) in enumerate(_other_chips(x, y)):
            for t in range(n):
                src = bufs[t].at[j] if len(bufs[t].shape) == 3 else bufs[t]
                cp = pltpu.make_async_remote_copy(src_ref=src, dst_ref=bufs[n + t].at[j], send_sem=ss.at[3 * t + j],
                                                  recv_sem=rs.at[3 * t + j], device_id=(cx, cy, c),
                                                  device_id_type=MESH)
                pairs.append((cp, cp))
        return pairs
    return fn


class _Exchange:
    def __init__(self, name, srcs, land_shapes, fn, n_sems):
        self.name, self.fn = name, fn
        lands = _alloc(land_shapes, name + "_alloc")
        self.n = len(srcs)
        self.bufs, sems, self.token = _split_call(name + "_start", list(srcs) + list(lands),
                                                  starts=[(n_sems, fn)])
        self.sems = sems[0]

    def finish(self, after=()):
        bufs, _, _ = _split_call(self.name + "_wait", self.bufs, waits=[(*self.sems, self.fn)], after=after)
        return bufs[:self.n], bufs[self.n:]


def _row_block(rows, cols, mult=8, limit=3 * 512 * 1024, itemsize=4):
    best = None
    for br in range(mult, rows + 1, mult):
        if rows % br == 0 and br * cols * itemsize <= limit:
            best = br
    assert best is not None, (rows, cols)
    return best


_GROUP_BLOCK_BYTES = 1024 * 1024


def _group_plan(ss):
    plan = []
    for s in ss:
        half, cols = s.shape[-2:]
        br = _row_block(half, cols, mult=16, limit=_GROUP_BLOCK_BYTES)
        plan.append((br, half // br))
    return plan, max(nr for _, nr in plan)


def _chip_partial(gs, ss, ids, name):
    n = len(gs)
    plan, steps = _group_plan(ss)

    def body(ids_ref, *refs):
        for t in range(n):
            refs[2 * n + t][...] = (refs[t][...] + refs[n + t][...]).astype(BF16)

    g_specs, s_specs, o_specs = [], [], []
    for (br, nr), s in zip(plan, ss):
        blk = (None, br, s.shape[2])
        g_specs.append(pl.BlockSpec(
            blk, lambda j, r, ids_ref, nr=nr: (ids_ref[2 + j], ids_ref[0] * nr + jnp.minimum(r, nr - 1), 0)))
        s_specs.append(pl.BlockSpec(blk, lambda j, r, ids_ref, nr=nr: (ids_ref[2 + j], jnp.minimum(r, nr - 1), 0)))
        o_specs.append(pl.BlockSpec(blk, lambda j, r, ids_ref, nr=nr: (j, jnp.minimum(r, nr - 1), 0)))
    return pl.pallas_call(
        body, name=name,
        grid_spec=pltpu.PrefetchScalarGridSpec(num_scalar_prefetch=1, grid=(3, steps),
                                               in_specs=g_specs + s_specs, out_specs=o_specs),
        out_shape=[jax.ShapeDtypeStruct((3,) + s.shape[1:], BF16) for s in ss],
        compiler_params=pltpu.CompilerParams(dimension_semantics=("arbitrary", "arbitrary"),
                                             vmem_limit_bytes=VMEM_LIMIT),
    )(ids, *gs, *ss)


def _chip_sum(gs, ss, qs, ids, name):
    n = len(gs)
    plan, steps = _group_plan(ss)

    def body(ids_ref, *refs):
        for t in range(n):
            q_ref = refs[2 * n + t]
            own = refs[t][...] + refs[n + t][...]
            refs[3 * n + t][...] = (own + q_ref[2].astype(F32)) + (q_ref[0].astype(F32) + q_ref[1].astype(F32))

    g_specs, s_specs, q_specs, o_specs = [], [], [], []
    for (br, nr), s in zip(plan, ss):
        cols = s.shape[2]
        g_specs.append(pl.BlockSpec(
            (None, br, cols), lambda r, ids_ref, nr=nr: (ids_ref[1], ids_ref[0] * nr + jnp.minimum(r, nr - 1), 0)))
        s_specs.append(pl.BlockSpec((None, br, cols), lambda r, ids_ref, nr=nr: (ids_ref[1], jnp.minimum(r, nr - 1), 0)))
        q_specs.append(pl.BlockSpec((3, br, cols), lambda r, ids_ref, nr=nr: (0, jnp.minimum(r, nr - 1), 0)))
        o_specs.append(pl.BlockSpec((br, cols), lambda r, ids_ref, nr=nr: (jnp.minimum(r, nr - 1), 0)))
    return pl.pallas_call(
        body, name=name,
        grid_spec=pltpu.PrefetchScalarGridSpec(num_scalar_prefetch=1, grid=(steps,),
                                               in_specs=g_specs + s_specs + q_specs, out_specs=o_specs),
        out_shape=[jax.ShapeDtypeStruct(s.shape[1:], F32) for s in ss],
        compiler_params=pltpu.CompilerParams(dimension_semantics=("arbitrary",), vmem_limit_bytes=VMEM_LIMIT),
    )(ids, *gs, *ss, *qs)


def _adamw_math(w, g, m, v):
    mn = ADAM_B1 * m + (1.0 - ADAM_B1) * g
    vn = ADAM_B2 * v + (1.0 - ADAM_B2) * (g * g)
    m_hat = mn / (1.0 - ADAM_B1 ** ADAM_STEP)
    v_hat = vn / (1.0 - ADAM_B2 ** ADAM_STEP)
    return -ADAM_LR * (m_hat / (jnp.sqrt(v_hat) + ADAM_EPS) + ADAM_WD * w), mn, vn


def _adamw_halves(w, own, sib, m, v, ids, name, layer=0, n_layers=1, stacked=None):
    C = w.shape[1]
    R = w.shape[0] // n_layers
    half = R // 2
    br = _row_block(half, C)
    nh = half // br
    base = layer * 2 * nh

    def body(ids_ref, w_ref, own_ref, sib_ref, m_ref, v_ref, *rest):
        g_ref, d_ref, mo_ref, vo_ref = rest[-4:]
        is_own = (pl.program_id(0) // nh) == ids_ref[0]
        g = jnp.where(is_own, own_ref[...], sib_ref[...])
        g_ref[...] = g
        d_ref[...], mo_ref[...], vo_ref[...] = _adamw_math(w_ref[...], g, m_ref[...], v_ref[...])

    full = pl.BlockSpec((br, C), lambda r, ids_ref: (base + r, 0))
    own_spec = pl.BlockSpec((br, C), lambda r, ids_ref: (jnp.clip(r - ids_ref[0] * nh, 0, nh - 1), 0))
    sib_spec = pl.BlockSpec((br, C), lambda r, ids_ref: (jnp.clip(r - (1 - ids_ref[0]) * nh, 0, nh - 1), 0))
    in_specs = [full, own_spec, sib_spec, full, full]
    args = [ids, w, own, sib, m, v]
    aliases = {}
    if stacked is not None:
        in_specs += [ANY] * 4
        args += list(stacked)
        aliases = {6 + k: k for k in range(4)}
    return pl.pallas_call(
        body, name=name,
        grid_spec=pltpu.PrefetchScalarGridSpec(
            num_scalar_prefetch=1, grid=(2 * nh,), in_specs=in_specs, out_specs=[full] * 4),
        out_shape=[jax.ShapeDtypeStruct(w.shape, F32)] * 4, input_output_aliases=aliases,
        compiler_params=_params(),
    )(*args)


_PACK_UNIT = 1024


def _pack(arrs):
    flat = []
    for a in arrs:
        f = a.reshape(-1).astype(F32)
        pad = (-f.shape[0]) % _PACK_UNIT
        if pad:
            f = jnp.concatenate([f, jnp.zeros((pad,), F32)])
        flat.append(f)
    return jnp.concatenate(flat).reshape(-1, 128)


def kernel(x, p, norm_mix, norm_ffn, norm_ple, norm_kv, norm_final, a_w_in, a_norm_v, a_w_s, a_b_s, a_w_out, w_kv, b_w_q, b_sinks, b_w_o, f_w_up, f_conv_w, f_conv_b, f_w_down, ple_w_in, ple_w_gate, ple_b_gate, loss_target, m_norm_mix, m_norm_ffn, m_norm_ple, m_norm_kv, m_norm_final, m_a_w_in, m_a_norm_v, m_a_w_s, m_a_b_s, m_a_w_out, m_w_kv, m_b_w_q, m_b_sinks, m_b_w_o, m_f_w_up, m_f_conv_w, m_f_conv_b, m_f_w_down, m_ple_w_in, m_ple_w_gate, m_ple_b_gate, v_norm_mix, v_norm_ffn, v_norm_ple, v_norm_kv, v_norm_final, v_a_w_in, v_a_norm_v, v_a_w_s, v_a_b_s, v_a_w_out, v_w_kv, v_b_w_q, v_b_sinks, v_b_w_o, v_f_w_up, v_f_conv_w, v_f_conv_b, v_f_w_down, v_ple_w_in, v_ple_w_gate, v_ple_b_gate):
    given = dict(locals())

    small_shard = _pack([a_norm_v, f_conv_w])
    pad_rows = (-small_shard.shape[0]) % 16
    if pad_rows:
        small_shard = jnp.concatenate([small_shard, jnp.zeros((pad_rows, 128), F32)])
    groups = [
        [(a_w_in, (0,), BF16), (a_w_out, (0,), BF16), (small_shard, (), F32)],
        [(f_w_up, (0,), BF16), (f_w_down, (0,), BF16)],
        [(ple_w_in, (0,), BF16), (ple_w_gate, (0,), BF16), (w_kv, (), BF16), (b_w_q, (0,), BF16),
         (b_w_o, (0,), BF16), (f_w_up, (1,), BF16), (f_w_down, (1,), BF16), (ple_w_in, (1,), BF16),
         (ple_w_gate, (1,), BF16)],
    ]
    first = list(range(len(groups[0])))
    lands0, sems0, token0 = _split_call("gather_start_g0", _cast_place(groups[0], "cast_place_g0"),
                                        starts=[(3 * len(first), _gather_ici(first))])
    rest, spans, start = [], [], 0
    for gi, items in enumerate(groups[1:], 1):
        rest += _cast_place(items, f"cast_place_g{gi}", deps=(token0,))
        spans.append(list(range(start, start + len(items))))
        start += len(items)
    rest, rest_sems, rest_token = _split_call("gather_start", rest,
                                              starts=[(3 * len(sp), _gather_ici(sp)) for sp in spans])
    group_bufs = [lands0] + [[rest[t] for t in sp] for sp in spans]
    ici_sems = sems0 + rest_sems

    def finish_group(gi, after):
        bufs = group_bufs[gi]
        local = list(range(len(bufs)))
        bufs, d2d_sems, _ = _split_call(f"gather_pass_g{gi}", bufs, waits=[(*ici_sems[gi], _gather_ici(local))],
                                        starts=[(3 * len(local), _gather_d2d(local))], after=after)
        bufs, _, _ = _split_call(f"gather_done_g{gi}", bufs, waits=[(*d2d_sems[0], _gather_d2d(local))])
        return bufs

    def stage0():
        b_in, b_out, b_small = finish_group(0, (rest_token,))
        small_full = b_small.reshape(N_SHARD, -1)
        gv_full = small_full[:, :256].reshape(1, D_MODEL)
        cw_full = small_full[:, _PACK_UNIT:_PACK_UNIT + 2 * 3 * FF_BLK].reshape(N_SHARD, 2, 3, FF_BLK)
        cw_full = jnp.transpose(cw_full, (1, 2, 0, 3)).reshape(2, 3, N_FF)
        return gv_full, cw_full, b_in, b_out.reshape(D_MODEL, D_MODEL)

    def stage1(after):
        b_up, b_dn = finish_group(1, after)
        return b_up, b_dn.reshape(D_FF, D_MODEL)

    def stage2(after):
        pin0, gate0, kv_w, wq, wo, up1, dn1, pin1, gate1 = finish_group(2, after)
        sq = lambda a: a.reshape(D_MODEL, -1)
        return dict(w_pin=[pin0, pin1], w_gate=[sq(gate0), sq(gate1)], w_kv=sq(kv_w), w_q=sq(wq), w_o=sq(wo),
                    w_up1=up1, w_dn1=dn1.reshape(D_FF, D_MODEL))

    dx, (loss, (out_g, out_d, out_m, out_v)) = _local_step(
        x[0], p.reshape(2, -1, PLE_DIM), loss_target[0], norm_mix, norm_ffn, norm_ple, norm_kv, norm_final, a_w_s, a_b_s,
        b_sinks, f_conv_b, ple_b_gate, stage0, stage1, stage2, _Reducer(given))
    weight_names = ['norm_mix', 'norm_ffn', 'norm_ple', 'norm_kv', 'norm_final', 'a_w_in', 'a_norm_v', 'a_w_s',
                    'a_b_s', 'a_w_out', 'w_kv', 'b_w_q', 'b_sinks', 'b_w_o', 'f_w_up', 'f_conv_w', 'f_conv_b',
                    'f_w_down', 'ple_w_in', 'ple_w_gate', 'ple_b_gate']
    return (loss, dx.reshape(x.shape), *[out_g[k] for k in weight_names], *[out_d[k] for k in weight_names],
            *[out_m[k] for k in weight_names], *[out_v[k] for k in weight_names])


def _local_step(xs, p, tgt, norm_mix, norm_ffn, norm_ple, norm_kv, norm_final, a_w_s, a_b_s, b_sinks,
                f_conv_b, ple_b_gate, stage0, stage1, stage2, sched):
    tril = jnp.tril(jnp.ones((CHUNK, CHUNK), F32))
    wsm = (a_w_s[0] * tril[None]).astype(BF16)
    bsb = jnp.broadcast_to(a_b_s[0][:, :, None], (A_GROUPS, CHUNK, CHUNK))
    sinks = b_sinks[0]
    row = lambda a: a.reshape(1, -1)

    gv_full, cw_full, w_in, w_out = stage0()
    h1, zp = _mixer_a_fwd(xs, row(norm_mix[0]), gv_full, wsm, bsb, w_in, w_out)
    w_up0, w_dn0 = stage1((h1,))
    h2, hh0, c0 = _ffn_fwd(h1, row(norm_ffn[0]), cw_full[0], row(f_conv_b[0]), w_up0, w_dn0, 0)
    rest = stage2((h2,))
    w_pin, w_gate, w_kv_f, w_q, w_o = rest['w_pin'], rest['w_gate'], rest['w_kv'], rest['w_q'], rest['w_o']
    w_up = [w_up0, rest['w_up1']]
    w_dn = [w_dn0, rest['w_dn1']]
    h3, kv, a0 = _ple_fwd_kv(h2, p, row(norm_ple[0]), row(ple_b_gate[0]), row(norm_kv), w_pin[0], w_gate[0], w_kv_f)
    h4, q, ao, probs, psink = _attn_fwd(h3, row(norm_mix[1]), kv, sinks, w_q, w_o)
    h5, hh1, c1 = _ffn_fwd(h4, row(norm_ffn[1]), cw_full[1], row(f_conv_b[1]), w_up[1], w_dn[1], 1)
    dh6, loss_acc, dn_final, a1 = _ple_fwd_final(
        h5, p, tgt, row(norm_ple[1]), row(ple_b_gate[1]), row(norm_final), w_pin[1], w_gate[1])

    def pieces(g):
        return g.reshape(N_SHARD, -1, g.shape[-1])

    dh5, g_pin1, g_gate1, dbg1, dnple1 = _ple_bwd(dh6, h5, p, a1, row(norm_ple[1]), w_pin[1], w_gate[1], 1)
    early = {('ple_w_in', 1): g_pin1, ('ple_w_gate', 1): pieces(g_gate1)}
    dh4, g_up1, g_dn1, dcw1, dcb1, dnffn1 = _ffn_bwd(
        dh5, h4, hh1, c1, row(norm_ffn[1]), cw_full[1], w_up[1], w_dn[1], 1)
    early['f_w_down', 1] = pieces(g_dn1)
    early['f_w_up', 1] = g_up1
    dh3a, g_wq, g_wo, dkv, dsink, dnmix1 = _attn_bwd(dh4, h3, q, kv, ao, probs, psink, row(norm_mix[1]), w_q, w_o)
    early['b_w_o', 0] = pieces(g_wo)
    early['b_w_q', 0] = pieces(g_wq)
    dh2, g_pin0, g_gate0, dbg0, dnple0, g_wkv, dnkv = _ple_bwd(
        dh3a, h2, p, a0, row(norm_ple[0]), w_pin[0], w_gate[0], 0,
        kv_args=(h3, dkv, row(norm_kv), w_kv_f))
    early['w_kv', 0] = pieces(g_wkv)
    early['ple_w_in', 0] = g_pin0
    early['ple_w_gate', 0] = pieces(g_gate0)
    deps = sched.early_ready(early)
    dh1, g_up0, g_dn0, dcw0, dcb0, dnffn0 = _ffn_bwd(
        dh2, h1, hh0, c0, row(norm_ffn[0]), cw_full[0], w_up[0], w_dn[0], 0, deps=deps,
        between=lambda part: sched.after_ffn_half((part,)))
    deps = sched.ffn0_ready({('f_w_down', 0): pieces(g_dn0), ('f_w_up', 0): g_up0})
    dx, g_win, g_wout, dws, dbs, dgv, dnmix0 = _mixer_a_bwd(
        dh1, xs, zp, row(norm_mix[0]), gv_full, wsm, bsb, tril, w_in, w_out, deps=deps)
    g_wout = pieces(g_wout)

    small_grads = {
        'norm_mix': jnp.concatenate([dnmix0, dnmix1]), 'norm_ffn': jnp.concatenate([dnffn0, dnffn1]),
        'norm_ple': jnp.concatenate([dnple0, dnple1]), 'norm_kv': dnkv, 'norm_final': dn_final,
        'a_norm_v': dgv, 'a_w_s': dws.reshape(A_GROUPS * CHUNK, CHUNK), 'a_b_s': dbs[:, :, 0],
        'b_sinks': dsink[0:1, :], 'f_conv_w': jnp.concatenate([dcw0, dcw1]),
        'f_conv_b': jnp.concatenate([dcb0, dcb1]), 'ple_b_gate': jnp.concatenate([dbg0, dbg1]),
        'loss': loss_acc,
    }
    outs = sched.finish({('a_w_in', 0): g_win, ('a_w_out', 0): g_wout}, small_grads, (dx,))
    return dx, outs


class _Reducer:
    def __init__(self, given):
        self.given = given
        cx, cy, cc = _mesh_pos()
        self.shard = 2 * cx + cy
        s = self.shard
        self.ids = jnp.stack([cc, s, s ^ 2, s ^ 1, s ^ 3]).astype(jnp.int32)
        self.out = [{}, {}, {}, {}]
        self.stacked = {}

    def _send(self, tag, grads, small=()):
        keys = list(grads)
        srcs = [grads[k] for k in keys] + list(small)
        shapes = [((N_SHARD, g.shape[1] // 2, g.shape[2]), F32) for g in srcs[:len(keys)]]
        shapes += [(s.shape, F32) for s in small]
        return keys, _Exchange(f"send_{tag}", srcs, shapes, _send_to_sibling(len(srcs)), len(srcs))

    def _exchange(self, tag, keys, send, after):
        srcs, lands = send.finish(after)
        n = len(keys)
        parts = _chip_partial(srcs[:n], lands[:n], self.ids, f"chip_partial_{tag}")
        shapes = [(p.shape, BF16) for p in parts]
        if len(srcs) > n:
            small = _small_add(srcs[n:], lands[n:])
            parts += small
            shapes += [((3,) + s.shape, F32) for s in small]
        exch = _Exchange(f"exch_{tag}", parts, shapes, _send_to_chips(len(parts)), 3 * len(parts))
        return (keys, srcs[:n], lands[:n], exch)

    def _swap(self, tag, state, after):
        keys, grads, sib, exch = state
        parts, recv = exch.finish(after)
        n = len(keys)
        own = _chip_sum(grads, sib, recv[:n], self.ids, f"chip_sum_{tag}")
        small_red = _small_sum(parts[n:], recv[n:]) if len(parts) > n else None
        return keys, _Exchange(f"swap_{tag}", own, [(o.shape, F32) for o in own], _send_to_sibling(n), n), small_red

    def _adamw(self, keys, swap, after):
        own, sib = swap.finish(after)
        last = None
        for (name, layer), o, s in zip(keys, own, sib):
            w = self.given[name]
            n_layers = w.shape[0] if w.ndim == 3 else 1
            c2 = w.shape[-1]
            res = _adamw_halves(w.reshape(-1, c2), o, s, self.given['m_' + name].reshape(-1, c2),
                                self.given['v_' + name].reshape(-1, c2), self.ids, f"adamw_{name}{layer}",
                                layer, n_layers, self.stacked.get(name))
            self.stacked[name] = res
            if layer == 0:
                for dst, r in zip(self.out, res):
                    dst[name] = r.reshape(w.shape)
            last = res[0]
        return last

    def early_ready(self, grads):
        self.e_keys, self.e_send = self._send("e", grads)
        return (self.e_send.token,)

    def after_ffn_half(self, after):
        self.e_state = self._exchange("e", self.e_keys, self.e_send, after)
        return (self.e_state[3].token,)

    def ffn0_ready(self, grads):
        _, self.e_swap, _ = self._swap("e", self.e_state, tuple(grads.values()))
        self.f_keys, self.f_send = self._send("f", grads)
        return (self.f_send.token, self.e_swap.token)

    def finish(self, grads, small_grads, after):
        small_names = list(small_grads)
        f_state = self._exchange("f", self.f_keys, self.f_send, after)
        a_keys, a_send = self._send("a", grads, [small_grads[k] for k in small_names])
        a_state = self._exchange("a", a_keys, a_send, (f_state[3].token,))
        e_done = self._adamw(self.e_keys, self.e_swap, (a_state[3].token,))
        f_keys, f_swap, _ = self._swap("f", f_state, (e_done,))
        f_done = self._adamw(f_keys, f_swap, ())
        _, a_swap, small_red = self._swap("a", a_state, (f_done,))
        self._adamw(a_keys, a_swap, ())

        given = self.given
        reduced = dict(zip(small_names, small_red))
        loss = reduced.pop('loss')[0, 0]
        names = list(reduced)
        items = []
        for k in names:
            g = reduced[k]
            cols = g.shape[1] // N_SHARD if k in ('a_norm_v', 'f_conv_w') else g.shape[1]
            view = lambda a: _lane_pad(a.reshape(g.shape[0], -1), cols)
            items.append((view(given[k]), g, view(given['m_' + k]), view(given['v_' + k])))
        res = _adamw_small(items, self.ids)
        for k, four in zip(names, res):
            width = given[k].size // four[0].shape[0]
            for dst, r in zip(self.out, four):
                dst[k] = r[:, :width].reshape(given[k].shape)
        return loss, self.out


def _lane_pad(a, cols):
    return a if a.shape[1] == cols else jnp.pad(a, ((0, 0), (0, cols - a.shape[1])))


def _small_add(a_list, b_list):
    n = len(a_list)

    def body(*refs):
        for t in range(n):
            refs[2 * n + t][...] = refs[t][...] + refs[n + t][...]

    return pl.pallas_call(body, name="chip_partial_small",
                          out_shape=[jax.ShapeDtypeStruct(a.shape, F32) for a in a_list])(*a_list, *b_list)


def _small_sum(parts, recvs):
    n = len(parts)

    def body(*refs):
        for t in range(n):
            q = refs[n + t]
            refs[2 * n + t][...] = (refs[t][...] + q[2]) + (q[0] + q[1])

    return pl.pallas_call(body, name="chip_sum_small",
                          out_shape=[jax.ShapeDtypeStruct(p.shape, F32) for p in parts])(*parts, *recvs)


def _adamw_small(items, ids):
    n = len(items)

    def body(ids_ref, *refs):
        for t in range(n):
            w_ref, g_ref, m_ref, v_ref = refs[4 * t:4 * t + 4]
            g_out, d_ref, mo_ref, vo_ref = refs[4 * n + 4 * t:4 * n + 4 * t + 4]
            g = g_ref[...]
            g_out[...] = g
            d_ref[...], mo_ref[...], vo_ref[...] = _adamw_math(w_ref[...], g, m_ref[...], v_ref[...])

    in_specs, out_specs, out_shape, args = [], [], [], []
    for w, g, m, v in items:
        full = pl.BlockSpec(w.shape, lambda i, ids_ref: (0, 0))
        g_spec = full if g.shape == w.shape else pl.BlockSpec(w.shape, lambda i, ids_ref: (0, ids_ref[1]))
        in_specs += [full, g_spec, full, full]
        out_specs += [full] * 4
        out_shape += [jax.ShapeDtypeStruct(w.shape, F32)] * 4
        args += [w, g, m, v]
    res = pl.pallas_call(
        body, name="adamw_small",
        grid_spec=pltpu.PrefetchScalarGridSpec(num_scalar_prefetch=1, grid=(1,), in_specs=in_specs,
                                               out_specs=out_specs),
        out_shape=out_shape, compiler_params=_params(),
    )(ids, *args)
    return [res[4 * t:4 * t + 4] for t in range(n)]
```

```python
import functools
import math

import numpy as np
import jax
import jax.numpy as jnp
from jax import lax
from jax.experimental import pallas as pl
from jax.experimental.pallas import tpu as pltpu

F32 = jnp.float32
BF16 = jnp.bfloat16

D_MODEL = 1024
CHUNK = 128
A_GROUPS = 8
HEAD_DIM = 64
N_Q_HEADS = 16
N_KV_HEADS = 4
GQA_GROUP = N_Q_HEADS // N_KV_HEADS
KV_DIM = N_KV_HEADS * HEAD_DIM
BLOCK = 128
D_FF = 2816
N_FF = 2 * D_FF
FF_BLK = N_FF // 4
PLE_DIM = 256
EPS = 1e-6
NEG = -1e30
N_SHARD = 4

ADAM_LR = 0.001
ADAM_B1 = 0.9
ADAM_B2 = 0.999
ADAM_EPS = 1e-08
ADAM_WD = 0.01
ADAM_STEP = 10

VMEM_LIMIT = 60 * 1024 * 1024
MESH = pl.DeviceIdType.MESH
ANY = pl.BlockSpec(memory_space=pl.ANY)
SMEM = pl.BlockSpec(memory_space=pltpu.SMEM)

_SLOPES = [float(np.float32(2.0 ** (-8.0 * (h + 1) / N_Q_HEADS))) for h in range(N_Q_HEADS)]


def _dot(a, b):
    return jnp.dot(a, b, preferred_element_type=F32)


def _dot_nt(a, b):
    return lax.dot_general(a, b, (((1,), (1,)), ((), ())), preferred_element_type=F32)


def _dot_tn(a, b):
    return lax.dot_general(a, b, (((0,), (0,)), ((), ())), preferred_element_type=F32)


def _rms(x, g):
    r = lax.rsqrt(jnp.mean(x * x, axis=-1, keepdims=True) + EPS)
    xh = x * r
    return xh * g, xh, r


def _rms_bwd(dy, xh, r, g):
    dxh = dy * g
    dg = jnp.sum(dy * xh, axis=0, keepdims=True)
    dx = r * (dxh - xh * jnp.mean(dxh * xh, axis=-1, keepdims=True))
    return dx, dg


_GELU_C = math.sqrt(2.0 / math.pi)


def _gelu(x):
    t = jnp.tanh(_GELU_C * (x + 0.044715 * (x * x * x)))
    return 0.5 * x * (1.0 + t)


def _gelu_grad(x):
    x2 = x * x
    t = jnp.tanh(_GELU_C * (x + 0.044715 * (x2 * x)))
    return 0.5 * (1.0 + t) + 0.5 * x * (1.0 - t * t) * (_GELU_C * (1.0 + 3.0 * 0.044715 * x2))


def _sigmoid(x):
    return 0.5 * jnp.tanh(0.5 * x) + 0.5


def _load_once(pairs, sem):
    @pl.when(pl.program_id(0) == 0)
    def _():
        cps = [pltpu.make_async_copy(s, d, sem.at[i]) for i, (s, d) in enumerate(pairs)]
        for cp in cps:
            cp.start()
        for cp in cps:
            cp.wait()


def _params(n_axes=1, vmem=VMEM_LIMIT):
    return pltpu.CompilerParams(dimension_semantics=("arbitrary",) * n_axes, vmem_limit_bytes=vmem)


def _row_spec(tm, n, rev_nt=None):
    if rev_nt is None:
        return pl.BlockSpec((tm, n), lambda i: (i, 0))
    return pl.BlockSpec((tm, n), lambda i: (rev_nt - 1 - i, 0))


def _const_spec(shape):
    nd = len(shape)
    return pl.BlockSpec(shape, lambda i: (0,) * nd)


def _add_deps(body, in_specs, args, deps):
    nd = len(deps)
    if nd == 0:
        return body, list(in_specs), list(args)

    def wrapped(*refs):
        return body(*refs[nd:])

    return wrapped, [ANY] * nd + list(in_specs), list(deps) + list(args)


def _zero_first(refs):
    @pl.when(pl.program_id(0) == 0)
    def _():
        for r in refs:
            r[...] = jnp.zeros(r.shape, r.dtype)


def _mixer_a_fwd(x, nmix, gv, wsm, bsb, w_in, w_out):
    T = x.shape[0]
    tm = min(512, T)
    nt = T // tm
    nw = 2 * D_MODEL // N_SHARD

    def body(x_ref, nmix_ref, gv_ref, ws_ref, bsb_ref, w_in_hbm, w_out_hbm,
             h1_ref, zp_ref, w_in_v, w_out_v, gated_v, sem):
        _load_once([(w_in_hbm, w_in_v), (w_out_hbm, w_out_v)], sem)
        xv = x_ref[...]
        xn = _rms(xv, nmix_ref[...])[0].astype(BF16)
        for j in range(N_SHARD):
            zp_ref[:, j * nw:(j + 1) * nw] = _dot(xn, w_in_v[j])
        z = _gelu(zp_ref[...])
        u = z[:, :D_MODEL]
        vn = _rms(z[:, D_MODEL:], gv_ref[...])[0].astype(BF16)
        for c in range(tm // CHUNK):
            rows = slice(c * CHUNK, (c + 1) * CHUNK)
            for h in range(A_GROUPS):
                cols = slice(h * CHUNK, (h + 1) * CHUNK)
                s = _dot(ws_ref[h], vn[rows, cols]) + bsb_ref[h]
                gated_v[rows, cols] = (u[rows, cols] * s).astype(BF16)
        h1_ref[...] = xv + _dot(gated_v[...], w_out_v[...])

    return pl.pallas_call(
        body, name="mixer_a_fwd", grid=(nt,),
        in_specs=[_row_spec(tm, D_MODEL), _const_spec((1, D_MODEL)), _const_spec((1, D_MODEL)),
                  _const_spec((A_GROUPS, CHUNK, CHUNK)), _const_spec((A_GROUPS, CHUNK, CHUNK)), ANY, ANY],
        out_specs=[_row_spec(tm, D_MODEL), _row_spec(tm, 2 * D_MODEL)],
        out_shape=[jax.ShapeDtypeStruct((T, D_MODEL), F32), jax.ShapeDtypeStruct((T, 2 * D_MODEL), F32)],
        scratch_shapes=[pltpu.VMEM((N_SHARD, D_MODEL, nw), BF16), pltpu.VMEM((D_MODEL, D_MODEL), BF16),
                        pltpu.VMEM((tm, D_MODEL), BF16), pltpu.SemaphoreType.DMA((2,))],
        compiler_params=_params(),
    )(x, nmix, gv, wsm, bsb, w_in, w_out)


def _mixer_a_bwd(dh, x, zp, nmix, gv, wsm, bsb, tril, w_in, w_out, deps=()):
    T = x.shape[0]
    tm = min(256, T)
    nt = T // tm
    nw = 2 * D_MODEL // N_SHARD

    def body(dh_ref, x_ref, zp_ref, nmix_ref, gv_ref, ws_ref, bsb_ref, tril_ref, w_in_hbm, w_out_hbm,
             dx_ref, dwin_ref, dwout_ref, dws_ref, dbs_ref, dgv_ref, dnmix_ref,
             w_in_v, w_out_v, du_v, dvn_v, dbs_v, gated_ref, sem):
        _load_once([(w_in_hbm, w_in_v), (w_out_hbm, w_out_v)], sem)
        _zero_first([dws_ref, dbs_v, dgv_ref, dnmix_ref, dwin_ref, dwout_ref])
        i = pl.program_id(0)
        dhv = dh_ref[...]
        dhb = dhv.astype(BF16)
        xv = x_ref[...]
        xn, xh, r = _rms(xv, nmix_ref[...])
        xnb = xn.astype(BF16)
        zpv = zp_ref[...]
        z = _gelu(zpv)
        u = z[:, :D_MODEL]
        vn_f, vh, rv = _rms(z[:, D_MODEL:], gv_ref[...])
        vn = vn_f.astype(BF16)
        dgated = _dot_nt(dhb, w_out_v[...])
        for c in range(tm // CHUNK):
            rows = slice(c * CHUNK, (c + 1) * CHUNK)
            for h in range(A_GROUPS):
                cols = slice(h * CHUNK, (h + 1) * CHUNK)
                vn_h = vn[rows, cols]
                s = _dot(ws_ref[h], vn_h) + bsb_ref[h]
                dgt = dgated[rows, cols]
                u_h = u[rows, cols]
                gated_ref[rows, cols] = (u_h * s).astype(BF16)
                du_v[rows, cols] = dgt * s
                ds = dgt * u_h
                dsb = ds.astype(BF16)
                dws_ref[h] += _dot_nt(dsb, vn_h)
                dbs_v[h] += ds
                dvn_v[rows, cols] = _dot_tn(ws_ref[h], dsb)
        dwout_ref[...] += _dot_tn(gated_ref[...], dhb)
        dv, dgv = _rms_bwd(dvn_v[...], vh, rv, gv_ref[...])
        dgv_ref[...] += dgv
        dzu = (du_v[...] * _gelu_grad(zpv[:, :D_MODEL])).astype(BF16)
        dzv = (dv * _gelu_grad(zpv[:, D_MODEL:])).astype(BF16)
        dzs = (dzu[:, :nw], dzu[:, nw:], dzv[:, :nw], dzv[:, nw:])
        dxn = jnp.zeros((tm, D_MODEL), F32)
        for j in range(N_SHARD):
            dxn = dxn + _dot_nt(dzs[j], w_in_v[j])
            dwin_ref[j] += _dot_tn(xnb, dzs[j])
        dxx, dn = _rms_bwd(dxn, xh, r, nmix_ref[...])
        dnmix_ref[...] += dn
        dx_ref[...] = dhv + dxx

        @pl.when(i == nt - 1)
        def _():
            for h in range(A_GROUPS):
                dws_ref[h] = dws_ref[h] * tril_ref[...]
                dbs_ref[h] = jnp.broadcast_to(jnp.sum(dbs_v[h], axis=1, keepdims=True), (CHUNK, CHUNK))

    grp = (A_GROUPS, CHUNK, CHUNK)
    body, in_specs, args = _add_deps(
        body, [_row_spec(tm, D_MODEL), _row_spec(tm, D_MODEL), _row_spec(tm, 2 * D_MODEL),
               _const_spec((1, D_MODEL)), _const_spec((1, D_MODEL)), _const_spec(grp), _const_spec(grp),
               _const_spec((CHUNK, CHUNK)), ANY, ANY],
        [dh, x, zp, nmix, gv, wsm, bsb, tril, w_in, w_out], deps)
    return pl.pallas_call(
        body, name="mixer_a_bwd", grid=(nt,), in_specs=in_specs,
        out_specs=[_row_spec(tm, D_MODEL), _const_spec((N_SHARD, D_MODEL, nw)), _const_spec((D_MODEL, D_MODEL)),
                   _const_spec(grp), _const_spec(grp), _const_spec((1, D_MODEL)), _const_spec((1, D_MODEL))],
        out_shape=[jax.ShapeDtypeStruct((T, D_MODEL), F32), jax.ShapeDtypeStruct((N_SHARD, D_MODEL, nw), F32),
                   jax.ShapeDtypeStruct((D_MODEL, D_MODEL), F32),
                   jax.ShapeDtypeStruct(grp, F32), jax.ShapeDtypeStruct(grp, F32),
                   jax.ShapeDtypeStruct((1, D_MODEL), F32), jax.ShapeDtypeStruct((1, D_MODEL), F32)],
        scratch_shapes=[pltpu.VMEM((N_SHARD, D_MODEL, nw), BF16), pltpu.VMEM((D_MODEL, D_MODEL), BF16),
                        pltpu.VMEM((tm, D_MODEL), F32), pltpu.VMEM((tm, D_MODEL), F32),
                        pltpu.VMEM(grp, F32), pltpu.VMEM((tm, D_MODEL), BF16), pltpu.SemaphoreType.DMA((2,))],
        compiler_params=_params(),
    )(*args)


def _load_ffn_weights(w_up_hbm, w_dn_hbm, layer, w_up_v, w_dn_v, sem):
    _load_once([(w_up_hbm, w_up_v), (w_dn_hbm, w_dn_v)], sem)


def _ffn_fwd(h, nffn, cw, cb, w_up, w_dn, layer):
    T = h.shape[0]
    tm = min(256, T)
    nt = T // tm

    def body(h_ref, n_ref, cw_ref, cb_ref, w_up_hbm, w_dn_hbm, out_ref, hh_ref, gate_ref,
             w_up_v, w_dn_v, carry_v, sem):
        _load_ffn_weights(w_up_hbm, w_dn_hbm, layer, w_up_v, w_dn_v, sem)
        _zero_first([carry_v])
        xv = h_ref[...]
        xf = _rms(xv, n_ref[...])[0].astype(BF16)
        acc = xv
        for j in range(2):
            cs = []
            for blk in (j, j + 2):
                cols = slice(blk * FF_BLK, (blk + 1) * FF_BLK)
                hh = _dot(xf, w_up_v[blk])
                hh_ref[:, cols] = hh.astype(BF16)
                ext = jnp.concatenate([carry_v[blk], hh], axis=0)
                carry_v[blk] = hh[tm - 8:, :]
                s1 = pltpu.roll(ext, 1, 0)[8:]
                s2 = pltpu.roll(ext, 2, 0)[8:]
                cs.append(cb_ref[:, cols] + cw_ref[0:1, cols] * s2 + cw_ref[1:2, cols] * s1
                          + cw_ref[2:3, cols] * hh)
            cg, cu = cs
            sg = _sigmoid(cg)
            sil = cg * sg
            act = (sil * cu).astype(BF16)
            for kind, val in enumerate((cu, sil, sg * (1.0 + cg * (1.0 - sg)))):
                gate_ref[:, kind * D_FF + j * FF_BLK:kind * D_FF + (j + 1) * FF_BLK] = val.astype(BF16)
            gate_ref[:, 3 * D_FF + j * FF_BLK:3 * D_FF + (j + 1) * FF_BLK] = act
            acc = acc + _dot(act, w_dn_v[j * FF_BLK:(j + 1) * FF_BLK, :])
        out_ref[...] = acc

    return pl.pallas_call(
        body, name=f"ffn_fwd{layer}", grid=(nt,),
        in_specs=[_row_spec(tm, D_MODEL), _const_spec((1, D_MODEL)), _const_spec((3, N_FF)),
                  _const_spec((1, N_FF)), ANY, ANY],
        out_specs=[_row_spec(tm, D_MODEL), _row_spec(tm, N_FF), _row_spec(tm, 4 * D_FF)],
        out_shape=[jax.ShapeDtypeStruct((T, D_MODEL), F32), jax.ShapeDtypeStruct((T, N_FF), BF16),
                   jax.ShapeDtypeStruct((T, 4 * D_FF), BF16)],
        scratch_shapes=[pltpu.VMEM((N_SHARD, D_MODEL, FF_BLK), BF16), pltpu.VMEM((D_FF, D_MODEL), BF16),
                        pltpu.VMEM((N_SHARD, 8, FF_BLK), F32), pltpu.SemaphoreType.DMA((2 * N_SHARD,))],
        compiler_params=_params(),
    )(h, nffn, cw, cb, w_up, w_dn)


def _wgrad(a, b, bn, col_sharded, name, deps=(), a_cols=None):
    T = a.shape[0]
    K, a_blk = (a.shape[1], 0) if a_cols is None else a_cols
    N = b.shape[1]
    tt = min(2048, T)
    nn, ntt = N // bn, T // tt
    kr = K // N_SHARD

    def body(a_ref, b_ref, o_ref):
        @pl.when(pl.program_id(1) == 0)
        def _():
            o_ref[...] = jnp.zeros(o_ref.shape, F32)
        d = _dot_tn(a_ref[...].astype(BF16), b_ref[...].astype(BF16))
        if col_sharded:
            o_ref[...] += d
        else:
            for j in range(N_SHARD):
                o_ref[j] += d[j * kr:(j + 1) * kr]

    if col_sharded:
        assert nn == N_SHARD
        out_spec = pl.BlockSpec((None, K, bn), lambda n, t: (n, 0, 0))
        out_shape = jax.ShapeDtypeStruct((N_SHARD, K, bn), F32)
    else:
        out_spec = pl.BlockSpec((N_SHARD, kr, bn), lambda n, t: (0, 0, n))
        out_shape = jax.ShapeDtypeStruct((N_SHARD, kr, N), F32)
    body, in_specs, args = _add_deps(
        body, [pl.BlockSpec((tt, K), lambda n, t: (t, a_blk)), pl.BlockSpec((tt, bn), lambda n, t: (t, n))],
        [a, b], deps)
    return pl.pallas_call(
        body, name=name, grid=(nn, ntt), in_specs=in_specs, out_specs=out_spec, out_shape=out_shape,
        compiler_params=pltpu.CompilerParams(dimension_semantics=("arbitrary",) * 2, vmem_limit_bytes=VMEM_LIMIT),
    )(*args)


def _ffn_bwd(dh, h, hh, gate, nffn, cw, w_up, w_dn, layer, deps=(), between=None):
    T = h.shape[0]
    tm = min(256, T)
    nt = T // tm

    def body(dh_ref, h_ref, hh_ref, u_ref, sil_ref, dsil_ref, n_ref, cw_ref, w_up_hbm, w_dn_hbm,
             dhin_ref, dhh_ref, xf_ref, dcw_ref, dcb_ref, dn_ref,
             w_up_v, w_dn_v, carry_v, sem):
        _load_ffn_weights(w_up_hbm, w_dn_hbm, layer, w_up_v, w_dn_v, sem)
        _zero_first([carry_v, dcw_ref, dcb_ref, dn_ref])
        dout = dh_ref[...]
        doutb = dout.astype(BF16)
        xf_f, xh, r = _rms(h_ref[...], n_ref[...])
        xf_ref[...] = xf_f.astype(BF16)
        dxf = jnp.zeros((tm, D_MODEL), F32)
        for j in range(2):
            blks = (j, j + 2)
            pair = slice(j * FF_BLK, (j + 1) * FF_BLK)
            dact = _dot_nt(doutb, w_dn_v[pair, :])
            dcs = (dact * u_ref[:, pair].astype(F32) * dsil_ref[:, pair].astype(F32),
                   dact * sil_ref[:, pair].astype(F32))
            for blk, dc in zip(blks, dcs):
                cols = slice(blk * FF_BLK, (blk + 1) * FF_BLK)
                hhv = hh_ref[:, cols].astype(F32)
                ext = jnp.concatenate([dc, carry_v[blk]], axis=0)
                carry_v[blk] = dc[:8, :]
                n = tm + 8
                a1 = pltpu.roll(ext, n - 1, 0)[:tm]
                a2 = pltpu.roll(ext, n - 2, 0)[:tm]
                dcb_ref[:, cols] += jnp.sum(dc, axis=0, keepdims=True)
                dcw_ref[0:1, cols] += jnp.sum(a2 * hhv, axis=0, keepdims=True)
                dcw_ref[1:2, cols] += jnp.sum(a1 * hhv, axis=0, keepdims=True)
                dcw_ref[2:3, cols] += jnp.sum(dc * hhv, axis=0, keepdims=True)
                dhh = (cw_ref[2:3, cols] * dc + cw_ref[1:2, cols] * a1 + cw_ref[0:1, cols] * a2).astype(BF16)
                dhh_ref[:, cols] = dhh
                dxf = dxf + _dot_nt(dhh, w_up_v[blk])
        dxx, dn = _rms_bwd(dxf, xh, r, n_ref[...])
        dn_ref[...] += dn
        dhin_ref[...] = dout + dxx

    rev = functools.partial(_row_spec, rev_nt=nt)

    def kind(k):
        return pl.BlockSpec((tm, D_FF), lambda i: (nt - 1 - i, k))

    body, in_specs, args = _add_deps(
        body, [rev(tm, D_MODEL), rev(tm, D_MODEL), rev(tm, N_FF), kind(0), kind(1), kind(2),
               _const_spec((1, D_MODEL)), _const_spec((3, N_FF)), ANY, ANY],
        [dh, h, hh, gate, gate, gate, nffn, cw, w_up, w_dn], deps)
    dhin, dhh, xf, dcw, dcb, dn = pl.pallas_call(
        body, name=f"ffn_bwd{layer}", grid=(nt,), in_specs=in_specs,
        out_specs=[rev(tm, D_MODEL), rev(tm, N_FF), rev(tm, D_MODEL),
                   _const_spec((3, N_FF)), _const_spec((1, N_FF)), _const_spec((1, D_MODEL))],
        out_shape=[jax.ShapeDtypeStruct((T, D_MODEL), F32),
                   jax.ShapeDtypeStruct((T, N_FF), BF16), jax.ShapeDtypeStruct((T, D_MODEL), BF16),
                   jax.ShapeDtypeStruct((3, N_FF), F32), jax.ShapeDtypeStruct((1, N_FF), F32),
                   jax.ShapeDtypeStruct((1, D_MODEL), F32)],
        scratch_shapes=[pltpu.VMEM((N_SHARD, D_MODEL, FF_BLK), BF16), pltpu.VMEM((D_FF, D_MODEL), BF16),
                        pltpu.VMEM((N_SHARD, 8, FF_BLK), F32), pltpu.SemaphoreType.DMA((2 * N_SHARD,))],
        compiler_params=_params(),
    )(*args)
    deps2 = between(dhin) if between is not None else ()
    dwdn = _wgrad(gate, dh, D_MODEL // 2, False, f"wgrad_ffn_down{layer}", deps=deps2, a_cols=(D_FF, 3))
    dwup = _wgrad(xf, dhh, FF_BLK, True, f"wgrad_ffn_up{layer}", deps=deps2)
    return dhin, dwup, dwdn, dcw, dcb, dn


def _load_ple_weights(w_pin_hbm, w_gate_hbm, layer, w_pin_v, w_gate_v, sem, extra=()):
    _load_once([(w_pin_hbm, w_pin_v), (w_gate_hbm, w_gate_v)] + list(extra), sem)


def _p_spec(tm, layer):
    return pl.BlockSpec((None, tm, PLE_DIM), lambda i: (layer, i, 0))


def _ple_terms(xv, p_ref, n_ref, bg_ref, w_pin_v, w_gate_v, pe_v, a_ref, saved):
    pw = D_MODEL // N_SHARD
    xg, xh, r = _rms(xv, n_ref[...])
    xgb = xg.astype(BF16)
    if saved:
        gate = _sigmoid(a_ref[...].astype(F32))
    else:
        a = _dot(xgb, w_gate_v[...]) + bg_ref[...]
        a_ref[...] = a.astype(BF16)
        gate = _sigmoid(a)
    pb = p_ref[...].astype(BF16)
    for j in range(N_SHARD):
        pe_v[:, j * pw:(j + 1) * pw] = _dot(pb, w_pin_v[j])
    pe = pe_v[...]
    return pe * gate, pe, gate, xgb, xh, r


def _ple_fwd_kv(h, p, nple, bg, nkv, w_pin, w_gate, w_kv):
    T = h.shape[0]
    tm = min(512, T)
    nt = T // tm
    pw = D_MODEL // N_SHARD

    def body(h_ref, p_ref, n_ref, bg_ref, nkv_ref, w_pin_hbm, w_gate_hbm, w_kv_hbm,
             out_ref, kv_ref, a_ref, w_pin_v, w_gate_v, w_kv_v, pe_v, sem):
        _load_ple_weights(w_pin_hbm, w_gate_hbm, 0, w_pin_v, w_gate_v, sem, [(w_kv_hbm, w_kv_v)])
        xv = h_ref[...]
        hn = xv + _ple_terms(xv, p_ref, n_ref, bg_ref, w_pin_v, w_gate_v, pe_v, a_ref, False)[0]
        out_ref[...] = hn
        kvn = _rms(hn, nkv_ref[...])[0].astype(BF16)
        kv_ref[...] = _dot(kvn, w_kv_v[...]).astype(BF16)

    vec = _const_spec((1, D_MODEL))
    return pl.pallas_call(
        body, name="ple_fwd0", grid=(nt,),
        in_specs=[_row_spec(tm, D_MODEL), _p_spec(tm, 0), vec, vec, vec, ANY, ANY, ANY],
        out_specs=[_row_spec(tm, D_MODEL), _row_spec(tm, 2 * KV_DIM), _row_spec(tm, D_MODEL)],
        out_shape=[jax.ShapeDtypeStruct((T, D_MODEL), F32), jax.ShapeDtypeStruct((T, 2 * KV_DIM), BF16),
                   jax.ShapeDtypeStruct((T, D_MODEL), BF16)],
        scratch_shapes=[pltpu.VMEM((N_SHARD, PLE_DIM, pw), BF16), pltpu.VMEM((D_MODEL, D_MODEL), BF16),
                        pltpu.VMEM((D_MODEL, 2 * KV_DIM), BF16), pltpu.VMEM((tm, D_MODEL), F32),
                        pltpu.SemaphoreType.DMA((2 * N_SHARD + 1,))],
        compiler_params=_params(),
    )(h, p, nple, bg, nkv, w_pin, w_gate, w_kv)


def _ple_fwd_final(h, p, tgt, nple, bg, nfin, w_pin, w_gate):
    T = h.shape[0]
    tm = min(512, T)
    nt = T // tm
    pw = D_MODEL // N_SHARD

    def body(h_ref, p_ref, t_ref, n_ref, bg_ref, nf_ref, w_pin_hbm, w_gate_hbm,
             dh_ref, loss_ref, dnf_ref, a_ref, w_pin_v, w_gate_v, pe_v, sem):
        _load_ple_weights(w_pin_hbm, w_gate_hbm, 1, w_pin_v, w_gate_v, sem)
        _zero_first([loss_ref, dnf_ref])
        xv = h_ref[...]
        hn = xv + _ple_terms(xv, p_ref, n_ref, bg_ref, w_pin_v, w_gate_v, pe_v, a_ref, False)[0]
        y, yh, r = _rms(hn, nf_ref[...])
        diff = y - t_ref[...]
        loss_ref[...] += 0.5 * jnp.sum(jnp.mean(diff * diff, axis=-1, keepdims=True))
        dy = diff * (1.0 / D_MODEL)
        dhn, dnf = _rms_bwd(dy, yh, r, nf_ref[...])
        dnf_ref[...] += dnf
        dh_ref[...] = dhn

    vec = _const_spec((1, D_MODEL))
    return pl.pallas_call(
        body, name="ple_fwd1", grid=(nt,),
        in_specs=[_row_spec(tm, D_MODEL), _p_spec(tm, 1), _row_spec(tm, D_MODEL), vec, vec, vec, ANY, ANY],
        out_specs=[_row_spec(tm, D_MODEL), _const_spec((8, 128)), vec, _row_spec(tm, D_MODEL)],
        out_shape=[jax.ShapeDtypeStruct((T, D_MODEL), F32), jax.ShapeDtypeStruct((8, 128), F32),
                   jax.ShapeDtypeStruct((1, D_MODEL), F32), jax.ShapeDtypeStruct((T, D_MODEL), BF16)],
        scratch_shapes=[pltpu.VMEM((N_SHARD, PLE_DIM, pw), BF16), pltpu.VMEM((D_MODEL, D_MODEL), BF16),
                        pltpu.VMEM((tm, D_MODEL), F32), pltpu.SemaphoreType.DMA((2 * N_SHARD,))],
        compiler_params=_params(),
    )(h, p, tgt, nple, bg, nfin, w_pin, w_gate)


def _ple_bwd(dh, hb, p, a, nple, w_pin, w_gate, layer, kv_args=None):
    T = hb.shape[0]
    tm = min(512, T)
    nt = T // tm
    with_kv = kv_args is not None
    pw = D_MODEL // N_SHARD

    def body(*refs):
        if with_kv:
            (dh_ref, hb_ref, p_ref, a_ref, n_ref, w_pin_hbm, w_gate_hbm, hc_ref, dkv_ref, nkv_ref, w_kv_hbm,
             dhb_ref, dwpin_ref, dwgate_ref, dbg_ref, dn_ref, dwkv_ref, dnkv_ref,
             w_pin_v, w_gate_v, pe_v, w_kv_v, sem) = refs
        else:
            (dh_ref, hb_ref, p_ref, a_ref, n_ref, w_pin_hbm, w_gate_hbm,
             dhb_ref, dwpin_ref, dwgate_ref, dbg_ref, dn_ref, w_pin_v, w_gate_v, pe_v, sem) = refs
        pairs = [(w_pin_hbm, w_pin_v), (w_gate_hbm, w_gate_v)]
        if with_kv:
            pairs.append((w_kv_hbm, w_kv_v))
        _load_once(pairs, sem)
        _zero_first([dwpin_ref, dwgate_ref, dbg_ref, dn_ref] + ([dwkv_ref, dnkv_ref] if with_kv else []))
        do = dh_ref[...]
        if with_kv:
            dkvb = dkv_ref[...].astype(BF16)
            dkvn = _dot_nt(dkvb, w_kv_v[...])
            kvn, kh, kr = _rms(hc_ref[...], nkv_ref[...])
            dwkv_ref[...] += _dot_tn(kvn.astype(BF16), dkvb)
            dk, dnkv = _rms_bwd(dkvn, kh, kr, nkv_ref[...])
            dnkv_ref[...] += dnkv
            do = do + dk
        _, pe, gate, xgb, xh, r = _ple_terms(hb_ref[...], p_ref, n_ref, None, w_pin_v, w_gate_v, pe_v, a_ref, True)
        dpe = (do * gate).astype(BF16)
        pb = p_ref[...].astype(BF16)
        for j in range(N_SHARD):
            dwpin_ref[j] += _dot_tn(pb, dpe[:, j * pw:(j + 1) * pw])
        da = do * pe * (gate * (1.0 - gate))
        dab = da.astype(BF16)
        dbg_ref[...] += jnp.sum(da, axis=0, keepdims=True)
        dxg = _dot_nt(dab, w_gate_v[...])
        dwgate_ref[...] += _dot_tn(xgb, dab)
        dxx, dn = _rms_bwd(dxg, xh, r, n_ref[...])
        dn_ref[...] += dn
        dhb_ref[...] = do + dxx

    vec = _const_spec((1, D_MODEL))
    row = _row_spec(tm, D_MODEL)
    in_specs = [row, row, _p_spec(tm, layer), row, vec, ANY, ANY]
    args = [dh, hb, p, a, nple, w_pin, w_gate]
    out_specs = [row, _const_spec((N_SHARD, PLE_DIM, pw)), _const_spec((D_MODEL, D_MODEL)), vec, vec]
    out_shape = [jax.ShapeDtypeStruct((T, D_MODEL), F32), jax.ShapeDtypeStruct((N_SHARD, PLE_DIM, pw), F32),
                 jax.ShapeDtypeStruct((D_MODEL, D_MODEL), F32),
                 jax.ShapeDtypeStruct((1, D_MODEL), F32), jax.ShapeDtypeStruct((1, D_MODEL), F32)]
    scratch = [pltpu.VMEM((N_SHARD, PLE_DIM, pw), BF16), pltpu.VMEM((D_MODEL, D_MODEL), BF16),
               pltpu.VMEM((tm, D_MODEL), F32)]
    if with_kv:
        hc, dkv, nkv, w_kv = kv_args
        in_specs += [row, _row_spec(tm, 2 * KV_DIM), vec, ANY]
        args += [hc, dkv, nkv, w_kv]
        out_specs += [_const_spec((D_MODEL, 2 * KV_DIM)), vec]
        out_shape += [jax.ShapeDtypeStruct((D_MODEL, 2 * KV_DIM), F32), jax.ShapeDtypeStruct((1, D_MODEL), F32)]
        scratch.append(pltpu.VMEM((D_MODEL, 2 * KV_DIM), BF16))
    scratch.append(pltpu.SemaphoreType.DMA((3,)))
    return pl.pallas_call(
        body, name=f"ple_bwd{layer}", grid=(nt,), in_specs=in_specs, out_specs=out_specs,
        out_shape=out_shape, scratch_shapes=scratch, compiler_params=_params(),
    )(*args)


GROUP_ROWS = GQA_GROUP * BLOCK


def _stack_heads(x, kh):
    return jnp.concatenate([x[:, (kh * GQA_GROUP + g) * HEAD_DIM:(kh * GQA_GROUP + g + 1) * HEAD_DIM]
                            for g in range(GQA_GROUP)], axis=0)


def _attn_fwd(h, nmix, kv, sinks, w_q, w_o):
    T = h.shape[0]
    tm = min(512, T)
    nt = T // tm
    nb = tm // BLOCK

    def body(h_ref, n_ref, kv_ref, kvp_ref, sink_ref, w_q_hbm, w_o_hbm,
             out_ref, q_ref, ao_ref, p_ref, psink_ref, w_q_v, w_o_v, kvs_v, sem):
        _load_once([(w_q_hbm, w_q_v), (w_o_hbm, w_o_v)], sem)
        ti = pl.program_id(0)
        xv = h_ref[...]
        xn = _rms(xv, n_ref[...])[0].astype(BF16)
        q_ref[...] = (_dot(xn, w_q_v[...]) * (HEAD_DIM ** -0.5)).astype(BF16)
        kvs_v[0:BLOCK, :] = kvp_ref[...]
        kvs_v[BLOCK:, :] = kv_ref[...]
        lane = lax.broadcasted_iota(jnp.int32, (BLOCK, 128), 1)
        ii = lax.broadcasted_iota(jnp.int32, (BLOCK, 2 * BLOCK), 0)
        jj = lax.broadcasted_iota(jnp.int32, (BLOCK, 2 * BLOCK), 1)
        dist = ii + BLOCK - jj
        inband = (dist >= 0) & (dist < BLOCK)
        distf = dist.astype(F32)

        def blk_body(b, carry):
            r0 = pl.multiple_of(b * BLOCK, BLOCK)
            valid = inband & ((jj >= BLOCK) | jnp.logical_not(jnp.logical_and(ti == 0, b == 0)))
            qb = q_ref[pl.ds(r0, BLOCK), :]
            band = kvs_v[pl.ds(r0, 2 * BLOCK), :]
            psink_mat = jnp.zeros((BLOCK, 128), F32)
            outs = []
            for hq in range(N_Q_HEADS):
                kh, g = divmod(hq, GQA_GROUP)
                k_h = band[:, kh * HEAD_DIM:(kh + 1) * HEAD_DIM]
                v_h = band[:, KV_DIM + kh * HEAD_DIM:KV_DIM + (kh + 1) * HEAD_DIM]
                s = _dot_nt(qb[:, hq * HEAD_DIM:(hq + 1) * HEAD_DIM], k_h) - _SLOPES[hq] * distf
                s = jnp.where(valid, s, NEG)
                sink = sink_ref[hq]
                m = jnp.maximum(jnp.max(s, axis=1, keepdims=True), sink)
                e = jnp.exp(s - m)
                esink = jnp.exp(sink - m)
                inv = 1.0 / (jnp.sum(e, axis=1, keepdims=True) + esink)
                pb = (e * inv).astype(BF16)
                p_ref[b, kh, g * BLOCK:(g + 1) * BLOCK, :] = pb
                outs.append(_dot(pb, v_h))
                psink_mat = jnp.where(lane == hq, esink * inv, psink_mat)
            ao_ref[pl.ds(r0, BLOCK), :] = jnp.concatenate(outs, axis=1).astype(BF16)
            psink_ref[pl.ds(r0, BLOCK), :] = psink_mat
            return carry

        lax.fori_loop(0, nb, blk_body, 0)
        out_ref[...] = xv + _dot(ao_ref[...], w_o_v[...])

    row = _row_spec(tm, D_MODEL)
    prev_spec = pl.BlockSpec((BLOCK, 2 * KV_DIM), lambda i: (jnp.maximum(i * nb - 1, 0), 0))
    return pl.pallas_call(
        body, name="attn_fwd", grid=(nt,),
        in_specs=[row, _const_spec((1, D_MODEL)), _row_spec(tm, 2 * KV_DIM), prev_spec, SMEM, ANY, ANY],
        out_specs=[row, row, row, pl.BlockSpec((nb, N_KV_HEADS, GROUP_ROWS, 2 * BLOCK), lambda i: (i, 0, 0, 0)),
                   _row_spec(tm, 128)],
        out_shape=[jax.ShapeDtypeStruct((T, D_MODEL), F32), jax.ShapeDtypeStruct((T, D_MODEL), BF16),
                   jax.ShapeDtypeStruct((T, D_MODEL), BF16),
                   jax.ShapeDtypeStruct((T // BLOCK, N_KV_HEADS, GROUP_ROWS, 2 * BLOCK), BF16),
                   jax.ShapeDtypeStruct((T, 128), F32)],
        scratch_shapes=[pltpu.VMEM((D_MODEL, D_MODEL), BF16), pltpu.VMEM((D_MODEL, D_MODEL), BF16),
                        pltpu.VMEM((tm + BLOCK, 2 * KV_DIM), BF16), pltpu.SemaphoreType.DMA((2,))],
        compiler_params=_params(),
    )(h, nmix, kv, kv, sinks, w_q, w_o)


def _attn_bwd(dh, h, q, kv, ao, p, psink, nmix, w_q, w_o):
    T = h.shape[0]
    tm = min(512, T)
    nt = T // tm
    nb = tm // BLOCK

    def body(dh_ref, h_ref, q_ref, kv_ref, kvp_ref, ao_ref, p_ref, psink_ref, n_ref, w_q_hbm, w_o_hbm,
             dhin_ref, dwq_ref, dwo_ref, dkv_ref, dsink_ref, dn_ref,
             w_q_v, w_o_v, kvs_v, dao_v, dq_v, dkv_v, carry_v, sem):
        _load_once([(w_q_hbm, w_q_v), (w_o_hbm, w_o_v)], sem)
        _zero_first([carry_v, dsink_ref, dn_ref, dwq_ref, dwo_ref])
        dout = dh_ref[...]
        doutb = dout.astype(BF16)
        dao_v[...] = _dot_nt(doutb, w_o_v[...])
        dwo_ref[...] += _dot_tn(ao_ref[...], doutb)
        kvs_v[0:BLOCK, :] = kvp_ref[...]
        kvs_v[BLOCK:, :] = kv_ref[...]
        dkv_v[0:tm, :] = jnp.zeros((tm, 2 * KV_DIM), F32)
        dkv_v[tm:, :] = carry_v[...]
        seg = (lax.broadcasted_iota(jnp.int32, (D_MODEL, 128), 0) // HEAD_DIM
               == lax.broadcasted_iota(jnp.int32, (D_MODEL, 128), 1)).astype(BF16)

        def blk_body(b, dsk):
            r0 = pl.multiple_of(b * BLOCK, BLOCK)
            qb = q_ref[pl.ds(r0, BLOCK), :]
            band = kvs_v[pl.ds(r0, 2 * BLOCK), :]
            aob = ao_ref[pl.ds(r0, BLOCK), :].astype(F32)
            daob = dao_v[pl.ds(r0, BLOCK), :]
            prod = daob * aob
            head = prod.astype(BF16)
            tail = (prod - head.astype(F32)).astype(BF16)
            dsk = dsk + psink_ref[pl.ds(r0, BLOCK), :] * (_dot(head, seg) + _dot(tail, seg))
            dqs = []
            dks = []
            dvs = []
            for kh in range(N_KV_HEADS):
                k_h = band[:, kh * HEAD_DIM:(kh + 1) * HEAD_DIM]
                v_h = band[:, KV_DIM + kh * HEAD_DIM:KV_DIM + (kh + 1) * HEAD_DIM]
                q_g = _stack_heads(qb, kh)
                dao_g = _stack_heads(daob, kh)
                prb = p_ref[b, kh]
                pr = prb.astype(F32)
                dd = jnp.sum(dao_g * _stack_heads(aob, kh), axis=1, keepdims=True)
                dao_gb = dao_g.astype(BF16)
                dp = _dot_nt(dao_gb, v_h)
                dsb = (pr * (dp - dd)).astype(BF16)
                dq_g = _dot(dsb, k_h) * (HEAD_DIM ** -0.5)
                dks.append(_dot_tn(dsb, q_g))
                dvs.append(_dot_tn(prb, dao_gb))
                for g in range(GQA_GROUP):
                    dqs.append(dq_g[g * BLOCK:(g + 1) * BLOCK])
            dq_v[pl.ds(r0, BLOCK), :] = jnp.concatenate(dqs, axis=1)
            dkv_v[pl.ds(r0, 2 * BLOCK), :] += jnp.concatenate(dks + dvs, axis=1)
            return dsk

        dsk = lax.fori_loop(0, nb, blk_body, jnp.zeros((BLOCK, 128), F32))
        dsink_ref[...] -= jnp.sum(dsk, axis=0, keepdims=True)
        dqb = dq_v[...].astype(BF16)
        dxn = _dot_nt(dqb, w_q_v[...])
        xn, xh, r = _rms(h_ref[...], n_ref[...])
        dwq_ref[...] += _dot_tn(xn.astype(BF16), dqb)
        dxx, dn = _rms_bwd(dxn, xh, r, n_ref[...])
        dn_ref[...] += dn
        dhin_ref[...] = dout + dxx
        dkv_ref[...] = dkv_v[BLOCK:, :]
        carry_v[...] = dkv_v[0:BLOCK, :]

    rev = functools.partial(_row_spec, rev_nt=nt)
    row = rev(tm, D_MODEL)
    prev_spec = pl.BlockSpec((BLOCK, 2 * KV_DIM), lambda i: (jnp.maximum((nt - 1 - i) * nb - 1, 0), 0))
    return pl.pallas_call(
        body, name="attn_bwd", grid=(nt,),
        in_specs=[row, row, row, rev(tm, 2 * KV_DIM), prev_spec, row,
                  pl.BlockSpec((nb, N_KV_HEADS, GROUP_ROWS, 2 * BLOCK), lambda i: (nt - 1 - i, 0, 0, 0)),
                  rev(tm, 128), _const_spec((1, D_MODEL)), ANY, ANY],
        out_specs=[row, _const_spec((D_MODEL, D_MODEL)), _const_spec((D_MODEL, D_MODEL)), rev(tm, 2 * KV_DIM),
                   _const_spec((8, 128)), _const_spec((1, D_MODEL))],
        out_shape=[jax.ShapeDtypeStruct((T, D_MODEL), F32), jax.ShapeDtypeStruct((D_MODEL, D_MODEL), F32),
                   jax.ShapeDtypeStruct((D_MODEL, D_MODEL), F32), jax.ShapeDtypeStruct((T, 2 * KV_DIM), F32),
                   jax.ShapeDtypeStruct((8, 128), F32), jax.ShapeDtypeStruct((1, D_MODEL), F32)],
        scratch_shapes=[pltpu.VMEM((D_MODEL, D_MODEL), BF16), pltpu.VMEM((D_MODEL, D_MODEL), BF16),
                        pltpu.VMEM((tm + BLOCK, 2 * KV_DIM), BF16), pltpu.VMEM((tm, D_MODEL), F32),
                        pltpu.VMEM((tm, D_MODEL), F32), pltpu.VMEM((tm + BLOCK, 2 * KV_DIM), F32),
                        pltpu.VMEM((BLOCK, 2 * KV_DIM), F32), pltpu.SemaphoreType.DMA((2,))],
        compiler_params=_params(),
    )(dh, h, q, kv, kv, ao, p, psink, nmix, w_q, w_o)


def _mesh_pos():
    return lax.axis_index("x"), lax.axis_index("y"), lax.axis_index("c")


def _other_chips(x, y):
    return [(1 - x, y), (x, 1 - y), (1 - x, 1 - y)]


HBM_SPEC = pl.BlockSpec(memory_space=pltpu.HBM)
SEM_SPEC = pl.BlockSpec(memory_space=pltpu.SEMAPHORE)


def _split_call(name, bufs, waits=(), starts=(), after=()):
    n, nw, ns, na = len(bufs), len(waits), len(starts), len(after)

    def body(*refs):
        brefs = refs[:n]
        wsems = [(refs[n + 2 * k], refs[n + 2 * k + 1]) for k in range(nw)]
        o = n + 2 * nw + na
        ssems = [(refs[o + 2 * k], refs[o + 2 * k + 1]) for k in range(ns)]
        for (ss, rs), (_, _, fn) in zip(wsems, waits):
            for sending, arriving in fn(brefs, ss, rs):
                sending.wait_send()
                arriving.wait_recv()
        for (ss, rs), (_, fn) in zip(ssems, starts):
            for sending, _ in fn(brefs, ss, rs):
                sending.start()
        if ns:
            token = refs[o + 2 * ns + n]
            token[...] = jnp.zeros(token.shape, token.dtype)

    out_shape, out_specs = [], []
    for cnt, _ in starts:
        out_shape += [pltpu.SemaphoreType.DMA((cnt,)), pltpu.SemaphoreType.DMA((cnt,))]
        out_specs += [SEM_SPEC, SEM_SPEC]
    out_shape += [pltpu.HBM(b.shape, b.dtype) for b in bufs]
    out_specs += [HBM_SPEC] * n
    if ns:
        out_shape.append(jax.ShapeDtypeStruct((8, 128), F32))
        out_specs.append(pl.BlockSpec(memory_space=pltpu.VMEM))
    args = [pltpu.with_memory_space_constraint(b, pltpu.HBM) for b in bufs]
    for ss, rs, _ in waits:
        args += [ss, rs]
    args += list(after)
    res = pl.pallas_call(
        body, name=name, out_shape=tuple(out_shape),
        in_specs=[HBM_SPEC] * n + [SEM_SPEC] * (2 * nw) + [ANY] * na, out_specs=tuple(out_specs),
        input_output_aliases={i: 2 * ns + i for i in range(n)},
        compiler_params=pltpu.CompilerParams(has_side_effects=pltpu.SideEffectType.DATAFLOW_SIDE_EFFECTING),
    )(*args)
    sems = [(res[2 * k], res[2 * k + 1]) for k in range(ns)]
    return list(res[2 * ns:2 * ns + n]), sems, (res[2 * ns + n] if ns else None)


def _cast_place(items, name, deps=()):
    n = len(items)
    mats = [a.shape[-2:] for a, _, _ in items]

    def body(*refs):
        ins, outs, scr, sem = refs[:n], refs[n:2 * n], refs[2 * n:3 * n], refs[3 * n]
        x, y, _ = _mesh_pos()
        cps = []
        for t in range(n):
            scr[t][...] = ins[t][...].astype(scr[t].dtype)
            cp = pltpu.make_async_copy(scr[t], outs[t].at[2 * x + y], sem.at[t])
            cp.start()
            cps.append(cp)
        for cp in cps:
            cp.wait()

    def spec(idx, shape):
        return pl.BlockSpec((None,) * len(idx) + tuple(shape), lambda i: tuple(idx) + (0, 0))

    body, in_specs, args = _add_deps(body, [spec(idx, mat) for (_, idx, _), mat in zip(items, mats)],
                                     [a for a, _, _ in items], deps)
    return pl.pallas_call(
        body, name=name, grid=(1,), in_specs=in_specs, out_specs=[ANY] * n,
        out_shape=[jax.ShapeDtypeStruct((N_SHARD,) + tuple(mat), dt) for (_, _, dt), mat in zip(items, mats)],
        scratch_shapes=[pltpu.VMEM(tuple(mat), dt) for (_, _, dt), mat in zip(items, mats)]
        + [pltpu.SemaphoreType.DMA((n,))],
        compiler_params=_params(),
    )(*args)


def _gather_ici(idx):
    def fn(bufs, ss, rs):
        x, y, c = _mesh_pos()
        pairs = []
        for k, t in enumerate(idx):
            half = bufs[t].shape[1] // 2
            mine = bufs[t].at[2 * x + y, pl.ds(c * half, half), :]
            for j, (cx, cy) in enumerate(_other_chips(x, y)):
                theirs = bufs[t].at[2 * cx + cy, pl.ds(c * half, half), :]
                sem = dict(send_sem=ss.at[3 * k + j], recv_sem=rs.at[3 * k + j],
                           device_id=(cx, cy, c), device_id_type=MESH)
                pairs.append((pltpu.make_async_remote_copy(src_ref=mine, dst_ref=mine, **sem),
                              pltpu.make_async_remote_copy(src_ref=mine, dst_ref=theirs, **sem)))
        return pairs
    return fn


def _gather_d2d(idx):
    def fn(bufs, ss, rs):
        x, y, c = _mesh_pos()
        pairs = []
        for k, t in enumerate(idx):
            half = bufs[t].shape[1] // 2
            for j, (cx, cy) in enumerate(_other_chips(x, y)):
                got = bufs[t].at[2 * cx + cy, pl.ds(c * half, half), :]
                theirs = bufs[t].at[2 * cx + cy, pl.ds((1 - c) * half, half), :]
                sem = dict(send_sem=ss.at[3 * k + j], recv_sem=rs.at[3 * k + j],
                           device_id=(x, y, 1 - c), device_id_type=MESH)
                pairs.append((pltpu.make_async_remote_copy(src_ref=got, dst_ref=got, **sem),
                              pltpu.make_async_remote_copy(src_ref=got, dst_ref=theirs, **sem)))
        return pairs
    return fn


def _alloc(shapes, name):
    def body(*refs):
        pass

    return pl.pallas_call(body, name=name, out_specs=[ANY] * len(shapes),
                          out_shape=[jax.ShapeDtypeStruct(s, d) for s, d in shapes])()


def _send_to_sibling(n):
    def fn(bufs, ss, rs):
        x, y, c = _mesh_pos()
        pairs = []
        for t in range(n):
            src = bufs[t]
            if len(src.shape) == 3:
                half = src.shape[1] // 2
                src = src.at[:, pl.ds((1 - c) * half, half), :]
            cp = pltpu.make_async_remote_copy(src_ref=src, dst_ref=bufs[n + t], send_sem=ss.at[t],
                                              recv_sem=rs.at[t], device_id=(x, y, 1 - c), device_id_type=MESH)
            pairs.append((cp, cp))
        return pairs
    return fn


def _send_to_chips(n):
    def fn(bufs, ss, rs):
        x, y, c = _mesh_pos()
        pairs = []
        for j, (cx, cy) in enumerate(_other_chips(x, y)):
            for t in range(n):
                src = bufs[t].at[j] if len(bufs[t].shape) == 3 else bufs[t]
                cp = pltpu.make_async_remote_copy(src_ref=src, dst_ref=bufs[n + t].at[j], send_sem=ss.at[3 * t + j],
                                                  recv_sem=rs.at[3 * t + j], device_id=(cx, cy, c),
                                                  device_id_type=MESH)
                pairs.append((cp, cp))
        return pairs
    return fn


class _Exchange:
    def __init__(self, name, srcs, land_shapes, fn, n_sems):
        self.name, self.fn = name, fn
        lands = _alloc(land_shapes, name + "_alloc")
        self.n = len(srcs)
        self.bufs, sems, self.token = _split_call(name + "_start", list(srcs) + list(lands),
                                                  starts=[(n_sems, fn)])
        self.sems = sems[0]

    def finish(self, after=()):
        bufs, _, _ = _split_call(self.name + "_wait", self.bufs, waits=[(*self.sems, self.fn)], after=after)
        return bufs[:self.n], bufs[self.n:]


def _row_block(rows, cols, mult=8, limit=3 * 512 * 1024, itemsize=4):
    best = None
    for br in range(mult, rows + 1, mult):
        if rows % br == 0 and br * cols * itemsize <= limit:
            best = br
    assert best is not None, (rows, cols)
    return best


_GROUP_BLOCK_BYTES = 2 * 1024 * 1024


def _group_plan(ss):
    plan = []
    for s in ss:
        half, cols = s.shape[-2:]
        br = _row_block(half, cols, mult=16, limit=_GROUP_BLOCK_BYTES)
        plan.append((br, half // br))
    return plan, max(nr for _, nr in plan)


def _chip_partial(gs, ss, ids, name):
    n = len(gs)
    plan, steps = _group_plan(ss)

    def body(ids_ref, *refs):
        for t in range(n):
            refs[2 * n + t][...] = (refs[t][...] + refs[n + t][...]).astype(BF16)

    g_specs, s_specs, o_specs = [], [], []
    for (br, nr), s in zip(plan, ss):
        blk = (None, br, s.shape[2])
        g_specs.append(pl.BlockSpec(
            blk, lambda j, r, ids_ref, nr=nr: (ids_ref[2 + j], ids_ref[0] * nr + jnp.minimum(r, nr - 1), 0)))
        s_specs.append(pl.BlockSpec(blk, lambda j, r, ids_ref, nr=nr: (ids_ref[2 + j], jnp.minimum(r, nr - 1), 0)))
        o_specs.append(pl.BlockSpec(blk, lambda j, r, ids_ref, nr=nr: (j, jnp.minimum(r, nr - 1), 0)))
    return pl.pallas_call(
        body, name=name,
        grid_spec=pltpu.PrefetchScalarGridSpec(num_scalar_prefetch=1, grid=(3, steps),
                                               in_specs=g_specs + s_specs, out_specs=o_specs),
        out_shape=[jax.ShapeDtypeStruct((3,) + s.shape[1:], BF16) for s in ss],
        compiler_params=pltpu.CompilerParams(dimension_semantics=("arbitrary", "arbitrary"),
                                             vmem_limit_bytes=VMEM_LIMIT),
    )(ids, *gs, *ss)


def _chip_sum(gs, ss, qs, ids, name):
    n = len(gs)
    plan, steps = _group_plan(ss)

    def body(ids_ref, *refs):
        for t in range(n):
            q_ref = refs[2 * n + t]
            own = refs[t][...] + refs[n + t][...]
            refs[3 * n + t][...] = (own + q_ref[2].astype(F32)) + (q_ref[0].astype(F32) + q_ref[1].astype(F32))

    g_specs, s_specs, q_specs, o_specs = [], [], [], []
    for (br, nr), s in zip(plan, ss):
        cols = s.shape[2]
        g_specs.append(pl.BlockSpec(
            (None, br, cols), lambda r, ids_ref, nr=nr: (ids_ref[1], ids_ref[0] * nr + jnp.minimum(r, nr - 1), 0)))
        s_specs.append(pl.BlockSpec((None, br, cols), lambda r, ids_ref, nr=nr: (ids_ref[1], jnp.minimum(r, nr - 1), 0)))
        q_specs.append(pl.BlockSpec((3, br, cols), lambda r, ids_ref, nr=nr: (0, jnp.minimum(r, nr - 1), 0)))
        o_specs.append(pl.BlockSpec((br, cols), lambda r, ids_ref, nr=nr: (jnp.minimum(r, nr - 1), 0)))
    return pl.pallas_call(
        body, name=name,
        grid_spec=pltpu.PrefetchScalarGridSpec(num_scalar_prefetch=1, grid=(steps,),
                                               in_specs=g_specs + s_specs + q_specs, out_specs=o_specs),
        out_shape=[jax.ShapeDtypeStruct(s.shape[1:], F32) for s in ss],
        compiler_params=pltpu.CompilerParams(dimension_semantics=("arbitrary",), vmem_limit_bytes=VMEM_LIMIT),
    )(ids, *gs, *ss, *qs)


def _adamw_math(w, g, m, v):
    mn = ADAM_B1 * m + (1.0 - ADAM_B1) * g
    vn = ADAM_B2 * v + (1.0 - ADAM_B2) * (g * g)
    m_hat = mn / (1.0 - ADAM_B1 ** ADAM_STEP)
    v_hat = vn / (1.0 - ADAM_B2 ** ADAM_STEP)
    return -ADAM_LR * (m_hat / (jnp.sqrt(v_hat) + ADAM_EPS) + ADAM_WD * w), mn, vn


def _adamw_halves(w, own, sib, m, v, ids, name, layer=0, n_layers=1, stacked=None):
    C = w.shape[1]
    R = w.shape[0] // n_layers
    half = R // 2
    br = _row_block(half, C)
    nh = half // br
    base = layer * 2 * nh

    def body(ids_ref, w_ref, own_ref, sib_ref, m_ref, v_ref, *rest):
        g_ref, d_ref, mo_ref, vo_ref = rest[-4:]
        is_own = (pl.program_id(0) // nh) == ids_ref[0]
        g = jnp.where(is_own, own_ref[...], sib_ref[...])
        g_ref[...] = g
        d_ref[...], mo_ref[...], vo_ref[...] = _adamw_math(w_ref[...], g, m_ref[...], v_ref[...])

    full = pl.BlockSpec((br, C), lambda r, ids_ref: (base + r, 0))
    own_spec = pl.BlockSpec((br, C), lambda r, ids_ref: (jnp.clip(r - ids_ref[0] * nh, 0, nh - 1), 0))
    sib_spec = pl.BlockSpec((br, C), lambda r, ids_ref: (jnp.clip(r - (1 - ids_ref[0]) * nh, 0, nh - 1), 0))
    in_specs = [full, own_spec, sib_spec, full, full]
    args = [ids, w, own, sib, m, v]
    aliases = {}
    if stacked is not None:
        in_specs += [ANY] * 4
        args += list(stacked)
        aliases = {6 + k: k for k in range(4)}
    return pl.pallas_call(
        body, name=name,
        grid_spec=pltpu.PrefetchScalarGridSpec(
            num_scalar_prefetch=1, grid=(2 * nh,), in_specs=in_specs, out_specs=[full] * 4),
        out_shape=[jax.ShapeDtypeStruct(w.shape, F32)] * 4, input_output_aliases=aliases,
        compiler_params=_params(),
    )(*args)


_PACK_UNIT = 1024


def _pack(arrs):
    flat = []
    for a in arrs:
        f = a.reshape(-1).astype(F32)
        pad = (-f.shape[0]) % _PACK_UNIT
        if pad:
            f = jnp.concatenate([f, jnp.zeros((pad,), F32)])
        flat.append(f)
    return jnp.concatenate(flat).reshape(-1, 128)


def kernel(x, p, norm_mix, norm_ffn, norm_ple, norm_kv, norm_final, a_w_in, a_norm_v, a_w_s, a_b_s, a_w_out, w_kv, b_w_q, b_sinks, b_w_o, f_w_up, f_conv_w, f_conv_b, f_w_down, ple_w_in, ple_w_gate, ple_b_gate, loss_target, m_norm_mix, m_norm_ffn, m_norm_ple, m_norm_kv, m_norm_final, m_a_w_in, m_a_norm_v, m_a_w_s, m_a_b_s, m_a_w_out, m_w_kv, m_b_w_q, m_b_sinks, m_b_w_o, m_f_w_up, m_f_conv_w, m_f_conv_b, m_f_w_down, m_ple_w_in, m_ple_w_gate, m_ple_b_gate, v_norm_mix, v_norm_ffn, v_norm_ple, v_norm_kv, v_norm_final, v_a_w_in, v_a_norm_v, v_a_w_s, v_a_b_s, v_a_w_out, v_w_kv, v_b_w_q, v_b_sinks, v_b_w_o, v_f_w_up, v_f_conv_w, v_f_conv_b, v_f_w_down, v_ple_w_in, v_ple_w_gate, v_ple_b_gate):
    given = dict(locals())

    small_shard = _pack([a_norm_v, f_conv_w])
    pad_rows = (-small_shard.shape[0]) % 16
    if pad_rows:
        small_shard = jnp.concatenate([small_shard, jnp.zeros((pad_rows, 128), F32)])
    groups = [
        [(a_w_in, (0,), BF16), (a_w_out, (0,), BF16), (small_shard, (), F32)],
        [(f_w_up, (0,), BF16), (f_w_down, (0,), BF16)],
        [(ple_w_in, (0,), BF16), (ple_w_gate, (0,), BF16), (w_kv, (), BF16), (b_w_q, (0,), BF16),
         (b_w_o, (0,), BF16), (f_w_up, (1,), BF16), (f_w_down, (1,), BF16), (ple_w_in, (1,), BF16),
         (ple_w_gate, (1,), BF16)],
    ]
    first = list(range(len(groups[0])))
    lands0, sems0, token0 = _split_call("gather_start_g0", _cast_place(groups[0], "cast_place_g0"),
                                        starts=[(3 * len(first), _gather_ici(first))])
    rest, spans, start = [], [], 0
    for gi, items in enumerate(groups[1:], 1):
        rest += _cast_place(items, f"cast_place_g{gi}", deps=(token0,))
        spans.append(list(range(start, start + len(items))))
        start += len(items)
    rest, rest_sems, rest_token = _split_call("gather_start", rest,
                                              starts=[(3 * len(sp), _gather_ici(sp)) for sp in spans])
    group_bufs = [lands0] + [[rest[t] for t in sp] for sp in spans]
    ici_sems = sems0 + rest_sems

    def finish_group(gi, after):
        bufs = group_bufs[gi]
        local = list(range(len(bufs)))
        bufs, d2d_sems, _ = _split_call(f"gather_pass_g{gi}", bufs, waits=[(*ici_sems[gi], _gather_ici(local))],
                                        starts=[(3 * len(local), _gather_d2d(local))], after=after)
        bufs, _, _ = _split_call(f"gather_done_g{gi}", bufs, waits=[(*d2d_sems[0], _gather_d2d(local))])
        return bufs

    def stage0():
        b_in, b_out, b_small = finish_group(0, (rest_token,))
        small_full = b_small.reshape(N_SHARD, -1)
        gv_full = small_full[:, :256].reshape(1, D_MODEL)
        cw_full = small_full[:, _PACK_UNIT:_PACK_UNIT + 2 * 3 * FF_BLK].reshape(N_SHARD, 2, 3, FF_BLK)
        cw_full = jnp.transpose(cw_full, (1, 2, 0, 3)).reshape(2, 3, N_FF)
        return gv_full, cw_full, b_in, b_out.reshape(D_MODEL, D_MODEL)

    def stage1(after):
        b_up, b_dn = finish_group(1, after)
        return b_up, b_dn.reshape(D_FF, D_MODEL)

    def stage2(after):
        pin0, gate0, kv_w, wq, wo, up1, dn1, pin1, gate1 = finish_group(2, after)
        sq = lambda a: a.reshape(D_MODEL, -1)
        return dict(w_pin=[pin0, pin1], w_gate=[sq(gate0), sq(gate1)], w_kv=sq(kv_w), w_q=sq(wq), w_o=sq(wo),
                    w_up1=up1, w_dn1=dn1.reshape(D_FF, D_MODEL))

    dx, (loss, (out_g, out_d, out_m, out_v)) = _local_step(
        x[0], p.reshape(2, -1, PLE_DIM), loss_target[0], norm_mix, norm_ffn, norm_ple, norm_kv, norm_final, a_w_s, a_b_s,
        b_sinks, f_conv_b, ple_b_gate, stage0, stage1, stage2, _Reducer(given))
    weight_names = ['norm_mix', 'norm_ffn', 'norm_ple', 'norm_kv', 'norm_final', 'a_w_in', 'a_norm_v', 'a_w_s',
                    'a_b_s', 'a_w_out', 'w_kv', 'b_w_q', 'b_sinks', 'b_w_o', 'f_w_up', 'f_conv_w', 'f_conv_b',
                    'f_w_down', 'ple_w_in', 'ple_w_gate', 'ple_b_gate']
    return (loss, dx.reshape(x.shape), *[out_g[k] for k in weight_names], *[out_d[k] for k in weight_names],
            *[out_m[k] for k in weight_names], *[out_v[k] for k in weight_names])


def _local_step(xs, p, tgt, norm_mix, norm_ffn, norm_ple, norm_kv, norm_final, a_w_s, a_b_s, b_sinks,
                f_conv_b, ple_b_gate, stage0, stage1, stage2, sched):
    tril = jnp.tril(jnp.ones((CHUNK, CHUNK), F32))
    wsm = (a_w_s[0] * tril[None]).astype(BF16)
    bsb = jnp.broadcast_to(a_b_s[0][:, :, None], (A_GROUPS, CHUNK, CHUNK))
    sinks = b_sinks[0]
    row = lambda a: a.reshape(1, -1)

    gv_full, cw_full, w_in, w_out = stage0()
    h1, zp = _mixer_a_fwd(xs, row(norm_mix[0]), gv_full, wsm, bsb, w_in, w_out)
    w_up0, w_dn0 = stage1((h1,))
    h2, hh0, c0 = _ffn_fwd(h1, row(norm_ffn[0]), cw_full[0], row(f_conv_b[0]), w_up0, w_dn0, 0)
    rest = stage2((h2,))
    w_pin, w_gate, w_kv_f, w_q, w_o = rest['w_pin'], rest['w_gate'], rest['w_kv'], rest['w_q'], rest['w_o']
    w_up = [w_up0, rest['w_up1']]
    w_dn = [w_dn0, rest['w_dn1']]
    h3, kv, a0 = _ple_fwd_kv(h2, p, row(norm_ple[0]), row(ple_b_gate[0]), row(norm_kv), w_pin[0], w_gate[0], w_kv_f)
    h4, q, ao, probs, psink = _attn_fwd(h3, row(norm_mix[1]), kv, sinks, w_q, w_o)
    h5, hh1, c1 = _ffn_fwd(h4, row(norm_ffn[1]), cw_full[1], row(f_conv_b[1]), w_up[1], w_dn[1], 1)
    dh6, loss_acc, dn_final, a1 = _ple_fwd_final(
        h5, p, tgt, row(norm_ple[1]), row(ple_b_gate[1]), row(norm_final), w_pin[1], w_gate[1])

    def pieces(g):
        return g.reshape(N_SHARD, -1, g.shape[-1])

    dh5, g_pin1, g_gate1, dbg1, dnple1 = _ple_bwd(dh6, h5, p, a1, row(norm_ple[1]), w_pin[1], w_gate[1], 1)
    early = {('ple_w_in', 1): g_pin1, ('ple_w_gate', 1): pieces(g_gate1)}
    dh4, g_up1, g_dn1, dcw1, dcb1, dnffn1 = _ffn_bwd(
        dh5, h4, hh1, c1, row(norm_ffn[1]), cw_full[1], w_up[1], w_dn[1], 1)
    early['f_w_down', 1] = pieces(g_dn1)
    early['f_w_up', 1] = g_up1
    dh3a, g_wq, g_wo, dkv, dsink, dnmix1 = _attn_bwd(dh4, h3, q, kv, ao, probs, psink, row(norm_mix[1]), w_q, w_o)
    early['b_w_o', 0] = pieces(g_wo)
    early['b_w_q', 0] = pieces(g_wq)
    dh2, g_pin0, g_gate0, dbg0, dnple0, g_wkv, dnkv = _ple_bwd(
        dh3a, h2, p, a0, row(norm_ple[0]), w_pin[0], w_gate[0], 0,
        kv_args=(h3, dkv, row(norm_kv), w_kv_f))
    early['w_kv', 0] = pieces(g_wkv)
    early['ple_w_in', 0] = g_pin0
    early['ple_w_gate', 0] = pieces(g_gate0)
    deps = sched.early_ready(early)
    dh1, g_up0, g_dn0, dcw0, dcb0, dnffn0 = _ffn_bwd(
        dh2, h1, hh0, c0, row(norm_ffn[0]), cw_full[0], w_up[0], w_dn[0], 0, deps=deps,
        between=lambda part: sched.after_ffn_half((part,)))
    deps = sched.ffn0_ready({('f_w_down', 0): pieces(g_dn0), ('f_w_up', 0): g_up0})
    dx, g_win, g_wout, dws, dbs, dgv, dnmix0 = _mixer_a_bwd(
        dh1, xs, zp, row(norm_mix[0]), gv_full, wsm, bsb, tril, w_in, w_out, deps=deps)
    g_wout = pieces(g_wout)

    small_grads = {
        'norm_mix': jnp.concatenate([dnmix0, dnmix1]), 'norm_ffn': jnp.concatenate([dnffn0, dnffn1]),
        'norm_ple': jnp.concatenate([dnple0, dnple1]), 'norm_kv': dnkv, 'norm_final': dn_final,
        'a_norm_v': dgv, 'a_w_s': dws.reshape(A_GROUPS * CHUNK, CHUNK), 'a_b_s': dbs[:, :, 0],
        'b_sinks': dsink[0:1, :], 'f_conv_w': jnp.concatenate([dcw0, dcw1]),
        'f_conv_b': jnp.concatenate([dcb0, dcb1]), 'ple_b_gate': jnp.concatenate([dbg0, dbg1]),
        'loss': loss_acc,
    }
    outs = sched.finish({('a_w_in', 0): g_win, ('a_w_out', 0): g_wout}, small_grads, (dx,))
    return dx, outs


class _Reducer:
    def __init__(self, given):
        self.given = given
        cx, cy, cc = _mesh_pos()
        self.shard = 2 * cx + cy
        s = self.shard
        self.ids = jnp.stack([cc, s, s ^ 2, s ^ 1, s ^ 3]).astype(jnp.int32)
        self.out = [{}, {}, {}, {}]
        self.stacked = {}

    def _send(self, tag, grads, small=()):
        keys = list(grads)
        srcs = [grads[k] for k in keys] + list(small)
        shapes = [((N_SHARD, g.shape[1] // 2, g.shape[2]), F32) for g in srcs[:len(keys)]]
        shapes += [(s.shape, F32) for s in small]
        return keys, _Exchange(f"send_{tag}", srcs, shapes, _send_to_sibling(len(srcs)), len(srcs))

    def _exchange(self, tag, keys, send, after):
        srcs, lands = send.finish(after)
        n = len(keys)
        parts = _chip_partial(srcs[:n], lands[:n], self.ids, f"chip_partial_{tag}")
        shapes = [(p.shape, BF16) for p in parts]
        if len(srcs) > n:
            small = _small_add(srcs[n:], lands[n:])
            parts += small
            shapes += [((3,) + s.shape, F32) for s in small]
        exch = _Exchange(f"exch_{tag}", parts, shapes, _send_to_chips(len(parts)), 3 * len(parts))
        return (keys, srcs[:n], lands[:n], exch)

    def _swap(self, tag, state, after):
        keys, grads, sib, exch = state
        parts, recv = exch.finish(after)
        n = len(keys)
        own = _chip_sum(grads, sib, recv[:n], self.ids, f"chip_sum_{tag}")
        small_red = _small_sum(parts[n:], recv[n:]) if len(parts) > n else None
        return keys, _Exchange(f"swap_{tag}", own, [(o.shape, F32) for o in own], _send_to_sibling(n), n), small_red

    def _adamw(self, keys, swap, after):
        own, sib = swap.finish(after)
        last = None
        for (name, layer), o, s in zip(keys, own, sib):
            w = self.given[name]
            n_layers = w.shape[0] if w.ndim == 3 else 1
            c2 = w.shape[-1]
            res = _adamw_halves(w.reshape(-1, c2), o, s, self.given['m_' + name].reshape(-1, c2),
                                self.given['v_' + name].reshape(-1, c2), self.ids, f"adamw_{name}{layer}",
                                layer, n_layers, self.stacked.get(name))
            self.stacked[name] = res
            if layer == 0:
                for dst, r in zip(self.out, res):
                    dst[name] = r.reshape(w.shape)
            last = res[0]
        return last

    def early_ready(self, grads):
        self.e_keys, self.e_send = self._send("e", grads)
        return (self.e_send.token,)

    def after_ffn_half(self, after):
        self.e_state = self._exchange("e", self.e_keys, self.e_send, after)
        return (self.e_state[3].token,)

    def ffn0_ready(self, grads):
        _, self.e_swap, _ = self._swap("e", self.e_state, tuple(grads.values()))
        self.f_keys, self.f_send = self._send("f", grads)
        return (self.f_send.token, self.e_swap.token)

    def finish(self, grads, small_grads, after):
        small_names = list(small_grads)
        f_state = self._exchange("f", self.f_keys, self.f_send, after)
        a_keys, a_send = self._send("a", grads, [small_grads[k] for k in small_names])
        a_state = self._exchange("a", a_keys, a_send, (f_state[3].token,))
        e_done = self._adamw(self.e_keys, self.e_swap, (a_state[3].token,))
        f_keys, f_swap, _ = self._swap("f", f_state, (e_done,))
        f_done = self._adamw(f_keys, f_swap, ())
        _, a_swap, small_red = self._swap("a", a_state, (f_done,))
        self._adamw(a_keys, a_swap, ())

        given = self.given
        reduced = dict(zip(small_names, small_red))
        loss = reduced.pop('loss')[0, 0]
        names = list(reduced)
        items = []
        for k in names:
            g = reduced[k]
            cols = g.shape[1] // N_SHARD if k in ('a_norm_v', 'f_conv_w') else g.shape[1]
            view = lambda a: _lane_pad(a.reshape(g.shape[0], -1), cols)
            items.append((view(given[k]), g, view(given['m_' + k]), view(given['v_' + k])))
        res = _adamw_small(items, self.ids)
        for k, four in zip(names, res):
            width = given[k].size // four[0].shape[0]
            for dst, r in zip(self.out, four):
                dst[k] = r[:, :width].reshape(given[k].shape)
        return loss, self.out


def _lane_pad(a, cols):
    return a if a.shape[1] == cols else jnp.pad(a, ((0, 0), (0, cols - a.shape[1])))


def _small_add(a_list, b_list):
    n = len(a_list)

    def body(*refs):
        for t in range(n):
            refs[2 * n + t][...] = refs[t][...] + refs[n + t][...]

    return pl.pallas_call(body, name="chip_partial_small",
                          out_shape=[jax.ShapeDtypeStruct(a.shape, F32) for a in a_list])(*a_list, *b_list)


def _small_sum(parts, recvs):
    n = len(parts)

    def body(*refs):
        for t in range(n):
            q = refs[n + t]
            refs[2 * n + t][...] = (refs[t][...] + q[2]) + (q[0] + q[1])

    return pl.pallas_call(body, name="chip_sum_small",
                          out_shape=[jax.ShapeDtypeStruct(p.shape, F32) for p in parts])(*parts, *recvs)


def _adamw_small(items, ids):
    n = len(items)

    def body(ids_ref, *refs):
        for t in range(n):
            w_ref, g_ref, m_ref, v_ref = refs[4 * t:4 * t + 4]
            g_out, d_ref, mo_ref, vo_ref = refs[4 * n + 4 * t:4 * n + 4 * t + 4]
            g = g_ref[...]
            g_out[...] = g
            d_ref[...], mo_ref[...], vo_ref[...] = _adamw_math(w_ref[...], g, m_ref[...], v_ref[...])

    in_specs, out_specs, out_shape, args = [], [], [], []
    for w, g, m, v in items:
        full = pl.BlockSpec(w.shape, lambda i, ids_ref: (0, 0))
        g_spec = full if g.shape == w.shape else pl.BlockSpec(w.shape, lambda i, ids_ref: (0, ids_ref[1]))
        in_specs += [full, g_spec, full, full]
        out_specs += [full] * 4
        out_shape += [jax.ShapeDtypeStruct(w.shape, F32)] * 4
        args += [w, g, m, v]
    res = pl.pallas_call(
        body, name="adamw_small",
        grid_spec=pltpu.PrefetchScalarGridSpec(num_scalar_prefetch=1, grid=(1,), in_specs=in_specs,
                                               out_specs=out_specs),
        out_shape=out_shape, compiler_params=_params(),
    )(ids, *args)
    return [res[4 * t:4 * t + 4] for t in range(n)]
```

```python
import functools
import math

import numpy as np
import jax
import jax.numpy as jnp
from jax import lax
from jax.experimental import pallas as pl
from jax.experimental.pallas import tpu as pltpu

F32 = jnp.float32
BF16 = jnp.bfloat16

D_MODEL = 1024
CHUNK = 128
A_GROUPS = 8
HEAD_DIM = 64
N_Q_HEADS = 16
N_KV_HEADS = 4
GQA_GROUP = N_Q_HEADS // N_KV_HEADS
KV_DIM = N_KV_HEADS * HEAD_DIM
BLOCK = 128
D_FF = 2816
N_FF = 2 * D_FF
FF_BLK = N_FF // 4
PLE_DIM = 256
EPS = 1e-6
NEG = -1e30
N_SHARD = 4

ADAM_LR = 0.001
ADAM_B1 = 0.9
ADAM_B2 = 0.999
ADAM_EPS = 1e-08
ADAM_WD = 0.01
ADAM_STEP = 10

VMEM_LIMIT = 60 * 1024 * 1024
MESH = pl.DeviceIdType.MESH
ANY = pl.BlockSpec(memory_space=pl.ANY)
SMEM = pl.BlockSpec(memory_space=pltpu.SMEM)

_SLOPES = [float(np.float32(2.0 ** (-8.0 * (h + 1) / N_Q_HEADS))) for h in range(N_Q_HEADS)]


def _dot(a, b):
    return jnp.dot(a, b, preferred_element_type=F32)


def _dot_nt(a, b):
    return lax.dot_general(a, b, (((1,), (1,)), ((), ())), preferred_element_type=F32)


def _dot_tn(a, b):
    return lax.dot_general(a, b, (((0,), (0,)), ((), ())), preferred_element_type=F32)


def _rms(x, g):
    r = lax.rsqrt(jnp.mean(x * x, axis=-1, keepdims=True) + EPS)
    xh = x * r
    return xh * g, xh, r


def _rms_bwd(dy, xh, r, g):
    dxh = dy * g
    dg = jnp.sum(dy * xh, axis=0, keepdims=True)
    dx = r * (dxh - xh * jnp.mean(dxh * xh, axis=-1, keepdims=True))
    return dx, dg


_GELU_C = math.sqrt(2.0 / math.pi)


def _gelu(x):
    t = jnp.tanh(_GELU_C * (x + 0.044715 * (x * x * x)))
    return 0.5 * x * (1.0 + t)


def _gelu_grad(x):
    x2 = x * x
    t = jnp.tanh(_GELU_C * (x + 0.044715 * (x2 * x)))
    return 0.5 * (1.0 + t) + 0.5 * x * (1.0 - t * t) * (_GELU_C * (1.0 + 3.0 * 0.044715 * x2))


def _sigmoid(x):
    return 0.5 * jnp.tanh(0.5 * x) + 0.5


def _load_once(pairs, sem):
    @pl.when(pl.program_id(0) == 0)
    def _():
        cps = [pltpu.make_async_copy(s, d, sem.at[i]) for i, (s, d) in enumerate(pairs)]
        for cp in cps:
            cp.start()
        for cp in cps:
            cp.wait()


def _params(n_axes=1, vmem=VMEM_LIMIT):
    return pltpu.CompilerParams(dimension_semantics=("arbitrary",) * n_axes, vmem_limit_bytes=vmem)


def _row_spec(tm, n, rev_nt=None):
    if rev_nt is None:
        return pl.BlockSpec((tm, n), lambda i: (i, 0))
    return pl.BlockSpec((tm, n), lambda i: (rev_nt - 1 - i, 0))


def _const_spec(shape):
    nd = len(shape)
    return pl.BlockSpec(shape, lambda i: (0,) * nd)


def _add_deps(body, in_specs, args, deps):
    nd = len(deps)
    if nd == 0:
        return body, list(in_specs), list(args)

    def wrapped(*refs):
        return body(*refs[nd:])

    return wrapped, [ANY] * nd + list(in_specs), list(deps) + list(args)


def _zero_first(refs):
    @pl.when(pl.program_id(0) == 0)
    def _():
        for r in refs:
            r[...] = jnp.zeros(r.shape, r.dtype)


def _mixer_a_fwd(x, nmix, gv, wsm, bsb, w_in, w_out):
    T = x.shape[0]
    tm = min(512, T)
    nt = T // tm
    nw = 2 * D_MODEL // N_SHARD

    def body(x_ref, nmix_ref, gv_ref, ws_ref, bsb_ref, w_in_hbm, w_out_hbm,
             h1_ref, zp_ref, w_in_v, w_out_v, gated_v, sem):
        _load_once([(w_in_hbm, w_in_v), (w_out_hbm, w_out_v)], sem)
        xv = x_ref[...]
        xn = _rms(xv, nmix_ref[...])[0].astype(BF16)
        for j in range(N_SHARD):
            zp_ref[:, j * nw:(j + 1) * nw] = _dot(xn, w_in_v[j])
        z = _gelu(zp_ref[...])
        u = z[:, :D_MODEL]
        vn = _rms(z[:, D_MODEL:], gv_ref[...])[0].astype(BF16)
        for c in range(tm // CHUNK):
            rows = slice(c * CHUNK, (c + 1) * CHUNK)
            for h in range(A_GROUPS):
                cols = slice(h * CHUNK, (h + 1) * CHUNK)
                s = _dot(ws_ref[h], vn[rows, cols]) + bsb_ref[h]
                gated_v[rows, cols] = (u[rows, cols] * s).astype(BF16)
        h1_ref[...] = xv + _dot(gated_v[...], w_out_v[...])

    return pl.pallas_call(
        body, name="mixer_a_fwd", grid=(nt,),
        in_specs=[_row_spec(tm, D_MODEL), _const_spec((1, D_MODEL)), _const_spec((1, D_MODEL)),
                  _const_spec((A_GROUPS, CHUNK, CHUNK)), _const_spec((A_GROUPS, CHUNK, CHUNK)), ANY, ANY],
        out_specs=[_row_spec(tm, D_MODEL), _row_spec(tm, 2 * D_MODEL)],
        out_shape=[jax.ShapeDtypeStruct((T, D_MODEL), F32), jax.ShapeDtypeStruct((T, 2 * D_MODEL), F32)],
        scratch_shapes=[pltpu.VMEM((N_SHARD, D_MODEL, nw), BF16), pltpu.VMEM((D_MODEL, D_MODEL), BF16),
                        pltpu.VMEM((tm, D_MODEL), BF16), pltpu.SemaphoreType.DMA((2,))],
        compiler_params=_params(),
    )(x, nmix, gv, wsm, bsb, w_in, w_out)


def _mixer_a_bwd(dh, x, zp, nmix, gv, wsm, bsb, tril, w_in, w_out, deps=()):
    T = x.shape[0]
    tm = min(256, T)
    nt = T // tm
    nw = 2 * D_MODEL // N_SHARD

    def body(dh_ref, x_ref, zp_ref, nmix_ref, gv_ref, ws_ref, bsb_ref, tril_ref, w_in_hbm, w_out_hbm,
             dx_ref, dwin_ref, dwout_ref, dws_ref, dbs_ref, dgv_ref, dnmix_ref,
             w_in_v, w_out_v, du_v, dvn_v, dbs_v, gated_ref, sem):
        _load_once([(w_in_hbm, w_in_v), (w_out_hbm, w_out_v)], sem)
        _zero_first([dws_ref, dbs_v, dgv_ref, dnmix_ref, dwin_ref, dwout_ref])
        i = pl.program_id(0)
        dhv = dh_ref[...]
        dhb = dhv.astype(BF16)
        xv = x_ref[...]
        xn, xh, r = _rms(xv, nmix_ref[...])
        xnb = xn.astype(BF16)
        zpv = zp_ref[...]
        z = _gelu(zpv)
        u = z[:, :D_MODEL]
        vn_f, vh, rv = _rms(z[:, D_MODEL:], gv_ref[...])
        vn = vn_f.astype(BF16)
        dgated = _dot_nt(dhb, w_out_v[...])
        for c in range(tm // CHUNK):
            rows = slice(c * CHUNK, (c + 1) * CHUNK)
            for h in range(A_GROUPS):
                cols = slice(h * CHUNK, (h + 1) * CHUNK)
                vn_h = vn[rows, cols]
                s = _dot(ws_ref[h], vn_h) + bsb_ref[h]
                dgt = dgated[rows, cols]
                u_h = u[rows, cols]
                gated_ref[rows, cols] = (u_h * s).astype(BF16)
                du_v[rows, cols] = dgt * s
                ds = dgt * u_h
                dsb = ds.astype(BF16)
                dws_ref[h] += _dot_nt(dsb, vn_h)
                dbs_v[h] += ds
                dvn_v[rows, cols] = _dot_tn(ws_ref[h], dsb)
        dwout_ref[...] += _dot_tn(gated_ref[...], dhb)
        dv, dgv = _rms_bwd(dvn_v[...], vh, rv, gv_ref[...])
        dgv_ref[...] += dgv
        dzu = (du_v[...] * _gelu_grad(zpv[:, :D_MODEL])).astype(BF16)
        dzv = (dv * _gelu_grad(zpv[:, D_MODEL:])).astype(BF16)
        dzs = (dzu[:, :nw], dzu[:, nw:], dzv[:, :nw], dzv[:, nw:])
        dxn = jnp.zeros((tm, D_MODEL), F32)
        for j in range(N_SHARD):
            dxn = dxn + _dot_nt(dzs[j], w_in_v[j])
            dwin_ref[j] += _dot_tn(xnb, dzs[j])
        dxx, dn = _rms_bwd(dxn, xh, r, nmix_ref[...])
        dnmix_ref[...] += dn
        dx_ref[...] = dhv + dxx

        @pl.when(i == nt - 1)
        def _():
            for h in range(A_GROUPS):
                dws_ref[h] = dws_ref[h] * tril_ref[...]
                dbs_ref[h] = jnp.broadcast_to(jnp.sum(dbs_v[h], axis=1, keepdims=True), (CHUNK, CHUNK))

    grp = (A_GROUPS, CHUNK, CHUNK)
    body, in_specs, args = _add_deps(
        body, [_row_spec(tm, D_MODEL), _row_spec(tm, D_MODEL), _row_spec(tm, 2 * D_MODEL),
               _const_spec((1, D_MODEL)), _const_spec((1, D_MODEL)), _const_spec(grp), _const_spec(grp),
               _const_spec((CHUNK, CHUNK)), ANY, ANY],
        [dh, x, zp, nmix, gv, wsm, bsb, tril, w_in, w_out], deps)
    return pl.pallas_call(
        body, name="mixer_a_bwd", grid=(nt,), in_specs=in_specs,
        out_specs=[_row_spec(tm, D_MODEL), _const_spec((N_SHARD, D_MODEL, nw)), _const_spec((D_MODEL, D_MODEL)),
                   _const_spec(grp), _const_spec(grp), _const_spec((1, D_MODEL)), _const_spec((1, D_MODEL))],
        out_shape=[jax.ShapeDtypeStruct((T, D_MODEL), F32), jax.ShapeDtypeStruct((N_SHARD, D_MODEL, nw), F32),
                   jax.ShapeDtypeStruct((D_MODEL, D_MODEL), F32),
                   jax.ShapeDtypeStruct(grp, F32), jax.ShapeDtypeStruct(grp, F32),
                   jax.ShapeDtypeStruct((1, D_MODEL), F32), jax.ShapeDtypeStruct((1, D_MODEL), F32)],
        scratch_shapes=[pltpu.VMEM((N_SHARD, D_MODEL, nw), BF16), pltpu.VMEM((D_MODEL, D_MODEL), BF16),
                        pltpu.VMEM((tm, D_MODEL), F32), pltpu.VMEM((tm, D_MODEL), F32),
                        pltpu.VMEM(grp, F32), pltpu.VMEM((tm, D_MODEL), BF16), pltpu.SemaphoreType.DMA((2,))],
        compiler_params=_params(),
    )(*args)


def _load_ffn_weights(w_up_hbm, w_dn_hbm, layer, w_up_v, w_dn_v, sem):
    _load_once([(w_up_hbm, w_up_v), (w_dn_hbm, w_dn_v)], sem)


def _ffn_fwd(h, nffn, cw, cb, w_up, w_dn, layer):
    T = h.shape[0]
    tm = min(256, T)
    nt = T // tm

    def body(h_ref, n_ref, cw_ref, cb_ref, w_up_hbm, w_dn_hbm, out_ref, hh_ref, gate_ref,
             w_up_v, w_dn_v, carry_v, sem):
        _load_ffn_weights(w_up_hbm, w_dn_hbm, layer, w_up_v, w_dn_v, sem)
        _zero_first([carry_v])
        xv = h_ref[...]
        xf = _rms(xv, n_ref[...])[0].astype(BF16)
        acc = xv
        for j in range(2):
            cs = []
            for blk in (j, j + 2):
                cols = slice(blk * FF_BLK, (blk + 1) * FF_BLK)
                hh = _dot(xf, w_up_v[blk])
                hh_ref[:, cols] = hh.astype(BF16)
                ext = jnp.concatenate([carry_v[blk], hh], axis=0)
                carry_v[blk] = hh[tm - 8:, :]
                s1 = pltpu.roll(ext, 1, 0)[8:]
                s2 = pltpu.roll(ext, 2, 0)[8:]
                cs.append(cb_ref[:, cols] + cw_ref[0:1, cols] * s2 + cw_ref[1:2, cols] * s1
                          + cw_ref[2:3, cols] * hh)
            cg, cu = cs
            sg = _sigmoid(cg)
            sil = cg * sg
            act = (sil * cu).astype(BF16)
            for kind, val in enumerate((cu, sil, sg + sil * (1.0 - sg))):
                gate_ref[:, kind * D_FF + j * FF_BLK:kind * D_FF + (j + 1) * FF_BLK] = val.astype(BF16)
            gate_ref[:, 3 * D_FF + j * FF_BLK:3 * D_FF + (j + 1) * FF_BLK] = act
            acc = acc + _dot(act, w_dn_v[j * FF_BLK:(j + 1) * FF_BLK, :])
        out_ref[...] = acc

    return pl.pallas_call(
        body, name=f"ffn_fwd{layer}", grid=(nt,),
        in_specs=[_row_spec(tm, D_MODEL), _const_spec((1, D_MODEL)), _const_spec((3, N_FF)),
                  _const_spec((1, N_FF)), ANY, ANY],
        out_specs=[_row_spec(tm, D_MODEL), _row_spec(tm, N_FF), _row_spec(tm, 4 * D_FF)],
        out_shape=[jax.ShapeDtypeStruct((T, D_MODEL), F32), jax.ShapeDtypeStruct((T, N_FF), BF16),
                   jax.ShapeDtypeStruct((T, 4 * D_FF), BF16)],
        scratch_shapes=[pltpu.VMEM((N_SHARD, D_MODEL, FF_BLK), BF16), pltpu.VMEM((D_FF, D_MODEL), BF16),
                        pltpu.VMEM((N_SHARD, 8, FF_BLK), F32), pltpu.SemaphoreType.DMA((2 * N_SHARD,))],
        compiler_params=_params(),
    )(h, nffn, cw, cb, w_up, w_dn)


def _wgrad(a, b, bn, col_sharded, name, deps=(), a_cols=None):
    T = a.shape[0]
    K, a_blk = (a.shape[1], 0) if a_cols is None else a_cols
    N = b.shape[1]
    tt = min(2048, T)
    nn, ntt = N // bn, T // tt
    kr = K // N_SHARD

    def body(a_ref, b_ref, o_ref):
        @pl.when(pl.program_id(1) == 0)
        def _():
            o_ref[...] = jnp.zeros(o_ref.shape, F32)
        d = _dot_tn(a_ref[...].astype(BF16), b_ref[...].astype(BF16))
        if col_sharded:
            o_ref[...] += d
        else:
            for j in range(N_SHARD):
                o_ref[j] += d[j * kr:(j + 1) * kr]

    if col_sharded:
        assert nn == N_SHARD
        out_spec = pl.BlockSpec((None, K, bn), lambda n, t: (n, 0, 0))
        out_shape = jax.ShapeDtypeStruct((N_SHARD, K, bn), F32)
    else:
        out_spec = pl.BlockSpec((N_SHARD, kr, bn), lambda n, t: (0, 0, n))
        out_shape = jax.ShapeDtypeStruct((N_SHARD, kr, N), F32)
    body, in_specs, args = _add_deps(
        body, [pl.BlockSpec((tt, K), lambda n, t: (t, a_blk)), pl.BlockSpec((tt, bn), lambda n, t: (t, n))],
        [a, b], deps)
    return pl.pallas_call(
        body, name=name, grid=(nn, ntt), in_specs=in_specs, out_specs=out_spec, out_shape=out_shape,
        compiler_params=pltpu.CompilerParams(dimension_semantics=("arbitrary",) * 2, vmem_limit_bytes=VMEM_LIMIT),
    )(*args)


def _ffn_bwd(dh, h, hh, gate, nffn, cw, w_up, w_dn, layer, deps=(), between=None):
    T = h.shape[0]
    tm = min(256, T)
    nt = T // tm

    def body(dh_ref, h_ref, hh_ref, u_ref, sil_ref, dsil_ref, n_ref, cw_ref, w_up_hbm, w_dn_hbm,
             dhin_ref, dhh_ref, xf_ref, dcw_ref, dcb_ref, dn_ref,
             w_up_v, w_dn_v, carry_v, sem):
        _load_ffn_weights(w_up_hbm, w_dn_hbm, layer, w_up_v, w_dn_v, sem)
        _zero_first([carry_v, dcw_ref, dcb_ref, dn_ref])
        dout = dh_ref[...]
        doutb = dout.astype(BF16)
        xf_f, xh, r = _rms(h_ref[...], n_ref[...])
        xf_ref[...] = xf_f.astype(BF16)
        dxf = jnp.zeros((tm, D_MODEL), F32)
        for j in range(2):
            blks = (j, j + 2)
            pair = slice(j * FF_BLK, (j + 1) * FF_BLK)
            dact = _dot_nt(doutb, w_dn_v[pair, :])
            dcs = (dact * u_ref[:, pair].astype(F32) * dsil_ref[:, pair].astype(F32),
                   dact * sil_ref[:, pair].astype(F32))
            for blk, dc in zip(blks, dcs):
                cols = slice(blk * FF_BLK, (blk + 1) * FF_BLK)
                hhv = hh_ref[:, cols].astype(F32)
                ext = jnp.concatenate([dc, carry_v[blk]], axis=0)
                carry_v[blk] = dc[:8, :]
                n = tm + 8
                a1 = pltpu.roll(ext, n - 1, 0)[:tm]
                a2 = pltpu.roll(ext, n - 2, 0)[:tm]
                dcb_ref[:, cols] += jnp.sum(dc, axis=0, keepdims=True)
                dcw_ref[0:1, cols] += jnp.sum(a2 * hhv, axis=0, keepdims=True)
                dcw_ref[1:2, cols] += jnp.sum(a1 * hhv, axis=0, keepdims=True)
                dcw_ref[2:3, cols] += jnp.sum(dc * hhv, axis=0, keepdims=True)
                dhh = (cw_ref[2:3, cols] * dc + cw_ref[1:2, cols] * a1 + cw_ref[0:1, cols] * a2).astype(BF16)
                dhh_ref[:, cols] = dhh
                dxf = dxf + _dot_nt(dhh, w_up_v[blk])
        dxx, dn = _rms_bwd(dxf, xh, r, n_ref[...])
        dn_ref[...] += dn
        dhin_ref[...] = dout + dxx

    rev = functools.partial(_row_spec, rev_nt=nt)

    def kind(k):
        return pl.BlockSpec((tm, D_FF), lambda i: (nt - 1 - i, k))

    body, in_specs, args = _add_deps(
        body, [rev(tm, D_MODEL), rev(tm, D_MODEL), rev(tm, N_FF), kind(0), kind(1), kind(2),
               _const_spec((1, D_MODEL)), _const_spec((3, N_FF)), ANY, ANY],
        [dh, h, hh, gate, gate, gate, nffn, cw, w_up, w_dn], deps)
    dhin, dhh, xf, dcw, dcb, dn = pl.pallas_call(
        body, name=f"ffn_bwd{layer}", grid=(nt,), in_specs=in_specs,
        out_specs=[rev(tm, D_MODEL), rev(tm, N_FF), rev(tm, D_MODEL),
                   _const_spec((3, N_FF)), _const_spec((1, N_FF)), _const_spec((1, D_MODEL))],
        out_shape=[jax.ShapeDtypeStruct((T, D_MODEL), F32),
                   jax.ShapeDtypeStruct((T, N_FF), BF16), jax.ShapeDtypeStruct((T, D_MODEL), BF16),
                   jax.ShapeDtypeStruct((3, N_FF), F32), jax.ShapeDtypeStruct((1, N_FF), F32),
                   jax.ShapeDtypeStruct((1, D_MODEL), F32)],
        scratch_shapes=[pltpu.VMEM((N_SHARD, D_MODEL, FF_BLK), BF16), pltpu.VMEM((D_FF, D_MODEL), BF16),
                        pltpu.VMEM((N_SHARD, 8, FF_BLK), F32), pltpu.SemaphoreType.DMA((2 * N_SHARD,))],
        compiler_params=_params(),
    )(*args)
    deps2 = between(dhin) if between is not None else ()
    dwdn = _wgrad(gate, dh, D_MODEL // 2, False, f"wgrad_ffn_down{layer}", deps=deps2, a_cols=(D_FF, 3))
    dwup = _wgrad(xf, dhh, FF_BLK, True, f"wgrad_ffn_up{layer}", deps=deps2)
    return dhin, dwup, dwdn, dcw, dcb, dn


def _load_ple_weights(w_pin_hbm, w_gate_hbm, layer, w_pin_v, w_gate_v, sem, extra=()):
    _load_once([(w_pin_hbm, w_pin_v), (w_gate_hbm, w_gate_v)] + list(extra), sem)


def _p_spec(tm, layer):
    return pl.BlockSpec((None, tm, PLE_DIM), lambda i: (layer, i, 0))


def _ple_terms(xv, p_ref, n_ref, bg_ref, w_pin_v, w_gate_v, pe_v, a_ref, saved):
    pw = D_MODEL // N_SHARD
    xg, xh, r = _rms(xv, n_ref[...])
    xgb = xg.astype(BF16)
    if saved:
        gate = _sigmoid(a_ref[...].astype(F32))
    else:
        a = _dot(xgb, w_gate_v[...]) + bg_ref[...]
        a_ref[...] = a.astype(BF16)
        gate = _sigmoid(a)
    pb = p_ref[...].astype(BF16)
    for j in range(N_SHARD):
        pe_v[:, j * pw:(j + 1) * pw] = _dot(pb, w_pin_v[j])
    pe = pe_v[...]
    return pe * gate, pe, gate, xgb, xh, r


def _ple_fwd_kv(h, p, nple, bg, nkv, w_pin, w_gate, w_kv):
    T = h.shape[0]
    tm = min(512, T)
    nt = T // tm
    pw = D_MODEL // N_SHARD

    def body(h_ref, p_ref, n_ref, bg_ref, nkv_ref, w_pin_hbm, w_gate_hbm, w_kv_hbm,
             out_ref, kv_ref, a_ref, w_pin_v, w_gate_v, w_kv_v, pe_v, sem):
        _load_ple_weights(w_pin_hbm, w_gate_hbm, 0, w_pin_v, w_gate_v, sem, [(w_kv_hbm, w_kv_v)])
        xv = h_ref[...]
        hn = xv + _ple_terms(xv, p_ref, n_ref, bg_ref, w_pin_v, w_gate_v, pe_v, a_ref, False)[0]
        out_ref[...] = hn
        kvn = _rms(hn, nkv_ref[...])[0].astype(BF16)
        kv_ref[...] = _dot(kvn, w_kv_v[...]).astype(BF16)

    vec = _const_spec((1, D_MODEL))
    return pl.pallas_call(
        body, name="ple_fwd0", grid=(nt,),
        in_specs=[_row_spec(tm, D_MODEL), _p_spec(tm, 0), vec, vec, vec, ANY, ANY, ANY],
        out_specs=[_row_spec(tm, D_MODEL), _row_spec(tm, 2 * KV_DIM), _row_spec(tm, D_MODEL)],
        out_shape=[jax.ShapeDtypeStruct((T, D_MODEL), F32), jax.ShapeDtypeStruct((T, 2 * KV_DIM), BF16),
                   jax.ShapeDtypeStruct((T, D_MODEL), BF16)],
        scratch_shapes=[pltpu.VMEM((N_SHARD, PLE_DIM, pw), BF16), pltpu.VMEM((D_MODEL, D_MODEL), BF16),
                        pltpu.VMEM((D_MODEL, 2 * KV_DIM), BF16), pltpu.VMEM((tm, D_MODEL), F32),
                        pltpu.SemaphoreType.DMA((2 * N_SHARD + 1,))],
        compiler_params=_params(),
    )(h, p, nple, bg, nkv, w_pin, w_gate, w_kv)


def _ple_fwd_final(h, p, tgt, nple, bg, nfin, w_pin, w_gate):
    T = h.shape[0]
    tm = min(512, T)
    nt = T // tm
    pw = D_MODEL // N_SHARD

    def body(h_ref, p_ref, t_ref, n_ref, bg_ref, nf_ref, w_pin_hbm, w_gate_hbm,
             dh_ref, loss_ref, dnf_ref, a_ref, w_pin_v, w_gate_v, pe_v, sem):
        _load_ple_weights(w_pin_hbm, w_gate_hbm, 1, w_pin_v, w_gate_v, sem)
        _zero_first([loss_ref, dnf_ref])
        xv = h_ref[...]
        hn = xv + _ple_terms(xv, p_ref, n_ref, bg_ref, w_pin_v, w_gate_v, pe_v, a_ref, False)[0]
        y, yh, r = _rms(hn, nf_ref[...])
        diff = y - t_ref[...]
        loss_ref[...] += 0.5 * jnp.sum(jnp.mean(diff * diff, axis=-1, keepdims=True))
        dy = diff * (1.0 / D_MODEL)
        dhn, dnf = _rms_bwd(dy, yh, r, nf_ref[...])
        dnf_ref[...] += dnf
        dh_ref[...] = dhn

    vec = _const_spec((1, D_MODEL))
    return pl.pallas_call(
        body, name="ple_fwd1", grid=(nt,),
        in_specs=[_row_spec(tm, D_MODEL), _p_spec(tm, 1), _row_spec(tm, D_MODEL), vec, vec, vec, ANY, ANY],
        out_specs=[_row_spec(tm, D_MODEL), _const_spec((8, 128)), vec, _row_spec(tm, D_MODEL)],
        out_shape=[jax.ShapeDtypeStruct((T, D_MODEL), F32), jax.ShapeDtypeStruct((8, 128), F32),
                   jax.ShapeDtypeStruct((1, D_MODEL), F32), jax.ShapeDtypeStruct((T, D_MODEL), BF16)],
        scratch_shapes=[pltpu.VMEM((N_SHARD, PLE_DIM, pw), BF16), pltpu.VMEM((D_MODEL, D_MODEL), BF16),
                        pltpu.VMEM((tm, D_MODEL), F32), pltpu.SemaphoreType.DMA((2 * N_SHARD,))],
        compiler_params=_params(),
    )(h, p, tgt, nple, bg, nfin, w_pin, w_gate)


def _ple_bwd(dh, hb, p, a, nple, w_pin, w_gate, layer, kv_args=None):
    T = hb.shape[0]
    tm = min(512, T)
    nt = T // tm
    with_kv = kv_args is not None
    pw = D_MODEL // N_SHARD

    def body(*refs):
        if with_kv:
            (dh_ref, hb_ref, p_ref, a_ref, n_ref, w_pin_hbm, w_gate_hbm, hc_ref, dkv_ref, nkv_ref, w_kv_hbm,
             dhb_ref, dwpin_ref, dwgate_ref, dbg_ref, dn_ref, dwkv_ref, dnkv_ref,
             w_pin_v, w_gate_v, pe_v, w_kv_v, sem) = refs
        else:
            (dh_ref, hb_ref, p_ref, a_ref, n_ref, w_pin_hbm, w_gate_hbm,
             dhb_ref, dwpin_ref, dwgate_ref, dbg_ref, dn_ref, w_pin_v, w_gate_v, pe_v, sem) = refs
        pairs = [(w_pin_hbm, w_pin_v), (w_gate_hbm, w_gate_v)]
        if with_kv:
            pairs.append((w_kv_hbm, w_kv_v))
        _load_once(pairs, sem)
        _zero_first([dwpin_ref, dwgate_ref, dbg_ref, dn_ref] + ([dwkv_ref, dnkv_ref] if with_kv else []))
        do = dh_ref[...]
        if with_kv:
            dkvb = dkv_ref[...].astype(BF16)
            dkvn = _dot_nt(dkvb, w_kv_v[...])
            kvn, kh, kr = _rms(hc_ref[...], nkv_ref[...])
            dwkv_ref[...] += _dot_tn(kvn.astype(BF16), dkvb)
            dk, dnkv = _rms_bwd(dkvn, kh, kr, nkv_ref[...])
            dnkv_ref[...] += dnkv
            do = do + dk
        _, pe, gate, xgb, xh, r = _ple_terms(hb_ref[...], p_ref, n_ref, None, w_pin_v, w_gate_v, pe_v, a_ref, True)
        dpe = (do * gate).astype(BF16)
        pb = p_ref[...].astype(BF16)
        for j in range(N_SHARD):
            dwpin_ref[j] += _dot_tn(pb, dpe[:, j * pw:(j + 1) * pw])
        da = do * pe * (gate * (1.0 - gate))
        dab = da.astype(BF16)
        dbg_ref[...] += jnp.sum(da, axis=0, keepdims=True)
        dxg = _dot_nt(dab, w_gate_v[...])
        dwgate_ref[...] += _dot_tn(xgb, dab)
        dxx, dn = _rms_bwd(dxg, xh, r, n_ref[...])
        dn_ref[...] += dn
        dhb_ref[...] = do + dxx

    vec = _const_spec((1, D_MODEL))
    row = _row_spec(tm, D_MODEL)
    in_specs = [row, row, _p_spec(tm, layer), row, vec, ANY, ANY]
    args = [dh, hb, p, a, nple, w_pin, w_gate]
    out_specs = [row, _const_spec((N_SHARD, PLE_DIM, pw)), _const_spec((D_MODEL, D_MODEL)), vec, vec]
    out_shape = [jax.ShapeDtypeStruct((T, D_MODEL), F32), jax.ShapeDtypeStruct((N_SHARD, PLE_DIM, pw), F32),
                 jax.ShapeDtypeStruct((D_MODEL, D_MODEL), F32),
                 jax.ShapeDtypeStruct((1, D_MODEL), F32), jax.ShapeDtypeStruct((1, D_MODEL), F32)]
    scratch = [pltpu.VMEM((N_SHARD, PLE_DIM, pw), BF16), pltpu.VMEM((D_MODEL, D_MODEL), BF16),
               pltpu.VMEM((tm, D_MODEL), F32)]
    if with_kv:
        hc, dkv, nkv, w_kv = kv_args
        in_specs += [row, _row_spec(tm, 2 * KV_DIM), vec, ANY]
        args += [hc, dkv, nkv, w_kv]
        out_specs += [_const_spec((D_MODEL, 2 * KV_DIM)), vec]
        out_shape += [jax.ShapeDtypeStruct((D_MODEL, 2 * KV_DIM), F32), jax.ShapeDtypeStruct((1, D_MODEL), F32)]
        scratch.append(pltpu.VMEM((D_MODEL, 2 * KV_DIM), BF16))
    scratch.append(pltpu.SemaphoreType.DMA((3,)))
    return pl.pallas_call(
        body, name=f"ple_bwd{layer}", grid=(nt,), in_specs=in_specs, out_specs=out_specs,
        out_shape=out_shape, scratch_shapes=scratch, compiler_params=_params(),
    )(*args)


GROUP_ROWS = GQA_GROUP * BLOCK


def _stack_heads(x, kh):
    return jnp.concatenate([x[:, (kh * GQA_GROUP + g) * HEAD_DIM:(kh * GQA_GROUP + g + 1) * HEAD_DIM]
                            for g in range(GQA_GROUP)], axis=0)


def _attn_fwd(h, nmix, kv, sinks, w_q, w_o):
    T = h.shape[0]
    tm = min(512, T)
    nt = T // tm
    nb = tm // BLOCK

    def body(h_ref, n_ref, kv_ref, kvp_ref, sink_ref, w_q_hbm, w_o_hbm,
             out_ref, q_ref, ao_ref, p_ref, psink_ref, w_q_v, w_o_v, kvs_v, sem):
        _load_once([(w_q_hbm, w_q_v), (w_o_hbm, w_o_v)], sem)
        ti = pl.program_id(0)
        xv = h_ref[...]
        xn = _rms(xv, n_ref[...])[0].astype(BF16)
        q_ref[...] = (_dot(xn, w_q_v[...]) * (HEAD_DIM ** -0.5)).astype(BF16)
        kvs_v[0:BLOCK, :] = kvp_ref[...]
        kvs_v[BLOCK:, :] = kv_ref[...]
        lane = lax.broadcasted_iota(jnp.int32, (BLOCK, 128), 1)
        ii = lax.broadcasted_iota(jnp.int32, (BLOCK, 2 * BLOCK), 0)
        jj = lax.broadcasted_iota(jnp.int32, (BLOCK, 2 * BLOCK), 1)
        dist = ii + BLOCK - jj
        inband = (dist >= 0) & (dist < BLOCK)
        distf = dist.astype(F32)

        def blk_body(b, carry):
            r0 = pl.multiple_of(b * BLOCK, BLOCK)
            valid = inband & ((jj >= BLOCK) | jnp.logical_not(jnp.logical_and(ti == 0, b == 0)))
            qb = q_ref[pl.ds(r0, BLOCK), :]
            band = kvs_v[pl.ds(r0, 2 * BLOCK), :]
            psink_mat = jnp.zeros((BLOCK, 128), F32)
            outs = []
            for hq in range(N_Q_HEADS):
                kh, g = divmod(hq, GQA_GROUP)
                k_h = band[:, kh * HEAD_DIM:(kh + 1) * HEAD_DIM]
                v_h = band[:, KV_DIM + kh * HEAD_DIM:KV_DIM + (kh + 1) * HEAD_DIM]
                s = _dot_nt(qb[:, hq * HEAD_DIM:(hq + 1) * HEAD_DIM], k_h) - _SLOPES[hq] * distf
                s = jnp.where(valid, s, NEG)
                sink = sink_ref[hq]
                m = jnp.maximum(jnp.max(s, axis=1, keepdims=True), sink)
                e = jnp.exp(s - m)
                esink = jnp.exp(sink - m)
                inv = 1.0 / (jnp.sum(e, axis=1, keepdims=True) + esink)
                pb = (e * inv).astype(BF16)
                p_ref[b, kh, g * BLOCK:(g + 1) * BLOCK, :] = pb
                outs.append(_dot(pb, v_h))
                psink_mat = jnp.where(lane == hq, esink * inv, psink_mat)
            ao_ref[pl.ds(r0, BLOCK), :] = jnp.concatenate(outs, axis=1).astype(BF16)
            psink_ref[pl.ds(r0, BLOCK), :] = psink_mat
            return carry

        lax.fori_loop(0, nb, blk_body, 0)
        out_ref[...] = xv + _dot(ao_ref[...], w_o_v[...])

    row = _row_spec(tm, D_MODEL)
    prev_spec = pl.BlockSpec((BLOCK, 2 * KV_DIM), lambda i: (jnp.maximum(i * nb - 1, 0), 0))
    return pl.pallas_call(
        body, name="attn_fwd", grid=(nt,),
        in_specs=[row, _const_spec((1, D_MODEL)), _row_spec(tm, 2 * KV_DIM), prev_spec, SMEM, ANY, ANY],
        out_specs=[row, row, row, pl.BlockSpec((nb, N_KV_HEADS, GROUP_ROWS, 2 * BLOCK), lambda i: (i, 0, 0, 0)),
                   _row_spec(tm, 128)],
        out_shape=[jax.ShapeDtypeStruct((T, D_MODEL), F32), jax.ShapeDtypeStruct((T, D_MODEL), BF16),
                   jax.ShapeDtypeStruct((T, D_MODEL), BF16),
                   jax.ShapeDtypeStruct((T // BLOCK, N_KV_HEADS, GROUP_ROWS, 2 * BLOCK), BF16),
                   jax.ShapeDtypeStruct((T, 128), F32)],
        scratch_shapes=[pltpu.VMEM((D_MODEL, D_MODEL), BF16), pltpu.VMEM((D_MODEL, D_MODEL), BF16),
                        pltpu.VMEM((tm + BLOCK, 2 * KV_DIM), BF16), pltpu.SemaphoreType.DMA((2,))],
        compiler_params=_params(),
    )(h, nmix, kv, kv, sinks, w_q, w_o)


def _attn_bwd(dh, h, q, kv, ao, p, psink, nmix, w_q, w_o):
    T = h.shape[0]
    tm = min(512, T)
    nt = T // tm
    nb = tm // BLOCK

    def body(dh_ref, h_ref, q_ref, kv_ref, kvp_ref, ao_ref, p_ref, psink_ref, n_ref, w_q_hbm, w_o_hbm,
             dhin_ref, dwq_ref, dwo_ref, dkv_ref, dsink_ref, dn_ref,
             w_q_v, w_o_v, kvs_v, dao_v, dq_v, dkv_v, carry_v, sem):
        _load_once([(w_q_hbm, w_q_v), (w_o_hbm, w_o_v)], sem)
        _zero_first([carry_v, dsink_ref, dn_ref, dwq_ref, dwo_ref])
        dout = dh_ref[...]
        doutb = dout.astype(BF16)
        dao_v[...] = _dot_nt(doutb, w_o_v[...])
        dwo_ref[...] += _dot_tn(ao_ref[...], doutb)
        kvs_v[0:BLOCK, :] = kvp_ref[...]
        kvs_v[BLOCK:, :] = kv_ref[...]
        dkv_v[0:tm, :] = jnp.zeros((tm, 2 * KV_DIM), F32)
        dkv_v[tm:, :] = carry_v[...]
        seg = (lax.broadcasted_iota(jnp.int32, (D_MODEL, 128), 0) // HEAD_DIM
               == lax.broadcasted_iota(jnp.int32, (D_MODEL, 128), 1)).astype(BF16)

        def blk_body(b, dsk):
            r0 = pl.multiple_of(b * BLOCK, BLOCK)
            qb = q_ref[pl.ds(r0, BLOCK), :]
            band = kvs_v[pl.ds(r0, 2 * BLOCK), :]
            aob = ao_ref[pl.ds(r0, BLOCK), :].astype(F32)
            daob = dao_v[pl.ds(r0, BLOCK), :]
            prod = daob * aob
            head = prod.astype(BF16)
            tail = (prod - head.astype(F32)).astype(BF16)
            dsk = dsk + psink_ref[pl.ds(r0, BLOCK), :] * (_dot(head, seg) + _dot(tail, seg))
            dqs = []
            dks = []
            dvs = []
            for kh in range(N_KV_HEADS):
                k_h = band[:, kh * HEAD_DIM:(kh + 1) * HEAD_DIM]
                v_h = band[:, KV_DIM + kh * HEAD_DIM:KV_DIM + (kh + 1) * HEAD_DIM]
                q_g = _stack_heads(qb, kh)
                dao_g = _stack_heads(daob, kh)
                prb = p_ref[b, kh]
                pr = prb.astype(F32)
                dd = jnp.sum(dao_g * _stack_heads(aob, kh), axis=1, keepdims=True)
                dao_gb = dao_g.astype(BF16)
                dp = _dot_nt(dao_gb, v_h)
                dsb = (pr * (dp - dd)).astype(BF16)
                dq_g = _dot(dsb, k_h) * (HEAD_DIM ** -0.5)
                dks.append(_dot_tn(dsb, q_g))
                dvs.append(_dot_tn(prb, dao_gb))
                for g in range(GQA_GROUP):
                    dqs.append(dq_g[g * BLOCK:(g + 1) * BLOCK])
            dq_v[pl.ds(r0, BLOCK), :] = jnp.concatenate(dqs, axis=1)
            dkv_v[pl.ds(r0, 2 * BLOCK), :] += jnp.concatenate(dks + dvs, axis=1)
            return dsk

        dsk = lax.fori_loop(0, nb, blk_body, jnp.zeros((BLOCK, 128), F32))
        dsink_ref[...] -= jnp.sum(dsk, axis=0, keepdims=True)
        dqb = dq_v[...].astype(BF16)
        dxn = _dot_nt(dqb, w_q_v[...])
        xn, xh, r = _rms(h_ref[...], n_ref[...])
        dwq_ref[...] += _dot_tn(xn.astype(BF16), dqb)
        dxx, dn = _rms_bwd(dxn, xh, r, n_ref[...])
        dn_ref[...] += dn
        dhin_ref[...] = dout + dxx
        dkv_ref[...] = dkv_v[BLOCK:, :]
        carry_v[...] = dkv_v[0:BLOCK, :]

    rev = functools.partial(_row_spec, rev_nt=nt)
    row = rev(tm, D_MODEL)
    prev_spec = pl.BlockSpec((BLOCK, 2 * KV_DIM), lambda i: (jnp.maximum((nt - 1 - i) * nb - 1, 0), 0))
    return pl.pallas_call(
        body, name="attn_bwd", grid=(nt,),
        in_specs=[row, row, row, rev(tm, 2 * KV_DIM), prev_spec, row,
                  pl.BlockSpec((nb, N_KV_HEADS, GROUP_ROWS, 2 * BLOCK), lambda i: (nt - 1 - i, 0, 0, 0)),
                  rev(tm, 128), _const_spec((1, D_MODEL)), ANY, ANY],
        out_specs=[row, _const_spec((D_MODEL, D_MODEL)), _const_spec((D_MODEL, D_MODEL)), rev(tm, 2 * KV_DIM),
                   _const_spec((8, 128)), _const_spec((1, D_MODEL))],
        out_shape=[jax.ShapeDtypeStruct((T, D_MODEL), F32), jax.ShapeDtypeStruct((D_MODEL, D_MODEL), F32),
                   jax.ShapeDtypeStruct((D_MODEL, D_MODEL), F32), jax.ShapeDtypeStruct((T, 2 * KV_DIM), F32),
                   jax.ShapeDtypeStruct((8, 128), F32), jax.ShapeDtypeStruct((1, D_MODEL), F32)],
        scratch_shapes=[pltpu.VMEM((D_MODEL, D_MODEL), BF16), pltpu.VMEM((D_MODEL, D_MODEL), BF16),
                        pltpu.VMEM((tm + BLOCK, 2 * KV_DIM), BF16), pltpu.VMEM((tm, D_MODEL), F32),
                        pltpu.VMEM((tm, D_MODEL), F32), pltpu.VMEM((tm + BLOCK, 2 * KV_DIM), F32),
                        pltpu.VMEM((BLOCK, 2 * KV_DIM), F32), pltpu.SemaphoreType.DMA((2,))],
        compiler_params=_params(),
    )(dh, h, q, kv, kv, ao, p, psink, nmix, w_q, w_o)


def _mesh_pos():
    return lax.axis_index("x"), lax.axis_index("y"), lax.axis_index("c")


def _other_chips(x, y):
    return [(1 - x, y), (x, 1 - y), (1 - x, 1 - y)]


HBM_SPEC = pl.BlockSpec(memory_space=pltpu.HBM)
SEM_SPEC = pl.BlockSpec(memory_space=pltpu.SEMAPHORE)


def _split_call(name, bufs, waits=(), starts=(), after=()):
    n, nw, ns, na = len(bufs), len(waits), len(starts), len(after)

    def body(*refs):
        brefs = refs[:n]
        wsems = [(refs[n + 2 * k], refs[n + 2 * k + 1]) for k in range(nw)]
        o = n + 2 * nw + na
        ssems = [(refs[o + 2 * k], refs[o + 2 * k + 1]) for k in range(ns)]
        for (ss, rs), (_, _, fn) in zip(wsems, waits):
            for sending, arriving in fn(brefs, ss, rs):
                sending.wait_send()
                arriving.wait_recv()
        for (ss, rs), (_, fn) in zip(ssems, starts):
            for sending, _ in fn(brefs, ss, rs):
                sending.start()
        if ns:
            token = refs[o + 2 * ns + n]
            token[...] = jnp.zeros(token.shape, token.dtype)

    out_shape, out_specs = [], []
    for cnt, _ in starts:
        out_shape += [pltpu.SemaphoreType.DMA((cnt,)), pltpu.SemaphoreType.DMA((cnt,))]
        out_specs += [SEM_SPEC, SEM_SPEC]
    out_shape += [pltpu.HBM(b.shape, b.dtype) for b in bufs]
    out_specs += [HBM_SPEC] * n
    if ns:
        out_shape.append(jax.ShapeDtypeStruct((8, 128), F32))
        out_specs.append(pl.BlockSpec(memory_space=pltpu.VMEM))
    args = [pltpu.with_memory_space_constraint(b, pltpu.HBM) for b in bufs]
    for ss, rs, _ in waits:
        args += [ss, rs]
    args += list(after)
    res = pl.pallas_call(
        body, name=name, out_shape=tuple(out_shape),
        in_specs=[HBM_SPEC] * n + [SEM_SPEC] * (2 * nw) + [ANY] * na, out_specs=tuple(out_specs),
        input_output_aliases={i: 2 * ns + i for i in range(n)},
        compiler_params=pltpu.CompilerParams(has_side_effects=pltpu.SideEffectType.DATAFLOW_SIDE_EFFECTING),
    )(*args)
    sems = [(res[2 * k], res[2 * k + 1]) for k in range(ns)]
    return list(res[2 * ns:2 * ns + n]), sems, (res[2 * ns + n] if ns else None)


def _cast_place(items, name, deps=()):
    n = len(items)
    mats = [a.shape[-2:] for a, _, _ in items]

    def body(*refs):
        ins, outs, scr, sem = refs[:n], refs[n:2 * n], refs[2 * n:3 * n], refs[3 * n]
        x, y, _ = _mesh_pos()
        cps = []
        for t in range(n):
            scr[t][...] = ins[t][...].astype(scr[t].dtype)
            cp = pltpu.make_async_copy(scr[t], outs[t].at[2 * x + y], sem.at[t])
            cp.start()
            cps.append(cp)
        for cp in cps:
            cp.wait()

    def spec(idx, shape):
        return pl.BlockSpec((None,) * len(idx) + tuple(shape), lambda i: tuple(idx) + (0, 0))

    body, in_specs, args = _add_deps(body, [spec(idx, mat) for (_, idx, _), mat in zip(items, mats)],
                                     [a for a, _, _ in items], deps)
    return pl.pallas_call(
        body, name=name, grid=(1,), in_specs=in_specs, out_specs=[ANY] * n,
        out_shape=[jax.ShapeDtypeStruct((N_SHARD,) + tuple(mat), dt) for (_, _, dt), mat in zip(items, mats)],
        scratch_shapes=[pltpu.VMEM(tuple(mat), dt) for (_, _, dt), mat in zip(items, mats)]
        + [pltpu.SemaphoreType.DMA((n,))],
        compiler_params=_params(),
    )(*args)


def _gather_ici(idx):
    def fn(bufs, ss, rs):
        x, y, c = _mesh_pos()
        pairs = []
        for k, t in enumerate(idx):
            half = bufs[t].shape[1] // 2
            mine = bufs[t].at[2 * x + y, pl.ds(c * half, half), :]
            for j, (cx, cy) in enumerate(_other_chips(x, y)):
                theirs = bufs[t].at[2 * cx + cy, pl.ds(c * half, half), :]
                sem = dict(send_sem=ss.at[3 * k + j], recv_sem=rs.at[3 * k + j],
                           device_id=(cx, cy, c), device_id_type=MESH)
                pairs.append((pltpu.make_async_remote_copy(src_ref=mine, dst_ref=mine, **sem),
                              pltpu.make_async_remote_copy(src_ref=mine, dst_ref=theirs, **sem)))
        return pairs
    return fn


def _gather_d2d(idx):
    def fn(bufs, ss, rs):
        x, y, c = _mesh_pos()
        pairs = []
        for k, t in enumerate(idx):
            half = bufs[t].shape[1] // 2
            for j, (cx, cy) in enumerate(_other_chips(x, y)):
                got = bufs[t].at[2 * cx + cy, pl.ds(c * half, half), :]
                theirs = bufs[t].at[2 * cx + cy, pl.ds((1 - c) * half, half), :]
                sem = dict(send_sem=ss.at[3 * k + j], recv_sem=rs.at[3 * k + j],
                           device_id=(x, y, 1 - c), device_id_type=MESH)
                pairs.append((pltpu.make_async_remote_copy(src_ref=got, dst_ref=got, **sem),
                              pltpu.make_async_remote_copy(src_ref=got, dst_ref=theirs, **sem)))
        return pairs
    return fn


def _alloc(shapes, name):
    def body(*refs):
        pass

    return pl.pallas_call(body, name=name, out_specs=[ANY] * len(shapes),
                          out_shape=[jax.ShapeDtypeStruct(s, d) for s, d in shapes])()


def _send_to_sibling(n):
    def fn(bufs, ss, rs):
        x, y, c = _mesh_pos()
        pairs = []
        for t in range(n):
            src = bufs[t]
            if len(src.shape) == 3:
                half = src.shape[1] // 2
                src = src.at[:, pl.ds((1 - c) * half, half), :]
            cp = pltpu.make_async_remote_copy(src_ref=src, dst_ref=bufs[n + t], send_sem=ss.at[t],
                                              recv_sem=rs.at[t], device_id=(x, y, 1 - c), device_id_type=MESH)
            pairs.append((cp, cp))
        return pairs
    return fn


def _send_to_chips(n):
    def fn(bufs, ss, rs):
        x, y, c = _mesh_pos()
        pairs = []
        for j, (cx, cy) in enumerate(_other_chips(x, y)):
            for t in range(n):
                src = bufs[t].at[j] if len(bufs[t].shape) == 3 else bufs[t]
                cp = pltpu.make_async_remote_copy(src_ref=src, dst_ref=bufs[n + t].at[j], send_sem=ss.at[3 * t + j],
                                                  recv_sem=rs.at[3 * t + j], device_id=(cx, cy, c),
                                                  device_id_type=MESH)
                pairs.append((cp, cp))
        return pairs
    return fn


class _Exchange:
    def __init__(self, name, srcs, land_shapes, fn, n_sems):
        self.name, self.fn = name, fn
        lands = _alloc(land_shapes, name + "_alloc")
        self.n = len(srcs)
        self.bufs, sems, self.token = _split_call(name + "_start", list(srcs) + list(lands),
                                                  starts=[(n_sems, fn)])
        self.sems = sems[0]

    def finish(self, after=()):
        bufs, _, _ = _split_call(self.name + "_wait", self.bufs, waits=[(*self.sems, self.fn)], after=after)
        return bufs[:self.n], bufs[self.n:]


def _row_block(rows, cols, mult=8, limit=3 * 512 * 1024, itemsize=4):
    best = None
    for br in range(mult, rows + 1, mult):
        if rows % br == 0 and br * cols * itemsize <= limit:
            best = br
    assert best is not None, (rows, cols)
    return best


_GROUP_BLOCK_BYTES = 1024 * 1024


def _group_plan(ss):
    plan = []
    for s in ss:
        half, cols = s.shape[-2:]
        br = _row_block(half, cols, mult=16, limit=_GROUP_BLOCK_BYTES)
        plan.append((br, half // br))
    return plan, max(nr for _, nr in plan)


def _chip_partial(gs, ss, ids, name):
    n = len(gs)
    plan, steps = _group_plan(ss)

    def body(ids_ref, *refs):
        for t in range(n):
            refs[2 * n + t][...] = (refs[t][...] + refs[n + t][...]).astype(BF16)

    g_specs, s_specs, o_specs = [], [], []
    for (br, nr), s in zip(plan, ss):
        blk = (None, br, s.shape[2])
        g_specs.append(pl.BlockSpec(
            blk, lambda j, r, ids_ref, nr=nr: (ids_ref[2 + j], ids_ref[0] * nr + jnp.minimum(r, nr - 1), 0)))
        s_specs.append(pl.BlockSpec(blk, lambda j, r, ids_ref, nr=nr: (ids_ref[2 + j], jnp.minimum(r, nr - 1), 0)))
        o_specs.append(pl.BlockSpec(blk, lambda j, r, ids_ref, nr=nr: (j, jnp.minimum(r, nr - 1), 0)))
    return pl.pallas_call(
        body, name=name,
        grid_spec=pltpu.PrefetchScalarGridSpec(num_scalar_prefetch=1, grid=(3, steps),
                                               in_specs=g_specs + s_specs, out_specs=o_specs),
        out_shape=[jax.ShapeDtypeStruct((3,) + s.shape[1:], BF16) for s in ss],
        compiler_params=pltpu.CompilerParams(dimension_semantics=("arbitrary", "arbitrary"),
                                             vmem_limit_bytes=VMEM_LIMIT),
    )(ids, *gs, *ss)


def _chip_sum(gs, ss, qs, ids, name):
    n = len(gs)
    plan, steps = _group_plan(ss)

    def body(ids_ref, *refs):
        for t in range(n):
            q_ref = refs[2 * n + t]
            own = refs[t][...] + refs[n + t][...]
            refs[3 * n + t][...] = (own + q_ref[2].astype(F32)) + (q_ref[0].astype(F32) + q_ref[1].astype(F32))

    g_specs, s_specs, q_specs, o_specs = [], [], [], []
    for (br, nr), s in zip(plan, ss):
        cols = s.shape[2]
        g_specs.append(pl.BlockSpec(
            (None, br, cols), lambda r, ids_ref, nr=nr: (ids_ref[1], ids_ref[0] * nr + jnp.minimum(r, nr - 1), 0)))
        s_specs.append(pl.BlockSpec((None, br, cols), lambda r, ids_ref, nr=nr: (ids_ref[1], jnp.minimum(r, nr - 1), 0)))
        q_specs.append(pl.BlockSpec((3, br, cols), lambda r, ids_ref, nr=nr: (0, jnp.minimum(r, nr - 1), 0)))
        o_specs.append(pl.BlockSpec((br, cols), lambda r, ids_ref, nr=nr: (jnp.minimum(r, nr - 1), 0)))
    return pl.pallas_call(
        body, name=name,
        grid_spec=pltpu.PrefetchScalarGridSpec(num_scalar_prefetch=1, grid=(steps,),
                                               in_specs=g_specs + s_specs + q_specs, out_specs=o_specs),
        out_shape=[jax.ShapeDtypeStruct(s.shape[1:], F32) for s in ss],
        compiler_params=pltpu.CompilerParams(dimension_semantics=("arbitrary",), vmem_limit_bytes=VMEM_LIMIT),
    )(ids, *gs, *ss, *qs)


def _adamw_math(w, g, m, v):
    mn = ADAM_B1 * m + (1.0 - ADAM_B1) * g
    vn = ADAM_B2 * v + (1.0 - ADAM_B2) * (g * g)
    m_hat = mn / (1.0 - ADAM_B1 ** ADAM_STEP)
    v_hat = vn / (1.0 - ADAM_B2 ** ADAM_STEP)
    return -ADAM_LR * (m_hat / (jnp.sqrt(v_hat) + ADAM_EPS) + ADAM_WD * w), mn, vn


def _adamw_halves(w, own, sib, m, v, ids, name, layer=0, n_layers=1, stacked=None):
    C = w.shape[1]
    R = w.shape[0] // n_layers
    half = R // 2
    br = _row_block(half, C)
    nh = half // br
    base = layer * 2 * nh

    def body(ids_ref, w_ref, own_ref, sib_ref, m_ref, v_ref, *rest):
        g_ref, d_ref, mo_ref, vo_ref = rest[-4:]
        is_own = (pl.program_id(0) // nh) == ids_ref[0]
        g = jnp.where(is_own, own_ref[...], sib_ref[...])
        g_ref[...] = g
        d_ref[...], mo_ref[...], vo_ref[...] = _adamw_math(w_ref[...], g, m_ref[...], v_ref[...])

    full = pl.BlockSpec((br, C), lambda r, ids_ref: (base + r, 0))
    own_spec = pl.BlockSpec((br, C), lambda r, ids_ref: (jnp.clip(r - ids_ref[0] * nh, 0, nh - 1), 0))
    sib_spec = pl.BlockSpec((br, C), lambda r, ids_ref: (jnp.clip(r - (1 - ids_ref[0]) * nh, 0, nh - 1), 0))
    in_specs = [full, own_spec, sib_spec, full, full]
    args = [ids, w, own, sib, m, v]
    aliases = {}
    if stacked is not None:
        in_specs += [ANY] * 4
        args += list(stacked)
        aliases = {6 + k: k for k in range(4)}
    return pl.pallas_call(
        body, name=name,
        grid_spec=pltpu.PrefetchScalarGridSpec(
            num_scalar_prefetch=1, grid=(2 * nh,), in_specs=in_specs, out_specs=[full] * 4),
        out_shape=[jax.ShapeDtypeStruct(w.shape, F32)] * 4, input_output_aliases=aliases,
        compiler_params=_params(),
    )(*args)


_PACK_UNIT = 1024


def _pack(arrs):
    flat = []
    for a in arrs:
        f = a.reshape(-1).astype(F32)
        pad = (-f.shape[0]) % _PACK_UNIT
        if pad:
            f = jnp.concatenate([f, jnp.zeros((pad,), F32)])
        flat.append(f)
    return jnp.concatenate(flat).reshape(-1, 128)


def kernel(x, p, norm_mix, norm_ffn, norm_ple, norm_kv, norm_final, a_w_in, a_norm_v, a_w_s, a_b_s, a_w_out, w_kv, b_w_q, b_sinks, b_w_o, f_w_up, f_conv_w, f_conv_b, f_w_down, ple_w_in, ple_w_gate, ple_b_gate, loss_target, m_norm_mix, m_norm_ffn, m_norm_ple, m_norm_kv, m_norm_final, m_a_w_in, m_a_norm_v, m_a_w_s, m_a_b_s, m_a_w_out, m_w_kv, m_b_w_q, m_b_sinks, m_b_w_o, m_f_w_up, m_f_conv_w, m_f_conv_b, m_f_w_down, m_ple_w_in, m_ple_w_gate, m_ple_b_gate, v_norm_mix, v_norm_ffn, v_norm_ple, v_norm_kv, v_norm_final, v_a_w_in, v_a_norm_v, v_a_w_s, v_a_b_s, v_a_w_out, v_w_kv, v_b_w_q, v_b_sinks, v_b_w_o, v_f_w_up, v_f_conv_w, v_f_conv_b, v_f_w_down, v_ple_w_in, v_ple_w_gate, v_ple_b_gate):
    given = dict(locals())

    small_shard = _pack([a_norm_v, f_conv_w])
    pad_rows = (-small_shard.shape[0]) % 16
    if pad_rows:
        small_shard = jnp.concatenate([small_shard, jnp.zeros((pad_rows, 128), F32)])
    groups = [
        [(a_w_in, (0,), BF16), (a_w_out, (0,), BF16), (small_shard, (), F32)],
        [(f_w_up, (0,), BF16), (f_w_down, (0,), BF16)],
        [(ple_w_in, (0,), BF16), (ple_w_gate, (0,), BF16), (w_kv, (), BF16), (b_w_q, (0,), BF16),
         (b_w_o, (0,), BF16), (f_w_up, (1,), BF16), (f_w_down, (1,), BF16), (ple_w_in, (1,), BF16),
         (ple_w_gate, (1,), BF16)],
    ]
    first = list(range(len(groups[0])))
    lands0, sems0, token0 = _split_call("gather_start_g0", _cast_place(groups[0], "cast_place_g0"),
                                        starts=[(3 * len(first), _gather_ici(first))])
    rest, spans, start = [], [], 0
    for gi, items in enumerate(groups[1:], 1):
        rest += _cast_place(items, f"cast_place_g{gi}", deps=(token0,))
        spans.append(list(range(start, start + len(items))))
        start += len(items)
    rest, rest_sems, rest_token = _split_call("gather_start", rest,
                                              starts=[(3 * len(sp), _gather_ici(sp)) for sp in spans])
    group_bufs = [lands0] + [[rest[t] for t in sp] for sp in spans]
    ici_sems = sems0 + rest_sems

    def finish_group(gi, after):
        bufs = group_bufs[gi]
        local = list(range(len(bufs)))
        bufs, d2d_sems, _ = _split_call(f"gather_pass_g{gi}", bufs, waits=[(*ici_sems[gi], _gather_ici(local))],
                                        starts=[(3 * len(local), _gather_d2d(local))], after=after)
        bufs, _, _ = _split_call(f"gather_done_g{gi}", bufs, waits=[(*d2d_sems[0], _gather_d2d(local))])
        return bufs

    def stage0():
        b_in, b_out, b_small = finish_group(0, (rest_token,))
        small_full = b_small.reshape(N_SHARD, -1)
        gv_full = small_full[:, :256].reshape(1, D_MODEL)
        cw_full = small_full[:, _PACK_UNIT:_PACK_UNIT + 2 * 3 * FF_BLK].reshape(N_SHARD, 2, 3, FF_BLK)
        cw_full = jnp.transpose(cw_full, (1, 2, 0, 3)).reshape(2, 3, N_FF)
        return gv_full, cw_full, b_in, b_out.reshape(D_MODEL, D_MODEL)

    def stage1(after):
        b_up, b_dn = finish_group(1, after)
        return b_up, b_dn.reshape(D_FF, D_MODEL)

    def stage2(after):
        pin0, gate0, kv_w, wq, wo, up1, dn1, pin1, gate1 = finish_group(2, after)
        sq = lambda a: a.reshape(D_MODEL, -1)
        return dict(w_pin=[pin0, pin1], w_gate=[sq(gate0), sq(gate1)], w_kv=sq(kv_w), w_q=sq(wq), w_o=sq(wo),
                    w_up1=up1, w_dn1=dn1.reshape(D_FF, D_MODEL))

    dx, (loss, (out_g, out_d, out_m, out_v)) = _local_step(
        x[0], p.reshape(2, -1, PLE_DIM), loss_target[0], norm_mix, norm_ffn, norm_ple, norm_kv, norm_final, a_w_s, a_b_s,
        b_sinks, f_conv_b, ple_b_gate, stage0, stage1, stage2, _Reducer(given))
    weight_names = ['norm_mix', 'norm_ffn', 'norm_ple', 'norm_kv', 'norm_final', 'a_w_in', 'a_norm_v', 'a_w_s',
                    'a_b_s', 'a_w_out', 'w_kv', 'b_w_q', 'b_sinks', 'b_w_o', 'f_w_up', 'f_conv_w', 'f_conv_b',
                    'f_w_down', 'ple_w_in', 'ple_w_gate', 'ple_b_gate']
    return (loss, dx.reshape(x.shape), *[out_g[k] for k in weight_names], *[out_d[k] for k in weight_names],
            *[out_m[k] for k in weight_names], *[out_v[k] for k in weight_names])


def _local_step(xs, p, tgt, norm_mix, norm_ffn, norm_ple, norm_kv, norm_final, a_w_s, a_b_s, b_sinks,
                f_conv_b, ple_b_gate, stage0, stage1, stage2, sched):
    tril = jnp.tril(jnp.ones((CHUNK, CHUNK), F32))
    wsm = (a_w_s[0] * tril[None]).astype(BF16)
    bsb = jnp.broadcast_to(a_b_s[0][:, :, None], (A_GROUPS, CHUNK, CHUNK))
    sinks = b_sinks[0]
    row = lambda a: a.reshape(1, -1)

    gv_full, cw_full, w_in, w_out = stage0()
    h1, zp = _mixer_a_fwd(xs, row(norm_mix[0]), gv_full, wsm, bsb, w_in, w_out)
    w_up0, w_dn0 = stage1((h1,))
    h2, hh0, c0 = _ffn_fwd(h1, row(norm_ffn[0]), cw_full[0], row(f_conv_b[0]), w_up0, w_dn0, 0)
    rest = stage2((h2,))
    w_pin, w_gate, w_kv_f, w_q, w_o = rest['w_pin'], rest['w_gate'], rest['w_kv'], rest['w_q'], rest['w_o']
    w_up = [w_up0, rest['w_up1']]
    w_dn = [w_dn0, rest['w_dn1']]
    h3, kv, a0 = _ple_fwd_kv(h2, p, row(norm_ple[0]), row(ple_b_gate[0]), row(norm_kv), w_pin[0], w_gate[0], w_kv_f)
    h4, q, ao, probs, psink = _attn_fwd(h3, row(norm_mix[1]), kv, sinks, w_q, w_o)
    h5, hh1, c1 = _ffn_fwd(h4, row(norm_ffn[1]), cw_full[1], row(f_conv_b[1]), w_up[1], w_dn[1], 1)
    dh6, loss_acc, dn_final, a1 = _ple_fwd_final(
        h5, p, tgt, row(norm_ple[1]), row(ple_b_gate[1]), row(norm_final), w_pin[1], w_gate[1])

    def pieces(g):
        return g.reshape(N_SHARD, -1, g.shape[-1])

    dh5, g_pin1, g_gate1, dbg1, dnple1 = _ple_bwd(dh6, h5, p, a1, row(norm_ple[1]), w_pin[1], w_gate[1], 1)
    early = {('ple_w_in', 1): g_pin1, ('ple_w_gate', 1): pieces(g_gate1)}
    dh4, g_up1, g_dn1, dcw1, dcb1, dnffn1 = _ffn_bwd(
        dh5, h4, hh1, c1, row(norm_ffn[1]), cw_full[1], w_up[1], w_dn[1], 1)
    early['f_w_down', 1] = pieces(g_dn1)
    early['f_w_up', 1] = g_up1
    dh3a, g_wq, g_wo, dkv, dsink, dnmix1 = _attn_bwd(dh4, h3, q, kv, ao, probs, psink, row(norm_mix[1]), w_q, w_o)
    early['b_w_o', 0] = pieces(g_wo)
    early['b_w_q', 0] = pieces(g_wq)
    dh2, g_pin0, g_gate0, dbg0, dnple0, g_wkv, dnkv = _ple_bwd(
        dh3a, h2, p, a0, row(norm_ple[0]), w_pin[0], w_gate[0], 0,
        kv_args=(h3, dkv, row(norm_kv), w_kv_f))
    early['w_kv', 0] = pieces(g_wkv)
    early['ple_w_in', 0] = g_pin0
    early['ple_w_gate', 0] = pieces(g_gate0)
    deps = sched.early_ready(early)
    dh1, g_up0, g_dn0, dcw0, dcb0, dnffn0 = _ffn_bwd(
        dh2, h1, hh0, c0, row(norm_ffn[0]), cw_full[0], w_up[0], w_dn[0], 0, deps=deps,
        between=lambda part: sched.after_ffn_half((part,)))
    deps = sched.ffn0_ready({('f_w_down', 0): pieces(g_dn0), ('f_w_up', 0): g_up0})
    dx, g_win, g_wout, dws, dbs, dgv, dnmix0 = _mixer_a_bwd(
        dh1, xs, zp, row(norm_mix[0]), gv_full, wsm, bsb, tril, w_in, w_out, deps=deps)
    g_wout = pieces(g_wout)

    small_grads = {
        'norm_mix': jnp.concatenate([dnmix0, dnmix1]), 'norm_ffn': jnp.concatenate([dnffn0, dnffn1]),
        'norm_ple': jnp.concatenate([dnple0, dnple1]), 'norm_kv': dnkv, 'norm_final': dn_final,
        'a_norm_v': dgv, 'a_w_s': dws.reshape(A_GROUPS * CHUNK, CHUNK), 'a_b_s': dbs[:, :, 0],
        'b_sinks': dsink[0:1, :], 'f_conv_w': jnp.concatenate([dcw0, dcw1]),
        'f_conv_b': jnp.concatenate([dcb0, dcb1]), 'ple_b_gate': jnp.concatenate([dbg0, dbg1]),
        'loss': loss_acc,
    }
    outs = sched.finish({('a_w_in', 0): g_win, ('a_w_out', 0): g_wout}, small_grads, (dx,))
    return dx, outs


class _Reducer:
    def __init__(self, given):
        self.given = given
        cx, cy, cc = _mesh_pos()
        self.shard = 2 * cx + cy
        s = self.shard
        self.ids = jnp.stack([cc, s, s ^ 2, s ^ 1, s ^ 3]).astype(jnp.int32)
        self.out = [{}, {}, {}, {}]
        self.stacked = {}

    def _send(self, tag, grads, small=()):
        keys = list(grads)
        srcs = [grads[k] for k in keys] + list(small)
        shapes = [((N_SHARD, g.shape[1] // 2, g.shape[2]), F32) for g in srcs[:len(keys)]]
        shapes += [(s.shape, F32) for s in small]
        return keys, _Exchange(f"send_{tag}", srcs, shapes, _send_to_sibling(len(srcs)), len(srcs))

    def _exchange(self, tag, keys, send, after):
        srcs, lands = send.finish(after)
        n = len(keys)
        parts = _chip_partial(srcs[:n], lands[:n], self.ids, f"chip_partial_{tag}")
        shapes = [(p.shape, BF16) for p in parts]
        if len(srcs) > n:
            small = _small_add(srcs[n:], lands[n:])
            parts += small
            shapes += [((3,) + s.shape, F32) for s in small]
        exch = _Exchange(f"exch_{tag}", parts, shapes, _send_to_chips(len(parts)), 3 * len(parts))
        return (keys, srcs[:n], lands[:n], exch)

    def _swap(self, tag, state, after):
        keys, grads, sib, exch = state
        parts, recv = exch.finish(after)
        n = len(keys)
        own = _chip_sum(grads, sib, recv[:n], self.ids, f"chip_sum_{tag}")
        small_red = _small_sum(parts[n:], recv[n:]) if len(parts) > n else None
        return keys, _Exchange(f"swap_{tag}", own, [(o.shape, F32) for o in own], _send_to_sibling(n), n), small_red

    def _adamw(self, keys, swap, after):
        own, sib = swap.finish(after)
        last = None
        for (name, layer), o, s in zip(keys, own, sib):
            w = self.given[name]
            n_layers = w.shape[0] if w.ndim == 3 else 1
            c2 = w.shape[-1]
            res = _adamw_halves(w.reshape(-1, c2), o, s, self.given['m_' + name].reshape(-1, c2),
                                self.given['v_' + name].reshape(-1, c2), self.ids, f"adamw_{name}{layer}",
                                layer, n_layers, self.stacked.get(name))
            self.stacked[name] = res
            if layer == 0:
                for dst, r in zip(self.out, res):
                    dst[name] = r.reshape(w.shape)
            last = res[0]
        return last

    def early_ready(self, grads):
        self.e_keys, self.e_send = self._send("e", grads)
        return (self.e_send.token,)

    def after_ffn_half(self, after):
        self.e_state = self._exchange("e", self.e_keys, self.e_send, after)
        return (self.e_state[3].token,)

    def ffn0_ready(self, grads):
        _, self.e_swap, _ = self._swap("e", self.e_state, tuple(grads.values()))
        self.f_keys, self.f_send = self._send("f", grads)
        return (self.f_send.token, self.e_swap.token)

    def finish(self, grads, small_grads, after):
        small_names = list(small_grads)
        f_state = self._exchange("f", self.f_keys, self.f_send, after)
        a_keys, a_send = self._send("a", grads, [small_grads[k] for k in small_names])
        a_state = self._exchange("a", a_keys, a_send, (f_state[3].token,))
        e_done = self._adamw(self.e_keys, self.e_swap, (a_state[3].token,))
        f_keys, f_swap, _ = self._swap("f", f_state, (e_done,))
        f_done = self._adamw(f_keys, f_swap, ())
        _, a_swap, small_red = self._swap("a", a_state, (f_done,))
        self._adamw(a_keys, a_swap, ())

        given = self.given
        reduced = dict(zip(small_names, small_red))
        loss = reduced.pop('loss')[0, 0]
        names = list(reduced)
        items = []
        for k in names:
            g = reduced[k]
            cols = g.shape[1] // N_SHARD if k in ('a_norm_v', 'f_conv_w') else g.shape[1]
            view = lambda a: _lane_pad(a.reshape(g.shape[0], -1), cols)
            items.append((view(given[k]), g, view(given['m_' + k]), view(given['v_' + k])))
        res = _adamw_small(items, self.ids)
        for k, four in zip(names, res):
            width = given[k].size // four[0].shape[0]
            for dst, r in zip(self.out, four):
                dst[k] = r[:, :width].reshape(given[k].shape)
        return loss, self.out


def _lane_pad(a, cols):
    return a if a.shape[1] == cols else jnp.pad(a, ((0, 0), (0, cols - a.shape[1])))


def _small_add(a_list, b_list):
    n = len(a_list)

    def body(*refs):
        for t in range(n):
            refs[2 * n + t][...] = refs[t][...] + refs[n + t][...]

    return pl.pallas_call(body, name="chip_partial_small",
                          out_shape=[jax.ShapeDtypeStruct(a.shape, F32) for a in a_list])(*a_list, *b_list)


def _small_sum(parts, recvs):
    n = len(parts)

    def body(*refs):
        for t in range(n):
            q = refs[n + t]
            refs[2 * n + t][...] = (refs[t][...] + q[2]) + (q[0] + q[1])

    return pl.pallas_call(body, name="chip_sum_small",
                          out_shape=[jax.ShapeDtypeStruct(p.shape, F32) for p in parts])(*parts, *recvs)


def _adamw_small(items, ids):
    n = len(items)

    def body(ids_ref, *refs):
        for t in range(n):
            w_ref, g_ref, m_ref, v_ref = refs[4 * t:4 * t + 4]
            g_out, d_ref, mo_ref, vo_ref = refs[4 * n + 4 * t:4 * n + 4 * t + 4]
            g = g_ref[...]
            g_out[...] = g
            d_ref[...], mo_ref[...], vo_ref[...] = _adamw_math(w_ref[...], g, m_ref[...], v_ref[...])

    in_specs, out_specs, out_shape, args = [], [], [], []
    for w, g, m, v in items:
        full = pl.BlockSpec(w.shape, lambda i, ids_ref: (0, 0))
        g_spec = full if g.shape == w.shape else pl.BlockSpec(w.shape, lambda i, ids_ref: (0, ids_ref[1]))
        in_specs += [full, g_spec, full, full]
        out_specs += [full] * 4
        out_shape += [jax.ShapeDtypeStruct(w.shape, F32)] * 4
        args += [w, g, m, v]
    res = pl.pallas_call(
        body, name="adamw_small",
        grid_spec=pltpu.PrefetchScalarGridSpec(num_scalar_prefetch=1, grid=(1,), in_specs=in_specs,
                                               out_specs=out_specs),
        out_shape=out_shape, compiler_params=_params(),
    )(ids, *args)
    return [res[4 * t:4 * t + 4] for t in range(n)]
```
